```python
import math
import jax, jax.numpy as jnp
from jax import lax
import numpy as np

D_MODEL = 1024
BATCH = 8
SEQ = 8192
DEPTH = 1

ATT_HEADS = 16
ATT_HEAD_DIM = 64
ATT_WIDTH = ATT_HEADS * ATT_HEAD_DIM
Q_BLOCK = 128
SSM_EXPAND = 2
SSM_INNER = SSM_EXPAND * D_MODEL
SSM_HEAD_DIM = 64
SSM_HEADS = SSM_INNER // SSM_HEAD_DIM
SSM_GROUPS = 4
SSM_HEADS_PER_GROUP = SSM_HEADS // SSM_GROUPS
SSM_STATE = 128
SSM_CONV = 4
SSM_CHUNK = 128
SSM_CONV_DIM = SSM_INNER + 2 * SSM_GROUPS * SSM_STATE
N_BRANCHES = 2
FFN_HIDDEN = -(-8 * D_MODEL // (3 * 256)) * 256
DEEPNORM_ALPHA = (2 * DEPTH) ** 0.25
DEEPNORM_BETA = (8 * DEPTH) ** -0.25
LN_EPS = 1e-5
RMS_EPS = 1e-5
IN_SIZES = (ATT_WIDTH, ATT_WIDTH, ATT_WIDTH, ATT_HEADS, SSM_INNER, SSM_CONV_DIM, SSM_HEADS, N_BRANCHES * D_MODEL)
IN_WIDTH = sum(IN_SIZES)

kernel_name = "fox_ssd_gated_hybrid_deepnorm"


def layer_norm(x, g, b):
    xf = x.astype(jnp.float32)
    mu = jnp.mean(xf, axis=-1, keepdims=True)
    var = jnp.mean(jnp.square(xf - mu), axis=-1, keepdims=True)
    return ((xf - mu) * lax.rsqrt(var + LN_EPS) * g.astype(jnp.float32) + b.astype(jnp.float32)).astype(x.dtype)


def forgetting_attention(q, k, v, log_f):
    bsz, seq, heads, dh = q.shape
    n_blk = seq // Q_BLOCK
    scale = 1.0 / math.sqrt(dh)
    cum = jnp.cumsum(log_f, axis=1).transpose(0, 2, 1)
    kh = k.transpose(0, 2, 1, 3)
    vh = v.transpose(0, 2, 1, 3)
    q_blocks = q.transpose(0, 2, 1, 3).reshape(bsz, heads, n_blk, Q_BLOCK, dh).transpose(2, 0, 1, 3, 4)
    dq_blocks = cum.reshape(bsz, heads, n_blk, Q_BLOCK).transpose(2, 0, 1, 3)
    key_pos = jnp.arange(seq)

    def one_block(args):
        qb, dqb, i = args
        s = jnp.einsum('bhqd,bhkd->bhqk', qb, kh, preferred_element_type=jnp.float32) * scale
        s = s + (dqb[..., :, None] - cum[..., None, :])
        q_pos = i * Q_BLOCK + jnp.arange(Q_BLOCK)
        causal = key_pos[None, :] <= q_pos[:, None]
        p = jax.nn.softmax(jnp.where(causal, s, -jnp.inf), axis=-1).astype(vh.dtype)
        return jnp.einsum('bhqk,bhkd->bhqd', p, vh)

    out = lax.map(one_block, (q_blocks, dq_blocks, jnp.arange(n_blk)))
    return out.transpose(1, 0, 3, 2, 4).reshape(bsz, seq, heads * dh)


def causal_depthwise_conv(u, w, b):
    out = lax.conv_general_dilated(u, w[:, None, :], window_strides=(1,), padding=[(SSM_CONV - 1, 0)],
                                   dimension_numbers=('NWC', 'WIO', 'NWC'), feature_group_count=u.shape[-1])
    return out + b


def ssd_chunked(x, dt, a, bmat, cmat):
    bsz, seq, heads, hd = x.shape
    nc, L = seq // SSM_CHUNK, SSM_CHUNK
    G, R, N = SSM_GROUPS, SSM_HEADS_PER_GROUP, SSM_STATE
    xc = x.reshape(bsz, nc, L, G, R, hd)
    dtc = dt.reshape(bsz, nc, L, G, R)
    bc = bmat.reshape(bsz, nc, L, G, N)
    cc = cmat.reshape(bsz, nc, L, G, N)
    da = dtc * a.reshape(G, R)
    acum = jnp.cumsum(da, axis=2).transpose(0, 1, 3, 4, 2)
    xdt = xc * dtc[..., None]
    idx = jnp.arange(L)
    causal = idx[:, None] >= idx[None, :]
    decay = jnp.exp(jnp.where(causal, acum[..., :, None] - acum[..., None, :], -jnp.inf))
    cb = jnp.einsum('bclgn,bcsgn->bcgls', cc, bc, preferred_element_type=jnp.float32)
    y_diag = jnp.einsum('bcgls,bcgrls,bcsgrp->bclgrp', cb, decay, xdt)
    decay_to_end = jnp.exp(acum[..., -1:] - acum)
    states = jnp.einsum('bcsgn,bcgrs,bcsgrp->bcgrpn', bc, decay_to_end, xdt)
    chunk_decay = jnp.exp(acum[..., -1])

    def step(h, inp):
        st, dec = inp
        return h * dec[..., None, None] + st, h

    h0 = jnp.zeros((bsz, G, R, hd, N), jnp.float32)
    _, h_prev = lax.scan(step, h0, (states.transpose(1, 0, 2, 3, 4, 5), chunk_decay.transpose(1, 0, 2, 3)))
    h_prev = h_prev.transpose(1, 0, 2, 3, 4, 5)
    y_off = jnp.einsum('bclgn,bcgrpn,bcgrl->bclgrp', cc, h_prev, jnp.exp(acum))
    return (y_diag + y_off).reshape(bsz, seq, heads, hd)


def mamba2_branch(z, xbc, dt_raw, conv_w, conv_b, dt_bias, a_log, d_skip, norm_w):
    bsz, seq, _ = z.shape
    xbc = jax.nn.silu(causal_depthwise_conv(xbc, conv_w, conv_b))
    xs, bm, cm = jnp.split(xbc, [SSM_INNER, SSM_INNER + SSM_GROUPS * SSM_STATE], axis=-1)
    xs = xs.reshape(bsz, seq, SSM_HEADS, SSM_HEAD_DIM)
    bm = bm.reshape(bsz, seq, SSM_GROUPS, SSM_STATE)
    cm = cm.reshape(bsz, seq, SSM_GROUPS, SSM_STATE)
    dt = jax.nn.softplus(dt_raw.astype(jnp.float32) + dt_bias.astype(jnp.float32))
    a = -jnp.exp(a_log.astype(jnp.float32))
    y = ssd_chunked(xs, dt, a, bm, cm) + d_skip.astype(jnp.float32)[:, None] * xs
    u = (y.reshape(bsz, seq, SSM_INNER) * jax.nn.silu(z.astype(jnp.float32))).reshape(bsz, seq, SSM_GROUPS, -1)
    u = u * lax.rsqrt(jnp.mean(jnp.square(u), axis=-1, keepdims=True) + RMS_EPS)
    return (u.reshape(bsz, seq, SSM_INNER) * norm_w.astype(jnp.float32)).astype(z.dtype)


def _fwd_setup_inputs(seed: int = 0) -> dict:
    key = jax.random.key(seed)
    ks = jax.random.split(key, 20)
    f32 = jnp.float32

    def nrm(k, shape, fan_in, mult=1.0):
        return jax.random.normal(k, shape, f32) * (fan_in ** -0.5) * mult

    dt0 = jnp.exp(jax.random.uniform(ks[5], (DEPTH, SSM_HEADS), f32, math.log(1e-3), math.log(1e-1)))
    return {
        "x": jax.random.normal(ks[0], (BATCH, SEQ, D_MODEL), f32),
        "w_in": nrm(ks[1], (DEPTH, D_MODEL, IN_WIDTH), D_MODEL),
        "b_forget": jax.random.uniform(ks[2], (DEPTH, ATT_HEADS), f32, 1.0, 6.0),
        "conv_w": jax.random.uniform(ks[3], (DEPTH, SSM_CONV, SSM_CONV_DIM), f32, -0.5, 0.5),
        "conv_b": 0.02 * jax.random.normal(ks[4], (DEPTH, SSM_CONV_DIM), f32),
        "dt_bias": dt0 + jnp.log(-jnp.expm1(-dt0)),
        "a_log": jnp.log(jax.random.uniform(ks[6], (DEPTH, SSM_HEADS), f32, 1.0, 16.0)),
        "d_skip": 1.0 + 0.1 * jax.random.normal(ks[7], (DEPTH, SSM_HEADS), f32),
        "ssm_norm_w": 1.0 + 0.1 * jax.random.normal(ks[8], (DEPTH, SSM_INNER), f32),
        "w_proj_attn": nrm(ks[9], (DEPTH, ATT_WIDTH, D_MODEL), ATT_WIDTH, DEEPNORM_BETA),
        "w_proj_ssm": nrm(ks[10], (DEPTH, SSM_INNER, D_MODEL), SSM_INNER, DEEPNORM_BETA),
        "b_gates": 0.1 * jax.random.normal(ks[11], (DEPTH, N_BRANCHES * D_MODEL), f32),
        "w_out": nrm(ks[12], (DEPTH, D_MODEL, D_MODEL), D_MODEL, DEEPNORM_BETA),
        "ln1_g": 1.0 + 0.1 * jax.random.normal(ks[13], (DEPTH, D_MODEL), f32),
        "ln1_b": 0.02 * jax.random.normal(ks[14], (DEPTH, D_MODEL), f32),
        "w_ffn_gate": nrm(ks[15], (DEPTH, D_MODEL, FFN_HIDDEN), D_MODEL),
        "w_ffn_up": nrm(ks[16], (DEPTH, D_MODEL, FFN_HIDDEN), D_MODEL),
        "w_ffn_down": nrm(ks[17], (DEPTH, FFN_HIDDEN, D_MODEL), FFN_HIDDEN, DEEPNORM_BETA),
        "ln2_g": 1.0 + 0.1 * jax.random.normal(ks[18], (DEPTH, D_MODEL), f32),
        "ln2_b": 0.02 * jax.random.normal(ks[19], (DEPTH, D_MODEL), f32),
    }


def _fwd_reference(x, w_in, b_forget, conv_w, conv_b, dt_bias, a_log, d_skip, ssm_norm_w, w_proj_attn,
              w_proj_ssm, b_gates, w_out, ln1_g, ln1_b, w_ffn_gate, w_ffn_up, w_ffn_down, ln2_g, ln2_b):
    bsz, seq, _ = x.shape
    split_idx = [int(i) for i in np.cumsum(IN_SIZES)[:-1]]
    for l in range(DEPTH):
        proj = x @ w_in[l]
        q, k, v, f_logit, z, xbc, dt_raw, gate_logit = jnp.split(proj, split_idx, axis=-1)
        log_f = jax.nn.log_sigmoid(f_logit.astype(jnp.float32) + b_forget[l].astype(jnp.float32))
        hs = (bsz, seq, ATT_HEADS, ATT_HEAD_DIM)
        attn = forgetting_attention(q.reshape(hs), k.reshape(hs), v.reshape(hs), log_f)
        attn_d = attn @ w_proj_attn[l]
        ssm = mamba2_branch(z, xbc, dt_raw, conv_w[l], conv_b[l], dt_bias[l], a_log[l], d_skip[l], ssm_norm_w[l])
        ssm_d = ssm @ w_proj_ssm[l]
        gates = jax.nn.sigmoid(gate_logit + b_gates[l]).reshape(bsz, seq, N_BRANCHES, D_MODEL)
        mixed = (gates[:, :, 0] * attn_d + gates[:, :, 1] * ssm_d) @ w_out[l]
        x = layer_norm(DEEPNORM_ALPHA * x + mixed, ln1_g[l], ln1_b[l])
        h = (jax.nn.silu(x @ w_ffn_gate[l]) * (x @ w_ffn_up[l])) @ w_ffn_down[l]
        x = layer_norm(DEEPNORM_ALPHA * x + h, ln2_g[l], ln2_b[l])
    return x


import jax as _jax
import jax.numpy as _jnp

TWIN_FORMAT = 'train_step'
FWD_PARAMS = ['x', 'w_in', 'b_forget', 'conv_w', 'conv_b', 'dt_bias', 'a_log', 'd_skip', 'ssm_norm_w', 'w_proj_attn', 'w_proj_ssm', 'b_gates', 'w_out', 'ln1_g', 'ln1_b', 'w_ffn_gate', 'w_ffn_up', 'w_ffn_down', 'ln2_g', 'ln2_b']
TWIN_WEIGHTS = ['w_in', 'b_forget', 'conv_w', 'conv_b', 'dt_bias', 'a_log', 'd_skip', 'ssm_norm_w', 'w_proj_attn', 'w_proj_ssm', 'b_gates', 'w_out', 'ln1_g', 'ln1_b', 'w_ffn_gate', 'w_ffn_up', 'w_ffn_down', 'ln2_g', 'ln2_b']
TWIN_DIFF_INPUT = 'x'
TWIN_INPUTS = ['x', 'w_in', 'b_forget', 'conv_w', 'conv_b', 'dt_bias', 'a_log', 'd_skip', 'ssm_norm_w', 'w_proj_attn', 'w_proj_ssm', 'b_gates', 'w_out', 'ln1_g', 'ln1_b', 'w_ffn_gate', 'w_ffn_up', 'w_ffn_down', 'ln2_g', 'ln2_b', 'loss_target', 'm_w_in', 'm_b_forget', 'm_conv_w', 'm_conv_b', 'm_dt_bias', 'm_a_log', 'm_d_skip', 'm_ssm_norm_w', 'm_w_proj_attn', 'm_w_proj_ssm', 'm_b_gates', 'm_w_out', 'm_ln1_g', 'm_ln1_b', 'm_w_ffn_gate', 'm_w_ffn_up', 'm_w_ffn_down', 'm_ln2_g', 'm_ln2_b', 'v_w_in', 'v_b_forget', 'v_conv_w', 'v_conv_b', 'v_dt_bias', 'v_a_log', 'v_d_skip', 'v_ssm_norm_w', 'v_w_proj_attn', 'v_w_proj_ssm', 'v_b_gates', 'v_w_out', 'v_ln1_g', 'v_ln1_b', 'v_w_ffn_gate', 'v_w_ffn_up', 'v_w_ffn_down', 'v_ln2_g', 'v_ln2_b']
TWIN_OUTPUTS = ['loss', 'grad_x', 'grad_w_in', 'grad_b_forget', 'grad_conv_w', 'grad_conv_b', 'grad_dt_bias', 'grad_a_log', 'grad_d_skip', 'grad_ssm_norm_w', 'grad_w_proj_attn', 'grad_w_proj_ssm', 'grad_b_gates', 'grad_w_out', 'grad_ln1_g', 'grad_ln1_b', 'grad_w_ffn_gate', 'grad_w_ffn_up', 'grad_w_ffn_down', 'grad_ln2_g', 'grad_ln2_b', 'delta_w_in', 'delta_b_forget', 'delta_conv_w', 'delta_conv_b', 'delta_dt_bias', 'delta_a_log', 'delta_d_skip', 'delta_ssm_norm_w', 'delta_w_proj_attn', 'delta_w_proj_ssm', 'delta_b_gates', 'delta_w_out', 'delta_ln1_g', 'delta_ln1_b', 'delta_w_ffn_gate', 'delta_w_ffn_up', 'delta_w_ffn_down', 'delta_ln2_g', 'delta_ln2_b', 'new_m_w_in', 'new_m_b_forget', 'new_m_conv_w', 'new_m_conv_b', 'new_m_dt_bias', 'new_m_a_log', 'new_m_d_skip', 'new_m_ssm_norm_w', 'new_m_w_proj_attn', 'new_m_w_proj_ssm', 'new_m_b_gates', 'new_m_w_out', 'new_m_ln1_g', 'new_m_ln1_b', 'new_m_w_ffn_gate', 'new_m_w_ffn_up', 'new_m_w_ffn_down', 'new_m_ln2_g', 'new_m_ln2_b', 'new_v_w_in', 'new_v_b_forget', 'new_v_conv_w', 'new_v_conv_b', 'new_v_dt_bias', 'new_v_a_log', 'new_v_d_skip', 'new_v_ssm_norm_w', 'new_v_w_proj_attn', 'new_v_w_proj_ssm', 'new_v_b_gates', 'new_v_w_out', 'new_v_ln1_g', 'new_v_ln1_b', 'new_v_w_ffn_gate', 'new_v_w_ffn_up', 'new_v_w_ffn_down', 'new_v_ln2_g', 'new_v_ln2_b']
TWIN_LEAF_KINDS = {'loss': 'loss', 'grad_x': 'grad_x', 'grad_w_in': 'grad_w', 'grad_b_forget': 'grad_w', 'grad_conv_w': 'grad_w', 'grad_conv_b': 'grad_w', 'grad_dt_bias': 'grad_w', 'grad_a_log': 'grad_w', 'grad_d_skip': 'grad_w', 'grad_ssm_norm_w': 'grad_w', 'grad_w_proj_attn': 'grad_w', 'grad_w_proj_ssm': 'grad_w', 'grad_b_gates': 'grad_w', 'grad_w_out': 'grad_w', 'grad_ln1_g': 'grad_w', 'grad_ln1_b': 'grad_w', 'grad_w_ffn_gate': 'grad_w', 'grad_w_ffn_up': 'grad_w', 'grad_w_ffn_down': 'grad_w', 'grad_ln2_g': 'grad_w', 'grad_ln2_b': 'grad_w', 'delta_w_in': 'delta_w', 'delta_b_forget': 'delta_w', 'delta_conv_w': 'delta_w', 'delta_conv_b': 'delta_w', 'delta_dt_bias': 'delta_w', 'delta_a_log': 'delta_w', 'delta_d_skip': 'delta_w', 'delta_ssm_norm_w': 'delta_w', 'delta_w_proj_attn': 'delta_w', 'delta_w_proj_ssm': 'delta_w', 'delta_b_gates': 'delta_w', 'delta_w_out': 'delta_w', 'delta_ln1_g': 'delta_w', 'delta_ln1_b': 'delta_w', 'delta_w_ffn_gate': 'delta_w', 'delta_w_ffn_up': 'delta_w', 'delta_w_ffn_down': 'delta_w', 'delta_ln2_g': 'delta_w', 'delta_ln2_b': 'delta_w', 'new_m_w_in': 'new_m', 'new_m_b_forget': 'new_m', 'new_m_conv_w': 'new_m', 'new_m_conv_b': 'new_m', 'new_m_dt_bias': 'new_m', 'new_m_a_log': 'new_m', 'new_m_d_skip': 'new_m', 'new_m_ssm_norm_w': 'new_m', 'new_m_w_proj_attn': 'new_m', 'new_m_w_proj_ssm': 'new_m', 'new_m_b_gates': 'new_m', 'new_m_w_out': 'new_m', 'new_m_ln1_g': 'new_m', 'new_m_ln1_b': 'new_m', 'new_m_w_ffn_gate': 'new_m', 'new_m_w_ffn_up': 'new_m', 'new_m_w_ffn_down': 'new_m', 'new_m_ln2_g': 'new_m', 'new_m_ln2_b': 'new_m', 'new_v_w_in': 'new_v', 'new_v_b_forget': 'new_v', 'new_v_conv_w': 'new_v', 'new_v_conv_b': 'new_v', 'new_v_dt_bias': 'new_v', 'new_v_a_log': 'new_v', 'new_v_d_skip': 'new_v', 'new_v_ssm_norm_w': 'new_v', 'new_v_w_proj_attn': 'new_v', 'new_v_w_proj_ssm': 'new_v', 'new_v_b_gates': 'new_v', 'new_v_w_out': 'new_v', 'new_v_ln1_g': 'new_v', 'new_v_ln1_b': 'new_v', 'new_v_w_ffn_gate': 'new_v', 'new_v_w_ffn_up': 'new_v', 'new_v_w_ffn_down': 'new_v', 'new_v_ln2_g': 'new_v', 'new_v_ln2_b': 'new_v'}


def _forward(args):
    return _fwd_reference(*[args[k] for k in FWD_PARAMS])


def _output_shape():
    def fwd():
        inp = _fwd_setup_inputs(0)
        return _fwd_reference(*[inp[k] for k in FWD_PARAMS])
    out = _jax.eval_shape(fwd)
    return out.shape, out.dtype

N_MICROBATCH = 1
ADAM_LR = 0.001
ADAM_B1 = 0.9
ADAM_B2 = 0.999
ADAM_EPS = 1e-08
ADAM_WD = 0.01
ADAM_STEP = 10
PER_EXAMPLE_BATCH_AXIS = {'x': 0, 'loss_target': 0}
SHARED_INPUTS = []
_WEIGHT_DTYPES = {'w_in': _jnp.float32, 'b_forget': _jnp.float32, 'conv_w': _jnp.float32, 'conv_b': _jnp.float32, 'dt_bias': _jnp.float32, 'a_log': _jnp.float32, 'd_skip': _jnp.float32, 'ssm_norm_w': _jnp.float32, 'w_proj_attn': _jnp.float32, 'w_proj_ssm': _jnp.float32, 'b_gates': _jnp.float32, 'w_out': _jnp.float32, 'ln1_g': _jnp.float32, 'ln1_b': _jnp.float32, 'w_ffn_gate': _jnp.float32, 'w_ffn_up': _jnp.float32, 'w_ffn_down': _jnp.float32, 'ln2_g': _jnp.float32, 'ln2_b': _jnp.float32}
MOMENT_SCALE = {'w_in': 2.179551e-02, 'b_forget': 7.681995e-02, 'conv_w': 4.648672e-02, 'conv_b': 6.814705e-02, 'dt_bias': 3.015057e-02, 'a_log': 2.162726e-02, 'd_skip': 3.190784e-01, 'ssm_norm_w': 3.077811e-02, 'w_proj_attn': 2.329068e-02, 'w_proj_ssm': 7.521008e-02, 'b_gates': 1.361148e-02, 'w_out': 7.805942e-02, 'ln1_g': 1.128108e+01, 'ln1_b': 1.041029e+00, 'w_ffn_gate': 4.614592e-02, 'w_ffn_up': 4.633615e-02, 'w_ffn_down': 1.301218e-01, 'ln2_g': 6.535407e+01, 'ln2_b': 1.636035e+00}


def _to_microbatches(a, axis):
    t = _jnp.moveaxis(a, axis, 0)
    t = t.reshape((N_MICROBATCH, t.shape[0] // N_MICROBATCH) + t.shape[1:])
    return _jnp.moveaxis(t, 1, axis + 1)


def setup_inputs(seed: int = 0) -> dict:
    inp = _fwd_setup_inputs(seed)
    key = _jax.random.fold_in(_jax.random.key(seed), 7919)
    shape, _ = _output_shape()
    out = dict(inp)
    out["loss_target"] = _jax.random.normal(_jax.random.fold_in(key, 0), shape, _jnp.float32)
    for i, name in enumerate(TWIN_WEIGHTS):
        w = inp[name].astype(_jnp.float32)
        if MOMENT_SCALE is None:
            s = _jnp.sqrt(_jnp.mean(_jnp.square(w)) + 1e-30)
        else:
            s = MOMENT_SCALE[name]
        km, kv = _jax.random.split(_jax.random.fold_in(key, i + 1))
        out[name] = w
        out["m_" + name] = s * _jax.random.normal(km, w.shape, _jnp.float32)
        out["v_" + name] = (s * s) * _jax.random.uniform(kv, w.shape, _jnp.float32, 0.5, 1.5)
    if N_MICROBATCH > 1:
        for name, axis in PER_EXAMPLE_BATCH_AXIS.items():
            out[name] = _to_microbatches(out[name], axis)
    return {'x': out['x'], 'w_in': out['w_in'], 'b_forget': out['b_forget'], 'conv_w': out['conv_w'], 'conv_b': out['conv_b'], 'dt_bias': out['dt_bias'], 'a_log': out['a_log'], 'd_skip': out['d_skip'], 'ssm_norm_w': out['ssm_norm_w'], 'w_proj_attn': out['w_proj_attn'], 'w_proj_ssm': out['w_proj_ssm'], 'b_gates': out['b_gates'], 'w_out': out['w_out'], 'ln1_g': out['ln1_g'], 'ln1_b': out['ln1_b'], 'w_ffn_gate': out['w_ffn_gate'], 'w_ffn_up': out['w_ffn_up'], 'w_ffn_down': out['w_ffn_down'], 'ln2_g': out['ln2_g'], 'ln2_b': out['ln2_b'], 'loss_target': out['loss_target'], 'm_w_in': out['m_w_in'], 'm_b_forget': out['m_b_forget'], 'm_conv_w': out['m_conv_w'], 'm_conv_b': out['m_conv_b'], 'm_dt_bias': out['m_dt_bias'], 'm_a_log': out['m_a_log'], 'm_d_skip': out['m_d_skip'], 'm_ssm_norm_w': out['m_ssm_norm_w'], 'm_w_proj_attn': out['m_w_proj_attn'], 'm_w_proj_ssm': out['m_w_proj_ssm'], 'm_b_gates': out['m_b_gates'], 'm_w_out': out['m_w_out'], 'm_ln1_g': out['m_ln1_g'], 'm_ln1_b': out['m_ln1_b'], 'm_w_ffn_gate': out['m_w_ffn_gate'], 'm_w_ffn_up': out['m_w_ffn_up'], 'm_w_ffn_down': out['m_w_ffn_down'], 'm_ln2_g': out['m_ln2_g'], 'm_ln2_b': out['m_ln2_b'], 'v_w_in': out['v_w_in'], 'v_b_forget': out['v_b_forget'], 'v_conv_w': out['v_conv_w'], 'v_conv_b': out['v_conv_b'], 'v_dt_bias': out['v_dt_bias'], 'v_a_log': out['v_a_log'], 'v_d_skip': out['v_d_skip'], 'v_ssm_norm_w': out['v_ssm_norm_w'], 'v_w_proj_attn': out['v_w_proj_attn'], 'v_w_proj_ssm': out['v_w_proj_ssm'], 'v_b_gates': out['v_b_gates'], 'v_w_out': out['v_w_out'], 'v_ln1_g': out['v_ln1_g'], 'v_ln1_b': out['v_ln1_b'], 'v_w_ffn_gate': out['v_w_ffn_gate'], 'v_w_ffn_up': out['v_w_ffn_up'], 'v_w_ffn_down': out['v_w_ffn_down'], 'v_ln2_g': out['v_ln2_g'], 'v_ln2_b': out['v_ln2_b']}


def _loss(weights, diff, rest, loss_target):
    with _jax.named_scope("forward"):
        args = {**rest, TWIN_DIFF_INPUT: diff, **{k: w.astype(_WEIGHT_DTYPES[k]) for k, w in weights.items()}}
        y = _forward(args)
    with _jax.named_scope("loss_head"):
        err = _jnp.square(y.astype(_jnp.float32) - loss_target)
        return 0.5 * _jnp.sum(_jnp.mean(err, axis=-1)) if err.ndim else 0.5 * err


def _adamw(w, g, m, v):
    m = ADAM_B1 * m + (1.0 - ADAM_B1) * g
    v = ADAM_B2 * v + (1.0 - ADAM_B2) * _jnp.square(g)
    m_hat = m / (1.0 - ADAM_B1 ** ADAM_STEP)
    v_hat = v / (1.0 - ADAM_B2 ** ADAM_STEP)
    delta = -ADAM_LR * (m_hat / (_jnp.sqrt(v_hat) + ADAM_EPS) + ADAM_WD * w)
    return delta, m, v


def reference(x, w_in, b_forget, conv_w, conv_b, dt_bias, a_log, d_skip, ssm_norm_w, w_proj_attn, w_proj_ssm, b_gates, w_out, ln1_g, ln1_b, w_ffn_gate, w_ffn_up, w_ffn_down, ln2_g, ln2_b, loss_target, m_w_in, m_b_forget, m_conv_w, m_conv_b, m_dt_bias, m_a_log, m_d_skip, m_ssm_norm_w, m_w_proj_attn, m_w_proj_ssm, m_b_gates, m_w_out, m_ln1_g, m_ln1_b, m_w_ffn_gate, m_w_ffn_up, m_w_ffn_down, m_ln2_g, m_ln2_b, v_w_in, v_b_forget, v_conv_w, v_conv_b, v_dt_bias, v_a_log, v_d_skip, v_ssm_norm_w, v_w_proj_attn, v_w_proj_ssm, v_b_gates, v_w_out, v_ln1_g, v_ln1_b, v_w_ffn_gate, v_w_ffn_up, v_w_ffn_down, v_ln2_g, v_ln2_b):
    given = dict(x=x, w_in=w_in, b_forget=b_forget, conv_w=conv_w, conv_b=conv_b, dt_bias=dt_bias, a_log=a_log, d_skip=d_skip, ssm_norm_w=ssm_norm_w, w_proj_attn=w_proj_attn, w_proj_ssm=w_proj_ssm, b_gates=b_gates, w_out=w_out, ln1_g=ln1_g, ln1_b=ln1_b, w_ffn_gate=w_ffn_gate, w_ffn_up=w_ffn_up, w_ffn_down=w_ffn_down, ln2_g=ln2_g, ln2_b=ln2_b, loss_target=loss_target, m_w_in=m_w_in, m_b_forget=m_b_forget, m_conv_w=m_conv_w, m_conv_b=m_conv_b, m_dt_bias=m_dt_bias, m_a_log=m_a_log, m_d_skip=m_d_skip, m_ssm_norm_w=m_ssm_norm_w, m_w_proj_attn=m_w_proj_attn, m_w_proj_ssm=m_w_proj_ssm, m_b_gates=m_b_gates, m_w_out=m_w_out, m_ln1_g=m_ln1_g, m_ln1_b=m_ln1_b, m_w_ffn_gate=m_w_ffn_gate, m_w_ffn_up=m_w_ffn_up, m_w_ffn_down=m_w_ffn_down, m_ln2_g=m_ln2_g, m_ln2_b=m_ln2_b, v_w_in=v_w_in, v_b_forget=v_b_forget, v_conv_w=v_conv_w, v_conv_b=v_conv_b, v_dt_bias=v_dt_bias, v_a_log=v_a_log, v_d_skip=v_d_skip, v_ssm_norm_w=v_ssm_norm_w, v_w_proj_attn=v_w_proj_attn, v_w_proj_ssm=v_w_proj_ssm, v_b_gates=v_b_gates, v_w_out=v_w_out, v_ln1_g=v_ln1_g, v_ln1_b=v_ln1_b, v_w_ffn_gate=v_w_ffn_gate, v_w_ffn_up=v_w_ffn_up, v_w_ffn_down=v_w_ffn_down, v_ln2_g=v_ln2_g, v_ln2_b=v_ln2_b)
    weights = {n: given[n] for n in TWIN_WEIGHTS}
    shared = {n: given[n] for n in SHARED_INPUTS}
    per_example = {n: given[n] for n in ['x']}
    grad_fn = _jax.value_and_grad(_loss, argnums=(0, 1))

    def one_microbatch(ex, loss_target):
        ex = dict(ex)
        diff = ex.pop(TWIN_DIFF_INPUT)
        return grad_fn(weights, diff, {**shared, **ex}, loss_target)

    if N_MICROBATCH == 1:
        loss, (grad_w, grad_x) = one_microbatch(per_example, given["loss_target"])
    else:
        def body(carry, xs):
            loss_sum, grad_sum = carry
            l_k, (gw_k, gx_k) = one_microbatch(xs[0], xs[1])
            with _jax.named_scope("update"):
                return (loss_sum + l_k, _jax.tree.map(_jnp.add, grad_sum, gw_k)), gx_k

        init = (_jnp.zeros((), _jnp.float32), _jax.tree.map(_jnp.zeros_like, weights))
        (loss, grad_w), grad_x = _jax.lax.scan(body, init, (per_example, given["loss_target"]))
    with _jax.named_scope("update"):
        delta_w, new_m, new_v = {}, {}, {}
        for n in TWIN_WEIGHTS:
            delta_w[n], new_m[n], new_v[n] = _adamw(weights[n], grad_w[n], given["m_" + n], given["v_" + n])
    return (loss, grad_x, *[grad_w[n] for n in TWIN_WEIGHTS], *[delta_w[n] for n in TWIN_WEIGHTS],
            *[new_m[n] for n in TWIN_WEIGHTS], *[new_v[n] for n in TWIN_WEIGHTS])
```

```python
import functools
import math

import jax
import jax.numpy as jnp
from jax import lax
from jax.experimental import pallas as pl
from jax.experimental.pallas import tpu as pltpu

F32, BF16 = jnp.float32, jnp.bfloat16
MESH = pl.DeviceIdType.MESH

D_MODEL = 1024
ATT_HEADS, ATT_HEAD_DIM = 16, 64
SSM_INNER, SSM_HEADS, SSM_GROUPS, SSM_STATE, SSM_CONV = 2048, 32, 4, 128, 4
SSM_CONV_DIM = SSM_INNER + 2 * SSM_GROUPS * SSM_STATE
GROUP_LANES = SSM_INNER // SSM_GROUPS
FFN_HIDDEN = 2816
ALPHA = 2.0 ** 0.25
LN_EPS = 1e-5
RMS_EPS = 1e-5
ADAM_LR, ADAM_B1, ADAM_B2, ADAM_EPS, ADAM_WD, ADAM_STEP = 0.001, 0.9, 0.999, 1e-08, 0.01, 10
IN_SIZES = (1024, 1024, 1024, 16, 2048, 3072, 32, 2048)
IN_WIDTH = sum(IN_SIZES)
RE_WIDTH = 3072 + 2048 + 3072 + 2048 + 128
RE_Z, RE_XBC, RE_GATE, RE_SMALL = 3072, 5120, 8192, 10240

LANES = 128
VMEM_CAP = 60 * 1024 * 1024
NEG = -1e30
TILES = dict(TM=512, TM2=256, TA=512, LC=256, TS=512, TB=256)


def _params(n_axes, vmem_bytes=None):
    return pltpu.CompilerParams(dimension_semantics=("arbitrary",) * n_axes,
                                vmem_limit_bytes=None if vmem_bytes is None else int(min(vmem_bytes, VMEM_CAP)))


def _sigmoid(v):
    return 1.0 / (1.0 + jnp.exp(-v))


def _softplus(v):
    return jnp.maximum(v, 0.0) + jnp.log(1.0 + jnp.exp(-jnp.abs(v)))


def _dot(a, b):
    return lax.dot_general(a, b, (((1,), (0,)), ((), ())), preferred_element_type=F32)


def _dot_nt(a, b):
    return lax.dot_general(a, b, (((1,), (1,)), ((), ())), preferred_element_type=F32)


def _dot_tn(a, b):
    return lax.dot_general(a, b, (((0,), (0,)), ((), ())), preferred_element_type=F32)


def _split3(v):
    hi = v.astype(BF16)
    r1 = v - hi.astype(F32)
    mid = r1.astype(BF16)
    lo = (r1 - mid.astype(F32)).astype(BF16)
    return hi, mid, lo


def _dot_exact_left(m01, v):
    hi, mid, lo = _split3(v)
    return _dot(m01, hi) + _dot(m01, mid) + _dot(m01, lo)


def _dot_exact_right(v, m01, terms=3):
    parts = _split3(v)[:terms]
    out = _dot(parts[0], m01)
    for p in parts[1:]:
        out = out + _dot(p, m01)
    return out


def _mm(name, M, N, tm, tn, lhs, rhs, pairs, e_fn, outs, *, nt=False, a_fn=None, n_cached=0, k_cached=0,
        rows=(), vecs_k=(), vecs_n=(), sums=()):
    ni, nj = M // tm, N // tn
    assert ni * tm == M and nj * tn == N, (name, M, N, tm, tn)
    n_l, n_vk, n_r, n_row, n_vn, n_o, n_s = len(lhs), len(vecs_k), len(rhs), len(rows), len(vecs_n), len(outs), len(sums)

    def body(*refs):
        pos = 0
        l_refs = refs[pos:pos + n_l]; pos += n_l
        vk_refs = refs[pos:pos + n_vk]; pos += n_vk
        r_refs = refs[pos:pos + n_r]; pos += n_r
        row_refs = refs[pos:pos + n_row]; pos += n_row
        vn_refs = refs[pos:pos + n_vn]; pos += n_vn
        o_refs = refs[pos:pos + n_o]; pos += n_o
        s_refs = refs[pos:pos + n_s]; pos += n_s
        i, j = pl.program_id(0), pl.program_id(1)
        if a_fn is not None:
            a_scr = refs[pos]

            @pl.when(j == 0)
            def _():
                vals = a_fn([r[...] for r in l_refs], [r[...] for r in vk_refs])
                for n, v in enumerate(vals):
                    a_scr[n] = v.astype(BF16)

            def get_l(n):
                return a_scr[n]
        else:
            def get_l(n):
                return l_refs[n][...]
        accs = []
        for li, ri in pairs:
            accs.append(_dot_nt(get_l(li), r_refs[ri][...]) if nt else _dot(get_l(li), r_refs[ri][...]))
        out_vals, sum_vals = e_fn(accs, [r[...] for r in row_refs], [r[...] for r in vn_refs], j)
        for r, v in zip(o_refs, out_vals):
            r[...] = v.astype(r.dtype)
        if n_s:
            col = pl.multiple_of(j * tn, LANES)

            @pl.when(i == 0)
            def _():
                for r, v in zip(s_refs, sum_vals):
                    r[:, pl.ds(col, tn)] = v

            @pl.when(i > 0)
            def _():
                for r, v in zip(s_refs, sum_vals):
                    r[:, pl.ds(col, tn)] += v

    in_specs, args, est = [], [], 0
    for arr, width, cb in lhs:
        in_specs.append(pl.BlockSpec((tm, width), lambda i, j, cb=cb: (i, cb)))
        args.append(arr); est += tm * width * arr.dtype.itemsize
    for arr in vecs_k:
        in_specs.append(pl.BlockSpec(arr.shape, lambda i, j: (0, 0)))
        args.append(arr); est += 8 * arr.shape[1] * 4
    for arr, off in rhs:
        if nt:
            in_specs.append(pl.BlockSpec((tn, arr.shape[1]), lambda i, j, off=off: (j + off, 0)))
            est += tn * arr.shape[1] * arr.dtype.itemsize
        else:
            in_specs.append(pl.BlockSpec((arr.shape[0], tn), lambda i, j, off=off: (0, j + off)))
            est += tn * arr.shape[0] * arr.dtype.itemsize
        args.append(arr)
    for arr, off in rows:
        in_specs.append(pl.BlockSpec((tm, tn), lambda i, j, off=off: (i, j + off)))
        args.append(arr); est += tm * tn * arr.dtype.itemsize
    for arr, off in vecs_n:
        in_specs.append(pl.BlockSpec((1, tn), lambda i, j, off=off: (0, j + off)))
        args.append(arr); est += 8 * tn * 4
    out_shape, out_specs = [], []
    for total, dtype, off in outs:
        out_shape.append(jax.ShapeDtypeStruct((M, total), dtype))
        out_specs.append(pl.BlockSpec((tm, tn), lambda i, j, off=off: (i, j + off)))
        est += tm * tn * jnp.dtype(dtype).itemsize
    for total in sums:
        out_shape.append(jax.ShapeDtypeStruct((1, total), F32))
        out_specs.append(pl.BlockSpec((1, total), lambda i, j: (0, 0)))
        est += 8 * total * 4
    scratch = []
    scr_bytes = 0
    if a_fn is not None:
        scratch.append(pltpu.VMEM((n_cached, tm, k_cached), BF16))
        scr_bytes = n_cached * tm * k_cached * 2
    vmem = 2 * est + scr_bytes + (len(pairs) + 4) * tm * tn * 4 + (8 << 20)
    res = pl.pallas_call(body, name=name, grid=(ni, nj), in_specs=in_specs, out_specs=out_specs, out_shape=out_shape,
                         scratch_shapes=scratch, compiler_params=_params(2, vmem))(*args)
    return res


def _mm_tn(name, a, g, ta, tn, ts, a_cols=None, a_off=0):
    S = a.shape[0]
    Ka = a.shape[1] if a_cols is None else a_cols
    N = g.shape[1]
    assert Ka % ta == 0 and N % tn == 0 and S % ts == 0, (name, Ka, N, S)
    aoff = a_off // ta

    def body(a_ref, g_ref, o_ref):
        s = pl.program_id(2)
        part = _dot_tn(a_ref[...], g_ref[...])

        @pl.when(s == 0)
        def _():
            o_ref[...] = part

        @pl.when(s > 0)
        def _():
            o_ref[...] += part

    vmem = 2 * (ts * ta * 2 + ts * tn * 2 + ta * tn * 4) + 2 * ta * tn * 4 + (8 << 20)
    return pl.pallas_call(
        body, name=name, grid=(Ka // ta, N // tn, S // ts),
        in_specs=[pl.BlockSpec((ts, ta), lambda ia, jn, s: (s, ia + aoff)), pl.BlockSpec((ts, tn), lambda ia, jn, s: (s, jn))],
        out_specs=pl.BlockSpec((ta, tn), lambda ia, jn, s: (ia, jn)),
        out_shape=jax.ShapeDtypeStruct((Ka, N), F32), compiler_params=_params(3, vmem))(a, g)


def _mm_k_nt(name, a, b, res, alpha, tm, tk):
    M, K = a.shape
    N = b.shape[0]
    nk = K // tk
    assert nk * tk == K and M % tm == 0

    def body(a_ref, b_ref, r_ref, o_ref, acc):
        k = pl.program_id(1)
        part = _dot_nt(a_ref[...], b_ref[...])

        @pl.when(k == 0)
        def _():
            acc[...] = part

        @pl.when(k > 0)
        def _():
            acc[...] += part

        @pl.when(k == nk - 1)
        def _():
            o_ref[...] = alpha * r_ref[...] + acc[...]

    vmem = 2 * (tm * tk * 2 + N * tk * 2 + 2 * tm * N * 4) + 2 * tm * N * 4 + (8 << 20)
    return pl.pallas_call(
        body, name=name, grid=(M // tm, nk),
        in_specs=[pl.BlockSpec((tm, tk), lambda i, k: (i, k)), pl.BlockSpec((N, tk), lambda i, k: (0, k)),
                  pl.BlockSpec((tm, N), lambda i, k: (i, 0))],
        out_specs=pl.BlockSpec((tm, N), lambda i, k: (i, 0)), out_shape=jax.ShapeDtypeStruct((M, N), F32),
        scratch_shapes=[pltpu.VMEM((tm, N), F32)], compiler_params=_params(2, vmem))(a, b, res)


def _tri(n, upper):
    r = lax.broadcasted_iota(jnp.int32, (n, n), 0)
    c = lax.broadcasted_iota(jnp.int32, (n, n), 1)
    return jnp.where((c >= r) if upper else (c <= r), 1.0, 0.0).astype(BF16)


def _logsig(v):
    return jnp.minimum(v, 0.0) - jnp.log(1.0 + jnp.exp(-jnp.abs(v)))


def _cum_fwd(small, bvec, tb):
    S = small.shape[0]

    def body(x_ref, b_ref, o_ref, carry):
        i = pl.program_id(0)

        @pl.when(i == 0)
        def _():
            carry[...] = jnp.zeros_like(carry)

        logf = _logsig(x_ref[...] + b_ref[...])
        cum = _dot_exact_left(_tri(tb, False), logf) + carry[0:1, :]
        o_ref[...] = cum
        carry[0:1, :] = cum[tb - 1:tb, :]

    return pl.pallas_call(
        body, name="cum_fwd", grid=(S // tb,),
        in_specs=[pl.BlockSpec((tb, LANES), lambda i: (i, 0)), pl.BlockSpec((1, LANES), lambda i: (0, 0))],
        out_specs=pl.BlockSpec((tb, LANES), lambda i: (i, 0)), out_shape=jax.ShapeDtypeStruct((S, LANES), F32),
        scratch_shapes=[pltpu.VMEM((8, LANES), F32)], compiler_params=_params(1))(small, bvec)


def _cum_bwd(dcum_k, dcum_q, small, bvec, tb):
    S = small.shape[0]
    nb = S // tb

    def body(dk_ref, dq_ref, x_ref, b_ref, o_ref, s_ref, carry):
        i = pl.program_id(0)

        @pl.when(i == 0)
        def _():
            carry[...] = jnp.zeros_like(carry)
            s_ref[...] = jnp.zeros_like(s_ref)

        rc = _dot_exact_left(_tri(tb, True), dk_ref[...] + dq_ref[...]) + carry[0:1, :]
        dfl = rc * _sigmoid(-(x_ref[...] + b_ref[...]))
        o_ref[...] = dfl
        s_ref[...] += jnp.sum(dfl, axis=0, keepdims=True)
        carry[0:1, :] = rc[0:1, :]

    rev = lambda i: (nb - 1 - i, 0)
    return pl.pallas_call(
        body, name="cum_bwd", grid=(nb,),
        in_specs=[pl.BlockSpec((tb, LANES), rev)] * 3 + [pl.BlockSpec((1, LANES), lambda i: (0, 0))],
        out_specs=[pl.BlockSpec((tb, LANES), rev), pl.BlockSpec((1, LANES), lambda i: (0, 0))],
        out_shape=[jax.ShapeDtypeStruct((S, LANES), F32), jax.ShapeDtypeStruct((1, LANES), F32)],
        scratch_shapes=[pltpu.VMEM((8, LANES), F32)], compiler_params=_params(1))(dcum_k, dcum_q, small, bvec)


def _lane_mask():
    return lax.broadcasted_iota(jnp.int32, (1, LANES), 1) < ATT_HEAD_DIM


def _attn_fwd(qkv, cq, ck_rows, T):
    S = qkv.shape[0]
    nq = S // T
    HP = ATT_HEADS // 2

    def body(q_ref, k_ref, v_ref, cq_ref, ck_ref, o_ref, o32_ref, lse_ref):
        i = pl.program_id(1)
        mA = _lane_mask()
        lane = lax.broadcasted_iota(jnp.int32, (1, LANES), 1)
        q2 = q_ref[...]
        zero = jnp.zeros_like(q2)
        qs = (jnp.where(mA, q2, zero), jnp.where(mA, zero, q2))
        cqs = (cq_ref[0, :, 0:1], cq_ref[0, :, 1:2])
        row = lax.broadcasted_iota(jnp.int32, (T, T), 0)
        col = lax.broadcasted_iota(jnp.int32, (T, T), 1)

        def block(j, carry, diag):
            off = pl.multiple_of(j * T, T)
            kj = k_ref[pl.ds(off, T), :]
            vj = v_ref[pl.ds(off, T), :]
            m0, l0, m1, l1, acc = carry
            new, alphas, pvs = [], [], []
            for h, (m, l) in enumerate(((m0, l0), (m1, l1))):
                ck = ck_ref[0, h:h + 1, pl.ds(off, T)]
                s = _dot_nt(qs[h], kj) + (cqs[h] - ck)
                if diag:
                    s = jnp.where(row >= col, s, NEG)
                m_new = jnp.maximum(m, jnp.max(s, axis=1, keepdims=True))
                p = jnp.exp(s - m_new)
                alpha = jnp.exp(m - m_new)
                l_new = alpha * l + jnp.sum(p, axis=1, keepdims=True)
                pvs.append(_dot(p.astype(BF16), vj))
                alphas.append(alpha)
                new += [m_new, l_new]
            acc = acc * jnp.where(mA, alphas[0], alphas[1]) + jnp.where(mA, pvs[0], pvs[1])
            return (new[0], new[1], new[2], new[3], acc)

        init = (jnp.full((T, 1), NEG, F32), jnp.zeros((T, 1), F32), jnp.full((T, 1), NEG, F32), jnp.zeros((T, 1), F32),
                jnp.zeros((T, LANES), F32))
        carry = lax.fori_loop(0, i, lambda j, c: block(j, c, False), init)
        m0, l0, m1, l1, acc = block(i, carry, True)
        out = acc / jnp.where(mA, l0, l1)
        o_ref[...] = out.astype(BF16)
        o32_ref[...] = out
        lse0, lse1 = m0 + jnp.log(l0), m1 + jnp.log(l1)
        lse_ref[...] = jnp.where(lane == 0, lse0, jnp.where(lane == 1, lse1, 0.0))

    vmem = 2 * (T * LANES * 2 + 2 * S * LANES * 2 + T * LANES * 4 + 8 * S * 4 + T * LANES * 2 + T * LANES * 4) + 10 * T * T * 4 + (8 << 20)
    return pl.pallas_call(
        body, name="attn_fwd", grid=(HP, nq),
        in_specs=[pl.BlockSpec((T, LANES), lambda hp, i: (i, hp)),
                  pl.BlockSpec((S, LANES), lambda hp, i: (0, HP + hp)),
                  pl.BlockSpec((S, LANES), lambda hp, i: (0, 2 * HP + hp)),
                  pl.BlockSpec((1, T, 2), lambda hp, i: (hp, i, 0)),
                  pl.BlockSpec((1, 2, S), lambda hp, i: (hp, 0, 0))],
        out_specs=[pl.BlockSpec((T, LANES), lambda hp, i: (i, hp))] * 3,
        out_shape=[jax.ShapeDtypeStruct((S, D_MODEL), BF16), jax.ShapeDtypeStruct((S, D_MODEL), F32),
                   jax.ShapeDtypeStruct((S, D_MODEL), F32)],
        compiler_params=_params(2, vmem))(qkv, qkv, qkv, cq, ck_rows)


def _attn_bwd(qkv, do, stats_rows, ck_cols, T):
    S = qkv.shape[0]
    nq = S // T
    HP = ATT_HEADS // 2

    def body(k_ref, v_ref, ck_ref, q_ref, do_ref, st_ref, dq_ref, dk_ref, dv_ref, dck_ref, dcq_ref, dq_acc):
        j = pl.program_id(1)
        mA = _lane_mask()
        lane = lax.broadcasted_iota(jnp.int32, (1, LANES), 1)
        masks = (mA, jnp.logical_not(mA))
        one_lane = (lane == ATT_HEAD_DIM, lane == 0)
        spare = (ATT_HEAD_DIM, 0)

        @pl.when(j == 0)
        def _():
            dq_acc[...] = jnp.zeros_like(dq_acc)

        kj = k_ref[...]
        vj = v_ref[...]
        zero = jnp.zeros_like(kj)
        ks = tuple(jnp.where(one_lane[h], jnp.ones_like(kj), jnp.where(masks[h], kj, zero)) for h in (0, 1))
        cks = (ck_ref[0, :, 0:1], ck_ref[0, :, 1:2])
        row = lax.broadcasted_iota(jnp.int32, (T, T), 0)
        col = lax.broadcasted_iota(jnp.int32, (T, T), 1)

        def block(i, carry, diag):
            dv_acc, dk0, dk1 = carry
            off = pl.multiple_of(i * T, T)
            qi = q_ref[pl.ds(off, T), :]
            doi = do_ref[pl.ds(off, T), :]
            dks = [dk0, dk1]
            for h in (0, 1):
                qh = jnp.where(masks[h], qi, zero)
                doh = jnp.where(masks[h], doi, zero)
                cq = st_ref[0, h:h + 1, pl.ds(off, T)]
                lse = st_ref[0, 2 + h:3 + h, pl.ds(off, T)]
                dd = st_ref[0, 4 + h:5 + h, pl.ds(off, T)]
                st = _dot_nt(kj, qh) + (cq - cks[h])
                if diag:
                    st = jnp.where(row <= col, st, NEG)
                pt = jnp.exp(st - lse)
                dpt = _dot_nt(vj, doh)
                dst = (pt * (dpt - dd)).astype(BF16)
                dv_acc = dv_acc + _dot(pt.astype(BF16), doh)
                q_aug = jnp.where(one_lane[h], jnp.ones_like(qh), qh)
                dks[h] = dks[h] + _dot(dst, q_aug)
                dq_acc[h, pl.ds(off, T), :] += _dot_tn(dst, ks[h])
            return (dv_acc, dks[0], dks[1])

        z = jnp.zeros((T, LANES), F32)
        carry = block(j, (z, z, z), True)
        dv_acc, dk0, dk1 = lax.fori_loop(j + 1, nq, lambda i, c: block(i, c, False), carry)
        dv_ref[...] = dv_acc.astype(BF16)
        dk_ref[...] = jnp.where(mA, dk0, dk1).astype(BF16)
        dck_ref[...] = jnp.where(lane == 0, -dk0[:, spare[0]:spare[0] + 1], jnp.where(lane == 1, -dk1[:, spare[1]:spare[1] + 1], 0.0))

        @pl.when(j == nq - 1)
        def _():
            dq0, dq1 = dq_acc[0], dq_acc[1]
            dq_ref[...] = (jnp.where(mA, dq0, dq1) * (1.0 / math.sqrt(ATT_HEAD_DIM))).astype(BF16)
            dcq_ref[...] = jnp.where(lane == 0, dq0[:, spare[0]:spare[0] + 1], jnp.where(lane == 1, dq1[:, spare[1]:spare[1] + 1], 0.0))

    vmem = (2 * (2 * T * LANES * 2 + T * LANES * 4 + 2 * S * LANES * 2 + 8 * S * 4 + S * LANES * 2 + S * LANES * 4
                 + 2 * T * LANES * 2 + T * LANES * 4) + 2 * S * LANES * 4 + 12 * T * T * 4 + (8 << 20))
    return pl.pallas_call(
        body, name="attn_bwd", grid=(HP, nq),
        in_specs=[pl.BlockSpec((T, LANES), lambda hp, j: (j, HP + hp)),
                  pl.BlockSpec((T, LANES), lambda hp, j: (j, 2 * HP + hp)),
                  pl.BlockSpec((1, T, 2), lambda hp, j: (hp, j, 0)),
                  pl.BlockSpec((S, LANES), lambda hp, j: (0, hp)),
                  pl.BlockSpec((S, LANES), lambda hp, j: (0, hp)),
                  pl.BlockSpec((1, 8, S), lambda hp, j: (hp, 0, 0))],
        out_specs=[pl.BlockSpec((S, LANES), lambda hp, j: (0, hp)),
                   pl.BlockSpec((T, LANES), lambda hp, j: (j, hp)),
                   pl.BlockSpec((T, LANES), lambda hp, j: (j, hp)),
                   pl.BlockSpec((T, LANES), lambda hp, j: (j, hp)),
                   pl.BlockSpec((S, LANES), lambda hp, j: (0, hp))],
        out_shape=[jax.ShapeDtypeStruct((S, D_MODEL), BF16)] * 3 + [jax.ShapeDtypeStruct((S, D_MODEL), F32)] * 2,
        scratch_shapes=[pltpu.VMEM((2, S, LANES), F32)], compiler_params=_params(2, vmem))(qkv, qkv, ck_cols, qkv, do, stats_rows)


HALO = 8


def _conv_fwd(u, w, b, ts, tc):
    S, C = u.shape
    hb = ts // HALO

    def body(u_ref, prev_ref, w_ref, b_ref, o_ref, ext):
        i = pl.program_id(0)
        ext[0:HALO, :] = jnp.where(i == 0, 0.0, prev_ref[...])
        ext[HALO:HALO + ts, :] = u_ref[...]
        acc = b_ref[...] + w_ref[3:4, :] * u_ref[...]
        for k in range(SSM_CONV - 1):
            d = SSM_CONV - 1 - k
            acc = acc + w_ref[k:k + 1, :] * ext[HALO - d:HALO - d + ts, :]
        o_ref[...] = acc * _sigmoid(acc)

    return pl.pallas_call(
        body, name="conv_fwd", grid=(S // ts, C // tc),
        in_specs=[pl.BlockSpec((ts, tc), lambda i, j: (i, j)),
                  pl.BlockSpec((HALO, tc), lambda i, j: (jnp.maximum(i * hb - 1, 0), j)),
                  pl.BlockSpec((SSM_CONV, tc), lambda i, j: (0, j)), pl.BlockSpec((1, tc), lambda i, j: (0, j))],
        out_specs=pl.BlockSpec((ts, tc), lambda i, j: (i, j)), out_shape=jax.ShapeDtypeStruct((S, C), F32),
        scratch_shapes=[pltpu.VMEM((ts + HALO, tc), F32)], compiler_params=_params(2))(u, u, w, b)


def _conv_bwd(u, dy, w, b, ts, tc):
    S, C = u.shape
    hb = ts // HALO
    nb = S // ts
    E = ts + 2 * HALO

    def body(u_ref, uprev_ref, unext_ref, dy_ref, dynext_ref, w_ref, b_ref, du_ref, dw_ref, db_ref, uext, gext):
        i = pl.program_id(1)
        last = i == nb - 1
        uext[0:HALO, :] = jnp.where(i == 0, 0.0, uprev_ref[...])
        uext[HALO:HALO + ts, :] = u_ref[...]
        uext[HALO + ts:E, :] = unext_ref[...]
        n = ts + HALO
        pre = b_ref[...] + w_ref[3:4, :] * uext[HALO:HALO + n, :]
        for k in range(SSM_CONV - 1):
            d = SSM_CONV - 1 - k
            pre = pre + w_ref[k:k + 1, :] * uext[HALO - d:HALO - d + n, :]
        sg = _sigmoid(pre)
        dsilu = sg * (1.0 + pre * (1.0 - sg))
        gext[0:ts, :] = dy_ref[...] * dsilu[0:ts, :]
        gext[ts:n, :] = jnp.where(last, 0.0, dynext_ref[...] * dsilu[ts:n, :])
        g = gext[0:ts, :]
        du = w_ref[3:4, :] * g
        for k in range(SSM_CONV - 1):
            d = SSM_CONV - 1 - k
            du = du + w_ref[k:k + 1, :] * gext[d:d + ts, :]
        du_ref[...] = du.astype(du_ref.dtype)
        dws = [jnp.sum(g * uext[HALO - (SSM_CONV - 1 - k):HALO - (SSM_CONV - 1 - k) + ts, :], axis=0, keepdims=True)
               for k in range(SSM_CONV)]
        dbs = jnp.sum(g, axis=0, keepdims=True)

        @pl.when(i == 0)
        def _():
            for k in range(SSM_CONV):
                dw_ref[k:k + 1, :] = dws[k]
            db_ref[...] = dbs

        @pl.when(i > 0)
        def _():
            for k in range(SSM_CONV):
                dw_ref[k:k + 1, :] += dws[k]
            db_ref[...] += dbs

    nxt = lambda j, i: (jnp.minimum((i + 1) * hb, S // HALO - 1), j)
    return pl.pallas_call(
        body, name="conv_bwd", grid=(C // tc, nb),
        in_specs=[pl.BlockSpec((ts, tc), lambda j, i: (i, j)),
                  pl.BlockSpec((HALO, tc), lambda j, i: (jnp.maximum(i * hb - 1, 0), j)),
                  pl.BlockSpec((HALO, tc), nxt),
                  pl.BlockSpec((ts, tc), lambda j, i: (i, j)),
                  pl.BlockSpec((HALO, tc), nxt),
                  pl.BlockSpec((SSM_CONV, tc), lambda j, i: (0, j)), pl.BlockSpec((1, tc), lambda j, i: (0, j))],
        out_specs=[pl.BlockSpec((ts, tc), lambda j, i: (i, j)), pl.BlockSpec((SSM_CONV, tc), lambda j, i: (0, j)),
                   pl.BlockSpec((1, tc), lambda j, i: (0, j))],
        out_shape=[jax.ShapeDtypeStruct((S, C), BF16), jax.ShapeDtypeStruct((SSM_CONV, C), F32), jax.ShapeDtypeStruct((1, C), F32)],
        scratch_shapes=[pltpu.VMEM((E, tc), F32), pltpu.VMEM((ts + HALO, tc), F32)],
        compiler_params=_params(2))(u, u, u, dy, dy, w, b)


def _head_sum_matrix():
    r = jnp.right_shift(lax.broadcasted_iota(jnp.int32, (GROUP_LANES, GROUP_LANES), 0), 6)
    c = jnp.right_shift(lax.broadcasted_iota(jnp.int32, (GROUP_LANES, GROUP_LANES), 1), 6)
    return jnp.where(r == c, 1.0, 0.0).astype(BF16)


def _ssd_common(dtb_ref, dtr_ref, bias_b, alog_b, bias_c, alog_c, L):
    a_b = -jnp.exp(alog_b)
    dt = _softplus(dtb_ref[...] + bias_b)
    acum = _dot_exact_left(_tri(L, False), dt * a_b)
    a_c = -jnp.exp(alog_c)
    dtr = _softplus(dtr_ref[0] + bias_c)
    acum_r = _dot_exact_right(dtr * a_c, _tri(L, True))
    return a_b, dt, acum, acum_r


def _ssd_specs(L, nc, rev):
    cc = (lambda c: nc - 1 - c) if rev else (lambda c: c)
    G = SSM_GROUPS
    blk = pl.BlockSpec((L, GROUP_LANES), lambda g, c: (cc(c), g))
    xs = blk
    bm = pl.BlockSpec((L, SSM_STATE), lambda g, c: (cc(c), SSM_INNER // SSM_STATE + g))
    cm = pl.BlockSpec((L, SSM_STATE), lambda g, c: (cc(c), SSM_INNER // SSM_STATE + G + g))
    dtr = pl.BlockSpec((1, 8, L), lambda g, c: (g, 0, cc(c)))
    vec = pl.BlockSpec((1, GROUP_LANES), lambda g, c: (0, g))
    colv = pl.BlockSpec((1, 8, 1), lambda g, c: (g, 0, 0))
    hs = pl.BlockSpec((1, 1, SSM_STATE, GROUP_LANES), lambda g, c: (g, cc(c), 0, 0))
    return blk, xs, bm, cm, dtr, vec, colv, hs


def _ssd_fwd(xbc, z, dtb, dtr, bias_b, alog_b, dskip_b, normw, bias_c, alog_c, L):
    S = z.shape[0]
    nc = S // L
    blk, xs, bm, cm, dtrs, vec, colv, hs = _ssd_specs(L, nc, False)

    def body(x_ref, b_ref, c_ref, z_ref, dtb_ref, dtr_ref, bias_ref, alog_ref, dskip_ref, nw_ref, biasc_ref, alogc_ref,
             y_ref, ssm_ref, hs_ref, h_scr):
        c = pl.program_id(1)

        @pl.when(c == 0)
        def _():
            h_scr[...] = jnp.zeros_like(h_scr)

        mA = _lane_mask()
        a_b, dt, acum, acum_r = _ssd_common(dtb_ref, dtr_ref, bias_ref[...], alog_ref[...], biasc_ref[0], alogc_ref[0], L)
        x = x_ref[...]
        cb, bb = c_ref[...].astype(BF16), b_ref[...].astype(BF16)
        hprev = h_scr[...]
        hs_ref[0, 0] = hprev
        xdt = x * dt
        xdt_b = xdt.astype(BF16)
        gmat = _dot_nt(cb, bb)
        row = lax.broadcasted_iota(jnp.int32, (L, L), 0)
        col = lax.broadcasted_iota(jnp.int32, (L, L), 1)
        parts = []
        for p in range(GROUP_LANES // LANES):
            xp = xdt_b[:, p * LANES:(p + 1) * LANES]
            yd = []
            for hh in (0, 1):
                r = 2 * p + hh
                acol = acum[:, r * ATT_HEAD_DIM:r * ATT_HEAD_DIM + 1]
                arow = acum_r[r:r + 1, :]
                lm = jnp.exp(jnp.where(row >= col, acol - arow, NEG))
                yd.append(_dot((gmat * lm).astype(BF16), xp))
            parts.append(jnp.where(mA, yd[0], yd[1]))
        ydiag = jnp.concatenate(parts, axis=1)
        yoff = jnp.exp(acum) * _dot(cb, hprev.astype(BF16))
        y = ydiag + yoff + dskip_ref[...] * x
        aend = acum[L - 1:L, :]
        wgt = (jnp.exp(aend - acum) * xdt).astype(BF16)
        h_scr[...] = jnp.exp(aend) * hprev + _dot_tn(bb, wgt)
        y_ref[...] = y
        zz = z_ref[...]
        u = y * (zz * _sigmoid(zz))
        rs = lax.rsqrt(jnp.mean(u * u, axis=1, keepdims=True) + RMS_EPS)
        ssm_ref[...] = (u * rs * nw_ref[...]).astype(BF16)

    return pl.pallas_call(
        body, name="ssd_fwd", grid=(SSM_GROUPS, nc),
        in_specs=[xs, bm, cm, blk, blk, dtrs, vec, vec, vec, vec, colv, colv],
        out_specs=[blk, blk, hs],
        out_shape=[jax.ShapeDtypeStruct((S, SSM_INNER), F32), jax.ShapeDtypeStruct((S, SSM_INNER), BF16),
                   jax.ShapeDtypeStruct((SSM_GROUPS, nc, SSM_STATE, GROUP_LANES), F32)],
        scratch_shapes=[pltpu.VMEM((SSM_STATE, GROUP_LANES), F32)],
        compiler_params=_params(2, 48 << 20))(xbc, xbc, xbc, z, dtb, dtr, bias_b, alog_b, dskip_b, normw, bias_c, alog_c)


def _ssd_bwd(xbc, z, y, dssm, hs_all, dtb, dtr, bias_b, alog_b, dskip_b, normw, bias_c, alog_c, L):
    S = z.shape[0]
    nc = S // L
    blk, xs, bm, cm, dtrs, vec, colv, hs = _ssd_specs(L, nc, True)

    def body(x_ref, b_ref, c_ref, z_ref, y_ref, dssm_ref, hs_ref, dtb_ref, dtr_ref, bias_ref, alog_ref, dskip_ref, nw_ref,
             biasc_ref, alogc_ref,
             dx_ref, db_ref, dc_ref, dz_ref, ddt_ref, dnw_ref, ddskip_ref, dbias_ref, dalog_ref, dh_scr):
        c = pl.program_id(1)

        @pl.when(c == 0)
        def _():
            dh_scr[...] = jnp.zeros_like(dh_scr)

        mA = _lane_mask()
        masks = (mA, jnp.logical_not(mA))
        a_b, dt, acum, acum_r = _ssd_common(dtb_ref, dtr_ref, bias_ref[...], alog_ref[...], biasc_ref[0], alogc_ref[0], L)
        x, zz, y, dssm = x_ref[...], z_ref[...], y_ref[...], dssm_ref[...]
        cb, bb = c_ref[...].astype(BF16), b_ref[...].astype(BF16)
        hprev = hs_ref[0, 0]
        hb = hprev.astype(BF16)
        ds = dh_scr[...]
        dsb = ds.astype(BF16)
        dskip = dskip_ref[...]
        aend = acum[L - 1:L, :]
        e_a, e_end = jnp.exp(acum), jnp.exp(aend)
        dte = jnp.exp(aend - acum)
        xdt = x * dt
        xdt_b = xdt.astype(BF16)
        sg = _sigmoid(zz)
        sz = zz * sg
        u = y * sz
        rs = lax.rsqrt(jnp.mean(u * u, axis=1, keepdims=True) + RMS_EPS)
        un = u * rs
        dun = dssm * nw_ref[...]
        du = rs * (dun - un * jnp.mean(dun * un, axis=1, keepdims=True))
        dy = du * sz
        dz_ref[...] = (du * y * sg * (1.0 + zz * (1.0 - sg))).astype(dz_ref.dtype)
        dy_b = dy.astype(BF16)
        dch_b = (dy * e_a).astype(BF16)
        dc = _dot_nt(dch_b, hb)
        dhprev = _dot_tn(cb, dch_b)
        gt = _dot_nt(bb, cb)
        row = lax.broadcasted_iota(jnp.int32, (L, L), 0)
        col = lax.broadcasted_iota(jnp.int32, (L, L), 1)
        dgt = jnp.zeros((L, L), F32)
        parts = []
        for p in range(GROUP_LANES // LANES):
            xp = xdt_b[:, p * LANES:(p + 1) * LANES]
            dyp = dy_b[:, p * LANES:(p + 1) * LANES]
            zero = jnp.zeros_like(dyp)
            acc = None
            for hh in (0, 1):
                r = 2 * p + hh
                acol = acum[:, r * ATT_HEAD_DIM:r * ATT_HEAD_DIM + 1]
                arow = acum_r[r:r + 1, :]
                lmt = jnp.exp(jnp.where(row <= col, arow - acol, NEG))
                dyh = jnp.where(masks[hh], dyp, zero)
                part = _dot((gt * lmt).astype(BF16), dyh)
                acc = part if acc is None else acc + part
                dgt = dgt + _dot_nt(xp, dyh) * lmt
            parts.append(acc)
        dxdt_diag = jnp.concatenate(parts, axis=1)
        dgt_b = dgt.astype(BF16)
        db = _dot(dgt_b, cb)
        dc = dc + _dot_tn(dgt_b, bb)
        dxdt_state = dte * _dot(bb, dsb)
        db = db + _dot_nt((dte * xdt).astype(BF16), dsb)
        dxdt = dxdt_diag + dxdt_state
        dy_r, xdt_r = dy_b.astype(F32), xdt_b.astype(F32)
        dac = dy_r * (y - dskip * x) - xdt_r * dxdt
        tail = jnp.sum(xdt_r * dxdt_state, axis=0, keepdims=True) + e_end * jnp.sum(ds * hprev, axis=0, keepdims=True)
        rowl = lax.broadcasted_iota(jnp.int32, (L, 1), 0)
        dac = dac + jnp.where(rowl == L - 1, tail, 0.0)
        rc = _dot_exact_left(_tri(L, True), dac)
        hsum = _head_sum_matrix()
        hs1 = _dot_exact_right(dxdt * x, hsum, 2)
        hs2 = _dot_exact_right(rc, hsum, 2)
        ddt = hs1 + a_b * hs2
        ddtraw = ddt * _sigmoid(dtb_ref[...] + bias_ref[...])
        dx_ref[...] = dskip * dy + dxdt * dt
        db_ref[...] = db
        dc_ref[...] = dc
        ddt_ref[...] = ddtraw
        dh_scr[...] = e_end * ds + dhprev
        sums = (jnp.sum(dssm * un, axis=0, keepdims=True), jnp.sum(dy * x, axis=0, keepdims=True),
                jnp.sum(ddtraw, axis=0, keepdims=True), a_b * jnp.sum(hs2 * dt, axis=0, keepdims=True))
        refs = (dnw_ref, ddskip_ref, dbias_ref, dalog_ref)

        @pl.when(c == 0)
        def _():
            for r, v in zip(refs, sums):
                r[...] = v

        @pl.when(c > 0)
        def _():
            for r, v in zip(refs, sums):
                r[...] += v

    nbc = pl.BlockSpec((L, SSM_STATE), lambda g, c: (nc - 1 - c, g))
    return pl.pallas_call(
        body, name="ssd_bwd", grid=(SSM_GROUPS, nc),
        in_specs=[xs, bm, cm, blk, blk, blk, hs, blk, dtrs, vec, vec, vec, vec, colv, colv],
        out_specs=[blk, nbc, nbc, blk, blk, vec, vec, vec, vec],
        out_shape=[jax.ShapeDtypeStruct((S, SSM_INNER), F32), jax.ShapeDtypeStruct((S, SSM_GROUPS * SSM_STATE), F32),
                   jax.ShapeDtypeStruct((S, SSM_GROUPS * SSM_STATE), F32), jax.ShapeDtypeStruct((S, SSM_INNER), BF16),
                   jax.ShapeDtypeStruct((S, SSM_INNER), F32)] + [jax.ShapeDtypeStruct((1, SSM_INNER), F32)] * 4,
        scratch_shapes=[pltpu.VMEM((SSM_STATE, GROUP_LANES), F32)],
        compiler_params=_params(2, 56 << 20))(xbc, xbc, xbc, z, y, dssm, hs_all, dtb, dtr, bias_b, alog_b, dskip_b, normw,
                                              bias_c, alog_c)


def _place():
    return lax.axis_index("x"), lax.axis_index("y"), lax.axis_index("c")


def _other_chips(x, y):
    return [(1 - x, y), (x, 1 - y), (1 - x, 1 - y)]


def _chip_gather(name, shards):
    n = len(shards)
    ANY = pl.BlockSpec(memory_space=pl.ANY)

    def body(*refs):
        ins, outs = refs[:n], refs[n:2 * n]
        send, recv, loc = refs[2 * n:]
        x, y, c = _place()
        me = 2 * x + y
        local = [pltpu.make_async_copy(ins[a], outs[a].at[me], loc.at[a]) for a in range(n)]
        for cp in local:
            cp.start()
        sends = []
        for k, (ox, oy) in enumerate(_other_chips(x, y)):
            for a in range(n):
                cp = pltpu.make_async_remote_copy(src_ref=ins[a], dst_ref=outs[a].at[me], send_sem=send.at[k, a],
                                                  recv_sem=recv.at[k, a], device_id=(ox, oy, c), device_id_type=MESH)
                cp.start()
                sends.append(cp)
        for k, (ox, oy) in enumerate(_other_chips(x, y)):
            src = 2 * ox + oy
            for a in range(n):
                pltpu.make_async_remote_copy(src_ref=ins[a], dst_ref=outs[a].at[src], send_sem=send.at[k, a],
                                             recv_sem=recv.at[k, a], device_id=(ox, oy, c), device_id_type=MESH).wait_recv()
        for cp in sends:
            cp.wait_send()
        for cp in local:
            cp.wait()

    return pl.pallas_call(
        body, name=name, in_specs=[ANY] * n, out_specs=[ANY] * n,
        out_shape=[jax.ShapeDtypeStruct((4,) + s.shape, s.dtype) for s in shards],
        scratch_shapes=[pltpu.SemaphoreType.DMA((3, n)), pltpu.SemaphoreType.DMA((3, n)), pltpu.SemaphoreType.DMA((n,))],
    )(*shards)


def _chip_scatter(name, blocks):
    n = len(blocks)
    ANY = pl.BlockSpec(memory_space=pl.ANY)

    def body(*refs):
        ins, outs = refs[:n], refs[n:2 * n]
        send, recv, loc = refs[2 * n:]
        x, y, c = _place()
        me = 2 * x + y
        local = [pltpu.make_async_copy(ins[a].at[me], outs[a].at[me], loc.at[a]) for a in range(n)]
        for cp in local:
            cp.start()
        sends = []
        for k, (ox, oy) in enumerate(_other_chips(x, y)):
            dst_chip = 2 * ox + oy
            for a in range(n):
                cp = pltpu.make_async_remote_copy(src_ref=ins[a].at[dst_chip], dst_ref=outs[a].at[me], send_sem=send.at[k, a],
                                                  recv_sem=recv.at[k, a], device_id=(ox, oy, c), device_id_type=MESH)
                cp.start()
                sends.append(cp)
        for k, (ox, oy) in enumerate(_other_chips(x, y)):
            src = 2 * ox + oy
            for a in range(n):
                pltpu.make_async_remote_copy(src_ref=ins[a].at[me], dst_ref=outs[a].at[src], send_sem=send.at[k, a],
                                             recv_sem=recv.at[k, a], device_id=(ox, oy, c), device_id_type=MESH).wait_recv()
        for cp in sends:
            cp.wait_send()
        for cp in local:
            cp.wait()

    return pl.pallas_call(
        body, name=name, in_specs=[ANY] * n, out_specs=[ANY] * n,
        out_shape=[jax.ShapeDtypeStruct(b.shape, b.dtype) for b in blocks],
        scratch_shapes=[pltpu.SemaphoreType.DMA((3, n)), pltpu.SemaphoreType.DMA((3, n)), pltpu.SemaphoreType.DMA((n,))],
    )(*blocks)


def _sibling_swap(name, arrs):
    n = len(arrs)
    ANY = pl.BlockSpec(memory_space=pl.ANY)

    def body(*refs):
        ins, outs = refs[:n], refs[n:2 * n]
        send, recv = refs[2 * n:]
        x, y, c = _place()
        cps = [pltpu.make_async_remote_copy(src_ref=ins[a], dst_ref=outs[a], send_sem=send.at[a], recv_sem=recv.at[a],
                                            device_id=(x, y, 1 - c), device_id_type=MESH) for a in range(n)]
        for cp in cps:
            cp.start()
        for cp in cps:
            cp.wait_recv()
        for cp in cps:
            cp.wait_send()

    return pl.pallas_call(
        body, name=name, in_specs=[ANY] * n, out_specs=[ANY] * n,
        out_shape=[jax.ShapeDtypeStruct(a.shape, a.dtype) for a in arrs],
        scratch_shapes=[pltpu.SemaphoreType.DMA((n,)), pltpu.SemaphoreType.DMA((n,))])(*arrs)


N_DEV = 8


def _all_sum_small(vec):
    P = vec.shape[1]

    def body(v_ref, o_ref, buf, send, recv):
        x, y, c = _place()
        me = 4 * x + 2 * y + c
        buf[me] = v_ref[...]
        sends = []
        for r in range(1, N_DEV):
            fx, fy, fc = (r >> 2) & 1, (r >> 1) & 1, r & 1
            peer = ((1 - x) if fx else x, (1 - y) if fy else y, (1 - c) if fc else c)
            cp = pltpu.make_async_remote_copy(src_ref=v_ref, dst_ref=buf.at[me], send_sem=send.at[r], recv_sem=recv.at[r],
                                              device_id=peer, device_id_type=MESH)
            cp.start()
            sends.append(cp)
        for r in range(1, N_DEV):
            fx, fy, fc = (r >> 2) & 1, (r >> 1) & 1, r & 1
            px, py, pc = (1 - x) if fx else x, (1 - y) if fy else y, (1 - c) if fc else c
            src = 4 * px + 2 * py + pc
            pltpu.make_async_remote_copy(src_ref=v_ref, dst_ref=buf.at[src], send_sem=send.at[r], recv_sem=recv.at[r],
                                         device_id=(px, py, pc), device_id_type=MESH).wait_recv()
        for cp in sends:
            cp.wait_send()
        tot = buf[0]
        for d in range(1, N_DEV):
            tot = tot + buf[d]
        o_ref[...] = tot

    return pl.pallas_call(
        body, name="all_sum_small", in_specs=[pl.BlockSpec(memory_space=pltpu.VMEM)],
        out_specs=pl.BlockSpec(memory_space=pltpu.VMEM), out_shape=jax.ShapeDtypeStruct((1, P), F32),
        scratch_shapes=[pltpu.VMEM((N_DEV, 1, P), F32), pltpu.SemaphoreType.DMA((N_DEV,)), pltpu.SemaphoreType.DMA((N_DEV,))],
    )(vec)


def _sum4(name, stack, tr):
    _, R, C = stack.shape

    def body(s_ref, o_ref):
        o_ref[...] = ((s_ref[0] + s_ref[1]) + s_ref[2]) + s_ref[3]

    return pl.pallas_call(body, name=name, grid=(R // tr,), in_specs=[pl.BlockSpec((4, tr, C), lambda i: (0, i, 0))],
                          out_specs=pl.BlockSpec((tr, C), lambda i: (i, 0)), out_shape=jax.ShapeDtypeStruct((R, C), F32),
                          compiler_params=_params(1, 40 << 20))(stack)


def _adamw(name, w, m, v, ga, gb, tr):
    R, C = w.shape
    c1 = 1.0 - ADAM_B1 ** ADAM_STEP
    c2 = 1.0 - ADAM_B2 ** ADAM_STEP
    two = gb is not None

    def body(*refs):
        if two:
            w_ref, m_ref, v_ref, ga_ref, gb_ref, g_ref, d_ref, nm_ref, nv_ref = refs
            g = ga_ref[...] + gb_ref[...]
        else:
            w_ref, m_ref, v_ref, ga_ref, g_ref, d_ref, nm_ref, nv_ref = refs
            g = ga_ref[...]
        nm = ADAM_B1 * m_ref[...] + (1.0 - ADAM_B1) * g
        nv = ADAM_B2 * v_ref[...] + (1.0 - ADAM_B2) * (g * g)
        g_ref[...] = g
        nm_ref[...] = nm
        nv_ref[...] = nv
        d_ref[...] = -ADAM_LR * ((nm / c1) / (jnp.sqrt(nv / c2) + ADAM_EPS) + ADAM_WD * w_ref[...])

    spec = pl.BlockSpec((tr, C), lambda i: (i, 0))
    args = (w, m, v, ga) + ((gb,) if two else ())
    return pl.pallas_call(body, name=name, grid=(R // tr,), in_specs=[spec] * len(args), out_specs=[spec] * 4,
                          out_shape=[jax.ShapeDtypeStruct((R, C), F32)] * 4, compiler_params=_params(1, 40 << 20))(*args)


def _row_tile(rows, cols, budget_bytes=1 << 20):
    best = None
    for t in range(8, rows + 1, 8):
        if rows % t == 0 and t * cols * 4 <= budget_bytes:
            best = t
    return best if best is not None else rows


def _ln_fwd(r, g, b):
    mu = jnp.mean(r, axis=1, keepdims=True)
    xc = r - mu
    rstd = lax.rsqrt(jnp.mean(xc * xc, axis=1, keepdims=True) + LN_EPS)
    xhat = xc * rstd
    return xhat * g + b, xhat, rstd


def _ln_bwd(dy, xhat, rstd, g):
    dxh = dy * g
    return rstd * (dxh - jnp.mean(dxh, axis=1, keepdims=True) - xhat * jnp.mean(dxh * xhat, axis=1, keepdims=True))


def _to_chip_blocks_cols(a):
    R, C4 = a.shape
    return a.reshape(R, 4, C4 // 4).transpose(1, 0, 2)


def _from_chip_blocks_cols(a):
    return a.transpose(1, 0, 2).reshape(a.shape[1], 4 * a.shape[2])


def kernel(x, w_in, b_forget, conv_w, conv_b, dt_bias, a_log, d_skip, ssm_norm_w, w_proj_attn, w_proj_ssm, b_gates, w_out, ln1_g, ln1_b, w_ffn_gate, w_ffn_up, w_ffn_down, ln2_g, ln2_b, loss_target, m_w_in, m_b_forget, m_conv_w, m_conv_b, m_dt_bias, m_a_log, m_d_skip, m_ssm_norm_w, m_w_proj_attn, m_w_proj_ssm, m_b_gates, m_w_out, m_ln1_g, m_ln1_b, m_w_ffn_gate, m_w_ffn_up, m_w_ffn_down, m_ln2_g, m_ln2_b, v_w_in, v_b_forget, v_conv_w, v_conv_b, v_dt_bias, v_a_log, v_d_skip, v_ssm_norm_w, v_w_proj_attn, v_w_proj_ssm, v_b_gates, v_w_out, v_ln1_g, v_ln1_b, v_w_ffn_gate, v_w_ffn_up, v_w_ffn_down, v_ln2_g, v_ln2_b):
    S = x.shape[1]
    D = D_MODEL
    TM, TM2, TA, LC, TS, TB = (min(TILES[k], S) for k in ("TM", "TM2", "TA", "LC", "TS", "TB"))
    xf = x[0]
    tgt = loss_target[0]
    xb = xf.astype(BF16)

    g_in, g_cw, g_pa, g_ps, g_out, g_fg, g_fu, g_fd = _chip_gather("gather_weights", [
        w_in[0].astype(BF16), conv_w[0], w_proj_attn[0].astype(BF16), w_proj_ssm[0].astype(BF16), w_out[0].astype(BF16),
        w_ffn_gate[0].astype(BF16), w_ffn_up[0].astype(BF16), w_ffn_down[0].astype(BF16)])
    w_full = _from_chip_blocks_cols(g_in)
    w_re = jnp.concatenate([w_full[:, 0:3072], w_full[:, 3088:5136], w_full[:, 5136:8208], w_full[:, 8240:10288],
                            w_full[:, 3072:3088], w_full[:, 8208:8240], jnp.zeros((D, 80), BF16)], axis=1)
    conv_w_full = _from_chip_blocks_cols(g_cw)
    wpa, wps, wout = g_pa.reshape(D, D), g_ps.reshape(SSM_INNER, D), g_out.reshape(D, D)
    wfg, wfu, wfd = _from_chip_blocks_cols(g_fg), _from_chip_blocks_cols(g_fu), g_fd.reshape(FFN_HIDDEN, D)

    def plain(accs, rows, vecs, j):
        return [accs[0]], []

    def q_scaled(accs, rows, vecs, j):
        return [accs[0] * jnp.where(j * 512 < D, 1.0 / math.sqrt(ATT_HEAD_DIM), 1.0)], []

    qkv, = _mm("proj_qkv", S, 3072, TM, 512, [(xb, D, 0)], [(w_re, 0)], [(0, 0)], q_scaled, [(3072, BF16, 0)])
    z, = _mm("proj_z", S, 2048, TM, 512, [(xb, D, 0)], [(w_re, RE_Z // 512)], [(0, 0)], plain, [(2048, F32, 0)])
    xbc_raw, = _mm("proj_xbc", S, 3072, TM, 512, [(xb, D, 0)], [(w_re, RE_XBC // 512)], [(0, 0)], plain, [(3072, F32, 0)])
    gl, = _mm("proj_gate", S, 2048, TM, 512, [(xb, D, 0)], [(w_re, RE_GATE // 512)], [(0, 0)], plain, [(2048, F32, 0)])
    small, = _mm("proj_small", S, 128, TM, 128, [(xb, D, 0)], [(w_re, RE_SMALL // 128)], [(0, 0)], plain, [(128, F32, 0)])

    bvec = jnp.concatenate([b_forget, jnp.zeros((1, LANES - ATT_HEADS), F32)], axis=1)
    cum = _cum_fwd(small, bvec, TB)[:, :ATT_HEADS]
    cum_rows = cum.T.reshape(8, 2, S)
    cum_cols = cum.reshape(S, 8, 2).transpose(1, 0, 2)
    o, o32, lse2d = _attn_fwd(qkv, cum_cols, cum_rows, TA)

    cb_row = conv_b
    xbc = _conv_fwd(xbc_raw, conv_w_full, cb_row, TS, 512)
    dt_raw = small[:, 16:48]
    dtb = jnp.repeat(dt_raw, ATT_HEAD_DIM, axis=1)
    dtr = dt_raw.T.reshape(SSM_GROUPS, 8, S)
    bias_b = jnp.repeat(dt_bias, ATT_HEAD_DIM, axis=1)
    alog_b = jnp.repeat(a_log, ATT_HEAD_DIM, axis=1)
    dskip_b = jnp.repeat(d_skip, ATT_HEAD_DIM, axis=1)
    bias_c = dt_bias.reshape(SSM_GROUPS, 8, 1)
    alog_c = a_log.reshape(SSM_GROUPS, 8, 1)
    y_ssd, ssm, hs_all = _ssd_fwd(xbc, z, dtb, dtr, bias_b, alog_b, dskip_b, ssm_norm_w, bias_c, alog_c, LC)

    def merge(accs, rows, vecs, j):
        g0, g1 = _sigmoid(rows[0] + vecs[0]), _sigmoid(rows[1] + vecs[1])
        return [g0 * accs[0] + g1 * accs[1], accs[0], accs[1]], []

    mix, attn_d, ssm_d = _mm("merge", S, D, TM, 512, [(o, D, 0), (ssm, SSM_INNER, 0)], [(wpa, 0), (wps, 0)], [(0, 0), (1, 1)],
                             merge, [(D, BF16, 0), (D, F32, 0), (D, F32, 0)], rows=[(gl, 0), (gl, 2)],
                             vecs_n=[(b_gates, 0), (b_gates, 2)])

    def out_ln1(accs, rows, vecs, j):
        r1 = ALPHA * rows[0] + accs[0]
        h1, _, _ = _ln_fwd(r1, vecs[0], vecs[1])
        return [r1, h1, h1], []

    r1, h1, h1b = _mm("out_ln1", S, D, TM2, D, [(mix, D, 0)], [(wout, 0)], [(0, 0)], out_ln1,
                      [(D, F32, 0), (D, F32, 0), (D, BF16, 0)], rows=[(xf, 0)], vecs_n=[(ln1_g, 0), (ln1_b, 0)])

    FT = FFN_HIDDEN // 2

    def swiglu(accs, rows, vecs, j):
        g, u = accs
        return [g, u, g * _sigmoid(g) * u], []

    gate, up, hmid = _mm("ffn_up", S, FFN_HIDDEN, TM2, FT, [(h1b, D, 0)], [(wfg, 0), (wfu, 0)], [(0, 0), (0, 1)], swiglu,
                         [(FFN_HIDDEN, F32, 0), (FFN_HIDDEN, F32, 0), (FFN_HIDDEN, BF16, 0)])

    def down_ln2_loss(accs, rows, vecs, j):
        r2 = ALPHA * rows[0] + accs[0]
        yv, xhat, rstd = _ln_fwd(r2, vecs[0], vecs[1])
        diff = yv - rows[1]
        dy = diff * (1.0 / D_MODEL)
        dr2 = _ln_bwd(dy, xhat, rstd, vecs[0])
        return [dr2, dr2], [jnp.sum(dy * xhat, axis=0, keepdims=True), jnp.sum(dy, axis=0, keepdims=True),
                            (0.5 / D_MODEL) * jnp.sum(diff * diff, axis=0, keepdims=True)]

    dr2, dr2b, dln2_g, dln2_b, loss_lanes = _mm("ffn_down_ln2", S, D, TM2, D, [(hmid, FFN_HIDDEN, 0)], [(wfd, 0)], [(0, 0)],
                                               down_ln2_loss, [(D, F32, 0), (D, BF16, 0)], rows=[(h1, 0), (tgt, 0)],
                                               vecs_n=[(ln2_g, 0), (ln2_b, 0)], sums=[D, D, D])
    loss = lax.psum(jnp.sum(loss_lanes), ("x", "y", "c"))

    def dswiglu(accs, rows, vecs, j):
        g, u = rows
        sg = _sigmoid(g)
        return [accs[0] * u * sg * (1.0 + g * (1.0 - sg)), accs[0] * g * sg], []

    dgate, dup = _mm("ffn_down_bwd", S, FFN_HIDDEN, TM2, FT, [(dr2b, D, 0)], [(wfd, 0)], [(0, 0)], dswiglu,
                     [(FFN_HIDDEN, BF16, 0), (FFN_HIDDEN, BF16, 0)], nt=True, rows=[(gate, 0), (up, 0)])
    dwfd = _mm_tn("dw_ffn_down", hmid, dr2b, FFN_HIDDEN // 2, D, TS)
    dwfg = _mm_tn("dw_ffn_gate", h1b, dgate, D, FT, TS)
    dwfu = _mm_tn("dw_ffn_up", h1b, dup, D, FT, TS)

    def dh1_ln1(accs, rows, vecs, j):
        dh1 = ALPHA * rows[0] + accs[0] + accs[1]
        _, xhat, rstd = _ln_fwd(rows[1], vecs[0], vecs[0])
        dr1 = _ln_bwd(dh1, xhat, rstd, vecs[0])
        return [dr1, dr1], [jnp.sum(dh1 * xhat, axis=0, keepdims=True), jnp.sum(dh1, axis=0, keepdims=True)]

    dr1, dr1b, dln1_g, dln1_b = _mm("ffn_up_bwd_ln1", S, D, TM2, D, [(dgate, FFN_HIDDEN, 0), (dup, FFN_HIDDEN, 0)],
                                    [(wfg, 0), (wfu, 0)], [(0, 0), (1, 1)], dh1_ln1, [(D, F32, 0), (D, BF16, 0)], nt=True,
                                    rows=[(dr2, 0), (r1, 0)], vecs_n=[(ln1_g, 0)], sums=[D, D])

    def dmerge(accs, rows, vecs, j):
        dmix = accs[0]
        g0, g1 = _sigmoid(rows[0] + vecs[0]), _sigmoid(rows[1] + vecs[1])
        dgl0 = dmix * rows[2] * g0 * (1.0 - g0)
        dgl1 = dmix * rows[3] * g1 * (1.0 - g1)
        return [dmix * g0, dmix * g1, dgl0, dgl1], [jnp.sum(dgl0, axis=0, keepdims=True), jnp.sum(dgl1, axis=0, keepdims=True)]

    d_attn_d, d_ssm_d, dgl0, dgl1, dbg0, dbg1 = _mm(
        "out_bwd", S, D, TM, 512, [(dr1b, D, 0)], [(wout, 0)], [(0, 0)], dmerge, [(D, BF16, 0)] * 4, nt=True,
        rows=[(gl, 0), (gl, 2), (attn_d, 0), (ssm_d, 0)], vecs_n=[(b_gates, 0), (b_gates, 2)], sums=[D, D])
    dwout = _mm_tn("dw_out", mix, dr1b, D, D, TS)
    dwpa = _mm_tn("dw_proj_attn", o, d_attn_d, D, D, TS)
    dwps = _mm_tn("dw_proj_ssm", ssm, d_ssm_d, D, D, TS)

    def do_and_rowdot(accs, rows, vecs, j):
        lane = lax.broadcasted_iota(jnp.int32, (1, LANES), 1)
        prod = accs[0].astype(BF16).astype(F32) * rows[0]
        d0 = jnp.sum(jnp.where(lane < ATT_HEAD_DIM, prod, 0.0), axis=1, keepdims=True)
        d1 = jnp.sum(jnp.where(lane < ATT_HEAD_DIM, 0.0, prod), axis=1, keepdims=True)
        return [accs[0], jnp.where(lane == 0, d0, jnp.where(lane == 1, d1, 0.0))], []

    do, dd2d = _mm("proj_attn_bwd", S, D, TM, LANES, [(d_attn_d, D, 0)], [(wpa, 0)], [(0, 0)], do_and_rowdot,
                   [(D, BF16, 0), (D, F32, 0)], nt=True, rows=[(o32, 0)])
    lse_rows = lse2d.reshape(S, 8, LANES)[:, :, 0:2].transpose(1, 2, 0)
    dd_rows = dd2d.reshape(S, 8, LANES)[:, :, 0:2].transpose(1, 2, 0)
    stats_rows = jnp.concatenate([cum_rows, lse_rows, dd_rows, jnp.zeros((8, 2, S), F32)], axis=1)
    dq, dk, dv, dck2d, dcq2d = _attn_bwd(qkv, do, stats_rows, cum_cols, TA)

    def per_head(a2d):
        a = a2d.reshape(S, 8, LANES)[:, :, 0:2].reshape(S, ATT_HEADS)
        return jnp.concatenate([a, jnp.zeros((S, LANES - ATT_HEADS), F32)], axis=1)

    dfl, dbf = _cum_bwd(per_head(dck2d), per_head(dcq2d), small, bvec, TB)

    dssm, = _mm("proj_ssm_bwd", S, SSM_INNER, TM, 512, [(d_ssm_d, D, 0)], [(wps, 0)], [(0, 0)], plain, [(SSM_INNER, F32, 0)],
                nt=True)
    dxs, dbm, dcm, dz, ddtb, dnw, ddskip_b, dbias_b, dalog_b = _ssd_bwd(
        xbc, z, y_ssd, dssm, hs_all, dtb, dtr, bias_b, alog_b, dskip_b, ssm_norm_w, bias_c, alog_c, LC)
    dxbc = jnp.concatenate([dxs, dbm, dcm], axis=1)
    dxbc_raw, dconv_w, dconv_b = _conv_bwd(xbc_raw, dxbc, conv_w_full, cb_row, TS, 512)
    ddt_raw = ddtb[:, ::ATT_HEAD_DIM]

    dsmall = jnp.concatenate([dfl[:, :ATT_HEADS], ddt_raw, jnp.zeros((S, 80), F32)], axis=1).astype(BF16)
    dproj = jnp.concatenate([dq, dk, dv, dz, dxbc_raw, dgl0, dgl1, dsmall], axis=1)
    grad_x = _mm_k_nt("dx", dproj, w_re, dr1, ALPHA, TM, RE_WIDTH // 3)
    dw_re = _mm_tn("dw_in", xb, dproj, D, RE_WIDTH // 9, TS)
    dw_full = jnp.concatenate([dw_re[:, 0:3072], dw_re[:, RE_SMALL:RE_SMALL + 16], dw_re[:, RE_Z:RE_Z + 2048],
                               dw_re[:, RE_XBC:RE_XBC + 3072], dw_re[:, RE_SMALL + 16:RE_SMALL + 48],
                               dw_re[:, RE_GATE:RE_GATE + 2048]], axis=1)

    blocks = [_to_chip_blocks_cols(dw_full), dwpa.reshape(4, D // 4, D), dwps.reshape(4, SSM_INNER // 4, D),
              dwout.reshape(4, D // 4, D), _to_chip_blocks_cols(dwfg), _to_chip_blocks_cols(dwfu),
              dwfd.reshape(4, FFN_HIDDEN // 4, D)]
    stacks = _chip_scatter("scatter_grads", blocks)
    names = ["w_in", "w_proj_attn", "w_proj_ssm", "w_out", "w_ffn_gate", "w_ffn_up", "w_ffn_down"]
    partial = [_sum4("sum_" + nm, st, _row_tile(st.shape[1], st.shape[2])) for nm, st in zip(names, stacks)]
    other = _sibling_swap("swap_partials", partial)
    big_w = [w_in, w_proj_attn, w_proj_ssm, w_out, w_ffn_gate, w_ffn_up, w_ffn_down]
    big_m = [m_w_in, m_w_proj_attn, m_w_proj_ssm, m_w_out, m_w_ffn_gate, m_w_ffn_up, m_w_ffn_down]
    big_v = [v_w_in, v_w_proj_attn, v_w_proj_ssm, v_w_out, v_w_ffn_gate, v_w_ffn_up, v_w_ffn_down]
    big = {}
    for nm, w_, m_, v_, pa, pb in zip(names, big_w, big_m, big_v, partial, other):
        res = _adamw("adamw_" + nm, w_[0], m_[0], v_[0], pa, pb, _row_tile(w_.shape[1], w_.shape[2]))
        big[nm] = [r[None] for r in res]

    pick = lambda a: a[:, ::ATT_HEAD_DIM]
    dd_skip = ddskip_b.reshape(1, SSM_HEADS, ATT_HEAD_DIM).sum(axis=2)
    pieces = [dbf[:, :ATT_HEADS], dconv_w.reshape(1, SSM_CONV * SSM_CONV_DIM), dconv_b, pick(dbias_b), pick(dalog_b), dd_skip,
              dnw, dbg0, dbg1, dln1_g, dln1_b, dln2_g, dln2_b]
    widths = [p.shape[1] for p in pieces]
    total = sum(widths)
    P = -(-total // LANES) * LANES
    packed = jnp.concatenate(pieces + [jnp.zeros((1, P - total), F32)], axis=1)
    summed = _all_sum_small(packed)
    offs = [0]
    for wd in widths:
        offs.append(offs[-1] + wd)
    sm = [summed[:, offs[k]:offs[k + 1]] for k in range(len(pieces))]
    g_bf, g_cw_full, g_cb, g_dtb, g_al, g_ds, g_nw = sm[0], sm[1].reshape(SSM_CONV, SSM_CONV_DIM), sm[2], sm[3], sm[4], sm[5], sm[6]
    g_bg = jnp.concatenate([sm[7], sm[8]], axis=1)
    g_l1g, g_l1b, g_l2g, g_l2b = sm[9], sm[10], sm[11], sm[12]
    chip = 2 * lax.axis_index("x") + lax.axis_index("y")
    cshard = SSM_CONV_DIM // 4
    g_cw_shard = lax.dynamic_slice_in_dim(g_cw_full, chip * cshard, cshard, axis=1)
    small_names = ["b_forget", "conv_w", "conv_b", "dt_bias", "a_log", "d_skip", "ssm_norm_w", "b_gates", "ln1_g", "ln1_b",
                   "ln2_g", "ln2_b"]
    small_g = [g_bf, g_cw_shard.reshape(1, -1), g_cb, g_dtb, g_al, g_ds, g_nw, g_bg, g_l1g, g_l1b, g_l2g, g_l2b]
    small_w = [b_forget, conv_w[0].reshape(1, -1), conv_b, dt_bias, a_log, d_skip, ssm_norm_w, b_gates, ln1_g, ln1_b, ln2_g, ln2_b]
    small_m = [m_b_forget, m_conv_w[0].reshape(1, -1), m_conv_b, m_dt_bias, m_a_log, m_d_skip, m_ssm_norm_w, m_b_gates, m_ln1_g,
               m_ln1_b, m_ln2_g, m_ln2_b]
    small_v = [v_b_forget, v_conv_w[0].reshape(1, -1), v_conv_b, v_dt_bias, v_a_log, v_d_skip, v_ssm_norm_w, v_b_gates, v_ln1_g,
               v_ln1_b, v_ln2_g, v_ln2_b]
    sw = [a.shape[1] for a in small_w]
    stot = sum(sw)
    SP = -(-stot // LANES) * LANES

    def pack(parts):
        return jnp.concatenate(list(parts) + [jnp.zeros((1, SP - stot), F32)], axis=1).reshape(SP // LANES, LANES)

    sres = _adamw("adamw_small", pack(small_w), pack(small_m), pack(small_v), pack(small_g), None, SP // LANES)
    soffs = [0]
    for wd in sw:
        soffs.append(soffs[-1] + wd)
    smalls = {}
    for k, nm in enumerate(small_names):
        vals = [r.reshape(1, SP)[:, soffs[k]:soffs[k + 1]] for r in sres]
        if nm == "conv_w":
            vals = [v_.reshape(1, SSM_CONV, cshard) for v_ in vals]
        smalls[nm] = vals

    order = ["w_in", "b_forget", "conv_w", "conv_b", "dt_bias", "a_log", "d_skip", "ssm_norm_w", "w_proj_attn", "w_proj_ssm",
             "b_gates", "w_out", "ln1_g", "ln1_b", "w_ffn_gate", "w_ffn_up", "w_ffn_down", "ln2_g", "ln2_b"]
    allres = {**big, **smalls}
    outs = [loss, grad_x[None]]
    for idx in range(4):
        outs += [allres[nm][idx] for nm in order]
    return tuple(outs)
```

```python
import functools
import math

import jax
import jax.numpy as jnp
from jax import lax
from jax.experimental import pallas as pl
from jax.experimental.pallas import tpu as pltpu

F32, BF16 = jnp.float32, jnp.bfloat16
MESH = pl.DeviceIdType.MESH

D_MODEL = 1024
ATT_HEADS, ATT_HEAD_DIM = 16, 64
SSM_INNER, SSM_HEADS, SSM_GROUPS, SSM_STATE, SSM_CONV = 2048, 32, 4, 128, 4
SSM_CONV_DIM = SSM_INNER + 2 * SSM_GROUPS * SSM_STATE
GROUP_LANES = SSM_INNER // SSM_GROUPS
FFN_HIDDEN = 2816
ALPHA = 2.0 ** 0.25
LN_EPS = 1e-5
RMS_EPS = 1e-5
ADAM_LR, ADAM_B1, ADAM_B2, ADAM_EPS, ADAM_WD, ADAM_STEP = 0.001, 0.9, 0.999, 1e-08, 0.01, 10
IN_SIZES = (1024, 1024, 1024, 16, 2048, 3072, 32, 2048)
IN_WIDTH = sum(IN_SIZES)
RE_WIDTH = 3072 + 2048 + 3072 + 2048 + 128
RE_Z, RE_XBC, RE_GATE, RE_SMALL = 3072, 5120, 8192, 10240

LANES = 128
VMEM_CAP = 60 * 1024 * 1024
NEG = -1e30
TILES = dict(TM=512, TM2=256, TA=512, LC=256, TS=512, TB=256)


def _params(n_axes, vmem_bytes=None):
    return pltpu.CompilerParams(dimension_semantics=("arbitrary",) * n_axes,
                                vmem_limit_bytes=None if vmem_bytes is None else int(min(vmem_bytes, VMEM_CAP)))


def _sigmoid(v):
    return 1.0 / (1.0 + jnp.exp(-v))


def _softplus(v):
    return jnp.maximum(v, 0.0) + jnp.log(1.0 + jnp.exp(-jnp.abs(v)))


def _dot(a, b):
    return lax.dot_general(a, b, (((1,), (0,)), ((), ())), preferred_element_type=F32)


def _dot_nt(a, b):
    return lax.dot_general(a, b, (((1,), (1,)), ((), ())), preferred_element_type=F32)


def _dot_tn(a, b):
    return lax.dot_general(a, b, (((0,), (0,)), ((), ())), preferred_element_type=F32)


def _split3(v):
    hi = v.astype(BF16)
    r1 = v - hi.astype(F32)
    mid = r1.astype(BF16)
    lo = (r1 - mid.astype(F32)).astype(BF16)
    return hi, mid, lo


def _dot_exact_left(m01, v):
    hi, mid, lo = _split3(v)
    return _dot(m01, hi) + _dot(m01, mid) + _dot(m01, lo)


def _dot_exact_right(v, m01, terms=3):
    parts = _split3(v)[:terms]
    out = _dot(parts[0], m01)
    for p in parts[1:]:
        out = out + _dot(p, m01)
    return out


def _mm(name, M, N, tm, tn, lhs, rhs, pairs, e_fn, outs, *, nt=False, rows=(), vecs_n=(), sums=()):
    ni, nj = M // tm, N // tn
    assert ni * tm == M and nj * tn == N, (name, M, N, tm, tn)
    n_l, n_r, n_row, n_vn, n_o, n_s = len(lhs), len(rhs), len(rows), len(vecs_n), len(outs), len(sums)

    def body(*refs):
        pos = 0
        l_refs = refs[pos:pos + n_l]; pos += n_l
        r_refs = refs[pos:pos + n_r]; pos += n_r
        row_refs = refs[pos:pos + n_row]; pos += n_row
        vn_refs = refs[pos:pos + n_vn]; pos += n_vn
        o_refs = refs[pos:pos + n_o]; pos += n_o
        s_refs = refs[pos:pos + n_s]; pos += n_s
        i, j = pl.program_id(0), pl.program_id(1)
        accs = []
        for li, ri in pairs:
            accs.append(_dot_nt(l_refs[li][...], r_refs[ri][...]) if nt else _dot(l_refs[li][...], r_refs[ri][...]))
        out_vals, sum_vals = e_fn(accs, [r[...] for r in row_refs], [r[...] for r in vn_refs], j)
        for r, v in zip(o_refs, out_vals):
            r[...] = v.astype(r.dtype)
        if n_s:
            col = pl.multiple_of(j * tn, LANES)

            @pl.when(i == 0)
            def _():
                for r, v in zip(s_refs, sum_vals):
                    r[:, pl.ds(col, tn)] = v

            @pl.when(i > 0)
            def _():
                for r, v in zip(s_refs, sum_vals):
                    r[:, pl.ds(col, tn)] += v

    in_specs, args, est = [], [], 0
    for arr, width, cb in lhs:
        in_specs.append(pl.BlockSpec((tm, width), lambda i, j, cb=cb: (i, cb)))
        args.append(arr); est += tm * width * arr.dtype.itemsize
    for arr, off, *ksub in rhs:
        if nt:
            kb, kw = ksub if ksub else (0, arr.shape[1])
            in_specs.append(pl.BlockSpec((tn, kw), lambda i, j, off=off, kb=kb: (j + off, kb)))
            est += tn * kw * arr.dtype.itemsize
        else:
            in_specs.append(pl.BlockSpec((arr.shape[0], tn), lambda i, j, off=off: (0, j + off)))
            est += tn * arr.shape[0] * arr.dtype.itemsize
        args.append(arr)
    for arr, off in rows:
        in_specs.append(pl.BlockSpec((tm, tn), lambda i, j, off=off: (i, j + off)))
        args.append(arr); est += tm * tn * arr.dtype.itemsize
    for arr, off in vecs_n:
        in_specs.append(pl.BlockSpec((1, tn), lambda i, j, off=off: (0, j + off)))
        args.append(arr); est += 8 * tn * 4
    out_shape, out_specs = [], []
    for total, dtype, off in outs:
        out_shape.append(jax.ShapeDtypeStruct((M, total), dtype))
        out_specs.append(pl.BlockSpec((tm, tn), lambda i, j, off=off: (i, j + off)))
        est += tm * tn * jnp.dtype(dtype).itemsize
    for total in sums:
        out_shape.append(jax.ShapeDtypeStruct((1, total), F32))
        out_specs.append(pl.BlockSpec((1, total), lambda i, j: (0, 0)))
        est += 8 * total * 4
    vmem = 2 * est + (len(pairs) + 4) * tm * tn * 4 + (8 << 20)
    return pl.pallas_call(body, name=name, grid=(ni, nj), in_specs=in_specs, out_specs=out_specs, out_shape=out_shape,
                          compiler_params=_params(2, vmem))(*args)


def _mm_tn(name, a, g, ta, tn, ts, a_cols=None, a_off=0):
    S = a.shape[0]
    Ka = a.shape[1] if a_cols is None else a_cols
    N = g.shape[1]
    assert Ka % ta == 0 and N % tn == 0 and S % ts == 0, (name, Ka, N, S)
    aoff = a_off // ta

    def body(a_ref, g_ref, o_ref):
        s = pl.program_id(2)
        part = _dot_tn(a_ref[...], g_ref[...])

        @pl.when(s == 0)
        def _():
            o_ref[...] = part

        @pl.when(s > 0)
        def _():
            o_ref[...] += part

    vmem = 2 * (ts * ta * 2 + ts * tn * 2 + ta * tn * 4) + 2 * ta * tn * 4 + (8 << 20)
    return pl.pallas_call(
        body, name=name, grid=(Ka // ta, N // tn, S // ts),
        in_specs=[pl.BlockSpec((ts, ta), lambda ia, jn, s: (s, ia + aoff)), pl.BlockSpec((ts, tn), lambda ia, jn, s: (s, jn))],
        out_specs=pl.BlockSpec((ta, tn), lambda ia, jn, s: (ia, jn)),
        out_shape=jax.ShapeDtypeStruct((Ka, N), F32), compiler_params=_params(3, vmem))(a, g)


def _tri(n, upper):
    r = lax.broadcasted_iota(jnp.int32, (n, n), 0)
    c = lax.broadcasted_iota(jnp.int32, (n, n), 1)
    return jnp.where((c >= r) if upper else (c <= r), 1.0, 0.0).astype(BF16)


def _logsig(v):
    return jnp.minimum(v, 0.0) - jnp.log(1.0 + jnp.exp(-jnp.abs(v)))


def _cum_fwd(small, bvec, tb):
    S = small.shape[0]

    def body(x_ref, b_ref, o_ref, carry):
        i = pl.program_id(0)

        @pl.when(i == 0)
        def _():
            carry[...] = jnp.zeros_like(carry)

        logf = _logsig(x_ref[...] + b_ref[...])
        cum = _dot_exact_left(_tri(tb, False), logf) + carry[0:1, :]
        o_ref[...] = cum
        carry[0:1, :] = cum[tb - 1:tb, :]

    return pl.pallas_call(
        body, name="cum_fwd", grid=(S // tb,),
        in_specs=[pl.BlockSpec((tb, LANES), lambda i: (i, 0)), pl.BlockSpec((1, LANES), lambda i: (0, 0))],
        out_specs=pl.BlockSpec((tb, LANES), lambda i: (i, 0)), out_shape=jax.ShapeDtypeStruct((S, LANES), F32),
        scratch_shapes=[pltpu.VMEM((8, LANES), F32)], compiler_params=_params(1))(small, bvec)


def _cum_bwd(dcum_k, dcum_q, small, bvec, tb):
    S = small.shape[0]
    nb = S // tb

    def body(dk_ref, dq_ref, x_ref, b_ref, o_ref, s_ref, carry):
        i = pl.program_id(0)

        @pl.when(i == 0)
        def _():
            carry[...] = jnp.zeros_like(carry)
            s_ref[...] = jnp.zeros_like(s_ref)

        rc = _dot_exact_left(_tri(tb, True), dk_ref[...] + dq_ref[...]) + carry[0:1, :]
        dfl = rc * _sigmoid(-(x_ref[...] + b_ref[...]))
        o_ref[...] = dfl
        s_ref[...] += jnp.sum(dfl, axis=0, keepdims=True)
        carry[0:1, :] = rc[0:1, :]

    rev = lambda i: (nb - 1 - i, 0)
    return pl.pallas_call(
        body, name="cum_bwd", grid=(nb,),
        in_specs=[pl.BlockSpec((tb, LANES), rev)] * 3 + [pl.BlockSpec((1, LANES), lambda i: (0, 0))],
        out_specs=[pl.BlockSpec((tb, LANES), rev), pl.BlockSpec((1, LANES), lambda i: (0, 0))],
        out_shape=[jax.ShapeDtypeStruct((S, LANES), F32), jax.ShapeDtypeStruct((1, LANES), F32)],
        scratch_shapes=[pltpu.VMEM((8, LANES), F32)], compiler_params=_params(1))(dcum_k, dcum_q, small, bvec)


def _lane_mask():
    return lax.broadcasted_iota(jnp.int32, (1, LANES), 1) < ATT_HEAD_DIM


def _attn_fwd(qkv, cq, ck_rows, T):
    S = qkv.shape[0]
    nq = S // T
    HP = ATT_HEADS // 2

    def body(q_ref, k_ref, v_ref, cq_ref, ck_ref, o_ref, o32_ref, lse_ref):
        i = pl.program_id(1)
        mA = _lane_mask()
        lane = lax.broadcasted_iota(jnp.int32, (1, LANES), 1)
        q2 = q_ref[...]
        zero = jnp.zeros_like(q2)
        qs = (jnp.where(mA, q2, zero), jnp.where(mA, zero, q2))
        cqs = (cq_ref[0, :, 0:1], cq_ref[0, :, 1:2])
        row = lax.broadcasted_iota(jnp.int32, (T, T), 0)
        col = lax.broadcasted_iota(jnp.int32, (T, T), 1)

        def block(j, carry, diag):
            off = pl.multiple_of(j * T, T)
            kj = k_ref[pl.ds(off, T), :]
            vj = v_ref[pl.ds(off, T), :]
            m0, l0, m1, l1, acc = carry
            new, alphas, pvs = [], [], []
            for h, (m, l) in enumerate(((m0, l0), (m1, l1))):
                ck = ck_ref[0, h:h + 1, pl.ds(off, T)]
                s = _dot_nt(qs[h], kj) + (cqs[h] - ck)
                if diag:
                    s = jnp.where(row >= col, s, NEG)
                m_new = jnp.maximum(m, jnp.max(s, axis=1, keepdims=True))
                p = jnp.exp(s - m_new)
                alpha = jnp.exp(m - m_new)
                l_new = alpha * l + jnp.sum(p, axis=1, keepdims=True)
                pvs.append(_dot(p.astype(BF16), vj))
                alphas.append(alpha)
                new += [m_new, l_new]
            acc = acc * jnp.where(mA, alphas[0], alphas[1]) + jnp.where(mA, pvs[0], pvs[1])
            return (new[0], new[1], new[2], new[3], acc)

        init = (jnp.full((T, 1), NEG, F32), jnp.zeros((T, 1), F32), jnp.full((T, 1), NEG, F32), jnp.zeros((T, 1), F32),
                jnp.zeros((T, LANES), F32))
        carry = lax.fori_loop(0, i, lambda j, c: block(j, c, False), init)
        m0, l0, m1, l1, acc = block(i, carry, True)
        out = acc / jnp.where(mA, l0, l1)
        o_ref[...] = out.astype(BF16)
        o32_ref[...] = out
        lse0, lse1 = m0 + jnp.log(l0), m1 + jnp.log(l1)
        lse_ref[...] = jnp.where(lane == 0, lse0, jnp.where(lane == 1, lse1, 0.0))

    vmem = 2 * (T * LANES * 2 + 2 * S * LANES * 2 + T * LANES * 4 + 8 * S * 4 + T * LANES * 2 + T * LANES * 4) + 10 * T * T * 4 + (8 << 20)
    return pl.pallas_call(
        body, name="attn_fwd", grid=(HP, nq),
        in_specs=[pl.BlockSpec((T, LANES), lambda hp, i: (i, hp)),
                  pl.BlockSpec((S, LANES), lambda hp, i: (0, HP + hp)),
                  pl.BlockSpec((S, LANES), lambda hp, i: (0, 2 * HP + hp)),
                  pl.BlockSpec((1, T, 2), lambda hp, i: (hp, i, 0)),
                  pl.BlockSpec((1, 2, S), lambda hp, i: (hp, 0, 0))],
        out_specs=[pl.BlockSpec((T, LANES), lambda hp, i: (i, hp))] * 3,
        out_shape=[jax.ShapeDtypeStruct((S, D_MODEL), BF16), jax.ShapeDtypeStruct((S, D_MODEL), F32),
                   jax.ShapeDtypeStruct((S, D_MODEL), F32)],
        compiler_params=_params(2, vmem))(qkv, qkv, qkv, cq, ck_rows)


def _attn_bwd(qkv, do, stats_rows, ck_cols, T):
    S = qkv.shape[0]
    nq = S // T
    HP = ATT_HEADS // 2

    def body(k_ref, v_ref, ck_ref, q_ref, do_ref, st_ref, dq_ref, dk_ref, dv_ref, dck_ref, dcq_ref, dq_acc):
        j = pl.program_id(1)
        mA = _lane_mask()
        lane = lax.broadcasted_iota(jnp.int32, (1, LANES), 1)
        masks = (mA, jnp.logical_not(mA))
        one_lane = (lane == ATT_HEAD_DIM, lane == 0)
        spare = (ATT_HEAD_DIM, 0)

        @pl.when(j == 0)
        def _():
            dq_acc[...] = jnp.zeros_like(dq_acc)

        kj = k_ref[...]
        vj = v_ref[...]
        zero = jnp.zeros_like(kj)
        ks = tuple(jnp.where(one_lane[h], jnp.ones_like(kj), jnp.where(masks[h], kj, zero)) for h in (0, 1))
        cks = (ck_ref[0, :, 0:1], ck_ref[0, :, 1:2])
        row = lax.broadcasted_iota(jnp.int32, (T, T), 0)
        col = lax.broadcasted_iota(jnp.int32, (T, T), 1)

        def block(i, carry, diag):
            dv_acc, dk0, dk1 = carry
            off = pl.multiple_of(i * T, T)
            qi = q_ref[pl.ds(off, T), :]
            doi = do_ref[pl.ds(off, T), :]
            dks = [dk0, dk1]
            for h in (0, 1):
                qh = jnp.where(masks[h], qi, zero)
                doh = jnp.where(masks[h], doi, zero)
                cq = st_ref[0, h:h + 1, pl.ds(off, T)]
                lse = st_ref[0, 2 + h:3 + h, pl.ds(off, T)]
                dd = st_ref[0, 4 + h:5 + h, pl.ds(off, T)]
                st = _dot_nt(kj, qh) + (cq - cks[h])
                if diag:
                    st = jnp.where(row <= col, st, NEG)
                pt = jnp.exp(st - lse)
                dpt = _dot_nt(vj, doh)
                dst = (pt * (dpt - dd)).astype(BF16)
                dv_acc = dv_acc + _dot(pt.astype(BF16), doh)
                q_aug = jnp.where(one_lane[h], jnp.ones_like(qh), qh)
                dks[h] = dks[h] + _dot(dst, q_aug)
                dq_acc[h, pl.ds(off, T), :] += _dot_tn(dst, ks[h])
            return (dv_acc, dks[0], dks[1])

        z = jnp.zeros((T, LANES), F32)
        carry = block(j, (z, z, z), True)
        dv_acc, dk0, dk1 = lax.fori_loop(j + 1, nq, lambda i, c: block(i, c, False), carry)
        dv_ref[...] = dv_acc.astype(BF16)
        dk_ref[...] = jnp.where(mA, dk0, dk1).astype(BF16)
        dck_ref[...] = jnp.where(lane == 0, -dk0[:, spare[0]:spare[0] + 1], jnp.where(lane == 1, -dk1[:, spare[1]:spare[1] + 1], 0.0))

        @pl.when(j == nq - 1)
        def _():
            dq0, dq1 = dq_acc[0], dq_acc[1]
            dq_ref[...] = (jnp.where(mA, dq0, dq1) * (1.0 / math.sqrt(ATT_HEAD_DIM))).astype(BF16)
            dcq_ref[...] = jnp.where(lane == 0, dq0[:, spare[0]:spare[0] + 1], jnp.where(lane == 1, dq1[:, spare[1]:spare[1] + 1], 0.0))

    vmem = (2 * (2 * T * LANES * 2 + T * LANES * 4 + 2 * S * LANES * 2 + 8 * S * 4 + S * LANES * 2 + S * LANES * 4
                 + 2 * T * LANES * 2 + T * LANES * 4) + 2 * S * LANES * 4 + 12 * T * T * 4 + (8 << 20))
    return pl.pallas_call(
        body, name="attn_bwd", grid=(HP, nq),
        in_specs=[pl.BlockSpec((T, LANES), lambda hp, j: (j, HP + hp)),
                  pl.BlockSpec((T, LANES), lambda hp, j: (j, 2 * HP + hp)),
                  pl.BlockSpec((1, T, 2), lambda hp, j: (hp, j, 0)),
                  pl.BlockSpec((S, LANES), lambda hp, j: (0, hp)),
                  pl.BlockSpec((S, LANES), lambda hp, j: (0, hp)),
                  pl.BlockSpec((1, 8, S), lambda hp, j: (hp, 0, 0))],
        out_specs=[pl.BlockSpec((S, LANES), lambda hp, j: (0, hp)),
                   pl.BlockSpec((T, LANES), lambda hp, j: (j, hp)),
                   pl.BlockSpec((T, LANES), lambda hp, j: (j, hp)),
                   pl.BlockSpec((T, LANES), lambda hp, j: (j, hp)),
                   pl.BlockSpec((S, LANES), lambda hp, j: (0, hp))],
        out_shape=[jax.ShapeDtypeStruct((S, D_MODEL), BF16)] * 3 + [jax.ShapeDtypeStruct((S, D_MODEL), F32)] * 2,
        scratch_shapes=[pltpu.VMEM((2, S, LANES), F32)], compiler_params=_params(2, vmem))(qkv, qkv, ck_cols, qkv, do, stats_rows)


HALO = 8


def _conv_fwd(u, w, b, ts, tc):
    S, C = u.shape
    hb = ts // HALO

    def body(u_ref, prev_ref, w_ref, b_ref, o_ref, ext):
        i = pl.program_id(0)
        ext[0:HALO, :] = jnp.where(i == 0, 0.0, prev_ref[...])
        ext[HALO:HALO + ts, :] = u_ref[...]
        acc = b_ref[...] + w_ref[3:4, :] * u_ref[...]
        for k in range(SSM_CONV - 1):
            d = SSM_CONV - 1 - k
            acc = acc + w_ref[k:k + 1, :] * ext[HALO - d:HALO - d + ts, :]
        o_ref[...] = acc * _sigmoid(acc)

    return pl.pallas_call(
        body, name="conv_fwd", grid=(S // ts, C // tc),
        in_specs=[pl.BlockSpec((ts, tc), lambda i, j: (i, j)),
                  pl.BlockSpec((HALO, tc), lambda i, j: (jnp.maximum(i * hb - 1, 0), j)),
                  pl.BlockSpec((SSM_CONV, tc), lambda i, j: (0, j)), pl.BlockSpec((1, tc), lambda i, j: (0, j))],
        out_specs=pl.BlockSpec((ts, tc), lambda i, j: (i, j)), out_shape=jax.ShapeDtypeStruct((S, C), F32),
        scratch_shapes=[pltpu.VMEM((ts + HALO, tc), F32)], compiler_params=_params(2))(u, u, w, b)


def _conv_bwd(u, dy, w, b, ts, tc):
    S, C = u.shape
    hb = ts // HALO
    nb = S // ts
    E = ts + 2 * HALO

    def body(u_ref, uprev_ref, unext_ref, dy_ref, dynext_ref, w_ref, b_ref, du_ref, dw_ref, db_ref, uext, gext):
        i = pl.program_id(1)
        last = i == nb - 1
        uext[0:HALO, :] = jnp.where(i == 0, 0.0, uprev_ref[...])
        uext[HALO:HALO + ts, :] = u_ref[...]
        uext[HALO + ts:E, :] = unext_ref[...]
        n = ts + HALO
        pre = b_ref[...] + w_ref[3:4, :] * uext[HALO:HALO + n, :]
        for k in range(SSM_CONV - 1):
            d = SSM_CONV - 1 - k
            pre = pre + w_ref[k:k + 1, :] * uext[HALO - d:HALO - d + n, :]
        sg = _sigmoid(pre)
        dsilu = sg * (1.0 + pre * (1.0 - sg))
        gext[0:ts, :] = dy_ref[...] * dsilu[0:ts, :]
        gext[ts:n, :] = jnp.where(last, 0.0, dynext_ref[...] * dsilu[ts:n, :])
        g = gext[0:ts, :]
        du = w_ref[3:4, :] * g
        for k in range(SSM_CONV - 1):
            d = SSM_CONV - 1 - k
            du = du + w_ref[k:k + 1, :] * gext[d:d + ts, :]
        du_ref[...] = du.astype(du_ref.dtype)
        dws = [jnp.sum(g * uext[HALO - (SSM_CONV - 1 - k):HALO - (SSM_CONV - 1 - k) + ts, :], axis=0, keepdims=True)
               for k in range(SSM_CONV)]
        dbs = jnp.sum(g, axis=0, keepdims=True)

        @pl.when(i == 0)
        def _():
            for k in range(SSM_CONV):
                dw_ref[k:k + 1, :] = dws[k]
            db_ref[...] = dbs

        @pl.when(i > 0)
        def _():
            for k in range(SSM_CONV):
                dw_ref[k:k + 1, :] += dws[k]
            db_ref[...] += dbs

    nxt = lambda j, i: (jnp.minimum((i + 1) * hb, S // HALO - 1), j)
    return pl.pallas_call(
        body, name="conv_bwd", grid=(C // tc, nb),
        in_specs=[pl.BlockSpec((ts, tc), lambda j, i: (i, j)),
                  pl.BlockSpec((HALO, tc), lambda j, i: (jnp.maximum(i * hb - 1, 0), j)),
                  pl.BlockSpec((HALO, tc), nxt),
                  pl.BlockSpec((ts, tc), lambda j, i: (i, j)),
                  pl.BlockSpec((HALO, tc), nxt),
                  pl.BlockSpec((SSM_CONV, tc), lambda j, i: (0, j)), pl.BlockSpec((1, tc), lambda j, i: (0, j))],
        out_specs=[pl.BlockSpec((ts, tc), lambda j, i: (i, j)), pl.BlockSpec((SSM_CONV, tc), lambda j, i: (0, j)),
                   pl.BlockSpec((1, tc), lambda j, i: (0, j))],
        out_shape=[jax.ShapeDtypeStruct((S, C), BF16), jax.ShapeDtypeStruct((SSM_CONV, C), F32), jax.ShapeDtypeStruct((1, C), F32)],
        scratch_shapes=[pltpu.VMEM((E, tc), F32), pltpu.VMEM((ts + HALO, tc), F32)],
        compiler_params=_params(2))(u, u, u, dy, dy, w, b)


def _head_sum_matrix():
    r = jnp.right_shift(lax.broadcasted_iota(jnp.int32, (GROUP_LANES, GROUP_LANES), 0), 6)
    c = jnp.right_shift(lax.broadcasted_iota(jnp.int32, (GROUP_LANES, GROUP_LANES), 1), 6)
    return jnp.where(r == c, 1.0, 0.0).astype(BF16)


def _ssd_common(dtb_ref, dtr_ref, bias_b, alog_b, bias_c, alog_c, L):
    a_b = -jnp.exp(alog_b)
    dt = _softplus(dtb_ref[...] + bias_b)
    acum = _dot_exact_left(_tri(L, False), dt * a_b)
    a_c = -jnp.exp(alog_c)
    dtr = _softplus(dtr_ref[0] + bias_c)
    acum_r = _dot_exact_right(dtr * a_c, _tri(L, True))
    return a_b, dt, acum, acum_r


def _ssd_specs(L, nc, rev):
    cc = (lambda c: nc - 1 - c) if rev else (lambda c: c)
    G = SSM_GROUPS
    blk = pl.BlockSpec((L, GROUP_LANES), lambda g, c: (cc(c), g))
    xs = blk
    bm = pl.BlockSpec((L, SSM_STATE), lambda g, c: (cc(c), SSM_INNER // SSM_STATE + g))
    cm = pl.BlockSpec((L, SSM_STATE), lambda g, c: (cc(c), SSM_INNER // SSM_STATE + G + g))
    dtr = pl.BlockSpec((1, 8, L), lambda g, c: (g, 0, cc(c)))
    vec = pl.BlockSpec((1, GROUP_LANES), lambda g, c: (0, g))
    colv = pl.BlockSpec((1, 8, 1), lambda g, c: (g, 0, 0))
    hs = pl.BlockSpec((1, 1, SSM_STATE, GROUP_LANES), lambda g, c: (g, cc(c), 0, 0))
    return blk, xs, bm, cm, dtr, vec, colv, hs


def _ssd_fwd(xbc, z, dtb, dtr, bias_b, alog_b, dskip_b, normw, bias_c, alog_c, L):
    S = z.shape[0]
    nc = S // L
    blk, xs, bm, cm, dtrs, vec, colv, hs = _ssd_specs(L, nc, False)

    def body(x_ref, b_ref, c_ref, z_ref, dtb_ref, dtr_ref, bias_ref, alog_ref, dskip_ref, nw_ref, biasc_ref, alogc_ref,
             y_ref, ssm_ref, hs_ref, h_scr):
        c = pl.program_id(1)

        @pl.when(c == 0)
        def _():
            h_scr[...] = jnp.zeros_like(h_scr)

        mA = _lane_mask()
        a_b, dt, acum, acum_r = _ssd_common(dtb_ref, dtr_ref, bias_ref[...], alog_ref[...], biasc_ref[0], alogc_ref[0], L)
        x = x_ref[...]
        cb, bb = c_ref[...].astype(BF16), b_ref[...].astype(BF16)
        hprev = h_scr[...]
        hs_ref[0, 0] = hprev
        xdt = x * dt
        xdt_b = xdt.astype(BF16)
        gmat = _dot_nt(cb, bb)
        row = lax.broadcasted_iota(jnp.int32, (L, L), 0)
        col = lax.broadcasted_iota(jnp.int32, (L, L), 1)
        parts = []
        for p in range(GROUP_LANES // LANES):
            xp = xdt_b[:, p * LANES:(p + 1) * LANES]
            yd = []
            for hh in (0, 1):
                r = 2 * p + hh
                acol = acum[:, r * ATT_HEAD_DIM:r * ATT_HEAD_DIM + 1]
                arow = acum_r[r:r + 1, :]
                lm = jnp.exp(jnp.where(row >= col, acol - arow, NEG))
                yd.append(_dot((gmat * lm).astype(BF16), xp))
            parts.append(jnp.where(mA, yd[0], yd[1]))
        ydiag = jnp.concatenate(parts, axis=1)
        yoff = jnp.exp(acum) * _dot(cb, hprev.astype(BF16))
        y = ydiag + yoff + dskip_ref[...] * x
        aend = acum[L - 1:L, :]
        wgt = (jnp.exp(aend - acum) * xdt).astype(BF16)
        h_scr[...] = jnp.exp(aend) * hprev + _dot_tn(bb, wgt)
        y_ref[...] = y
        zz = z_ref[...]
        u = y * (zz * _sigmoid(zz))
        rs = lax.rsqrt(jnp.mean(u * u, axis=1, keepdims=True) + RMS_EPS)
        ssm_ref[...] = (u * rs * nw_ref[...]).astype(BF16)

    return pl.pallas_call(
        body, name="ssd_fwd", grid=(SSM_GROUPS, nc),
        in_specs=[xs, bm, cm, blk, blk, dtrs, vec, vec, vec, vec, colv, colv],
        out_specs=[blk, blk, hs],
        out_shape=[jax.ShapeDtypeStruct((S, SSM_INNER), F32), jax.ShapeDtypeStruct((S, SSM_INNER), BF16),
                   jax.ShapeDtypeStruct((SSM_GROUPS, nc, SSM_STATE, GROUP_LANES), F32)],
        scratch_shapes=[pltpu.VMEM((SSM_STATE, GROUP_LANES), F32)],
        compiler_params=_params(2, 48 << 20))(xbc, xbc, xbc, z, dtb, dtr, bias_b, alog_b, dskip_b, normw, bias_c, alog_c)


def _ssd_bwd(xbc, z, y, dssm, hs_all, dtb, dtr, bias_b, alog_b, dskip_b, normw, bias_c, alog_c, L):
    S = z.shape[0]
    nc = S // L
    blk, xs, bm, cm, dtrs, vec, colv, hs = _ssd_specs(L, nc, True)

    def body(x_ref, b_ref, c_ref, z_ref, y_ref, dssm_ref, hs_ref, dtb_ref, dtr_ref, bias_ref, alog_ref, dskip_ref, nw_ref,
             biasc_ref, alogc_ref,
             dx_ref, db_ref, dc_ref, dz_ref, ddt_ref, dnw_ref, ddskip_ref, dbias_ref, dalog_ref, dh_scr):
        c = pl.program_id(1)

        @pl.when(c == 0)
        def _():
            dh_scr[...] = jnp.zeros_like(dh_scr)

        mA = _lane_mask()
        masks = (mA, jnp.logical_not(mA))
        a_b, dt, acum, acum_r = _ssd_common(dtb_ref, dtr_ref, bias_ref[...], alog_ref[...], biasc_ref[0], alogc_ref[0], L)
        x, zz, y, dssm = x_ref[...], z_ref[...], y_ref[...], dssm_ref[...]
        cb, bb = c_ref[...].astype(BF16), b_ref[...].astype(BF16)
        hprev = hs_ref[0, 0]
        hb = hprev.astype(BF16)
        ds = dh_scr[...]
        dsb = ds.astype(BF16)
        dskip = dskip_ref[...]
        aend = acum[L - 1:L, :]
        e_a, e_end = jnp.exp(acum), jnp.exp(aend)
        dte = jnp.exp(aend - acum)
        xdt = x * dt
        xdt_b = xdt.astype(BF16)
        sg = _sigmoid(zz)
        sz = zz * sg
        u = y * sz
        rs = lax.rsqrt(jnp.mean(u * u, axis=1, keepdims=True) + RMS_EPS)
        un = u * rs
        dun = dssm * nw_ref[...]
        du = rs * (dun - un * jnp.mean(dun * un, axis=1, keepdims=True))
        dy = du * sz
        dz_ref[...] = (du * y * sg * (1.0 + zz * (1.0 - sg))).astype(dz_ref.dtype)
        dy_b = dy.astype(BF16)
        dch_b = (dy * e_a).astype(BF16)
        dc = _dot_nt(dch_b, hb)
        dhprev = _dot_tn(cb, dch_b)
        gt = _dot_nt(bb, cb)
        row = lax.broadcasted_iota(jnp.int32, (L, L), 0)
        col = lax.broadcasted_iota(jnp.int32, (L, L), 1)
        dgt = jnp.zeros((L, L), F32)
        parts = []
        for p in range(GROUP_LANES // LANES):
            xp = xdt_b[:, p * LANES:(p + 1) * LANES]
            dyp = dy_b[:, p * LANES:(p + 1) * LANES]
            zero = jnp.zeros_like(dyp)
            acc = None
            for hh in (0, 1):
                r = 2 * p + hh
                acol = acum[:, r * ATT_HEAD_DIM:r * ATT_HEAD_DIM + 1]
                arow = acum_r[r:r + 1, :]
                lmt = jnp.exp(jnp.where(row <= col, arow - acol, NEG))
                dyh = jnp.where(masks[hh], dyp, zero)
                part = _dot((gt * lmt).astype(BF16), dyh)
                acc = part if acc is None else acc + part
                dgt = dgt + _dot_nt(xp, dyh) * lmt
            parts.append(acc)
        dxdt_diag = jnp.concatenate(parts, axis=1)
        dgt_b = dgt.astype(BF16)
        db = _dot(dgt_b, cb)
        dc = dc + _dot_tn(dgt_b, bb)
        dxdt_state = dte * _dot(bb, dsb)
        db = db + _dot_nt((dte * xdt).astype(BF16), dsb)
        dxdt = dxdt_diag + dxdt_state
        dy_r, xdt_r = dy_b.astype(F32), xdt_b.astype(F32)
        dac = dy_r * (y - dskip * x) - xdt_r * dxdt
        tail = jnp.sum(xdt_r * dxdt_state, axis=0, keepdims=True) + e_end * jnp.sum(ds * hprev, axis=0, keepdims=True)
        rowl = lax.broadcasted_iota(jnp.int32, (L, 1), 0)
        dac = dac + jnp.where(rowl == L - 1, tail, 0.0)
        rc = _dot_exact_left(_tri(L, True), dac)
        hsum = _head_sum_matrix()
        hs1 = _dot_exact_right(dxdt * x, hsum, 2)
        hs2 = _dot_exact_right(rc, hsum, 2)
        ddt = hs1 + a_b * hs2
        ddtraw = ddt * _sigmoid(dtb_ref[...] + bias_ref[...])
        dx_ref[...] = dskip * dy + dxdt * dt
        db_ref[...] = db
        dc_ref[...] = dc
        ddt_ref[...] = ddtraw
        dh_scr[...] = e_end * ds + dhprev
        sums = (jnp.sum(dssm * un, axis=0, keepdims=True), jnp.sum(dy * x, axis=0, keepdims=True),
                jnp.sum(ddtraw, axis=0, keepdims=True), a_b * jnp.sum(hs2 * dt, axis=0, keepdims=True))
        refs = (dnw_ref, ddskip_ref, dbias_ref, dalog_ref)

        @pl.when(c == 0)
        def _():
            for r, v in zip(refs, sums):
                r[...] = v

        @pl.when(c > 0)
        def _():
            for r, v in zip(refs, sums):
                r[...] += v

    nbc = pl.BlockSpec((L, SSM_STATE), lambda g, c: (nc - 1 - c, g))
    return pl.pallas_call(
        body, name="ssd_bwd", grid=(SSM_GROUPS, nc),
        in_specs=[xs, bm, cm, blk, blk, blk, hs, blk, dtrs, vec, vec, vec, vec, colv, colv],
        out_specs=[blk, nbc, nbc, blk, blk, vec, vec, vec, vec],
        out_shape=[jax.ShapeDtypeStruct((S, SSM_INNER), F32), jax.ShapeDtypeStruct((S, SSM_GROUPS * SSM_STATE), F32),
                   jax.ShapeDtypeStruct((S, SSM_GROUPS * SSM_STATE), F32), jax.ShapeDtypeStruct((S, SSM_INNER), BF16),
                   jax.ShapeDtypeStruct((S, SSM_INNER), F32)] + [jax.ShapeDtypeStruct((1, SSM_INNER), F32)] * 4,
        scratch_shapes=[pltpu.VMEM((SSM_STATE, GROUP_LANES), F32)],
        compiler_params=_params(2, 56 << 20))(xbc, xbc, xbc, z, y, dssm, hs_all, dtb, dtr, bias_b, alog_b, dskip_b, normw,
                                              bias_c, alog_c)


def _place():
    return lax.axis_index("x"), lax.axis_index("y"), lax.axis_index("c")


def _other_chips(x, y):
    return [(1 - x, y), (x, 1 - y), (1 - x, 1 - y)]


def _half_rows(rows, which):
    hr = rows // 2
    if isinstance(which, int):
        return pl.ds(which * hr, hr)
    return pl.ds(pl.multiple_of(which * hr, 8), hr)


def _chip_gather(name, shards, split):
    n = len(shards)
    ANY = pl.BlockSpec(memory_space=pl.ANY)

    def body(*refs):
        ins, outs = refs[:n], refs[n:2 * n]
        send, recv, fsend, frecv, loc = refs[2 * n:]
        x, y, c = _place()
        me = 2 * x + y
        sibling = (x, y, 1 - c)
        chips = _other_chips(x, y)

        def piece(a, chip_idx, which):
            if split[a]:
                return outs[a].at[chip_idx, _half_rows(shards[a].shape[0], which)]
            return outs[a].at[chip_idx]

        def ici(k, a, to_chip, src_chip):
            src = ins[a].at[_half_rows(shards[a].shape[0], c)] if split[a] else ins[a]
            return pltpu.make_async_remote_copy(src_ref=src, dst_ref=piece(a, src_chip, c), send_sem=send.at[k, a],
                                                recv_sem=recv.at[k, a], device_id=(*to_chip, c), device_id_type=MESH)

        def fwd(k, a, src_chip, which):
            return pltpu.make_async_remote_copy(src_ref=piece(a, src_chip, which), dst_ref=piece(a, src_chip, which),
                                                send_sem=fsend.at[k, a], recv_sem=frecv.at[k, a], device_id=sibling,
                                                device_id_type=MESH)

        local = [pltpu.make_async_copy(ins[a], outs[a].at[me], loc.at[a]) for a in range(n)]
        for cp in local:
            cp.start()
        sends = []
        for k, chip in enumerate(chips):
            for a in range(n):
                cp = ici(k, a, chip, me)
                cp.start()
                sends.append(cp)
        for k, (ox, oy) in enumerate(chips):
            src = 2 * ox + oy
            for a in range(n):
                ici(k, a, (ox, oy), src).wait_recv()
                if split[a]:
                    cp = fwd(k, a, src, c)
                    cp.start()
                    sends.append(cp)
        for k, (ox, oy) in enumerate(chips):
            for a in range(n):
                if split[a]:
                    fwd(k, a, 2 * ox + oy, 1 - c).wait_recv()
        for cp in sends:
            cp.wait_send()
        for cp in local:
            cp.wait()

    sem = pltpu.SemaphoreType.DMA((3, n))
    return pl.pallas_call(
        body, name=name, in_specs=[ANY] * n, out_specs=[ANY] * n,
        out_shape=[jax.ShapeDtypeStruct((4,) + s.shape, s.dtype) for s in shards],
        scratch_shapes=[sem, sem, sem, sem, pltpu.SemaphoreType.DMA((n,))])(*shards)


def _chip_scatter(name, blocks):
    n = len(blocks)
    ANY = pl.BlockSpec(memory_space=pl.ANY)

    def body(*refs):
        ins, outs = refs[:n], refs[n:2 * n]
        send, recv, loc = refs[2 * n:]
        x, y, c = _place()
        me = 2 * x + y
        local = [pltpu.make_async_copy(ins[a].at[me], outs[a].at[me], loc.at[a]) for a in range(n)]
        for cp in local:
            cp.start()
        sends = []
        for k, (ox, oy) in enumerate(_other_chips(x, y)):
            dst_chip = 2 * ox + oy
            for a in range(n):
                cp = pltpu.make_async_remote_copy(src_ref=ins[a].at[dst_chip], dst_ref=outs[a].at[me], send_sem=send.at[k, a],
                                                  recv_sem=recv.at[k, a], device_id=(ox, oy, c), device_id_type=MESH)
                cp.start()
                sends.append(cp)
        for k, (ox, oy) in enumerate(_other_chips(x, y)):
            src = 2 * ox + oy
            for a in range(n):
                pltpu.make_async_remote_copy(src_ref=ins[a].at[me], dst_ref=outs[a].at[src], send_sem=send.at[k, a],
                                             recv_sem=recv.at[k, a], device_id=(ox, oy, c), device_id_type=MESH).wait_recv()
        for cp in sends:
            cp.wait_send()
        for cp in local:
            cp.wait()

    return pl.pallas_call(
        body, name=name, in_specs=[ANY] * n, out_specs=[ANY] * n,
        out_shape=[jax.ShapeDtypeStruct(b.shape, b.dtype) for b in blocks],
        scratch_shapes=[pltpu.SemaphoreType.DMA((3, n)), pltpu.SemaphoreType.DMA((3, n)), pltpu.SemaphoreType.DMA((n,))],
    )(*blocks)


def _half_to_sibling(name, blocks):
    n = len(blocks)
    ANY = pl.BlockSpec(memory_space=pl.ANY)

    def body(*refs):
        ins, outs = refs[:n], refs[n:2 * n]
        send, recv = refs[2 * n:]
        x, y, c = _place()
        cps = [pltpu.make_async_remote_copy(src_ref=ins[a].at[:, _half_rows(blocks[a].shape[1], 1 - c)], dst_ref=outs[a],
                                            send_sem=send.at[a], recv_sem=recv.at[a], device_id=(x, y, 1 - c),
                                            device_id_type=MESH) for a in range(n)]
        for cp in cps:
            cp.start()
        for cp in cps:
            cp.wait_recv()
        for cp in cps:
            cp.wait_send()

    return pl.pallas_call(
        body, name=name, in_specs=[ANY] * n, out_specs=[ANY] * n,
        out_shape=[jax.ShapeDtypeStruct((4, b.shape[1] // 2, b.shape[2]), b.dtype) for b in blocks],
        scratch_shapes=[pltpu.SemaphoreType.DMA((n,)), pltpu.SemaphoreType.DMA((n,))])(*blocks)


def _halves_merge(name, halves):
    n = len(halves)
    ANY = pl.BlockSpec(memory_space=pl.ANY)

    def body(*refs):
        ins, outs = refs[:n], refs[n:2 * n]
        send, recv, loc = refs[2 * n:]
        x, y, c = _place()
        local, remote = [], []
        for a in range(n):
            rows = 2 * halves[a].shape[0]
            mine = outs[a].at[_half_rows(rows, c)]
            local.append(pltpu.make_async_copy(ins[a], mine, loc.at[a]))
            remote.append(pltpu.make_async_remote_copy(src_ref=ins[a], dst_ref=mine, send_sem=send.at[a], recv_sem=recv.at[a],
                                                       device_id=(x, y, 1 - c), device_id_type=MESH))
        for cp in local + remote:
            cp.start()
        for a in range(n):
            theirs = outs[a].at[_half_rows(2 * halves[a].shape[0], 1 - c)]
            pltpu.make_async_remote_copy(src_ref=ins[a], dst_ref=theirs, send_sem=send.at[a], recv_sem=recv.at[a],
                                         device_id=(x, y, 1 - c), device_id_type=MESH).wait_recv()
        for cp in remote:
            cp.wait_send()
        for cp in local:
            cp.wait()

    return pl.pallas_call(
        body, name=name, in_specs=[ANY] * n, out_specs=[ANY] * n,
        out_shape=[jax.ShapeDtypeStruct((2 * h.shape[0], h.shape[1]), h.dtype) for h in halves],
        scratch_shapes=[pltpu.SemaphoreType.DMA((n,)), pltpu.SemaphoreType.DMA((n,)), pltpu.SemaphoreType.DMA((n,))])(*halves)


N_DEV = 8


def _all_sum_small(vec):
    P = vec.shape[1]

    def body(v_ref, o_ref, buf, send, recv):
        x, y, c = _place()
        me = 4 * x + 2 * y + c
        buf[me] = v_ref[...]

        def peer(r):
            return ((1 - x) if (r >> 2) & 1 else x, (1 - y) if (r >> 1) & 1 else y, (1 - c) if r & 1 else c)

        sends = []
        for r in range(1, N_DEV):
            cp = pltpu.make_async_remote_copy(src_ref=v_ref, dst_ref=buf.at[me], send_sem=send.at[r], recv_sem=recv.at[r],
                                              device_id=peer(r), device_id_type=MESH)
            cp.start()
            sends.append(cp)
        for r in range(1, N_DEV):
            px, py, pc = peer(r)
            pltpu.make_async_remote_copy(src_ref=v_ref, dst_ref=buf.at[4 * px + 2 * py + pc], send_sem=send.at[r],
                                         recv_sem=recv.at[r], device_id=(px, py, pc), device_id_type=MESH).wait_recv()
        for cp in sends:
            cp.wait_send()
        tot = buf[0]
        for d in range(1, N_DEV):
            tot = tot + buf[d]
        o_ref[...] = tot

    return pl.pallas_call(
        body, name="all_sum_small", in_specs=[pl.BlockSpec(memory_space=pltpu.VMEM)],
        out_specs=pl.BlockSpec(memory_space=pltpu.VMEM), out_shape=jax.ShapeDtypeStruct((1, P), F32),
        scratch_shapes=[pltpu.VMEM((N_DEV, 1, P), F32), pltpu.SemaphoreType.DMA((N_DEV,)), pltpu.SemaphoreType.DMA((N_DEV,))],
    )(vec)


def _half_sum(name, blocks, theirs, core, tr):
    _, R, C = blocks.shape
    hr = R // 2
    nb = hr // tr
    assert nb * tr == hr

    def body(c_ref, a_ref, b_ref, o_ref):
        o_ref[...] = (a_ref[...] + b_ref[...]).astype(BF16)

    grid_spec = pltpu.PrefetchScalarGridSpec(
        num_scalar_prefetch=1, grid=(4, nb),
        in_specs=[pl.BlockSpec((1, tr, C), lambda b, i, c_ref: (b, c_ref[0] * nb + i, 0)),
                  pl.BlockSpec((1, tr, C), lambda b, i, c_ref: (b, i, 0))],
        out_specs=pl.BlockSpec((1, tr, C), lambda b, i, c_ref: (b, i, 0)))
    return pl.pallas_call(body, name=name, grid_spec=grid_spec, out_shape=jax.ShapeDtypeStruct((4, hr, C), BF16),
                          compiler_params=_params(2, 40 << 20))(core, blocks, theirs)


def _sum4(name, stack, tr):
    _, R, C = stack.shape

    def body(s_ref, o_ref):
        s = s_ref[...].astype(F32)
        o_ref[...] = ((s[0] + s[1]) + s[2]) + s[3]

    return pl.pallas_call(body, name=name, grid=(R // tr,), in_specs=[pl.BlockSpec((4, tr, C), lambda i: (0, i, 0))],
                          out_specs=pl.BlockSpec((tr, C), lambda i: (i, 0)), out_shape=jax.ShapeDtypeStruct((R, C), F32),
                          compiler_params=_params(1, 40 << 20))(stack)


def _adamw(name, w, m, v, g, tr):
    R, C = w.shape
    c1 = 1.0 - ADAM_B1 ** ADAM_STEP
    c2 = 1.0 - ADAM_B2 ** ADAM_STEP

    def body(w_ref, m_ref, v_ref, ga_ref, g_ref, d_ref, nm_ref, nv_ref):
        g = ga_ref[...]
        nm = ADAM_B1 * m_ref[...] + (1.0 - ADAM_B1) * g
        nv = ADAM_B2 * v_ref[...] + (1.0 - ADAM_B2) * (g * g)
        g_ref[...] = g
        nm_ref[...] = nm
        nv_ref[...] = nv
        d_ref[...] = -ADAM_LR * ((nm / c1) / (jnp.sqrt(nv / c2) + ADAM_EPS) + ADAM_WD * w_ref[...])

    spec = pl.BlockSpec((tr, C), lambda i: (i, 0))
    return pl.pallas_call(body, name=name, grid=(R // tr,), in_specs=[spec] * 4, out_specs=[spec] * 4,
                          out_shape=[jax.ShapeDtypeStruct((R, C), F32)] * 4, compiler_params=_params(1, 40 << 20))(w, m, v, g)


def _row_tile(rows, cols, budget_bytes=1 << 20, mult=8):
    best = None
    for t in range(mult, rows + 1, mult):
        if rows % t == 0 and t * cols * 4 <= budget_bytes:
            best = t
    return best if best is not None else rows


def _ln_fwd(r, g, b):
    mu = jnp.mean(r, axis=1, keepdims=True)
    xc = r - mu
    rstd = lax.rsqrt(jnp.mean(xc * xc, axis=1, keepdims=True) + LN_EPS)
    xhat = xc * rstd
    return xhat * g + b, xhat, rstd


def _ln_bwd(dy, xhat, rstd, g):
    dxh = dy * g
    return rstd * (dxh - jnp.mean(dxh, axis=1, keepdims=True) - xhat * jnp.mean(dxh * xhat, axis=1, keepdims=True))


def _to_chip_blocks_cols(a):
    R, C4 = a.shape
    return a.reshape(R, 4, C4 // 4).transpose(1, 0, 2)


def _from_chip_blocks_cols(a):
    return a.transpose(1, 0, 2).reshape(a.shape[1], 4 * a.shape[2])


def kernel(x, w_in, b_forget, conv_w, conv_b, dt_bias, a_log, d_skip, ssm_norm_w, w_proj_attn, w_proj_ssm, b_gates, w_out, ln1_g, ln1_b, w_ffn_gate, w_ffn_up, w_ffn_down, ln2_g, ln2_b, loss_target, m_w_in, m_b_forget, m_conv_w, m_conv_b, m_dt_bias, m_a_log, m_d_skip, m_ssm_norm_w, m_w_proj_attn, m_w_proj_ssm, m_b_gates, m_w_out, m_ln1_g, m_ln1_b, m_w_ffn_gate, m_w_ffn_up, m_w_ffn_down, m_ln2_g, m_ln2_b, v_w_in, v_b_forget, v_conv_w, v_conv_b, v_dt_bias, v_a_log, v_d_skip, v_ssm_norm_w, v_w_proj_attn, v_w_proj_ssm, v_b_gates, v_w_out, v_ln1_g, v_ln1_b, v_w_ffn_gate, v_w_ffn_up, v_w_ffn_down, v_ln2_g, v_ln2_b):
    S = x.shape[1]
    D = D_MODEL
    TM, TM2, TA, LC, TS, TB = (min(TILES[k], S) for k in ("TM", "TM2", "TA", "LC", "TS", "TB"))
    xf = x[0]
    tgt = loss_target[0]
    xb = xf.astype(BF16)

    g_in, g_cw, g_pa, g_ps, g_out, g_fg, g_fu, g_fd = _chip_gather("gather_weights", [
        w_in[0].astype(BF16), conv_w[0], w_proj_attn[0].astype(BF16), w_proj_ssm[0].astype(BF16), w_out[0].astype(BF16),
        w_ffn_gate[0].astype(BF16), w_ffn_up[0].astype(BF16), w_ffn_down[0].astype(BF16)],
        [True, False, True, True, True, True, True, True])
    w_full = _from_chip_blocks_cols(g_in)
    w_re = jnp.concatenate([w_full[:, 0:3072], w_full[:, 3088:5136], w_full[:, 5136:8208], w_full[:, 8240:10288],
                            w_full[:, 3072:3088], w_full[:, 8208:8240], jnp.zeros((D, 80), BF16)], axis=1)
    conv_w_full = _from_chip_blocks_cols(g_cw)
    wpa, wps, wout = g_pa.reshape(D, D), g_ps.reshape(SSM_INNER, D), g_out.reshape(D, D)
    wfg, wfu, wfd = _from_chip_blocks_cols(g_fg), _from_chip_blocks_cols(g_fu), g_fd.reshape(FFN_HIDDEN, D)

    def plain(accs, rows, vecs, j):
        return [accs[0]], []

    def q_scaled(accs, rows, vecs, j):
        return [accs[0] * jnp.where(j * 512 < D, 1.0 / math.sqrt(ATT_HEAD_DIM), 1.0)], []

    qkv, = _mm("proj_qkv", S, 3072, TM, 512, [(xb, D, 0)], [(w_re, 0)], [(0, 0)], q_scaled, [(3072, BF16, 0)])
    z, = _mm("proj_z", S, 2048, TM, 512, [(xb, D, 0)], [(w_re, RE_Z // 512)], [(0, 0)], plain, [(2048, F32, 0)])
    xbc_raw, = _mm("proj_xbc", S, 3072, TM, 512, [(xb, D, 0)], [(w_re, RE_XBC // 512)], [(0, 0)], plain, [(3072, F32, 0)])
    gl, = _mm("proj_gate", S, 2048, TM, 512, [(xb, D, 0)], [(w_re, RE_GATE // 512)], [(0, 0)], plain, [(2048, F32, 0)])
    small, = _mm("proj_small", S, 128, TM, 128, [(xb, D, 0)], [(w_re, RE_SMALL // 128)], [(0, 0)], plain, [(128, F32, 0)])

    bvec = jnp.concatenate([b_forget, jnp.zeros((1, LANES - ATT_HEADS), F32)], axis=1)
    cum = _cum_fwd(small, bvec, TB)[:, :ATT_HEADS]
    cum_rows = cum.T.reshape(8, 2, S)
    cum_cols = cum.reshape(S, 8, 2).transpose(1, 0, 2)
    o, o32, lse2d = _attn_fwd(qkv, cum_cols, cum_rows, TA)

    cb_row = conv_b
    xbc = _conv_fwd(xbc_raw, conv_w_full, cb_row, TS, 512)
    dt_raw = small[:, 16:48]
    dtb = jnp.repeat(dt_raw, ATT_HEAD_DIM, axis=1)
    dtr = dt_raw.T.reshape(SSM_GROUPS, 8, S)
    bias_b = jnp.repeat(dt_bias, ATT_HEAD_DIM, axis=1)
    alog_b = jnp.repeat(a_log, ATT_HEAD_DIM, axis=1)
    dskip_b = jnp.repeat(d_skip, ATT_HEAD_DIM, axis=1)
    bias_c = dt_bias.reshape(SSM_GROUPS, 8, 1)
    alog_c = a_log.reshape(SSM_GROUPS, 8, 1)
    y_ssd, ssm, hs_all = _ssd_fwd(xbc, z, dtb, dtr, bias_b, alog_b, dskip_b, ssm_norm_w, bias_c, alog_c, LC)

    def merge(accs, rows, vecs, j):
        g0, g1 = _sigmoid(rows[0] + vecs[0]), _sigmoid(rows[1] + vecs[1])
        return [g0 * accs[0] + g1 * accs[1], accs[0], accs[1]], []

    mix, attn_d, ssm_d = _mm("merge", S, D, TM, 512, [(o, D, 0), (ssm, SSM_INNER, 0)], [(wpa, 0), (wps, 0)], [(0, 0), (1, 1)],
                             merge, [(D, BF16, 0), (D, F32, 0), (D, F32, 0)], rows=[(gl, 0), (gl, 2)],
                             vecs_n=[(b_gates, 0), (b_gates, 2)])

    def out_ln1(accs, rows, vecs, j):
        r1 = ALPHA * rows[0] + accs[0]
        h1, _, _ = _ln_fwd(r1, vecs[0], vecs[1])
        return [r1, h1, h1], []

    r1, h1, h1b = _mm("out_ln1", S, D, TM2, D, [(mix, D, 0)], [(wout, 0)], [(0, 0)], out_ln1,
                      [(D, F32, 0), (D, F32, 0), (D, BF16, 0)], rows=[(xf, 0)], vecs_n=[(ln1_g, 0), (ln1_b, 0)])

    FT = FFN_HIDDEN // 2

    def swiglu(accs, rows, vecs, j):
        g, u = accs
        return [g, u, g * _sigmoid(g) * u], []

    gate, up, hmid = _mm("ffn_up", S, FFN_HIDDEN, TM2, FT, [(h1b, D, 0)], [(wfg, 0), (wfu, 0)], [(0, 0), (0, 1)], swiglu,
                         [(FFN_HIDDEN, F32, 0), (FFN_HIDDEN, F32, 0), (FFN_HIDDEN, BF16, 0)])

    def down_ln2_loss(accs, rows, vecs, j):
        r2 = ALPHA * rows[0] + accs[0]
        yv, xhat, rstd = _ln_fwd(r2, vecs[0], vecs[1])
        diff = yv - rows[1]
        dy = diff * (1.0 / D_MODEL)
        dr2 = _ln_bwd(dy, xhat, rstd, vecs[0])
        return [dr2, dr2], [jnp.sum(dy * xhat, axis=0, keepdims=True), jnp.sum(dy, axis=0, keepdims=True),
                            (0.5 / D_MODEL) * jnp.sum(diff * diff, axis=0, keepdims=True)]

    dr2, dr2b, dln2_g, dln2_b, loss_lanes = _mm("ffn_down_ln2", S, D, TM2, D, [(hmid, FFN_HIDDEN, 0)], [(wfd, 0)], [(0, 0)],
                                               down_ln2_loss, [(D, F32, 0), (D, BF16, 0)], rows=[(h1, 0), (tgt, 0)],
                                               vecs_n=[(ln2_g, 0), (ln2_b, 0)], sums=[D, D, D])
    loss = lax.psum(jnp.sum(loss_lanes), ("x", "y", "c"))

    def dswiglu(accs, rows, vecs, j):
        g, u = rows
        sg = _sigmoid(g)
        return [accs[0] * u * sg * (1.0 + g * (1.0 - sg)), accs[0] * g * sg], []

    dgate, dup = _mm("ffn_down_bwd", S, FFN_HIDDEN, TM2, FT, [(dr2b, D, 0)], [(wfd, 0)], [(0, 0)], dswiglu,
                     [(FFN_HIDDEN, BF16, 0), (FFN_HIDDEN, BF16, 0)], nt=True, rows=[(gate, 0), (up, 0)])
    dwfd = _mm_tn("dw_ffn_down", hmid, dr2b, FFN_HIDDEN // 2, D, TS)
    dwfg = _mm_tn("dw_ffn_gate", h1b, dgate, D, FT, TS)
    dwfu = _mm_tn("dw_ffn_up", h1b, dup, D, FT, TS)

    def dh1_ln1(accs, rows, vecs, j):
        dh1 = ALPHA * rows[0] + accs[0] + accs[1]
        _, xhat, rstd = _ln_fwd(rows[1], vecs[0], vecs[0])
        dr1 = _ln_bwd(dh1, xhat, rstd, vecs[0])
        return [dr1, dr1], [jnp.sum(dh1 * xhat, axis=0, keepdims=True), jnp.sum(dh1, axis=0, keepdims=True)]

    dr1, dr1b, dln1_g, dln1_b = _mm("ffn_up_bwd_ln1", S, D, TM2, D, [(dgate, FFN_HIDDEN, 0), (dup, FFN_HIDDEN, 0)],
                                    [(wfg, 0), (wfu, 0)], [(0, 0), (1, 1)], dh1_ln1, [(D, F32, 0), (D, BF16, 0)], nt=True,
                                    rows=[(dr2, 0), (r1, 0)], vecs_n=[(ln1_g, 0)], sums=[D, D])

    def dmerge(accs, rows, vecs, j):
        dmix = accs[0]
        g0, g1 = _sigmoid(rows[0] + vecs[0]), _sigmoid(rows[1] + vecs[1])
        dgl0 = dmix * rows[2] * g0 * (1.0 - g0)
        dgl1 = dmix * rows[3] * g1 * (1.0 - g1)
        return [dmix * g0, dmix * g1, dgl0, dgl1], [jnp.sum(dgl0, axis=0, keepdims=True), jnp.sum(dgl1, axis=0, keepdims=True)]

    d_attn_d, d_ssm_d, dgl0, dgl1, dbg0, dbg1 = _mm(
        "out_bwd", S, D, TM, 512, [(dr1b, D, 0)], [(wout, 0)], [(0, 0)], dmerge, [(D, BF16, 0)] * 4, nt=True,
        rows=[(gl, 0), (gl, 2), (attn_d, 0), (ssm_d, 0)], vecs_n=[(b_gates, 0), (b_gates, 2)], sums=[D, D])
    dwout = _mm_tn("dw_out", mix, dr1b, D, D, TS)
    dwpa = _mm_tn("dw_proj_attn", o, d_attn_d, D, D, TS)
    dwps = _mm_tn("dw_proj_ssm", ssm, d_ssm_d, D, D, TS)

    def do_and_rowdot(accs, rows, vecs, j):
        lane = lax.broadcasted_iota(jnp.int32, (1, LANES), 1)
        prod = accs[0].astype(BF16).astype(F32) * rows[0]
        d0 = jnp.sum(jnp.where(lane < ATT_HEAD_DIM, prod, 0.0), axis=1, keepdims=True)
        d1 = jnp.sum(jnp.where(lane < ATT_HEAD_DIM, 0.0, prod), axis=1, keepdims=True)
        return [accs[0], jnp.where(lane == 0, d0, jnp.where(lane == 1, d1, 0.0))], []

    do, dd2d = _mm("proj_attn_bwd", S, D, TM, LANES, [(d_attn_d, D, 0)], [(wpa, 0)], [(0, 0)], do_and_rowdot,
                   [(D, BF16, 0), (D, F32, 0)], nt=True, rows=[(o32, 0)])
    lse_rows = lse2d.reshape(S, 8, LANES)[:, :, 0:2].transpose(1, 2, 0)
    dd_rows = dd2d.reshape(S, 8, LANES)[:, :, 0:2].transpose(1, 2, 0)
    stats_rows = jnp.concatenate([cum_rows, lse_rows, dd_rows, jnp.zeros((8, 2, S), F32)], axis=1)
    dq, dk, dv, dck2d, dcq2d = _attn_bwd(qkv, do, stats_rows, cum_cols, TA)

    def per_head(a2d):
        a = a2d.reshape(S, 8, LANES)[:, :, 0:2].reshape(S, ATT_HEADS)
        return jnp.concatenate([a, jnp.zeros((S, LANES - ATT_HEADS), F32)], axis=1)

    dfl, dbf = _cum_bwd(per_head(dck2d), per_head(dcq2d), small, bvec, TB)

    dssm, = _mm("proj_ssm_bwd", S, SSM_INNER, TM, 512, [(d_ssm_d, D, 0)], [(wps, 0)], [(0, 0)], plain, [(SSM_INNER, F32, 0)],
                nt=True)
    dxs, dbm, dcm, dz, ddtb, dnw, ddskip_b, dbias_b, dalog_b = _ssd_bwd(
        xbc, z, y_ssd, dssm, hs_all, dtb, dtr, bias_b, alog_b, dskip_b, ssm_norm_w, bias_c, alog_c, LC)
    dxbc = jnp.concatenate([dxs, dbm, dcm], axis=1)
    dxbc_raw, dconv_w, dconv_b = _conv_bwd(xbc_raw, dxbc, conv_w_full, cb_row, TS, 512)
    ddt_raw = ddtb[:, ::ATT_HEAD_DIM]

    dsmall = jnp.concatenate([dfl[:, :ATT_HEADS], ddt_raw, jnp.zeros((S, 80), F32)], axis=1).astype(BF16)
    def dx_first(accs, rows, vecs, j):
        return [ALPHA * rows[0] + sum(accs[1:], accs[0])], []

    def dx_more(accs, rows, vecs, j):
        return [rows[0] + sum(accs[1:], accs[0])], []

    wk = lambda col, width=D: (w_re, 0, col // width, width)
    dx_part, = _mm("dx_a", S, D, TM2, D, [(dq, D, 0), (dk, D, 0), (dv, D, 0), (dz, D, 0), (dz, D, 1)],
                   [wk(0), wk(1024), wk(2048), wk(RE_Z), wk(RE_Z + 1024)], [(k, k) for k in range(5)], dx_first,
                   [(D, F32, 0)], nt=True, rows=[(dr1, 0)])
    grad_x, = _mm("dx_b", S, D, TM2, D,
                  [(dxbc_raw, D, 0), (dxbc_raw, D, 1), (dxbc_raw, D, 2), (dgl0, D, 0), (dgl1, D, 0), (dsmall, LANES, 0)],
                  [wk(RE_XBC), wk(RE_XBC + 1024), wk(RE_XBC + 2048), wk(RE_GATE), wk(RE_GATE + 1024), wk(RE_SMALL, LANES)],
                  [(k, k) for k in range(6)], dx_more, [(D, F32, 0)], nt=True, rows=[(dx_part, 0)])
    dw_q, dw_k, dw_v = (_mm_tn("dw_in_" + nm, xb, g_, D, D, TS) for nm, g_ in (("q", dq), ("k", dk), ("v", dv)))
    dw_z = _mm_tn("dw_in_z", xb, dz, D, D, TS)
    dw_xbc = _mm_tn("dw_in_xbc", xb, dxbc_raw, D, D, TS)
    dw_g0, dw_g1 = _mm_tn("dw_in_g0", xb, dgl0, D, D, TS), _mm_tn("dw_in_g1", xb, dgl1, D, D, TS)
    dw_s = _mm_tn("dw_in_small", xb, dsmall, D, LANES, TS)
    dw_full = jnp.concatenate([dw_q, dw_k, dw_v, dw_s[:, 0:ATT_HEADS], dw_z, dw_xbc, dw_s[:, ATT_HEADS:ATT_HEADS + SSM_HEADS],
                               dw_g0, dw_g1], axis=1)

    blocks = [_to_chip_blocks_cols(dw_full), dwpa.reshape(4, D // 4, D), dwps.reshape(4, SSM_INNER // 4, D),
              dwout.reshape(4, D // 4, D), _to_chip_blocks_cols(dwfg), _to_chip_blocks_cols(dwfu),
              dwfd.reshape(4, FFN_HIDDEN // 4, D)]
    names = ["w_in", "w_proj_attn", "w_proj_ssm", "w_out", "w_ffn_gate", "w_ffn_up", "w_ffn_down"]
    core = lax.axis_index("c").astype(jnp.int32).reshape(1)
    theirs = _half_to_sibling("swap_halves", blocks)
    halves = [_half_sum("halfsum_" + nm, b, t, core, _row_tile(b.shape[1] // 2, b.shape[2], mult=16))
              for nm, b, t in zip(names, blocks, theirs)]
    stacks = _chip_scatter("scatter_grads", halves)
    reduced = [_sum4("sum_" + nm, st, _row_tile(st.shape[1], st.shape[2], mult=16)) for nm, st in zip(names, stacks)]
    grads = _halves_merge("merge_halves", reduced)
    big_w = [w_in, w_proj_attn, w_proj_ssm, w_out, w_ffn_gate, w_ffn_up, w_ffn_down]
    big_m = [m_w_in, m_w_proj_attn, m_w_proj_ssm, m_w_out, m_w_ffn_gate, m_w_ffn_up, m_w_ffn_down]
    big_v = [v_w_in, v_w_proj_attn, v_w_proj_ssm, v_w_out, v_w_ffn_gate, v_w_ffn_up, v_w_ffn_down]
    big = {}
    for nm, w_, m_, v_, g_ in zip(names, big_w, big_m, big_v, grads):
        res = _adamw("adamw_" + nm, w_[0], m_[0], v_[0], g_, _row_tile(w_.shape[1], w_.shape[2]))
        big[nm] = [r[None] for r in res]

    pick = lambda a: a[:, ::ATT_HEAD_DIM]
    dd_skip = ddskip_b.reshape(1, SSM_HEADS, ATT_HEAD_DIM).sum(axis=2)
    pieces = [dbf[:, :ATT_HEADS], dconv_w.reshape(1, SSM_CONV * SSM_CONV_DIM), dconv_b, pick(dbias_b), pick(dalog_b), dd_skip,
              dnw, dbg0, dbg1, dln1_g, dln1_b, dln2_g, dln2_b]
    widths = [p.shape[1] for p in pieces]
    total = sum(widths)
    P = -(-total // LANES) * LANES
    packed = jnp.concatenate(pieces + [jnp.zeros((1, P - total), F32)], axis=1)
    summed = _all_sum_small(packed)
    offs = [0]
    for wd in widths:
        offs.append(offs[-1] + wd)
    sm = [summed[:, offs[k]:offs[k + 1]] for k in range(len(pieces))]
    g_bf, g_cw_full, g_cb, g_dtb, g_al, g_ds, g_nw = sm[0], sm[1].reshape(SSM_CONV, SSM_CONV_DIM), sm[2], sm[3], sm[4], sm[5], sm[6]
    g_bg = jnp.concatenate([sm[7], sm[8]], axis=1)
    g_l1g, g_l1b, g_l2g, g_l2b = sm[9], sm[10], sm[11], sm[12]
    chip = 2 * lax.axis_index("x") + lax.axis_index("y")
    cshard = SSM_CONV_DIM // 4
    g_cw_shard = lax.dynamic_slice_in_dim(g_cw_full, chip * cshard, cshard, axis=1)
    small_names = ["b_forget", "conv_w", "conv_b", "dt_bias", "a_log", "d_skip", "ssm_norm_w", "b_gates", "ln1_g", "ln1_b",
                   "ln2_g", "ln2_b"]
    small_g = [g_bf, g_cw_shard.reshape(1, -1), g_cb, g_dtb, g_al, g_ds, g_nw, g_bg, g_l1g, g_l1b, g_l2g, g_l2b]
    small_w = [b_forget, conv_w[0].reshape(1, -1), conv_b, dt_bias, a_log, d_skip, ssm_norm_w, b_gates, ln1_g, ln1_b, ln2_g, ln2_b]
    small_m = [m_b_forget, m_conv_w[0].reshape(1, -1), m_conv_b, m_dt_bias, m_a_log, m_d_skip, m_ssm_norm_w, m_b_gates, m_ln1_g,
               m_ln1_b, m_ln2_g, m_ln2_b]
    small_v = [v_b_forget, v_conv_w[0].reshape(1, -1), v_conv_b, v_dt_bias, v_a_log, v_d_skip, v_ssm_norm_w, v_b_gates, v_ln1_g,
               v_ln1_b, v_ln2_g, v_ln2_b]
    sw = [a.shape[1] for a in small_w]
    stot = sum(sw)
    SP = -(-stot // LANES) * LANES

    def pack(parts):
        return jnp.concatenate(list(parts) + [jnp.zeros((1, SP - stot), F32)], axis=1).reshape(SP // LANES, LANES)

    sres = _adamw("adamw_small", pack(small_w), pack(small_m), pack(small_v), pack(small_g), SP // LANES)
    soffs = [0]
    for wd in sw:
        soffs.append(soffs[-1] + wd)
    smalls = {}
    for k, nm in enumerate(small_names):
        vals = [r.reshape(1, SP)[:, soffs[k]:soffs[k + 1]] for r in sres]
        if nm == "conv_w":
            vals = [v_.reshape(1, SSM_CONV, cshard) for v_ in vals]
        smalls[nm] = vals

    order = ["w_in", "b_forget", "conv_w", "conv_b", "dt_bias", "a_log", "d_skip", "ssm_norm_w", "w_proj_attn", "w_proj_ssm",
             "b_gates", "w_out", "ln1_g", "ln1_b", "w_ffn_gate", "w_ffn_up", "w_ffn_down", "ln2_g", "ln2_b"]
    allres = {**big, **smalls}
    outs = [loss, grad_x[None]]
    for idx in range(4):
        outs += [allres[nm][idx] for nm in order]
    return tuple(outs)
```

```python
import functools
import math

import jax
import jax.numpy as jnp
from jax import lax
from jax.experimental import pallas as pl
from jax.experimental.pallas import tpu as pltpu

F32, BF16 = jnp.float32, jnp.bfloat16
MESH = pl.DeviceIdType.MESH

D_MODEL = 1024
ATT_HEADS, ATT_HEAD_DIM = 16, 64
SSM_INNER, SSM_HEADS, SSM_GROUPS, SSM_STATE, SSM_CONV = 2048, 32, 4, 128, 4
SSM_CONV_DIM = SSM_INNER + 2 * SSM_GROUPS * SSM_STATE
GROUP_LANES = SSM_INNER // SSM_GROUPS
FFN_HIDDEN = 2816
ALPHA = 2.0 ** 0.25
LN_EPS = 1e-5
RMS_EPS = 1e-5
ADAM_LR, ADAM_B1, ADAM_B2, ADAM_EPS, ADAM_WD, ADAM_STEP = 0.001, 0.9, 0.999, 1e-08, 0.01, 10
IN_SIZES = (1024, 1024, 1024, 16, 2048, 3072, 32, 2048)
IN_WIDTH = sum(IN_SIZES)
RE_WIDTH = 3072 + 2048 + 3072 + 2048 + 128
RE_Z, RE_XBC, RE_GATE, RE_SMALL = 3072, 5120, 8192, 10240

LANES = 128
VMEM_CAP = 60 * 1024 * 1024
NEG = -1e30
TILES = dict(TM=512, TM2=256, TA=512, LC=256, TS=512, TB=256)


def _params(n_axes, vmem_bytes=None):
    return pltpu.CompilerParams(dimension_semantics=("arbitrary",) * n_axes,
                                vmem_limit_bytes=None if vmem_bytes is None else int(min(vmem_bytes, VMEM_CAP)))


def _sigmoid(v):
    return 1.0 / (1.0 + jnp.exp(-v))


def _softplus(v):
    return jnp.maximum(v, 0.0) + jnp.log(1.0 + jnp.exp(-jnp.abs(v)))


def _dot(a, b):
    return lax.dot_general(a, b, (((1,), (0,)), ((), ())), preferred_element_type=F32)


def _dot_nt(a, b):
    return lax.dot_general(a, b, (((1,), (1,)), ((), ())), preferred_element_type=F32)


def _dot_tn(a, b):
    return lax.dot_general(a, b, (((0,), (0,)), ((), ())), preferred_element_type=F32)


def _split3(v):
    hi = v.astype(BF16)
    r1 = v - hi.astype(F32)
    mid = r1.astype(BF16)
    lo = (r1 - mid.astype(F32)).astype(BF16)
    return hi, mid, lo


def _dot_exact_left(m01, v):
    hi, mid, lo = _split3(v)
    return _dot(m01, hi) + _dot(m01, mid) + _dot(m01, lo)


def _dot_exact_right(v, m01, terms=3):
    parts = _split3(v)[:terms]
    out = _dot(parts[0], m01)
    for p in parts[1:]:
        out = out + _dot(p, m01)
    return out


def _mm(name, M, N, tm, tn, lhs, rhs, pairs, e_fn, outs, *, nt=False, rows=(), vecs_n=(), sums=()):
    ni, nj = M // tm, N // tn
    assert ni * tm == M and nj * tn == N, (name, M, N, tm, tn)
    n_l, n_r, n_row, n_vn, n_o, n_s = len(lhs), len(rhs), len(rows), len(vecs_n), len(outs), len(sums)

    def body(*refs):
        pos = 0
        l_refs = refs[pos:pos + n_l]; pos += n_l
        r_refs = refs[pos:pos + n_r]; pos += n_r
        row_refs = refs[pos:pos + n_row]; pos += n_row
        vn_refs = refs[pos:pos + n_vn]; pos += n_vn
        o_refs = refs[pos:pos + n_o]; pos += n_o
        s_refs = refs[pos:pos + n_s]; pos += n_s
        i, j = pl.program_id(0), pl.program_id(1)
        accs = []
        for li, ri in pairs:
            accs.append(_dot_nt(l_refs[li][...], r_refs[ri][...]) if nt else _dot(l_refs[li][...], r_refs[ri][...]))
        out_vals, sum_vals = e_fn(accs, [r[...] for r in row_refs], [r[...] for r in vn_refs], j)
        for r, v in zip(o_refs, out_vals):
            r[...] = v.astype(r.dtype)
        if n_s:
            col = pl.multiple_of(j * tn, LANES)

            @pl.when(i == 0)
            def _():
                for r, v in zip(s_refs, sum_vals):
                    r[:, pl.ds(col, tn)] = v

            @pl.when(i > 0)
            def _():
                for r, v in zip(s_refs, sum_vals):
                    r[:, pl.ds(col, tn)] += v

    in_specs, args, est = [], [], 0
    for arr, width, cb in lhs:
        in_specs.append(pl.BlockSpec((tm, width), lambda i, j, cb=cb: (i, cb)))
        args.append(arr); est += tm * width * arr.dtype.itemsize
    for arr, off, *ksub in rhs:
        if nt:
            kb, kw = ksub if ksub else (0, arr.shape[1])
            in_specs.append(pl.BlockSpec((tn, kw), lambda i, j, off=off, kb=kb: (j + off, kb)))
            est += tn * kw * arr.dtype.itemsize
        else:
            in_specs.append(pl.BlockSpec((arr.shape[0], tn), lambda i, j, off=off: (0, j + off)))
            est += tn * arr.shape[0] * arr.dtype.itemsize
        args.append(arr)
    for arr, off in rows:
        in_specs.append(pl.BlockSpec((tm, tn), lambda i, j, off=off: (i, j + off)))
        args.append(arr); est += tm * tn * arr.dtype.itemsize
    for arr, off in vecs_n:
        in_specs.append(pl.BlockSpec((1, tn), lambda i, j, off=off: (0, j + off)))
        args.append(arr); est += 8 * tn * 4
    out_shape, out_specs = [], []
    for total, dtype, off in outs:
        out_shape.append(jax.ShapeDtypeStruct((M, total), dtype))
        out_specs.append(pl.BlockSpec((tm, tn), lambda i, j, off=off: (i, j + off)))
        est += tm * tn * jnp.dtype(dtype).itemsize
    for total in sums:
        out_shape.append(jax.ShapeDtypeStruct((1, total), F32))
        out_specs.append(pl.BlockSpec((1, total), lambda i, j: (0, 0)))
        est += 8 * total * 4
    vmem = 2 * est + (len(pairs) + 4) * tm * tn * 4 + (8 << 20)
    return pl.pallas_call(body, name=name, grid=(ni, nj), in_specs=in_specs, out_specs=out_specs, out_shape=out_shape,
                          compiler_params=_params(2, vmem))(*args)


def _mm_tn(name, a, g, ta, tn, ts, a_cols=None, a_off=0):
    S = a.shape[0]
    Ka = a.shape[1] if a_cols is None else a_cols
    N = g.shape[1]
    assert Ka % ta == 0 and N % tn == 0 and S % ts == 0, (name, Ka, N, S)
    aoff = a_off // ta

    def body(a_ref, g_ref, o_ref):
        s = pl.program_id(2)
        part = _dot_tn(a_ref[...], g_ref[...])

        @pl.when(s == 0)
        def _():
            o_ref[...] = part

        @pl.when(s > 0)
        def _():
            o_ref[...] += part

    vmem = 2 * (ts * ta * 2 + ts * tn * 2 + ta * tn * 4) + 2 * ta * tn * 4 + (8 << 20)
    return pl.pallas_call(
        body, name=name, grid=(Ka // ta, N // tn, S // ts),
        in_specs=[pl.BlockSpec((ts, ta), lambda ia, jn, s: (s, ia + aoff)), pl.BlockSpec((ts, tn), lambda ia, jn, s: (s, jn))],
        out_specs=pl.BlockSpec((ta, tn), lambda ia, jn, s: (ia, jn)),
        out_shape=jax.ShapeDtypeStruct((Ka, N), F32), compiler_params=_params(3, vmem))(a, g)


def _tri(n, upper):
    r = lax.broadcasted_iota(jnp.int32, (n, n), 0)
    c = lax.broadcasted_iota(jnp.int32, (n, n), 1)
    return jnp.where((c >= r) if upper else (c <= r), 1.0, 0.0).astype(BF16)


def _logsig(v):
    return jnp.minimum(v, 0.0) - jnp.log(1.0 + jnp.exp(-jnp.abs(v)))


def _cum_fwd(small, bvec, tb):
    S = small.shape[0]

    def body(x_ref, b_ref, o_ref, carry):
        i = pl.program_id(0)

        @pl.when(i == 0)
        def _():
            carry[...] = jnp.zeros_like(carry)

        logf = _logsig(x_ref[...] + b_ref[...])
        cum = _dot_exact_left(_tri(tb, False), logf) + carry[0:1, :]
        o_ref[...] = cum
        carry[0:1, :] = cum[tb - 1:tb, :]

    return pl.pallas_call(
        body, name="cum_fwd", grid=(S // tb,),
        in_specs=[pl.BlockSpec((tb, LANES), lambda i: (i, 0)), pl.BlockSpec((1, LANES), lambda i: (0, 0))],
        out_specs=pl.BlockSpec((tb, LANES), lambda i: (i, 0)), out_shape=jax.ShapeDtypeStruct((S, LANES), F32),
        scratch_shapes=[pltpu.VMEM((8, LANES), F32)], compiler_params=_params(1))(small, bvec)


def _cum_bwd(dcum_k, dcum_q, small, bvec, tb):
    S = small.shape[0]
    nb = S // tb

    def body(dk_ref, dq_ref, x_ref, b_ref, o_ref, s_ref, carry):
        i = pl.program_id(0)

        @pl.when(i == 0)
        def _():
            carry[...] = jnp.zeros_like(carry)
            s_ref[...] = jnp.zeros_like(s_ref)

        rc = _dot_exact_left(_tri(tb, True), dk_ref[...] + dq_ref[...]) + carry[0:1, :]
        dfl = rc * _sigmoid(-(x_ref[...] + b_ref[...]))
        o_ref[...] = dfl
        s_ref[...] += jnp.sum(dfl, axis=0, keepdims=True)
        carry[0:1, :] = rc[0:1, :]

    rev = lambda i: (nb - 1 - i, 0)
    return pl.pallas_call(
        body, name="cum_bwd", grid=(nb,),
        in_specs=[pl.BlockSpec((tb, LANES), rev)] * 3 + [pl.BlockSpec((1, LANES), lambda i: (0, 0))],
        out_specs=[pl.BlockSpec((tb, LANES), rev), pl.BlockSpec((1, LANES), lambda i: (0, 0))],
        out_shape=[jax.ShapeDtypeStruct((S, LANES), F32), jax.ShapeDtypeStruct((1, LANES), F32)],
        scratch_shapes=[pltpu.VMEM((8, LANES), F32)], compiler_params=_params(1))(dcum_k, dcum_q, small, bvec)


N_AUG = 3


def _lane():
    return lax.broadcasted_iota(jnp.int32, (1, LANES), 1)


def _lane_mask():
    return _lane() < ATT_HEAD_DIM


def _aug_base(h):
    return ATT_HEAD_DIM * (1 - h)


def _attn_prep(qkv, cum_cols, T):
    S = qkv.shape[0]
    HP = ATT_HEADS // 2

    def body(q_ref, k_ref, c_ref, qa_ref, ka_ref):
        lane = _lane()
        q = q_ref[...].astype(F32)
        k = k_ref[...].astype(F32)
        for h in (0, 1):
            base = _aug_base(h)
            own = (lane < ATT_HEAD_DIM) if h == 0 else (lane >= ATT_HEAD_DIM)
            terms = [t.astype(F32) for t in _split3(c_ref[0, :, h:h + 1])]
            qa = jnp.where(lane == base + N_AUG, 0.0, jnp.where((lane >= base) & (lane < base + N_AUG), 1.0, q))
            ka = jnp.where(lane == base + N_AUG, 1.0, jnp.where(own, k, 0.0))
            for t in range(N_AUG):
                ka = jnp.where(lane == base + t, -terms[t], ka)
            qa_ref[:, h * LANES:(h + 1) * LANES] = qa.astype(BF16)
            ka_ref[:, h * LANES:(h + 1) * LANES] = ka.astype(BF16)

    return pl.pallas_call(
        body, name="attn_prep", grid=(S // T, HP),
        in_specs=[pl.BlockSpec((T, LANES), lambda i, hp: (i, hp)), pl.BlockSpec((T, LANES), lambda i, hp: (i, HP + hp)),
                  pl.BlockSpec((1, T, 2), lambda i, hp: (hp, i, 0))],
        out_specs=[pl.BlockSpec((T, 2 * LANES), lambda i, hp: (i, hp))] * 2,
        out_shape=[jax.ShapeDtypeStruct((S, 2 * D_MODEL), BF16)] * 2, compiler_params=_params(2))(qkv, qkv, cum_cols)


def _attn_fwd(qa, ka, qkv, T):
    S = qkv.shape[0]
    nq = S // T
    HP = ATT_HEADS // 2

    def body(q0_ref, q1_ref, k0_ref, k1_ref, v_ref, o_ref, o32_ref, lse_ref):
        i = pl.program_id(1)
        qs = (q0_ref[...], q1_ref[...])
        k_refs = (k0_ref, k1_ref)
        row = lax.broadcasted_iota(jnp.int32, (T, T), 0)
        col = lax.broadcasted_iota(jnp.int32, (T, T), 1)
        head_rows = lax.broadcasted_iota(jnp.int32, (LANES, 1), 0) < ATT_HEAD_DIM

        def block(j, carry, diag):
            off = pl.multiple_of(j * T, T)
            vj = v_ref[pl.ds(off, T), :]
            m0, l0, m1, l1, acc = carry
            new, alphas, pvs = [], [], []
            for h, (m, l) in enumerate(((m0, l0), (m1, l1))):
                st = _dot_nt(k_refs[h][pl.ds(off, T), :], qs[h])
                if diag:
                    st = jnp.where(row <= col, st, NEG)
                m_new = jnp.maximum(m, jnp.max(st, axis=0, keepdims=True))
                p = jnp.exp(st - m_new)
                alpha = jnp.exp(m - m_new)
                l_new = alpha * l + jnp.sum(p, axis=0, keepdims=True)
                pvs.append(_dot_tn(vj, p.astype(BF16)))
                alphas.append(alpha)
                new += [m_new, l_new]
            acc = acc * jnp.where(head_rows, alphas[0], alphas[1]) + jnp.where(head_rows, pvs[0], pvs[1])
            return (new[0], new[1], new[2], new[3], acc)

        init = (jnp.full((1, T), NEG, F32), jnp.zeros((1, T), F32), jnp.full((1, T), NEG, F32), jnp.zeros((1, T), F32),
                jnp.zeros((LANES, T), F32))
        carry = lax.fori_loop(0, i // 2, lambda jj, c: block(2 * jj + 1, block(2 * jj, c, False), False), init)
        carry = lax.cond(i % 2 == 1, lambda c: block(i - 1, c, False), lambda c: c, carry)
        m0, l0, m1, l1, acc = block(i, carry, True)
        out = (acc / jnp.where(head_rows, l0, l1)).T
        o_ref[...] = out.astype(BF16)
        o32_ref[...] = out
        lse_ref[0, 0:1, :] = m0 + jnp.log(l0)
        lse_ref[0, 1:2, :] = m1 + jnp.log(l1)

    vmem = 2 * (2 * T * LANES * 2 + 3 * S * LANES * 2 + T * LANES * (2 + 4) + 8 * T * 4) + 10 * T * T * 4 + (8 << 20)
    qspec = lambda h: pl.BlockSpec((T, LANES), lambda hp, i, h=h: (i, 2 * hp + h))
    kspec = lambda h: pl.BlockSpec((S, LANES), lambda hp, i, h=h: (0, 2 * hp + h))
    return pl.pallas_call(
        body, name="attn_fwd", grid=(HP, nq),
        in_specs=[qspec(0), qspec(1), kspec(0), kspec(1), pl.BlockSpec((S, LANES), lambda hp, i: (0, 2 * HP + hp))],
        out_specs=[pl.BlockSpec((T, LANES), lambda hp, i: (i, hp)), pl.BlockSpec((T, LANES), lambda hp, i: (i, hp)),
                   pl.BlockSpec((1, 2, T), lambda hp, i: (hp, 0, i))],
        out_shape=[jax.ShapeDtypeStruct((S, D_MODEL), BF16), jax.ShapeDtypeStruct((S, D_MODEL), F32),
                   jax.ShapeDtypeStruct((HP, 2, S), F32)],
        compiler_params=_params(2, vmem))(qa, qa, ka, ka, qkv)


def _attn_bwd(qa, ka, qkv, do, stats_rows, T):
    S = qkv.shape[0]
    nq = S // T
    HP = ATT_HEADS // 2

    def body(k0_ref, k1_ref, v_ref, q0_ref, q1_ref, do_ref, st_ref, dq_ref, dk_ref, dv_ref, dck_ref, dcq_ref, dq_acc):
        j = pl.program_id(1)
        mA = _lane_mask()
        lane = _lane()
        masks = (mA, jnp.logical_not(mA))
        q_refs = (q0_ref, q1_ref)

        @pl.when(j == 0)
        def _():
            dq_acc[...] = jnp.zeros_like(dq_acc)

        kas = (k0_ref[...], k1_ref[...])
        vj = v_ref[...]
        row = lax.broadcasted_iota(jnp.int32, (T, T), 0)
        col = lax.broadcasted_iota(jnp.int32, (T, T), 1)

        def block(i, carry, diag):
            dv_acc, dk0, dk1 = carry
            off = pl.multiple_of(i * T, T)
            doi = do_ref[pl.ds(off, T), :]
            zero = jnp.zeros_like(doi)
            dks = [dk0, dk1]
            for h in (0, 1):
                qh = q_refs[h][pl.ds(off, T), :]
                doh = jnp.where(masks[h], doi, zero)
                lse = st_ref[0, h:h + 1, pl.ds(off, T)]
                dd = st_ref[0, 2 + h:3 + h, pl.ds(off, T)]
                st = _dot_nt(kas[h], qh)
                if diag:
                    st = jnp.where(row <= col, st, NEG)
                pt = jnp.exp(st - lse)
                dpt = _dot_nt(vj, doh)
                dst = (pt * (dpt - dd)).astype(BF16)
                dv_acc = dv_acc + _dot(pt.astype(BF16), doh)
                dks[h] = dks[h] + _dot(dst, qh)
                dq_acc[h, pl.ds(off, T), :] += _dot_tn(dst, kas[h])
            return (dv_acc, dks[0], dks[1])

        z = jnp.zeros((T, LANES), F32)
        carry = block(j, (z, z, z), True)
        dv_acc, dk0, dk1 = lax.fori_loop(j + 1, nq, lambda i, c: block(i, c, False), carry)
        dv_ref[...] = dv_acc.astype(BF16)
        dk_ref[...] = jnp.where(mA, dk0, dk1).astype(BF16)
        ones_q = (_aug_base(0), _aug_base(1))
        dck_ref[...] = jnp.where(lane == 0, -dk0[:, ones_q[0]:ones_q[0] + 1],
                                 jnp.where(lane == 1, -dk1[:, ones_q[1]:ones_q[1] + 1], 0.0))

        @pl.when(j == nq - 1)
        def _():
            dq0, dq1 = dq_acc[0], dq_acc[1]
            ones_k = (_aug_base(0) + N_AUG, _aug_base(1) + N_AUG)
            dq_ref[...] = (jnp.where(mA, dq0, dq1) * (1.0 / math.sqrt(ATT_HEAD_DIM))).astype(BF16)
            dcq_ref[...] = jnp.where(lane == 0, dq0[:, ones_k[0]:ones_k[0] + 1],
                                     jnp.where(lane == 1, dq1[:, ones_k[1]:ones_k[1] + 1], 0.0))

    vmem = (2 * (3 * T * LANES * 2 + 3 * S * LANES * 2 + 8 * S * 4 + S * LANES * (2 + 4) + 2 * T * LANES * 2 + T * LANES * 4)
            + 2 * S * LANES * 4 + 12 * T * T * 4 + (8 << 20))
    kspec = lambda h: pl.BlockSpec((T, LANES), lambda hp, j, h=h: (j, 2 * hp + h))
    qspec = lambda h: pl.BlockSpec((S, LANES), lambda hp, j, h=h: (0, 2 * hp + h))
    blk = pl.BlockSpec((T, LANES), lambda hp, j: (j, hp))
    full = pl.BlockSpec((S, LANES), lambda hp, j: (0, hp))
    return pl.pallas_call(
        body, name="attn_bwd", grid=(HP, nq),
        in_specs=[kspec(0), kspec(1), pl.BlockSpec((T, LANES), lambda hp, j: (j, 2 * HP + hp)), qspec(0), qspec(1), full,
                  pl.BlockSpec((1, 8, S), lambda hp, j: (hp, 0, 0))],
        out_specs=[full, blk, blk, blk, full],
        out_shape=[jax.ShapeDtypeStruct((S, D_MODEL), BF16)] * 3 + [jax.ShapeDtypeStruct((S, D_MODEL), F32)] * 2,
        scratch_shapes=[pltpu.VMEM((2, S, LANES), F32)], compiler_params=_params(2, vmem))(ka, ka, qkv, qa, qa, do, stats_rows)


HALO = 8


def _conv_fwd(u, w, b, ts, tc):
    S, C = u.shape
    hb = ts // HALO

    def body(u_ref, prev_ref, w_ref, b_ref, o_ref, ext):
        i = pl.program_id(0)
        ext[0:HALO, :] = jnp.where(i == 0, 0.0, prev_ref[...])
        ext[HALO:HALO + ts, :] = u_ref[...]
        acc = b_ref[...] + w_ref[3:4, :] * u_ref[...]
        for k in range(SSM_CONV - 1):
            d = SSM_CONV - 1 - k
            acc = acc + w_ref[k:k + 1, :] * ext[HALO - d:HALO - d + ts, :]
        o_ref[...] = acc * _sigmoid(acc)

    return pl.pallas_call(
        body, name="conv_fwd", grid=(S // ts, C // tc),
        in_specs=[pl.BlockSpec((ts, tc), lambda i, j: (i, j)),
                  pl.BlockSpec((HALO, tc), lambda i, j: (jnp.maximum(i * hb - 1, 0), j)),
                  pl.BlockSpec((SSM_CONV, tc), lambda i, j: (0, j)), pl.BlockSpec((1, tc), lambda i, j: (0, j))],
        out_specs=pl.BlockSpec((ts, tc), lambda i, j: (i, j)), out_shape=jax.ShapeDtypeStruct((S, C), F32),
        scratch_shapes=[pltpu.VMEM((ts + HALO, tc), F32)], compiler_params=_params(2))(u, u, w, b)


def _conv_bwd(u, dy, w, b, ts, tc):
    S, C = u.shape
    hb = ts // HALO
    nb = S // ts
    E = ts + 2 * HALO

    def body(u_ref, uprev_ref, unext_ref, dy_ref, dynext_ref, w_ref, b_ref, du_ref, dw_ref, db_ref, uext, gext):
        i = pl.program_id(1)
        last = i == nb - 1
        uext[0:HALO, :] = jnp.where(i == 0, 0.0, uprev_ref[...])
        uext[HALO:HALO + ts, :] = u_ref[...]
        uext[HALO + ts:E, :] = unext_ref[...]
        n = ts + HALO
        pre = b_ref[...] + w_ref[3:4, :] * uext[HALO:HALO + n, :]
        for k in range(SSM_CONV - 1):
            d = SSM_CONV - 1 - k
            pre = pre + w_ref[k:k + 1, :] * uext[HALO - d:HALO - d + n, :]
        sg = _sigmoid(pre)
        dsilu = sg * (1.0 + pre * (1.0 - sg))
        gext[0:ts, :] = dy_ref[...] * dsilu[0:ts, :]
        gext[ts:n, :] = jnp.where(last, 0.0, dynext_ref[...] * dsilu[ts:n, :])
        g = gext[0:ts, :]
        du = w_ref[3:4, :] * g
        for k in range(SSM_CONV - 1):
            d = SSM_CONV - 1 - k
            du = du + w_ref[k:k + 1, :] * gext[d:d + ts, :]
        du_ref[...] = du.astype(du_ref.dtype)
        dws = [jnp.sum(g * uext[HALO - (SSM_CONV - 1 - k):HALO - (SSM_CONV - 1 - k) + ts, :], axis=0, keepdims=True)
               for k in range(SSM_CONV)]
        dbs = jnp.sum(g, axis=0, keepdims=True)

        @pl.when(i == 0)
        def _():
            for k in range(SSM_CONV):
                dw_ref[k:k + 1, :] = dws[k]
            db_ref[...] = dbs

        @pl.when(i > 0)
        def _():
            for k in range(SSM_CONV):
                dw_ref[k:k + 1, :] += dws[k]
            db_ref[...] += dbs

    nxt = lambda j, i: (jnp.minimum((i + 1) * hb, S // HALO - 1), j)
    return pl.pallas_call(
        body, name="conv_bwd", grid=(C // tc, nb),
        in_specs=[pl.BlockSpec((ts, tc), lambda j, i: (i, j)),
                  pl.BlockSpec((HALO, tc), lambda j, i: (jnp.maximum(i * hb - 1, 0), j)),
                  pl.BlockSpec((HALO, tc), nxt),
                  pl.BlockSpec((ts, tc), lambda j, i: (i, j)),
                  pl.BlockSpec((HALO, tc), nxt),
                  pl.BlockSpec((SSM_CONV, tc), lambda j, i: (0, j)), pl.BlockSpec((1, tc), lambda j, i: (0, j))],
        out_specs=[pl.BlockSpec((ts, tc), lambda j, i: (i, j)), pl.BlockSpec((SSM_CONV, tc), lambda j, i: (0, j)),
                   pl.BlockSpec((1, tc), lambda j, i: (0, j))],
        out_shape=[jax.ShapeDtypeStruct((S, C), BF16), jax.ShapeDtypeStruct((SSM_CONV, C), F32), jax.ShapeDtypeStruct((1, C), F32)],
        scratch_shapes=[pltpu.VMEM((E, tc), F32), pltpu.VMEM((ts + HALO, tc), F32)],
        compiler_params=_params(2))(u, u, u, dy, dy, w, b)


def _head_sum_matrix():
    r = jnp.right_shift(lax.broadcasted_iota(jnp.int32, (GROUP_LANES, GROUP_LANES), 0), 6)
    c = jnp.right_shift(lax.broadcasted_iota(jnp.int32, (GROUP_LANES, GROUP_LANES), 1), 6)
    return jnp.where(r == c, 1.0, 0.0).astype(BF16)


def _ssd_common(dtb_ref, dtr_ref, bias_b, alog_b, bias_c, alog_c, L):
    a_b = -jnp.exp(alog_b)
    dt = _softplus(dtb_ref[...] + bias_b)
    acum = _dot_exact_left(_tri(L, False), dt * a_b)
    a_c = -jnp.exp(alog_c)
    dtr = _softplus(dtr_ref[0] + bias_c)
    acum_r = _dot_exact_right(dtr * a_c, _tri(L, True))
    return a_b, dt, acum, acum_r


def _ssd_specs(L, nc, rev):
    cc = (lambda c: nc - 1 - c) if rev else (lambda c: c)
    G = SSM_GROUPS
    blk = pl.BlockSpec((L, GROUP_LANES), lambda g, c: (cc(c), g))
    xs = blk
    bm = pl.BlockSpec((L, SSM_STATE), lambda g, c: (cc(c), SSM_INNER // SSM_STATE + g))
    cm = pl.BlockSpec((L, SSM_STATE), lambda g, c: (cc(c), SSM_INNER // SSM_STATE + G + g))
    dtr = pl.BlockSpec((1, 8, L), lambda g, c: (g, 0, cc(c)))
    vec = pl.BlockSpec((1, GROUP_LANES), lambda g, c: (0, g))
    colv = pl.BlockSpec((1, 8, 1), lambda g, c: (g, 0, 0))
    hs = pl.BlockSpec((1, 1, SSM_STATE, GROUP_LANES), lambda g, c: (g, cc(c), 0, 0))
    return blk, xs, bm, cm, dtr, vec, colv, hs


def _ssd_fwd(xbc, z, dtb, dtr, bias_b, alog_b, dskip_b, normw, bias_c, alog_c, L):
    S = z.shape[0]
    nc = S // L
    blk, xs, bm, cm, dtrs, vec, colv, hs = _ssd_specs(L, nc, False)

    def body(x_ref, b_ref, c_ref, z_ref, dtb_ref, dtr_ref, bias_ref, alog_ref, dskip_ref, nw_ref, biasc_ref, alogc_ref,
             y_ref, ssm_ref, hs_ref, h_scr):
        c = pl.program_id(1)

        @pl.when(c == 0)
        def _():
            h_scr[...] = jnp.zeros_like(h_scr)

        mA = _lane_mask()
        a_b, dt, acum, acum_r = _ssd_common(dtb_ref, dtr_ref, bias_ref[...], alog_ref[...], biasc_ref[0], alogc_ref[0], L)
        x = x_ref[...]
        cb, bb = c_ref[...].astype(BF16), b_ref[...].astype(BF16)
        hprev = h_scr[...]
        hs_ref[0, 0] = hprev
        xdt = x * dt
        xdt_b = xdt.astype(BF16)
        gmat = _dot_nt(cb, bb)
        row = lax.broadcasted_iota(jnp.int32, (L, L), 0)
        col = lax.broadcasted_iota(jnp.int32, (L, L), 1)
        parts = []
        for p in range(GROUP_LANES // LANES):
            xp = xdt_b[:, p * LANES:(p + 1) * LANES]
            yd = []
            for hh in (0, 1):
                r = 2 * p + hh
                acol = acum[:, r * ATT_HEAD_DIM:r * ATT_HEAD_DIM + 1]
                arow = acum_r[r:r + 1, :]
                lm = jnp.exp(jnp.where(row >= col, acol - arow, NEG))
                yd.append(_dot((gmat * lm).astype(BF16), xp))
            parts.append(jnp.where(mA, yd[0], yd[1]))
        ydiag = jnp.concatenate(parts, axis=1)
        yoff = jnp.exp(acum) * _dot(cb, hprev.astype(BF16))
        y = ydiag + yoff + dskip_ref[...] * x
        aend = acum[L - 1:L, :]
        wgt = (jnp.exp(aend - acum) * xdt).astype(BF16)
        h_scr[...] = jnp.exp(aend) * hprev + _dot_tn(bb, wgt)
        y_ref[...] = y
        zz = z_ref[...]
        u = y * (zz * _sigmoid(zz))
        rs = lax.rsqrt(jnp.mean(u * u, axis=1, keepdims=True) + RMS_EPS)
        ssm_ref[...] = (u * rs * nw_ref[...]).astype(BF16)

    return pl.pallas_call(
        body, name="ssd_fwd", grid=(SSM_GROUPS, nc),
        in_specs=[xs, bm, cm, blk, blk, dtrs, vec, vec, vec, vec, colv, colv],
        out_specs=[blk, blk, hs],
        out_shape=[jax.ShapeDtypeStruct((S, SSM_INNER), F32), jax.ShapeDtypeStruct((S, SSM_INNER), BF16),
                   jax.ShapeDtypeStruct((SSM_GROUPS, nc, SSM_STATE, GROUP_LANES), F32)],
        scratch_shapes=[pltpu.VMEM((SSM_STATE, GROUP_LANES), F32)],
        compiler_params=_params(2, 48 << 20))(xbc, xbc, xbc, z, dtb, dtr, bias_b, alog_b, dskip_b, normw, bias_c, alog_c)


def _ssd_bwd(xbc, z, y, dssm, hs_all, dtb, dtr, bias_b, alog_b, dskip_b, normw, bias_c, alog_c, L):
    S = z.shape[0]
    nc = S // L
    blk, xs, bm, cm, dtrs, vec, colv, hs = _ssd_specs(L, nc, True)

    def body(x_ref, b_ref, c_ref, z_ref, y_ref, dssm_ref, hs_ref, dtb_ref, dtr_ref, bias_ref, alog_ref, dskip_ref, nw_ref,
             biasc_ref, alogc_ref,
             dx_ref, db_ref, dc_ref, dz_ref, ddt_ref, dnw_ref, ddskip_ref, dbias_ref, dalog_ref, dh_scr):
        c = pl.program_id(1)

        @pl.when(c == 0)
        def _():
            dh_scr[...] = jnp.zeros_like(dh_scr)

        mA = _lane_mask()
        masks = (mA, jnp.logical_not(mA))
        a_b, dt, acum, acum_r = _ssd_common(dtb_ref, dtr_ref, bias_ref[...], alog_ref[...], biasc_ref[0], alogc_ref[0], L)
        x, zz, y, dssm = x_ref[...], z_ref[...], y_ref[...], dssm_ref[...]
        cb, bb = c_ref[...].astype(BF16), b_ref[...].astype(BF16)
        hprev = hs_ref[0, 0]
        hb = hprev.astype(BF16)
        ds = dh_scr[...]
        dsb = ds.astype(BF16)
        dskip = dskip_ref[...]
        aend = acum[L - 1:L, :]
        e_a, e_end = jnp.exp(acum), jnp.exp(aend)
        dte = jnp.exp(aend - acum)
        xdt = x * dt
        xdt_b = xdt.astype(BF16)
        sg = _sigmoid(zz)
        sz = zz * sg
        u = y * sz
        rs = lax.rsqrt(jnp.mean(u * u, axis=1, keepdims=True) + RMS_EPS)
        un = u * rs
        dun = dssm * nw_ref[...]
        du = rs * (dun - un * jnp.mean(dun * un, axis=1, keepdims=True))
        dy = du * sz
        dz_ref[...] = (du * y * sg * (1.0 + zz * (1.0 - sg))).astype(dz_ref.dtype)
        dy_b = dy.astype(BF16)
        dch_b = (dy * e_a).astype(BF16)
        dc = _dot_nt(dch_b, hb)
        dhprev = _dot_tn(cb, dch_b)
        gt = _dot_nt(bb, cb)
        row = lax.broadcasted_iota(jnp.int32, (L, L), 0)
        col = lax.broadcasted_iota(jnp.int32, (L, L), 1)
        dgt = jnp.zeros((L, L), F32)
        parts = []
        for p in range(GROUP_LANES // LANES):
            xp = xdt_b[:, p * LANES:(p + 1) * LANES]
            dyp = dy_b[:, p * LANES:(p + 1) * LANES]
            zero = jnp.zeros_like(dyp)
            acc = None
            for hh in (0, 1):
                r = 2 * p + hh
                acol = acum[:, r * ATT_HEAD_DIM:r * ATT_HEAD_DIM + 1]
                arow = acum_r[r:r + 1, :]
                lmt = jnp.exp(jnp.where(row <= col, arow - acol, NEG))
                dyh = jnp.where(masks[hh], dyp, zero)
                part = _dot((gt * lmt).astype(BF16), dyh)
                acc = part if acc is None else acc + part
                dgt = dgt + _dot_nt(xp, dyh) * lmt
            parts.append(acc)
        dxdt_diag = jnp.concatenate(parts, axis=1)
        dgt_b = dgt.astype(BF16)
        db = _dot(dgt_b, cb)
        dc = dc + _dot_tn(dgt_b, bb)
        dxdt_state = dte * _dot(bb, dsb)
        db = db + _dot_nt((dte * xdt).astype(BF16), dsb)
        dxdt = dxdt_diag + dxdt_state
        dy_r, xdt_r = dy_b.astype(F32), xdt_b.astype(F32)
        dac = dy_r * (y - dskip * x) - xdt_r * dxdt
        tail = jnp.sum(xdt_r * dxdt_state, axis=0, keepdims=True) + e_end * jnp.sum(ds * hprev, axis=0, keepdims=True)
        rowl = lax.broadcasted_iota(jnp.int32, (L, 1), 0)
        dac = dac + jnp.where(rowl == L - 1, tail, 0.0)
        rc = _dot_exact_left(_tri(L, True), dac)
        hsum = _head_sum_matrix()
        hs1 = _dot_exact_right(dxdt * x, hsum, 2)
        hs2 = _dot_exact_right(rc, hsum, 2)
        ddt = hs1 + a_b * hs2
        ddtraw = ddt * _sigmoid(dtb_ref[...] + bias_ref[...])
        dx_ref[...] = dskip * dy + dxdt * dt
        db_ref[...] = db
        dc_ref[...] = dc
        ddt_ref[...] = ddtraw
        dh_scr[...] = e_end * ds + dhprev
        sums = (jnp.sum(dssm * un, axis=0, keepdims=True), jnp.sum(dy * x, axis=0, keepdims=True),
                jnp.sum(ddtraw, axis=0, keepdims=True), a_b * jnp.sum(hs2 * dt, axis=0, keepdims=True))
        refs = (dnw_ref, ddskip_ref, dbias_ref, dalog_ref)

        @pl.when(c == 0)
        def _():
            for r, v in zip(refs, sums):
                r[...] = v

        @pl.when(c > 0)
        def _():
            for r, v in zip(refs, sums):
                r[...] += v

    nbc = pl.BlockSpec((L, SSM_STATE), lambda g, c: (nc - 1 - c, g))
    return pl.pallas_call(
        body, name="ssd_bwd", grid=(SSM_GROUPS, nc),
        in_specs=[xs, bm, cm, blk, blk, blk, hs, blk, dtrs, vec, vec, vec, vec, colv, colv],
        out_specs=[blk, nbc, nbc, blk, blk, vec, vec, vec, vec],
        out_shape=[jax.ShapeDtypeStruct((S, SSM_INNER), F32), jax.ShapeDtypeStruct((S, SSM_GROUPS * SSM_STATE), F32),
                   jax.ShapeDtypeStruct((S, SSM_GROUPS * SSM_STATE), F32), jax.ShapeDtypeStruct((S, SSM_INNER), BF16),
                   jax.ShapeDtypeStruct((S, SSM_INNER), F32)] + [jax.ShapeDtypeStruct((1, SSM_INNER), F32)] * 4,
        scratch_shapes=[pltpu.VMEM((SSM_STATE, GROUP_LANES), F32)],
        compiler_params=_params(2, 56 << 20))(xbc, xbc, xbc, z, y, dssm, hs_all, dtb, dtr, bias_b, alog_b, dskip_b, normw,
                                              bias_c, alog_c)


def _place():
    return lax.axis_index("x"), lax.axis_index("y"), lax.axis_index("c")


def _other_chips(x, y):
    return [(1 - x, y), (x, 1 - y), (1 - x, 1 - y)]


def _half_rows(rows, which):
    hr = rows // 2
    if isinstance(which, int):
        return pl.ds(which * hr, hr)
    return pl.ds(pl.multiple_of(which * hr, 8), hr)


def _chip_gather(name, shards, split):
    n = len(shards)
    ANY = pl.BlockSpec(memory_space=pl.ANY)

    def body(*refs):
        ins, outs = refs[:n], refs[n:2 * n]
        send, recv, fsend, frecv = refs[2 * n:]
        x, y, c = _place()
        me = 2 * x + y
        sibling = (x, y, 1 - c)
        chips = _other_chips(x, y)

        def piece(a, chip_idx, which):
            if split[a]:
                return outs[a].at[chip_idx, _half_rows(shards[a].shape[0], which)]
            return outs[a].at[chip_idx]

        def ici(k, a, to_chip, src_chip):
            src = ins[a].at[_half_rows(shards[a].shape[0], c)] if split[a] else ins[a]
            return pltpu.make_async_remote_copy(src_ref=src, dst_ref=piece(a, src_chip, c), send_sem=send.at[k, a],
                                                recv_sem=recv.at[k, a], device_id=(*to_chip, c), device_id_type=MESH)

        def fwd(k, a, src_chip, which):
            return pltpu.make_async_remote_copy(src_ref=piece(a, src_chip, which), dst_ref=piece(a, src_chip, which),
                                                send_sem=fsend.at[k, a], recv_sem=frecv.at[k, a], device_id=sibling,
                                                device_id_type=MESH)

        sends = []
        for k, chip in enumerate(chips):
            for a in range(n):
                cp = ici(k, a, chip, me)
                cp.start()
                sends.append(cp)
        for k, (ox, oy) in enumerate(chips):
            src = 2 * ox + oy
            for a in range(n):
                ici(k, a, (ox, oy), src).wait_recv()
                if split[a]:
                    cp = fwd(k, a, src, c)
                    cp.start()
                    sends.append(cp)
        for k, (ox, oy) in enumerate(chips):
            for a in range(n):
                if split[a]:
                    fwd(k, a, 2 * ox + oy, 1 - c).wait_recv()
        for cp in sends:
            cp.wait_send()

    sem = pltpu.SemaphoreType.DMA((3, n))
    return pl.pallas_call(
        body, name=name, in_specs=[ANY] * n, out_specs=[ANY] * n,
        out_shape=[jax.ShapeDtypeStruct((4,) + s.shape, s.dtype) for s in shards],
        scratch_shapes=[sem, sem, sem, sem])(*shards)


def _chip_scatter(name, blocks):
    n = len(blocks)
    ANY = pl.BlockSpec(memory_space=pl.ANY)

    def body(*refs):
        ins, outs = refs[:n], refs[n:2 * n]
        send, recv, loc = refs[2 * n:]
        x, y, c = _place()
        me = 2 * x + y
        local = [pltpu.make_async_copy(ins[a].at[me], outs[a].at[me], loc.at[a]) for a in range(n)]
        for cp in local:
            cp.start()
        sends = []
        for k, (ox, oy) in enumerate(_other_chips(x, y)):
            dst_chip = 2 * ox + oy
            for a in range(n):
                cp = pltpu.make_async_remote_copy(src_ref=ins[a].at[dst_chip], dst_ref=outs[a].at[me], send_sem=send.at[k, a],
                                                  recv_sem=recv.at[k, a], device_id=(ox, oy, c), device_id_type=MESH)
                cp.start()
                sends.append(cp)
        for k, (ox, oy) in enumerate(_other_chips(x, y)):
            src = 2 * ox + oy
            for a in range(n):
                pltpu.make_async_remote_copy(src_ref=ins[a].at[me], dst_ref=outs[a].at[src], send_sem=send.at[k, a],
                                             recv_sem=recv.at[k, a], device_id=(ox, oy, c), device_id_type=MESH).wait_recv()
        for cp in sends:
            cp.wait_send()
        for cp in local:
            cp.wait()

    return pl.pallas_call(
        body, name=name, in_specs=[ANY] * n, out_specs=[ANY] * n,
        out_shape=[jax.ShapeDtypeStruct(b.shape, b.dtype) for b in blocks],
        scratch_shapes=[pltpu.SemaphoreType.DMA((3, n)), pltpu.SemaphoreType.DMA((3, n)), pltpu.SemaphoreType.DMA((n,))],
    )(*blocks)


def _half_to_sibling(name, blocks):
    n = len(blocks)
    ANY = pl.BlockSpec(memory_space=pl.ANY)

    def body(*refs):
        ins, outs = refs[:n], refs[n:2 * n]
        send, recv = refs[2 * n:]
        x, y, c = _place()
        cps = [pltpu.make_async_remote_copy(src_ref=ins[a].at[:, _half_rows(blocks[a].shape[1], 1 - c)], dst_ref=outs[a],
                                            send_sem=send.at[a], recv_sem=recv.at[a], device_id=(x, y, 1 - c),
                                            device_id_type=MESH) for a in range(n)]
        for cp in cps:
            cp.start()
        for cp in cps:
            cp.wait_recv()
        for cp in cps:
            cp.wait_send()

    return pl.pallas_call(
        body, name=name, in_specs=[ANY] * n, out_specs=[ANY] * n,
        out_shape=[jax.ShapeDtypeStruct((4, b.shape[1] // 2, b.shape[2]), b.dtype) for b in blocks],
        scratch_shapes=[pltpu.SemaphoreType.DMA((n,)), pltpu.SemaphoreType.DMA((n,))])(*blocks)


def _sibling_swap(name, arrs):
    n = len(arrs)
    ANY = pl.BlockSpec(memory_space=pl.ANY)

    def body(*refs):
        ins, outs = refs[:n], refs[n:2 * n]
        send, recv = refs[2 * n:]
        x, y, c = _place()
        cps = [pltpu.make_async_remote_copy(src_ref=ins[a], dst_ref=outs[a], send_sem=send.at[a], recv_sem=recv.at[a],
                                            device_id=(x, y, 1 - c), device_id_type=MESH) for a in range(n)]
        for cp in cps:
            cp.start()
        for cp in cps:
            cp.wait_recv()
        for cp in cps:
            cp.wait_send()

    return pl.pallas_call(
        body, name=name, in_specs=[ANY] * n, out_specs=[ANY] * n,
        out_shape=[jax.ShapeDtypeStruct(a.shape, a.dtype) for a in arrs],
        scratch_shapes=[pltpu.SemaphoreType.DMA((n,)), pltpu.SemaphoreType.DMA((n,))])(*arrs)


N_DEV = 8


def _all_sum_small(vec):
    P = vec.shape[1]

    def body(v_ref, o_ref, buf, send, recv):
        x, y, c = _place()
        me = 4 * x + 2 * y + c
        buf[me] = v_ref[...]

        def peer(r):
            return ((1 - x) if (r >> 2) & 1 else x, (1 - y) if (r >> 1) & 1 else y, (1 - c) if r & 1 else c)

        sends = []
        for r in range(1, N_DEV):
            cp = pltpu.make_async_remote_copy(src_ref=v_ref, dst_ref=buf.at[me], send_sem=send.at[r], recv_sem=recv.at[r],
                                              device_id=peer(r), device_id_type=MESH)
            cp.start()
            sends.append(cp)
        for r in range(1, N_DEV):
            px, py, pc = peer(r)
            pltpu.make_async_remote_copy(src_ref=v_ref, dst_ref=buf.at[4 * px + 2 * py + pc], send_sem=send.at[r],
                                         recv_sem=recv.at[r], device_id=(px, py, pc), device_id_type=MESH).wait_recv()
        for cp in sends:
            cp.wait_send()
        tot = buf[0]
        for d in range(1, N_DEV):
            tot = tot + buf[d]
        o_ref[...] = tot

    return pl.pallas_call(
        body, name="all_sum_small", in_specs=[pl.BlockSpec(memory_space=pltpu.VMEM)],
        out_specs=pl.BlockSpec(memory_space=pltpu.VMEM), out_shape=jax.ShapeDtypeStruct((1, P), F32),
        scratch_shapes=[pltpu.VMEM((N_DEV, 1, P), F32), pltpu.SemaphoreType.DMA((N_DEV,)), pltpu.SemaphoreType.DMA((N_DEV,))],
    )(vec)


def _half_sum(name, blocks, theirs, core, tr):
    _, R, C = blocks.shape
    hr = R // 2
    nb = hr // tr
    assert nb * tr == hr

    def body(c_ref, a_ref, b_ref, o_ref):
        o_ref[...] = (a_ref[...] + b_ref[...]).astype(BF16)

    grid_spec = pltpu.PrefetchScalarGridSpec(
        num_scalar_prefetch=1, grid=(4, nb),
        in_specs=[pl.BlockSpec((1, tr, C), lambda b, i, c_ref: (b, c_ref[0] * nb + i, 0)),
                  pl.BlockSpec((1, tr, C), lambda b, i, c_ref: (b, i, 0))],
        out_specs=pl.BlockSpec((1, tr, C), lambda b, i, c_ref: (b, i, 0)))
    return pl.pallas_call(body, name=name, grid_spec=grid_spec, out_shape=jax.ShapeDtypeStruct((4, hr, C), BF16),
                          compiler_params=_params(2, 40 << 20))(core, blocks, theirs)


def _sum4(name, stack, tr):
    _, R, C = stack.shape

    def body(s_ref, o_ref):
        s = s_ref[...].astype(F32)
        o_ref[...] = ((s[0] + s[1]) + s[2]) + s[3]

    return pl.pallas_call(body, name=name, grid=(R // tr,), in_specs=[pl.BlockSpec((4, tr, C), lambda i: (0, i, 0))],
                          out_specs=pl.BlockSpec((tr, C), lambda i: (i, 0)), out_shape=jax.ShapeDtypeStruct((R, C), F32),
                          compiler_params=_params(1, 40 << 20))(stack)


def _adamw_math(w, m, v, g):
    c1 = 1.0 - ADAM_B1 ** ADAM_STEP
    c2 = 1.0 - ADAM_B2 ** ADAM_STEP
    nm = ADAM_B1 * m + (1.0 - ADAM_B1) * g
    nv = ADAM_B2 * v + (1.0 - ADAM_B2) * (g * g)
    return -ADAM_LR * ((nm / c1) / (jnp.sqrt(nv / c2) + ADAM_EPS) + ADAM_WD * w), nm, nv


def _adamw(name, w, m, v, g, tr):
    R, C = w.shape

    def body(w_ref, m_ref, v_ref, ga_ref, g_ref, d_ref, nm_ref, nv_ref):
        g = ga_ref[...]
        g_ref[...] = g
        d_ref[...], nm_ref[...], nv_ref[...] = _adamw_math(w_ref[...], m_ref[...], v_ref[...], g)

    spec = pl.BlockSpec((tr, C), lambda i: (i, 0))
    return pl.pallas_call(body, name=name, grid=(R // tr,), in_specs=[spec] * 4, out_specs=[spec] * 4,
                          out_shape=[jax.ShapeDtypeStruct((R, C), F32)] * 4, compiler_params=_params(1, 40 << 20))(w, m, v, g)


def _adamw_halves(name, w, m, v, mine, theirs, core, tr):
    R, C = w.shape
    nb = (R // 2) // tr
    assert 2 * nb * tr == R

    def body(c_ref, w_ref, m_ref, v_ref, a_ref, b_ref, g_ref, d_ref, nm_ref, nv_ref):
        g = jnp.where((pl.program_id(0) // nb) == c_ref[0], a_ref[...], b_ref[...])
        g_ref[...] = g
        d_ref[...], nm_ref[...], nv_ref[...] = _adamw_math(w_ref[...], m_ref[...], v_ref[...], g)

    spec = pl.BlockSpec((tr, C), lambda i, c_ref: (i, 0))
    half = lambda own: pl.BlockSpec((tr, C), lambda i, c_ref, own=own: (
        jnp.clip(i - (c_ref[0] if own else 1 - c_ref[0]) * nb, 0, nb - 1), 0))
    grid_spec = pltpu.PrefetchScalarGridSpec(num_scalar_prefetch=1, grid=(R // tr,),
                                             in_specs=[spec, spec, spec, half(True), half(False)], out_specs=[spec] * 4)
    return pl.pallas_call(body, name=name, grid_spec=grid_spec, out_shape=[jax.ShapeDtypeStruct((R, C), F32)] * 4,
                          compiler_params=_params(1, 40 << 20))(core, w, m, v, mine, theirs)


def _row_tile(rows, cols, budget_bytes=1 << 20, mult=8):
    best = None
    for t in range(mult, rows + 1, mult):
        if rows % t == 0 and t * cols * 4 <= budget_bytes:
            best = t
    return best if best is not None else rows


def _ln_fwd(r, g, b):
    mu = jnp.mean(r, axis=1, keepdims=True)
    xc = r - mu
    rstd = lax.rsqrt(jnp.mean(xc * xc, axis=1, keepdims=True) + LN_EPS)
    xhat = xc * rstd
    return xhat * g + b, xhat, rstd


def _ln_bwd(dy, xhat, rstd, g):
    dxh = dy * g
    return rstd * (dxh - jnp.mean(dxh, axis=1, keepdims=True) - xhat * jnp.mean(dxh * xhat, axis=1, keepdims=True))


def _to_chip_blocks_cols(a):
    R, C4 = a.shape
    return a.reshape(R, 4, C4 // 4).transpose(1, 0, 2)


def _from_chip_blocks_cols(a):
    return a.transpose(1, 0, 2).reshape(a.shape[1], 4 * a.shape[2])


def kernel(x, w_in, b_forget, conv_w, conv_b, dt_bias, a_log, d_skip, ssm_norm_w, w_proj_attn, w_proj_ssm, b_gates, w_out, ln1_g, ln1_b, w_ffn_gate, w_ffn_up, w_ffn_down, ln2_g, ln2_b, loss_target, m_w_in, m_b_forget, m_conv_w, m_conv_b, m_dt_bias, m_a_log, m_d_skip, m_ssm_norm_w, m_w_proj_attn, m_w_proj_ssm, m_b_gates, m_w_out, m_ln1_g, m_ln1_b, m_w_ffn_gate, m_w_ffn_up, m_w_ffn_down, m_ln2_g, m_ln2_b, v_w_in, v_b_forget, v_conv_w, v_conv_b, v_dt_bias, v_a_log, v_d_skip, v_ssm_norm_w, v_w_proj_attn, v_w_proj_ssm, v_b_gates, v_w_out, v_ln1_g, v_ln1_b, v_w_ffn_gate, v_w_ffn_up, v_w_ffn_down, v_ln2_g, v_ln2_b):
    S = x.shape[1]
    D = D_MODEL
    TM, TM2, TA, LC, TS, TB = (min(TILES[k], S) for k in ("TM", "TM2", "TA", "LC", "TS", "TB"))
    xf = x[0]
    tgt = loss_target[0]
    xb = xf.astype(BF16)

    shards = [w_in[0].astype(BF16), conv_w[0], w_proj_attn[0].astype(BF16), w_proj_ssm[0].astype(BF16), w_out[0].astype(BF16),
              w_ffn_gate[0].astype(BF16), w_ffn_up[0].astype(BF16), w_ffn_down[0].astype(BF16)]
    chip = 2 * lax.axis_index("x") + lax.axis_index("y")
    gathered = _chip_gather("gather_weights", shards, [True, False, True, True, True, True, True, True])
    g_in, g_cw, g_pa, g_ps, g_out, g_fg, g_fu, g_fd = (lax.dynamic_update_slice(g, sh[None], (chip, 0, 0))
                                                       for g, sh in zip(gathered, shards))
    w_full = _from_chip_blocks_cols(g_in)
    w_re = jnp.concatenate([w_full[:, 0:3072], w_full[:, 3088:5136], w_full[:, 5136:8208], w_full[:, 8240:10288],
                            w_full[:, 3072:3088], w_full[:, 8208:8240], jnp.zeros((D, 80), BF16)], axis=1)
    conv_w_full = _from_chip_blocks_cols(g_cw)
    wpa, wps, wout = g_pa.reshape(D, D), g_ps.reshape(SSM_INNER, D), g_out.reshape(D, D)
    wfg, wfu, wfd = _from_chip_blocks_cols(g_fg), _from_chip_blocks_cols(g_fu), g_fd.reshape(FFN_HIDDEN, D)

    def plain(accs, rows, vecs, j):
        return [accs[0]], []

    def q_scaled(accs, rows, vecs, j):
        return [accs[0] * jnp.where(j * 512 < D, 1.0 / math.sqrt(ATT_HEAD_DIM), 1.0)], []

    qkv, = _mm("proj_qkv", S, 3072, TM, 512, [(xb, D, 0)], [(w_re, 0)], [(0, 0)], q_scaled, [(3072, BF16, 0)])
    z, = _mm("proj_z", S, 2048, TM, 512, [(xb, D, 0)], [(w_re, RE_Z // 512)], [(0, 0)], plain, [(2048, F32, 0)])
    xbc_raw, = _mm("proj_xbc", S, 3072, TM, 512, [(xb, D, 0)], [(w_re, RE_XBC // 512)], [(0, 0)], plain, [(3072, F32, 0)])
    gl, = _mm("proj_gate", S, 2048, TM, 512, [(xb, D, 0)], [(w_re, RE_GATE // 512)], [(0, 0)], plain, [(2048, F32, 0)])
    small, = _mm("proj_small", S, 128, TM, 128, [(xb, D, 0)], [(w_re, RE_SMALL // 128)], [(0, 0)], plain, [(128, F32, 0)])

    bvec = jnp.concatenate([b_forget, jnp.zeros((1, LANES - ATT_HEADS), F32)], axis=1)
    cum = _cum_fwd(small, bvec, TB)[:, :ATT_HEADS]
    cum_cols = cum.reshape(S, 8, 2).transpose(1, 0, 2)
    qa, ka = _attn_prep(qkv, cum_cols, TM)
    o, o32, lse_rows = _attn_fwd(qa, ka, qkv, TA)

    cb_row = conv_b
    xbc = _conv_fwd(xbc_raw, conv_w_full, cb_row, TS, 512)
    dt_raw = small[:, 16:48]
    dtb = jnp.repeat(dt_raw, ATT_HEAD_DIM, axis=1)
    dtr = dt_raw.T.reshape(SSM_GROUPS, 8, S)
    bias_b = jnp.repeat(dt_bias, ATT_HEAD_DIM, axis=1)
    alog_b = jnp.repeat(a_log, ATT_HEAD_DIM, axis=1)
    dskip_b = jnp.repeat(d_skip, ATT_HEAD_DIM, axis=1)
    bias_c = dt_bias.reshape(SSM_GROUPS, 8, 1)
    alog_c = a_log.reshape(SSM_GROUPS, 8, 1)
    y_ssd, ssm, hs_all = _ssd_fwd(xbc, z, dtb, dtr, bias_b, alog_b, dskip_b, ssm_norm_w, bias_c, alog_c, LC)

    def merge(accs, rows, vecs, j):
        g0, g1 = _sigmoid(rows[0] + vecs[0]), _sigmoid(rows[1] + vecs[1])
        return [g0 * accs[0] + g1 * accs[1], accs[0], accs[1]], []

    mix, attn_d, ssm_d = _mm("merge", S, D, TM, 512, [(o, D, 0), (ssm, SSM_INNER, 0)], [(wpa, 0), (wps, 0)], [(0, 0), (1, 1)],
                             merge, [(D, BF16, 0), (D, F32, 0), (D, F32, 0)], rows=[(gl, 0), (gl, 2)],
                             vecs_n=[(b_gates, 0), (b_gates, 2)])

    def out_ln1(accs, rows, vecs, j):
        r1 = ALPHA * rows[0] + accs[0]
        h1, _, _ = _ln_fwd(r1, vecs[0], vecs[1])
        return [r1, h1, h1], []

    r1, h1, h1b = _mm("out_ln1", S, D, TM2, D, [(mix, D, 0)], [(wout, 0)], [(0, 0)], out_ln1,
                      [(D, F32, 0), (D, F32, 0), (D, BF16, 0)], rows=[(xf, 0)], vecs_n=[(ln1_g, 0), (ln1_b, 0)])

    FT = FFN_HIDDEN // 2

    def swiglu(accs, rows, vecs, j):
        g, u = accs
        return [g, u, g * _sigmoid(g) * u], []

    gate, up, hmid = _mm("ffn_up", S, FFN_HIDDEN, TM2, FT, [(h1b, D, 0)], [(wfg, 0), (wfu, 0)], [(0, 0), (0, 1)], swiglu,
                         [(FFN_HIDDEN, F32, 0), (FFN_HIDDEN, F32, 0), (FFN_HIDDEN, BF16, 0)])

    def down_ln2_loss(accs, rows, vecs, j):
        r2 = ALPHA * rows[0] + accs[0]
        yv, xhat, rstd = _ln_fwd(r2, vecs[0], vecs[1])
        diff = yv - rows[1]
        dy = diff * (1.0 / D_MODEL)
        dr2 = _ln_bwd(dy, xhat, rstd, vecs[0])
        return [dr2, dr2], [jnp.sum(dy * xhat, axis=0, keepdims=True), jnp.sum(dy, axis=0, keepdims=True),
                            (0.5 / D_MODEL) * jnp.sum(diff * diff, axis=0, keepdims=True)]

    dr2, dr2b, dln2_g, dln2_b, loss_lanes = _mm("ffn_down_ln2", S, D, TM2, D, [(hmid, FFN_HIDDEN, 0)], [(wfd, 0)], [(0, 0)],
                                               down_ln2_loss, [(D, F32, 0), (D, BF16, 0)], rows=[(h1, 0), (tgt, 0)],
                                               vecs_n=[(ln2_g, 0), (ln2_b, 0)], sums=[D, D, D])
    loss = lax.psum(jnp.sum(loss_lanes), ("x", "y", "c"))

    def dswiglu(accs, rows, vecs, j):
        g, u = rows
        sg = _sigmoid(g)
        return [accs[0] * u * sg * (1.0 + g * (1.0 - sg)), accs[0] * g * sg], []

    dgate, dup = _mm("ffn_down_bwd", S, FFN_HIDDEN, TM2, FT, [(dr2b, D, 0)], [(wfd, 0)], [(0, 0)], dswiglu,
                     [(FFN_HIDDEN, BF16, 0), (FFN_HIDDEN, BF16, 0)], nt=True, rows=[(gate, 0), (up, 0)])
    dwfd = _mm_tn("dw_ffn_down", hmid, dr2b, FFN_HIDDEN // 2, D, TS)
    dwfg = _mm_tn("dw_ffn_gate", h1b, dgate, D, FT, TS)
    dwfu = _mm_tn("dw_ffn_up", h1b, dup, D, FT, TS)

    def dh1_ln1(accs, rows, vecs, j):
        dh1 = ALPHA * rows[0] + accs[0] + accs[1]
        _, xhat, rstd = _ln_fwd(rows[1], vecs[0], vecs[0])
        dr1 = _ln_bwd(dh1, xhat, rstd, vecs[0])
        return [dr1, dr1], [jnp.sum(dh1 * xhat, axis=0, keepdims=True), jnp.sum(dh1, axis=0, keepdims=True)]

    dr1, dr1b, dln1_g, dln1_b = _mm("ffn_up_bwd_ln1", S, D, TM2, D, [(dgate, FFN_HIDDEN, 0), (dup, FFN_HIDDEN, 0)],
                                    [(wfg, 0), (wfu, 0)], [(0, 0), (1, 1)], dh1_ln1, [(D, F32, 0), (D, BF16, 0)], nt=True,
                                    rows=[(dr2, 0), (r1, 0)], vecs_n=[(ln1_g, 0)], sums=[D, D])

    def dmerge(accs, rows, vecs, j):
        dmix = accs[0]
        g0, g1 = _sigmoid(rows[0] + vecs[0]), _sigmoid(rows[1] + vecs[1])
        dgl0 = dmix * rows[2] * g0 * (1.0 - g0)
        dgl1 = dmix * rows[3] * g1 * (1.0 - g1)
        return [dmix * g0, dmix * g1, dgl0, dgl1], [jnp.sum(dgl0, axis=0, keepdims=True), jnp.sum(dgl1, axis=0, keepdims=True)]

    d_attn_d, d_ssm_d, dgl0, dgl1, dbg0, dbg1 = _mm(
        "out_bwd", S, D, TM, 512, [(dr1b, D, 0)], [(wout, 0)], [(0, 0)], dmerge, [(D, BF16, 0)] * 4, nt=True,
        rows=[(gl, 0), (gl, 2), (attn_d, 0), (ssm_d, 0)], vecs_n=[(b_gates, 0), (b_gates, 2)], sums=[D, D])
    dwout = _mm_tn("dw_out", mix, dr1b, D, D, TS)
    dwpa = _mm_tn("dw_proj_attn", o, d_attn_d, D, D, TS)
    dwps = _mm_tn("dw_proj_ssm", ssm, d_ssm_d, D, D, TS)

    def do_and_rowdot(accs, rows, vecs, j):
        lane = lax.broadcasted_iota(jnp.int32, (1, LANES), 1)
        prod = accs[0].astype(BF16).astype(F32) * rows[0]
        d0 = jnp.sum(jnp.where(lane < ATT_HEAD_DIM, prod, 0.0), axis=1, keepdims=True)
        d1 = jnp.sum(jnp.where(lane < ATT_HEAD_DIM, 0.0, prod), axis=1, keepdims=True)
        return [accs[0], jnp.where(lane == 0, d0, jnp.where(lane == 1, d1, 0.0))], []

    do, dd2d = _mm("proj_attn_bwd", S, D, TM, LANES, [(d_attn_d, D, 0)], [(wpa, 0)], [(0, 0)], do_and_rowdot,
                   [(D, BF16, 0), (D, F32, 0)], nt=True, rows=[(o32, 0)])
    dd_rows = dd2d.reshape(S, 8, LANES)[:, :, 0:2].transpose(1, 2, 0)
    stats_rows = jnp.concatenate([lse_rows, dd_rows, jnp.zeros((8, 4, S), F32)], axis=1)
    dq, dk, dv, dck2d, dcq2d = _attn_bwd(qa, ka, qkv, do, stats_rows, TA)

    def per_head(a2d):
        a = a2d.reshape(S, 8, LANES)[:, :, 0:2].reshape(S, ATT_HEADS)
        return jnp.concatenate([a, jnp.zeros((S, LANES - ATT_HEADS), F32)], axis=1)

    dfl, dbf = _cum_bwd(per_head(dck2d), per_head(dcq2d), small, bvec, TB)

    dssm, = _mm("proj_ssm_bwd", S, SSM_INNER, TM, 512, [(d_ssm_d, D, 0)], [(wps, 0)], [(0, 0)], plain, [(SSM_INNER, F32, 0)],
                nt=True)
    dxs, dbm, dcm, dz, ddtb, dnw, ddskip_b, dbias_b, dalog_b = _ssd_bwd(
        xbc, z, y_ssd, dssm, hs_all, dtb, dtr, bias_b, alog_b, dskip_b, ssm_norm_w, bias_c, alog_c, LC)
    dxbc = jnp.concatenate([dxs, dbm, dcm], axis=1)
    dxbc_raw, dconv_w, dconv_b = _conv_bwd(xbc_raw, dxbc, conv_w_full, cb_row, TS, 512)
    ddt_raw = ddtb[:, ::ATT_HEAD_DIM]

    dsmall = jnp.concatenate([dfl[:, :ATT_HEADS], ddt_raw, jnp.zeros((S, 80), F32)], axis=1).astype(BF16)
    def dx_first(accs, rows, vecs, j):
        return [ALPHA * rows[0] + sum(accs[1:], accs[0])], []

    def dx_more(accs, rows, vecs, j):
        return [rows[0] + sum(accs[1:], accs[0])], []

    wk = lambda col, width=D: (w_re, 0, col // width, width)
    dx_part, = _mm("dx_a", S, D, TM2, D, [(dq, D, 0), (dk, D, 0), (dv, D, 0), (dz, D, 0), (dz, D, 1)],
                   [wk(0), wk(1024), wk(2048), wk(RE_Z), wk(RE_Z + 1024)], [(k, k) for k in range(5)], dx_first,
                   [(D, F32, 0)], nt=True, rows=[(dr1, 0)])
    grad_x, = _mm("dx_b", S, D, TM2, D,
                  [(dxbc_raw, D, 0), (dxbc_raw, D, 1), (dxbc_raw, D, 2), (dgl0, D, 0), (dgl1, D, 0), (dsmall, LANES, 0)],
                  [wk(RE_XBC), wk(RE_XBC + 1024), wk(RE_XBC + 2048), wk(RE_GATE), wk(RE_GATE + 1024), wk(RE_SMALL, LANES)],
                  [(k, k) for k in range(6)], dx_more, [(D, F32, 0)], nt=True, rows=[(dx_part, 0)])
    dw_q, dw_k, dw_v = (_mm_tn("dw_in_" + nm, xb, g_, D, D, TS) for nm, g_ in (("q", dq), ("k", dk), ("v", dv)))
    dw_z = _mm_tn("dw_in_z", xb, dz, D, D, TS)
    dw_xbc = _mm_tn("dw_in_xbc", xb, dxbc_raw, D, D, TS)
    dw_g0, dw_g1 = _mm_tn("dw_in_g0", xb, dgl0, D, D, TS), _mm_tn("dw_in_g1", xb, dgl1, D, D, TS)
    dw_s = _mm_tn("dw_in_small", xb, dsmall, D, LANES, TS)
    dw_full = jnp.concatenate([dw_q, dw_k, dw_v, dw_s[:, 0:ATT_HEADS], dw_z, dw_xbc, dw_s[:, ATT_HEADS:ATT_HEADS + SSM_HEADS],
                               dw_g0, dw_g1], axis=1)

    blocks = [_to_chip_blocks_cols(dw_full), dwpa.reshape(4, D // 4, D), dwps.reshape(4, SSM_INNER // 4, D),
              dwout.reshape(4, D // 4, D), _to_chip_blocks_cols(dwfg), _to_chip_blocks_cols(dwfu),
              dwfd.reshape(4, FFN_HIDDEN // 4, D)]
    names = ["w_in", "w_proj_attn", "w_proj_ssm", "w_out", "w_ffn_gate", "w_ffn_up", "w_ffn_down"]
    core = lax.axis_index("c").astype(jnp.int32).reshape(1)
    theirs = _half_to_sibling("swap_halves", blocks)
    halves = [_half_sum("halfsum_" + nm, b, t, core, _row_tile(b.shape[1] // 2, b.shape[2], mult=16))
              for nm, b, t in zip(names, blocks, theirs)]
    stacks = _chip_scatter("scatter_grads", halves)
    reduced = [_sum4("sum_" + nm, st, _row_tile(st.shape[1], st.shape[2], mult=16)) for nm, st in zip(names, stacks)]
    other = _sibling_swap("swap_reduced", reduced)
    big_w = [w_in, w_proj_attn, w_proj_ssm, w_out, w_ffn_gate, w_ffn_up, w_ffn_down]
    big_m = [m_w_in, m_w_proj_attn, m_w_proj_ssm, m_w_out, m_w_ffn_gate, m_w_ffn_up, m_w_ffn_down]
    big_v = [v_w_in, v_w_proj_attn, v_w_proj_ssm, v_w_out, v_w_ffn_gate, v_w_ffn_up, v_w_ffn_down]
    big = {}
    for nm, w_, m_, v_, mine, theirs in zip(names, big_w, big_m, big_v, reduced, other):
        res = _adamw_halves("adamw_" + nm, w_[0], m_[0], v_[0], mine, theirs, core, _row_tile(w_.shape[1] // 2, w_.shape[2]))
        big[nm] = [r[None] for r in res]

    pick = lambda a: a[:, ::ATT_HEAD_DIM]
    dd_skip = ddskip_b.reshape(1, SSM_HEADS, ATT_HEAD_DIM).sum(axis=2)
    pieces = [dbf[:, :ATT_HEADS], dconv_w.reshape(1, SSM_CONV * SSM_CONV_DIM), dconv_b, pick(dbias_b), pick(dalog_b), dd_skip,
              dnw, dbg0, dbg1, dln1_g, dln1_b, dln2_g, dln2_b]
    widths = [p.shape[1] for p in pieces]
    total = sum(widths)
    P = -(-total // LANES) * LANES
    packed = jnp.concatenate(pieces + [jnp.zeros((1, P - total), F32)], axis=1)
    summed = _all_sum_small(packed)
    offs = [0]
    for wd in widths:
        offs.append(offs[-1] + wd)
    sm = [summed[:, offs[k]:offs[k + 1]] for k in range(len(pieces))]
    g_bf, g_cw_full, g_cb, g_dtb, g_al, g_ds, g_nw = sm[0], sm[1].reshape(SSM_CONV, SSM_CONV_DIM), sm[2], sm[3], sm[4], sm[5], sm[6]
    g_bg = jnp.concatenate([sm[7], sm[8]], axis=1)
    g_l1g, g_l1b, g_l2g, g_l2b = sm[9], sm[10], sm[11], sm[12]
    cshard = SSM_CONV_DIM // 4
    g_cw_shard = lax.dynamic_slice_in_dim(g_cw_full, chip * cshard, cshard, axis=1)
    small_names = ["b_forget", "conv_w", "conv_b", "dt_bias", "a_log", "d_skip", "ssm_norm_w", "b_gates", "ln1_g", "ln1_b",
                   "ln2_g", "ln2_b"]
    small_g = [g_bf, g_cw_shard.reshape(1, -1), g_cb, g_dtb, g_al, g_ds, g_nw, g_bg, g_l1g, g_l1b, g_l2g, g_l2b]
    small_w = [b_forget, conv_w[0].reshape(1, -1), conv_b, dt_bias, a_log, d_skip, ssm_norm_w, b_gates, ln1_g, ln1_b, ln2_g, ln2_b]
    small_m = [m_b_forget, m_conv_w[0].reshape(1, -1), m_conv_b, m_dt_bias, m_a_log, m_d_skip, m_ssm_norm_w, m_b_gates, m_ln1_g,
               m_ln1_b, m_ln2_g, m_ln2_b]
    small_v = [v_b_forget, v_conv_w[0].reshape(1, -1), v_conv_b, v_dt_bias, v_a_log, v_d_skip, v_ssm_norm_w, v_b_gates, v_ln1_g,
               v_ln1_b, v_ln2_g, v_ln2_b]
    sw = [a.shape[1] for a in small_w]
    stot = sum(sw)
    SP = -(-stot // LANES) * LANES

    def pack(parts):
        return jnp.concatenate(list(parts) + [jnp.zeros((1, SP - stot), F32)], axis=1).reshape(SP // LANES, LANES)

    sres = _adamw("adamw_small", pack(small_w), pack(small_m), pack(small_v), pack(small_g), SP // LANES)
    soffs = [0]
    for wd in sw:
        soffs.append(soffs[-1] + wd)
    smalls = {}
    for k, nm in enumerate(small_names):
        vals = [r.reshape(1, SP)[:, soffs[k]:soffs[k + 1]] for r in sres]
        if nm == "conv_w":
            vals = [v_.reshape(1, SSM_CONV, cshard) for v_ in vals]
        smalls[nm] = vals

    order = ["w_in", "b_forget", "conv_w", "conv_b", "dt_bias", "a_log", "d_skip", "ssm_norm_w", "w_proj_attn", "w_proj_ssm",
             "b_gates", "w_out", "ln1_g", "ln1_b", "w_ffn_gate", "w_ffn_up", "w_ffn_down", "ln2_g", "ln2_b"]
    allres = {**big, **smalls}
    outs = [loss, grad_x[None]]
    for idx in range(4):
        outs += [allres[nm][idx] for nm in order]
    return tuple(outs)
```

```python
import functools
import math

import jax
import jax.numpy as jnp
from jax import lax
from jax.experimental import pallas as pl
from jax.experimental.pallas import tpu as pltpu

F32, BF16 = jnp.float32, jnp.bfloat16
MESH = pl.DeviceIdType.MESH

D_MODEL = 1024
ATT_HEADS, ATT_HEAD_DIM = 16, 64
SSM_INNER, SSM_HEADS, SSM_GROUPS, SSM_STATE, SSM_CONV = 2048, 32, 4, 128, 4
SSM_CONV_DIM = SSM_INNER + 2 * SSM_GROUPS * SSM_STATE
GROUP_LANES = SSM_INNER // SSM_GROUPS
FFN_HIDDEN = 2816
ALPHA = 2.0 ** 0.25
LN_EPS = 1e-5
RMS_EPS = 1e-5
ADAM_LR, ADAM_B1, ADAM_B2, ADAM_EPS, ADAM_WD, ADAM_STEP = 0.001, 0.9, 0.999, 1e-08, 0.01, 10
IN_SIZES = (1024, 1024, 1024, 16, 2048, 3072, 32, 2048)
IN_WIDTH = sum(IN_SIZES)
RE_WIDTH = 3072 + 2048 + 3072 + 2048 + 128
RE_Z, RE_XBC, RE_GATE, RE_SMALL = 3072, 5120, 8192, 10240

LANES = 128
VMEM_CAP = 60 * 1024 * 1024
NEG = -1e30
TILES = dict(TM=1024, TM2=256, TA=512, LC=256, TS=2048, TB=256)


def _params(n_axes, vmem_bytes=None):
    return pltpu.CompilerParams(dimension_semantics=("arbitrary",) * n_axes,
                                vmem_limit_bytes=None if vmem_bytes is None else int(min(vmem_bytes, VMEM_CAP)))


def _sigmoid(v):
    return 1.0 / (1.0 + jnp.exp(-v))


def _softplus(v):
    return jnp.maximum(v, 0.0) + jnp.log(1.0 + jnp.exp(-jnp.abs(v)))


def _dot(a, b):
    return lax.dot_general(a, b, (((1,), (0,)), ((), ())), preferred_element_type=F32)


def _dot_nt(a, b):
    return lax.dot_general(a, b, (((1,), (1,)), ((), ())), preferred_element_type=F32)


def _dot_tn(a, b):
    return lax.dot_general(a, b, (((0,), (0,)), ((), ())), preferred_element_type=F32)


def _split3(v):
    hi = v.astype(BF16)
    r1 = v - hi.astype(F32)
    mid = r1.astype(BF16)
    lo = (r1 - mid.astype(F32)).astype(BF16)
    return hi, mid, lo


def _dot_exact_left(m01, v):
    hi, mid, lo = _split3(v)
    return _dot(m01, hi) + _dot(m01, mid) + _dot(m01, lo)


def _dot_exact_right(v, m01, terms=3):
    parts = _split3(v)[:terms]
    out = _dot(parts[0], m01)
    for p in parts[1:]:
        out = out + _dot(p, m01)
    return out


def _mm(name, M, N, tm, tn, lhs, rhs, pairs, e_fn, outs, *, nt=False, rows=(), vecs_n=(), sums=()):
    ni, nj = M // tm, N // tn
    assert ni * tm == M and nj * tn == N, (name, M, N, tm, tn)
    n_l, n_r, n_row, n_vn, n_o, n_s = len(lhs), len(rhs), len(rows), len(vecs_n), len(outs), len(sums)

    def body(*refs):
        pos = 0
        l_refs = refs[pos:pos + n_l]; pos += n_l
        r_refs = refs[pos:pos + n_r]; pos += n_r
        row_refs = refs[pos:pos + n_row]; pos += n_row
        vn_refs = refs[pos:pos + n_vn]; pos += n_vn
        o_refs = refs[pos:pos + n_o]; pos += n_o
        s_refs = refs[pos:pos + n_s]; pos += n_s
        i, j = pl.program_id(0), pl.program_id(1)
        accs = []
        for li, ri in pairs:
            accs.append(_dot_nt(l_refs[li][...], r_refs[ri][...]) if nt else _dot(l_refs[li][...], r_refs[ri][...]))
        out_vals, sum_vals = e_fn(accs, [r[...] for r in row_refs], [r[...] for r in vn_refs], j)
        for r, v in zip(o_refs, out_vals):
            r[...] = v.astype(r.dtype)
        if n_s:
            col = pl.multiple_of(j * tn, LANES)

            @pl.when(i == 0)
            def _():
                for r, v in zip(s_refs, sum_vals):
                    r[:, pl.ds(col, tn)] = v

            @pl.when(i > 0)
            def _():
                for r, v in zip(s_refs, sum_vals):
                    r[:, pl.ds(col, tn)] += v

    in_specs, args, est = [], [], 0
    for arr, width, cb in lhs:
        in_specs.append(pl.BlockSpec((tm, width), lambda i, j, cb=cb: (i, cb)))
        args.append(arr); est += tm * width * arr.dtype.itemsize
    for arr, off, *ksub in rhs:
        if nt:
            kb, kw = ksub if ksub else (0, arr.shape[1])
            in_specs.append(pl.BlockSpec((tn, kw), lambda i, j, off=off, kb=kb: (j + off, kb)))
            est += tn * kw * arr.dtype.itemsize
        else:
            in_specs.append(pl.BlockSpec((arr.shape[0], tn), lambda i, j, off=off: (0, j + off)))
            est += tn * arr.shape[0] * arr.dtype.itemsize
        args.append(arr)
    for arr, off in rows:
        in_specs.append(pl.BlockSpec((tm, tn), lambda i, j, off=off: (i, j + off)))
        args.append(arr); est += tm * tn * arr.dtype.itemsize
    for arr, off in vecs_n:
        in_specs.append(pl.BlockSpec((1, tn), lambda i, j, off=off: (0, j + off)))
        args.append(arr); est += 8 * tn * 4
    out_shape, out_specs = [], []
    for total, dtype, off in outs:
        out_shape.append(jax.ShapeDtypeStruct((M, total), dtype))
        out_specs.append(pl.BlockSpec((tm, tn), lambda i, j, off=off: (i, j + off)))
        est += tm * tn * jnp.dtype(dtype).itemsize
    for total in sums:
        out_shape.append(jax.ShapeDtypeStruct((1, total), F32))
        out_specs.append(pl.BlockSpec((1, total), lambda i, j: (0, 0)))
        est += 8 * total * 4
    vmem = 2 * est + (len(pairs) + 4) * tm * tn * 4 + (8 << 20)
    return pl.pallas_call(body, name=name, grid=(ni, nj), in_specs=in_specs, out_specs=out_specs, out_shape=out_shape,
                          compiler_params=_params(2, vmem))(*args)


def _mm_tn(name, a, g, ta, tn, ts, a_cols=None, a_off=0):
    S = a.shape[0]
    Ka = a.shape[1] if a_cols is None else a_cols
    N = g.shape[1]
    assert Ka % ta == 0 and N % tn == 0 and S % ts == 0, (name, Ka, N, S)
    aoff = a_off // ta

    def body(a_ref, g_ref, o_ref):
        s = pl.program_id(2)
        part = _dot_tn(a_ref[...], g_ref[...])

        @pl.when(s == 0)
        def _():
            o_ref[...] = part

        @pl.when(s > 0)
        def _():
            o_ref[...] += part

    vmem = 2 * (ts * ta * 2 + ts * tn * 2 + ta * tn * 4) + 2 * ta * tn * 4 + (8 << 20)
    return pl.pallas_call(
        body, name=name, grid=(Ka // ta, N // tn, S // ts),
        in_specs=[pl.BlockSpec((ts, ta), lambda ia, jn, s: (s, ia + aoff)), pl.BlockSpec((ts, tn), lambda ia, jn, s: (s, jn))],
        out_specs=pl.BlockSpec((ta, tn), lambda ia, jn, s: (ia, jn)),
        out_shape=jax.ShapeDtypeStruct((Ka, N), F32), compiler_params=_params(3, vmem))(a, g)


def _tri(n, upper):
    r = lax.broadcasted_iota(jnp.int32, (n, n), 0)
    c = lax.broadcasted_iota(jnp.int32, (n, n), 1)
    return jnp.where((c >= r) if upper else (c <= r), 1.0, 0.0).astype(BF16)


def _logsig(v):
    return jnp.minimum(v, 0.0) - jnp.log(1.0 + jnp.exp(-jnp.abs(v)))


def _cum_fwd(small, bvec, tb):
    S = small.shape[0]

    def body(x_ref, b_ref, o_ref, carry):
        i = pl.program_id(0)

        @pl.when(i == 0)
        def _():
            carry[...] = jnp.zeros_like(carry)

        logf = _logsig(x_ref[...] + b_ref[...])
        cum = _dot_exact_left(_tri(tb, False), logf) + carry[0:1, :]
        o_ref[...] = cum
        carry[0:1, :] = cum[tb - 1:tb, :]

    return pl.pallas_call(
        body, name="cum_fwd", grid=(S // tb,),
        in_specs=[pl.BlockSpec((tb, LANES), lambda i: (i, 0)), pl.BlockSpec((1, LANES), lambda i: (0, 0))],
        out_specs=pl.BlockSpec((tb, LANES), lambda i: (i, 0)), out_shape=jax.ShapeDtypeStruct((S, LANES), F32),
        scratch_shapes=[pltpu.VMEM((8, LANES), F32)], compiler_params=_params(1))(small, bvec)


def _cum_bwd(dcum_k, dcum_q, small, bvec, tb):
    S = small.shape[0]
    nb = S // tb

    def body(dk_ref, dq_ref, x_ref, b_ref, o_ref, s_ref, carry):
        i = pl.program_id(0)

        @pl.when(i == 0)
        def _():
            carry[...] = jnp.zeros_like(carry)
            s_ref[...] = jnp.zeros_like(s_ref)

        rc = _dot_exact_left(_tri(tb, True), dk_ref[...] + dq_ref[...]) + carry[0:1, :]
        dfl = rc * _sigmoid(-(x_ref[...] + b_ref[...]))
        o_ref[...] = dfl
        s_ref[...] += jnp.sum(dfl, axis=0, keepdims=True)
        carry[0:1, :] = rc[0:1, :]

    rev = lambda i: (nb - 1 - i, 0)
    return pl.pallas_call(
        body, name="cum_bwd", grid=(nb,),
        in_specs=[pl.BlockSpec((tb, LANES), rev)] * 3 + [pl.BlockSpec((1, LANES), lambda i: (0, 0))],
        out_specs=[pl.BlockSpec((tb, LANES), rev), pl.BlockSpec((1, LANES), lambda i: (0, 0))],
        out_shape=[jax.ShapeDtypeStruct((S, LANES), F32), jax.ShapeDtypeStruct((1, LANES), F32)],
        scratch_shapes=[pltpu.VMEM((8, LANES), F32)], compiler_params=_params(1))(dcum_k, dcum_q, small, bvec)


N_AUG = 3


def _lane():
    return lax.broadcasted_iota(jnp.int32, (1, LANES), 1)


def _lane_mask():
    return _lane() < ATT_HEAD_DIM


def _aug_base(h):
    return ATT_HEAD_DIM * (1 - h)


def _attn_prep(qkv, cum_cols, T):
    S = qkv.shape[0]
    HP = ATT_HEADS // 2

    def body(q_ref, k_ref, c_ref, qa_ref, ka_ref):
        lane = _lane()
        q = q_ref[...].astype(F32)
        k = k_ref[...].astype(F32)
        for h in (0, 1):
            base = _aug_base(h)
            own = (lane < ATT_HEAD_DIM) if h == 0 else (lane >= ATT_HEAD_DIM)
            terms = [t.astype(F32) for t in _split3(c_ref[0, :, h:h + 1])]
            qa = jnp.where(lane == base + N_AUG, 0.0, jnp.where((lane >= base) & (lane < base + N_AUG), 1.0, q))
            ka = jnp.where(lane == base + N_AUG, 1.0, jnp.where(own, k, 0.0))
            for t in range(N_AUG):
                ka = jnp.where(lane == base + t, -terms[t], ka)
            qa_ref[:, h * LANES:(h + 1) * LANES] = qa.astype(BF16)
            ka_ref[:, h * LANES:(h + 1) * LANES] = ka.astype(BF16)

    return pl.pallas_call(
        body, name="attn_prep", grid=(S // T, HP),
        in_specs=[pl.BlockSpec((T, LANES), lambda i, hp: (i, hp)), pl.BlockSpec((T, LANES), lambda i, hp: (i, HP + hp)),
                  pl.BlockSpec((1, T, 2), lambda i, hp: (hp, i, 0))],
        out_specs=[pl.BlockSpec((T, 2 * LANES), lambda i, hp: (i, hp))] * 2,
        out_shape=[jax.ShapeDtypeStruct((S, 2 * D_MODEL), BF16)] * 2, compiler_params=_params(2))(qkv, qkv, cum_cols)


def _attn_fwd(qa, ka, qkv, T):
    S = qkv.shape[0]
    nq = S // T
    HP = ATT_HEADS // 2

    def body(q0_ref, q1_ref, k0_ref, k1_ref, v_ref, o_ref, o32_ref, lse_ref):
        i = pl.program_id(1)
        qs = (q0_ref[...], q1_ref[...])
        k_refs = (k0_ref, k1_ref)
        row = lax.broadcasted_iota(jnp.int32, (T, T), 0)
        col = lax.broadcasted_iota(jnp.int32, (T, T), 1)
        head_rows = lax.broadcasted_iota(jnp.int32, (LANES, 1), 0) < ATT_HEAD_DIM

        def block(j, carry, diag):
            off = pl.multiple_of(j * T, T)
            vj = v_ref[pl.ds(off, T), :]
            m0, l0, m1, l1, acc = carry
            new, alphas, pvs = [], [], []
            for h, (m, l) in enumerate(((m0, l0), (m1, l1))):
                st = _dot_nt(k_refs[h][pl.ds(off, T), :], qs[h])
                if diag:
                    st = jnp.where(row <= col, st, NEG)
                m_new = jnp.maximum(m, jnp.max(st, axis=0, keepdims=True))
                p = jnp.exp(st - m_new)
                alpha = jnp.exp(m - m_new)
                l_new = alpha * l + jnp.sum(p, axis=0, keepdims=True)
                pvs.append(_dot_tn(vj, p.astype(BF16)))
                alphas.append(alpha)
                new += [m_new, l_new]
            acc = acc * jnp.where(head_rows, alphas[0], alphas[1]) + jnp.where(head_rows, pvs[0], pvs[1])
            return (new[0], new[1], new[2], new[3], acc)

        init = (jnp.full((1, T), NEG, F32), jnp.zeros((1, T), F32), jnp.full((1, T), NEG, F32), jnp.zeros((1, T), F32),
                jnp.zeros((LANES, T), F32))
        carry = lax.fori_loop(0, i // 2, lambda jj, c: block(2 * jj + 1, block(2 * jj, c, False), False), init)
        carry = lax.cond(i % 2 == 1, lambda c: block(i - 1, c, False), lambda c: c, carry)
        m0, l0, m1, l1, acc = block(i, carry, True)
        out = (acc / jnp.where(head_rows, l0, l1)).T
        o_ref[...] = out.astype(BF16)
        o32_ref[...] = out
        lse_ref[0, 0:1, :] = m0 + jnp.log(l0)
        lse_ref[0, 1:2, :] = m1 + jnp.log(l1)

    vmem = 2 * (2 * T * LANES * 2 + 3 * S * LANES * 2 + T * LANES * (2 + 4) + 8 * T * 4) + 10 * T * T * 4 + (8 << 20)
    qspec = lambda h: pl.BlockSpec((T, LANES), lambda hp, i, h=h: (i, 2 * hp + h))
    kspec = lambda h: pl.BlockSpec((S, LANES), lambda hp, i, h=h: (0, 2 * hp + h))
    return pl.pallas_call(
        body, name="attn_fwd", grid=(HP, nq),
        in_specs=[qspec(0), qspec(1), kspec(0), kspec(1), pl.BlockSpec((S, LANES), lambda hp, i: (0, 2 * HP + hp))],
        out_specs=[pl.BlockSpec((T, LANES), lambda hp, i: (i, hp)), pl.BlockSpec((T, LANES), lambda hp, i: (i, hp)),
                   pl.BlockSpec((1, 2, T), lambda hp, i: (hp, 0, i))],
        out_shape=[jax.ShapeDtypeStruct((S, D_MODEL), BF16), jax.ShapeDtypeStruct((S, D_MODEL), F32),
                   jax.ShapeDtypeStruct((HP, 2, S), F32)],
        compiler_params=_params(2, vmem))(qa, qa, ka, ka, qkv)


def _attn_stats(do, o32, lse_rows, T):
    S = do.shape[0]
    HP = ATT_HEADS // 2

    def body(do_ref, o_ref, lse_ref, st_ref):
        r = lax.broadcasted_iota(jnp.int32, (8, LANES), 0)
        lane = lax.broadcasted_iota(jnp.int32, (8, LANES), 1)
        sel = jnp.where(((r == 2) & (lane < ATT_HEAD_DIM)) | ((r == 3) & (lane >= ATT_HEAD_DIM)), 1.0, 0.0).astype(BF16)
        hi, mid, lo = _split3(do_ref[...].astype(F32) * o_ref[...])
        st_ref[0] = _dot_nt(sel, hi) + _dot_nt(sel, mid) + _dot_nt(sel, lo)
        st_ref[0, 0:2, :] = lse_ref[0]

    return pl.pallas_call(
        body, name="attn_stats", grid=(HP, S // T),
        in_specs=[pl.BlockSpec((T, LANES), lambda hp, i: (i, hp)), pl.BlockSpec((T, LANES), lambda hp, i: (i, hp)),
                  pl.BlockSpec((1, 2, T), lambda hp, i: (hp, 0, i))],
        out_specs=pl.BlockSpec((1, 8, T), lambda hp, i: (hp, 0, i)), out_shape=jax.ShapeDtypeStruct((HP, 8, S), F32),
        compiler_params=_params(2))(do, o32, lse_rows)


def _attn_bwd(qa, ka, qkv, do, stats_rows, T):
    S = qkv.shape[0]
    nq = S // T
    HP = ATT_HEADS // 2

    def body(k0_ref, k1_ref, v_ref, q0_ref, q1_ref, do_ref, st_ref, dq_ref, dk_ref, dv_ref, dck_ref, dcq_ref, dq_acc):
        j = pl.program_id(1)
        mA = _lane_mask()
        masks = (mA, jnp.logical_not(mA))
        q_refs = (q0_ref, q1_ref)

        @pl.when(j == 0)
        def _():
            dq_acc[...] = jnp.zeros_like(dq_acc)

        kas = (k0_ref[...], k1_ref[...])
        vj = v_ref[...]
        row = lax.broadcasted_iota(jnp.int32, (T, T), 0)
        col = lax.broadcasted_iota(jnp.int32, (T, T), 1)

        def block(i, carry, diag):
            dv_acc, dk0, dk1 = carry
            off = pl.multiple_of(i * T, T)
            doi = do_ref[pl.ds(off, T), :]
            zero = jnp.zeros_like(doi)
            dks = [dk0, dk1]
            for h in (0, 1):
                qh = q_refs[h][pl.ds(off, T), :]
                doh = jnp.where(masks[h], doi, zero)
                lse = st_ref[0, h:h + 1, pl.ds(off, T)]
                dd = st_ref[0, 2 + h:3 + h, pl.ds(off, T)]
                st = _dot_nt(kas[h], qh)
                if diag:
                    st = jnp.where(row <= col, st, NEG)
                pt = jnp.exp(st - lse)
                dpt = _dot_nt(vj, doh)
                dst = (pt * (dpt - dd)).astype(BF16)
                dv_acc = dv_acc + _dot(pt.astype(BF16), doh)
                dks[h] = dks[h] + _dot(dst, qh)
                dq_acc[h, pl.ds(off, T), :] += _dot_tn(dst, kas[h])
            return (dv_acc, dks[0], dks[1])

        z = jnp.zeros((T, LANES), F32)
        carry = block(j, (z, z, z), True)
        dv_acc, dk0, dk1 = lax.fori_loop(j + 1, nq, lambda i, c: block(i, c, False), carry)
        dv_ref[...] = dv_acc.astype(BF16)
        dk_ref[...] = jnp.where(mA, dk0, dk1).astype(BF16)
        ones_q = (_aug_base(0), _aug_base(1))
        dck_ref[0, :, 0:1] = -dk0[:, ones_q[0]:ones_q[0] + 1]
        dck_ref[0, :, 1:2] = -dk1[:, ones_q[1]:ones_q[1] + 1]

        @pl.when(j == nq - 1)
        def _():
            dq0, dq1 = dq_acc[0], dq_acc[1]
            ones_k = (_aug_base(0) + N_AUG, _aug_base(1) + N_AUG)
            dq_ref[...] = (jnp.where(mA, dq0, dq1) * (1.0 / math.sqrt(ATT_HEAD_DIM))).astype(BF16)
            dcq_ref[0, :, 0:1] = dq0[:, ones_k[0]:ones_k[0] + 1]
            dcq_ref[0, :, 1:2] = dq1[:, ones_k[1]:ones_k[1] + 1]

    vmem = (2 * (3 * T * LANES * 2 + 3 * S * LANES * 2 + 8 * S * 4 + S * LANES * (2 + 4) + 2 * T * LANES * 2 + T * LANES * 4)
            + 2 * S * LANES * 4 + 12 * T * T * 4 + (8 << 20))
    kspec = lambda h: pl.BlockSpec((T, LANES), lambda hp, j, h=h: (j, 2 * hp + h))
    qspec = lambda h: pl.BlockSpec((S, LANES), lambda hp, j, h=h: (0, 2 * hp + h))
    blk = pl.BlockSpec((T, LANES), lambda hp, j: (j, hp))
    full = pl.BlockSpec((S, LANES), lambda hp, j: (0, hp))
    return pl.pallas_call(
        body, name="attn_bwd", grid=(HP, nq),
        in_specs=[kspec(0), kspec(1), pl.BlockSpec((T, LANES), lambda hp, j: (j, 2 * HP + hp)), qspec(0), qspec(1), full,
                  pl.BlockSpec((1, 8, S), lambda hp, j: (hp, 0, 0))],
        out_specs=[full, blk, blk, pl.BlockSpec((1, T, 2), lambda hp, j: (hp, j, 0)),
                   pl.BlockSpec((1, S, 2), lambda hp, j: (hp, 0, 0))],
        out_shape=[jax.ShapeDtypeStruct((S, D_MODEL), BF16)] * 3 + [jax.ShapeDtypeStruct((HP, S, 2), F32)] * 2,
        scratch_shapes=[pltpu.VMEM((2, S, LANES), F32)], compiler_params=_params(2, vmem))(ka, ka, qkv, qa, qa, do, stats_rows)


HALO = 8


def _conv_fwd(u, w, b, ts, tc):
    S, C = u.shape
    hb = ts // HALO

    def body(u_ref, prev_ref, w_ref, b_ref, o_ref, ext):
        i = pl.program_id(0)
        ext[0:HALO, :] = jnp.where(i == 0, 0.0, prev_ref[...])
        ext[HALO:HALO + ts, :] = u_ref[...]
        acc = b_ref[...] + w_ref[3:4, :] * u_ref[...]
        for k in range(SSM_CONV - 1):
            d = SSM_CONV - 1 - k
            acc = acc + w_ref[k:k + 1, :] * ext[HALO - d:HALO - d + ts, :]
        o_ref[...] = acc * _sigmoid(acc)

    return pl.pallas_call(
        body, name="conv_fwd", grid=(S // ts, C // tc),
        in_specs=[pl.BlockSpec((ts, tc), lambda i, j: (i, j)),
                  pl.BlockSpec((HALO, tc), lambda i, j: (jnp.maximum(i * hb - 1, 0), j)),
                  pl.BlockSpec((SSM_CONV, tc), lambda i, j: (0, j)), pl.BlockSpec((1, tc), lambda i, j: (0, j))],
        out_specs=pl.BlockSpec((ts, tc), lambda i, j: (i, j)), out_shape=jax.ShapeDtypeStruct((S, C), F32),
        scratch_shapes=[pltpu.VMEM((ts + HALO, tc), F32)], compiler_params=_params(2))(u, u, w, b)


def _conv_bwd(name, u, dy, w, b, ts, tc, col0):
    S, C = dy.shape
    cb = col0 // tc
    assert cb * tc == col0
    hb = ts // HALO
    nb = S // ts
    E = ts + 2 * HALO

    def body(u_ref, uprev_ref, unext_ref, dy_ref, dynext_ref, w_ref, b_ref, du_ref, dw_ref, db_ref, uext, gext):
        i = pl.program_id(1)
        last = i == nb - 1
        uext[0:HALO, :] = jnp.where(i == 0, 0.0, uprev_ref[...])
        uext[HALO:HALO + ts, :] = u_ref[...]
        uext[HALO + ts:E, :] = unext_ref[...]
        n = ts + HALO
        pre = b_ref[...] + w_ref[3:4, :] * uext[HALO:HALO + n, :]
        for k in range(SSM_CONV - 1):
            d = SSM_CONV - 1 - k
            pre = pre + w_ref[k:k + 1, :] * uext[HALO - d:HALO - d + n, :]
        sg = _sigmoid(pre)
        dsilu = sg * (1.0 + pre * (1.0 - sg))
        gext[0:ts, :] = dy_ref[...] * dsilu[0:ts, :]
        gext[ts:n, :] = jnp.where(last, 0.0, dynext_ref[...] * dsilu[ts:n, :])
        g = gext[0:ts, :]
        du = w_ref[3:4, :] * g
        for k in range(SSM_CONV - 1):
            d = SSM_CONV - 1 - k
            du = du + w_ref[k:k + 1, :] * gext[d:d + ts, :]
        du_ref[...] = du.astype(du_ref.dtype)
        dws = [jnp.sum(g * uext[HALO - (SSM_CONV - 1 - k):HALO - (SSM_CONV - 1 - k) + ts, :], axis=0, keepdims=True)
               for k in range(SSM_CONV)]
        dbs = jnp.sum(g, axis=0, keepdims=True)

        @pl.when(i == 0)
        def _():
            for k in range(SSM_CONV):
                dw_ref[k:k + 1, :] = dws[k]
            db_ref[...] = dbs

        @pl.when(i > 0)
        def _():
            for k in range(SSM_CONV):
                dw_ref[k:k + 1, :] += dws[k]
            db_ref[...] += dbs

    nxt = lambda off: (lambda j, i: (jnp.minimum((i + 1) * hb, S // HALO - 1), j + off))
    return pl.pallas_call(
        body, name=name, grid=(C // tc, nb),
        in_specs=[pl.BlockSpec((ts, tc), lambda j, i: (i, j + cb)),
                  pl.BlockSpec((HALO, tc), lambda j, i: (jnp.maximum(i * hb - 1, 0), j + cb)),
                  pl.BlockSpec((HALO, tc), nxt(cb)),
                  pl.BlockSpec((ts, tc), lambda j, i: (i, j)),
                  pl.BlockSpec((HALO, tc), nxt(0)),
                  pl.BlockSpec((SSM_CONV, tc), lambda j, i: (0, j + cb)), pl.BlockSpec((1, tc), lambda j, i: (0, j + cb))],
        out_specs=[pl.BlockSpec((ts, tc), lambda j, i: (i, j)), pl.BlockSpec((SSM_CONV, tc), lambda j, i: (0, j)),
                   pl.BlockSpec((1, tc), lambda j, i: (0, j))],
        out_shape=[jax.ShapeDtypeStruct((S, C), BF16), jax.ShapeDtypeStruct((SSM_CONV, C), F32), jax.ShapeDtypeStruct((1, C), F32)],
        scratch_shapes=[pltpu.VMEM((E, tc), F32), pltpu.VMEM((ts + HALO, tc), F32)],
        compiler_params=_params(2))(u, u, u, dy, dy, w, b)


def _head_sum_matrix():
    r = jnp.right_shift(lax.broadcasted_iota(jnp.int32, (GROUP_LANES, GROUP_LANES), 0), 6)
    c = jnp.right_shift(lax.broadcasted_iota(jnp.int32, (GROUP_LANES, GROUP_LANES), 1), 6)
    return jnp.where(r == c, 1.0, 0.0).astype(BF16)


def _head_expand():
    r = lax.broadcasted_iota(jnp.int32, (8, GROUP_LANES), 0)
    c = jnp.right_shift(lax.broadcasted_iota(jnp.int32, (8, GROUP_LANES), 1), 6)
    return jnp.where(r == c, 1.0, 0.0).astype(BF16)


def _head_pick():
    lane = lax.broadcasted_iota(jnp.int32, (GROUP_LANES, 8), 0)
    r = lax.broadcasted_iota(jnp.int32, (GROUP_LANES, 8), 1)
    return jnp.where(lane == r * ATT_HEAD_DIM, 1.0, 0.0).astype(BF16)


def _ssd_common(dtc_ref, dtr_ref, bias_r, alog_b, bias_c, alog_c, L):
    a_b = -jnp.exp(alog_b)
    dt = _dot_exact_right(_softplus(dtc_ref[0] + bias_r), _head_expand())
    acum = _dot_exact_left(_tri(L, False), dt * a_b)
    a_c = -jnp.exp(alog_c)
    dtr = _softplus(dtr_ref[0] + bias_c)
    acum_r = _dot_exact_right(dtr * a_c, _tri(L, True))
    return a_b, dt, acum, acum_r


def _ssd_specs(L, nc, rev):
    cc = (lambda c: nc - 1 - c) if rev else (lambda c: c)
    G = SSM_GROUPS
    blk = pl.BlockSpec((L, GROUP_LANES), lambda g, c: (cc(c), g))
    dtc = pl.BlockSpec((1, L, 8), lambda g, c: (g, cc(c), 0))
    rowv = pl.BlockSpec((1, 1, 8), lambda g, c: (g, 0, 0))
    xs = blk
    bm = pl.BlockSpec((L, SSM_STATE), lambda g, c: (cc(c), SSM_INNER // SSM_STATE + g))
    cm = pl.BlockSpec((L, SSM_STATE), lambda g, c: (cc(c), SSM_INNER // SSM_STATE + G + g))
    dtr = pl.BlockSpec((1, 8, L), lambda g, c: (g, 0, cc(c)))
    vec = pl.BlockSpec((1, GROUP_LANES), lambda g, c: (0, g))
    colv = pl.BlockSpec((1, 8, 1), lambda g, c: (g, 0, 0))
    hs = pl.BlockSpec((1, 1, SSM_STATE, GROUP_LANES), lambda g, c: (g, cc(c), 0, 0))
    return blk, xs, bm, cm, dtc, dtr, vec, rowv, colv, hs


def _ssd_fwd(xbc, z, dtc, dtr, bias_r, alog_b, dskip_b, normw, bias_c, alog_c, L):
    S = z.shape[0]
    nc = S // L
    blk, xs, bm, cm, dtcs, dtrs, vec, rowv, colv, hs = _ssd_specs(L, nc, False)

    def body(x_ref, b_ref, c_ref, z_ref, dtc_ref, dtr_ref, bias_ref, alog_ref, dskip_ref, nw_ref, biasc_ref, alogc_ref,
             y_ref, ssm_ref, hs_ref, h_scr):
        c = pl.program_id(1)

        @pl.when(c == 0)
        def _():
            h_scr[...] = jnp.zeros_like(h_scr)

        mA = _lane_mask()
        a_b, dt, acum, acum_r = _ssd_common(dtc_ref, dtr_ref, bias_ref[0], alog_ref[...], biasc_ref[0], alogc_ref[0], L)
        x = x_ref[...]
        cb, bb = c_ref[...].astype(BF16), b_ref[...].astype(BF16)
        hprev = h_scr[...]
        hs_ref[0, 0] = hprev
        xdt = x * dt
        xdt_b = xdt.astype(BF16)
        gmat = _dot_nt(cb, bb)
        row = lax.broadcasted_iota(jnp.int32, (L, L), 0)
        col = lax.broadcasted_iota(jnp.int32, (L, L), 1)
        parts = []
        for p in range(GROUP_LANES // LANES):
            xp = xdt_b[:, p * LANES:(p + 1) * LANES]
            yd = []
            for hh in (0, 1):
                r = 2 * p + hh
                acol = acum[:, r * ATT_HEAD_DIM:r * ATT_HEAD_DIM + 1]
                arow = acum_r[r:r + 1, :]
                lm = jnp.exp(jnp.where(row >= col, acol - arow, NEG))
                yd.append(_dot((gmat * lm).astype(BF16), xp))
            parts.append(jnp.where(mA, yd[0], yd[1]))
        ydiag = jnp.concatenate(parts, axis=1)
        yoff = jnp.exp(acum) * _dot(cb, hprev.astype(BF16))
        y = ydiag + yoff + dskip_ref[...] * x
        aend = acum[L - 1:L, :]
        wgt = (jnp.exp(aend - acum) * xdt).astype(BF16)
        h_scr[...] = jnp.exp(aend) * hprev + _dot_tn(bb, wgt)
        y_ref[...] = y
        zz = z_ref[...]
        u = y * (zz * _sigmoid(zz))
        rs = lax.rsqrt(jnp.mean(u * u, axis=1, keepdims=True) + RMS_EPS)
        ssm_ref[...] = (u * rs * nw_ref[...]).astype(BF16)

    return pl.pallas_call(
        body, name="ssd_fwd", grid=(SSM_GROUPS, nc),
        in_specs=[xs, bm, cm, blk, dtcs, dtrs, rowv, vec, vec, vec, colv, colv],
        out_specs=[blk, blk, hs],
        out_shape=[jax.ShapeDtypeStruct((S, SSM_INNER), F32), jax.ShapeDtypeStruct((S, SSM_INNER), BF16),
                   jax.ShapeDtypeStruct((SSM_GROUPS, nc, SSM_STATE, GROUP_LANES), F32)],
        scratch_shapes=[pltpu.VMEM((SSM_STATE, GROUP_LANES), F32)],
        compiler_params=_params(2, 48 << 20))(xbc, xbc, xbc, z, dtc, dtr, bias_r, alog_b, dskip_b, normw, bias_c, alog_c)


def _ssd_bwd(xbc, z, y, dssm, hs_all, dtc, dtr, bias_r, alog_b, dskip_b, normw, bias_c, alog_c, L):
    S = z.shape[0]
    nc = S // L
    blk, xs, bm, cm, dtcs, dtrs, vec, rowv, colv, hs = _ssd_specs(L, nc, True)

    def body(x_ref, b_ref, c_ref, z_ref, y_ref, dssm_ref, hs_ref, dtc_ref, dtr_ref, bias_ref, alog_ref, dskip_ref, nw_ref,
             biasc_ref, alogc_ref,
             dx_ref, db_ref, dc_ref, dz_ref, ddt_ref, dnw_ref, ddskip_ref, dalog_ref, dbias_ref, dh_scr):
        c = pl.program_id(1)

        @pl.when(c == 0)
        def _():
            dh_scr[...] = jnp.zeros_like(dh_scr)

        mA = _lane_mask()
        masks = (mA, jnp.logical_not(mA))
        a_b, dt, acum, acum_r = _ssd_common(dtc_ref, dtr_ref, bias_ref[0], alog_ref[...], biasc_ref[0], alogc_ref[0], L)
        x, zz, y, dssm = x_ref[...], z_ref[...], y_ref[...], dssm_ref[...]
        cb, bb = c_ref[...].astype(BF16), b_ref[...].astype(BF16)
        hprev = hs_ref[0, 0]
        hb = hprev.astype(BF16)
        ds = dh_scr[...]
        dsb = ds.astype(BF16)
        dskip = dskip_ref[...]
        aend = acum[L - 1:L, :]
        e_a, e_end = jnp.exp(acum), jnp.exp(aend)
        dte = jnp.exp(aend - acum)
        xdt = x * dt
        xdt_b = xdt.astype(BF16)
        sg = _sigmoid(zz)
        sz = zz * sg
        u = y * sz
        rs = lax.rsqrt(jnp.mean(u * u, axis=1, keepdims=True) + RMS_EPS)
        un = u * rs
        dun = dssm * nw_ref[...]
        du = rs * (dun - un * jnp.mean(dun * un, axis=1, keepdims=True))
        dy = du * sz
        dz_ref[...] = (du * y * sg * (1.0 + zz * (1.0 - sg))).astype(dz_ref.dtype)
        dy_b = dy.astype(BF16)
        dch_b = (dy * e_a).astype(BF16)
        dc = _dot_nt(dch_b, hb)
        dhprev = _dot_tn(cb, dch_b)
        gt = _dot_nt(bb, cb)
        row = lax.broadcasted_iota(jnp.int32, (L, L), 0)
        col = lax.broadcasted_iota(jnp.int32, (L, L), 1)
        dgt = jnp.zeros((L, L), F32)
        parts = []
        for p in range(GROUP_LANES // LANES):
            xp = xdt_b[:, p * LANES:(p + 1) * LANES]
            dyp = dy_b[:, p * LANES:(p + 1) * LANES]
            zero = jnp.zeros_like(dyp)
            acc = None
            for hh in (0, 1):
                r = 2 * p + hh
                acol = acum[:, r * ATT_HEAD_DIM:r * ATT_HEAD_DIM + 1]
                arow = acum_r[r:r + 1, :]
                lmt = jnp.exp(jnp.where(row <= col, arow - acol, NEG))
                dyh = jnp.where(masks[hh], dyp, zero)
                part = _dot((gt * lmt).astype(BF16), dyh)
                acc = part if acc is None else acc + part
                dgt = dgt + _dot_nt(xp, dyh) * lmt
            parts.append(acc)
        dxdt_diag = jnp.concatenate(parts, axis=1)
        dgt_b = dgt.astype(BF16)
        db = _dot(dgt_b, cb)
        dc = dc + _dot_tn(dgt_b, bb)
        dxdt_state = dte * _dot(bb, dsb)
        db = db + _dot_nt((dte * xdt).astype(BF16), dsb)
        dxdt = dxdt_diag + dxdt_state
        dy_r, xdt_r = dy_b.astype(F32), xdt_b.astype(F32)
        dac = dy_r * (y - dskip * x) - xdt_r * dxdt
        tail = jnp.sum(xdt_r * dxdt_state, axis=0, keepdims=True) + e_end * jnp.sum(ds * hprev, axis=0, keepdims=True)
        rowl = lax.broadcasted_iota(jnp.int32, (L, 1), 0)
        dac = dac + jnp.where(rowl == L - 1, tail, 0.0)
        rc = _dot_exact_left(_tri(L, True), dac)
        hsum = _head_sum_matrix()
        hs1 = _dot_exact_right(dxdt * x, hsum, 2)
        hs2 = _dot_exact_right(rc, hsum, 2)
        ddt = hs1 + a_b * hs2
        ddtraw = _dot_exact_right(ddt, _head_pick(), 2) * _sigmoid(dtc_ref[0] + bias_ref[0])
        dx_ref[...] = dskip * dy + dxdt * dt
        db_ref[...] = db
        dc_ref[...] = dc
        ddt_ref[0] = ddtraw
        dh_scr[...] = e_end * ds + dhprev
        sums = (jnp.sum(dssm * un, axis=0, keepdims=True), jnp.sum(dy * x, axis=0, keepdims=True),
                a_b * jnp.sum(hs2 * dt, axis=0, keepdims=True))
        refs = (dnw_ref, ddskip_ref, dalog_ref)
        dbias = jnp.sum(ddtraw, axis=0, keepdims=True)

        @pl.when(c == 0)
        def _():
            for r, v in zip(refs, sums):
                r[...] = v
            dbias_ref[0] = dbias

        @pl.when(c > 0)
        def _():
            for r, v in zip(refs, sums):
                r[...] += v
            dbias_ref[0] += dbias

    nbc = pl.BlockSpec((L, SSM_STATE), lambda g, c: (nc - 1 - c, g))
    return pl.pallas_call(
        body, name="ssd_bwd", grid=(SSM_GROUPS, nc),
        in_specs=[xs, bm, cm, blk, blk, blk, hs, dtcs, dtrs, rowv, vec, vec, vec, colv, colv],
        out_specs=[blk, nbc, nbc, blk, dtcs, vec, vec, vec, rowv],
        out_shape=[jax.ShapeDtypeStruct((S, SSM_INNER), F32), jax.ShapeDtypeStruct((S, SSM_GROUPS * SSM_STATE), F32),
                   jax.ShapeDtypeStruct((S, SSM_GROUPS * SSM_STATE), F32), jax.ShapeDtypeStruct((S, SSM_INNER), BF16),
                   jax.ShapeDtypeStruct((SSM_GROUPS, S, 8), F32)] + [jax.ShapeDtypeStruct((1, SSM_INNER), F32)] * 3
                  + [jax.ShapeDtypeStruct((SSM_GROUPS, 1, 8), F32)],
        scratch_shapes=[pltpu.VMEM((SSM_STATE, GROUP_LANES), F32)],
        compiler_params=_params(2, 56 << 20))(xbc, xbc, xbc, z, y, dssm, hs_all, dtc, dtr, bias_r, alog_b, dskip_b, normw,
                                              bias_c, alog_c)


def _place():
    return lax.axis_index("x"), lax.axis_index("y"), lax.axis_index("c")


def _other_chips(x, y):
    return [(1 - x, y), (x, 1 - y), (1 - x, 1 - y)]


def _half_rows(rows, which):
    hr = rows // 2
    if isinstance(which, int):
        return pl.ds(which * hr, hr)
    return pl.ds(pl.multiple_of(which * hr, 8), hr)


def _chip_gather(name, shards, split):
    n = len(shards)
    ANY = pl.BlockSpec(memory_space=pl.ANY)

    def body(*refs):
        ins, outs = refs[:n], refs[n:2 * n]
        send, recv, fsend, frecv = refs[2 * n:]
        x, y, c = _place()
        me = 2 * x + y
        sibling = (x, y, 1 - c)
        chips = _other_chips(x, y)

        def piece(a, chip_idx, which):
            if split[a]:
                return outs[a].at[chip_idx, _half_rows(shards[a].shape[0], which)]
            return outs[a].at[chip_idx]

        def ici(k, a, to_chip, src_chip):
            src = ins[a].at[_half_rows(shards[a].shape[0], c)] if split[a] else ins[a]
            return pltpu.make_async_remote_copy(src_ref=src, dst_ref=piece(a, src_chip, c), send_sem=send.at[k, a],
                                                recv_sem=recv.at[k, a], device_id=(*to_chip, c), device_id_type=MESH)

        def fwd(k, a, src_chip, which):
            return pltpu.make_async_remote_copy(src_ref=piece(a, src_chip, which), dst_ref=piece(a, src_chip, which),
                                                send_sem=fsend.at[k, a], recv_sem=frecv.at[k, a], device_id=sibling,
                                                device_id_type=MESH)

        sends = []
        for k, chip in enumerate(chips):
            for a in range(n):
                cp = ici(k, a, chip, me)
                cp.start()
                sends.append(cp)
        for k, (ox, oy) in enumerate(chips):
            src = 2 * ox + oy
            for a in range(n):
                ici(k, a, (ox, oy), src).wait_recv()
                if split[a]:
                    cp = fwd(k, a, src, c)
                    cp.start()
                    sends.append(cp)
        for k, (ox, oy) in enumerate(chips):
            for a in range(n):
                if split[a]:
                    fwd(k, a, 2 * ox + oy, 1 - c).wait_recv()
        for cp in sends:
            cp.wait_send()

    sem = pltpu.SemaphoreType.DMA((3, n))
    return pl.pallas_call(
        body, name=name, in_specs=[ANY] * n, out_specs=[ANY] * n,
        out_shape=[jax.ShapeDtypeStruct((4,) + s.shape, s.dtype) for s in shards],
        scratch_shapes=[sem, sem, sem, sem])(*shards)


def _chip_scatter(name, blocks):
    n = len(blocks)
    ANY = pl.BlockSpec(memory_space=pl.ANY)

    def body(*refs):
        ins, outs = refs[:n], refs[n:2 * n]
        send, recv, loc = refs[2 * n:]
        x, y, c = _place()
        me = 2 * x + y
        local = [pltpu.make_async_copy(ins[a].at[me], outs[a].at[me], loc.at[a]) for a in range(n)]
        for cp in local:
            cp.start()
        sends = []
        for k, (ox, oy) in enumerate(_other_chips(x, y)):
            dst_chip = 2 * ox + oy
            for a in range(n):
                cp = pltpu.make_async_remote_copy(src_ref=ins[a].at[dst_chip], dst_ref=outs[a].at[me], send_sem=send.at[k, a],
                                                  recv_sem=recv.at[k, a], device_id=(ox, oy, c), device_id_type=MESH)
                cp.start()
                sends.append(cp)
        for k, (ox, oy) in enumerate(_other_chips(x, y)):
            src = 2 * ox + oy
            for a in range(n):
                pltpu.make_async_remote_copy(src_ref=ins[a].at[me], dst_ref=outs[a].at[src], send_sem=send.at[k, a],
                                             recv_sem=recv.at[k, a], device_id=(ox, oy, c), device_id_type=MESH).wait_recv()
        for cp in sends:
            cp.wait_send()
        for cp in local:
            cp.wait()

    return pl.pallas_call(
        body, name=name, in_specs=[ANY] * n, out_specs=[ANY] * n,
        out_shape=[jax.ShapeDtypeStruct(b.shape, b.dtype) for b in blocks],
        scratch_shapes=[pltpu.SemaphoreType.DMA((3, n)), pltpu.SemaphoreType.DMA((3, n)), pltpu.SemaphoreType.DMA((n,))],
    )(*blocks)


def _half_to_sibling(name, blocks):
    n = len(blocks)
    ANY = pl.BlockSpec(memory_space=pl.ANY)

    def body(*refs):
        ins, outs = refs[:n], refs[n:2 * n]
        send, recv = refs[2 * n:]
        x, y, c = _place()
        cps = [pltpu.make_async_remote_copy(src_ref=ins[a].at[:, _half_rows(blocks[a].shape[1], 1 - c)], dst_ref=outs[a],
                                            send_sem=send.at[a], recv_sem=recv.at[a], device_id=(x, y, 1 - c),
                                            device_id_type=MESH) for a in range(n)]
        for cp in cps:
            cp.start()
        for cp in cps:
            cp.wait_recv()
        for cp in cps:
            cp.wait_send()

    return pl.pallas_call(
        body, name=name, in_specs=[ANY] * n, out_specs=[ANY] * n,
        out_shape=[jax.ShapeDtypeStruct((4, b.shape[1] // 2, b.shape[2]), b.dtype) for b in blocks],
        scratch_shapes=[pltpu.SemaphoreType.DMA((n,)), pltpu.SemaphoreType.DMA((n,))])(*blocks)


def _sibling_swap(name, arrs):
    n = len(arrs)
    ANY = pl.BlockSpec(memory_space=pl.ANY)

    def body(*refs):
        ins, outs = refs[:n], refs[n:2 * n]
        send, recv = refs[2 * n:]
        x, y, c = _place()
        cps = [pltpu.make_async_remote_copy(src_ref=ins[a], dst_ref=outs[a], send_sem=send.at[a], recv_sem=recv.at[a],
                                            device_id=(x, y, 1 - c), device_id_type=MESH) for a in range(n)]
        for cp in cps:
            cp.start()
        for cp in cps:
            cp.wait_recv()
        for cp in cps:
            cp.wait_send()

    return pl.pallas_call(
        body, name=name, in_specs=[ANY] * n, out_specs=[ANY] * n,
        out_shape=[jax.ShapeDtypeStruct(a.shape, a.dtype) for a in arrs],
        scratch_shapes=[pltpu.SemaphoreType.DMA((n,)), pltpu.SemaphoreType.DMA((n,))])(*arrs)


N_DEV = 8


def _all_sum_small(vec):
    P = vec.shape[1]

    def body(v_ref, o_ref, buf, send, recv):
        x, y, c = _place()
        me = 4 * x + 2 * y + c
        buf[me] = v_ref[...]

        def peer(r):
            return ((1 - x) if (r >> 2) & 1 else x, (1 - y) if (r >> 1) & 1 else y, (1 - c) if r & 1 else c)

        sends = []
        for r in range(1, N_DEV):
            cp = pltpu.make_async_remote_copy(src_ref=v_ref, dst_ref=buf.at[me], send_sem=send.at[r], recv_sem=recv.at[r],
                                              device_id=peer(r), device_id_type=MESH)
            cp.start()
            sends.append(cp)
        for r in range(1, N_DEV):
            px, py, pc = peer(r)
            pltpu.make_async_remote_copy(src_ref=v_ref, dst_ref=buf.at[4 * px + 2 * py + pc], send_sem=send.at[r],
                                         recv_sem=recv.at[r], device_id=(px, py, pc), device_id_type=MESH).wait_recv()
        for cp in sends:
            cp.wait_send()
        tot = buf[0]
        for d in range(1, N_DEV):
            tot = tot + buf[d]
        o_ref[...] = tot

    return pl.pallas_call(
        body, name="all_sum_small", in_specs=[pl.BlockSpec(memory_space=pltpu.VMEM)],
        out_specs=pl.BlockSpec(memory_space=pltpu.VMEM), out_shape=jax.ShapeDtypeStruct((1, P), F32),
        scratch_shapes=[pltpu.VMEM((N_DEV, 1, P), F32), pltpu.SemaphoreType.DMA((N_DEV,)), pltpu.SemaphoreType.DMA((N_DEV,))],
    )(vec)


def _half_sum(name, blocks, theirs, core, tr):
    _, R, C = blocks.shape
    hr = R // 2
    nb = hr // tr
    assert nb * tr == hr

    def body(c_ref, a_ref, b_ref, o_ref):
        o_ref[...] = (a_ref[...] + b_ref[...]).astype(BF16)

    grid_spec = pltpu.PrefetchScalarGridSpec(
        num_scalar_prefetch=1, grid=(4, nb),
        in_specs=[pl.BlockSpec((1, tr, C), lambda b, i, c_ref: (b, c_ref[0] * nb + i, 0)),
                  pl.BlockSpec((1, tr, C), lambda b, i, c_ref: (b, i, 0))],
        out_specs=pl.BlockSpec((1, tr, C), lambda b, i, c_ref: (b, i, 0)))
    return pl.pallas_call(body, name=name, grid_spec=grid_spec, out_shape=jax.ShapeDtypeStruct((4, hr, C), BF16),
                          compiler_params=_params(2, 40 << 20))(core, blocks, theirs)


def _sum4(name, stack, tr):
    _, R, C = stack.shape

    def body(s_ref, o_ref):
        s = s_ref[...].astype(F32)
        o_ref[...] = ((s[0] + s[1]) + s[2]) + s[3]

    return pl.pallas_call(body, name=name, grid=(R // tr,), in_specs=[pl.BlockSpec((4, tr, C), lambda i: (0, i, 0))],
                          out_specs=pl.BlockSpec((tr, C), lambda i: (i, 0)), out_shape=jax.ShapeDtypeStruct((R, C), F32),
                          compiler_params=_params(1, 40 << 20))(stack)


def _adamw_math(w, m, v, g):
    c1 = 1.0 - ADAM_B1 ** ADAM_STEP
    c2 = 1.0 - ADAM_B2 ** ADAM_STEP
    nm = ADAM_B1 * m + (1.0 - ADAM_B1) * g
    nv = ADAM_B2 * v + (1.0 - ADAM_B2) * (g * g)
    return -ADAM_LR * ((nm / c1) / (jnp.sqrt(nv / c2) + ADAM_EPS) + ADAM_WD * w), nm, nv


def _adamw(name, w, m, v, g, tr):
    R, C = w.shape

    def body(w_ref, m_ref, v_ref, ga_ref, g_ref, d_ref, nm_ref, nv_ref):
        g = ga_ref[...]
        g_ref[...] = g
        d_ref[...], nm_ref[...], nv_ref[...] = _adamw_math(w_ref[...], m_ref[...], v_ref[...], g)

    spec = pl.BlockSpec((tr, C), lambda i: (i, 0))
    return pl.pallas_call(body, name=name, grid=(R // tr,), in_specs=[spec] * 4, out_specs=[spec] * 4,
                          out_shape=[jax.ShapeDtypeStruct((R, C), F32)] * 4, compiler_params=_params(1, 40 << 20))(w, m, v, g)


def _adamw_halves(name, w, m, v, mine, theirs, core, tr):
    _, R, C = w.shape
    nb = (R // 2) // tr
    assert 2 * nb * tr == R

    def body(c_ref, w_ref, m_ref, v_ref, a_ref, b_ref, g_ref, d_ref, nm_ref, nv_ref):
        g = jnp.where((pl.program_id(0) // nb) == c_ref[0], a_ref[...], b_ref[...])
        g_ref[0] = g
        d_ref[0], nm_ref[0], nv_ref[0] = _adamw_math(w_ref[0], m_ref[0], v_ref[0], g)

    spec = pl.BlockSpec((1, tr, C), lambda i, c_ref: (0, i, 0))
    half = lambda own: pl.BlockSpec((tr, C), lambda i, c_ref, own=own: (
        jnp.clip(i - (c_ref[0] if own else 1 - c_ref[0]) * nb, 0, nb - 1), 0))
    grid_spec = pltpu.PrefetchScalarGridSpec(num_scalar_prefetch=1, grid=(R // tr,),
                                             in_specs=[spec, spec, spec, half(True), half(False)], out_specs=[spec] * 4)
    return pl.pallas_call(body, name=name, grid_spec=grid_spec, out_shape=[jax.ShapeDtypeStruct((1, R, C), F32)] * 4,
                          compiler_params=_params(1, 40 << 20))(core, w, m, v, mine, theirs)


def _row_tile(rows, cols, budget_bytes=1 << 20, mult=8):
    best = None
    for t in range(mult, rows + 1, mult):
        if rows % t == 0 and t * cols * 4 <= budget_bytes:
            best = t
    return best if best is not None else rows


def _ln_fwd(r, g, b):
    mu = jnp.mean(r, axis=1, keepdims=True)
    xc = r - mu
    rstd = lax.rsqrt(jnp.mean(xc * xc, axis=1, keepdims=True) + LN_EPS)
    xhat = xc * rstd
    return xhat * g + b, xhat, rstd


def _ln_bwd(dy, xhat, rstd, g):
    dxh = dy * g
    return rstd * (dxh - jnp.mean(dxh, axis=1, keepdims=True) - xhat * jnp.mean(dxh * xhat, axis=1, keepdims=True))


def _to_chip_blocks_cols(a):
    R, C4 = a.shape
    return a.reshape(R, 4, C4 // 4).transpose(1, 0, 2)


def _from_chip_blocks_cols(a):
    return a.transpose(1, 0, 2).reshape(a.shape[1], 4 * a.shape[2])


def kernel(x, w_in, b_forget, conv_w, conv_b, dt_bias, a_log, d_skip, ssm_norm_w, w_proj_attn, w_proj_ssm, b_gates, w_out, ln1_g, ln1_b, w_ffn_gate, w_ffn_up, w_ffn_down, ln2_g, ln2_b, loss_target, m_w_in, m_b_forget, m_conv_w, m_conv_b, m_dt_bias, m_a_log, m_d_skip, m_ssm_norm_w, m_w_proj_attn, m_w_proj_ssm, m_b_gates, m_w_out, m_ln1_g, m_ln1_b, m_w_ffn_gate, m_w_ffn_up, m_w_ffn_down, m_ln2_g, m_ln2_b, v_w_in, v_b_forget, v_conv_w, v_conv_b, v_dt_bias, v_a_log, v_d_skip, v_ssm_norm_w, v_w_proj_attn, v_w_proj_ssm, v_b_gates, v_w_out, v_ln1_g, v_ln1_b, v_w_ffn_gate, v_w_ffn_up, v_w_ffn_down, v_ln2_g, v_ln2_b):
    S = x.shape[1]
    D = D_MODEL
    TM, TM2, TA, LC, TS, TB = (min(TILES[k], S) for k in ("TM", "TM2", "TA", "LC", "TS", "TB"))
    xf = x[0]
    tgt = loss_target[0]
    xb = xf.astype(BF16)

    shards = [w_in[0].astype(BF16), conv_w[0], w_proj_attn[0].astype(BF16), w_proj_ssm[0].astype(BF16), w_out[0].astype(BF16),
              w_ffn_gate[0].astype(BF16), w_ffn_up[0].astype(BF16), w_ffn_down[0].astype(BF16)]
    chip = 2 * lax.axis_index("x") + lax.axis_index("y")
    gathered = _chip_gather("gather_weights", shards, [True, False, True, True, True, True, True, True])
    g_in, g_cw, g_pa, g_ps, g_out, g_fg, g_fu, g_fd = (lax.dynamic_update_slice(g, sh[None], (chip, 0, 0))
                                                       for g, sh in zip(gathered, shards))
    w_full = _from_chip_blocks_cols(g_in)
    w_re = jnp.concatenate([w_full[:, 0:3072], w_full[:, 3088:5136], w_full[:, 5136:8208], w_full[:, 8240:10288],
                            w_full[:, 3072:3088], w_full[:, 8208:8240], jnp.zeros((D, 80), BF16)], axis=1)
    conv_w_full = _from_chip_blocks_cols(g_cw)
    wpa, wps, wout = g_pa.reshape(D, D), g_ps.reshape(SSM_INNER, D), g_out.reshape(D, D)
    wfg, wfu, wfd = _from_chip_blocks_cols(g_fg), _from_chip_blocks_cols(g_fu), g_fd.reshape(FFN_HIDDEN, D)

    def plain(accs, rows, vecs, j):
        return [accs[0]], []

    def q_scaled(accs, rows, vecs, j):
        return [accs[0] * jnp.where(j * 512 < D, 1.0 / math.sqrt(ATT_HEAD_DIM), 1.0)], []

    qkv, = _mm("proj_qkv", S, 3072, TM, 512, [(xb, D, 0)], [(w_re, 0)], [(0, 0)], q_scaled, [(3072, BF16, 0)])
    z, = _mm("proj_z", S, 2048, TM, 512, [(xb, D, 0)], [(w_re, RE_Z // 512)], [(0, 0)], plain, [(2048, F32, 0)])
    xbc_raw, = _mm("proj_xbc", S, 3072, TM, 512, [(xb, D, 0)], [(w_re, RE_XBC // 512)], [(0, 0)], plain, [(3072, F32, 0)])
    gl, = _mm("proj_gate", S, 2048, TM, 512, [(xb, D, 0)], [(w_re, RE_GATE // 512)], [(0, 0)], plain, [(2048, F32, 0)])
    small, = _mm("proj_small", S, 128, TM, 128, [(xb, D, 0)], [(w_re, RE_SMALL // 128)], [(0, 0)], plain, [(128, F32, 0)])

    bvec = jnp.concatenate([b_forget, jnp.zeros((1, LANES - ATT_HEADS), F32)], axis=1)
    cum = _cum_fwd(small, bvec, TB)[:, :ATT_HEADS]
    cum_cols = cum.reshape(S, 8, 2).transpose(1, 0, 2)
    qa, ka = _attn_prep(qkv, cum_cols, TM)
    o, o32, lse_rows = _attn_fwd(qa, ka, qkv, TA)

    cb_row = conv_b
    xbc = _conv_fwd(xbc_raw, conv_w_full, cb_row, min(512, S), 512)
    dt_raw = small[:, 16:48]
    dtc = dt_raw.reshape(S, SSM_GROUPS, 8).transpose(1, 0, 2)
    dtr = dt_raw.T.reshape(SSM_GROUPS, 8, S)
    bias_r = dt_bias.reshape(SSM_GROUPS, 1, 8)
    alog_b = jnp.repeat(a_log, ATT_HEAD_DIM, axis=1)
    dskip_b = jnp.repeat(d_skip, ATT_HEAD_DIM, axis=1)
    bias_c = dt_bias.reshape(SSM_GROUPS, 8, 1)
    alog_c = a_log.reshape(SSM_GROUPS, 8, 1)
    y_ssd, ssm, hs_all = _ssd_fwd(xbc, z, dtc, dtr, bias_r, alog_b, dskip_b, ssm_norm_w, bias_c, alog_c, LC)

    def merge(accs, rows, vecs, j):
        g0, g1 = _sigmoid(rows[0] + vecs[0]), _sigmoid(rows[1] + vecs[1])
        return [g0 * accs[0] + g1 * accs[1], accs[0], accs[1]], []

    mix, attn_d, ssm_d = _mm("merge", S, D, TM, 512, [(o, D, 0), (ssm, SSM_INNER, 0)], [(wpa, 0), (wps, 0)], [(0, 0), (1, 1)],
                             merge, [(D, BF16, 0), (D, F32, 0), (D, F32, 0)], rows=[(gl, 0), (gl, 2)],
                             vecs_n=[(b_gates, 0), (b_gates, 2)])

    def out_ln1(accs, rows, vecs, j):
        r1 = ALPHA * rows[0] + accs[0]
        h1, _, _ = _ln_fwd(r1, vecs[0], vecs[1])
        return [r1, h1, h1], []

    r1, h1, h1b = _mm("out_ln1", S, D, TM2, D, [(mix, D, 0)], [(wout, 0)], [(0, 0)], out_ln1,
                      [(D, F32, 0), (D, F32, 0), (D, BF16, 0)], rows=[(xf, 0)], vecs_n=[(ln1_g, 0), (ln1_b, 0)])

    FT = FFN_HIDDEN // 2

    def swiglu(accs, rows, vecs, j):
        g, u = accs
        return [g, u, g * _sigmoid(g) * u], []

    gate, up, hmid = _mm("ffn_up", S, FFN_HIDDEN, TM2, FT, [(h1b, D, 0)], [(wfg, 0), (wfu, 0)], [(0, 0), (0, 1)], swiglu,
                         [(FFN_HIDDEN, F32, 0), (FFN_HIDDEN, F32, 0), (FFN_HIDDEN, BF16, 0)])

    def down_ln2_loss(accs, rows, vecs, j):
        r2 = ALPHA * rows[0] + accs[0]
        yv, xhat, rstd = _ln_fwd(r2, vecs[0], vecs[1])
        diff = yv - rows[1]
        dy = diff * (1.0 / D_MODEL)
        dr2 = _ln_bwd(dy, xhat, rstd, vecs[0])
        return [dr2, dr2], [jnp.sum(dy * xhat, axis=0, keepdims=True), jnp.sum(dy, axis=0, keepdims=True),
                            (0.5 / D_MODEL) * jnp.sum(diff * diff, axis=0, keepdims=True)]

    dr2, dr2b, dln2_g, dln2_b, loss_lanes = _mm("ffn_down_ln2", S, D, TM2, D, [(hmid, FFN_HIDDEN, 0)], [(wfd, 0)], [(0, 0)],
                                               down_ln2_loss, [(D, F32, 0), (D, BF16, 0)], rows=[(h1, 0), (tgt, 0)],
                                               vecs_n=[(ln2_g, 0), (ln2_b, 0)], sums=[D, D, D])
    loss = lax.psum(jnp.sum(loss_lanes), ("x", "y", "c"))

    def dswiglu(accs, rows, vecs, j):
        g, u = rows
        sg = _sigmoid(g)
        return [accs[0] * u * sg * (1.0 + g * (1.0 - sg)), accs[0] * g * sg], []

    dgate, dup = _mm("ffn_down_bwd", S, FFN_HIDDEN, TM2, FT, [(dr2b, D, 0)], [(wfd, 0)], [(0, 0)], dswiglu,
                     [(FFN_HIDDEN, BF16, 0), (FFN_HIDDEN, BF16, 0)], nt=True, rows=[(gate, 0), (up, 0)])
    dwfd = _mm_tn("dw_ffn_down", hmid, dr2b, FFN_HIDDEN // 2, D, TS)
    dwfg = _mm_tn("dw_ffn_gate", h1b, dgate, D, FT, TS)
    dwfu = _mm_tn("dw_ffn_up", h1b, dup, D, FT, TS)

    def dh1_ln1(accs, rows, vecs, j):
        dh1 = ALPHA * rows[0] + accs[0] + accs[1]
        _, xhat, rstd = _ln_fwd(rows[1], vecs[0], vecs[0])
        dr1 = _ln_bwd(dh1, xhat, rstd, vecs[0])
        return [dr1, dr1], [jnp.sum(dh1 * xhat, axis=0, keepdims=True), jnp.sum(dh1, axis=0, keepdims=True)]

    dr1, dr1b, dln1_g, dln1_b = _mm("ffn_up_bwd_ln1", S, D, TM2, D, [(dgate, FFN_HIDDEN, 0), (dup, FFN_HIDDEN, 0)],
                                    [(wfg, 0), (wfu, 0)], [(0, 0), (1, 1)], dh1_ln1, [(D, F32, 0), (D, BF16, 0)], nt=True,
                                    rows=[(dr2, 0), (r1, 0)], vecs_n=[(ln1_g, 0)], sums=[D, D])

    def dmerge(accs, rows, vecs, j):
        dmix = accs[0]
        g0, g1 = _sigmoid(rows[0] + vecs[0]), _sigmoid(rows[1] + vecs[1])
        dgl0 = dmix * rows[2] * g0 * (1.0 - g0)
        dgl1 = dmix * rows[3] * g1 * (1.0 - g1)
        return [dmix * g0, dmix * g1, dgl0, dgl1], [jnp.sum(dgl0, axis=0, keepdims=True), jnp.sum(dgl1, axis=0, keepdims=True)]

    d_attn_d, d_ssm_d, dgl0, dgl1, dbg0, dbg1 = _mm(
        "out_bwd", S, D, TM, 512, [(dr1b, D, 0)], [(wout, 0)], [(0, 0)], dmerge, [(D, BF16, 0)] * 4, nt=True,
        rows=[(gl, 0), (gl, 2), (attn_d, 0), (ssm_d, 0)], vecs_n=[(b_gates, 0), (b_gates, 2)], sums=[D, D])
    dwout = _mm_tn("dw_out", mix, dr1b, D, D, TS)
    dwpa = _mm_tn("dw_proj_attn", o, d_attn_d, D, D, TS)
    dwps = _mm_tn("dw_proj_ssm", ssm, d_ssm_d, D, D, TS)

    do, = _mm("proj_attn_bwd", S, D, TM, 512, [(d_attn_d, D, 0)], [(wpa, 0)], [(0, 0)], plain, [(D, BF16, 0)], nt=True)
    stats_rows = _attn_stats(do, o32, lse_rows, TA)
    dq, dk, dv, dck, dcq = _attn_bwd(qa, ka, qkv, do, stats_rows, TA)

    def per_head(a):
        a = a.transpose(1, 0, 2).reshape(S, ATT_HEADS)
        return jnp.concatenate([a, jnp.zeros((S, LANES - ATT_HEADS), F32)], axis=1)

    dfl, dbf = _cum_bwd(per_head(dck), per_head(dcq), small, bvec, TB)

    dssm, = _mm("proj_ssm_bwd", S, SSM_INNER, TM, 512, [(d_ssm_d, D, 0)], [(wps, 0)], [(0, 0)], plain, [(SSM_INNER, F32, 0)],
                nt=True)
    dxs, dbm, dcm, dz, ddt8, dnw, ddskip_b, dalog_b, dbias8 = _ssd_bwd(
        xbc, z, y_ssd, dssm, hs_all, dtc, dtr, bias_r, alog_b, dskip_b, ssm_norm_w, bias_c, alog_c, LC)
    TC = min(512, S)
    du_x, dcw_x, dcb_x = _conv_bwd("conv_bwd_x", xbc_raw, dxs, conv_w_full, cb_row, TC, 512, 0)
    du_b, dcw_b, dcb_b = _conv_bwd("conv_bwd_b", xbc_raw, dbm, conv_w_full, cb_row, TC, 512, SSM_INNER)
    du_c, dcw_c, dcb_c = _conv_bwd("conv_bwd_c", xbc_raw, dcm, conv_w_full, cb_row, TC, 512, SSM_INNER + SSM_GROUPS * SSM_STATE)
    dconv_w = jnp.concatenate([dcw_x, dcw_b, dcw_c], axis=1)
    dconv_b = jnp.concatenate([dcb_x, dcb_b, dcb_c], axis=1)
    ddt_raw = ddt8.transpose(1, 0, 2).reshape(S, SSM_HEADS)

    dsmall = jnp.concatenate([dfl[:, :ATT_HEADS], ddt_raw, jnp.zeros((S, 80), F32)], axis=1).astype(BF16)
    def dx_first(accs, rows, vecs, j):
        return [ALPHA * rows[0] + sum(accs[1:], accs[0])], []

    def dx_more(accs, rows, vecs, j):
        return [rows[0] + sum(accs[1:], accs[0])], []

    wk = lambda col, width=D: (w_re, 0, col // width, width)
    dx_part, = _mm("dx_a", S, D, TM2, D, [(dq, D, 0), (dk, D, 0), (dv, D, 0), (dz, D, 0), (dz, D, 1)],
                   [wk(0), wk(1024), wk(2048), wk(RE_Z), wk(RE_Z + 1024)], [(k, k) for k in range(5)], dx_first,
                   [(D, F32, 0)], nt=True, rows=[(dr1, 0)])
    HB = SSM_GROUPS * SSM_STATE
    grad_x, = _mm("dx_b", S, D, TM2, D,
                  [(du_x, D, 0), (du_x, D, 1), (du_b, HB, 0), (du_c, HB, 0), (dgl0, D, 0), (dgl1, D, 0), (dsmall, LANES, 0)],
                  [wk(RE_XBC), wk(RE_XBC + 1024), wk(RE_XBC + 2048, HB), wk(RE_XBC + 2048 + HB, HB), wk(RE_GATE),
                   wk(RE_GATE + 1024), wk(RE_SMALL, LANES)],
                  [(k, k) for k in range(7)], dx_more, [(D, F32, 0)], nt=True, rows=[(dx_part, 0)])
    dw_q, dw_k, dw_v = (_mm_tn("dw_in_" + nm, xb, g_, D, D, TS) for nm, g_ in (("q", dq), ("k", dk), ("v", dv)))
    dw_z = _mm_tn("dw_in_z", xb, dz, D, D, TS)
    dw_xbc = jnp.concatenate([_mm_tn("dw_in_xs", xb, du_x, D, D, TS), _mm_tn("dw_in_b", xb, du_b, D, HB, TS),
                              _mm_tn("dw_in_c", xb, du_c, D, HB, TS)], axis=1)
    dw_g0, dw_g1 = _mm_tn("dw_in_g0", xb, dgl0, D, D, TS), _mm_tn("dw_in_g1", xb, dgl1, D, D, TS)
    dw_s = _mm_tn("dw_in_small", xb, dsmall, D, LANES, TS)
    dw_full = jnp.concatenate([dw_q, dw_k, dw_v, dw_s[:, 0:ATT_HEADS], dw_z, dw_xbc, dw_s[:, ATT_HEADS:ATT_HEADS + SSM_HEADS],
                               dw_g0, dw_g1], axis=1)

    blocks = [_to_chip_blocks_cols(dw_full), dwpa.reshape(4, D // 4, D), dwps.reshape(4, SSM_INNER // 4, D),
              dwout.reshape(4, D // 4, D), _to_chip_blocks_cols(dwfg), _to_chip_blocks_cols(dwfu),
              dwfd.reshape(4, FFN_HIDDEN // 4, D)]
    names = ["w_in", "w_proj_attn", "w_proj_ssm", "w_out", "w_ffn_gate", "w_ffn_up", "w_ffn_down"]
    core = lax.axis_index("c").astype(jnp.int32).reshape(1)
    theirs = _half_to_sibling("swap_halves", blocks)
    halves = [_half_sum("halfsum_" + nm, b, t, core, _row_tile(b.shape[1] // 2, b.shape[2], mult=16))
              for nm, b, t in zip(names, blocks, theirs)]
    stacks = _chip_scatter("scatter_grads", halves)
    reduced = [_sum4("sum_" + nm, st, _row_tile(st.shape[1], st.shape[2], mult=16)) for nm, st in zip(names, stacks)]
    other = _sibling_swap("swap_reduced", reduced)
    big_w = [w_in, w_proj_attn, w_proj_ssm, w_out, w_ffn_gate, w_ffn_up, w_ffn_down]
    big_m = [m_w_in, m_w_proj_attn, m_w_proj_ssm, m_w_out, m_w_ffn_gate, m_w_ffn_up, m_w_ffn_down]
    big_v = [v_w_in, v_w_proj_attn, v_w_proj_ssm, v_w_out, v_w_ffn_gate, v_w_ffn_up, v_w_ffn_down]
    big = {}
    for nm, w_, m_, v_, mine, theirs in zip(names, big_w, big_m, big_v, reduced, other):
        big[nm] = _adamw_halves("adamw_" + nm, w_, m_, v_, mine, theirs, core, _row_tile(w_.shape[1] // 2, w_.shape[2]))

    pick = lambda a: a[:, ::ATT_HEAD_DIM]
    dd_skip = ddskip_b.reshape(1, SSM_HEADS, ATT_HEAD_DIM).sum(axis=2)
    pieces = [dbf[:, :ATT_HEADS], dconv_w.reshape(1, SSM_CONV * SSM_CONV_DIM), dconv_b, dbias8.reshape(1, SSM_HEADS), pick(dalog_b), dd_skip,
              dnw, dbg0, dbg1, dln1_g, dln1_b, dln2_g, dln2_b]
    widths = [p.shape[1] for p in pieces]
    total = sum(widths)
    P = -(-total // LANES) * LANES
    packed = jnp.concatenate(pieces + [jnp.zeros((1, P - total), F32)], axis=1)
    summed = _all_sum_small(packed)
    offs = [0]
    for wd in widths:
        offs.append(offs[-1] + wd)
    sm = [summed[:, offs[k]:offs[k + 1]] for k in range(len(pieces))]
    g_bf, g_cw_full, g_cb, g_dtb, g_al, g_ds, g_nw = sm[0], sm[1].reshape(SSM_CONV, SSM_CONV_DIM), sm[2], sm[3], sm[4], sm[5], sm[6]
    g_bg = jnp.concatenate([sm[7], sm[8]], axis=1)
    g_l1g, g_l1b, g_l2g, g_l2b = sm[9], sm[10], sm[11], sm[12]
    cshard = SSM_CONV_DIM // 4
    g_cw_shard = lax.dynamic_slice_in_dim(g_cw_full, chip * cshard, cshard, axis=1)
    small_names = ["b_forget", "conv_w", "conv_b", "dt_bias", "a_log", "d_skip", "ssm_norm_w", "b_gates", "ln1_g", "ln1_b",
                   "ln2_g", "ln2_b"]
    small_g = [g_bf, g_cw_shard.reshape(1, -1), g_cb, g_dtb, g_al, g_ds, g_nw, g_bg, g_l1g, g_l1b, g_l2g, g_l2b]
    small_w = [b_forget, conv_w[0].reshape(1, -1), conv_b, dt_bias, a_log, d_skip, ssm_norm_w, b_gates, ln1_g, ln1_b, ln2_g, ln2_b]
    small_m = [m_b_forget, m_conv_w[0].reshape(1, -1), m_conv_b, m_dt_bias, m_a_log, m_d_skip, m_ssm_norm_w, m_b_gates, m_ln1_g,
               m_ln1_b, m_ln2_g, m_ln2_b]
    small_v = [v_b_forget, v_conv_w[0].reshape(1, -1), v_conv_b, v_dt_bias, v_a_log, v_d_skip, v_ssm_norm_w, v_b_gates, v_ln1_g,
               v_ln1_b, v_ln2_g, v_ln2_b]
    sw = [a.shape[1] for a in small_w]
    stot = sum(sw)
    SP = -(-stot // LANES) * LANES

    def pack(parts):
        return jnp.concatenate(list(parts) + [jnp.zeros((1, SP - stot), F32)], axis=1).reshape(SP // LANES, LANES)

    sres = _adamw("adamw_small", pack(small_w), pack(small_m), pack(small_v), pack(small_g), SP // LANES)
    soffs = [0]
    for wd in sw:
        soffs.append(soffs[-1] + wd)
    smalls = {}
    for k, nm in enumerate(small_names):
        vals = [r.reshape(1, SP)[:, soffs[k]:soffs[k + 1]] for r in sres]
        if nm == "conv_w":
            vals = [v_.reshape(1, SSM_CONV, cshard) for v_ in vals]
        smalls[nm] = vals

    order = ["w_in", "b_forget", "conv_w", "conv_b", "dt_bias", "a_log", "d_skip", "ssm_norm_w", "w_proj_attn", "w_proj_ssm",
             "b_gates", "w_out", "ln1_g", "ln1_b", "w_ffn_gate", "w_ffn_up", "w_ffn_down", "ln2_g", "ln2_b"]
    allres = {**big, **smalls}
    outs = [loss, grad_x[None]]
    for idx in range(4):
        outs += [allres[nm][idx] for nm in order]
    return tuple(outs)
```

```python
import functools
import math

import jax
import jax.numpy as jnp
from jax import lax
from jax.experimental import pallas as pl
from jax.experimental.pallas import tpu as pltpu

F32, BF16 = jnp.float32, jnp.bfloat16
MESH = pl.DeviceIdType.MESH

D_MODEL = 1024
ATT_HEADS, ATT_HEAD_DIM = 16, 64
SSM_INNER, SSM_HEADS, SSM_GROUPS, SSM_STATE, SSM_CONV = 2048, 32, 4, 128, 4
SSM_CONV_DIM = SSM_INNER + 2 * SSM_GROUPS * SSM_STATE
GROUP_LANES = SSM_INNER // SSM_GROUPS
FFN_HIDDEN = 2816
ALPHA = 2.0 ** 0.25
LN_EPS = 1e-5
RMS_EPS = 1e-5
ADAM_LR, ADAM_B1, ADAM_B2, ADAM_EPS, ADAM_WD, ADAM_STEP = 0.001, 0.9, 0.999, 1e-08, 0.01, 10
IN_SIZES = (1024, 1024, 1024, 16, 2048, 3072, 32, 2048)
IN_WIDTH = sum(IN_SIZES)
RE_WIDTH = 3072 + 2048 + 3072 + 2048 + 128
RE_Z, RE_XBC, RE_GATE, RE_SMALL = 3072, 5120, 8192, 10240

LANES = 128
VMEM_CAP = 60 * 1024 * 1024
NEG = -1e30
TILES = dict(TM=1024, TM2=256, TA=512, AQF=2048, LC=256, TS=2048, TB=256)


def _params(n_axes, vmem_bytes=None):
    return pltpu.CompilerParams(dimension_semantics=("arbitrary",) * n_axes,
                                vmem_limit_bytes=None if vmem_bytes is None else int(min(vmem_bytes, VMEM_CAP)))


def _sigmoid(v):
    return 1.0 / (1.0 + jnp.exp(-v))


def _softplus(v):
    return jnp.maximum(v, 0.0) + jnp.log(1.0 + jnp.exp(-jnp.abs(v)))


def _dot(a, b):
    return lax.dot_general(a, b, (((1,), (0,)), ((), ())), preferred_element_type=F32)


def _dot_nt(a, b):
    return lax.dot_general(a, b, (((1,), (1,)), ((), ())), preferred_element_type=F32)


def _dot_tn(a, b):
    return lax.dot_general(a, b, (((0,), (0,)), ((), ())), preferred_element_type=F32)


def _split3(v):
    hi = v.astype(BF16)
    r1 = v - hi.astype(F32)
    mid = r1.astype(BF16)
    lo = (r1 - mid.astype(F32)).astype(BF16)
    return hi, mid, lo


def _dot_exact_left(m01, v):
    hi, mid, lo = _split3(v)
    return _dot(m01, hi) + _dot(m01, mid) + _dot(m01, lo)


def _dot_exact_right(v, m01, terms=3):
    parts = _split3(v)[:terms]
    out = _dot(parts[0], m01)
    for p in parts[1:]:
        out = out + _dot(p, m01)
    return out


def _mm(name, M, N, tm, tn, lhs, rhs, pairs, e_fn, outs, *, nt=False, rows=(), vecs_n=(), sums=()):
    ni, nj = M // tm, N // tn
    assert ni * tm == M and nj * tn == N, (name, M, N, tm, tn)
    n_l, n_r, n_row, n_vn, n_o, n_s = len(lhs), len(rhs), len(rows), len(vecs_n), len(outs), len(sums)

    def body(*refs):
        pos = 0
        l_refs = refs[pos:pos + n_l]; pos += n_l
        r_refs = refs[pos:pos + n_r]; pos += n_r
        row_refs = refs[pos:pos + n_row]; pos += n_row
        vn_refs = refs[pos:pos + n_vn]; pos += n_vn
        o_refs = refs[pos:pos + n_o]; pos += n_o
        s_refs = refs[pos:pos + n_s]; pos += n_s
        i, j = pl.program_id(0), pl.program_id(1)
        accs = []
        for li, ri in pairs:
            accs.append(_dot_nt(l_refs[li][...], r_refs[ri][...]) if nt else _dot(l_refs[li][...], r_refs[ri][...]))
        out_vals, sum_vals = e_fn(accs, [r[...] for r in row_refs], [r[...] for r in vn_refs], j)
        for r, v in zip(o_refs, out_vals):
            r[...] = v.astype(r.dtype)
        if n_s:
            col = pl.multiple_of(j * tn, LANES)

            @pl.when(i == 0)
            def _():
                for r, v in zip(s_refs, sum_vals):
                    r[:, pl.ds(col, tn)] = v

            @pl.when(i > 0)
            def _():
                for r, v in zip(s_refs, sum_vals):
                    r[:, pl.ds(col, tn)] += v

    in_specs, args, est = [], [], 0
    for arr, width, cb in lhs:
        in_specs.append(pl.BlockSpec((tm, width), lambda i, j, cb=cb: (i, cb)))
        args.append(arr); est += tm * width * arr.dtype.itemsize
    for arr, off, *ksub in rhs:
        if nt:
            kb, kw = ksub if ksub else (0, arr.shape[1])
            in_specs.append(pl.BlockSpec((tn, kw), lambda i, j, off=off, kb=kb: (j + off, kb)))
            est += tn * kw * arr.dtype.itemsize
        else:
            in_specs.append(pl.BlockSpec((arr.shape[0], tn), lambda i, j, off=off: (0, j + off)))
            est += tn * arr.shape[0] * arr.dtype.itemsize
        args.append(arr)
    for arr, off in rows:
        in_specs.append(pl.BlockSpec((tm, tn), lambda i, j, off=off: (i, j + off)))
        args.append(arr); est += tm * tn * arr.dtype.itemsize
    for arr, off in vecs_n:
        in_specs.append(pl.BlockSpec((1, tn), lambda i, j, off=off: (0, j + off)))
        args.append(arr); est += 8 * tn * 4
    out_shape, out_specs = [], []
    for total, dtype, off in outs:
        out_shape.append(jax.ShapeDtypeStruct((M, total), dtype))
        out_specs.append(pl.BlockSpec((tm, tn), lambda i, j, off=off: (i, j + off)))
        est += tm * tn * jnp.dtype(dtype).itemsize
    for total in sums:
        out_shape.append(jax.ShapeDtypeStruct((1, total), F32))
        out_specs.append(pl.BlockSpec((1, total), lambda i, j: (0, 0)))
        est += 8 * total * 4
    vmem = 2 * est + (len(pairs) + 4) * tm * tn * 4 + (8 << 20)
    return pl.pallas_call(body, name=name, grid=(ni, nj), in_specs=in_specs, out_specs=out_specs, out_shape=out_shape,
                          compiler_params=_params(2, vmem))(*args)


def _mm_tn(name, a, g, ta, tn, ts, a_cols=None, a_off=0):
    S = a.shape[0]
    Ka = a.shape[1] if a_cols is None else a_cols
    N = g.shape[1]
    assert Ka % ta == 0 and N % tn == 0 and S % ts == 0, (name, Ka, N, S)
    aoff = a_off // ta

    def body(a_ref, g_ref, o_ref):
        s = pl.program_id(2)
        part = _dot_tn(a_ref[...], g_ref[...])

        @pl.when(s == 0)
        def _():
            o_ref[...] = part

        @pl.when(s > 0)
        def _():
            o_ref[...] += part

    vmem = 2 * (ts * ta * 2 + ts * tn * 2 + ta * tn * 4) + 2 * ta * tn * 4 + (8 << 20)
    return pl.pallas_call(
        body, name=name, grid=(Ka // ta, N // tn, S // ts),
        in_specs=[pl.BlockSpec((ts, ta), lambda ia, jn, s: (s, ia + aoff)), pl.BlockSpec((ts, tn), lambda ia, jn, s: (s, jn))],
        out_specs=pl.BlockSpec((ta, tn), lambda ia, jn, s: (ia, jn)),
        out_shape=jax.ShapeDtypeStruct((Ka, N), F32), compiler_params=_params(3, vmem))(a, g)


def _tri(n, upper):
    r = lax.broadcasted_iota(jnp.int32, (n, n), 0)
    c = lax.broadcasted_iota(jnp.int32, (n, n), 1)
    return jnp.where((c >= r) if upper else (c <= r), 1.0, 0.0).astype(BF16)


def _logsig(v):
    return jnp.minimum(v, 0.0) - jnp.log(1.0 + jnp.exp(-jnp.abs(v)))


def _cum_fwd(small, bvec, tb):
    S = small.shape[0]

    def body(x_ref, b_ref, o_ref, carry):
        i = pl.program_id(0)

        @pl.when(i == 0)
        def _():
            carry[...] = jnp.zeros_like(carry)

        logf = _logsig(x_ref[...] + b_ref[...])
        cum = _dot_exact_left(_tri(tb, False), logf) + carry[0:1, :]
        o_ref[...] = cum
        carry[0:1, :] = cum[tb - 1:tb, :]

    return pl.pallas_call(
        body, name="cum_fwd", grid=(S // tb,),
        in_specs=[pl.BlockSpec((tb, LANES), lambda i: (i, 0)), pl.BlockSpec((1, LANES), lambda i: (0, 0))],
        out_specs=pl.BlockSpec((tb, LANES), lambda i: (i, 0)), out_shape=jax.ShapeDtypeStruct((S, LANES), F32),
        scratch_shapes=[pltpu.VMEM((8, LANES), F32)], compiler_params=_params(1))(small, bvec)


def _cum_bwd(dcum_k, dcum_q, small, bvec, tb):
    S = small.shape[0]
    nb = S // tb

    def body(dk_ref, dq_ref, x_ref, b_ref, o_ref, s_ref, carry):
        i = pl.program_id(0)

        @pl.when(i == 0)
        def _():
            carry[...] = jnp.zeros_like(carry)
            s_ref[...] = jnp.zeros_like(s_ref)

        rc = _dot_exact_left(_tri(tb, True), dk_ref[...] + dq_ref[...]) + carry[0:1, :]
        dfl = rc * _sigmoid(-(x_ref[...] + b_ref[...]))
        o_ref[...] = dfl
        s_ref[...] += jnp.sum(dfl, axis=0, keepdims=True)
        carry[0:1, :] = rc[0:1, :]

    rev = lambda i: (nb - 1 - i, 0)
    return pl.pallas_call(
        body, name="cum_bwd", grid=(nb,),
        in_specs=[pl.BlockSpec((tb, LANES), rev)] * 3 + [pl.BlockSpec((1, LANES), lambda i: (0, 0))],
        out_specs=[pl.BlockSpec((tb, LANES), rev), pl.BlockSpec((1, LANES), lambda i: (0, 0))],
        out_shape=[jax.ShapeDtypeStruct((S, LANES), F32), jax.ShapeDtypeStruct((1, LANES), F32)],
        scratch_shapes=[pltpu.VMEM((8, LANES), F32)], compiler_params=_params(1))(dcum_k, dcum_q, small, bvec)


N_AUG = 3


def _lane():
    return lax.broadcasted_iota(jnp.int32, (1, LANES), 1)


def _lane_mask():
    return _lane() < ATT_HEAD_DIM


def _aug_base(h):
    return ATT_HEAD_DIM * (1 - h)


def _attn_prep(qkv, cum_cols, T):
    S = qkv.shape[0]
    HP = ATT_HEADS // 2

    def body(q_ref, k_ref, c_ref, qa_ref, ka_ref):
        lane = _lane()
        q = q_ref[...].astype(F32)
        k = k_ref[...].astype(F32)
        for h in (0, 1):
            base = _aug_base(h)
            own = (lane < ATT_HEAD_DIM) if h == 0 else (lane >= ATT_HEAD_DIM)
            terms = [t.astype(F32) for t in _split3(c_ref[0, :, h:h + 1])]
            qa = jnp.where(lane == base + N_AUG, 0.0, jnp.where((lane >= base) & (lane < base + N_AUG), 1.0, q))
            ka = jnp.where(lane == base + N_AUG, 1.0, jnp.where(own, k, 0.0))
            for t in range(N_AUG):
                ka = jnp.where(lane == base + t, -terms[t], ka)
            qa_ref[:, h * LANES:(h + 1) * LANES] = qa.astype(BF16)
            ka_ref[:, h * LANES:(h + 1) * LANES] = ka.astype(BF16)

    return pl.pallas_call(
        body, name="attn_prep", grid=(S // T, HP),
        in_specs=[pl.BlockSpec((T, LANES), lambda i, hp: (i, hp)), pl.BlockSpec((T, LANES), lambda i, hp: (i, HP + hp)),
                  pl.BlockSpec((1, T, 2), lambda i, hp: (hp, i, 0))],
        out_specs=[pl.BlockSpec((T, 2 * LANES), lambda i, hp: (i, hp))] * 2,
        out_shape=[jax.ShapeDtypeStruct((S, 2 * D_MODEL), BF16)] * 2, compiler_params=_params(2))(qkv, qkv, cum_cols)


def _attn_fwd(qa, ka, qkv, T, TK):
    S = qkv.shape[0]
    nq = S // T
    r = T // TK
    HP = ATT_HEADS // 2

    def body(q0_ref, q1_ref, k0_ref, k1_ref, v_ref, o_ref, o32_ref, lse_ref):
        i = pl.program_id(1)
        qs = (q0_ref[...], q1_ref[...])
        k_refs = (k0_ref, k1_ref)
        row = lax.broadcasted_iota(jnp.int32, (TK, T), 0)
        col = lax.broadcasted_iota(jnp.int32, (TK, T), 1)
        head_rows = lax.broadcasted_iota(jnp.int32, (LANES, 1), 0) < ATT_HEAD_DIM

        def block(j, carry, q0):
            off = pl.multiple_of(j * TK, TK)
            vj = v_ref[pl.ds(off, TK), :]
            full = q0 is None
            q0 = 0 if full else q0
            m0, l0, m1, l1, acc = carry
            new, alphas, pvs = [], [], []
            for h, (m, l) in enumerate(((m0, l0), (m1, l1))):
                st = _dot_nt(k_refs[h][pl.ds(off, TK), :], qs[h][q0:, :])
                if not full:
                    st = jnp.where(row[:, :T - q0] <= col[:, :T - q0], st, NEG)
                m_old, l_old = m[:, q0:], l[:, q0:]
                m_new = jnp.maximum(m_old, jnp.max(st, axis=0, keepdims=True))
                p = jnp.exp(st - m_new)
                alpha = jnp.exp(m_old - m_new)
                l_new = alpha * l_old + jnp.sum(p, axis=0, keepdims=True)
                pvs.append(_dot_tn(vj, p.astype(BF16)))
                alphas.append(alpha)
                new += [m_new, l_new]
            part = acc[:, q0:] * jnp.where(head_rows, alphas[0], alphas[1]) + jnp.where(head_rows, pvs[0], pvs[1])
            if q0:
                keep = lambda old, upd: jnp.concatenate([old[:, :q0], upd], axis=1)
                return (keep(m0, new[0]), keep(l0, new[1]), keep(m1, new[2]), keep(l1, new[3]), keep(acc, part))
            return (new[0], new[1], new[2], new[3], part)

        init = (jnp.full((1, T), NEG, F32), jnp.zeros((1, T), F32), jnp.full((1, T), NEG, F32), jnp.zeros((1, T), F32),
                jnp.zeros((LANES, T), F32))
        n_full = i * r
        carry = lax.fori_loop(0, n_full // 2, lambda jj, c: block(2 * jj + 1, block(2 * jj, c, None), None), init)
        carry = lax.cond(n_full % 2 == 1, lambda c: block(n_full - 1, c, None), lambda c: c, carry)
        for d in range(r):
            carry = block(n_full + d, carry, d * TK)
        m0, l0, m1, l1, acc = carry
        out = (acc / jnp.where(head_rows, l0, l1)).T
        o_ref[...] = out.astype(BF16)
        o32_ref[...] = out
        lse_ref[0, 0:1, :] = m0 + jnp.log(l0)
        lse_ref[0, 1:2, :] = m1 + jnp.log(l1)

    vmem = 2 * (2 * T * LANES * 2 + 3 * S * LANES * 2 + T * LANES * (2 + 4) + 8 * T * 4) + 10 * T * TK * 4 + (8 << 20)
    qspec = lambda h: pl.BlockSpec((T, LANES), lambda hp, i, h=h: (i, 2 * hp + h))
    kspec = lambda h: pl.BlockSpec((S, LANES), lambda hp, i, h=h: (0, 2 * hp + h))
    return pl.pallas_call(
        body, name="attn_fwd", grid=(HP, nq),
        in_specs=[qspec(0), qspec(1), kspec(0), kspec(1), pl.BlockSpec((S, LANES), lambda hp, i: (0, 2 * HP + hp))],
        out_specs=[pl.BlockSpec((T, LANES), lambda hp, i: (i, hp)), pl.BlockSpec((T, LANES), lambda hp, i: (i, hp)),
                   pl.BlockSpec((1, 2, T), lambda hp, i: (hp, 0, i))],
        out_shape=[jax.ShapeDtypeStruct((S, D_MODEL), BF16), jax.ShapeDtypeStruct((S, D_MODEL), F32),
                   jax.ShapeDtypeStruct((HP, 2, S), F32)],
        compiler_params=_params(2, vmem))(qa, qa, ka, ka, qkv)


def _attn_stats(do, o32, lse_rows, T):
    S = do.shape[0]
    HP = ATT_HEADS // 2

    def body(do_ref, o_ref, lse_ref, st_ref):
        r = lax.broadcasted_iota(jnp.int32, (8, LANES), 0)
        lane = lax.broadcasted_iota(jnp.int32, (8, LANES), 1)
        sel = jnp.where(((r == 2) & (lane < ATT_HEAD_DIM)) | ((r == 3) & (lane >= ATT_HEAD_DIM)), 1.0, 0.0).astype(BF16)
        hi, mid, lo = _split3(do_ref[...].astype(F32) * o_ref[...])
        st_ref[0] = _dot_nt(sel, hi) + _dot_nt(sel, mid) + _dot_nt(sel, lo)
        st_ref[0, 0:2, :] = lse_ref[0]

    return pl.pallas_call(
        body, name="attn_stats", grid=(HP, S // T),
        in_specs=[pl.BlockSpec((T, LANES), lambda hp, i: (i, hp)), pl.BlockSpec((T, LANES), lambda hp, i: (i, hp)),
                  pl.BlockSpec((1, 2, T), lambda hp, i: (hp, 0, i))],
        out_specs=pl.BlockSpec((1, 8, T), lambda hp, i: (hp, 0, i)), out_shape=jax.ShapeDtypeStruct((HP, 8, S), F32),
        compiler_params=_params(2))(do, o32, lse_rows)


def _attn_bwd(qa, ka, qkv, do, stats_rows, T):
    S = qkv.shape[0]
    nq = S // T
    HP = ATT_HEADS // 2

    def body(k0_ref, k1_ref, v_ref, q0_ref, q1_ref, do_ref, st_ref, dq_ref, dk_ref, dv_ref, dck_ref, dcq_ref, dq_acc):
        j = pl.program_id(1)
        mA = _lane_mask()
        masks = (mA, jnp.logical_not(mA))
        q_refs = (q0_ref, q1_ref)

        @pl.when(j == 0)
        def _():
            dq_acc[...] = jnp.zeros_like(dq_acc)

        kas = (k0_ref[...], k1_ref[...])
        vj = v_ref[...]
        row = lax.broadcasted_iota(jnp.int32, (T, T), 0)
        col = lax.broadcasted_iota(jnp.int32, (T, T), 1)

        def block(i, carry, diag):
            dv_acc, dk0, dk1 = carry
            off = pl.multiple_of(i * T, T)
            doi = do_ref[pl.ds(off, T), :]
            zero = jnp.zeros_like(doi)
            dks = [dk0, dk1]
            for h in (0, 1):
                qh = q_refs[h][pl.ds(off, T), :]
                doh = jnp.where(masks[h], doi, zero)
                lse = st_ref[0, h:h + 1, pl.ds(off, T)]
                dd = st_ref[0, 2 + h:3 + h, pl.ds(off, T)]
                st = _dot_nt(kas[h], qh)
                if diag:
                    st = jnp.where(row <= col, st, NEG)
                pt = jnp.exp(st - lse)
                dpt = _dot_nt(vj, doh)
                dst = (pt * (dpt - dd)).astype(BF16)
                dv_acc = dv_acc + _dot(pt.astype(BF16), doh)
                dks[h] = dks[h] + _dot(dst, qh)
                dq_acc[h, pl.ds(off, T), :] += _dot_tn(dst, kas[h])
            return (dv_acc, dks[0], dks[1])

        z = jnp.zeros((T, LANES), F32)
        carry = block(j, (z, z, z), True)
        dv_acc, dk0, dk1 = lax.fori_loop(j + 1, nq, lambda i, c: block(i, c, False), carry)
        dv_ref[...] = dv_acc.astype(BF16)
        dk_ref[...] = jnp.where(mA, dk0, dk1).astype(BF16)
        ones_q = (_aug_base(0), _aug_base(1))
        dck_ref[0, :, 0:1] = -dk0[:, ones_q[0]:ones_q[0] + 1]
        dck_ref[0, :, 1:2] = -dk1[:, ones_q[1]:ones_q[1] + 1]

        @pl.when(j == nq - 1)
        def _():
            dq0, dq1 = dq_acc[0], dq_acc[1]
            ones_k = (_aug_base(0) + N_AUG, _aug_base(1) + N_AUG)
            dq_ref[...] = (jnp.where(mA, dq0, dq1) * (1.0 / math.sqrt(ATT_HEAD_DIM))).astype(BF16)
            dcq_ref[0, :, 0:1] = dq0[:, ones_k[0]:ones_k[0] + 1]
            dcq_ref[0, :, 1:2] = dq1[:, ones_k[1]:ones_k[1] + 1]

    vmem = (2 * (3 * T * LANES * 2 + 3 * S * LANES * 2 + 8 * S * 4 + S * LANES * (2 + 4) + 2 * T * LANES * 2 + T * LANES * 4)
            + 2 * S * LANES * 4 + 12 * T * T * 4 + (8 << 20))
    kspec = lambda h: pl.BlockSpec((T, LANES), lambda hp, j, h=h: (j, 2 * hp + h))
    qspec = lambda h: pl.BlockSpec((S, LANES), lambda hp, j, h=h: (0, 2 * hp + h))
    blk = pl.BlockSpec((T, LANES), lambda hp, j: (j, hp))
    full = pl.BlockSpec((S, LANES), lambda hp, j: (0, hp))
    return pl.pallas_call(
        body, name="attn_bwd", grid=(HP, nq),
        in_specs=[kspec(0), kspec(1), pl.BlockSpec((T, LANES), lambda hp, j: (j, 2 * HP + hp)), qspec(0), qspec(1), full,
                  pl.BlockSpec((1, 8, S), lambda hp, j: (hp, 0, 0))],
        out_specs=[full, blk, blk, pl.BlockSpec((1, T, 2), lambda hp, j: (hp, j, 0)),
                   pl.BlockSpec((1, S, 2), lambda hp, j: (hp, 0, 0))],
        out_shape=[jax.ShapeDtypeStruct((S, D_MODEL), BF16)] * 3 + [jax.ShapeDtypeStruct((HP, S, 2), F32)] * 2,
        scratch_shapes=[pltpu.VMEM((2, S, LANES), F32)], compiler_params=_params(2, vmem))(ka, ka, qkv, qa, qa, do, stats_rows)


HALO = 8


def _conv_fwd(u, w, b, ts, tc):
    S, C = u.shape
    hb = ts // HALO

    def body(u_ref, prev_ref, w_ref, b_ref, o_ref, ext):
        i = pl.program_id(0)
        ext[0:HALO, :] = jnp.where(i == 0, 0.0, prev_ref[...])
        ext[HALO:HALO + ts, :] = u_ref[...]
        acc = b_ref[...] + w_ref[3:4, :] * u_ref[...]
        for k in range(SSM_CONV - 1):
            d = SSM_CONV - 1 - k
            acc = acc + w_ref[k:k + 1, :] * ext[HALO - d:HALO - d + ts, :]
        o_ref[...] = acc * _sigmoid(acc)

    return pl.pallas_call(
        body, name="conv_fwd", grid=(S // ts, C // tc),
        in_specs=[pl.BlockSpec((ts, tc), lambda i, j: (i, j)),
                  pl.BlockSpec((HALO, tc), lambda i, j: (jnp.maximum(i * hb - 1, 0), j)),
                  pl.BlockSpec((SSM_CONV, tc), lambda i, j: (0, j)), pl.BlockSpec((1, tc), lambda i, j: (0, j))],
        out_specs=pl.BlockSpec((ts, tc), lambda i, j: (i, j)), out_shape=jax.ShapeDtypeStruct((S, C), F32),
        scratch_shapes=[pltpu.VMEM((ts + HALO, tc), F32)], compiler_params=_params(2))(u, u, w, b)


def _conv_bwd(name, u, dy, w, b, ts, tc, col0):
    S, C = dy.shape
    cb = col0 // tc
    assert cb * tc == col0
    hb = ts // HALO
    nb = S // ts
    E = ts + 2 * HALO

    def body(u_ref, uprev_ref, unext_ref, dy_ref, dynext_ref, w_ref, b_ref, du_ref, dw_ref, db_ref, uext, gext):
        i = pl.program_id(1)
        last = i == nb - 1
        uext[0:HALO, :] = jnp.where(i == 0, 0.0, uprev_ref[...])
        uext[HALO:HALO + ts, :] = u_ref[...]
        uext[HALO + ts:E, :] = unext_ref[...]
        n = ts + HALO
        pre = b_ref[...] + w_ref[3:4, :] * uext[HALO:HALO + n, :]
        for k in range(SSM_CONV - 1):
            d = SSM_CONV - 1 - k
            pre = pre + w_ref[k:k + 1, :] * uext[HALO - d:HALO - d + n, :]
        sg = _sigmoid(pre)
        dsilu = sg * (1.0 + pre * (1.0 - sg))
        gext[0:ts, :] = dy_ref[...] * dsilu[0:ts, :]
        gext[ts:n, :] = jnp.where(last, 0.0, dynext_ref[...] * dsilu[ts:n, :])
        g = gext[0:ts, :]
        du = w_ref[3:4, :] * g
        for k in range(SSM_CONV - 1):
            d = SSM_CONV - 1 - k
            du = du + w_ref[k:k + 1, :] * gext[d:d + ts, :]
        du_ref[...] = du.astype(du_ref.dtype)
        dws = [jnp.sum(g * uext[HALO - (SSM_CONV - 1 - k):HALO - (SSM_CONV - 1 - k) + ts, :], axis=0, keepdims=True)
               for k in range(SSM_CONV)]
        dbs = jnp.sum(g, axis=0, keepdims=True)

        @pl.when(i == 0)
        def _():
            for k in range(SSM_CONV):
                dw_ref[k:k + 1, :] = dws[k]
            db_ref[...] = dbs

        @pl.when(i > 0)
        def _():
            for k in range(SSM_CONV):
                dw_ref[k:k + 1, :] += dws[k]
            db_ref[...] += dbs

    nxt = lambda off: (lambda j, i: (jnp.minimum((i + 1) * hb, S // HALO - 1), j + off))
    return pl.pallas_call(
        body, name=name, grid=(C // tc, nb),
        in_specs=[pl.BlockSpec((ts, tc), lambda j, i: (i, j + cb)),
                  pl.BlockSpec((HALO, tc), lambda j, i: (jnp.maximum(i * hb - 1, 0), j + cb)),
                  pl.BlockSpec((HALO, tc), nxt(cb)),
                  pl.BlockSpec((ts, tc), lambda j, i: (i, j)),
                  pl.BlockSpec((HALO, tc), nxt(0)),
                  pl.BlockSpec((SSM_CONV, tc), lambda j, i: (0, j + cb)), pl.BlockSpec((1, tc), lambda j, i: (0, j + cb))],
        out_specs=[pl.BlockSpec((ts, tc), lambda j, i: (i, j)), pl.BlockSpec((SSM_CONV, tc), lambda j, i: (0, j)),
                   pl.BlockSpec((1, tc), lambda j, i: (0, j))],
        out_shape=[jax.ShapeDtypeStruct((S, C), BF16), jax.ShapeDtypeStruct((SSM_CONV, C), F32), jax.ShapeDtypeStruct((1, C), F32)],
        scratch_shapes=[pltpu.VMEM((E, tc), F32), pltpu.VMEM((ts + HALO, tc), F32)],
        compiler_params=_params(2))(u, u, u, dy, dy, w, b)


def _head_sum():
    lane = jnp.right_shift(lax.broadcasted_iota(jnp.int32, (GROUP_LANES, 8), 0), 6)
    r = lax.broadcasted_iota(jnp.int32, (GROUP_LANES, 8), 1)
    return jnp.where(lane == r, 1.0, 0.0).astype(BF16)


def _head_expand():
    r = lax.broadcasted_iota(jnp.int32, (8, GROUP_LANES), 0)
    c = jnp.right_shift(lax.broadcasted_iota(jnp.int32, (8, GROUP_LANES), 1), 6)
    return jnp.where(r == c, 1.0, 0.0).astype(BF16)


def _ssd_common(dtc_ref, dtr_ref, bias_r, alog_b, bias_c, alog_c, L):
    a_b = -jnp.exp(alog_b)
    dt = _dot_exact_right(_softplus(dtc_ref[0] + bias_r), _head_expand())
    acum = _dot_exact_left(_tri(L, False), dt * a_b)
    a_c = -jnp.exp(alog_c)
    dtr = _softplus(dtr_ref[0] + bias_c)
    acum_r = _dot_exact_right(dtr * a_c, _tri(L, True))
    return a_b, dt, acum, acum_r


def _ssd_specs(L, nc, rev):
    cc = (lambda c: nc - 1 - c) if rev else (lambda c: c)
    G = SSM_GROUPS
    blk = pl.BlockSpec((L, GROUP_LANES), lambda g, c: (cc(c), g))
    dtc = pl.BlockSpec((1, L, 8), lambda g, c: (g, cc(c), 0))
    rowv = pl.BlockSpec((1, 1, 8), lambda g, c: (g, 0, 0))
    xs = blk
    bm = pl.BlockSpec((L, SSM_STATE), lambda g, c: (cc(c), SSM_INNER // SSM_STATE + g))
    cm = pl.BlockSpec((L, SSM_STATE), lambda g, c: (cc(c), SSM_INNER // SSM_STATE + G + g))
    dtr = pl.BlockSpec((1, 8, L), lambda g, c: (g, 0, cc(c)))
    vec = pl.BlockSpec((1, GROUP_LANES), lambda g, c: (0, g))
    colv = pl.BlockSpec((1, 8, 1), lambda g, c: (g, 0, 0))
    hs = pl.BlockSpec((1, 1, SSM_STATE, GROUP_LANES), lambda g, c: (g, cc(c), 0, 0))
    return blk, xs, bm, cm, dtc, dtr, vec, rowv, colv, hs


def _ssd_fwd(xbc, z, dtc, dtr, bias_r, alog_b, dskip_b, normw, bias_c, alog_c, L):
    S = z.shape[0]
    nc = S // L
    blk, xs, bm, cm, dtcs, dtrs, vec, rowv, colv, hs = _ssd_specs(L, nc, False)

    def body(x_ref, b_ref, c_ref, z_ref, dtc_ref, dtr_ref, bias_ref, alog_ref, dskip_ref, nw_ref, biasc_ref, alogc_ref,
             y_ref, ssm_ref, hs_ref, h_scr):
        c = pl.program_id(1)

        @pl.when(c == 0)
        def _():
            h_scr[...] = jnp.zeros_like(h_scr)

        mA = _lane_mask()
        a_b, dt, acum, acum_r = _ssd_common(dtc_ref, dtr_ref, bias_ref[0], alog_ref[...], biasc_ref[0], alogc_ref[0], L)
        x = x_ref[...]
        cb, bb = c_ref[...].astype(BF16), b_ref[...].astype(BF16)
        hprev = h_scr[...]
        hs_ref[0, 0] = hprev
        xdt = x * dt
        xdt_b = xdt.astype(BF16)
        gmat = _dot_nt(cb, bb)
        row = lax.broadcasted_iota(jnp.int32, (L, L), 0)
        col = lax.broadcasted_iota(jnp.int32, (L, L), 1)
        parts = []
        for p in range(GROUP_LANES // LANES):
            xp = xdt_b[:, p * LANES:(p + 1) * LANES]
            yd = []
            for hh in (0, 1):
                r = 2 * p + hh
                acol = acum[:, r * ATT_HEAD_DIM:r * ATT_HEAD_DIM + 1]
                arow = acum_r[r:r + 1, :]
                lm = jnp.exp(jnp.where(row >= col, acol - arow, NEG))
                yd.append(_dot((gmat * lm).astype(BF16), xp))
            parts.append(jnp.where(mA, yd[0], yd[1]))
        ydiag = jnp.concatenate(parts, axis=1)
        yoff = jnp.exp(acum) * _dot(cb, hprev.astype(BF16))
        y = ydiag + yoff + dskip_ref[...] * x
        aend = acum[L - 1:L, :]
        wgt = (jnp.exp(aend - acum) * xdt).astype(BF16)
        h_scr[...] = jnp.exp(aend) * hprev + _dot_tn(bb, wgt)
        y_ref[...] = y
        zz = z_ref[...]
        u = y * (zz * _sigmoid(zz))
        rs = lax.rsqrt(jnp.mean(u * u, axis=1, keepdims=True) + RMS_EPS)
        ssm_ref[...] = (u * rs * nw_ref[...]).astype(BF16)

    return pl.pallas_call(
        body, name="ssd_fwd", grid=(SSM_GROUPS, nc),
        in_specs=[xs, bm, cm, blk, dtcs, dtrs, rowv, vec, vec, vec, colv, colv],
        out_specs=[blk, blk, hs],
        out_shape=[jax.ShapeDtypeStruct((S, SSM_INNER), F32), jax.ShapeDtypeStruct((S, SSM_INNER), BF16),
                   jax.ShapeDtypeStruct((SSM_GROUPS, nc, SSM_STATE, GROUP_LANES), F32)],
        scratch_shapes=[pltpu.VMEM((SSM_STATE, GROUP_LANES), F32)],
        compiler_params=_params(2, 48 << 20))(xbc, xbc, xbc, z, dtc, dtr, bias_r, alog_b, dskip_b, normw, bias_c, alog_c)


def _ssd_bwd(xbc, z, y, dssm, hs_all, dtc, dtr, bias_r, alog_r, alog_b, dskip_b, normw, bias_c, alog_c, L):
    S = z.shape[0]
    nc = S // L
    blk, xs, bm, cm, dtcs, dtrs, vec, rowv, colv, hs = _ssd_specs(L, nc, True)

    def body(x_ref, b_ref, c_ref, z_ref, y_ref, dssm_ref, hs_ref, dtc_ref, dtr_ref, bias_ref, alogr_ref, alog_ref, dskip_ref, nw_ref,
             biasc_ref, alogc_ref,
             dx_ref, db_ref, dc_ref, dz_ref, ddt_ref, dnw_ref, ddskip_ref, dalog_ref, dbias_ref, dh_scr):
        c = pl.program_id(1)

        @pl.when(c == 0)
        def _():
            dh_scr[...] = jnp.zeros_like(dh_scr)

        mA = _lane_mask()
        masks = (mA, jnp.logical_not(mA))
        a_b, dt, acum, acum_r = _ssd_common(dtc_ref, dtr_ref, bias_ref[0], alog_ref[...], biasc_ref[0], alogc_ref[0], L)
        x, zz, y, dssm = x_ref[...], z_ref[...], y_ref[...], dssm_ref[...]
        cb, bb = c_ref[...].astype(BF16), b_ref[...].astype(BF16)
        hprev = hs_ref[0, 0]
        hb = hprev.astype(BF16)
        ds = dh_scr[...]
        dsb = ds.astype(BF16)
        dskip = dskip_ref[...]
        aend = acum[L - 1:L, :]
        e_a, e_end = jnp.exp(acum), jnp.exp(aend)
        dte = jnp.exp(aend - acum)
        xdt = x * dt
        xdt_b = xdt.astype(BF16)
        sg = _sigmoid(zz)
        sz = zz * sg
        u = y * sz
        rs = lax.rsqrt(jnp.mean(u * u, axis=1, keepdims=True) + RMS_EPS)
        un = u * rs
        dun = dssm * nw_ref[...]
        du = rs * (dun - un * jnp.mean(dun * un, axis=1, keepdims=True))
        dy = du * sz
        dz_ref[...] = (du * y * sg * (1.0 + zz * (1.0 - sg))).astype(dz_ref.dtype)
        dy_b = dy.astype(BF16)
        dch_b = (dy * e_a).astype(BF16)
        dc = _dot_nt(dch_b, hb)
        dhprev = _dot_tn(cb, dch_b)
        gt = _dot_nt(bb, cb)
        row = lax.broadcasted_iota(jnp.int32, (L, L), 0)
        col = lax.broadcasted_iota(jnp.int32, (L, L), 1)
        dgt = jnp.zeros((L, L), F32)
        parts = []
        for p in range(GROUP_LANES // LANES):
            xp = xdt_b[:, p * LANES:(p + 1) * LANES]
            dyp = dy_b[:, p * LANES:(p + 1) * LANES]
            zero = jnp.zeros_like(dyp)
            acc = None
            for hh in (0, 1):
                r = 2 * p + hh
                acol = acum[:, r * ATT_HEAD_DIM:r * ATT_HEAD_DIM + 1]
                arow = acum_r[r:r + 1, :]
                lmt = jnp.exp(jnp.where(row <= col, arow - acol, NEG))
                dyh = jnp.where(masks[hh], dyp, zero)
                part = _dot((gt * lmt).astype(BF16), dyh)
                acc = part if acc is None else acc + part
                dgt = dgt + _dot_nt(xp, dyh) * lmt
            parts.append(acc)
        dxdt_diag = jnp.concatenate(parts, axis=1)
        dgt_b = dgt.astype(BF16)
        db = _dot(dgt_b, cb)
        dc = dc + _dot_tn(dgt_b, bb)
        dxdt_state = dte * _dot(bb, dsb)
        db = db + _dot_nt((dte * xdt).astype(BF16), dsb)
        dxdt = dxdt_diag + dxdt_state
        dy_r, xdt_r = dy_b.astype(F32), xdt_b.astype(F32)
        dac = dy_r * (y - dskip * x) - xdt_r * dxdt
        tail = jnp.sum(xdt_r * dxdt_state, axis=0, keepdims=True) + e_end * jnp.sum(ds * hprev, axis=0, keepdims=True)
        rowl = lax.broadcasted_iota(jnp.int32, (L, 1), 0)
        dac = dac + jnp.where(rowl == L - 1, tail, 0.0)
        rc = _dot_exact_left(_tri(L, True), dac)
        hsum = _head_sum()
        hs1 = _dot_exact_right(dxdt * x, hsum, 2)
        hs2 = _dot_exact_right(rc, hsum, 2)
        a8 = -jnp.exp(alogr_ref[0])
        dtraw8 = dtc_ref[0] + bias_ref[0]
        ddtraw = (hs1 + a8 * hs2) * _sigmoid(dtraw8)
        dx_ref[...] = dskip * dy + dxdt * dt
        db_ref[...] = db
        dc_ref[...] = dc
        ddt_ref[0] = ddtraw
        dh_scr[...] = e_end * ds + dhprev
        sums = (jnp.sum(dssm * un, axis=0, keepdims=True), jnp.sum(dy * x, axis=0, keepdims=True))
        refs = (dnw_ref, ddskip_ref)
        sums8 = (a8 * jnp.sum(hs2 * _softplus(dtraw8), axis=0, keepdims=True), jnp.sum(ddtraw, axis=0, keepdims=True))
        refs8 = (dalog_ref, dbias_ref)

        @pl.when(c == 0)
        def _():
            for r, v in zip(refs, sums):
                r[...] = v
            for r, v in zip(refs8, sums8):
                r[0] = v

        @pl.when(c > 0)
        def _():
            for r, v in zip(refs, sums):
                r[...] += v
            for r, v in zip(refs8, sums8):
                r[0] += v

    nbc = pl.BlockSpec((L, SSM_STATE), lambda g, c: (nc - 1 - c, g))
    return pl.pallas_call(
        body, name="ssd_bwd", grid=(SSM_GROUPS, nc),
        in_specs=[xs, bm, cm, blk, blk, blk, hs, dtcs, dtrs, rowv, rowv, vec, vec, vec, colv, colv],
        out_specs=[blk, nbc, nbc, blk, dtcs, vec, vec, rowv, rowv],
        out_shape=[jax.ShapeDtypeStruct((S, SSM_INNER), F32), jax.ShapeDtypeStruct((S, SSM_GROUPS * SSM_STATE), F32),
                   jax.ShapeDtypeStruct((S, SSM_GROUPS * SSM_STATE), F32), jax.ShapeDtypeStruct((S, SSM_INNER), BF16),
                   jax.ShapeDtypeStruct((SSM_GROUPS, S, 8), F32)] + [jax.ShapeDtypeStruct((1, SSM_INNER), F32)] * 2
                  + [jax.ShapeDtypeStruct((SSM_GROUPS, 1, 8), F32)] * 2,
        scratch_shapes=[pltpu.VMEM((SSM_STATE, GROUP_LANES), F32)],
        compiler_params=_params(2, 56 << 20))(xbc, xbc, xbc, z, y, dssm, hs_all, dtc, dtr, bias_r, alog_r, alog_b, dskip_b,
                                              normw, bias_c, alog_c)


def _place():
    return lax.axis_index("x"), lax.axis_index("y"), lax.axis_index("c")


def _other_chips(x, y):
    return [(1 - x, y), (x, 1 - y), (1 - x, 1 - y)]


def _half_rows(rows, which):
    hr = rows // 2
    if isinstance(which, int):
        return pl.ds(which * hr, hr)
    return pl.ds(pl.multiple_of(which * hr, 8), hr)


def _chip_gather(name, shards, split):
    n = len(shards)
    ANY = pl.BlockSpec(memory_space=pl.ANY)

    def body(*refs):
        ins, outs = refs[:n], refs[n:2 * n]
        send, recv, fsend, frecv = refs[2 * n:]
        x, y, c = _place()
        me = 2 * x + y
        sibling = (x, y, 1 - c)
        chips = _other_chips(x, y)

        def piece(a, chip_idx, which):
            if split[a]:
                return outs[a].at[chip_idx, _half_rows(shards[a].shape[0], which)]
            return outs[a].at[chip_idx]

        def ici(k, a, to_chip, src_chip):
            src = ins[a].at[_half_rows(shards[a].shape[0], c)] if split[a] else ins[a]
            return pltpu.make_async_remote_copy(src_ref=src, dst_ref=piece(a, src_chip, c), send_sem=send.at[k, a],
                                                recv_sem=recv.at[k, a], device_id=(*to_chip, c), device_id_type=MESH)

        def fwd(k, a, src_chip, which):
            return pltpu.make_async_remote_copy(src_ref=piece(a, src_chip, which), dst_ref=piece(a, src_chip, which),
                                                send_sem=fsend.at[k, a], recv_sem=frecv.at[k, a], device_id=sibling,
                                                device_id_type=MESH)

        sends = []
        for k, chip in enumerate(chips):
            for a in range(n):
                cp = ici(k, a, chip, me)
                cp.start()
                sends.append(cp)
        for k, (ox, oy) in enumerate(chips):
            src = 2 * ox + oy
            for a in range(n):
                ici(k, a, (ox, oy), src).wait_recv()
                if split[a]:
                    cp = fwd(k, a, src, c)
                    cp.start()
                    sends.append(cp)
        for k, (ox, oy) in enumerate(chips):
            for a in range(n):
                if split[a]:
                    fwd(k, a, 2 * ox + oy, 1 - c).wait_recv()
        for cp in sends:
            cp.wait_send()

    sem = pltpu.SemaphoreType.DMA((3, n))
    return pl.pallas_call(
        body, name=name, in_specs=[ANY] * n, out_specs=[ANY] * n,
        out_shape=[jax.ShapeDtypeStruct((4,) + s.shape, s.dtype) for s in shards],
        scratch_shapes=[sem, sem, sem, sem])(*shards)


def _chip_scatter(name, blocks):
    n = len(blocks)
    ANY = pl.BlockSpec(memory_space=pl.ANY)

    def body(*refs):
        ins, outs = refs[:n], refs[n:2 * n]
        send, recv, loc = refs[2 * n:]
        x, y, c = _place()
        me = 2 * x + y
        local = [pltpu.make_async_copy(ins[a].at[me], outs[a].at[me], loc.at[a]) for a in range(n)]
        for cp in local:
            cp.start()
        sends = []
        for k, (ox, oy) in enumerate(_other_chips(x, y)):
            dst_chip = 2 * ox + oy
            for a in range(n):
                cp = pltpu.make_async_remote_copy(src_ref=ins[a].at[dst_chip], dst_ref=outs[a].at[me], send_sem=send.at[k, a],
                                                  recv_sem=recv.at[k, a], device_id=(ox, oy, c), device_id_type=MESH)
                cp.start()
                sends.append(cp)
        for k, (ox, oy) in enumerate(_other_chips(x, y)):
            src = 2 * ox + oy
            for a in range(n):
                pltpu.make_async_remote_copy(src_ref=ins[a].at[me], dst_ref=outs[a].at[src], send_sem=send.at[k, a],
                                             recv_sem=recv.at[k, a], device_id=(ox, oy, c), device_id_type=MESH).wait_recv()
        for cp in sends:
            cp.wait_send()
        for cp in local:
            cp.wait()

    return pl.pallas_call(
        body, name=name, in_specs=[ANY] * n, out_specs=[ANY] * n,
        out_shape=[jax.ShapeDtypeStruct(b.shape, b.dtype) for b in blocks],
        scratch_shapes=[pltpu.SemaphoreType.DMA((3, n)), pltpu.SemaphoreType.DMA((3, n)), pltpu.SemaphoreType.DMA((n,))],
    )(*blocks)


def _half_to_sibling(name, blocks):
    n = len(blocks)
    ANY = pl.BlockSpec(memory_space=pl.ANY)

    def body(*refs):
        ins, outs = refs[:n], refs[n:2 * n]
        send, recv = refs[2 * n:]
        x, y, c = _place()
        cps = [pltpu.make_async_remote_copy(src_ref=ins[a].at[:, _half_rows(blocks[a].shape[1], 1 - c)], dst_ref=outs[a],
                                            send_sem=send.at[a], recv_sem=recv.at[a], device_id=(x, y, 1 - c),
                                            device_id_type=MESH) for a in range(n)]
        for cp in cps:
            cp.start()
        for cp in cps:
            cp.wait_recv()
        for cp in cps:
            cp.wait_send()

    return pl.pallas_call(
        body, name=name, in_specs=[ANY] * n, out_specs=[ANY] * n,
        out_shape=[jax.ShapeDtypeStruct((4, b.shape[1] // 2, b.shape[2]), b.dtype) for b in blocks],
        scratch_shapes=[pltpu.SemaphoreType.DMA((n,)), pltpu.SemaphoreType.DMA((n,))])(*blocks)


def _sibling_swap(name, arrs):
    n = len(arrs)
    ANY = pl.BlockSpec(memory_space=pl.ANY)

    def body(*refs):
        ins, outs = refs[:n], refs[n:2 * n]
        send, recv = refs[2 * n:]
        x, y, c = _place()
        cps = [pltpu.make_async_remote_copy(src_ref=ins[a], dst_ref=outs[a], send_sem=send.at[a], recv_sem=recv.at[a],
                                            device_id=(x, y, 1 - c), device_id_type=MESH) for a in range(n)]
        for cp in cps:
            cp.start()
        for cp in cps:
            cp.wait_recv()
        for cp in cps:
            cp.wait_send()

    return pl.pallas_call(
        body, name=name, in_specs=[ANY] * n, out_specs=[ANY] * n,
        out_shape=[jax.ShapeDtypeStruct(a.shape, a.dtype) for a in arrs],
        scratch_shapes=[pltpu.SemaphoreType.DMA((n,)), pltpu.SemaphoreType.DMA((n,))])(*arrs)


N_DEV = 8


def _all_sum_small(vec):
    P = vec.shape[1]

    def body(v_ref, o_ref, buf, send, recv):
        x, y, c = _place()
        me = 4 * x + 2 * y + c
        buf[me] = v_ref[...]

        def peer(r):
            return ((1 - x) if (r >> 2) & 1 else x, (1 - y) if (r >> 1) & 1 else y, (1 - c) if r & 1 else c)

        sends = []
        for r in range(1, N_DEV):
            cp = pltpu.make_async_remote_copy(src_ref=v_ref, dst_ref=buf.at[me], send_sem=send.at[r], recv_sem=recv.at[r],
                                              device_id=peer(r), device_id_type=MESH)
            cp.start()
            sends.append(cp)
        for r in range(1, N_DEV):
            px, py, pc = peer(r)
            pltpu.make_async_remote_copy(src_ref=v_ref, dst_ref=buf.at[4 * px + 2 * py + pc], send_sem=send.at[r],
                                         recv_sem=recv.at[r], device_id=(px, py, pc), device_id_type=MESH).wait_recv()
        for cp in sends:
            cp.wait_send()
        tot = buf[0]
        for d in range(1, N_DEV):
            tot = tot + buf[d]
        o_ref[...] = tot

    return pl.pallas_call(
        body, name="all_sum_small", in_specs=[pl.BlockSpec(memory_space=pltpu.VMEM)],
        out_specs=pl.BlockSpec(memory_space=pltpu.VMEM), out_shape=jax.ShapeDtypeStruct((1, P), F32),
        scratch_shapes=[pltpu.VMEM((N_DEV, 1, P), F32), pltpu.SemaphoreType.DMA((N_DEV,)), pltpu.SemaphoreType.DMA((N_DEV,))],
    )(vec)


def _half_sum(name, blocks, theirs, core, tr):
    _, R, C = blocks.shape
    hr = R // 2
    nb = hr // tr
    assert nb * tr == hr

    def body(c_ref, a_ref, b_ref, o_ref):
        o_ref[...] = (a_ref[...] + b_ref[...]).astype(BF16)

    grid_spec = pltpu.PrefetchScalarGridSpec(
        num_scalar_prefetch=1, grid=(4, nb),
        in_specs=[pl.BlockSpec((1, tr, C), lambda b, i, c_ref: (b, c_ref[0] * nb + i, 0)),
                  pl.BlockSpec((1, tr, C), lambda b, i, c_ref: (b, i, 0))],
        out_specs=pl.BlockSpec((1, tr, C), lambda b, i, c_ref: (b, i, 0)))
    return pl.pallas_call(body, name=name, grid_spec=grid_spec, out_shape=jax.ShapeDtypeStruct((4, hr, C), BF16),
                          compiler_params=_params(2, 40 << 20))(core, blocks, theirs)


def _sum4(name, stack, tr):
    _, R, C = stack.shape

    def body(s_ref, o_ref):
        s = s_ref[...].astype(F32)
        o_ref[...] = ((s[0] + s[1]) + s[2]) + s[3]

    return pl.pallas_call(body, name=name, grid=(R // tr,), in_specs=[pl.BlockSpec((4, tr, C), lambda i: (0, i, 0))],
                          out_specs=pl.BlockSpec((tr, C), lambda i: (i, 0)), out_shape=jax.ShapeDtypeStruct((R, C), F32),
                          compiler_params=_params(1, 40 << 20))(stack)


def _adamw_math(w, m, v, g):
    c1 = 1.0 - ADAM_B1 ** ADAM_STEP
    c2 = 1.0 - ADAM_B2 ** ADAM_STEP
    nm = ADAM_B1 * m + (1.0 - ADAM_B1) * g
    nv = ADAM_B2 * v + (1.0 - ADAM_B2) * (g * g)
    return -ADAM_LR * ((nm / c1) / (jnp.sqrt(nv / c2) + ADAM_EPS) + ADAM_WD * w), nm, nv


def _adamw(name, w, m, v, g, tr):
    R, C = w.shape

    def body(w_ref, m_ref, v_ref, ga_ref, g_ref, d_ref, nm_ref, nv_ref):
        g = ga_ref[...]
        g_ref[...] = g
        d_ref[...], nm_ref[...], nv_ref[...] = _adamw_math(w_ref[...], m_ref[...], v_ref[...], g)

    spec = pl.BlockSpec((tr, C), lambda i: (i, 0))
    return pl.pallas_call(body, name=name, grid=(R // tr,), in_specs=[spec] * 4, out_specs=[spec] * 4,
                          out_shape=[jax.ShapeDtypeStruct((R, C), F32)] * 4, compiler_params=_params(1, 40 << 20))(w, m, v, g)


def _adamw_halves(name, w, m, v, mine, theirs, core, tr):
    _, R, C = w.shape
    nb = (R // 2) // tr
    assert 2 * nb * tr == R

    def body(c_ref, w_ref, m_ref, v_ref, a_ref, b_ref, g_ref, d_ref, nm_ref, nv_ref):
        g = jnp.where((pl.program_id(0) // nb) == c_ref[0], a_ref[...], b_ref[...])
        g_ref[0] = g
        d_ref[0], nm_ref[0], nv_ref[0] = _adamw_math(w_ref[0], m_ref[0], v_ref[0], g)

    spec = pl.BlockSpec((1, tr, C), lambda i, c_ref: (0, i, 0))
    half = lambda own: pl.BlockSpec((tr, C), lambda i, c_ref, own=own: (
        jnp.clip(i - (c_ref[0] if own else 1 - c_ref[0]) * nb, 0, nb - 1), 0))
    grid_spec = pltpu.PrefetchScalarGridSpec(num_scalar_prefetch=1, grid=(R // tr,),
                                             in_specs=[spec, spec, spec, half(True), half(False)], out_specs=[spec] * 4)
    return pl.pallas_call(body, name=name, grid_spec=grid_spec, out_shape=[jax.ShapeDtypeStruct((1, R, C), F32)] * 4,
                          compiler_params=_params(1, 40 << 20))(core, w, m, v, mine, theirs)


def _row_tile(rows, cols, budget_bytes=1 << 20, mult=8):
    best = None
    for t in range(mult, rows + 1, mult):
        if rows % t == 0 and t * cols * 4 <= budget_bytes:
            best = t
    return best if best is not None else rows


def _ln_fwd(r, g, b):
    mu = jnp.mean(r, axis=1, keepdims=True)
    xc = r - mu
    rstd = lax.rsqrt(jnp.mean(xc * xc, axis=1, keepdims=True) + LN_EPS)
    xhat = xc * rstd
    return xhat * g + b, xhat, rstd


def _ln_bwd(dy, xhat, rstd, g):
    dxh = dy * g
    return rstd * (dxh - jnp.mean(dxh, axis=1, keepdims=True) - xhat * jnp.mean(dxh * xhat, axis=1, keepdims=True))


def _to_chip_blocks_cols(a):
    R, C4 = a.shape
    return a.reshape(R, 4, C4 // 4).transpose(1, 0, 2)


def _from_chip_blocks_cols(a):
    return a.transpose(1, 0, 2).reshape(a.shape[1], 4 * a.shape[2])


def kernel(x, w_in, b_forget, conv_w, conv_b, dt_bias, a_log, d_skip, ssm_norm_w, w_proj_attn, w_proj_ssm, b_gates, w_out, ln1_g, ln1_b, w_ffn_gate, w_ffn_up, w_ffn_down, ln2_g, ln2_b, loss_target, m_w_in, m_b_forget, m_conv_w, m_conv_b, m_dt_bias, m_a_log, m_d_skip, m_ssm_norm_w, m_w_proj_attn, m_w_proj_ssm, m_b_gates, m_w_out, m_ln1_g, m_ln1_b, m_w_ffn_gate, m_w_ffn_up, m_w_ffn_down, m_ln2_g, m_ln2_b, v_w_in, v_b_forget, v_conv_w, v_conv_b, v_dt_bias, v_a_log, v_d_skip, v_ssm_norm_w, v_w_proj_attn, v_w_proj_ssm, v_b_gates, v_w_out, v_ln1_g, v_ln1_b, v_w_ffn_gate, v_w_ffn_up, v_w_ffn_down, v_ln2_g, v_ln2_b):
    S = x.shape[1]
    D = D_MODEL
    TM, TM2, TA, AQF, LC, TS, TB = (min(TILES[k], S) for k in ("TM", "TM2", "TA", "AQF", "LC", "TS", "TB"))
    xf = x[0]
    tgt = loss_target[0]
    xb = xf.astype(BF16)

    shards = [w_in[0].astype(BF16), conv_w[0], w_proj_attn[0].astype(BF16), w_proj_ssm[0].astype(BF16), w_out[0].astype(BF16),
              w_ffn_gate[0].astype(BF16), w_ffn_up[0].astype(BF16), w_ffn_down[0].astype(BF16)]
    chip = 2 * lax.axis_index("x") + lax.axis_index("y")
    gathered = _chip_gather("gather_weights", shards, [True, False, True, True, True, True, True, True])
    g_in, g_cw, g_pa, g_ps, g_out, g_fg, g_fu, g_fd = (lax.dynamic_update_slice(g, sh[None], (chip, 0, 0))
                                                       for g, sh in zip(gathered, shards))
    w_full = _from_chip_blocks_cols(g_in)
    w_re = jnp.concatenate([w_full[:, 0:3072], w_full[:, 3088:5136], w_full[:, 5136:8208], w_full[:, 8240:10288],
                            w_full[:, 3072:3088], w_full[:, 8208:8240], jnp.zeros((D, 80), BF16)], axis=1)
    conv_w_full = _from_chip_blocks_cols(g_cw)
    wpa, wps, wout = g_pa.reshape(D, D), g_ps.reshape(SSM_INNER, D), g_out.reshape(D, D)
    wfg, wfu, wfd = _from_chip_blocks_cols(g_fg), _from_chip_blocks_cols(g_fu), g_fd.reshape(FFN_HIDDEN, D)

    def plain(accs, rows, vecs, j):
        return [accs[0]], []

    def q_scaled(accs, rows, vecs, j):
        return [accs[0] * jnp.where(j * 512 < D, 1.0 / math.sqrt(ATT_HEAD_DIM), 1.0)], []

    qkv, = _mm("proj_qkv", S, 3072, TM, 512, [(xb, D, 0)], [(w_re, 0)], [(0, 0)], q_scaled, [(3072, BF16, 0)])
    z, = _mm("proj_z", S, 2048, TM, 512, [(xb, D, 0)], [(w_re, RE_Z // 512)], [(0, 0)], plain, [(2048, F32, 0)])
    xbc_raw, = _mm("proj_xbc", S, 3072, TM, 512, [(xb, D, 0)], [(w_re, RE_XBC // 512)], [(0, 0)], plain, [(3072, F32, 0)])
    gl, = _mm("proj_gate", S, 2048, TM, 512, [(xb, D, 0)], [(w_re, RE_GATE // 512)], [(0, 0)], plain, [(2048, F32, 0)])
    small, = _mm("proj_small", S, 128, TM, 128, [(xb, D, 0)], [(w_re, RE_SMALL // 128)], [(0, 0)], plain, [(128, F32, 0)])

    bvec = jnp.concatenate([b_forget, jnp.zeros((1, LANES - ATT_HEADS), F32)], axis=1)
    cum = _cum_fwd(small, bvec, TB)[:, :ATT_HEADS]
    cum_cols = cum.reshape(S, 8, 2).transpose(1, 0, 2)
    qa, ka = _attn_prep(qkv, cum_cols, TM)
    o, o32, lse_rows = _attn_fwd(qa, ka, qkv, AQF, TA)

    cb_row = conv_b
    xbc = _conv_fwd(xbc_raw, conv_w_full, cb_row, min(512, S), 512)
    dt_raw = small[:, 16:48]
    dtc = dt_raw.reshape(S, SSM_GROUPS, 8).transpose(1, 0, 2)
    dtr = dt_raw.T.reshape(SSM_GROUPS, 8, S)
    bias_r = dt_bias.reshape(SSM_GROUPS, 1, 8)
    alog_b = jnp.repeat(a_log, ATT_HEAD_DIM, axis=1)
    dskip_b = jnp.repeat(d_skip, ATT_HEAD_DIM, axis=1)
    bias_c = dt_bias.reshape(SSM_GROUPS, 8, 1)
    alog_c = a_log.reshape(SSM_GROUPS, 8, 1)
    y_ssd, ssm, hs_all = _ssd_fwd(xbc, z, dtc, dtr, bias_r, alog_b, dskip_b, ssm_norm_w, bias_c, alog_c, LC)

    def merge(accs, rows, vecs, j):
        g0, g1 = _sigmoid(rows[0] + vecs[0]), _sigmoid(rows[1] + vecs[1])
        return [g0 * accs[0] + g1 * accs[1], accs[0], accs[1]], []

    mix, attn_d, ssm_d = _mm("merge", S, D, TM, 512, [(o, D, 0), (ssm, SSM_INNER, 0)], [(wpa, 0), (wps, 0)], [(0, 0), (1, 1)],
                             merge, [(D, BF16, 0), (D, F32, 0), (D, F32, 0)], rows=[(gl, 0), (gl, 2)],
                             vecs_n=[(b_gates, 0), (b_gates, 2)])

    def out_ln1(accs, rows, vecs, j):
        r1 = ALPHA * rows[0] + accs[0]
        h1, _, _ = _ln_fwd(r1, vecs[0], vecs[1])
        return [r1, h1, h1], []

    r1, h1, h1b = _mm("out_ln1", S, D, TM2, D, [(mix, D, 0)], [(wout, 0)], [(0, 0)], out_ln1,
                      [(D, F32, 0), (D, F32, 0), (D, BF16, 0)], rows=[(xf, 0)], vecs_n=[(ln1_g, 0), (ln1_b, 0)])

    FT = FFN_HIDDEN // 2

    def swiglu(accs, rows, vecs, j):
        g, u = accs
        return [g, u, g * _sigmoid(g) * u], []

    gate, up, hmid = _mm("ffn_up", S, FFN_HIDDEN, TM2, FT, [(h1b, D, 0)], [(wfg, 0), (wfu, 0)], [(0, 0), (0, 1)], swiglu,
                         [(FFN_HIDDEN, F32, 0), (FFN_HIDDEN, F32, 0), (FFN_HIDDEN, BF16, 0)])

    def down_ln2_loss(accs, rows, vecs, j):
        r2 = ALPHA * rows[0] + accs[0]
        yv, xhat, rstd = _ln_fwd(r2, vecs[0], vecs[1])
        diff = yv - rows[1]
        dy = diff * (1.0 / D_MODEL)
        dr2 = _ln_bwd(dy, xhat, rstd, vecs[0])
        return [dr2, dr2], [jnp.sum(dy * xhat, axis=0, keepdims=True), jnp.sum(dy, axis=0, keepdims=True),
                            (0.5 / D_MODEL) * jnp.sum(diff * diff, axis=0, keepdims=True)]

    dr2, dr2b, dln2_g, dln2_b, loss_lanes = _mm("ffn_down_ln2", S, D, TM2, D, [(hmid, FFN_HIDDEN, 0)], [(wfd, 0)], [(0, 0)],
                                               down_ln2_loss, [(D, F32, 0), (D, BF16, 0)], rows=[(h1, 0), (tgt, 0)],
                                               vecs_n=[(ln2_g, 0), (ln2_b, 0)], sums=[D, D, D])
    loss = lax.psum(jnp.sum(loss_lanes), ("x", "y", "c"))

    def dswiglu(accs, rows, vecs, j):
        g, u = rows
        sg = _sigmoid(g)
        return [accs[0] * u * sg * (1.0 + g * (1.0 - sg)), accs[0] * g * sg], []

    dgate, dup = _mm("ffn_down_bwd", S, FFN_HIDDEN, TM2, FT, [(dr2b, D, 0)], [(wfd, 0)], [(0, 0)], dswiglu,
                     [(FFN_HIDDEN, BF16, 0), (FFN_HIDDEN, BF16, 0)], nt=True, rows=[(gate, 0), (up, 0)])
    dwfd = _mm_tn("dw_ffn_down", hmid, dr2b, FFN_HIDDEN // 2, D, TS)
    dwfg = _mm_tn("dw_ffn_gate", h1b, dgate, D, FT, TS)
    dwfu = _mm_tn("dw_ffn_up", h1b, dup, D, FT, TS)

    def dh1_ln1(accs, rows, vecs, j):
        dh1 = ALPHA * rows[0] + accs[0] + accs[1]
        _, xhat, rstd = _ln_fwd(rows[1], vecs[0], vecs[0])
        dr1 = _ln_bwd(dh1, xhat, rstd, vecs[0])
        return [dr1, dr1], [jnp.sum(dh1 * xhat, axis=0, keepdims=True), jnp.sum(dh1, axis=0, keepdims=True)]

    dr1, dr1b, dln1_g, dln1_b = _mm("ffn_up_bwd_ln1", S, D, TM2, D, [(dgate, FFN_HIDDEN, 0), (dup, FFN_HIDDEN, 0)],
                                    [(wfg, 0), (wfu, 0)], [(0, 0), (1, 1)], dh1_ln1, [(D, F32, 0), (D, BF16, 0)], nt=True,
                                    rows=[(dr2, 0), (r1, 0)], vecs_n=[(ln1_g, 0)], sums=[D, D])

    def dmerge(accs, rows, vecs, j):
        dmix = accs[0]
        g0, g1 = _sigmoid(rows[0] + vecs[0]), _sigmoid(rows[1] + vecs[1])
        dgl0 = dmix * rows[2] * g0 * (1.0 - g0)
        dgl1 = dmix * rows[3] * g1 * (1.0 - g1)
        return [dmix * g0, dmix * g1, dgl0, dgl1], [jnp.sum(dgl0, axis=0, keepdims=True), jnp.sum(dgl1, axis=0, keepdims=True)]

    d_attn_d, d_ssm_d, dgl0, dgl1, dbg0, dbg1 = _mm(
        "out_bwd", S, D, TM, 512, [(dr1b, D, 0)], [(wout, 0)], [(0, 0)], dmerge, [(D, BF16, 0)] * 4, nt=True,
        rows=[(gl, 0), (gl, 2), (attn_d, 0), (ssm_d, 0)], vecs_n=[(b_gates, 0), (b_gates, 2)], sums=[D, D])
    dwout = _mm_tn("dw_out", mix, dr1b, D, D, TS)
    dwpa = _mm_tn("dw_proj_attn", o, d_attn_d, D, D, TS)
    dwps = _mm_tn("dw_proj_ssm", ssm, d_ssm_d, D, D, TS)

    do, = _mm("proj_attn_bwd", S, D, TM, 512, [(d_attn_d, D, 0)], [(wpa, 0)], [(0, 0)], plain, [(D, BF16, 0)], nt=True)
    stats_rows = _attn_stats(do, o32, lse_rows, AQF)
    dq, dk, dv, dck, dcq = _attn_bwd(qa, ka, qkv, do, stats_rows, TA)

    def per_head(a):
        a = a.transpose(1, 0, 2).reshape(S, ATT_HEADS)
        return jnp.concatenate([a, jnp.zeros((S, LANES - ATT_HEADS), F32)], axis=1)

    dfl, dbf = _cum_bwd(per_head(dck), per_head(dcq), small, bvec, TB)

    dssm, = _mm("proj_ssm_bwd", S, SSM_INNER, TM, 512, [(d_ssm_d, D, 0)], [(wps, 0)], [(0, 0)], plain, [(SSM_INNER, F32, 0)],
                nt=True)
    dxs, dbm, dcm, dz, ddt8, dnw, ddskip_b, dalog8, dbias8 = _ssd_bwd(
        xbc, z, y_ssd, dssm, hs_all, dtc, dtr, bias_r, a_log.reshape(SSM_GROUPS, 1, 8), alog_b, dskip_b, ssm_norm_w, bias_c,
        alog_c, LC)
    TC = min(512, S)
    du_x, dcw_x, dcb_x = _conv_bwd("conv_bwd_x", xbc_raw, dxs, conv_w_full, cb_row, TC, 512, 0)
    du_b, dcw_b, dcb_b = _conv_bwd("conv_bwd_b", xbc_raw, dbm, conv_w_full, cb_row, TC, 512, SSM_INNER)
    du_c, dcw_c, dcb_c = _conv_bwd("conv_bwd_c", xbc_raw, dcm, conv_w_full, cb_row, TC, 512, SSM_INNER + SSM_GROUPS * SSM_STATE)
    dconv_w = jnp.concatenate([dcw_x, dcw_b, dcw_c], axis=1)
    dconv_b = jnp.concatenate([dcb_x, dcb_b, dcb_c], axis=1)
    ddt_raw = ddt8.transpose(1, 0, 2).reshape(S, SSM_HEADS)

    dsmall = jnp.concatenate([dfl[:, :ATT_HEADS], ddt_raw, jnp.zeros((S, 80), F32)], axis=1).astype(BF16)
    def dx_first(accs, rows, vecs, j):
        return [ALPHA * rows[0] + sum(accs[1:], accs[0])], []

    def dx_more(accs, rows, vecs, j):
        return [rows[0] + sum(accs[1:], accs[0])], []

    wk = lambda col, width=D: (w_re, 0, col // width, width)
    dx_part, = _mm("dx_a", S, D, TM2, D, [(dq, D, 0), (dk, D, 0), (dv, D, 0), (dz, D, 0), (dz, D, 1)],
                   [wk(0), wk(1024), wk(2048), wk(RE_Z), wk(RE_Z + 1024)], [(k, k) for k in range(5)], dx_first,
                   [(D, F32, 0)], nt=True, rows=[(dr1, 0)])
    HB = SSM_GROUPS * SSM_STATE
    grad_x, = _mm("dx_b", S, D, TM2, D,
                  [(du_x, D, 0), (du_x, D, 1), (du_b, HB, 0), (du_c, HB, 0), (dgl0, D, 0), (dgl1, D, 0), (dsmall, LANES, 0)],
                  [wk(RE_XBC), wk(RE_XBC + 1024), wk(RE_XBC + 2048, HB), wk(RE_XBC + 2048 + HB, HB), wk(RE_GATE),
                   wk(RE_GATE + 1024), wk(RE_SMALL, LANES)],
                  [(k, k) for k in range(7)], dx_more, [(D, F32, 0)], nt=True, rows=[(dx_part, 0)])
    dw_q, dw_k, dw_v = (_mm_tn("dw_in_" + nm, xb, g_, D, D, TS) for nm, g_ in (("q", dq), ("k", dk), ("v", dv)))
    dw_z = _mm_tn("dw_in_z", xb, dz, D, D, TS)
    dw_xbc = jnp.concatenate([_mm_tn("dw_in_xs", xb, du_x, D, D, TS), _mm_tn("dw_in_b", xb, du_b, D, HB, TS),
                              _mm_tn("dw_in_c", xb, du_c, D, HB, TS)], axis=1)
    dw_g0, dw_g1 = _mm_tn("dw_in_g0", xb, dgl0, D, D, TS), _mm_tn("dw_in_g1", xb, dgl1, D, D, TS)
    dw_s = _mm_tn("dw_in_small", xb, dsmall, D, LANES, TS)
    dw_full = jnp.concatenate([dw_q, dw_k, dw_v, dw_s[:, 0:ATT_HEADS], dw_z, dw_xbc, dw_s[:, ATT_HEADS:ATT_HEADS + SSM_HEADS],
                               dw_g0, dw_g1], axis=1)

    blocks = [_to_chip_blocks_cols(dw_full), dwpa.reshape(4, D // 4, D), dwps.reshape(4, SSM_INNER // 4, D),
              dwout.reshape(4, D // 4, D), _to_chip_blocks_cols(dwfg), _to_chip_blocks_cols(dwfu),
              dwfd.reshape(4, FFN_HIDDEN // 4, D)]
    names = ["w_in", "w_proj_attn", "w_proj_ssm", "w_out", "w_ffn_gate", "w_ffn_up", "w_ffn_down"]
    core = lax.axis_index("c").astype(jnp.int32).reshape(1)
    theirs = _half_to_sibling("swap_halves", blocks)
    halves = [_half_sum("halfsum_" + nm, b, t, core, _row_tile(b.shape[1] // 2, b.shape[2], mult=16))
              for nm, b, t in zip(names, blocks, theirs)]
    stacks = _chip_scatter("scatter_grads", halves)
    reduced = [_sum4("sum_" + nm, st, _row_tile(st.shape[1], st.shape[2], mult=16)) for nm, st in zip(names, stacks)]
    other = _sibling_swap("swap_reduced", reduced)
    big_w = [w_in, w_proj_attn, w_proj_ssm, w_out, w_ffn_gate, w_ffn_up, w_ffn_down]
    big_m = [m_w_in, m_w_proj_attn, m_w_proj_ssm, m_w_out, m_w_ffn_gate, m_w_ffn_up, m_w_ffn_down]
    big_v = [v_w_in, v_w_proj_attn, v_w_proj_ssm, v_w_out, v_w_ffn_gate, v_w_ffn_up, v_w_ffn_down]
    big = {}
    for nm, w_, m_, v_, mine, theirs in zip(names, big_w, big_m, big_v, reduced, other):
        big[nm] = _adamw_halves("adamw_" + nm, w_, m_, v_, mine, theirs, core, _row_tile(w_.shape[1] // 2, w_.shape[2]))

    dd_skip = ddskip_b.reshape(1, SSM_HEADS, ATT_HEAD_DIM).sum(axis=2)
    pieces = [dbf[:, :ATT_HEADS], dconv_w.reshape(1, SSM_CONV * SSM_CONV_DIM), dconv_b, dbias8.reshape(1, SSM_HEADS), dalog8.reshape(1, SSM_HEADS), dd_skip,
              dnw, dbg0, dbg1, dln1_g, dln1_b, dln2_g, dln2_b]
    widths = [p.shape[1] for p in pieces]
    total = sum(widths)
    P = -(-total // LANES) * LANES
    packed = jnp.concatenate(pieces + [jnp.zeros((1, P - total), F32)], axis=1)
    summed = _all_sum_small(packed)
    offs = [0]
    for wd in widths:
        offs.append(offs[-1] + wd)
    sm = [summed[:, offs[k]:offs[k + 1]] for k in range(len(pieces))]
    g_bf, g_cw_full, g_cb, g_dtb, g_al, g_ds, g_nw = sm[0], sm[1].reshape(SSM_CONV, SSM_CONV_DIM), sm[2], sm[3], sm[4], sm[5], sm[6]
    g_bg = jnp.concatenate([sm[7], sm[8]], axis=1)
    g_l1g, g_l1b, g_l2g, g_l2b = sm[9], sm[10], sm[11], sm[12]
    cshard = SSM_CONV_DIM // 4
    g_cw_shard = lax.dynamic_slice_in_dim(g_cw_full, chip * cshard, cshard, axis=1)
    small_names = ["b_forget", "conv_w", "conv_b", "dt_bias", "a_log", "d_skip", "ssm_norm_w", "b_gates", "ln1_g", "ln1_b",
                   "ln2_g", "ln2_b"]
    small_g = [g_bf, g_cw_shard.reshape(1, -1), g_cb, g_dtb, g_al, g_ds, g_nw, g_bg, g_l1g, g_l1b, g_l2g, g_l2b]
    small_w = [b_forget, conv_w[0].reshape(1, -1), conv_b, dt_bias, a_log, d_skip, ssm_norm_w, b_gates, ln1_g, ln1_b, ln2_g, ln2_b]
    small_m = [m_b_forget, m_conv_w[0].reshape(1, -1), m_conv_b, m_dt_bias, m_a_log, m_d_skip, m_ssm_norm_w, m_b_gates, m_ln1_g,
               m_ln1_b, m_ln2_g, m_ln2_b]
    small_v = [v_b_forget, v_conv_w[0].reshape(1, -1), v_conv_b, v_dt_bias, v_a_log, v_d_skip, v_ssm_norm_w, v_b_gates, v_ln1_g,
               v_ln1_b, v_ln2_g, v_ln2_b]
    sw = [a.shape[1] for a in small_w]
    stot = sum(sw)
    SP = -(-stot // LANES) * LANES

    def pack(parts):
        return jnp.concatenate(list(parts) + [jnp.zeros((1, SP - stot), F32)], axis=1).reshape(SP // LANES, LANES)

    sres = _adamw("adamw_small", pack(small_w), pack(small_m), pack(small_v), pack(small_g), SP // LANES)
    soffs = [0]
    for wd in sw:
        soffs.append(soffs[-1] + wd)
    smalls = {}
    for k, nm in enumerate(small_names):
        vals = [r.reshape(1, SP)[:, soffs[k]:soffs[k + 1]] for r in sres]
        if nm == "conv_w":
            vals = [v_.reshape(1, SSM_CONV, cshard) for v_ in vals]
        smalls[nm] = vals

    order = ["w_in", "b_forget", "conv_w", "conv_b", "dt_bias", "a_log", "d_skip", "ssm_norm_w", "w_proj_attn", "w_proj_ssm",
             "b_gates", "w_out", "ln1_g", "ln1_b", "w_ffn_gate", "w_ffn_up", "w_ffn_down", "ln2_g", "ln2_b"]
    allres = {**big, **smalls}
    outs = [loss, grad_x[None]]
    for idx in range(4):
        outs += [allres[nm][idx] for nm in order]
    return tuple(outs)
```

```python
import functools
import math

import jax
import jax.numpy as jnp
from jax import lax
from jax.experimental import pallas as pl
from jax.experimental.pallas import tpu as pltpu

F32, BF16 = jnp.float32, jnp.bfloat16
MESH = pl.DeviceIdType.MESH

D_MODEL = 1024
ATT_HEADS, ATT_HEAD_DIM = 16, 64
SSM_INNER, SSM_HEADS, SSM_GROUPS, SSM_STATE, SSM_CONV = 2048, 32, 4, 128, 4
SSM_CONV_DIM = SSM_INNER + 2 * SSM_GROUPS * SSM_STATE
GROUP_LANES = SSM_INNER // SSM_GROUPS
FFN_HIDDEN = 2816
ALPHA = 2.0 ** 0.25
LN_EPS = 1e-5
RMS_EPS = 1e-5
ADAM_LR, ADAM_B1, ADAM_B2, ADAM_EPS, ADAM_WD, ADAM_STEP = 0.001, 0.9, 0.999, 1e-08, 0.01, 10
IN_SIZES = (1024, 1024, 1024, 16, 2048, 3072, 32, 2048)
IN_WIDTH = sum(IN_SIZES)
RE_WIDTH = 3072 + 2048 + 3072 + 2048 + 128
RE_Z, RE_XBC, RE_GATE, RE_SMALL = 3072, 5120, 8192, 10240

LANES = 128
VMEM_CAP = 60 * 1024 * 1024
NEG = -1e30
TILES = dict(TM=1024, TM2=256, TA=512, AQF=2048, LC=256, TS=2048, TB=256)


def _params(n_axes, vmem_bytes=None):
    return pltpu.CompilerParams(dimension_semantics=("arbitrary",) * n_axes,
                                vmem_limit_bytes=None if vmem_bytes is None else int(min(vmem_bytes, VMEM_CAP)))


def _sigmoid(v):
    return 1.0 / (1.0 + jnp.exp(-v))


def _softplus(v):
    return jnp.maximum(v, 0.0) + jnp.log(1.0 + jnp.exp(-jnp.abs(v)))


def _dot(a, b):
    return lax.dot_general(a, b, (((1,), (0,)), ((), ())), preferred_element_type=F32)


def _dot_nt(a, b):
    return lax.dot_general(a, b, (((1,), (1,)), ((), ())), preferred_element_type=F32)


def _dot_tn(a, b):
    return lax.dot_general(a, b, (((0,), (0,)), ((), ())), preferred_element_type=F32)


def _split3(v):
    hi = v.astype(BF16)
    r1 = v - hi.astype(F32)
    mid = r1.astype(BF16)
    lo = (r1 - mid.astype(F32)).astype(BF16)
    return hi, mid, lo


def _dot_exact_left(m01, v):
    hi, mid, lo = _split3(v)
    return _dot(m01, hi) + _dot(m01, mid) + _dot(m01, lo)


def _dot_exact_right(v, m01, terms=3):
    parts = _split3(v)[:terms]
    out = _dot(parts[0], m01)
    for p in parts[1:]:
        out = out + _dot(p, m01)
    return out


def _mm(name, M, N, tm, tn, lhs, rhs, pairs, e_fn, outs, *, nt=False, rows=(), vecs_n=(), sums=(), after=()):
    ni, nj = M // tm, N // tn
    assert ni * tm == M and nj * tn == N, (name, M, N, tm, tn)
    n_l, n_r, n_row, n_vn, n_o, n_s = len(lhs), len(rhs), len(rows), len(vecs_n), len(outs), len(sums)

    def body(*refs):
        pos = 0
        l_refs = refs[pos:pos + n_l]; pos += n_l
        r_refs = refs[pos:pos + n_r]; pos += n_r
        row_refs = refs[pos:pos + n_row]; pos += n_row
        vn_refs = refs[pos:pos + n_vn]; pos += n_vn + len(after)
        o_refs = refs[pos:pos + n_o]; pos += n_o
        s_refs = refs[pos:pos + n_s]; pos += n_s
        i, j = pl.program_id(0), pl.program_id(1)
        accs = []
        for li, ri in pairs:
            accs.append(_dot_nt(l_refs[li][...], r_refs[ri][...]) if nt else _dot(l_refs[li][...], r_refs[ri][...]))
        out_vals, sum_vals = e_fn(accs, [r[...] for r in row_refs], [r[...] for r in vn_refs], j)
        for r, v in zip(o_refs, out_vals):
            r[...] = v.astype(r.dtype)
        if n_s:
            col = pl.multiple_of(j * tn, LANES)

            @pl.when(i == 0)
            def _():
                for r, v in zip(s_refs, sum_vals):
                    r[:, pl.ds(col, tn)] = v

            @pl.when(i > 0)
            def _():
                for r, v in zip(s_refs, sum_vals):
                    r[:, pl.ds(col, tn)] += v

    in_specs, args, est = [], [], 0
    for arr, width, cb in lhs:
        in_specs.append(pl.BlockSpec((tm, width), lambda i, j, cb=cb: (i, cb)))
        args.append(arr); est += tm * width * arr.dtype.itemsize
    for arr, off, *ksub in rhs:
        if nt:
            kb, kw = ksub if ksub else (0, arr.shape[1])
            in_specs.append(pl.BlockSpec((tn, kw), lambda i, j, off=off, kb=kb: (j + off, kb)))
            est += tn * kw * arr.dtype.itemsize
        else:
            in_specs.append(pl.BlockSpec((arr.shape[0], tn), lambda i, j, off=off: (0, j + off)))
            est += tn * arr.shape[0] * arr.dtype.itemsize
        args.append(arr)
    for arr, off in rows:
        in_specs.append(pl.BlockSpec((tm, tn), lambda i, j, off=off: (i, j + off)))
        args.append(arr); est += tm * tn * arr.dtype.itemsize
    for arr, off in vecs_n:
        in_specs.append(pl.BlockSpec((1, tn), lambda i, j, off=off: (0, j + off)))
        args.append(arr); est += 8 * tn * 4
    for arr in after:
        in_specs.append(pl.BlockSpec(memory_space=pl.ANY))
        args.append(arr)
    out_shape, out_specs = [], []
    for total, dtype, off in outs:
        out_shape.append(jax.ShapeDtypeStruct((M, total), dtype))
        out_specs.append(pl.BlockSpec((tm, tn), lambda i, j, off=off: (i, j + off)))
        est += tm * tn * jnp.dtype(dtype).itemsize
    for total in sums:
        out_shape.append(jax.ShapeDtypeStruct((1, total), F32))
        out_specs.append(pl.BlockSpec((1, total), lambda i, j: (0, 0)))
        est += 8 * total * 4
    vmem = 2 * est + (len(pairs) + 4) * tm * tn * 4 + (8 << 20)
    return pl.pallas_call(body, name=name, grid=(ni, nj), in_specs=in_specs, out_specs=out_specs, out_shape=out_shape,
                          compiler_params=_params(2, vmem))(*args)


def _mm_tn(name, a, g, ta, tn, ts, a_cols=None, a_off=0):
    S = a.shape[0]
    Ka = a.shape[1] if a_cols is None else a_cols
    N = g.shape[1]
    assert Ka % ta == 0 and N % tn == 0 and S % ts == 0, (name, Ka, N, S)
    aoff = a_off // ta

    def body(a_ref, g_ref, o_ref):
        s = pl.program_id(2)
        part = _dot_tn(a_ref[...], g_ref[...])

        @pl.when(s == 0)
        def _():
            o_ref[...] = part

        @pl.when(s > 0)
        def _():
            o_ref[...] += part

    vmem = 2 * (ts * ta * 2 + ts * tn * 2 + ta * tn * 4) + 2 * ta * tn * 4 + (8 << 20)
    return pl.pallas_call(
        body, name=name, grid=(Ka // ta, N // tn, S // ts),
        in_specs=[pl.BlockSpec((ts, ta), lambda ia, jn, s: (s, ia + aoff)), pl.BlockSpec((ts, tn), lambda ia, jn, s: (s, jn))],
        out_specs=pl.BlockSpec((ta, tn), lambda ia, jn, s: (ia, jn)),
        out_shape=jax.ShapeDtypeStruct((Ka, N), F32), compiler_params=_params(3, vmem))(a, g)


def _tri(n, upper):
    r = lax.broadcasted_iota(jnp.int32, (n, n), 0)
    c = lax.broadcasted_iota(jnp.int32, (n, n), 1)
    return jnp.where((c >= r) if upper else (c <= r), 1.0, 0.0).astype(BF16)


def _logsig(v):
    return jnp.minimum(v, 0.0) - jnp.log(1.0 + jnp.exp(-jnp.abs(v)))


def _cum_fwd(small, bvec, tb):
    S = small.shape[0]

    def body(x_ref, b_ref, o_ref, carry):
        i = pl.program_id(0)

        @pl.when(i == 0)
        def _():
            carry[...] = jnp.zeros_like(carry)

        logf = _logsig(x_ref[...] + b_ref[...])
        cum = _dot_exact_left(_tri(tb, False), logf) + carry[0:1, :]
        o_ref[...] = cum
        carry[0:1, :] = cum[tb - 1:tb, :]

    return pl.pallas_call(
        body, name="cum_fwd", grid=(S // tb,),
        in_specs=[pl.BlockSpec((tb, LANES), lambda i: (i, 0)), pl.BlockSpec((1, LANES), lambda i: (0, 0))],
        out_specs=pl.BlockSpec((tb, LANES), lambda i: (i, 0)), out_shape=jax.ShapeDtypeStruct((S, LANES), F32),
        scratch_shapes=[pltpu.VMEM((8, LANES), F32)], compiler_params=_params(1))(small, bvec)


def _cum_bwd(dcum_k, dcum_q, small, bvec, tb):
    S = small.shape[0]
    nb = S // tb

    def body(dk_ref, dq_ref, x_ref, b_ref, o_ref, s_ref, carry):
        i = pl.program_id(0)

        @pl.when(i == 0)
        def _():
            carry[...] = jnp.zeros_like(carry)
            s_ref[...] = jnp.zeros_like(s_ref)

        rc = _dot_exact_left(_tri(tb, True), dk_ref[...] + dq_ref[...]) + carry[0:1, :]
        dfl = rc * _sigmoid(-(x_ref[...] + b_ref[...]))
        o_ref[...] = dfl
        s_ref[...] += jnp.sum(dfl, axis=0, keepdims=True)
        carry[0:1, :] = rc[0:1, :]

    rev = lambda i: (nb - 1 - i, 0)
    return pl.pallas_call(
        body, name="cum_bwd", grid=(nb,),
        in_specs=[pl.BlockSpec((tb, LANES), rev)] * 3 + [pl.BlockSpec((1, LANES), lambda i: (0, 0))],
        out_specs=[pl.BlockSpec((tb, LANES), rev), pl.BlockSpec((1, LANES), lambda i: (0, 0))],
        out_shape=[jax.ShapeDtypeStruct((S, LANES), F32), jax.ShapeDtypeStruct((1, LANES), F32)],
        scratch_shapes=[pltpu.VMEM((8, LANES), F32)], compiler_params=_params(1))(dcum_k, dcum_q, small, bvec)


N_AUG = 3


def _lane():
    return lax.broadcasted_iota(jnp.int32, (1, LANES), 1)


def _lane_mask():
    return _lane() < ATT_HEAD_DIM


def _aug_base(h):
    return ATT_HEAD_DIM * (1 - h)


def _attn_prep(qkv, cum_cols, T):
    S = qkv.shape[0]
    HP = ATT_HEADS // 2

    def body(q_ref, k_ref, c_ref, qa_ref, ka_ref):
        lane = _lane()
        q = q_ref[...].astype(F32)
        k = k_ref[...].astype(F32)
        for h in (0, 1):
            base = _aug_base(h)
            own = (lane < ATT_HEAD_DIM) if h == 0 else (lane >= ATT_HEAD_DIM)
            terms = [t.astype(F32) for t in _split3(c_ref[0, :, h:h + 1])]
            qa = jnp.where(lane == base + N_AUG, 0.0, jnp.where((lane >= base) & (lane < base + N_AUG), 1.0, q))
            ka = jnp.where(lane == base + N_AUG, 1.0, jnp.where(own, k, 0.0))
            for t in range(N_AUG):
                ka = jnp.where(lane == base + t, -terms[t], ka)
            qa_ref[:, h * LANES:(h + 1) * LANES] = qa.astype(BF16)
            ka_ref[:, h * LANES:(h + 1) * LANES] = ka.astype(BF16)

    return pl.pallas_call(
        body, name="attn_prep", grid=(S // T, HP),
        in_specs=[pl.BlockSpec((T, LANES), lambda i, hp: (i, hp)), pl.BlockSpec((T, LANES), lambda i, hp: (i, HP + hp)),
                  pl.BlockSpec((1, T, 2), lambda i, hp: (hp, i, 0))],
        out_specs=[pl.BlockSpec((T, 2 * LANES), lambda i, hp: (i, hp))] * 2,
        out_shape=[jax.ShapeDtypeStruct((S, 2 * D_MODEL), BF16)] * 2, compiler_params=_params(2))(qkv, qkv, cum_cols)


def _attn_fwd(qa, ka, qkv, T, TK):
    S = qkv.shape[0]
    nq = S // T
    r = T // TK
    HP = ATT_HEADS // 2

    def body(q0_ref, q1_ref, k0_ref, k1_ref, v_ref, o_ref, o32_ref, lse_ref):
        i = pl.program_id(1)
        qs = (q0_ref[...], q1_ref[...])
        k_refs = (k0_ref, k1_ref)
        row = lax.broadcasted_iota(jnp.int32, (TK, T), 0)
        col = lax.broadcasted_iota(jnp.int32, (TK, T), 1)
        head_rows = lax.broadcasted_iota(jnp.int32, (LANES, 1), 0) < ATT_HEAD_DIM

        def block(j, carry, q0):
            off = pl.multiple_of(j * TK, TK)
            vj = v_ref[pl.ds(off, TK), :]
            full = q0 is None
            q0 = 0 if full else q0
            m0, l0, m1, l1, acc = carry
            new, alphas, pvs = [], [], []
            for h, (m, l) in enumerate(((m0, l0), (m1, l1))):
                st = _dot_nt(k_refs[h][pl.ds(off, TK), :], qs[h][q0:, :])
                if not full:
                    st = jnp.where(row[:, :T - q0] <= col[:, :T - q0], st, NEG)
                m_old, l_old = m[:, q0:], l[:, q0:]
                m_new = jnp.maximum(m_old, jnp.max(st, axis=0, keepdims=True))
                p = jnp.exp(st - m_new)
                alpha = jnp.exp(m_old - m_new)
                l_new = alpha * l_old + jnp.sum(p, axis=0, keepdims=True)
                pvs.append(_dot_tn(vj, p.astype(BF16)))
                alphas.append(alpha)
                new += [m_new, l_new]
            part = acc[:, q0:] * jnp.where(head_rows, alphas[0], alphas[1]) + jnp.where(head_rows, pvs[0], pvs[1])
            if q0:
                keep = lambda old, upd: jnp.concatenate([old[:, :q0], upd], axis=1)
                return (keep(m0, new[0]), keep(l0, new[1]), keep(m1, new[2]), keep(l1, new[3]), keep(acc, part))
            return (new[0], new[1], new[2], new[3], part)

        init = (jnp.full((1, T), NEG, F32), jnp.zeros((1, T), F32), jnp.full((1, T), NEG, F32), jnp.zeros((1, T), F32),
                jnp.zeros((LANES, T), F32))
        n_full = i * r
        carry = lax.fori_loop(0, n_full // 2, lambda jj, c: block(2 * jj + 1, block(2 * jj, c, None), None), init)
        carry = lax.cond(n_full % 2 == 1, lambda c: block(n_full - 1, c, None), lambda c: c, carry)
        for d in range(r):
            carry = block(n_full + d, carry, d * TK)
        m0, l0, m1, l1, acc = carry
        out = (acc / jnp.where(head_rows, l0, l1)).T
        o_ref[...] = out.astype(BF16)
        o32_ref[...] = out
        lse_ref[0, 0:1, :] = m0 + jnp.log(l0)
        lse_ref[0, 1:2, :] = m1 + jnp.log(l1)

    vmem = 2 * (2 * T * LANES * 2 + 3 * S * LANES * 2 + T * LANES * (2 + 4) + 8 * T * 4) + 10 * T * TK * 4 + (8 << 20)
    qspec = lambda h: pl.BlockSpec((T, LANES), lambda hp, i, h=h: (i, 2 * hp + h))
    kspec = lambda h: pl.BlockSpec((S, LANES), lambda hp, i, h=h: (0, 2 * hp + h))
    return pl.pallas_call(
        body, name="attn_fwd", grid=(HP, nq),
        in_specs=[qspec(0), qspec(1), kspec(0), kspec(1), pl.BlockSpec((S, LANES), lambda hp, i: (0, 2 * HP + hp))],
        out_specs=[pl.BlockSpec((T, LANES), lambda hp, i: (i, hp)), pl.BlockSpec((T, LANES), lambda hp, i: (i, hp)),
                   pl.BlockSpec((1, 2, T), lambda hp, i: (hp, 0, i))],
        out_shape=[jax.ShapeDtypeStruct((S, D_MODEL), BF16), jax.ShapeDtypeStruct((S, D_MODEL), F32),
                   jax.ShapeDtypeStruct((HP, 2, S), F32)],
        compiler_params=_params(2, vmem))(qa, qa, ka, ka, qkv)


def _attn_stats(do, o32, lse_rows, T):
    S = do.shape[0]
    HP = ATT_HEADS // 2

    def body(do_ref, o_ref, lse_ref, st_ref):
        r = lax.broadcasted_iota(jnp.int32, (8, LANES), 0)
        lane = lax.broadcasted_iota(jnp.int32, (8, LANES), 1)
        sel = jnp.where(((r == 2) & (lane < ATT_HEAD_DIM)) | ((r == 3) & (lane >= ATT_HEAD_DIM)), 1.0, 0.0).astype(BF16)
        hi, mid, lo = _split3(do_ref[...].astype(F32) * o_ref[...])
        st_ref[0] = _dot_nt(sel, hi) + _dot_nt(sel, mid) + _dot_nt(sel, lo)
        st_ref[0, 0:2, :] = lse_ref[0]

    return pl.pallas_call(
        body, name="attn_stats", grid=(HP, S // T),
        in_specs=[pl.BlockSpec((T, LANES), lambda hp, i: (i, hp)), pl.BlockSpec((T, LANES), lambda hp, i: (i, hp)),
                  pl.BlockSpec((1, 2, T), lambda hp, i: (hp, 0, i))],
        out_specs=pl.BlockSpec((1, 8, T), lambda hp, i: (hp, 0, i)), out_shape=jax.ShapeDtypeStruct((HP, 8, S), F32),
        compiler_params=_params(2))(do, o32, lse_rows)


def _attn_bwd(qa, ka, qkv, do, stats_rows, T):
    S = qkv.shape[0]
    nq = S // T
    HP = ATT_HEADS // 2

    def body(k0_ref, k1_ref, v_ref, q0_ref, q1_ref, do_ref, st_ref, dq_ref, dk_ref, dv_ref, dck_ref, dcq_ref, dq_acc):
        j = pl.program_id(1)
        mA = _lane_mask()
        masks = (mA, jnp.logical_not(mA))
        q_refs = (q0_ref, q1_ref)

        @pl.when(j == 0)
        def _():
            dq_acc[...] = jnp.zeros_like(dq_acc)

        kas = (k0_ref[...], k1_ref[...])
        vj = v_ref[...]
        row = lax.broadcasted_iota(jnp.int32, (T, T), 0)
        col = lax.broadcasted_iota(jnp.int32, (T, T), 1)

        def block(i, carry, diag):
            dv_acc, dk0, dk1 = carry
            off = pl.multiple_of(i * T, T)
            doi = do_ref[pl.ds(off, T), :]
            zero = jnp.zeros_like(doi)
            dks = [dk0, dk1]
            for h in (0, 1):
                qh = q_refs[h][pl.ds(off, T), :]
                doh = jnp.where(masks[h], doi, zero)
                lse = st_ref[0, h:h + 1, pl.ds(off, T)]
                dd = st_ref[0, 2 + h:3 + h, pl.ds(off, T)]
                st = _dot_nt(kas[h], qh)
                if diag:
                    st = jnp.where(row <= col, st, NEG)
                pt = jnp.exp(st - lse)
                dpt = _dot_nt(vj, doh)
                dst = (pt * (dpt - dd)).astype(BF16)
                dv_acc = dv_acc + _dot(pt.astype(BF16), doh)
                dks[h] = dks[h] + _dot(dst, qh)
                dq_acc[h, :, pl.ds(off, T)] += _dot_tn(kas[h], dst)
            return (dv_acc, dks[0], dks[1])

        z = jnp.zeros((T, LANES), F32)
        carry = block(j, (z, z, z), True)
        dv_acc, dk0, dk1 = lax.fori_loop(j + 1, nq, lambda i, c: block(i, c, False), carry)
        dv_ref[...] = dv_acc.astype(BF16)
        dk_ref[...] = jnp.where(mA, dk0, dk1).astype(BF16)
        ones_q = (_aug_base(0), _aug_base(1))
        dck_ref[0, :, 0:1] = -dk0[:, ones_q[0]:ones_q[0] + 1]
        dck_ref[0, :, 1:2] = -dk1[:, ones_q[1]:ones_q[1] + 1]

        @pl.when(j == nq - 1)
        def _():
            dq0, dq1 = dq_acc[0].T, dq_acc[1].T
            ones_k = (_aug_base(0) + N_AUG, _aug_base(1) + N_AUG)
            dq_ref[...] = (jnp.where(mA, dq0, dq1) * (1.0 / math.sqrt(ATT_HEAD_DIM))).astype(BF16)
            dcq_ref[0, :, 0:1] = dq0[:, ones_k[0]:ones_k[0] + 1]
            dcq_ref[0, :, 1:2] = dq1[:, ones_k[1]:ones_k[1] + 1]

    vmem = (2 * (3 * T * LANES * 2 + 3 * S * LANES * 2 + 8 * S * 4 + S * LANES * (2 + 4) + 2 * T * LANES * 2 + T * LANES * 4)
            + 2 * S * LANES * 4 + 12 * T * T * 4 + (8 << 20))
    kspec = lambda h: pl.BlockSpec((T, LANES), lambda hp, j, h=h: (j, 2 * hp + h))
    qspec = lambda h: pl.BlockSpec((S, LANES), lambda hp, j, h=h: (0, 2 * hp + h))
    blk = pl.BlockSpec((T, LANES), lambda hp, j: (j, hp))
    full = pl.BlockSpec((S, LANES), lambda hp, j: (0, hp))
    return pl.pallas_call(
        body, name="attn_bwd", grid=(HP, nq),
        in_specs=[kspec(0), kspec(1), pl.BlockSpec((T, LANES), lambda hp, j: (j, 2 * HP + hp)), qspec(0), qspec(1), full,
                  pl.BlockSpec((1, 8, S), lambda hp, j: (hp, 0, 0))],
        out_specs=[full, blk, blk, pl.BlockSpec((1, T, 2), lambda hp, j: (hp, j, 0)),
                   pl.BlockSpec((1, S, 2), lambda hp, j: (hp, 0, 0))],
        out_shape=[jax.ShapeDtypeStruct((S, D_MODEL), BF16)] * 3 + [jax.ShapeDtypeStruct((HP, S, 2), F32)] * 2,
        scratch_shapes=[pltpu.VMEM((2, LANES, S), F32)], compiler_params=_params(2, vmem))(ka, ka, qkv, qa, qa, do, stats_rows)


HALO = 8


def _conv_fwd(u, w, b, ts, tc):
    S, C = u.shape
    hb = ts // HALO

    def body(u_ref, prev_ref, w_ref, b_ref, o_ref, ext):
        i = pl.program_id(0)
        ext[0:HALO, :] = jnp.where(i == 0, 0.0, prev_ref[...])
        ext[HALO:HALO + ts, :] = u_ref[...]
        acc = b_ref[...] + w_ref[3:4, :] * u_ref[...]
        for k in range(SSM_CONV - 1):
            d = SSM_CONV - 1 - k
            acc = acc + w_ref[k:k + 1, :] * ext[HALO - d:HALO - d + ts, :]
        o_ref[...] = acc * _sigmoid(acc)

    return pl.pallas_call(
        body, name="conv_fwd", grid=(S // ts, C // tc),
        in_specs=[pl.BlockSpec((ts, tc), lambda i, j: (i, j)),
                  pl.BlockSpec((HALO, tc), lambda i, j: (jnp.maximum(i * hb - 1, 0), j)),
                  pl.BlockSpec((SSM_CONV, tc), lambda i, j: (0, j)), pl.BlockSpec((1, tc), lambda i, j: (0, j))],
        out_specs=pl.BlockSpec((ts, tc), lambda i, j: (i, j)), out_shape=jax.ShapeDtypeStruct((S, C), F32),
        scratch_shapes=[pltpu.VMEM((ts + HALO, tc), F32)], compiler_params=_params(2))(u, u, w, b)


def _conv_bwd(name, u, dy, w, b, ts, tc, col0):
    S, C = dy.shape
    cb = col0 // tc
    assert cb * tc == col0
    hb = ts // HALO
    nb = S // ts
    E = ts + 2 * HALO

    def body(u_ref, uprev_ref, unext_ref, dy_ref, dynext_ref, w_ref, b_ref, du_ref, dw_ref, db_ref, uext, gext):
        i = pl.program_id(1)
        last = i == nb - 1
        uext[0:HALO, :] = jnp.where(i == 0, 0.0, uprev_ref[...])
        uext[HALO:HALO + ts, :] = u_ref[...]
        uext[HALO + ts:E, :] = unext_ref[...]
        n = ts + HALO
        pre = b_ref[...] + w_ref[3:4, :] * uext[HALO:HALO + n, :]
        for k in range(SSM_CONV - 1):
            d = SSM_CONV - 1 - k
            pre = pre + w_ref[k:k + 1, :] * uext[HALO - d:HALO - d + n, :]
        sg = _sigmoid(pre)
        dsilu = sg * (1.0 + pre * (1.0 - sg))
        gext[0:ts, :] = dy_ref[...] * dsilu[0:ts, :]
        gext[ts:n, :] = jnp.where(last, 0.0, dynext_ref[...] * dsilu[ts:n, :])
        g = gext[0:ts, :]
        du = w_ref[3:4, :] * g
        for k in range(SSM_CONV - 1):
            d = SSM_CONV - 1 - k
            du = du + w_ref[k:k + 1, :] * gext[d:d + ts, :]
        du_ref[...] = du.astype(du_ref.dtype)
        dws = [jnp.sum(g * uext[HALO - (SSM_CONV - 1 - k):HALO - (SSM_CONV - 1 - k) + ts, :], axis=0, keepdims=True)
               for k in range(SSM_CONV)]
        dbs = jnp.sum(g, axis=0, keepdims=True)

        @pl.when(i == 0)
        def _():
            for k in range(SSM_CONV):
                dw_ref[k:k + 1, :] = dws[k]
            db_ref[...] = dbs

        @pl.when(i > 0)
        def _():
            for k in range(SSM_CONV):
                dw_ref[k:k + 1, :] += dws[k]
            db_ref[...] += dbs

    nxt = lambda off: (lambda j, i: (jnp.minimum((i + 1) * hb, S // HALO - 1), j + off))
    return pl.pallas_call(
        body, name=name, grid=(C // tc, nb),
        in_specs=[pl.BlockSpec((ts, tc), lambda j, i: (i, j + cb)),
                  pl.BlockSpec((HALO, tc), lambda j, i: (jnp.maximum(i * hb - 1, 0), j + cb)),
                  pl.BlockSpec((HALO, tc), nxt(cb)),
                  pl.BlockSpec((ts, tc), lambda j, i: (i, j)),
                  pl.BlockSpec((HALO, tc), nxt(0)),
                  pl.BlockSpec((SSM_CONV, tc), lambda j, i: (0, j + cb)), pl.BlockSpec((1, tc), lambda j, i: (0, j + cb))],
        out_specs=[pl.BlockSpec((ts, tc), lambda j, i: (i, j)), pl.BlockSpec((SSM_CONV, tc), lambda j, i: (0, j)),
                   pl.BlockSpec((1, tc), lambda j, i: (0, j))],
        out_shape=[jax.ShapeDtypeStruct((S, C), BF16), jax.ShapeDtypeStruct((SSM_CONV, C), F32), jax.ShapeDtypeStruct((1, C), F32)],
        scratch_shapes=[pltpu.VMEM((E, tc), F32), pltpu.VMEM((ts + HALO, tc), F32)],
        compiler_params=_params(2))(u, u, u, dy, dy, w, b)


def _head_sum():
    lane = jnp.right_shift(lax.broadcasted_iota(jnp.int32, (GROUP_LANES, 8), 0), 6)
    r = lax.broadcasted_iota(jnp.int32, (GROUP_LANES, 8), 1)
    return jnp.where(lane == r, 1.0, 0.0).astype(BF16)


def _head_expand():
    r = lax.broadcasted_iota(jnp.int32, (8, GROUP_LANES), 0)
    c = jnp.right_shift(lax.broadcasted_iota(jnp.int32, (8, GROUP_LANES), 1), 6)
    return jnp.where(r == c, 1.0, 0.0).astype(BF16)


def _ssd_common(dtc_ref, dtr_ref, bias_r, alog_b, bias_c, alog_c, L):
    a_b = -jnp.exp(alog_b)
    dt = _dot_exact_right(_softplus(dtc_ref[0] + bias_r), _head_expand())
    acum = _dot_exact_left(_tri(L, False), dt * a_b)
    a_c = -jnp.exp(alog_c)
    dtr = _softplus(dtr_ref[0] + bias_c)
    acum_r = _dot_exact_right(dtr * a_c, _tri(L, True))
    return a_b, dt, acum, acum_r


def _ssd_specs(L, nc, rev):
    cc = (lambda c: nc - 1 - c) if rev else (lambda c: c)
    G = SSM_GROUPS
    blk = pl.BlockSpec((L, GROUP_LANES), lambda g, c: (cc(c), g))
    dtc = pl.BlockSpec((1, L, 8), lambda g, c: (g, cc(c), 0))
    rowv = pl.BlockSpec((1, 1, 8), lambda g, c: (g, 0, 0))
    xs = blk
    bm = pl.BlockSpec((L, SSM_STATE), lambda g, c: (cc(c), SSM_INNER // SSM_STATE + g))
    cm = pl.BlockSpec((L, SSM_STATE), lambda g, c: (cc(c), SSM_INNER // SSM_STATE + G + g))
    dtr = pl.BlockSpec((1, 8, L), lambda g, c: (g, 0, cc(c)))
    vec = pl.BlockSpec((1, GROUP_LANES), lambda g, c: (0, g))
    colv = pl.BlockSpec((1, 8, 1), lambda g, c: (g, 0, 0))
    hs = pl.BlockSpec((1, 1, SSM_STATE, GROUP_LANES), lambda g, c: (g, cc(c), 0, 0))
    return blk, xs, bm, cm, dtc, dtr, vec, rowv, colv, hs


def _ssd_fwd(xbc, z, dtc, dtr, bias_r, alog_b, dskip_b, normw, bias_c, alog_c, L):
    S = z.shape[0]
    nc = S // L
    blk, xs, bm, cm, dtcs, dtrs, vec, rowv, colv, hs = _ssd_specs(L, nc, False)

    def body(x_ref, b_ref, c_ref, z_ref, dtc_ref, dtr_ref, bias_ref, alog_ref, dskip_ref, nw_ref, biasc_ref, alogc_ref,
             y_ref, ssm_ref, hs_ref, h_scr):
        c = pl.program_id(1)

        @pl.when(c == 0)
        def _():
            h_scr[...] = jnp.zeros_like(h_scr)

        mA = _lane_mask()
        a_b, dt, acum, acum_r = _ssd_common(dtc_ref, dtr_ref, bias_ref[0], alog_ref[...], biasc_ref[0], alogc_ref[0], L)
        x = x_ref[...]
        cb, bb = c_ref[...].astype(BF16), b_ref[...].astype(BF16)
        hprev = h_scr[...]
        hs_ref[0, 0] = hprev
        xdt = x * dt
        xdt_b = xdt.astype(BF16)
        gmat = _dot_nt(cb, bb)
        row = lax.broadcasted_iota(jnp.int32, (L, L), 0)
        col = lax.broadcasted_iota(jnp.int32, (L, L), 1)
        parts = []
        for p in range(GROUP_LANES // LANES):
            xp = xdt_b[:, p * LANES:(p + 1) * LANES]
            yd = []
            for hh in (0, 1):
                r = 2 * p + hh
                acol = acum[:, r * ATT_HEAD_DIM:r * ATT_HEAD_DIM + 1]
                arow = acum_r[r:r + 1, :]
                lm = jnp.exp(jnp.where(row >= col, acol - arow, NEG))
                yd.append(_dot((gmat * lm).astype(BF16), xp))
            parts.append(jnp.where(mA, yd[0], yd[1]))
        ydiag = jnp.concatenate(parts, axis=1)
        yoff = jnp.exp(acum) * _dot(cb, hprev.astype(BF16))
        y = ydiag + yoff + dskip_ref[...] * x
        aend = acum[L - 1:L, :]
        wgt = (jnp.exp(aend - acum) * xdt).astype(BF16)
        h_scr[...] = jnp.exp(aend) * hprev + _dot_tn(bb, wgt)
        y_ref[...] = y
        zz = z_ref[...]
        u = y * (zz * _sigmoid(zz))
        rs = lax.rsqrt(jnp.mean(u * u, axis=1, keepdims=True) + RMS_EPS)
        ssm_ref[...] = (u * rs * nw_ref[...]).astype(BF16)

    return pl.pallas_call(
        body, name="ssd_fwd", grid=(SSM_GROUPS, nc),
        in_specs=[xs, bm, cm, blk, dtcs, dtrs, rowv, vec, vec, vec, colv, colv],
        out_specs=[blk, blk, hs],
        out_shape=[jax.ShapeDtypeStruct((S, SSM_INNER), F32), jax.ShapeDtypeStruct((S, SSM_INNER), BF16),
                   jax.ShapeDtypeStruct((SSM_GROUPS, nc, SSM_STATE, GROUP_LANES), F32)],
        scratch_shapes=[pltpu.VMEM((SSM_STATE, GROUP_LANES), F32)],
        compiler_params=_params(2, 48 << 20))(xbc, xbc, xbc, z, dtc, dtr, bias_r, alog_b, dskip_b, normw, bias_c, alog_c)


def _ssd_bwd(xbc, z, y, dssm, hs_all, dtc, dtr, bias_r, alog_r, alog_b, dskip_b, normw, bias_c, alog_c, L):
    S = z.shape[0]
    nc = S // L
    blk, xs, bm, cm, dtcs, dtrs, vec, rowv, colv, hs = _ssd_specs(L, nc, True)

    def body(x_ref, b_ref, c_ref, z_ref, y_ref, dssm_ref, hs_ref, dtc_ref, dtr_ref, bias_ref, alogr_ref, alog_ref, dskip_ref, nw_ref,
             biasc_ref, alogc_ref,
             dx_ref, db_ref, dc_ref, dz_ref, ddt_ref, dnw_ref, ddskip_ref, dalog_ref, dbias_ref, dh_scr):
        c = pl.program_id(1)

        @pl.when(c == 0)
        def _():
            dh_scr[...] = jnp.zeros_like(dh_scr)

        mA = _lane_mask()
        masks = (mA, jnp.logical_not(mA))
        a_b, dt, acum, acum_r = _ssd_common(dtc_ref, dtr_ref, bias_ref[0], alog_ref[...], biasc_ref[0], alogc_ref[0], L)
        x, zz, y, dssm = x_ref[...], z_ref[...], y_ref[...], dssm_ref[...]
        cb, bb = c_ref[...].astype(BF16), b_ref[...].astype(BF16)
        hprev = hs_ref[0, 0]
        hb = hprev.astype(BF16)
        ds = dh_scr[...]
        dsb = ds.astype(BF16)
        dskip = dskip_ref[...]
        aend = acum[L - 1:L, :]
        e_a, e_end = jnp.exp(acum), jnp.exp(aend)
        dte = jnp.exp(aend - acum)
        xdt = x * dt
        xdt_b = xdt.astype(BF16)
        sg = _sigmoid(zz)
        sz = zz * sg
        u = y * sz
        rs = lax.rsqrt(jnp.mean(u * u, axis=1, keepdims=True) + RMS_EPS)
        un = u * rs
        dun = dssm * nw_ref[...]
        du = rs * (dun - un * jnp.mean(dun * un, axis=1, keepdims=True))
        dy = du * sz
        dz_ref[...] = (du * y * sg * (1.0 + zz * (1.0 - sg))).astype(dz_ref.dtype)
        dy_b = dy.astype(BF16)
        dch_b = (dy * e_a).astype(BF16)
        dc = _dot_nt(dch_b, hb)
        dhprev = _dot_tn(cb, dch_b)
        gt = _dot_nt(bb, cb)
        row = lax.broadcasted_iota(jnp.int32, (L, L), 0)
        col = lax.broadcasted_iota(jnp.int32, (L, L), 1)
        dgt = jnp.zeros((L, L), F32)
        parts = []
        for p in range(GROUP_LANES // LANES):
            xp = xdt_b[:, p * LANES:(p + 1) * LANES]
            dyp = dy_b[:, p * LANES:(p + 1) * LANES]
            zero = jnp.zeros_like(dyp)
            acc = None
            for hh in (0, 1):
                r = 2 * p + hh
                acol = acum[:, r * ATT_HEAD_DIM:r * ATT_HEAD_DIM + 1]
                arow = acum_r[r:r + 1, :]
                lmt = jnp.exp(jnp.where(row <= col, arow - acol, NEG))
                dyh = jnp.where(masks[hh], dyp, zero)
                part = _dot((gt * lmt).astype(BF16), dyh)
                acc = part if acc is None else acc + part
                dgt = dgt + _dot_nt(xp, dyh) * lmt
            parts.append(acc)
        dxdt_diag = jnp.concatenate(parts, axis=1)
        dgt_b = dgt.astype(BF16)
        db = _dot(dgt_b, cb)
        dc = dc + _dot_tn(dgt_b, bb)
        dxdt_state = dte * _dot(bb, dsb)
        db = db + _dot_nt((dte * xdt).astype(BF16), dsb)
        dxdt = dxdt_diag + dxdt_state
        dy_r, xdt_r = dy_b.astype(F32), xdt_b.astype(F32)
        dac = dy_r * (y - dskip * x) - xdt_r * dxdt
        tail = jnp.sum(xdt_r * dxdt_state, axis=0, keepdims=True) + e_end * jnp.sum(ds * hprev, axis=0, keepdims=True)
        rowl = lax.broadcasted_iota(jnp.int32, (L, 1), 0)
        dac = dac + jnp.where(rowl == L - 1, tail, 0.0)
        rc = _dot_exact_left(_tri(L, True), dac)
        hsum = _head_sum()
        hs1 = _dot_exact_right(dxdt * x, hsum, 2)
        hs2 = _dot_exact_right(rc, hsum, 2)
        a8 = -jnp.exp(alogr_ref[0])
        dtraw8 = dtc_ref[0] + bias_ref[0]
        ddtraw = (hs1 + a8 * hs2) * _sigmoid(dtraw8)
        dx_ref[...] = dskip * dy + dxdt * dt
        db_ref[...] = db
        dc_ref[...] = dc
        ddt_ref[0] = ddtraw
        dh_scr[...] = e_end * ds + dhprev
        sums = (jnp.sum(dssm * un, axis=0, keepdims=True), jnp.sum(dy * x, axis=0, keepdims=True))
        refs = (dnw_ref, ddskip_ref)
        sums8 = (a8 * jnp.sum(hs2 * _softplus(dtraw8), axis=0, keepdims=True), jnp.sum(ddtraw, axis=0, keepdims=True))
        refs8 = (dalog_ref, dbias_ref)

        @pl.when(c == 0)
        def _():
            for r, v in zip(refs, sums):
                r[...] = v
            for r, v in zip(refs8, sums8):
                r[0] = v

        @pl.when(c > 0)
        def _():
            for r, v in zip(refs, sums):
                r[...] += v
            for r, v in zip(refs8, sums8):
                r[0] += v

    nbc = pl.BlockSpec((L, SSM_STATE), lambda g, c: (nc - 1 - c, g))
    return pl.pallas_call(
        body, name="ssd_bwd", grid=(SSM_GROUPS, nc),
        in_specs=[xs, bm, cm, blk, blk, blk, hs, dtcs, dtrs, rowv, rowv, vec, vec, vec, colv, colv],
        out_specs=[blk, nbc, nbc, blk, dtcs, vec, vec, rowv, rowv],
        out_shape=[jax.ShapeDtypeStruct((S, SSM_INNER), F32), jax.ShapeDtypeStruct((S, SSM_GROUPS * SSM_STATE), F32),
                   jax.ShapeDtypeStruct((S, SSM_GROUPS * SSM_STATE), F32), jax.ShapeDtypeStruct((S, SSM_INNER), BF16),
                   jax.ShapeDtypeStruct((SSM_GROUPS, S, 8), F32)] + [jax.ShapeDtypeStruct((1, SSM_INNER), F32)] * 2
                  + [jax.ShapeDtypeStruct((SSM_GROUPS, 1, 8), F32)] * 2,
        scratch_shapes=[pltpu.VMEM((SSM_STATE, GROUP_LANES), F32)],
        compiler_params=_params(2, 56 << 20))(xbc, xbc, xbc, z, y, dssm, hs_all, dtc, dtr, bias_r, alog_r, alog_b, dskip_b,
                                              normw, bias_c, alog_c)


def _place():
    return lax.axis_index("x"), lax.axis_index("y"), lax.axis_index("c")


def _other_chips(x, y):
    return [(1 - x, y), (x, 1 - y), (1 - x, 1 - y)]


def _half_rows(rows, which):
    hr = rows // 2
    if isinstance(which, int):
        return pl.ds(which * hr, hr)
    return pl.ds(pl.multiple_of(which * hr, 8), hr)


def _chip_gather(name, shards, split):
    n = len(shards)
    ANY = pl.BlockSpec(memory_space=pl.ANY)

    def body(*refs):
        ins, outs = refs[:n], refs[n:2 * n]
        send, recv, fsend, frecv = refs[2 * n:]
        x, y, c = _place()
        me = 2 * x + y
        sibling = (x, y, 1 - c)
        chips = _other_chips(x, y)

        def piece(a, chip_idx, which):
            if split[a]:
                return outs[a].at[chip_idx, _half_rows(shards[a].shape[0], which)]
            return outs[a].at[chip_idx]

        def ici(k, a, to_chip, src_chip):
            src = ins[a].at[_half_rows(shards[a].shape[0], c)] if split[a] else ins[a]
            return pltpu.make_async_remote_copy(src_ref=src, dst_ref=piece(a, src_chip, c), send_sem=send.at[k, a],
                                                recv_sem=recv.at[k, a], device_id=(*to_chip, c), device_id_type=MESH)

        def fwd(k, a, src_chip, which):
            return pltpu.make_async_remote_copy(src_ref=piece(a, src_chip, which), dst_ref=piece(a, src_chip, which),
                                                send_sem=fsend.at[k, a], recv_sem=frecv.at[k, a], device_id=sibling,
                                                device_id_type=MESH)

        sends = []
        for k, chip in enumerate(chips):
            for a in range(n):
                cp = ici(k, a, chip, me)
                cp.start()
                sends.append(cp)
        for k, (ox, oy) in enumerate(chips):
            src = 2 * ox + oy
            for a in range(n):
                ici(k, a, (ox, oy), src).wait_recv()
                if split[a]:
                    cp = fwd(k, a, src, c)
                    cp.start()
                    sends.append(cp)
        for k, (ox, oy) in enumerate(chips):
            for a in range(n):
                if split[a]:
                    fwd(k, a, 2 * ox + oy, 1 - c).wait_recv()
        for cp in sends:
            cp.wait_send()

    sem = pltpu.SemaphoreType.DMA((3, n))
    return pl.pallas_call(
        body, name=name, in_specs=[ANY] * n, out_specs=[ANY] * n,
        out_shape=[jax.ShapeDtypeStruct((4,) + s.shape, s.dtype) for s in shards],
        scratch_shapes=[sem, sem, sem, sem])(*shards)


def _chip_copies_start(name, srcs, per_chip_src, after):
    n = len(srcs)
    HBM = pl.BlockSpec(memory_space=pltpu.HBM)
    SEM = pl.BlockSpec(memory_space=pltpu.SEMAPHORE)
    lands = [pltpu.with_memory_space_constraint(lax.empty(a.shape if per_chip_src else (4,) + a.shape, a.dtype), pltpu.HBM)
             for a in srcs]

    def body(*refs):
        ins, land = refs[:n], refs[n:2 * n]
        send, recv = refs[2 * n + 1], refs[2 * n + 2]
        token = refs[-1]
        x, y, c = _place()
        me = 2 * x + y
        for k, (ox, oy) in enumerate(_other_chips(x, y)):
            for a in range(n):
                src = ins[a].at[2 * ox + oy] if per_chip_src else ins[a]
                pltpu.make_async_remote_copy(src_ref=src, dst_ref=land[a].at[me], send_sem=send.at[k * n + a], recv_sem=recv.at[k * n + a],
                                             device_id=(ox, oy, c), device_id_type=MESH).start()
        token[...] = jnp.zeros_like(token)

    sem = pltpu.SemaphoreType.DMA((3 * n,))
    res = pl.pallas_call(
        body, name=name,
        out_shape=[sem, sem] + [pltpu.HBM(a.shape, a.dtype) for a in srcs] + [pltpu.HBM(b.shape, b.dtype) for b in lands]
                  + [jax.ShapeDtypeStruct((8, LANES), F32)],
        in_specs=[HBM] * (2 * n) + [pl.BlockSpec(memory_space=pl.ANY)],
        out_specs=[SEM, SEM] + [HBM] * (2 * n) + [pl.BlockSpec(memory_space=pltpu.VMEM)],
        input_output_aliases={k: 2 + k for k in range(2 * n)},
        compiler_params=pltpu.CompilerParams(has_side_effects=pltpu.SideEffectType.DATAFLOW_SIDE_EFFECTING),
    )(*[pltpu.with_memory_space_constraint(a, pltpu.HBM) for a in srcs], *lands, after)
    return res[:-1], res[-1]


def _chip_copies_wait(name, started, per_chip_src, after):
    send, recv = started[0], started[1]
    n = (len(started) - 2) // 2
    srcs, lands = started[2:2 + n], started[2 + n:]
    HBM = pl.BlockSpec(memory_space=pltpu.HBM)
    SEM = pl.BlockSpec(memory_space=pltpu.SEMAPHORE)

    def body(*refs):
        ins, land = refs[:n], refs[n:2 * n]
        send_sem, recv_sem = refs[2 * n], refs[2 * n + 1]
        x, y, c = _place()
        me = 2 * x + y
        for k, (ox, oy) in enumerate(_other_chips(x, y)):
            for a in range(n):
                src = ins[a].at[me] if per_chip_src else ins[a]
                cp = pltpu.make_async_remote_copy(src_ref=src, dst_ref=land[a].at[2 * ox + oy], send_sem=send_sem.at[k * n + a],
                                                  recv_sem=recv_sem.at[k * n + a], device_id=(ox, oy, c), device_id_type=MESH)
                cp.wait_send()
                cp.wait_recv()

    res = pl.pallas_call(
        body, name=name,
        out_shape=[pltpu.HBM(a.shape, a.dtype) for a in srcs] + [pltpu.HBM(b.shape, b.dtype) for b in lands],
        in_specs=[HBM] * (2 * n) + [SEM, SEM, pl.BlockSpec(memory_space=pl.ANY)], out_specs=[HBM] * (2 * n),
        input_output_aliases={k: k for k in range(2 * n)},
        compiler_params=pltpu.CompilerParams(has_side_effects=pltpu.SideEffectType.DATAFLOW_SIDE_EFFECTING),
    )(*srcs, *lands, send, recv, after)
    return res[n:]


def _half_to_sibling(name, blocks):
    n = len(blocks)
    ANY = pl.BlockSpec(memory_space=pl.ANY)

    def body(*refs):
        ins, outs = refs[:n], refs[n:2 * n]
        send, recv = refs[2 * n:]
        x, y, c = _place()
        cps = [pltpu.make_async_remote_copy(src_ref=ins[a].at[:, _half_rows(blocks[a].shape[1], 1 - c)], dst_ref=outs[a],
                                            send_sem=send.at[a], recv_sem=recv.at[a], device_id=(x, y, 1 - c),
                                            device_id_type=MESH) for a in range(n)]
        for cp in cps:
            cp.start()
        for cp in cps:
            cp.wait_recv()
        for cp in cps:
            cp.wait_send()

    return pl.pallas_call(
        body, name=name, in_specs=[ANY] * n, out_specs=[ANY] * n,
        out_shape=[jax.ShapeDtypeStruct((4, b.shape[1] // 2, b.shape[2]), b.dtype) for b in blocks],
        scratch_shapes=[pltpu.SemaphoreType.DMA((n,)), pltpu.SemaphoreType.DMA((n,))])(*blocks)


def _sibling_swap(name, arrs):
    n = len(arrs)
    ANY = pl.BlockSpec(memory_space=pl.ANY)

    def body(*refs):
        ins, outs = refs[:n], refs[n:2 * n]
        send, recv = refs[2 * n:]
        x, y, c = _place()
        cps = [pltpu.make_async_remote_copy(src_ref=ins[a], dst_ref=outs[a], send_sem=send.at[a], recv_sem=recv.at[a],
                                            device_id=(x, y, 1 - c), device_id_type=MESH) for a in range(n)]
        for cp in cps:
            cp.start()
        for cp in cps:
            cp.wait_recv()
        for cp in cps:
            cp.wait_send()

    return pl.pallas_call(
        body, name=name, in_specs=[ANY] * n, out_specs=[ANY] * n,
        out_shape=[jax.ShapeDtypeStruct(a.shape, a.dtype) for a in arrs],
        scratch_shapes=[pltpu.SemaphoreType.DMA((n,)), pltpu.SemaphoreType.DMA((n,))])(*arrs)


N_DEV = 8


def _all_sum_small(vec):
    P = vec.shape[1]

    def body(v_ref, o_ref, buf, send, recv):
        x, y, c = _place()
        me = 4 * x + 2 * y + c
        buf[me] = v_ref[...]

        def peer(r):
            return ((1 - x) if (r >> 2) & 1 else x, (1 - y) if (r >> 1) & 1 else y, (1 - c) if r & 1 else c)

        sends = []
        for r in range(1, N_DEV):
            cp = pltpu.make_async_remote_copy(src_ref=v_ref, dst_ref=buf.at[me], send_sem=send.at[r], recv_sem=recv.at[r],
                                              device_id=peer(r), device_id_type=MESH)
            cp.start()
            sends.append(cp)
        for r in range(1, N_DEV):
            px, py, pc = peer(r)
            pltpu.make_async_remote_copy(src_ref=v_ref, dst_ref=buf.at[4 * px + 2 * py + pc], send_sem=send.at[r],
                                         recv_sem=recv.at[r], device_id=(px, py, pc), device_id_type=MESH).wait_recv()
        for cp in sends:
            cp.wait_send()
        tot = buf[0]
        for d in range(1, N_DEV):
            tot = tot + buf[d]
        o_ref[...] = tot

    return pl.pallas_call(
        body, name="all_sum_small", in_specs=[pl.BlockSpec(memory_space=pltpu.VMEM)],
        out_specs=pl.BlockSpec(memory_space=pltpu.VMEM), out_shape=jax.ShapeDtypeStruct((1, P), F32),
        scratch_shapes=[pltpu.VMEM((N_DEV, 1, P), F32), pltpu.SemaphoreType.DMA((N_DEV,)), pltpu.SemaphoreType.DMA((N_DEV,))],
    )(vec)


def _half_sum(name, blocks, theirs, core, tr):
    _, R, C = blocks.shape
    hr = R // 2
    nb = hr // tr
    assert nb * tr == hr

    def body(c_ref, a_ref, b_ref, o_ref):
        o_ref[...] = (a_ref[...] + b_ref[...]).astype(BF16)

    grid_spec = pltpu.PrefetchScalarGridSpec(
        num_scalar_prefetch=1, grid=(4, nb),
        in_specs=[pl.BlockSpec((1, tr, C), lambda b, i, c_ref: (b, c_ref[0] * nb + i, 0)),
                  pl.BlockSpec((1, tr, C), lambda b, i, c_ref: (b, i, 0))],
        out_specs=pl.BlockSpec((1, tr, C), lambda b, i, c_ref: (b, i, 0)))
    return pl.pallas_call(body, name=name, grid_spec=grid_spec, out_shape=jax.ShapeDtypeStruct((4, hr, C), BF16),
                          compiler_params=_params(2, 40 << 20))(core, blocks, theirs)


def _sum4(name, stack, mine, chip, tr):
    _, R, C = stack.shape

    def body(chip_ref, s_ref, m_ref, o_ref):
        t = [jnp.where(chip_ref[0] == j, m_ref[j], s_ref[j]).astype(F32) for j in range(4)]
        o_ref[...] = ((t[0] + t[1]) + t[2]) + t[3]

    blk = pl.BlockSpec((4, tr, C), lambda i, chip_ref: (0, i, 0))
    grid_spec = pltpu.PrefetchScalarGridSpec(num_scalar_prefetch=1, grid=(R // tr,), in_specs=[blk, blk],
                                             out_specs=pl.BlockSpec((tr, C), lambda i, chip_ref: (i, 0)))
    return pl.pallas_call(body, name=name, grid_spec=grid_spec, out_shape=jax.ShapeDtypeStruct((R, C), F32),
                          compiler_params=_params(1, 40 << 20))(chip, stack, mine)


def _adamw_math(w, m, v, g):
    c1 = 1.0 - ADAM_B1 ** ADAM_STEP
    c2 = 1.0 - ADAM_B2 ** ADAM_STEP
    nm = ADAM_B1 * m + (1.0 - ADAM_B1) * g
    nv = ADAM_B2 * v + (1.0 - ADAM_B2) * (g * g)
    return -ADAM_LR * ((nm / c1) / (jnp.sqrt(nv / c2) + ADAM_EPS) + ADAM_WD * w), nm, nv


def _adamw(name, w, m, v, g, tr):
    R, C = w.shape

    def body(w_ref, m_ref, v_ref, ga_ref, g_ref, d_ref, nm_ref, nv_ref):
        g = ga_ref[...]
        g_ref[...] = g
        d_ref[...], nm_ref[...], nv_ref[...] = _adamw_math(w_ref[...], m_ref[...], v_ref[...], g)

    spec = pl.BlockSpec((tr, C), lambda i: (i, 0))
    return pl.pallas_call(body, name=name, grid=(R // tr,), in_specs=[spec] * 4, out_specs=[spec] * 4,
                          out_shape=[jax.ShapeDtypeStruct((R, C), F32)] * 4, compiler_params=_params(1, 40 << 20))(w, m, v, g)


def _adamw_halves(name, w, m, v, mine, theirs, core, tr):
    _, R, C = w.shape
    nb = (R // 2) // tr
    assert 2 * nb * tr == R

    def body(c_ref, w_ref, m_ref, v_ref, a_ref, b_ref, g_ref, d_ref, nm_ref, nv_ref):
        g = jnp.where((pl.program_id(0) // nb) == c_ref[0], a_ref[...], b_ref[...])
        g_ref[0] = g
        d_ref[0], nm_ref[0], nv_ref[0] = _adamw_math(w_ref[0], m_ref[0], v_ref[0], g)

    spec = pl.BlockSpec((1, tr, C), lambda i, c_ref: (0, i, 0))
    half = lambda own: pl.BlockSpec((tr, C), lambda i, c_ref, own=own: (
        jnp.clip(i - (c_ref[0] if own else 1 - c_ref[0]) * nb, 0, nb - 1), 0))
    grid_spec = pltpu.PrefetchScalarGridSpec(num_scalar_prefetch=1, grid=(R // tr,),
                                             in_specs=[spec, spec, spec, half(True), half(False)], out_specs=[spec] * 4)
    return pl.pallas_call(body, name=name, grid_spec=grid_spec, out_shape=[jax.ShapeDtypeStruct((1, R, C), F32)] * 4,
                          compiler_params=_params(1, 40 << 20))(core, w, m, v, mine, theirs)


def _row_tile(rows, cols, budget_bytes=1 << 20, mult=8):
    best = None
    for t in range(mult, rows + 1, mult):
        if rows % t == 0 and t * cols * 4 <= budget_bytes:
            best = t
    return best if best is not None else rows


def _ln_fwd(r, g, b):
    mu = jnp.mean(r, axis=1, keepdims=True)
    xc = r - mu
    rstd = lax.rsqrt(jnp.mean(xc * xc, axis=1, keepdims=True) + LN_EPS)
    xhat = xc * rstd
    return xhat * g + b, xhat, rstd


def _ln_bwd(dy, xhat, rstd, g):
    dxh = dy * g
    return rstd * (dxh - jnp.mean(dxh, axis=1, keepdims=True) - xhat * jnp.mean(dxh * xhat, axis=1, keepdims=True))


def _to_chip_blocks_cols(a):
    R, C4 = a.shape
    return a.reshape(R, 4, C4 // 4).transpose(1, 0, 2)


def _from_chip_blocks_cols(a):
    return a.transpose(1, 0, 2).reshape(a.shape[1], 4 * a.shape[2])


def kernel(x, w_in, b_forget, conv_w, conv_b, dt_bias, a_log, d_skip, ssm_norm_w, w_proj_attn, w_proj_ssm, b_gates, w_out, ln1_g, ln1_b, w_ffn_gate, w_ffn_up, w_ffn_down, ln2_g, ln2_b, loss_target, m_w_in, m_b_forget, m_conv_w, m_conv_b, m_dt_bias, m_a_log, m_d_skip, m_ssm_norm_w, m_w_proj_attn, m_w_proj_ssm, m_b_gates, m_w_out, m_ln1_g, m_ln1_b, m_w_ffn_gate, m_w_ffn_up, m_w_ffn_down, m_ln2_g, m_ln2_b, v_w_in, v_b_forget, v_conv_w, v_conv_b, v_dt_bias, v_a_log, v_d_skip, v_ssm_norm_w, v_w_proj_attn, v_w_proj_ssm, v_b_gates, v_w_out, v_ln1_g, v_ln1_b, v_w_ffn_gate, v_w_ffn_up, v_w_ffn_down, v_ln2_g, v_ln2_b):
    S = x.shape[1]
    D = D_MODEL
    TM, TM2, TA, AQF, LC, TS, TB = (min(TILES[k], S) for k in ("TM", "TM2", "TA", "AQF", "LC", "TS", "TB"))
    xf = x[0]
    tgt = loss_target[0]
    xb = xf.astype(BF16)

    shards = [w_in[0].astype(BF16), conv_w[0], w_proj_attn[0].astype(BF16), w_proj_ssm[0].astype(BF16), w_out[0].astype(BF16),
              w_ffn_gate[0].astype(BF16), w_ffn_up[0].astype(BF16), w_ffn_down[0].astype(BF16)]
    chip = 2 * lax.axis_index("x") + lax.axis_index("y")
    own = lambda gathered, mine: [lax.dynamic_update_slice(g, sh[None], (chip, 0, 0)) for g, sh in zip(gathered, mine)]
    g_in, g_cw = own(_chip_gather("gather_w_in", shards[:2], [True, False]), shards[:2])
    later, gather_token = _chip_copies_start("gather_rest_start", shards[2:], False, g_cw)
    w_full = _from_chip_blocks_cols(g_in)
    w_re = jnp.concatenate([w_full[:, 0:3072], w_full[:, 3088:5136], w_full[:, 5136:8208], w_full[:, 8240:10288],
                            w_full[:, 3072:3088], w_full[:, 8208:8240], jnp.zeros((D, 80), BF16)], axis=1)
    conv_w_full = _from_chip_blocks_cols(g_cw)

    def plain(accs, rows, vecs, j):
        return [accs[0]], []

    def q_scaled(accs, rows, vecs, j):
        return [accs[0] * jnp.where(j * 512 < D, 1.0 / math.sqrt(ATT_HEAD_DIM), 1.0)], []

    qkv, = _mm("proj_qkv", S, 3072, TM, 512, [(xb, D, 0)], [(w_re, 0)], [(0, 0)], q_scaled, [(3072, BF16, 0)],
               after=[gather_token])
    z, = _mm("proj_z", S, 2048, TM, 512, [(xb, D, 0)], [(w_re, RE_Z // 512)], [(0, 0)], plain, [(2048, F32, 0)])
    xbc_raw, = _mm("proj_xbc", S, 3072, TM, 512, [(xb, D, 0)], [(w_re, RE_XBC // 512)], [(0, 0)], plain, [(3072, F32, 0)])
    gl, = _mm("proj_gate", S, 2048, TM, 512, [(xb, D, 0)], [(w_re, RE_GATE // 512)], [(0, 0)], plain, [(2048, F32, 0)])
    small, = _mm("proj_small", S, 128, TM, 128, [(xb, D, 0)], [(w_re, RE_SMALL // 128)], [(0, 0)], plain, [(128, F32, 0)])

    bvec = jnp.concatenate([b_forget, jnp.zeros((1, LANES - ATT_HEADS), F32)], axis=1)
    cum = _cum_fwd(small, bvec, TB)[:, :ATT_HEADS]
    cum_cols = cum.reshape(S, 8, 2).transpose(1, 0, 2)
    qa, ka = _attn_prep(qkv, cum_cols, TM)
    o, o32, lse_rows = _attn_fwd(qa, ka, qkv, AQF, TA)

    cb_row = conv_b
    xbc = _conv_fwd(xbc_raw, conv_w_full, cb_row, min(512, S), 512)
    dt_raw = small[:, 16:48]
    dtc = dt_raw.reshape(S, SSM_GROUPS, 8).transpose(1, 0, 2)
    dtr = dt_raw.T.reshape(SSM_GROUPS, 8, S)
    bias_r = dt_bias.reshape(SSM_GROUPS, 1, 8)
    alog_b = jnp.repeat(a_log, ATT_HEAD_DIM, axis=1)
    dskip_b = jnp.repeat(d_skip, ATT_HEAD_DIM, axis=1)
    bias_c = dt_bias.reshape(SSM_GROUPS, 8, 1)
    alog_c = a_log.reshape(SSM_GROUPS, 8, 1)
    y_ssd, ssm, hs_all = _ssd_fwd(xbc, z, dtc, dtr, bias_r, alog_b, dskip_b, ssm_norm_w, bias_c, alog_c, LC)

    def merge(accs, rows, vecs, j):
        g0, g1 = _sigmoid(rows[0] + vecs[0]), _sigmoid(rows[1] + vecs[1])
        return [g0 * accs[0] + g1 * accs[1], accs[0], accs[1]], []

    g_pa, g_ps, g_out, g_fg, g_fu, g_fd = own(_chip_copies_wait("gather_rest_wait", later, False, o), shards[2:])
    wpa, wps, wout = g_pa.reshape(D, D), g_ps.reshape(SSM_INNER, D), g_out.reshape(D, D)
    wfg, wfu, wfd = _from_chip_blocks_cols(g_fg), _from_chip_blocks_cols(g_fu), g_fd.reshape(FFN_HIDDEN, D)
    mix, attn_d, ssm_d = _mm("merge", S, D, TM, 512, [(o, D, 0), (ssm, SSM_INNER, 0)], [(wpa, 0), (wps, 0)], [(0, 0), (1, 1)],
                             merge, [(D, BF16, 0), (D, F32, 0), (D, F32, 0)], rows=[(gl, 0), (gl, 2)],
                             vecs_n=[(b_gates, 0), (b_gates, 2)])

    def out_ln1(accs, rows, vecs, j):
        r1 = ALPHA * rows[0] + accs[0]
        h1, _, _ = _ln_fwd(r1, vecs[0], vecs[1])
        return [r1, h1, h1], []

    r1, h1, h1b = _mm("out_ln1", S, D, TM2, D, [(mix, D, 0)], [(wout, 0)], [(0, 0)], out_ln1,
                      [(D, F32, 0), (D, F32, 0), (D, BF16, 0)], rows=[(xf, 0)], vecs_n=[(ln1_g, 0), (ln1_b, 0)])

    FT = FFN_HIDDEN // 2

    def swiglu(accs, rows, vecs, j):
        g, u = accs
        return [g, u, g * _sigmoid(g) * u], []

    gate, up, hmid = _mm("ffn_up", S, FFN_HIDDEN, TM2, FT, [(h1b, D, 0)], [(wfg, 0), (wfu, 0)], [(0, 0), (0, 1)], swiglu,
                         [(FFN_HIDDEN, F32, 0), (FFN_HIDDEN, F32, 0), (FFN_HIDDEN, BF16, 0)])

    def down_ln2_loss(accs, rows, vecs, j):
        r2 = ALPHA * rows[0] + accs[0]
        yv, xhat, rstd = _ln_fwd(r2, vecs[0], vecs[1])
        diff = yv - rows[1]
        dy = diff * (1.0 / D_MODEL)
        dr2 = _ln_bwd(dy, xhat, rstd, vecs[0])
        return [dr2, dr2], [jnp.sum(dy * xhat, axis=0, keepdims=True), jnp.sum(dy, axis=0, keepdims=True),
                            (0.5 / D_MODEL) * jnp.sum(diff * diff, axis=0, keepdims=True)]

    dr2, dr2b, dln2_g, dln2_b, loss_lanes = _mm("ffn_down_ln2", S, D, TM2, D, [(hmid, FFN_HIDDEN, 0)], [(wfd, 0)], [(0, 0)],
                                               down_ln2_loss, [(D, F32, 0), (D, BF16, 0)], rows=[(h1, 0), (tgt, 0)],
                                               vecs_n=[(ln2_g, 0), (ln2_b, 0)], sums=[D, D, D])
    loss = lax.psum(jnp.sum(loss_lanes), ("x", "y", "c"))

    def dswiglu(accs, rows, vecs, j):
        g, u = rows
        sg = _sigmoid(g)
        return [accs[0] * u * sg * (1.0 + g * (1.0 - sg)), accs[0] * g * sg], []

    dgate, dup = _mm("ffn_down_bwd", S, FFN_HIDDEN, TM2, FT, [(dr2b, D, 0)], [(wfd, 0)], [(0, 0)], dswiglu,
                     [(FFN_HIDDEN, BF16, 0), (FFN_HIDDEN, BF16, 0)], nt=True, rows=[(gate, 0), (up, 0)])
    dwfd = _mm_tn("dw_ffn_down", hmid, dr2b, FFN_HIDDEN // 2, D, TS)
    dwfg = _mm_tn("dw_ffn_gate", h1b, dgate, D, FT, TS)
    dwfu = _mm_tn("dw_ffn_up", h1b, dup, D, FT, TS)
    core = lax.axis_index("c").astype(jnp.int32).reshape(1)

    def send_grads(tag, names_, blocks_, after_):
        theirs_ = _half_to_sibling("swap_halves_" + tag, blocks_)
        halves_ = [_half_sum("halfsum_" + nm, b, t, core, _row_tile(b.shape[1] // 2, b.shape[2], mult=16))
                   for nm, b, t in zip(names_, blocks_, theirs_)]
        started_, token_ = _chip_copies_start("scatter_" + tag + "_start", halves_, True, after_)
        return halves_, started_, token_

    ffn_names = ["w_ffn_gate", "w_ffn_up", "w_ffn_down"]
    ffn_halves, ffn_started, ffn_token = send_grads(
        "ffn", ffn_names, [_to_chip_blocks_cols(dwfg), _to_chip_blocks_cols(dwfu), dwfd.reshape(4, FFN_HIDDEN // 4, D)], dwfu)

    def dh1_ln1(accs, rows, vecs, j):
        dh1 = ALPHA * rows[0] + accs[0] + accs[1]
        _, xhat, rstd = _ln_fwd(rows[1], vecs[0], vecs[0])
        dr1 = _ln_bwd(dh1, xhat, rstd, vecs[0])
        return [dr1, dr1], [jnp.sum(dh1 * xhat, axis=0, keepdims=True), jnp.sum(dh1, axis=0, keepdims=True)]

    dr1, dr1b, dln1_g, dln1_b = _mm("ffn_up_bwd_ln1", S, D, TM2, D, [(dgate, FFN_HIDDEN, 0), (dup, FFN_HIDDEN, 0)],
                                    [(wfg, 0), (wfu, 0)], [(0, 0), (1, 1)], dh1_ln1, [(D, F32, 0), (D, BF16, 0)], nt=True,
                                    rows=[(dr2, 0), (r1, 0)], vecs_n=[(ln1_g, 0)], sums=[D, D], after=[ffn_token])

    def dmerge(accs, rows, vecs, j):
        dmix = accs[0]
        g0, g1 = _sigmoid(rows[0] + vecs[0]), _sigmoid(rows[1] + vecs[1])
        dgl0 = dmix * rows[2] * g0 * (1.0 - g0)
        dgl1 = dmix * rows[3] * g1 * (1.0 - g1)
        return [dmix * g0, dmix * g1, dgl0, dgl1], [jnp.sum(dgl0, axis=0, keepdims=True), jnp.sum(dgl1, axis=0, keepdims=True)]

    d_attn_d, d_ssm_d, dgl0, dgl1, dbg0, dbg1 = _mm(
        "out_bwd", S, D, TM, 512, [(dr1b, D, 0)], [(wout, 0)], [(0, 0)], dmerge, [(D, BF16, 0)] * 4, nt=True,
        rows=[(gl, 0), (gl, 2), (attn_d, 0), (ssm_d, 0)], vecs_n=[(b_gates, 0), (b_gates, 2)], sums=[D, D])
    dwout = _mm_tn("dw_out", mix, dr1b, D, D, TS)
    dwpa = _mm_tn("dw_proj_attn", o, d_attn_d, D, D, TS)
    dwps = _mm_tn("dw_proj_ssm", ssm, d_ssm_d, D, D, TS)
    mid_names = ["w_proj_attn", "w_proj_ssm", "w_out"]
    mid_halves, mid_started, mid_token = send_grads(
        "mid", mid_names, [dwpa.reshape(4, D // 4, D), dwps.reshape(4, SSM_INNER // 4, D), dwout.reshape(4, D // 4, D)], dwps)

    do, = _mm("proj_attn_bwd", S, D, TM, 512, [(d_attn_d, D, 0)], [(wpa, 0)], [(0, 0)], plain, [(D, BF16, 0)], nt=True,
              after=[mid_token])
    stats_rows = _attn_stats(do, o32, lse_rows, AQF)
    dq, dk, dv, dck, dcq = _attn_bwd(qa, ka, qkv, do, stats_rows, TA)

    def per_head(a):
        a = a.transpose(1, 0, 2).reshape(S, ATT_HEADS)
        return jnp.concatenate([a, jnp.zeros((S, LANES - ATT_HEADS), F32)], axis=1)

    dfl, dbf = _cum_bwd(per_head(dck), per_head(dcq), small, bvec, TB)

    dssm, = _mm("proj_ssm_bwd", S, SSM_INNER, TM, 512, [(d_ssm_d, D, 0)], [(wps, 0)], [(0, 0)], plain, [(SSM_INNER, F32, 0)],
                nt=True)
    dxs, dbm, dcm, dz, ddt8, dnw, ddskip_b, dalog8, dbias8 = _ssd_bwd(
        xbc, z, y_ssd, dssm, hs_all, dtc, dtr, bias_r, a_log.reshape(SSM_GROUPS, 1, 8), alog_b, dskip_b, ssm_norm_w, bias_c,
        alog_c, LC)
    TC = min(512, S)
    du_x, dcw_x, dcb_x = _conv_bwd("conv_bwd_x", xbc_raw, dxs, conv_w_full, cb_row, TC, 512, 0)
    du_b, dcw_b, dcb_b = _conv_bwd("conv_bwd_b", xbc_raw, dbm, conv_w_full, cb_row, TC, 512, SSM_INNER)
    du_c, dcw_c, dcb_c = _conv_bwd("conv_bwd_c", xbc_raw, dcm, conv_w_full, cb_row, TC, 512, SSM_INNER + SSM_GROUPS * SSM_STATE)
    dconv_w = jnp.concatenate([dcw_x, dcw_b, dcw_c], axis=1)
    dconv_b = jnp.concatenate([dcb_x, dcb_b, dcb_c], axis=1)
    ddt_raw = ddt8.transpose(1, 0, 2).reshape(S, SSM_HEADS)

    dsmall = jnp.concatenate([dfl[:, :ATT_HEADS], ddt_raw, jnp.zeros((S, 80), F32)], axis=1).astype(BF16)
    HB = SSM_GROUPS * SSM_STATE
    dw_q, dw_k, dw_v = (_mm_tn("dw_in_" + nm, xb, g_, D, D, TS) for nm, g_ in (("q", dq), ("k", dk), ("v", dv)))
    dw_z = _mm_tn("dw_in_z", xb, dz, D, D, TS)
    dw_xbc = jnp.concatenate([_mm_tn("dw_in_xs", xb, du_x, D, D, TS), _mm_tn("dw_in_b", xb, du_b, D, HB, TS),
                              _mm_tn("dw_in_c", xb, du_c, D, HB, TS)], axis=1)
    dw_g0, dw_g1 = _mm_tn("dw_in_g0", xb, dgl0, D, D, TS), _mm_tn("dw_in_g1", xb, dgl1, D, D, TS)
    dw_s = _mm_tn("dw_in_small", xb, dsmall, D, LANES, TS)
    dw_full = jnp.concatenate([dw_q, dw_k, dw_v, dw_s[:, 0:ATT_HEADS], dw_z, dw_xbc, dw_s[:, ATT_HEADS:ATT_HEADS + SSM_HEADS],
                               dw_g0, dw_g1], axis=1)

    in_halves, in_started, in_token = send_grads("in", ["w_in"], [_to_chip_blocks_cols(dw_full)], dw_full)
    def dx_first(accs, rows, vecs, j):
        return [ALPHA * rows[0] + sum(accs[1:], accs[0])], []

    def dx_more(accs, rows, vecs, j):
        return [rows[0] + sum(accs[1:], accs[0])], []

    wk = lambda col, width=D: (w_re, 0, col // width, width)
    dx_part, = _mm("dx_a", S, D, TM2, D, [(dq, D, 0), (dk, D, 0), (dv, D, 0), (dz, D, 0), (dz, D, 1)],
                   [wk(0), wk(1024), wk(2048), wk(RE_Z), wk(RE_Z + 1024)], [(k, k) for k in range(5)], dx_first,
                   [(D, F32, 0)], nt=True, rows=[(dr1, 0)], after=[in_token])
    grad_x, = _mm("dx_b", S, D, TM2, D,
                  [(du_x, D, 0), (du_x, D, 1), (du_b, HB, 0), (du_c, HB, 0), (dgl0, D, 0), (dgl1, D, 0), (dsmall, LANES, 0)],
                  [wk(RE_XBC), wk(RE_XBC + 1024), wk(RE_XBC + 2048, HB), wk(RE_XBC + 2048 + HB, HB), wk(RE_GATE),
                   wk(RE_GATE + 1024), wk(RE_SMALL, LANES)],
                  [(k, k) for k in range(7)], dx_more, [(D, F32, 0)], nt=True, rows=[(dx_part, 0)])
    names = ["w_in"] + mid_names + ffn_names
    halves = in_halves + mid_halves + ffn_halves
    stacks = (_chip_copies_wait("scatter_in_wait", in_started, True, grad_x)
              + _chip_copies_wait("scatter_mid_wait", mid_started, True, grad_x)
              + _chip_copies_wait("scatter_ffn_wait", ffn_started, True, grad_x))
    chip1 = chip.astype(jnp.int32).reshape(1)
    reduced = [_sum4("sum_" + nm, st, hv, chip1, _row_tile(st.shape[1], st.shape[2], mult=16))
               for nm, st, hv in zip(names, stacks, halves)]
    other = _sibling_swap("swap_reduced", reduced)
    big_w = [w_in, w_proj_attn, w_proj_ssm, w_out, w_ffn_gate, w_ffn_up, w_ffn_down]
    big_m = [m_w_in, m_w_proj_attn, m_w_proj_ssm, m_w_out, m_w_ffn_gate, m_w_ffn_up, m_w_ffn_down]
    big_v = [v_w_in, v_w_proj_attn, v_w_proj_ssm, v_w_out, v_w_ffn_gate, v_w_ffn_up, v_w_ffn_down]
    big = {}
    for nm, w_, m_, v_, mine, theirs in zip(names, big_w, big_m, big_v, reduced, other):
        big[nm] = _adamw_halves("adamw_" + nm, w_, m_, v_, mine, theirs, core, _row_tile(w_.shape[1] // 2, w_.shape[2]))

    dd_skip = ddskip_b.reshape(1, SSM_HEADS, ATT_HEAD_DIM).sum(axis=2)
    pieces = [dbf[:, :ATT_HEADS], dconv_w.reshape(1, SSM_CONV * SSM_CONV_DIM), dconv_b, dbias8.reshape(1, SSM_HEADS), dalog8.reshape(1, SSM_HEADS), dd_skip,
              dnw, dbg0, dbg1, dln1_g, dln1_b, dln2_g, dln2_b]
    widths = [p.shape[1] for p in pieces]
    total = sum(widths)
    P = -(-total // LANES) * LANES
    packed = jnp.concatenate(pieces + [jnp.zeros((1, P - total), F32)], axis=1)
    summed = _all_sum_small(packed)
    offs = [0]
    for wd in widths:
        offs.append(offs[-1] + wd)
    sm = [summed[:, offs[k]:offs[k + 1]] for k in range(len(pieces))]
    g_bf, g_cw_full, g_cb, g_dtb, g_al, g_ds, g_nw = sm[0], sm[1].reshape(SSM_CONV, SSM_CONV_DIM), sm[2], sm[3], sm[4], sm[5], sm[6]
    g_bg = jnp.concatenate([sm[7], sm[8]], axis=1)
    g_l1g, g_l1b, g_l2g, g_l2b = sm[9], sm[10], sm[11], sm[12]
    cshard = SSM_CONV_DIM // 4
    g_cw_shard = lax.dynamic_slice_in_dim(g_cw_full, chip * cshard, cshard, axis=1)
    small_names = ["b_forget", "conv_w", "conv_b", "dt_bias", "a_log", "d_skip", "ssm_norm_w", "b_gates", "ln1_g", "ln1_b",
                   "ln2_g", "ln2_b"]
    small_g = [g_bf, g_cw_shard.reshape(1, -1), g_cb, g_dtb, g_al, g_ds, g_nw, g_bg, g_l1g, g_l1b, g_l2g, g_l2b]
    small_w = [b_forget, conv_w[0].reshape(1, -1), conv_b, dt_bias, a_log, d_skip, ssm_norm_w, b_gates, ln1_g, ln1_b, ln2_g, ln2_b]
    small_m = [m_b_forget, m_conv_w[0].reshape(1, -1), m_conv_b, m_dt_bias, m_a_log, m_d_skip, m_ssm_norm_w, m_b_gates, m_ln1_g,
               m_ln1_b, m_ln2_g, m_ln2_b]
    small_v = [v_b_forget, v_conv_w[0].reshape(1, -1), v_conv_b, v_dt_bias, v_a_log, v_d_skip, v_ssm_norm_w, v_b_gates, v_ln1_g,
               v_ln1_b, v_ln2_g, v_ln2_b]
    sw = [a.shape[1] for a in small_w]
    stot = sum(sw)
    SP = -(-stot // LANES) * LANES

    def pack(parts):
        return jnp.concatenate(list(parts) + [jnp.zeros((1, SP - stot), F32)], axis=1).reshape(SP // LANES, LANES)

    sres = _adamw("adamw_small", pack(small_w), pack(small_m), pack(small_v), pack(small_g), SP // LANES)
    soffs = [0]
    for wd in sw:
        soffs.append(soffs[-1] + wd)
    smalls = {}
    for k, nm in enumerate(small_names):
        vals = [r.reshape(1, SP)[:, soffs[k]:soffs[k + 1]] for r in sres]
        if nm == "conv_w":
            vals = [v_.reshape(1, SSM_CONV, cshard) for v_ in vals]
        smalls[nm] = vals

    order = ["w_in", "b_forget", "conv_w", "conv_b", "dt_bias", "a_log", "d_skip", "ssm_norm_w", "w_proj_attn", "w_proj_ssm",
             "b_gates", "w_out", "ln1_g", "ln1_b", "w_ffn_gate", "w_ffn_up", "w_ffn_down", "ln2_g", "ln2_b"]
    allres = {**big, **smalls}
    outs = [loss, grad_x[None]]
    for idx in range(4):
        outs += [allres[nm][idx] for nm in order]
    return tuple(outs)
```

```python
import functools
import math

import jax
import jax.numpy as jnp
from jax import lax
from jax.experimental import pallas as pl
from jax.experimental.pallas import tpu as pltpu

F32, BF16 = jnp.float32, jnp.bfloat16
MESH = pl.DeviceIdType.MESH

D_MODEL = 1024
ATT_HEADS, ATT_HEAD_DIM = 16, 64
SSM_INNER, SSM_HEADS, SSM_GROUPS, SSM_STATE, SSM_CONV = 2048, 32, 4, 128, 4
SSM_CONV_DIM = SSM_INNER + 2 * SSM_GROUPS * SSM_STATE
GROUP_LANES = SSM_INNER // SSM_GROUPS
FFN_HIDDEN = 2816
ALPHA = 2.0 ** 0.25
LN_EPS = 1e-5
RMS_EPS = 1e-5
ADAM_LR, ADAM_B1, ADAM_B2, ADAM_EPS, ADAM_WD, ADAM_STEP = 0.001, 0.9, 0.999, 1e-08, 0.01, 10
IN_SIZES = (1024, 1024, 1024, 16, 2048, 3072, 32, 2048)
IN_WIDTH = sum(IN_SIZES)
RE_WIDTH = 3072 + 2048 + 3072 + 2048 + 128
RE_Z, RE_XBC, RE_GATE, RE_SMALL = 3072, 5120, 8192, 10240

LANES = 128
VMEM_CAP = 60 * 1024 * 1024
NEG = -1e30
TILES = dict(TM=1024, TM2=256, TA=512, AQF=2048, LC=256, CV=512, TS=2048, TB=256)


def _params(n_axes, vmem_bytes=None):
    return pltpu.CompilerParams(dimension_semantics=("arbitrary",) * n_axes,
                                vmem_limit_bytes=None if vmem_bytes is None else int(min(vmem_bytes, VMEM_CAP)))


def _sigmoid(v):
    return 1.0 / (1.0 + jnp.exp(-v))


def _softplus(v):
    return jnp.maximum(v, 0.0) + jnp.log(1.0 + jnp.exp(-jnp.abs(v)))


def _dot(a, b):
    return lax.dot_general(a, b, (((1,), (0,)), ((), ())), preferred_element_type=F32)


def _dot_nt(a, b):
    return lax.dot_general(a, b, (((1,), (1,)), ((), ())), preferred_element_type=F32)


def _dot_tn(a, b):
    return lax.dot_general(a, b, (((0,), (0,)), ((), ())), preferred_element_type=F32)


def _split3(v):
    hi = v.astype(BF16)
    r1 = v - hi.astype(F32)
    mid = r1.astype(BF16)
    lo = (r1 - mid.astype(F32)).astype(BF16)
    return hi, mid, lo


def _dot_exact_left(m01, v):
    hi, mid, lo = _split3(v)
    return _dot(m01, hi) + _dot(m01, mid) + _dot(m01, lo)


def _dot_exact_right(v, m01, terms=3):
    parts = _split3(v)[:terms]
    out = _dot(parts[0], m01)
    for p in parts[1:]:
        out = out + _dot(p, m01)
    return out


def _mm(name, M, N, tm, tn, lhs, rhs, pairs, e_fn, outs, *, nt=False, rows=(), vecs_n=(), sums=(), after=()):
    ni, nj = M // tm, N // tn
    assert ni * tm == M and nj * tn == N, (name, M, N, tm, tn)
    n_l, n_r, n_row, n_vn, n_o, n_s = len(lhs), len(rhs), len(rows), len(vecs_n), len(outs), len(sums)

    def body(*refs):
        pos = 0
        l_refs = refs[pos:pos + n_l]; pos += n_l
        r_refs = refs[pos:pos + n_r]; pos += n_r
        row_refs = refs[pos:pos + n_row]; pos += n_row
        vn_refs = refs[pos:pos + n_vn]; pos += n_vn + len(after)
        o_refs = refs[pos:pos + n_o]; pos += n_o
        s_refs = refs[pos:pos + n_s]; pos += n_s
        i, j = pl.program_id(0), pl.program_id(1)
        accs = []
        for li, ri in pairs:
            accs.append(_dot_nt(l_refs[li][...], r_refs[ri][...]) if nt else _dot(l_refs[li][...], r_refs[ri][...]))
        out_vals, sum_vals = e_fn(accs, [r[...] for r in row_refs], [r[...] for r in vn_refs], j)
        for r, v in zip(o_refs, out_vals):
            r[...] = v.astype(r.dtype)
        if n_s:
            col = pl.multiple_of(j * tn, LANES)

            @pl.when(i == 0)
            def _():
                for r, v in zip(s_refs, sum_vals):
                    r[:, pl.ds(col, tn)] = v

            @pl.when(i > 0)
            def _():
                for r, v in zip(s_refs, sum_vals):
                    r[:, pl.ds(col, tn)] += v

    in_specs, args, est = [], [], 0
    for arr, width, cb in lhs:
        in_specs.append(pl.BlockSpec((tm, width), lambda i, j, cb=cb: (i, cb)))
        args.append(arr); est += tm * width * arr.dtype.itemsize
    for arr, off, *ksub in rhs:
        if nt:
            kb, kw = ksub if ksub else (0, arr.shape[1])
            in_specs.append(pl.BlockSpec((tn, kw), lambda i, j, off=off, kb=kb: (j + off, kb)))
            est += tn * kw * arr.dtype.itemsize
        else:
            in_specs.append(pl.BlockSpec((arr.shape[0], tn), lambda i, j, off=off: (0, j + off)))
            est += tn * arr.shape[0] * arr.dtype.itemsize
        args.append(arr)
    for arr, off in rows:
        in_specs.append(pl.BlockSpec((tm, tn), lambda i, j, off=off: (i, j + off)))
        args.append(arr); est += tm * tn * arr.dtype.itemsize
    for arr, off in vecs_n:
        in_specs.append(pl.BlockSpec((1, tn), lambda i, j, off=off: (0, j + off)))
        args.append(arr); est += 8 * tn * 4
    for arr in after:
        in_specs.append(pl.BlockSpec(memory_space=pl.ANY))
        args.append(arr)
    out_shape, out_specs = [], []
    for total, dtype, off in outs:
        out_shape.append(jax.ShapeDtypeStruct((M, total), dtype))
        out_specs.append(pl.BlockSpec((tm, tn), lambda i, j, off=off: (i, j + off)))
        est += tm * tn * jnp.dtype(dtype).itemsize
    for total in sums:
        out_shape.append(jax.ShapeDtypeStruct((1, total), F32))
        out_specs.append(pl.BlockSpec((1, total), lambda i, j: (0, 0)))
        est += 8 * total * 4
    vmem = 2 * est + (len(pairs) + 4) * tm * tn * 4 + (8 << 20)
    return pl.pallas_call(body, name=name, grid=(ni, nj), in_specs=in_specs, out_specs=out_specs, out_shape=out_shape,
                          compiler_params=_params(2, vmem))(*args)


def _mm_tn(name, a, g, ta, tn, ts, a_cols=None, a_off=0):
    S = a.shape[0]
    Ka = a.shape[1] if a_cols is None else a_cols
    N = g.shape[1]
    assert Ka % ta == 0 and N % tn == 0 and S % ts == 0, (name, Ka, N, S)
    aoff = a_off // ta

    def body(a_ref, g_ref, o_ref):
        s = pl.program_id(2)
        part = _dot_tn(a_ref[...], g_ref[...])

        @pl.when(s == 0)
        def _():
            o_ref[...] = part

        @pl.when(s > 0)
        def _():
            o_ref[...] += part

    vmem = 2 * (ts * ta * 2 + ts * tn * 2 + ta * tn * 4) + 2 * ta * tn * 4 + (8 << 20)
    return pl.pallas_call(
        body, name=name, grid=(Ka // ta, N // tn, S // ts),
        in_specs=[pl.BlockSpec((ts, ta), lambda ia, jn, s: (s, ia + aoff)), pl.BlockSpec((ts, tn), lambda ia, jn, s: (s, jn))],
        out_specs=pl.BlockSpec((ta, tn), lambda ia, jn, s: (ia, jn)),
        out_shape=jax.ShapeDtypeStruct((Ka, N), F32), compiler_params=_params(3, vmem))(a, g)


def _tri(n, upper):
    r = lax.broadcasted_iota(jnp.int32, (n, n), 0)
    c = lax.broadcasted_iota(jnp.int32, (n, n), 1)
    return jnp.where((c >= r) if upper else (c <= r), 1.0, 0.0).astype(BF16)


def _logsig(v):
    return jnp.minimum(v, 0.0) - jnp.log(1.0 + jnp.exp(-jnp.abs(v)))


def _cum_fwd(small, bvec, tb):
    S = small.shape[0]

    def body(x_ref, b_ref, o_ref, carry):
        i = pl.program_id(0)

        @pl.when(i == 0)
        def _():
            carry[...] = jnp.zeros_like(carry)

        logf = _logsig(x_ref[...] + b_ref[...])
        cum = _dot_exact_left(_tri(tb, False), logf) + carry[0:1, :]
        o_ref[...] = cum
        carry[0:1, :] = cum[tb - 1:tb, :]

    return pl.pallas_call(
        body, name="cum_fwd", grid=(S // tb,),
        in_specs=[pl.BlockSpec((tb, LANES), lambda i: (i, 0)), pl.BlockSpec((1, LANES), lambda i: (0, 0))],
        out_specs=pl.BlockSpec((tb, LANES), lambda i: (i, 0)), out_shape=jax.ShapeDtypeStruct((S, LANES), F32),
        scratch_shapes=[pltpu.VMEM((8, LANES), F32)], compiler_params=_params(1))(small, bvec)


def _cum_bwd(dcum_k, dcum_q, small, bvec, tb):
    S = small.shape[0]
    nb = S // tb

    def body(dk_ref, dq_ref, x_ref, b_ref, o_ref, s_ref, carry):
        i = pl.program_id(0)

        @pl.when(i == 0)
        def _():
            carry[...] = jnp.zeros_like(carry)
            s_ref[...] = jnp.zeros_like(s_ref)

        rc = _dot_exact_left(_tri(tb, True), dk_ref[...] + dq_ref[...]) + carry[0:1, :]
        dfl = rc * _sigmoid(-(x_ref[...] + b_ref[...]))
        o_ref[...] = dfl
        s_ref[...] += jnp.sum(dfl, axis=0, keepdims=True)
        carry[0:1, :] = rc[0:1, :]

    rev = lambda i: (nb - 1 - i, 0)
    return pl.pallas_call(
        body, name="cum_bwd", grid=(nb,),
        in_specs=[pl.BlockSpec((tb, LANES), rev)] * 3 + [pl.BlockSpec((1, LANES), lambda i: (0, 0))],
        out_specs=[pl.BlockSpec((tb, LANES), rev), pl.BlockSpec((1, LANES), lambda i: (0, 0))],
        out_shape=[jax.ShapeDtypeStruct((S, LANES), F32), jax.ShapeDtypeStruct((1, LANES), F32)],
        scratch_shapes=[pltpu.VMEM((8, LANES), F32)], compiler_params=_params(1))(dcum_k, dcum_q, small, bvec)


N_AUG = 3


def _lane():
    return lax.broadcasted_iota(jnp.int32, (1, LANES), 1)


def _lane_mask():
    return _lane() < ATT_HEAD_DIM


def _aug_base(h):
    return ATT_HEAD_DIM * (1 - h)


def _attn_prep(qkv, cum_cols, T):
    S = qkv.shape[0]
    HP = ATT_HEADS // 2

    def body(q_ref, k_ref, c_ref, qa_ref, ka_ref):
        lane = _lane()
        q = q_ref[...].astype(F32)
        k = k_ref[...].astype(F32)
        for h in (0, 1):
            base = _aug_base(h)
            own = (lane < ATT_HEAD_DIM) if h == 0 else (lane >= ATT_HEAD_DIM)
            terms = [t.astype(F32) for t in _split3(c_ref[0, :, h:h + 1])]
            qa = jnp.where(lane == base + N_AUG, 0.0, jnp.where((lane >= base) & (lane < base + N_AUG), 1.0, q))
            ka = jnp.where(lane == base + N_AUG, 1.0, jnp.where(own, k, 0.0))
            for t in range(N_AUG):
                ka = jnp.where(lane == base + t, -terms[t], ka)
            qa_ref[:, h * LANES:(h + 1) * LANES] = qa.astype(BF16)
            ka_ref[:, h * LANES:(h + 1) * LANES] = ka.astype(BF16)

    return pl.pallas_call(
        body, name="attn_prep", grid=(S // T, HP),
        in_specs=[pl.BlockSpec((T, LANES), lambda i, hp: (i, hp)), pl.BlockSpec((T, LANES), lambda i, hp: (i, HP + hp)),
                  pl.BlockSpec((1, T, 2), lambda i, hp: (hp, i, 0))],
        out_specs=[pl.BlockSpec((T, 2 * LANES), lambda i, hp: (i, hp))] * 2,
        out_shape=[jax.ShapeDtypeStruct((S, 2 * D_MODEL), BF16)] * 2, compiler_params=_params(2))(qkv, qkv, cum_cols)


def _attn_fwd(qa, ka, qkv, T, TK):
    S = qkv.shape[0]
    nq = S // T
    r = T // TK
    HP = ATT_HEADS // 2

    def body(q0_ref, q1_ref, k0_ref, k1_ref, v_ref, o_ref, o32_ref, lse_ref):
        i = pl.program_id(1)
        qs = (q0_ref[...], q1_ref[...])
        k_refs = (k0_ref, k1_ref)
        row = lax.broadcasted_iota(jnp.int32, (TK, T), 0)
        col = lax.broadcasted_iota(jnp.int32, (TK, T), 1)
        head_rows = lax.broadcasted_iota(jnp.int32, (LANES, 1), 0) < ATT_HEAD_DIM

        def block(j, carry, q0):
            off = pl.multiple_of(j * TK, TK)
            vj = v_ref[pl.ds(off, TK), :]
            full = q0 is None
            q0 = 0 if full else q0
            m0, l0, m1, l1, acc = carry
            new, alphas, pvs = [], [], []
            for h, (m, l) in enumerate(((m0, l0), (m1, l1))):
                st = _dot_nt(k_refs[h][pl.ds(off, TK), :], qs[h][q0:, :])
                if not full:
                    st = jnp.where(row[:, :T - q0] <= col[:, :T - q0], st, NEG)
                m_old, l_old = m[:, q0:], l[:, q0:]
                m_new = jnp.maximum(m_old, jnp.max(st, axis=0, keepdims=True))
                p = jnp.exp(st - m_new)
                alpha = jnp.exp(m_old - m_new)
                l_new = alpha * l_old + jnp.sum(p, axis=0, keepdims=True)
                pvs.append(_dot_tn(vj, p.astype(BF16)))
                alphas.append(alpha)
                new += [m_new, l_new]
            part = acc[:, q0:] * jnp.where(head_rows, alphas[0], alphas[1]) + jnp.where(head_rows, pvs[0], pvs[1])
            if q0:
                keep = lambda old, upd: jnp.concatenate([old[:, :q0], upd], axis=1)
                return (keep(m0, new[0]), keep(l0, new[1]), keep(m1, new[2]), keep(l1, new[3]), keep(acc, part))
            return (new[0], new[1], new[2], new[3], part)

        init = (jnp.full((1, T), NEG, F32), jnp.zeros((1, T), F32), jnp.full((1, T), NEG, F32), jnp.zeros((1, T), F32),
                jnp.zeros((LANES, T), F32))
        n_full = i * r
        carry = lax.fori_loop(0, n_full // 2, lambda jj, c: block(2 * jj + 1, block(2 * jj, c, None), None), init)
        carry = lax.cond(n_full % 2 == 1, lambda c: block(n_full - 1, c, None), lambda c: c, carry)
        for d in range(r):
            carry = block(n_full + d, carry, d * TK)
        m0, l0, m1, l1, acc = carry
        out = (acc / jnp.where(head_rows, l0, l1)).T
        o_ref[...] = out.astype(BF16)
        o32_ref[...] = out
        lse_ref[0, 0:1, :] = m0 + jnp.log(l0)
        lse_ref[0, 1:2, :] = m1 + jnp.log(l1)

    vmem = 2 * (2 * T * LANES * 2 + 3 * S * LANES * 2 + T * LANES * (2 + 4) + 8 * T * 4) + 10 * T * TK * 4 + (8 << 20)
    qspec = lambda h: pl.BlockSpec((T, LANES), lambda hp, i, h=h: (i, 2 * hp + h))
    kspec = lambda h: pl.BlockSpec((S, LANES), lambda hp, i, h=h: (0, 2 * hp + h))
    return pl.pallas_call(
        body, name="attn_fwd", grid=(HP, nq),
        in_specs=[qspec(0), qspec(1), kspec(0), kspec(1), pl.BlockSpec((S, LANES), lambda hp, i: (0, 2 * HP + hp))],
        out_specs=[pl.BlockSpec((T, LANES), lambda hp, i: (i, hp)), pl.BlockSpec((T, LANES), lambda hp, i: (i, hp)),
                   pl.BlockSpec((1, 2, T), lambda hp, i: (hp, 0, i))],
        out_shape=[jax.ShapeDtypeStruct((S, D_MODEL), BF16), jax.ShapeDtypeStruct((S, D_MODEL), F32),
                   jax.ShapeDtypeStruct((HP, 2, S), F32)],
        compiler_params=_params(2, vmem))(qa, qa, ka, ka, qkv)


def _attn_stats(do, o32, lse_cols, T):
    S = do.shape[0]
    HP = ATT_HEADS // 2

    def body(do_ref, o_ref, lse_ref, st_ref):
        lane = lax.broadcasted_iota(jnp.int32, (LANES, 8), 0)
        c = lax.broadcasted_iota(jnp.int32, (LANES, 8), 1)
        sel = jnp.where(((c == 2) & (lane < ATT_HEAD_DIM)) | ((c == 3) & (lane >= ATT_HEAD_DIM)), 1.0, 0.0).astype(BF16)
        dd = _dot_exact_right(do_ref[...].astype(F32) * o_ref[...], sel)
        c8 = lax.broadcasted_iota(jnp.int32, (1, 8), 1)
        st_ref[0] = jnp.where(c8 == 0, lse_ref[0, :, 0:1], jnp.where(c8 == 1, lse_ref[0, :, 1:2], dd))

    return pl.pallas_call(
        body, name="attn_stats", grid=(HP, S // T),
        in_specs=[pl.BlockSpec((T, LANES), lambda hp, i: (i, hp)), pl.BlockSpec((T, LANES), lambda hp, i: (i, hp)),
                  pl.BlockSpec((1, T, 2), lambda hp, i: (hp, i, 0))],
        out_specs=pl.BlockSpec((1, T, 8), lambda hp, i: (hp, i, 0)), out_shape=jax.ShapeDtypeStruct((HP, S, 8), F32),
        compiler_params=_params(2))(do, o32, lse_cols)


def _attn_bwd(qa, ka, qkv, do, stats, T):
    S = qkv.shape[0]
    nq = S // T
    HP = ATT_HEADS // 2

    def body(k0_ref, k1_ref, v_ref, q0_ref, q1_ref, do_ref, st_ref, dq_ref, dk_ref, dv_ref, dck_ref, dcq_ref, dq_acc):
        j = pl.program_id(1)
        mA = _lane_mask()
        masks = (mA, jnp.logical_not(mA))
        q_refs = (q0_ref, q1_ref)

        @pl.when(j == 0)
        def _():
            dq_acc[...] = jnp.zeros_like(dq_acc)

        kas = (k0_ref[...], k1_ref[...])
        vj = v_ref[...]
        row = lax.broadcasted_iota(jnp.int32, (T, T), 0)
        col = lax.broadcasted_iota(jnp.int32, (T, T), 1)

        def block(i, carry, diag):
            dvt, dkt0, dkt1 = carry
            off = pl.multiple_of(i * T, T)
            doi = do_ref[pl.ds(off, T), :]
            zero = jnp.zeros_like(doi)
            dkts = [dkt0, dkt1]
            for h in (0, 1):
                qh = q_refs[h][pl.ds(off, T), :]
                doh = jnp.where(masks[h], doi, zero)
                lse = st_ref[0, pl.ds(off, T), h:h + 1]
                dd = st_ref[0, pl.ds(off, T), 2 + h:3 + h]
                sc = _dot_nt(qh, kas[h])
                if diag:
                    sc = jnp.where(row >= col, sc, NEG)
                p = jnp.exp(sc - lse)
                dp = _dot_nt(doh, vj)
                ds = (p * (dp - dd)).astype(BF16)
                dvt = dvt + _dot_tn(doh, p.astype(BF16))
                dkts[h] = dkts[h] + _dot_tn(qh, ds)
                dq_acc[h, pl.ds(off, T), :] += _dot(ds, kas[h])
            return (dvt, dkts[0], dkts[1])

        z = jnp.zeros((LANES, T), F32)
        carry = block(j, (z, z, z), True)
        dvt, dkt0, dkt1 = lax.fori_loop(j + 1, nq, lambda i, c: block(i, c, False), carry)
        dv_ref[...] = dvt.T.astype(BF16)
        dk_ref[...] = jnp.where(mA, dkt0.T, dkt1.T).astype(BF16)
        ones_q = (_aug_base(0), _aug_base(1))
        dck_ref[0, 0:1, :] = -dkt0[ones_q[0]:ones_q[0] + 1, :]
        dck_ref[0, 1:2, :] = -dkt1[ones_q[1]:ones_q[1] + 1, :]

        @pl.when(j == nq - 1)
        def _():
            dq0, dq1 = dq_acc[0], dq_acc[1]
            ones_k = (_aug_base(0) + N_AUG, _aug_base(1) + N_AUG)
            dq_ref[...] = (jnp.where(mA, dq0, dq1) * (1.0 / math.sqrt(ATT_HEAD_DIM))).astype(BF16)
            dcq_ref[0, :, 0:1] = dq0[:, ones_k[0]:ones_k[0] + 1]
            dcq_ref[0, :, 1:2] = dq1[:, ones_k[1]:ones_k[1] + 1]

    vmem = (2 * (3 * T * LANES * 2 + 3 * S * LANES * 2 + S * LANES * 4 + S * LANES * (2 + 4) + 2 * T * LANES * 2 + 8 * T * 4)
            + 2 * S * LANES * 4 + 12 * T * T * 4 + (8 << 20))
    kspec = lambda h: pl.BlockSpec((T, LANES), lambda hp, j, h=h: (j, 2 * hp + h))
    qspec = lambda h: pl.BlockSpec((S, LANES), lambda hp, j, h=h: (0, 2 * hp + h))
    blk = pl.BlockSpec((T, LANES), lambda hp, j: (j, hp))
    full = pl.BlockSpec((S, LANES), lambda hp, j: (0, hp))
    return pl.pallas_call(
        body, name="attn_bwd", grid=(HP, nq),
        in_specs=[kspec(0), kspec(1), pl.BlockSpec((T, LANES), lambda hp, j: (j, 2 * HP + hp)), qspec(0), qspec(1), full,
                  pl.BlockSpec((1, S, 8), lambda hp, j: (hp, 0, 0))],
        out_specs=[full, blk, blk, pl.BlockSpec((1, 2, T), lambda hp, j: (hp, 0, j)),
                   pl.BlockSpec((1, S, 2), lambda hp, j: (hp, 0, 0))],
        out_shape=[jax.ShapeDtypeStruct((S, D_MODEL), BF16)] * 3 + [jax.ShapeDtypeStruct((HP, 2, S), F32),
                                                                     jax.ShapeDtypeStruct((HP, S, 2), F32)],
        scratch_shapes=[pltpu.VMEM((2, S, LANES), F32)], compiler_params=_params(2, vmem))(ka, ka, qkv, qa, qa, do, stats)


HALO = 8


def _shift_down(x, d, above):
    r = pltpu.roll(x, d, 0)
    head = jnp.where(lax.broadcasted_iota(jnp.int32, (HALO, 1), 0) < d, pltpu.roll(above, d, 0), r[0:HALO])
    return head if x.shape[0] == HALO else jnp.concatenate([head, r[HALO:]], axis=0)


def _shift_up(x, d, below):
    n = x.shape[0]
    r = pltpu.roll(x, n - d, 0)
    tail = jnp.where(lax.broadcasted_iota(jnp.int32, (HALO, 1), 0) >= HALO - d, pltpu.roll(below, HALO - d, 0), r[n - HALO:])
    return jnp.concatenate([r[:n - HALO], tail], axis=0)


def _conv_fwd(u, w, b, ts, tc):
    S, C = u.shape
    hb = ts // HALO

    def body(u_ref, prev_ref, w_ref, b_ref, o_ref):
        i = pl.program_id(0)
        x = u_ref[...]
        above = jnp.where(i == 0, 0.0, prev_ref[...])
        acc = b_ref[...] + w_ref[3:4, :] * x
        for k in range(SSM_CONV - 1):
            acc = acc + w_ref[k:k + 1, :] * _shift_down(x, SSM_CONV - 1 - k, above)
        o_ref[...] = acc * _sigmoid(acc)

    return pl.pallas_call(
        body, name="conv_fwd", grid=(S // ts, C // tc),
        in_specs=[pl.BlockSpec((ts, tc), lambda i, j: (i, j)),
                  pl.BlockSpec((HALO, tc), lambda i, j: (jnp.maximum(i * hb - 1, 0), j)),
                  pl.BlockSpec((SSM_CONV, tc), lambda i, j: (0, j)), pl.BlockSpec((1, tc), lambda i, j: (0, j))],
        out_specs=pl.BlockSpec((ts, tc), lambda i, j: (i, j)), out_shape=jax.ShapeDtypeStruct((S, C), F32),
        compiler_params=_params(2))(u, u, w, b)


def _conv_bwd(name, u, dy, w, b, ts, tc, col0):
    S, C = dy.shape
    cb = col0 // tc
    assert cb * tc == col0
    hb = ts // HALO
    nb = S // ts

    def body(u_ref, uprev_ref, unext_ref, dy_ref, dynext_ref, w_ref, b_ref, du_ref, dw_ref, db_ref):
        i = pl.program_id(1)
        x = u_ref[...]
        above = jnp.where(i == 0, 0.0, uprev_ref[...])
        ws = [w_ref[k:k + 1, :] for k in range(SSM_CONV)]

        def dsilu(pre):
            sg = _sigmoid(pre)
            return sg * (1.0 + pre * (1.0 - sg))

        shifted = [_shift_down(x, SSM_CONV - 1 - k, above) for k in range(SSM_CONV - 1)] + [x]
        pre = b_ref[...]
        for k in range(SSM_CONV):
            pre = pre + ws[k] * shifted[k]
        g = dy_ref[...] * dsilu(pre)
        nxt = unext_ref[...]
        tail = x[ts - HALO:, :]
        pre_n = b_ref[...] + ws[SSM_CONV - 1] * nxt
        for k in range(SSM_CONV - 1):
            pre_n = pre_n + ws[k] * _shift_down(nxt, SSM_CONV - 1 - k, tail)
        g_next = jnp.where(i == nb - 1, 0.0, dynext_ref[...] * dsilu(pre_n))
        du = ws[SSM_CONV - 1] * g
        for k in range(SSM_CONV - 1):
            du = du + ws[k] * _shift_up(g, SSM_CONV - 1 - k, g_next)
        du_ref[...] = du.astype(du_ref.dtype)
        dws = [jnp.sum(g * shifted[k], axis=0, keepdims=True) for k in range(SSM_CONV)]
        dbs = jnp.sum(g, axis=0, keepdims=True)

        @pl.when(i == 0)
        def _():
            for k in range(SSM_CONV):
                dw_ref[k:k + 1, :] = dws[k]
            db_ref[...] = dbs

        @pl.when(i > 0)
        def _():
            for k in range(SSM_CONV):
                dw_ref[k:k + 1, :] += dws[k]
            db_ref[...] += dbs

    nxt = lambda off: (lambda j, i: (jnp.minimum((i + 1) * hb, S // HALO - 1), j + off))
    return pl.pallas_call(
        body, name=name, grid=(C // tc, nb),
        in_specs=[pl.BlockSpec((ts, tc), lambda j, i: (i, j + cb)),
                  pl.BlockSpec((HALO, tc), lambda j, i: (jnp.maximum(i * hb - 1, 0), j + cb)),
                  pl.BlockSpec((HALO, tc), nxt(cb)),
                  pl.BlockSpec((ts, tc), lambda j, i: (i, j)),
                  pl.BlockSpec((HALO, tc), nxt(0)),
                  pl.BlockSpec((SSM_CONV, tc), lambda j, i: (0, j + cb)), pl.BlockSpec((1, tc), lambda j, i: (0, j + cb))],
        out_specs=[pl.BlockSpec((ts, tc), lambda j, i: (i, j)), pl.BlockSpec((SSM_CONV, tc), lambda j, i: (0, j)),
                   pl.BlockSpec((1, tc), lambda j, i: (0, j))],
        out_shape=[jax.ShapeDtypeStruct((S, C), BF16), jax.ShapeDtypeStruct((SSM_CONV, C), F32), jax.ShapeDtypeStruct((1, C), F32)],
        compiler_params=_params(2))(u, u, u, dy, dy, w, b)


def _head_sum():
    lane = jnp.right_shift(lax.broadcasted_iota(jnp.int32, (GROUP_LANES, 8), 0), 6)
    r = lax.broadcasted_iota(jnp.int32, (GROUP_LANES, 8), 1)
    return jnp.where(lane == r, 1.0, 0.0).astype(BF16)


def _head_expand():
    r = lax.broadcasted_iota(jnp.int32, (8, GROUP_LANES), 0)
    c = jnp.right_shift(lax.broadcasted_iota(jnp.int32, (8, GROUP_LANES), 1), 6)
    return jnp.where(r == c, 1.0, 0.0).astype(BF16)


def _ssd_common(dtc_ref, dtr_ref, bias_r, alog_b, bias_c, alog_c, L):
    a_b = -jnp.exp(alog_b)
    dt = _dot_exact_right(_softplus(dtc_ref[0] + bias_r), _head_expand())
    acum = _dot_exact_left(_tri(L, False), dt * a_b)
    a_c = -jnp.exp(alog_c)
    dtr = _softplus(dtr_ref[0] + bias_c)
    acum_r = _dot_exact_right(dtr * a_c, _tri(L, True))
    return a_b, dt, acum, acum_r


def _ssd_specs(L, nc, rev):
    cc = (lambda c: nc - 1 - c) if rev else (lambda c: c)
    G = SSM_GROUPS
    blk = pl.BlockSpec((L, GROUP_LANES), lambda g, c: (cc(c), g))
    dtc = pl.BlockSpec((1, L, 8), lambda g, c: (g, cc(c), 0))
    rowv = pl.BlockSpec((1, 1, 8), lambda g, c: (g, 0, 0))
    xs = blk
    bm = pl.BlockSpec((L, SSM_STATE), lambda g, c: (cc(c), SSM_INNER // SSM_STATE + g))
    cm = pl.BlockSpec((L, SSM_STATE), lambda g, c: (cc(c), SSM_INNER // SSM_STATE + G + g))
    dtr = pl.BlockSpec((1, 8, L), lambda g, c: (g, 0, cc(c)))
    vec = pl.BlockSpec((1, GROUP_LANES), lambda g, c: (0, g))
    colv = pl.BlockSpec((1, 8, 1), lambda g, c: (g, 0, 0))
    hs = pl.BlockSpec((1, 1, SSM_STATE, GROUP_LANES), lambda g, c: (g, cc(c), 0, 0))
    return blk, xs, bm, cm, dtc, dtr, vec, rowv, colv, hs


def _ssd_fwd(xbc, z, dtc, dtr, bias_r, alog_b, dskip_b, normw, bias_c, alog_c, L):
    S = z.shape[0]
    nc = S // L
    blk, xs, bm, cm, dtcs, dtrs, vec, rowv, colv, hs = _ssd_specs(L, nc, False)

    def body(x_ref, b_ref, c_ref, z_ref, dtc_ref, dtr_ref, bias_ref, alog_ref, dskip_ref, nw_ref, biasc_ref, alogc_ref,
             y_ref, ssm_ref, hs_ref, h_scr):
        c = pl.program_id(1)

        @pl.when(c == 0)
        def _():
            h_scr[...] = jnp.zeros_like(h_scr)

        mA = _lane_mask()
        a_b, dt, acum, acum_r = _ssd_common(dtc_ref, dtr_ref, bias_ref[0], alog_ref[...], biasc_ref[0], alogc_ref[0], L)
        x = x_ref[...]
        cb, bb = c_ref[...].astype(BF16), b_ref[...].astype(BF16)
        hprev = h_scr[...]
        hs_ref[0, 0] = hprev
        xdt = x * dt
        xdt_b = xdt.astype(BF16)
        gmat = _dot_nt(cb, bb)
        row = lax.broadcasted_iota(jnp.int32, (L, L), 0)
        col = lax.broadcasted_iota(jnp.int32, (L, L), 1)
        parts = []
        for p in range(GROUP_LANES // LANES):
            xp = xdt_b[:, p * LANES:(p + 1) * LANES]
            yd = []
            for hh in (0, 1):
                r = 2 * p + hh
                acol = acum[:, r * ATT_HEAD_DIM:r * ATT_HEAD_DIM + 1]
                arow = acum_r[r:r + 1, :]
                lm = jnp.exp(jnp.where(row >= col, acol - arow, NEG))
                yd.append(_dot((gmat * lm).astype(BF16), xp))
            parts.append(jnp.where(mA, yd[0], yd[1]))
        ydiag = jnp.concatenate(parts, axis=1)
        yoff = jnp.exp(acum) * _dot(cb, hprev.astype(BF16))
        y = ydiag + yoff + dskip_ref[...] * x
        aend = acum[L - 1:L, :]
        wgt = (jnp.exp(aend - acum) * xdt).astype(BF16)
        h_scr[...] = jnp.exp(aend) * hprev + _dot_tn(bb, wgt)
        y_ref[...] = y
        zz = z_ref[...]
        u = y * (zz * _sigmoid(zz))
        rs = lax.rsqrt(jnp.mean(u * u, axis=1, keepdims=True) + RMS_EPS)
        ssm_ref[...] = (u * rs * nw_ref[...]).astype(BF16)

    return pl.pallas_call(
        body, name="ssd_fwd", grid=(SSM_GROUPS, nc),
        in_specs=[xs, bm, cm, blk, dtcs, dtrs, rowv, vec, vec, vec, colv, colv],
        out_specs=[blk, blk, hs],
        out_shape=[jax.ShapeDtypeStruct((S, SSM_INNER), F32), jax.ShapeDtypeStruct((S, SSM_INNER), BF16),
                   jax.ShapeDtypeStruct((SSM_GROUPS, nc, SSM_STATE, GROUP_LANES), F32)],
        scratch_shapes=[pltpu.VMEM((SSM_STATE, GROUP_LANES), F32)],
        compiler_params=_params(2, 48 << 20))(xbc, xbc, xbc, z, dtc, dtr, bias_r, alog_b, dskip_b, normw, bias_c, alog_c)


def _ssd_bwd(xbc, z, y, dssm, hs_all, dtc, dtr, bias_r, alog_r, alog_b, dskip_b, normw, bias_c, alog_c, L):
    S = z.shape[0]
    nc = S // L
    blk, xs, bm, cm, dtcs, dtrs, vec, rowv, colv, hs = _ssd_specs(L, nc, True)

    def body(x_ref, b_ref, c_ref, z_ref, y_ref, dssm_ref, hs_ref, dtc_ref, dtr_ref, bias_ref, alogr_ref, alog_ref, dskip_ref, nw_ref,
             biasc_ref, alogc_ref,
             dx_ref, db_ref, dc_ref, dz_ref, ddt_ref, dnw_ref, ddskip_ref, dalog_ref, dbias_ref, dh_scr):
        c = pl.program_id(1)

        @pl.when(c == 0)
        def _():
            dh_scr[...] = jnp.zeros_like(dh_scr)

        mA = _lane_mask()
        masks = (mA, jnp.logical_not(mA))
        a_b, dt, acum, acum_r = _ssd_common(dtc_ref, dtr_ref, bias_ref[0], alog_ref[...], biasc_ref[0], alogc_ref[0], L)
        x, zz, y, dssm = x_ref[...], z_ref[...], y_ref[...], dssm_ref[...]
        cb, bb = c_ref[...].astype(BF16), b_ref[...].astype(BF16)
        hprev = hs_ref[0, 0]
        hb = hprev.astype(BF16)
        ds = dh_scr[...]
        dsb = ds.astype(BF16)
        dskip = dskip_ref[...]
        aend = acum[L - 1:L, :]
        e_a, e_end = jnp.exp(acum), jnp.exp(aend)
        dte = jnp.exp(aend - acum)
        xdt = x * dt
        xdt_b = xdt.astype(BF16)
        sg = _sigmoid(zz)
        sz = zz * sg
        u = y * sz
        rs = lax.rsqrt(jnp.mean(u * u, axis=1, keepdims=True) + RMS_EPS)
        un = u * rs
        dun = dssm * nw_ref[...]
        du = rs * (dun - un * jnp.mean(dun * un, axis=1, keepdims=True))
        dy = du * sz
        dz_ref[...] = (du * y * sg * (1.0 + zz * (1.0 - sg))).astype(dz_ref.dtype)
        dy_b = dy.astype(BF16)
        dch_b = (dy * e_a).astype(BF16)
        dc = _dot_nt(dch_b, hb)
        dhprev = _dot_tn(cb, dch_b)
        gt = _dot_nt(bb, cb)
        row = lax.broadcasted_iota(jnp.int32, (L, L), 0)
        col = lax.broadcasted_iota(jnp.int32, (L, L), 1)
        dgt = jnp.zeros((L, L), F32)
        parts = []
        for p in range(GROUP_LANES // LANES):
            xp = xdt_b[:, p * LANES:(p + 1) * LANES]
            dyp = dy_b[:, p * LANES:(p + 1) * LANES]
            zero = jnp.zeros_like(dyp)
            acc = None
            for hh in (0, 1):
                r = 2 * p + hh
                acol = acum[:, r * ATT_HEAD_DIM:r * ATT_HEAD_DIM + 1]
                arow = acum_r[r:r + 1, :]
                lmt = jnp.exp(jnp.where(row <= col, arow - acol, NEG))
                dyh = jnp.where(masks[hh], dyp, zero)
                part = _dot((gt * lmt).astype(BF16), dyh)
                acc = part if acc is None else acc + part
                dgt = dgt + _dot_nt(xp, dyh) * lmt
            parts.append(acc)
        dxdt_diag = jnp.concatenate(parts, axis=1)
        dgt_b = dgt.astype(BF16)
        db = _dot(dgt_b, cb)
        dc = dc + _dot_tn(dgt_b, bb)
        dxdt_state = dte * _dot(bb, dsb)
        db = db + _dot_nt((dte * xdt).astype(BF16), dsb)
        dxdt = dxdt_diag + dxdt_state
        dy_r, xdt_r = dy_b.astype(F32), xdt_b.astype(F32)
        dac = dy_r * (y - dskip * x) - xdt_r * dxdt
        tail = jnp.sum(xdt_r * dxdt_state, axis=0, keepdims=True) + e_end * jnp.sum(ds * hprev, axis=0, keepdims=True)
        rowl = lax.broadcasted_iota(jnp.int32, (L, 1), 0)
        dac = dac + jnp.where(rowl == L - 1, tail, 0.0)
        rc = _dot_exact_left(_tri(L, True), dac)
        hsum = _head_sum()
        hs1 = _dot_exact_right(dxdt * x, hsum, 2)
        hs2 = _dot_exact_right(rc, hsum, 2)
        a8 = -jnp.exp(alogr_ref[0])
        dtraw8 = dtc_ref[0] + bias_ref[0]
        ddtraw = (hs1 + a8 * hs2) * _sigmoid(dtraw8)
        dx_ref[...] = dskip * dy + dxdt * dt
        db_ref[...] = db
        dc_ref[...] = dc
        ddt_ref[0] = ddtraw
        dh_scr[...] = e_end * ds + dhprev
        sums = (jnp.sum(dssm * un, axis=0, keepdims=True), jnp.sum(dy * x, axis=0, keepdims=True))
        refs = (dnw_ref, ddskip_ref)
        sums8 = (a8 * jnp.sum(hs2 * _softplus(dtraw8), axis=0, keepdims=True), jnp.sum(ddtraw, axis=0, keepdims=True))
        refs8 = (dalog_ref, dbias_ref)

        @pl.when(c == 0)
        def _():
            for r, v in zip(refs, sums):
                r[...] = v
            for r, v in zip(refs8, sums8):
                r[0] = v

        @pl.when(c > 0)
        def _():
            for r, v in zip(refs, sums):
                r[...] += v
            for r, v in zip(refs8, sums8):
                r[0] += v

    nbc = pl.BlockSpec((L, SSM_STATE), lambda g, c: (nc - 1 - c, g))
    return pl.pallas_call(
        body, name="ssd_bwd", grid=(SSM_GROUPS, nc),
        in_specs=[xs, bm, cm, blk, blk, blk, hs, dtcs, dtrs, rowv, rowv, vec, vec, vec, colv, colv],
        out_specs=[blk, nbc, nbc, blk, dtcs, vec, vec, rowv, rowv],
        out_shape=[jax.ShapeDtypeStruct((S, SSM_INNER), F32), jax.ShapeDtypeStruct((S, SSM_GROUPS * SSM_STATE), F32),
                   jax.ShapeDtypeStruct((S, SSM_GROUPS * SSM_STATE), F32), jax.ShapeDtypeStruct((S, SSM_INNER), BF16),
                   jax.ShapeDtypeStruct((SSM_GROUPS, S, 8), F32)] + [jax.ShapeDtypeStruct((1, SSM_INNER), F32)] * 2
                  + [jax.ShapeDtypeStruct((SSM_GROUPS, 1, 8), F32)] * 2,
        scratch_shapes=[pltpu.VMEM((SSM_STATE, GROUP_LANES), F32)],
        compiler_params=_params(2, 56 << 20))(xbc, xbc, xbc, z, y, dssm, hs_all, dtc, dtr, bias_r, alog_r, alog_b, dskip_b,
                                              normw, bias_c, alog_c)


def _place():
    return lax.axis_index("x"), lax.axis_index("y"), lax.axis_index("c")


def _other_chips(x, y):
    return [(1 - x, y), (x, 1 - y), (1 - x, 1 - y)]


def _half_rows(rows, which):
    hr = rows // 2
    if isinstance(which, int):
        return pl.ds(which * hr, hr)
    return pl.ds(pl.multiple_of(which * hr, 8), hr)


def _chip_gather(name, shards, split):
    n = len(shards)
    ANY = pl.BlockSpec(memory_space=pl.ANY)

    def body(*refs):
        ins, outs = refs[:n], refs[n:2 * n]
        send, recv, fsend, frecv = refs[2 * n:]
        x, y, c = _place()
        me = 2 * x + y
        sibling = (x, y, 1 - c)
        chips = _other_chips(x, y)

        def piece(a, chip_idx, which):
            if split[a]:
                return outs[a].at[chip_idx, _half_rows(shards[a].shape[0], which)]
            return outs[a].at[chip_idx]

        def ici(k, a, to_chip, src_chip):
            src = ins[a].at[_half_rows(shards[a].shape[0], c)] if split[a] else ins[a]
            return pltpu.make_async_remote_copy(src_ref=src, dst_ref=piece(a, src_chip, c), send_sem=send.at[k, a],
                                                recv_sem=recv.at[k, a], device_id=(*to_chip, c), device_id_type=MESH)

        def fwd(k, a, src_chip, which):
            return pltpu.make_async_remote_copy(src_ref=piece(a, src_chip, which), dst_ref=piece(a, src_chip, which),
                                                send_sem=fsend.at[k, a], recv_sem=frecv.at[k, a], device_id=sibling,
                                                device_id_type=MESH)

        sends = []
        for k, chip in enumerate(chips):
            for a in range(n):
                cp = ici(k, a, chip, me)
                cp.start()
                sends.append(cp)
        for k, (ox, oy) in enumerate(chips):
            src = 2 * ox + oy
            for a in range(n):
                ici(k, a, (ox, oy), src).wait_recv()
                if split[a]:
                    cp = fwd(k, a, src, c)
                    cp.start()
                    sends.append(cp)
        for k, (ox, oy) in enumerate(chips):
            for a in range(n):
                if split[a]:
                    fwd(k, a, 2 * ox + oy, 1 - c).wait_recv()
        for cp in sends:
            cp.wait_send()

    sem = pltpu.SemaphoreType.DMA((3, n))
    return pl.pallas_call(
        body, name=name, in_specs=[ANY] * n, out_specs=[ANY] * n,
        out_shape=[jax.ShapeDtypeStruct((4,) + s.shape, s.dtype) for s in shards],
        scratch_shapes=[sem, sem, sem, sem])(*shards)


def _chip_copies_start(name, srcs, per_chip_src, after):
    n = len(srcs)
    HBM = pl.BlockSpec(memory_space=pltpu.HBM)
    SEM = pl.BlockSpec(memory_space=pltpu.SEMAPHORE)
    lands = [pltpu.with_memory_space_constraint(lax.empty(a.shape if per_chip_src else (4,) + a.shape, a.dtype), pltpu.HBM)
             for a in srcs]

    def body(*refs):
        ins, land = refs[:n], refs[n:2 * n]
        send, recv = refs[2 * n + 1], refs[2 * n + 2]
        token = refs[-1]
        x, y, c = _place()
        me = 2 * x + y
        for k, (ox, oy) in enumerate(_other_chips(x, y)):
            for a in range(n):
                src = ins[a].at[2 * ox + oy] if per_chip_src else ins[a]
                pltpu.make_async_remote_copy(src_ref=src, dst_ref=land[a].at[me], send_sem=send.at[k * n + a], recv_sem=recv.at[k * n + a],
                                             device_id=(ox, oy, c), device_id_type=MESH).start()
        token[...] = jnp.zeros_like(token)

    sem = pltpu.SemaphoreType.DMA((3 * n,))
    res = pl.pallas_call(
        body, name=name,
        out_shape=[sem, sem] + [pltpu.HBM(a.shape, a.dtype) for a in srcs] + [pltpu.HBM(b.shape, b.dtype) for b in lands]
                  + [jax.ShapeDtypeStruct((8, LANES), F32)],
        in_specs=[HBM] * (2 * n) + [pl.BlockSpec(memory_space=pl.ANY)],
        out_specs=[SEM, SEM] + [HBM] * (2 * n) + [pl.BlockSpec(memory_space=pltpu.VMEM)],
        input_output_aliases={k: 2 + k for k in range(2 * n)},
        compiler_params=pltpu.CompilerParams(has_side_effects=pltpu.SideEffectType.DATAFLOW_SIDE_EFFECTING),
    )(*[pltpu.with_memory_space_constraint(a, pltpu.HBM) for a in srcs], *lands, after)
    return res[:-1], res[-1]


def _chip_copies_wait(name, started, per_chip_src, after):
    send, recv = started[0], started[1]
    n = (len(started) - 2) // 2
    srcs, lands = started[2:2 + n], started[2 + n:]
    HBM = pl.BlockSpec(memory_space=pltpu.HBM)
    SEM = pl.BlockSpec(memory_space=pltpu.SEMAPHORE)

    def body(*refs):
        ins, land = refs[:n], refs[n:2 * n]
        send_sem, recv_sem = refs[2 * n], refs[2 * n + 1]
        x, y, c = _place()
        me = 2 * x + y
        for k, (ox, oy) in enumerate(_other_chips(x, y)):
            for a in range(n):
                src = ins[a].at[me] if per_chip_src else ins[a]
                cp = pltpu.make_async_remote_copy(src_ref=src, dst_ref=land[a].at[2 * ox + oy], send_sem=send_sem.at[k * n + a],
                                                  recv_sem=recv_sem.at[k * n + a], device_id=(ox, oy, c), device_id_type=MESH)
                cp.wait_send()
                cp.wait_recv()

    res = pl.pallas_call(
        body, name=name,
        out_shape=[pltpu.HBM(a.shape, a.dtype) for a in srcs] + [pltpu.HBM(b.shape, b.dtype) for b in lands],
        in_specs=[HBM] * (2 * n) + [SEM, SEM, pl.BlockSpec(memory_space=pl.ANY)], out_specs=[HBM] * (2 * n),
        input_output_aliases={k: k for k in range(2 * n)},
        compiler_params=pltpu.CompilerParams(has_side_effects=pltpu.SideEffectType.DATAFLOW_SIDE_EFFECTING),
    )(*srcs, *lands, send, recv, after)
    return res[n:]


def _half_to_sibling(name, blocks):
    n = len(blocks)
    ANY = pl.BlockSpec(memory_space=pl.ANY)

    def body(*refs):
        ins, outs = refs[:n], refs[n:2 * n]
        send, recv = refs[2 * n:]
        x, y, c = _place()
        cps = [pltpu.make_async_remote_copy(src_ref=ins[a].at[:, _half_rows(blocks[a].shape[1], 1 - c)], dst_ref=outs[a],
                                            send_sem=send.at[a], recv_sem=recv.at[a], device_id=(x, y, 1 - c),
                                            device_id_type=MESH) for a in range(n)]
        for cp in cps:
            cp.start()
        for cp in cps:
            cp.wait_recv()
        for cp in cps:
            cp.wait_send()

    return pl.pallas_call(
        body, name=name, in_specs=[ANY] * n, out_specs=[ANY] * n,
        out_shape=[jax.ShapeDtypeStruct((4, b.shape[1] // 2, b.shape[2]), b.dtype) for b in blocks],
        scratch_shapes=[pltpu.SemaphoreType.DMA((n,)), pltpu.SemaphoreType.DMA((n,))])(*blocks)


def _sibling_swap(name, arrs):
    n = len(arrs)
    ANY = pl.BlockSpec(memory_space=pl.ANY)

    def body(*refs):
        ins, outs = refs[:n], refs[n:2 * n]
        send, recv = refs[2 * n:]
        x, y, c = _place()
        cps = [pltpu.make_async_remote_copy(src_ref=ins[a], dst_ref=outs[a], send_sem=send.at[a], recv_sem=recv.at[a],
                                            device_id=(x, y, 1 - c), device_id_type=MESH) for a in range(n)]
        for cp in cps:
            cp.start()
        for cp in cps:
            cp.wait_recv()
        for cp in cps:
            cp.wait_send()

    return pl.pallas_call(
        body, name=name, in_specs=[ANY] * n, out_specs=[ANY] * n,
        out_shape=[jax.ShapeDtypeStruct(a.shape, a.dtype) for a in arrs],
        scratch_shapes=[pltpu.SemaphoreType.DMA((n,)), pltpu.SemaphoreType.DMA((n,))])(*arrs)


N_DEV = 8


def _all_sum_small(vec):
    P = vec.shape[1]

    def body(v_ref, o_ref, buf, send, recv):
        x, y, c = _place()
        me = 4 * x + 2 * y + c
        buf[me] = v_ref[...]

        def peer(r):
            return ((1 - x) if (r >> 2) & 1 else x, (1 - y) if (r >> 1) & 1 else y, (1 - c) if r & 1 else c)

        sends = []
        for r in range(1, N_DEV):
            cp = pltpu.make_async_remote_copy(src_ref=v_ref, dst_ref=buf.at[me], send_sem=send.at[r], recv_sem=recv.at[r],
                                              device_id=peer(r), device_id_type=MESH)
            cp.start()
            sends.append(cp)
        for r in range(1, N_DEV):
            px, py, pc = peer(r)
            pltpu.make_async_remote_copy(src_ref=v_ref, dst_ref=buf.at[4 * px + 2 * py + pc], send_sem=send.at[r],
                                         recv_sem=recv.at[r], device_id=(px, py, pc), device_id_type=MESH).wait_recv()
        for cp in sends:
            cp.wait_send()
        tot = buf[0]
        for d in range(1, N_DEV):
            tot = tot + buf[d]
        o_ref[...] = tot

    return pl.pallas_call(
        body, name="all_sum_small", in_specs=[pl.BlockSpec(memory_space=pltpu.VMEM)],
        out_specs=pl.BlockSpec(memory_space=pltpu.VMEM), out_shape=jax.ShapeDtypeStruct((1, P), F32),
        scratch_shapes=[pltpu.VMEM((N_DEV, 1, P), F32), pltpu.SemaphoreType.DMA((N_DEV,)), pltpu.SemaphoreType.DMA((N_DEV,))],
    )(vec)


def _half_sum(name, blocks, theirs, core, tr):
    _, R, C = blocks.shape
    hr = R // 2
    nb = hr // tr
    assert nb * tr == hr

    def body(c_ref, a_ref, b_ref, o_ref):
        o_ref[...] = (a_ref[...] + b_ref[...]).astype(BF16)

    grid_spec = pltpu.PrefetchScalarGridSpec(
        num_scalar_prefetch=1, grid=(4, nb),
        in_specs=[pl.BlockSpec((1, tr, C), lambda b, i, c_ref: (b, c_ref[0] * nb + i, 0)),
                  pl.BlockSpec((1, tr, C), lambda b, i, c_ref: (b, i, 0))],
        out_specs=pl.BlockSpec((1, tr, C), lambda b, i, c_ref: (b, i, 0)))
    return pl.pallas_call(body, name=name, grid_spec=grid_spec, out_shape=jax.ShapeDtypeStruct((4, hr, C), BF16),
                          compiler_params=_params(2, 40 << 20))(core, blocks, theirs)


def _sum4(name, stack, mine, chip, tr):
    _, R, C = stack.shape

    def body(chip_ref, s_ref, m_ref, o_ref):
        t = [jnp.where(chip_ref[0] == j, m_ref[j], s_ref[j]).astype(F32) for j in range(4)]
        o_ref[...] = ((t[0] + t[1]) + t[2]) + t[3]

    blk = pl.BlockSpec((4, tr, C), lambda i, chip_ref: (0, i, 0))
    grid_spec = pltpu.PrefetchScalarGridSpec(num_scalar_prefetch=1, grid=(R // tr,), in_specs=[blk, blk],
                                             out_specs=pl.BlockSpec((tr, C), lambda i, chip_ref: (i, 0)))
    return pl.pallas_call(body, name=name, grid_spec=grid_spec, out_shape=jax.ShapeDtypeStruct((R, C), F32),
                          compiler_params=_params(1, 40 << 20))(chip, stack, mine)


def _adamw_math(w, m, v, g):
    c1 = 1.0 - ADAM_B1 ** ADAM_STEP
    c2 = 1.0 - ADAM_B2 ** ADAM_STEP
    nm = ADAM_B1 * m + (1.0 - ADAM_B1) * g
    nv = ADAM_B2 * v + (1.0 - ADAM_B2) * (g * g)
    return -ADAM_LR * ((nm / c1) / (jnp.sqrt(nv / c2) + ADAM_EPS) + ADAM_WD * w), nm, nv


def _adamw(name, w, m, v, g, tr):
    R, C = w.shape

    def body(w_ref, m_ref, v_ref, ga_ref, g_ref, d_ref, nm_ref, nv_ref):
        g = ga_ref[...]
        g_ref[...] = g
        d_ref[...], nm_ref[...], nv_ref[...] = _adamw_math(w_ref[...], m_ref[...], v_ref[...], g)

    spec = pl.BlockSpec((tr, C), lambda i: (i, 0))
    return pl.pallas_call(body, name=name, grid=(R // tr,), in_specs=[spec] * 4, out_specs=[spec] * 4,
                          out_shape=[jax.ShapeDtypeStruct((R, C), F32)] * 4, compiler_params=_params(1, 40 << 20))(w, m, v, g)


def _adamw_halves(name, w, m, v, mine, theirs, core, tr):
    _, R, C = w.shape
    nb = (R // 2) // tr
    assert 2 * nb * tr == R

    def body(c_ref, w_ref, m_ref, v_ref, a_ref, b_ref, g_ref, d_ref, nm_ref, nv_ref):
        g = jnp.where((pl.program_id(0) // nb) == c_ref[0], a_ref[...], b_ref[...])
        g_ref[0] = g
        d_ref[0], nm_ref[0], nv_ref[0] = _adamw_math(w_ref[0], m_ref[0], v_ref[0], g)

    spec = pl.BlockSpec((1, tr, C), lambda i, c_ref: (0, i, 0))
    half = lambda own: pl.BlockSpec((tr, C), lambda i, c_ref, own=own: (
        jnp.clip(i - (c_ref[0] if own else 1 - c_ref[0]) * nb, 0, nb - 1), 0))
    grid_spec = pltpu.PrefetchScalarGridSpec(num_scalar_prefetch=1, grid=(R // tr,),
                                             in_specs=[spec, spec, spec, half(True), half(False)], out_specs=[spec] * 4)
    return pl.pallas_call(body, name=name, grid_spec=grid_spec, out_shape=[jax.ShapeDtypeStruct((1, R, C), F32)] * 4,
                          compiler_params=_params(1, 40 << 20))(core, w, m, v, mine, theirs)


def _row_tile(rows, cols, budget_bytes=1 << 20, mult=8):
    best = None
    for t in range(mult, rows + 1, mult):
        if rows % t == 0 and t * cols * 4 <= budget_bytes:
            best = t
    return best if best is not None else rows


def _ln_fwd(r, g, b):
    mu = jnp.mean(r, axis=1, keepdims=True)
    xc = r - mu
    rstd = lax.rsqrt(jnp.mean(xc * xc, axis=1, keepdims=True) + LN_EPS)
    xhat = xc * rstd
    return xhat * g + b, xhat, rstd


def _ln_bwd(dy, xhat, rstd, g):
    dxh = dy * g
    return rstd * (dxh - jnp.mean(dxh, axis=1, keepdims=True) - xhat * jnp.mean(dxh * xhat, axis=1, keepdims=True))


def _to_chip_blocks_cols(a):
    R, C4 = a.shape
    return a.reshape(R, 4, C4 // 4).transpose(1, 0, 2)


def _from_chip_blocks_cols(a):
    return a.transpose(1, 0, 2).reshape(a.shape[1], 4 * a.shape[2])


def kernel(x, w_in, b_forget, conv_w, conv_b, dt_bias, a_log, d_skip, ssm_norm_w, w_proj_attn, w_proj_ssm, b_gates, w_out, ln1_g, ln1_b, w_ffn_gate, w_ffn_up, w_ffn_down, ln2_g, ln2_b, loss_target, m_w_in, m_b_forget, m_conv_w, m_conv_b, m_dt_bias, m_a_log, m_d_skip, m_ssm_norm_w, m_w_proj_attn, m_w_proj_ssm, m_b_gates, m_w_out, m_ln1_g, m_ln1_b, m_w_ffn_gate, m_w_ffn_up, m_w_ffn_down, m_ln2_g, m_ln2_b, v_w_in, v_b_forget, v_conv_w, v_conv_b, v_dt_bias, v_a_log, v_d_skip, v_ssm_norm_w, v_w_proj_attn, v_w_proj_ssm, v_b_gates, v_w_out, v_ln1_g, v_ln1_b, v_w_ffn_gate, v_w_ffn_up, v_w_ffn_down, v_ln2_g, v_ln2_b):
    S = x.shape[1]
    D = D_MODEL
    TM, TM2, TA, AQF, LC, CV, TS, TB = (min(TILES[k], S) for k in ("TM", "TM2", "TA", "AQF", "LC", "CV", "TS", "TB"))
    xf = x[0]
    tgt = loss_target[0]
    xb = xf.astype(BF16)

    shards = [w_in[0].astype(BF16), conv_w[0], w_proj_attn[0].astype(BF16), w_proj_ssm[0].astype(BF16), w_out[0].astype(BF16),
              w_ffn_gate[0].astype(BF16), w_ffn_up[0].astype(BF16), w_ffn_down[0].astype(BF16)]
    chip = 2 * lax.axis_index("x") + lax.axis_index("y")
    own = lambda gathered, mine: [lax.dynamic_update_slice(g, sh[None], (chip, 0, 0)) for g, sh in zip(gathered, mine)]
    g_in, g_cw = own(_chip_gather("gather_w_in", shards[:2], [True, False]), shards[:2])
    later, gather_token = _chip_copies_start("gather_rest_start", shards[2:], False, g_cw)
    w_full = _from_chip_blocks_cols(g_in)
    w_re = jnp.concatenate([w_full[:, 0:3072], w_full[:, 3088:5136], w_full[:, 5136:8208], w_full[:, 8240:10288],
                            w_full[:, 3072:3088], w_full[:, 8208:8240], jnp.zeros((D, 80), BF16)], axis=1)
    conv_w_full = _from_chip_blocks_cols(g_cw)

    def plain(accs, rows, vecs, j):
        return [accs[0]], []

    def q_scaled(accs, rows, vecs, j):
        return [accs[0] * jnp.where(j * 512 < D, 1.0 / math.sqrt(ATT_HEAD_DIM), 1.0)], []

    qkv, = _mm("proj_qkv", S, 3072, TM, 512, [(xb, D, 0)], [(w_re, 0)], [(0, 0)], q_scaled, [(3072, BF16, 0)],
               after=[gather_token])
    z, = _mm("proj_z", S, 2048, TM, 512, [(xb, D, 0)], [(w_re, RE_Z // 512)], [(0, 0)], plain, [(2048, F32, 0)])
    xbc_raw, = _mm("proj_xbc", S, 3072, TM, 512, [(xb, D, 0)], [(w_re, RE_XBC // 512)], [(0, 0)], plain, [(3072, F32, 0)])
    gl, = _mm("proj_gate", S, 2048, TM, 512, [(xb, D, 0)], [(w_re, RE_GATE // 512)], [(0, 0)], plain, [(2048, F32, 0)])
    small, = _mm("proj_small", S, 128, TM, 128, [(xb, D, 0)], [(w_re, RE_SMALL // 128)], [(0, 0)], plain, [(128, F32, 0)])

    bvec = jnp.concatenate([b_forget, jnp.zeros((1, LANES - ATT_HEADS), F32)], axis=1)
    cum = _cum_fwd(small, bvec, TB)[:, :ATT_HEADS]
    cum_cols = cum.reshape(S, 8, 2).transpose(1, 0, 2)
    qa, ka = _attn_prep(qkv, cum_cols, TM)
    o, o32, lse_rows = _attn_fwd(qa, ka, qkv, AQF, TA)

    cb_row = conv_b
    xbc = _conv_fwd(xbc_raw, conv_w_full, cb_row, CV, 512)
    dt_raw = small[:, 16:48]
    dtc = dt_raw.reshape(S, SSM_GROUPS, 8).transpose(1, 0, 2)
    dtr = dt_raw.T.reshape(SSM_GROUPS, 8, S)
    bias_r = dt_bias.reshape(SSM_GROUPS, 1, 8)
    alog_b = jnp.repeat(a_log, ATT_HEAD_DIM, axis=1)
    dskip_b = jnp.repeat(d_skip, ATT_HEAD_DIM, axis=1)
    bias_c = dt_bias.reshape(SSM_GROUPS, 8, 1)
    alog_c = a_log.reshape(SSM_GROUPS, 8, 1)
    y_ssd, ssm, hs_all = _ssd_fwd(xbc, z, dtc, dtr, bias_r, alog_b, dskip_b, ssm_norm_w, bias_c, alog_c, LC)

    def merge(accs, rows, vecs, j):
        g0, g1 = _sigmoid(rows[0] + vecs[0]), _sigmoid(rows[1] + vecs[1])
        return [g0 * accs[0] + g1 * accs[1], accs[0], accs[1]], []

    g_pa, g_ps, g_out, g_fg, g_fu, g_fd = own(_chip_copies_wait("gather_rest_wait", later, False, o), shards[2:])
    wpa, wps, wout = g_pa.reshape(D, D), g_ps.reshape(SSM_INNER, D), g_out.reshape(D, D)
    wfg, wfu, wfd = _from_chip_blocks_cols(g_fg), _from_chip_blocks_cols(g_fu), g_fd.reshape(FFN_HIDDEN, D)
    mix, attn_d, ssm_d = _mm("merge", S, D, TM, 512, [(o, D, 0), (ssm, SSM_INNER, 0)], [(wpa, 0), (wps, 0)], [(0, 0), (1, 1)],
                             merge, [(D, BF16, 0), (D, F32, 0), (D, F32, 0)], rows=[(gl, 0), (gl, 2)],
                             vecs_n=[(b_gates, 0), (b_gates, 2)])

    def out_ln1(accs, rows, vecs, j):
        r1 = ALPHA * rows[0] + accs[0]
        h1, _, _ = _ln_fwd(r1, vecs[0], vecs[1])
        return [r1, h1, h1], []

    r1, h1, h1b = _mm("out_ln1", S, D, TM2, D, [(mix, D, 0)], [(wout, 0)], [(0, 0)], out_ln1,
                      [(D, F32, 0), (D, F32, 0), (D, BF16, 0)], rows=[(xf, 0)], vecs_n=[(ln1_g, 0), (ln1_b, 0)])

    FT = FFN_HIDDEN // 2

    def swiglu(accs, rows, vecs, j):
        g, u = accs
        return [g, u, g * _sigmoid(g) * u], []

    gate, up, hmid = _mm("ffn_up", S, FFN_HIDDEN, TM2, FT, [(h1b, D, 0)], [(wfg, 0), (wfu, 0)], [(0, 0), (0, 1)], swiglu,
                         [(FFN_HIDDEN, F32, 0), (FFN_HIDDEN, F32, 0), (FFN_HIDDEN, BF16, 0)])

    def down_ln2_loss(accs, rows, vecs, j):
        r2 = ALPHA * rows[0] + accs[0]
        yv, xhat, rstd = _ln_fwd(r2, vecs[0], vecs[1])
        diff = yv - rows[1]
        dy = diff * (1.0 / D_MODEL)
        dr2 = _ln_bwd(dy, xhat, rstd, vecs[0])
        return [dr2, dr2], [jnp.sum(dy * xhat, axis=0, keepdims=True), jnp.sum(dy, axis=0, keepdims=True),
                            (0.5 / D_MODEL) * jnp.sum(diff * diff, axis=0, keepdims=True)]

    dr2, dr2b, dln2_g, dln2_b, loss_lanes = _mm("ffn_down_ln2", S, D, TM2, D, [(hmid, FFN_HIDDEN, 0)], [(wfd, 0)], [(0, 0)],
                                               down_ln2_loss, [(D, F32, 0), (D, BF16, 0)], rows=[(h1, 0), (tgt, 0)],
                                               vecs_n=[(ln2_g, 0), (ln2_b, 0)], sums=[D, D, D])
    loss = lax.psum(jnp.sum(loss_lanes), ("x", "y", "c"))

    def dswiglu(accs, rows, vecs, j):
        g, u = rows
        sg = _sigmoid(g)
        return [accs[0] * u * sg * (1.0 + g * (1.0 - sg)), accs[0] * g * sg], []

    dgate, dup = _mm("ffn_down_bwd", S, FFN_HIDDEN, TM2, FT, [(dr2b, D, 0)], [(wfd, 0)], [(0, 0)], dswiglu,
                     [(FFN_HIDDEN, BF16, 0), (FFN_HIDDEN, BF16, 0)], nt=True, rows=[(gate, 0), (up, 0)])
    dwfd = _mm_tn("dw_ffn_down", hmid, dr2b, FFN_HIDDEN // 2, D, TS)
    dwfg = _mm_tn("dw_ffn_gate", h1b, dgate, D, FT, TS)
    dwfu = _mm_tn("dw_ffn_up", h1b, dup, D, FT, TS)
    core = lax.axis_index("c").astype(jnp.int32).reshape(1)

    def send_grads(tag, names_, blocks_, after_):
        theirs_ = _half_to_sibling("swap_halves_" + tag, blocks_)
        halves_ = [_half_sum("halfsum_" + nm, b, t, core, _row_tile(b.shape[1] // 2, b.shape[2], mult=16))
                   for nm, b, t in zip(names_, blocks_, theirs_)]
        started_, token_ = _chip_copies_start("scatter_" + tag + "_start", halves_, True, after_)
        return halves_, started_, token_

    ffn_names = ["w_ffn_gate", "w_ffn_up", "w_ffn_down"]
    ffn_halves, ffn_started, ffn_token = send_grads(
        "ffn", ffn_names, [_to_chip_blocks_cols(dwfg), _to_chip_blocks_cols(dwfu), dwfd.reshape(4, FFN_HIDDEN // 4, D)], dwfu)

    def dh1_ln1(accs, rows, vecs, j):
        dh1 = ALPHA * rows[0] + accs[0] + accs[1]
        _, xhat, rstd = _ln_fwd(rows[1], vecs[0], vecs[0])
        dr1 = _ln_bwd(dh1, xhat, rstd, vecs[0])
        return [dr1, dr1], [jnp.sum(dh1 * xhat, axis=0, keepdims=True), jnp.sum(dh1, axis=0, keepdims=True)]

    dr1, dr1b, dln1_g, dln1_b = _mm("ffn_up_bwd_ln1", S, D, TM2, D, [(dgate, FFN_HIDDEN, 0), (dup, FFN_HIDDEN, 0)],
                                    [(wfg, 0), (wfu, 0)], [(0, 0), (1, 1)], dh1_ln1, [(D, F32, 0), (D, BF16, 0)], nt=True,
                                    rows=[(dr2, 0), (r1, 0)], vecs_n=[(ln1_g, 0)], sums=[D, D], after=[ffn_token])

    def dmerge(accs, rows, vecs, j):
        dmix = accs[0]
        g0, g1 = _sigmoid(rows[0] + vecs[0]), _sigmoid(rows[1] + vecs[1])
        dgl0 = dmix * rows[2] * g0 * (1.0 - g0)
        dgl1 = dmix * rows[3] * g1 * (1.0 - g1)
        return [dmix * g0, dmix * g1, dgl0, dgl1], [jnp.sum(dgl0, axis=0, keepdims=True), jnp.sum(dgl1, axis=0, keepdims=True)]

    d_attn_d, d_ssm_d, dgl0, dgl1, dbg0, dbg1 = _mm(
        "out_bwd", S, D, TM, 512, [(dr1b, D, 0)], [(wout, 0)], [(0, 0)], dmerge, [(D, BF16, 0)] * 4, nt=True,
        rows=[(gl, 0), (gl, 2), (attn_d, 0), (ssm_d, 0)], vecs_n=[(b_gates, 0), (b_gates, 2)], sums=[D, D])
    dwout = _mm_tn("dw_out", mix, dr1b, D, D, TS)
    dwpa = _mm_tn("dw_proj_attn", o, d_attn_d, D, D, TS)
    dwps = _mm_tn("dw_proj_ssm", ssm, d_ssm_d, D, D, TS)
    mid_names = ["w_proj_attn", "w_proj_ssm", "w_out"]
    mid_halves, mid_started, mid_token = send_grads(
        "mid", mid_names, [dwpa.reshape(4, D // 4, D), dwps.reshape(4, SSM_INNER // 4, D), dwout.reshape(4, D // 4, D)], dwps)

    do, = _mm("proj_attn_bwd", S, D, TM, 512, [(d_attn_d, D, 0)], [(wpa, 0)], [(0, 0)], plain, [(D, BF16, 0)], nt=True,
              after=[mid_token])
    stats = _attn_stats(do, o32, lse_rows.transpose(0, 2, 1), AQF)
    dq, dk, dv, dck, dcq = _attn_bwd(qa, ka, qkv, do, stats, TA)

    def per_head(a):
        a = a.transpose(1, 0, 2).reshape(S, ATT_HEADS)
        return jnp.concatenate([a, jnp.zeros((S, LANES - ATT_HEADS), F32)], axis=1)

    dfl, dbf = _cum_bwd(per_head(dck.transpose(0, 2, 1)), per_head(dcq), small, bvec, TB)

    dssm, = _mm("proj_ssm_bwd", S, SSM_INNER, TM, 512, [(d_ssm_d, D, 0)], [(wps, 0)], [(0, 0)], plain, [(SSM_INNER, F32, 0)],
                nt=True)
    dxs, dbm, dcm, dz, ddt8, dnw, ddskip_b, dalog8, dbias8 = _ssd_bwd(
        xbc, z, y_ssd, dssm, hs_all, dtc, dtr, bias_r, a_log.reshape(SSM_GROUPS, 1, 8), alog_b, dskip_b, ssm_norm_w, bias_c,
        alog_c, LC)
    du_x, dcw_x, dcb_x = _conv_bwd("conv_bwd_x", xbc_raw, dxs, conv_w_full, cb_row, CV, 512, 0)
    du_b, dcw_b, dcb_b = _conv_bwd("conv_bwd_b", xbc_raw, dbm, conv_w_full, cb_row, CV, 512, SSM_INNER)
    du_c, dcw_c, dcb_c = _conv_bwd("conv_bwd_c", xbc_raw, dcm, conv_w_full, cb_row, CV, 512, SSM_INNER + SSM_GROUPS * SSM_STATE)
    dconv_w = jnp.concatenate([dcw_x, dcw_b, dcw_c], axis=1)
    dconv_b = jnp.concatenate([dcb_x, dcb_b, dcb_c], axis=1)
    ddt_raw = ddt8.transpose(1, 0, 2).reshape(S, SSM_HEADS)

    dsmall = jnp.concatenate([dfl[:, :ATT_HEADS], ddt_raw, jnp.zeros((S, 80), F32)], axis=1).astype(BF16)
    HB = SSM_GROUPS * SSM_STATE
    dw_q, dw_k, dw_v = (_mm_tn("dw_in_" + nm, xb, g_, D, D, TS) for nm, g_ in (("q", dq), ("k", dk), ("v", dv)))
    dw_z = _mm_tn("dw_in_z", xb, dz, D, D, TS)
    dw_xbc = jnp.concatenate([_mm_tn("dw_in_xs", xb, du_x, D, D, TS), _mm_tn("dw_in_b", xb, du_b, D, HB, TS),
                              _mm_tn("dw_in_c", xb, du_c, D, HB, TS)], axis=1)
    dw_g0, dw_g1 = _mm_tn("dw_in_g0", xb, dgl0, D, D, TS), _mm_tn("dw_in_g1", xb, dgl1, D, D, TS)
    dw_s = _mm_tn("dw_in_small", xb, dsmall, D, LANES, TS)
    dw_full = jnp.concatenate([dw_q, dw_k, dw_v, dw_s[:, 0:ATT_HEADS], dw_z, dw_xbc, dw_s[:, ATT_HEADS:ATT_HEADS + SSM_HEADS],
                               dw_g0, dw_g1], axis=1)

    in_halves, in_started, in_token = send_grads("in", ["w_in"], [_to_chip_blocks_cols(dw_full)], dw_full)
    def dx_first(accs, rows, vecs, j):
        return [ALPHA * rows[0] + sum(accs[1:], accs[0])], []

    def dx_more(accs, rows, vecs, j):
        return [rows[0] + sum(accs[1:], accs[0])], []

    wk = lambda col, width=D: (w_re, 0, col // width, width)
    dx_part, = _mm("dx_a", S, D, TM2, D, [(dq, D, 0), (dk, D, 0), (dv, D, 0), (dz, D, 0), (dz, D, 1)],
                   [wk(0), wk(1024), wk(2048), wk(RE_Z), wk(RE_Z + 1024)], [(k, k) for k in range(5)], dx_first,
                   [(D, F32, 0)], nt=True, rows=[(dr1, 0)], after=[in_token])
    grad_x, = _mm("dx_b", S, D, TM2, D,
                  [(du_x, D, 0), (du_x, D, 1), (du_b, HB, 0), (du_c, HB, 0), (dgl0, D, 0), (dgl1, D, 0), (dsmall, LANES, 0)],
                  [wk(RE_XBC), wk(RE_XBC + 1024), wk(RE_XBC + 2048, HB), wk(RE_XBC + 2048 + HB, HB), wk(RE_GATE),
                   wk(RE_GATE + 1024), wk(RE_SMALL, LANES)],
                  [(k, k) for k in range(7)], dx_more, [(D, F32, 0)], nt=True, rows=[(dx_part, 0)])
    names = ["w_in"] + mid_names + ffn_names
    halves = in_halves + mid_halves + ffn_halves
    stacks = (_chip_copies_wait("scatter_in_wait", in_started, True, grad_x)
              + _chip_copies_wait("scatter_mid_wait", mid_started, True, grad_x)
              + _chip_copies_wait("scatter_ffn_wait", ffn_started, True, grad_x))
    chip1 = chip.astype(jnp.int32).reshape(1)
    reduced = [_sum4("sum_" + nm, st, hv, chip1, _row_tile(st.shape[1], st.shape[2], mult=16))
               for nm, st, hv in zip(names, stacks, halves)]
    other = _sibling_swap("swap_reduced", reduced)
    big_w = [w_in, w_proj_attn, w_proj_ssm, w_out, w_ffn_gate, w_ffn_up, w_ffn_down]
    big_m = [m_w_in, m_w_proj_attn, m_w_proj_ssm, m_w_out, m_w_ffn_gate, m_w_ffn_up, m_w_ffn_down]
    big_v = [v_w_in, v_w_proj_attn, v_w_proj_ssm, v_w_out, v_w_ffn_gate, v_w_ffn_up, v_w_ffn_down]
    big = {}
    for nm, w_, m_, v_, mine, theirs in zip(names, big_w, big_m, big_v, reduced, other):
        big[nm] = _adamw_halves("adamw_" + nm, w_, m_, v_, mine, theirs, core, _row_tile(w_.shape[1] // 2, w_.shape[2]))

    dd_skip = ddskip_b.reshape(1, SSM_HEADS, ATT_HEAD_DIM).sum(axis=2)
    pieces = [dbf[:, :ATT_HEADS], dconv_w.reshape(1, SSM_CONV * SSM_CONV_DIM), dconv_b, dbias8.reshape(1, SSM_HEADS), dalog8.reshape(1, SSM_HEADS), dd_skip,
              dnw, dbg0, dbg1, dln1_g, dln1_b, dln2_g, dln2_b]
    widths = [p.shape[1] for p in pieces]
    total = sum(widths)
    P = -(-total // LANES) * LANES
    packed = jnp.concatenate(pieces + [jnp.zeros((1, P - total), F32)], axis=1)
    summed = _all_sum_small(packed)
    offs = [0]
    for wd in widths:
        offs.append(offs[-1] + wd)
    sm = [summed[:, offs[k]:offs[k + 1]] for k in range(len(pieces))]
    g_bf, g_cw_full, g_cb, g_dtb, g_al, g_ds, g_nw = sm[0], sm[1].reshape(SSM_CONV, SSM_CONV_DIM), sm[2], sm[3], sm[4], sm[5], sm[6]
    g_bg = jnp.concatenate([sm[7], sm[8]], axis=1)
    g_l1g, g_l1b, g_l2g, g_l2b = sm[9], sm[10], sm[11], sm[12]
    cshard = SSM_CONV_DIM // 4
    g_cw_shard = lax.dynamic_slice_in_dim(g_cw_full, chip * cshard, cshard, axis=1)
    small_names = ["b_forget", "conv_w", "conv_b", "dt_bias", "a_log", "d_skip", "ssm_norm_w", "b_gates", "ln1_g", "ln1_b",
                   "ln2_g", "ln2_b"]
    small_g = [g_bf, g_cw_shard.reshape(1, -1), g_cb, g_dtb, g_al, g_ds, g_nw, g_bg, g_l1g, g_l1b, g_l2g, g_l2b]
    small_w = [b_forget, conv_w[0].reshape(1, -1), conv_b, dt_bias, a_log, d_skip, ssm_norm_w, b_gates, ln1_g, ln1_b, ln2_g, ln2_b]
    small_m = [m_b_forget, m_conv_w[0].reshape(1, -1), m_conv_b, m_dt_bias, m_a_log, m_d_skip, m_ssm_norm_w, m_b_gates, m_ln1_g,
               m_ln1_b, m_ln2_g, m_ln2_b]
    small_v = [v_b_forget, v_conv_w[0].reshape(1, -1), v_conv_b, v_dt_bias, v_a_log, v_d_skip, v_ssm_norm_w, v_b_gates, v_ln1_g,
               v_ln1_b, v_ln2_g, v_ln2_b]
    sw = [a.shape[1] for a in small_w]
    stot = sum(sw)
    SP = -(-stot // LANES) * LANES

    def pack(parts):
        return jnp.concatenate(list(parts) + [jnp.zeros((1, SP - stot), F32)], axis=1).reshape(SP // LANES, LANES)

    sres = _adamw("adamw_small", pack(small_w), pack(small_m), pack(small_v), pack(small_g), SP // LANES)
    soffs = [0]
    for wd in sw:
        soffs.append(soffs[-1] + wd)
    smalls = {}
    for k, nm in enumerate(small_names):
        vals = [r.reshape(1, SP)[:, soffs[k]:soffs[k + 1]] for r in sres]
        if nm == "conv_w":
            vals = [v_.reshape(1, SSM_CONV, cshard) for v_ in vals]
        smalls[nm] = vals

    order = ["w_in", "b_forget", "conv_w", "conv_b", "dt_bias", "a_log", "d_skip", "ssm_norm_w", "w_proj_attn", "w_proj_ssm",
             "b_gates", "w_out", "ln1_g", "ln1_b", "w_ffn_gate", "w_ffn_up", "w_ffn_down", "ln2_g", "ln2_b"]
    allres = {**big, **smalls}
    outs = [loss, grad_x[None]]
    for idx in range(4):
        outs += [allres[nm][idx] for nm in order]
    return tuple(outs)
```

```python
import functools
import math

import jax
import jax.numpy as jnp
from jax import lax
from jax.experimental import pallas as pl
from jax.experimental.pallas import tpu as pltpu

F32, BF16 = jnp.float32, jnp.bfloat16
MESH = pl.DeviceIdType.MESH

D_MODEL = 1024
ATT_HEADS, ATT_HEAD_DIM = 16, 64
SSM_INNER, SSM_HEADS, SSM_GROUPS, SSM_STATE, SSM_CONV = 2048, 32, 4, 128, 4
SSM_CONV_DIM = SSM_INNER + 2 * SSM_GROUPS * SSM_STATE
GROUP_LANES = SSM_INNER // SSM_GROUPS
FFN_HIDDEN = 2816
ALPHA = 2.0 ** 0.25
LN_EPS = 1e-5
RMS_EPS = 1e-5
ADAM_LR, ADAM_B1, ADAM_B2, ADAM_EPS, ADAM_WD, ADAM_STEP = 0.001, 0.9, 0.999, 1e-08, 0.01, 10
IN_SIZES = (1024, 1024, 1024, 16, 2048, 3072, 32, 2048)
IN_WIDTH = sum(IN_SIZES)
RE_WIDTH = 3072 + 2048 + 3072 + 2048 + 128
RE_Z, RE_XBC, RE_GATE, RE_SMALL = 3072, 5120, 8192, 10240

LANES = 128
VMEM_CAP = 60 * 1024 * 1024
NEG = -1e30
TILES = dict(TM=1024, TM2=256, TM3=512, TA=512, AQF=2048, LC=256, CV=512, TS=2048, TB=256)


def _params(n_axes, vmem_bytes=None):
    return pltpu.CompilerParams(dimension_semantics=("arbitrary",) * n_axes,
                                vmem_limit_bytes=None if vmem_bytes is None else int(min(vmem_bytes, VMEM_CAP)))


def _sigmoid(v):
    return 1.0 / (1.0 + jnp.exp(-v))


def _softplus(v):
    return jnp.maximum(v, 0.0) + jnp.log(1.0 + jnp.exp(-jnp.abs(v)))


def _dot(a, b):
    return lax.dot_general(a, b, (((1,), (0,)), ((), ())), preferred_element_type=F32)


def _dot_nt(a, b):
    return lax.dot_general(a, b, (((1,), (1,)), ((), ())), preferred_element_type=F32)


def _dot_tn(a, b):
    return lax.dot_general(a, b, (((0,), (0,)), ((), ())), preferred_element_type=F32)


def _split3(v):
    hi = v.astype(BF16)
    r1 = v - hi.astype(F32)
    mid = r1.astype(BF16)
    lo = (r1 - mid.astype(F32)).astype(BF16)
    return hi, mid, lo


def _dot_exact_left(m01, v):
    hi, mid, lo = _split3(v)
    return _dot(m01, hi) + _dot(m01, mid) + _dot(m01, lo)


def _dot_exact_right(v, m01, terms=3):
    parts = _split3(v)[:terms]
    out = _dot(parts[0], m01)
    for p in parts[1:]:
        out = out + _dot(p, m01)
    return out


def _mm(name, M, N, tm, tn, lhs, rhs, pairs, e_fn, outs, *, nt=False, rows=(), vecs_n=(), sums=(), after=()):
    ni, nj = M // tm, N // tn
    assert ni * tm == M and nj * tn == N, (name, M, N, tm, tn)
    n_l, n_r, n_row, n_vn, n_o, n_s = len(lhs), len(rhs), len(rows), len(vecs_n), len(outs), len(sums)

    def body(*refs):
        pos = 0
        l_refs = refs[pos:pos + n_l]; pos += n_l
        r_refs = refs[pos:pos + n_r]; pos += n_r
        row_refs = refs[pos:pos + n_row]; pos += n_row
        vn_refs = refs[pos:pos + n_vn]; pos += n_vn + len(after)
        o_refs = refs[pos:pos + n_o]; pos += n_o
        s_refs = refs[pos:pos + n_s]; pos += n_s
        i, j = pl.program_id(0), pl.program_id(1)
        accs = []
        for li, ri in pairs:
            accs.append(_dot_nt(l_refs[li][...], r_refs[ri][...]) if nt else _dot(l_refs[li][...], r_refs[ri][...]))
        out_vals, sum_vals = e_fn(accs, [r[...] for r in row_refs], [r[...] for r in vn_refs], j)
        for r, v in zip(o_refs, out_vals):
            r[...] = v.astype(r.dtype)
        if n_s:
            col = pl.multiple_of(j * tn, LANES)

            @pl.when(i == 0)
            def _():
                for r, v in zip(s_refs, sum_vals):
                    r[:, pl.ds(col, tn)] = v

            @pl.when(i > 0)
            def _():
                for r, v in zip(s_refs, sum_vals):
                    r[:, pl.ds(col, tn)] += v

    in_specs, args, est = [], [], 0
    for arr, width, cb in lhs:
        in_specs.append(pl.BlockSpec((tm, width), lambda i, j, cb=cb: (i, cb)))
        args.append(arr); est += tm * width * arr.dtype.itemsize
    for arr, off, *ksub in rhs:
        if nt:
            kb, kw = ksub if ksub else (0, arr.shape[1])
            in_specs.append(pl.BlockSpec((tn, kw), lambda i, j, off=off, kb=kb: (j + off, kb)))
            est += tn * kw * arr.dtype.itemsize
        else:
            in_specs.append(pl.BlockSpec((arr.shape[0], tn), lambda i, j, off=off: (0, j + off)))
            est += tn * arr.shape[0] * arr.dtype.itemsize
        args.append(arr)
    for arr, off in rows:
        in_specs.append(pl.BlockSpec((tm, tn), lambda i, j, off=off: (i, j + off)))
        args.append(arr); est += tm * tn * arr.dtype.itemsize
    for arr, off in vecs_n:
        in_specs.append(pl.BlockSpec((1, tn), lambda i, j, off=off: (0, j + off)))
        args.append(arr); est += 8 * tn * 4
    for arr in after:
        in_specs.append(pl.BlockSpec(memory_space=pl.ANY))
        args.append(arr)
    out_shape, out_specs = [], []
    for total, dtype, off in outs:
        out_shape.append(jax.ShapeDtypeStruct((M, total), dtype))
        out_specs.append(pl.BlockSpec((tm, tn), lambda i, j, off=off: (i, j + off)))
        est += tm * tn * jnp.dtype(dtype).itemsize
    for total in sums:
        out_shape.append(jax.ShapeDtypeStruct((1, total), F32))
        out_specs.append(pl.BlockSpec((1, total), lambda i, j: (0, 0)))
        est += 8 * total * 4
    vmem = 2 * est + (len(pairs) + 4) * tm * tn * 4 + (8 << 20)
    return pl.pallas_call(body, name=name, grid=(ni, nj), in_specs=in_specs, out_specs=out_specs, out_shape=out_shape,
                          compiler_params=_params(2, vmem))(*args)


def _mm_tn(name, a, g, ta, tn, ts, a_cols=None, a_off=0):
    S = a.shape[0]
    Ka = a.shape[1] if a_cols is None else a_cols
    N = g.shape[1]
    assert Ka % ta == 0 and N % tn == 0 and S % ts == 0, (name, Ka, N, S)
    aoff = a_off // ta

    def body(a_ref, g_ref, o_ref):
        s = pl.program_id(2)
        part = _dot_tn(a_ref[...], g_ref[...])

        @pl.when(s == 0)
        def _():
            o_ref[...] = part

        @pl.when(s > 0)
        def _():
            o_ref[...] += part

    vmem = 2 * (ts * ta * 2 + ts * tn * 2 + ta * tn * 4) + 2 * ta * tn * 4 + (8 << 20)
    return pl.pallas_call(
        body, name=name, grid=(Ka // ta, N // tn, S // ts),
        in_specs=[pl.BlockSpec((ts, ta), lambda ia, jn, s: (s, ia + aoff)), pl.BlockSpec((ts, tn), lambda ia, jn, s: (s, jn))],
        out_specs=pl.BlockSpec((ta, tn), lambda ia, jn, s: (ia, jn)),
        out_shape=jax.ShapeDtypeStruct((Ka, N), F32), compiler_params=_params(3, vmem))(a, g)


def _tri(n, upper):
    r = lax.broadcasted_iota(jnp.int32, (n, n), 0)
    c = lax.broadcasted_iota(jnp.int32, (n, n), 1)
    return jnp.where((c >= r) if upper else (c <= r), 1.0, 0.0).astype(BF16)


def _logsig(v):
    return jnp.minimum(v, 0.0) - jnp.log(1.0 + jnp.exp(-jnp.abs(v)))


def _cum_fwd(small, bvec, tb):
    S = small.shape[0]

    def body(x_ref, b_ref, o_ref, carry):
        i = pl.program_id(0)

        @pl.when(i == 0)
        def _():
            carry[...] = jnp.zeros_like(carry)

        logf = _logsig(x_ref[...] + b_ref[...])
        cum = _dot_exact_left(_tri(tb, False), logf) + carry[0:1, :]
        o_ref[...] = cum
        carry[0:1, :] = cum[tb - 1:tb, :]

    return pl.pallas_call(
        body, name="cum_fwd", grid=(S // tb,),
        in_specs=[pl.BlockSpec((tb, LANES), lambda i: (i, 0)), pl.BlockSpec((1, LANES), lambda i: (0, 0))],
        out_specs=pl.BlockSpec((tb, LANES), lambda i: (i, 0)), out_shape=jax.ShapeDtypeStruct((S, LANES), F32),
        scratch_shapes=[pltpu.VMEM((8, LANES), F32)], compiler_params=_params(1))(small, bvec)


def _cum_bwd(dcum_k, dcum_q, small, bvec, tb):
    S = small.shape[0]
    nb = S // tb

    def body(dk_ref, dq_ref, x_ref, b_ref, o_ref, s_ref, carry):
        i = pl.program_id(0)

        @pl.when(i == 0)
        def _():
            carry[...] = jnp.zeros_like(carry)
            s_ref[...] = jnp.zeros_like(s_ref)

        rc = _dot_exact_left(_tri(tb, True), dk_ref[...] + dq_ref[...]) + carry[0:1, :]
        dfl = rc * _sigmoid(-(x_ref[...] + b_ref[...]))
        o_ref[...] = dfl
        s_ref[...] += jnp.sum(dfl, axis=0, keepdims=True)
        carry[0:1, :] = rc[0:1, :]

    rev = lambda i: (nb - 1 - i, 0)
    return pl.pallas_call(
        body, name="cum_bwd", grid=(nb,),
        in_specs=[pl.BlockSpec((tb, LANES), rev)] * 3 + [pl.BlockSpec((1, LANES), lambda i: (0, 0))],
        out_specs=[pl.BlockSpec((tb, LANES), rev), pl.BlockSpec((1, LANES), lambda i: (0, 0))],
        out_shape=[jax.ShapeDtypeStruct((S, LANES), F32), jax.ShapeDtypeStruct((1, LANES), F32)],
        scratch_shapes=[pltpu.VMEM((8, LANES), F32)], compiler_params=_params(1))(dcum_k, dcum_q, small, bvec)


N_AUG = 3


def _lane():
    return lax.broadcasted_iota(jnp.int32, (1, LANES), 1)


def _lane_mask():
    return _lane() < ATT_HEAD_DIM


def _aug_base(h):
    return ATT_HEAD_DIM * (1 - h)


def _attn_prep(qkv, cum_cols, T):
    S = qkv.shape[0]
    HP = ATT_HEADS // 2

    def body(q_ref, k_ref, c_ref, qa_ref, ka_ref):
        lane = _lane()
        q = q_ref[...]
        k = k_ref[...]
        one, zero = jnp.ones_like(q), jnp.zeros_like(q)
        for h in (0, 1):
            base = _aug_base(h)
            own = (lane < ATT_HEAD_DIM) if h == 0 else (lane >= ATT_HEAD_DIM)
            term_lanes = (lane >= base) & (lane < base + N_AUG)
            terms = [t.astype(F32) for t in _split3(c_ref[0, :, h:h + 1])]
            neg = jnp.where(lane == base, -terms[0], jnp.where(lane == base + 1, -terms[1], -terms[2])).astype(BF16)
            qa_ref[:, h * LANES:(h + 1) * LANES] = jnp.where(lane == base + N_AUG, zero, jnp.where(term_lanes, one, q))
            ka_ref[:, h * LANES:(h + 1) * LANES] = jnp.where(term_lanes, neg, jnp.where(lane == base + N_AUG, one,
                                                                                         jnp.where(own, k, zero)))

    return pl.pallas_call(
        body, name="attn_prep", grid=(S // T, HP),
        in_specs=[pl.BlockSpec((T, LANES), lambda i, hp: (i, hp)), pl.BlockSpec((T, LANES), lambda i, hp: (i, HP + hp)),
                  pl.BlockSpec((1, T, 2), lambda i, hp: (hp, i, 0))],
        out_specs=[pl.BlockSpec((T, 2 * LANES), lambda i, hp: (i, hp))] * 2,
        out_shape=[jax.ShapeDtypeStruct((S, 2 * D_MODEL), BF16)] * 2, compiler_params=_params(2))(qkv, qkv, cum_cols)


def _attn_fwd(qa, ka, qkv, T, TK):
    S = qkv.shape[0]
    nq = S // T
    r = T // TK
    HP = ATT_HEADS // 2

    def body(q0_ref, q1_ref, k0_ref, k1_ref, v_ref, o_ref, o32_ref, lse_ref):
        i = pl.program_id(1)
        qs = (q0_ref[...], q1_ref[...])
        k_refs = (k0_ref, k1_ref)
        row = lax.broadcasted_iota(jnp.int32, (TK, T), 0)
        col = lax.broadcasted_iota(jnp.int32, (TK, T), 1)
        head_rows = lax.broadcasted_iota(jnp.int32, (LANES, 1), 0) < ATT_HEAD_DIM

        def block(j, carry, q0):
            off = pl.multiple_of(j * TK, TK)
            vj = v_ref[pl.ds(off, TK), :]
            full = q0 is None
            q0 = 0 if full else q0
            m0, l0, m1, l1, acc = carry
            new, alphas, pvs = [], [], []
            for h, (m, l) in enumerate(((m0, l0), (m1, l1))):
                st = _dot_nt(k_refs[h][pl.ds(off, TK), :], qs[h][q0:, :])
                if not full:
                    st = jnp.where(row[:, :T - q0] <= col[:, :T - q0], st, NEG)
                m_old, l_old = m[:, q0:], l[:, q0:]
                m_new = jnp.maximum(m_old, jnp.max(st, axis=0, keepdims=True))
                p = jnp.exp(st - m_new)
                alpha = jnp.exp(m_old - m_new)
                l_new = alpha * l_old + jnp.sum(p, axis=0, keepdims=True)
                pvs.append(_dot_tn(vj, p.astype(BF16)))
                alphas.append(alpha)
                new += [m_new, l_new]
            part = acc[:, q0:] * jnp.where(head_rows, alphas[0], alphas[1]) + jnp.where(head_rows, pvs[0], pvs[1])
            if q0:
                keep = lambda old, upd: jnp.concatenate([old[:, :q0], upd], axis=1)
                return (keep(m0, new[0]), keep(l0, new[1]), keep(m1, new[2]), keep(l1, new[3]), keep(acc, part))
            return (new[0], new[1], new[2], new[3], part)

        init = (jnp.full((1, T), NEG, F32), jnp.zeros((1, T), F32), jnp.full((1, T), NEG, F32), jnp.zeros((1, T), F32),
                jnp.zeros((LANES, T), F32))
        n_full = i * r
        carry = lax.fori_loop(0, n_full // 2, lambda jj, c: block(2 * jj + 1, block(2 * jj, c, None), None), init)
        carry = lax.cond(n_full % 2 == 1, lambda c: block(n_full - 1, c, None), lambda c: c, carry)
        for d in range(r):
            carry = block(n_full + d, carry, d * TK)
        m0, l0, m1, l1, acc = carry
        out = (acc / jnp.where(head_rows, l0, l1)).T
        o_ref[...] = out.astype(BF16)
        o32_ref[...] = out
        lse_ref[0, 0:1, :] = m0 + jnp.log(l0)
        lse_ref[0, 1:2, :] = m1 + jnp.log(l1)

    vmem = 2 * (2 * T * LANES * 2 + 3 * S * LANES * 2 + T * LANES * (2 + 4) + 8 * T * 4) + 10 * T * TK * 4 + (8 << 20)
    qspec = lambda h: pl.BlockSpec((T, LANES), lambda hp, i, h=h: (i, 2 * hp + h))
    kspec = lambda h: pl.BlockSpec((S, LANES), lambda hp, i, h=h: (0, 2 * hp + h))
    return pl.pallas_call(
        body, name="attn_fwd", grid=(HP, nq),
        in_specs=[qspec(0), qspec(1), kspec(0), kspec(1), pl.BlockSpec((S, LANES), lambda hp, i: (0, 2 * HP + hp))],
        out_specs=[pl.BlockSpec((T, LANES), lambda hp, i: (i, hp)), pl.BlockSpec((T, LANES), lambda hp, i: (i, hp)),
                   pl.BlockSpec((1, 2, T), lambda hp, i: (hp, 0, i))],
        out_shape=[jax.ShapeDtypeStruct((S, D_MODEL), BF16), jax.ShapeDtypeStruct((S, D_MODEL), F32),
                   jax.ShapeDtypeStruct((HP, 2, S), F32)],
        compiler_params=_params(2, vmem))(qa, qa, ka, ka, qkv)


def _attn_stats(do, o32, lse_cols, T):
    S = do.shape[0]
    HP = ATT_HEADS // 2

    def body(do_ref, o_ref, lse_ref, st_ref):
        lane = lax.broadcasted_iota(jnp.int32, (LANES, 8), 0)
        c = lax.broadcasted_iota(jnp.int32, (LANES, 8), 1)
        sel = jnp.where(((c == 2) & (lane < ATT_HEAD_DIM)) | ((c == 3) & (lane >= ATT_HEAD_DIM)), 1.0, 0.0).astype(BF16)
        dd = _dot_exact_right(do_ref[...].astype(F32) * o_ref[...], sel)
        c8 = lax.broadcasted_iota(jnp.int32, (1, 8), 1)
        st_ref[0] = jnp.where(c8 == 0, lse_ref[0, :, 0:1], jnp.where(c8 == 1, lse_ref[0, :, 1:2], dd))

    return pl.pallas_call(
        body, name="attn_stats", grid=(HP, S // T),
        in_specs=[pl.BlockSpec((T, LANES), lambda hp, i: (i, hp)), pl.BlockSpec((T, LANES), lambda hp, i: (i, hp)),
                  pl.BlockSpec((1, T, 2), lambda hp, i: (hp, i, 0))],
        out_specs=pl.BlockSpec((1, T, 8), lambda hp, i: (hp, i, 0)), out_shape=jax.ShapeDtypeStruct((HP, S, 8), F32),
        compiler_params=_params(2))(do, o32, lse_cols)


def _attn_bwd(qa, ka, qkv, do, stats, T):
    S = qkv.shape[0]
    nq = S // T
    HP = ATT_HEADS // 2

    def body(k0_ref, k1_ref, v_ref, q0_ref, q1_ref, do_ref, st_ref, dq_ref, dk_ref, dv_ref, dck_ref, dcq_ref, dq_acc):
        j = pl.program_id(1)
        mA = _lane_mask()
        masks = (mA, jnp.logical_not(mA))
        q_refs = (q0_ref, q1_ref)

        @pl.when(j == 0)
        def _():
            dq_acc[...] = jnp.zeros_like(dq_acc)

        kas = (k0_ref[...], k1_ref[...])
        vj = v_ref[...]
        row = lax.broadcasted_iota(jnp.int32, (T, T), 0)
        col = lax.broadcasted_iota(jnp.int32, (T, T), 1)

        def block(i, carry, diag):
            dvt, dkt0, dkt1 = carry
            off = pl.multiple_of(i * T, T)
            doi = do_ref[pl.ds(off, T), :]
            zero = jnp.zeros_like(doi)
            dkts = [dkt0, dkt1]
            for h in (0, 1):
                qh = q_refs[h][pl.ds(off, T), :]
                doh = jnp.where(masks[h], doi, zero)
                lse = st_ref[0, pl.ds(off, T), h:h + 1]
                dd = st_ref[0, pl.ds(off, T), 2 + h:3 + h]
                sc = _dot_nt(qh, kas[h])
                if diag:
                    sc = jnp.where(row >= col, sc, NEG)
                p = jnp.exp(sc - lse)
                dp = _dot_nt(doh, vj)
                ds = (p * (dp - dd)).astype(BF16)
                dvt = dvt + _dot_tn(doh, p.astype(BF16))
                dkts[h] = dkts[h] + _dot_tn(qh, ds)
                dq_acc[h, pl.ds(off, T), :] += _dot(ds, kas[h])
            return (dvt, dkts[0], dkts[1])

        z = jnp.zeros((LANES, T), F32)
        carry = block(j, (z, z, z), True)
        dvt, dkt0, dkt1 = lax.fori_loop(j + 1, nq, lambda i, c: block(i, c, False), carry)
        dv_ref[...] = dvt.T.astype(BF16)
        dk_ref[...] = jnp.where(mA, dkt0.T, dkt1.T).astype(BF16)
        ones_q = (_aug_base(0), _aug_base(1))
        dck_ref[0, 0:1, :] = -dkt0[ones_q[0]:ones_q[0] + 1, :]
        dck_ref[0, 1:2, :] = -dkt1[ones_q[1]:ones_q[1] + 1, :]

        @pl.when(j == nq - 1)
        def _():
            dq0, dq1 = dq_acc[0], dq_acc[1]
            ones_k = (_aug_base(0) + N_AUG, _aug_base(1) + N_AUG)
            dq_ref[...] = (jnp.where(mA, dq0, dq1) * (1.0 / math.sqrt(ATT_HEAD_DIM))).astype(BF16)
            dcq_ref[0, :, 0:1] = dq0[:, ones_k[0]:ones_k[0] + 1]
            dcq_ref[0, :, 1:2] = dq1[:, ones_k[1]:ones_k[1] + 1]

    vmem = (2 * (3 * T * LANES * 2 + 3 * S * LANES * 2 + S * LANES * 4 + S * LANES * (2 + 4) + 2 * T * LANES * 2 + 8 * T * 4)
            + 2 * S * LANES * 4 + 12 * T * T * 4 + (8 << 20))
    kspec = lambda h: pl.BlockSpec((T, LANES), lambda hp, j, h=h: (j, 2 * hp + h))
    qspec = lambda h: pl.BlockSpec((S, LANES), lambda hp, j, h=h: (0, 2 * hp + h))
    blk = pl.BlockSpec((T, LANES), lambda hp, j: (j, hp))
    full = pl.BlockSpec((S, LANES), lambda hp, j: (0, hp))
    return pl.pallas_call(
        body, name="attn_bwd", grid=(HP, nq),
        in_specs=[kspec(0), kspec(1), pl.BlockSpec((T, LANES), lambda hp, j: (j, 2 * HP + hp)), qspec(0), qspec(1), full,
                  pl.BlockSpec((1, S, 8), lambda hp, j: (hp, 0, 0))],
        out_specs=[full, blk, blk, pl.BlockSpec((1, 2, T), lambda hp, j: (hp, 0, j)),
                   pl.BlockSpec((1, S, 2), lambda hp, j: (hp, 0, 0))],
        out_shape=[jax.ShapeDtypeStruct((S, D_MODEL), BF16)] * 3 + [jax.ShapeDtypeStruct((HP, 2, S), F32),
                                                                     jax.ShapeDtypeStruct((HP, S, 2), F32)],
        scratch_shapes=[pltpu.VMEM((2, S, LANES), F32)], compiler_params=_params(2, vmem))(ka, ka, qkv, qa, qa, do, stats)


HALO = 8


def _shift_down(x, d, above):
    r = pltpu.roll(x, d, 0)
    head = jnp.where(lax.broadcasted_iota(jnp.int32, (HALO, 1), 0) < d, pltpu.roll(above, d, 0), r[0:HALO])
    return head if x.shape[0] == HALO else jnp.concatenate([head, r[HALO:]], axis=0)


def _shift_up(x, d, below):
    n = x.shape[0]
    r = pltpu.roll(x, n - d, 0)
    tail = jnp.where(lax.broadcasted_iota(jnp.int32, (HALO, 1), 0) >= HALO - d, pltpu.roll(below, HALO - d, 0), r[n - HALO:])
    return jnp.concatenate([r[:n - HALO], tail], axis=0)


def _conv_fwd(u, w, b, ts, tc):
    S, C = u.shape
    hb = ts // HALO

    def body(u_ref, prev_ref, w_ref, b_ref, o_ref):
        i = pl.program_id(0)
        x = u_ref[...]
        above = jnp.where(i == 0, 0.0, prev_ref[...])
        acc = b_ref[...] + w_ref[3:4, :] * x
        for k in range(SSM_CONV - 1):
            acc = acc + w_ref[k:k + 1, :] * _shift_down(x, SSM_CONV - 1 - k, above)
        o_ref[...] = acc * _sigmoid(acc)

    return pl.pallas_call(
        body, name="conv_fwd", grid=(S // ts, C // tc),
        in_specs=[pl.BlockSpec((ts, tc), lambda i, j: (i, j)),
                  pl.BlockSpec((HALO, tc), lambda i, j: (jnp.maximum(i * hb - 1, 0), j)),
                  pl.BlockSpec((SSM_CONV, tc), lambda i, j: (0, j)), pl.BlockSpec((1, tc), lambda i, j: (0, j))],
        out_specs=pl.BlockSpec((ts, tc), lambda i, j: (i, j)), out_shape=jax.ShapeDtypeStruct((S, C), F32),
        compiler_params=_params(2))(u, u, w, b)


def _conv_bwd(name, u, dy, w, b, ts, tc, col0):
    S, C = dy.shape
    cb = col0 // tc
    assert cb * tc == col0
    hb = ts // HALO
    nb = S // ts

    def body(u_ref, uprev_ref, unext_ref, dy_ref, dynext_ref, w_ref, b_ref, du_ref, dw_ref, db_ref):
        i = pl.program_id(1)
        x = u_ref[...]
        above = jnp.where(i == 0, 0.0, uprev_ref[...])
        ws = [w_ref[k:k + 1, :] for k in range(SSM_CONV)]

        def dsilu(pre):
            sg = _sigmoid(pre)
            return sg * (1.0 + pre * (1.0 - sg))

        shifted = [_shift_down(x, SSM_CONV - 1 - k, above) for k in range(SSM_CONV - 1)] + [x]
        pre = b_ref[...]
        for k in range(SSM_CONV):
            pre = pre + ws[k] * shifted[k]
        g = dy_ref[...] * dsilu(pre)
        nxt = unext_ref[...]
        tail = x[ts - HALO:, :]
        pre_n = b_ref[...] + ws[SSM_CONV - 1] * nxt
        for k in range(SSM_CONV - 1):
            pre_n = pre_n + ws[k] * _shift_down(nxt, SSM_CONV - 1 - k, tail)
        g_next = jnp.where(i == nb - 1, 0.0, dynext_ref[...] * dsilu(pre_n))
        du = ws[SSM_CONV - 1] * g
        for k in range(SSM_CONV - 1):
            du = du + ws[k] * _shift_up(g, SSM_CONV - 1 - k, g_next)
        du_ref[...] = du.astype(du_ref.dtype)
        dws = [jnp.sum(g * shifted[k], axis=0, keepdims=True) for k in range(SSM_CONV)]
        dbs = jnp.sum(g, axis=0, keepdims=True)

        @pl.when(i == 0)
        def _():
            for k in range(SSM_CONV):
                dw_ref[k:k + 1, :] = dws[k]
            db_ref[...] = dbs

        @pl.when(i > 0)
        def _():
            for k in range(SSM_CONV):
                dw_ref[k:k + 1, :] += dws[k]
            db_ref[...] += dbs

    nxt = lambda off: (lambda j, i: (jnp.minimum((i + 1) * hb, S // HALO - 1), j + off))
    return pl.pallas_call(
        body, name=name, grid=(C // tc, nb),
        in_specs=[pl.BlockSpec((ts, tc), lambda j, i: (i, j + cb)),
                  pl.BlockSpec((HALO, tc), lambda j, i: (jnp.maximum(i * hb - 1, 0), j + cb)),
                  pl.BlockSpec((HALO, tc), nxt(cb)),
                  pl.BlockSpec((ts, tc), lambda j, i: (i, j)),
                  pl.BlockSpec((HALO, tc), nxt(0)),
                  pl.BlockSpec((SSM_CONV, tc), lambda j, i: (0, j + cb)), pl.BlockSpec((1, tc), lambda j, i: (0, j + cb))],
        out_specs=[pl.BlockSpec((ts, tc), lambda j, i: (i, j)), pl.BlockSpec((SSM_CONV, tc), lambda j, i: (0, j)),
                   pl.BlockSpec((1, tc), lambda j, i: (0, j))],
        out_shape=[jax.ShapeDtypeStruct((S, C), BF16), jax.ShapeDtypeStruct((SSM_CONV, C), F32), jax.ShapeDtypeStruct((1, C), F32)],
        compiler_params=_params(2))(u, u, u, dy, dy, w, b)


def _head_sum():
    lane = jnp.right_shift(lax.broadcasted_iota(jnp.int32, (GROUP_LANES, 8), 0), 6)
    r = lax.broadcasted_iota(jnp.int32, (GROUP_LANES, 8), 1)
    return jnp.where(lane == r, 1.0, 0.0).astype(BF16)


def _head_expand():
    r = lax.broadcasted_iota(jnp.int32, (8, GROUP_LANES), 0)
    c = jnp.right_shift(lax.broadcasted_iota(jnp.int32, (8, GROUP_LANES), 1), 6)
    return jnp.where(r == c, 1.0, 0.0).astype(BF16)


def _ssd_common(dtc_ref, dtr_ref, bias_r, alog_b, bias_c, alog_c, L):
    a_b = -jnp.exp(alog_b)
    dt = _dot_exact_right(_softplus(dtc_ref[0] + bias_r), _head_expand())
    acum = _dot_exact_left(_tri(L, False), dt * a_b)
    a_c = -jnp.exp(alog_c)
    dtr = _softplus(dtr_ref[0] + bias_c)
    acum_r = _dot_exact_right(dtr * a_c, _tri(L, True))
    return a_b, dt, acum, acum_r


def _ssd_specs(L, nc, rev):
    cc = (lambda c: nc - 1 - c) if rev else (lambda c: c)
    G = SSM_GROUPS
    blk = pl.BlockSpec((L, GROUP_LANES), lambda g, c: (cc(c), g))
    dtc = pl.BlockSpec((1, L, 8), lambda g, c: (g, cc(c), 0))
    rowv = pl.BlockSpec((1, 1, 8), lambda g, c: (g, 0, 0))
    xs = blk
    bm = pl.BlockSpec((L, SSM_STATE), lambda g, c: (cc(c), SSM_INNER // SSM_STATE + g))
    cm = pl.BlockSpec((L, SSM_STATE), lambda g, c: (cc(c), SSM_INNER // SSM_STATE + G + g))
    dtr = pl.BlockSpec((1, 8, L), lambda g, c: (g, 0, cc(c)))
    vec = pl.BlockSpec((1, GROUP_LANES), lambda g, c: (0, g))
    colv = pl.BlockSpec((1, 8, 1), lambda g, c: (g, 0, 0))
    hs = pl.BlockSpec((1, 1, SSM_STATE, GROUP_LANES), lambda g, c: (g, cc(c), 0, 0))
    return blk, xs, bm, cm, dtc, dtr, vec, rowv, colv, hs


def _ssd_fwd(xbc, z, dtc, dtr, bias_r, alog_b, dskip_b, normw, bias_c, alog_c, L):
    S = z.shape[0]
    nc = S // L
    blk, xs, bm, cm, dtcs, dtrs, vec, rowv, colv, hs = _ssd_specs(L, nc, False)

    def body(x_ref, b_ref, c_ref, z_ref, dtc_ref, dtr_ref, bias_ref, alog_ref, dskip_ref, nw_ref, biasc_ref, alogc_ref,
             y_ref, ssm_ref, hs_ref, h_scr):
        c = pl.program_id(1)

        @pl.when(c == 0)
        def _():
            h_scr[...] = jnp.zeros_like(h_scr)

        mA = _lane_mask()
        a_b, dt, acum, acum_r = _ssd_common(dtc_ref, dtr_ref, bias_ref[0], alog_ref[...], biasc_ref[0], alogc_ref[0], L)
        x = x_ref[...]
        cb, bb = c_ref[...].astype(BF16), b_ref[...].astype(BF16)
        hprev = h_scr[...]
        hs_ref[0, 0] = hprev
        xdt = x * dt
        xdt_b = xdt.astype(BF16)
        gmat = _dot_nt(cb, bb)
        row = lax.broadcasted_iota(jnp.int32, (L, L), 0)
        col = lax.broadcasted_iota(jnp.int32, (L, L), 1)
        parts = []
        for p in range(GROUP_LANES // LANES):
            xp = xdt_b[:, p * LANES:(p + 1) * LANES]
            yd = []
            for hh in (0, 1):
                r = 2 * p + hh
                acol = acum[:, r * ATT_HEAD_DIM:r * ATT_HEAD_DIM + 1]
                arow = acum_r[r:r + 1, :]
                lm = jnp.exp(jnp.where(row >= col, acol - arow, NEG))
                yd.append(_dot((gmat * lm).astype(BF16), xp))
            parts.append(jnp.where(mA, yd[0], yd[1]))
        ydiag = jnp.concatenate(parts, axis=1)
        yoff = jnp.exp(acum) * _dot(cb, hprev.astype(BF16))
        y = ydiag + yoff + dskip_ref[...] * x
        aend = acum[L - 1:L, :]
        wgt = (jnp.exp(aend - acum) * xdt).astype(BF16)
        h_scr[...] = jnp.exp(aend) * hprev + _dot_tn(bb, wgt)
        y_ref[...] = y
        zz = z_ref[...]
        u = y * (zz * _sigmoid(zz))
        rs = lax.rsqrt(jnp.mean(u * u, axis=1, keepdims=True) + RMS_EPS)
        ssm_ref[...] = (u * rs * nw_ref[...]).astype(BF16)

    return pl.pallas_call(
        body, name="ssd_fwd", grid=(SSM_GROUPS, nc),
        in_specs=[xs, bm, cm, blk, dtcs, dtrs, rowv, vec, vec, vec, colv, colv],
        out_specs=[blk, blk, hs],
        out_shape=[jax.ShapeDtypeStruct((S, SSM_INNER), F32), jax.ShapeDtypeStruct((S, SSM_INNER), BF16),
                   jax.ShapeDtypeStruct((SSM_GROUPS, nc, SSM_STATE, GROUP_LANES), F32)],
        scratch_shapes=[pltpu.VMEM((SSM_STATE, GROUP_LANES), F32)],
        compiler_params=_params(2, 48 << 20))(xbc, xbc, xbc, z, dtc, dtr, bias_r, alog_b, dskip_b, normw, bias_c, alog_c)


def _ssd_bwd(xbc, z, y, dssm, hs_all, dtc, dtr, bias_r, alog_r, alog_b, dskip_b, normw, bias_c, alog_c, L):
    S = z.shape[0]
    nc = S // L
    blk, xs, bm, cm, dtcs, dtrs, vec, rowv, colv, hs = _ssd_specs(L, nc, True)

    def body(x_ref, b_ref, c_ref, z_ref, y_ref, dssm_ref, hs_ref, dtc_ref, dtr_ref, bias_ref, alogr_ref, alog_ref, dskip_ref, nw_ref,
             biasc_ref, alogc_ref,
             dx_ref, db_ref, dc_ref, dz_ref, ddt_ref, dnw_ref, ddskip_ref, dalog_ref, dbias_ref, dh_scr):
        c = pl.program_id(1)

        @pl.when(c == 0)
        def _():
            dh_scr[...] = jnp.zeros_like(dh_scr)

        mA = _lane_mask()
        masks = (mA, jnp.logical_not(mA))
        a_b, dt, acum, acum_r = _ssd_common(dtc_ref, dtr_ref, bias_ref[0], alog_ref[...], biasc_ref[0], alogc_ref[0], L)
        x, zz, y, dssm = x_ref[...], z_ref[...], y_ref[...], dssm_ref[...]
        cb, bb = c_ref[...].astype(BF16), b_ref[...].astype(BF16)
        hprev = hs_ref[0, 0]
        hb = hprev.astype(BF16)
        ds = dh_scr[...]
        dsb = ds.astype(BF16)
        dskip = dskip_ref[...]
        aend = acum[L - 1:L, :]
        e_a, e_end = jnp.exp(acum), jnp.exp(aend)
        dte = jnp.exp(aend - acum)
        xdt = x * dt
        xdt_b = xdt.astype(BF16)
        sg = _sigmoid(zz)
        sz = zz * sg
        u = y * sz
        rs = lax.rsqrt(jnp.mean(u * u, axis=1, keepdims=True) + RMS_EPS)
        un = u * rs
        dun = dssm * nw_ref[...]
        du = rs * (dun - un * jnp.mean(dun * un, axis=1, keepdims=True))
        dy = du * sz
        dz_ref[...] = (du * y * sg * (1.0 + zz * (1.0 - sg))).astype(dz_ref.dtype)
        dy_b = dy.astype(BF16)
        dch_b = (dy * e_a).astype(BF16)
        dc = _dot_nt(dch_b, hb)
        dhprev = _dot_tn(cb, dch_b)
        gt = _dot_nt(bb, cb)
        row = lax.broadcasted_iota(jnp.int32, (L, L), 0)
        col = lax.broadcasted_iota(jnp.int32, (L, L), 1)
        dgt = jnp.zeros((L, L), F32)
        parts = []
        for p in range(GROUP_LANES // LANES):
            xp = xdt_b[:, p * LANES:(p + 1) * LANES]
            dyp = dy_b[:, p * LANES:(p + 1) * LANES]
            zero = jnp.zeros_like(dyp)
            acc = None
            for hh in (0, 1):
                r = 2 * p + hh
                acol = acum[:, r * ATT_HEAD_DIM:r * ATT_HEAD_DIM + 1]
                arow = acum_r[r:r + 1, :]
                lmt = jnp.exp(jnp.where(row <= col, arow - acol, NEG))
                dyh = jnp.where(masks[hh], dyp, zero)
                part = _dot((gt * lmt).astype(BF16), dyh)
                acc = part if acc is None else acc + part
                dgt = dgt + _dot_nt(xp, dyh) * lmt
            parts.append(acc)
        dxdt_diag = jnp.concatenate(parts, axis=1)
        dgt_b = dgt.astype(BF16)
        db = _dot(dgt_b, cb)
        dc = dc + _dot_tn(dgt_b, bb)
        dxdt_state = dte * _dot(bb, dsb)
        db = db + _dot_nt((dte * xdt).astype(BF16), dsb)
        dxdt = dxdt_diag + dxdt_state
        dy_r, xdt_r = dy_b.astype(F32), xdt_b.astype(F32)
        dac = dy_r * (y - dskip * x) - xdt_r * dxdt
        tail = jnp.sum(xdt_r * dxdt_state, axis=0, keepdims=True) + e_end * jnp.sum(ds * hprev, axis=0, keepdims=True)
        rowl = lax.broadcasted_iota(jnp.int32, (L, 1), 0)
        dac = dac + jnp.where(rowl == L - 1, tail, 0.0)
        rc = _dot_exact_left(_tri(L, True), dac)
        hsum = _head_sum()
        hs1 = _dot_exact_right(dxdt * x, hsum, 2)
        hs2 = _dot_exact_right(rc, hsum, 2)
        a8 = -jnp.exp(alogr_ref[0])
        dtraw8 = dtc_ref[0] + bias_ref[0]
        ddtraw = (hs1 + a8 * hs2) * _sigmoid(dtraw8)
        dx_ref[...] = dskip * dy + dxdt * dt
        db_ref[...] = db
        dc_ref[...] = dc
        ddt_ref[0] = ddtraw
        dh_scr[...] = e_end * ds + dhprev
        sums = (jnp.sum(dssm * un, axis=0, keepdims=True), jnp.sum(dy * x, axis=0, keepdims=True))
        refs = (dnw_ref, ddskip_ref)
        sums8 = (a8 * jnp.sum(hs2 * _softplus(dtraw8), axis=0, keepdims=True), jnp.sum(ddtraw, axis=0, keepdims=True))
        refs8 = (dalog_ref, dbias_ref)

        @pl.when(c == 0)
        def _():
            for r, v in zip(refs, sums):
                r[...] = v
            for r, v in zip(refs8, sums8):
                r[0] = v

        @pl.when(c > 0)
        def _():
            for r, v in zip(refs, sums):
                r[...] += v
            for r, v in zip(refs8, sums8):
                r[0] += v

    nbc = pl.BlockSpec((L, SSM_STATE), lambda g, c: (nc - 1 - c, g))
    return pl.pallas_call(
        body, name="ssd_bwd", grid=(SSM_GROUPS, nc),
        in_specs=[xs, bm, cm, blk, blk, blk, hs, dtcs, dtrs, rowv, rowv, vec, vec, vec, colv, colv],
        out_specs=[blk, nbc, nbc, blk, dtcs, vec, vec, rowv, rowv],
        out_shape=[jax.ShapeDtypeStruct((S, SSM_INNER), F32), jax.ShapeDtypeStruct((S, SSM_GROUPS * SSM_STATE), F32),
                   jax.ShapeDtypeStruct((S, SSM_GROUPS * SSM_STATE), F32), jax.ShapeDtypeStruct((S, SSM_INNER), BF16),
                   jax.ShapeDtypeStruct((SSM_GROUPS, S, 8), F32)] + [jax.ShapeDtypeStruct((1, SSM_INNER), F32)] * 2
                  + [jax.ShapeDtypeStruct((SSM_GROUPS, 1, 8), F32)] * 2,
        scratch_shapes=[pltpu.VMEM((SSM_STATE, GROUP_LANES), F32)],
        compiler_params=_params(2, 56 << 20))(xbc, xbc, xbc, z, y, dssm, hs_all, dtc, dtr, bias_r, alog_r, alog_b, dskip_b,
                                              normw, bias_c, alog_c)


def _place():
    return lax.axis_index("x"), lax.axis_index("y"), lax.axis_index("c")


def _other_chips(x, y):
    return [(1 - x, y), (x, 1 - y), (1 - x, 1 - y)]


def _half_rows(rows, which):
    hr = rows // 2
    if isinstance(which, int):
        return pl.ds(which * hr, hr)
    return pl.ds(pl.multiple_of(which * hr, 8), hr)


def _chip_gather(name, shards, split):
    n = len(shards)
    ANY = pl.BlockSpec(memory_space=pl.ANY)

    def body(*refs):
        ins, outs = refs[:n], refs[n:2 * n]
        send, recv, fsend, frecv = refs[2 * n:]
        x, y, c = _place()
        me = 2 * x + y
        sibling = (x, y, 1 - c)
        chips = _other_chips(x, y)

        def piece(a, chip_idx, which):
            if split[a]:
                return outs[a].at[chip_idx, _half_rows(shards[a].shape[0], which)]
            return outs[a].at[chip_idx]

        def ici(k, a, to_chip, src_chip):
            src = ins[a].at[_half_rows(shards[a].shape[0], c)] if split[a] else ins[a]
            return pltpu.make_async_remote_copy(src_ref=src, dst_ref=piece(a, src_chip, c), send_sem=send.at[k, a],
                                                recv_sem=recv.at[k, a], device_id=(*to_chip, c), device_id_type=MESH)

        def fwd(k, a, src_chip, which):
            return pltpu.make_async_remote_copy(src_ref=piece(a, src_chip, which), dst_ref=piece(a, src_chip, which),
                                                send_sem=fsend.at[k, a], recv_sem=frecv.at[k, a], device_id=sibling,
                                                device_id_type=MESH)

        sends = []
        for k, chip in enumerate(chips):
            for a in range(n):
                cp = ici(k, a, chip, me)
                cp.start()
                sends.append(cp)
        for k, (ox, oy) in enumerate(chips):
            src = 2 * ox + oy
            for a in range(n):
                ici(k, a, (ox, oy), src).wait_recv()
                if split[a]:
                    cp = fwd(k, a, src, c)
                    cp.start()
                    sends.append(cp)
        for k, (ox, oy) in enumerate(chips):
            for a in range(n):
                if split[a]:
                    fwd(k, a, 2 * ox + oy, 1 - c).wait_recv()
        for cp in sends:
            cp.wait_send()

    sem = pltpu.SemaphoreType.DMA((3, n))
    return pl.pallas_call(
        body, name=name, in_specs=[ANY] * n, out_specs=[ANY] * n,
        out_shape=[jax.ShapeDtypeStruct((4,) + s.shape, s.dtype) for s in shards],
        scratch_shapes=[sem, sem, sem, sem])(*shards)


def _chip_copies_start(name, srcs, per_chip_src, after):
    n = len(srcs)
    HBM = pl.BlockSpec(memory_space=pltpu.HBM)
    SEM = pl.BlockSpec(memory_space=pltpu.SEMAPHORE)
    lands = [pltpu.with_memory_space_constraint(lax.empty(a.shape if per_chip_src else (4,) + a.shape, a.dtype), pltpu.HBM)
             for a in srcs]

    def body(*refs):
        ins, land = refs[:n], refs[n:2 * n]
        send, recv = refs[2 * n + 1], refs[2 * n + 2]
        token = refs[-1]
        x, y, c = _place()
        me = 2 * x + y
        for k, (ox, oy) in enumerate(_other_chips(x, y)):
            for a in range(n):
                src = ins[a].at[2 * ox + oy] if per_chip_src else ins[a]
                pltpu.make_async_remote_copy(src_ref=src, dst_ref=land[a].at[me], send_sem=send.at[k * n + a], recv_sem=recv.at[k * n + a],
                                             device_id=(ox, oy, c), device_id_type=MESH).start()
        token[...] = jnp.zeros_like(token)

    sem = pltpu.SemaphoreType.DMA((3 * n,))
    res = pl.pallas_call(
        body, name=name,
        out_shape=[sem, sem] + [pltpu.HBM(a.shape, a.dtype) for a in srcs] + [pltpu.HBM(b.shape, b.dtype) for b in lands]
                  + [jax.ShapeDtypeStruct((8, LANES), F32)],
        in_specs=[HBM] * (2 * n) + [pl.BlockSpec(memory_space=pl.ANY)],
        out_specs=[SEM, SEM] + [HBM] * (2 * n) + [pl.BlockSpec(memory_space=pltpu.VMEM)],
        input_output_aliases={k: 2 + k for k in range(2 * n)},
        compiler_params=pltpu.CompilerParams(has_side_effects=pltpu.SideEffectType.DATAFLOW_SIDE_EFFECTING),
    )(*[pltpu.with_memory_space_constraint(a, pltpu.HBM) for a in srcs], *lands, after)
    return res[:-1], res[-1]


def _chip_copies_wait(name, started, per_chip_src, after):
    send, recv = started[0], started[1]
    n = (len(started) - 2) // 2
    srcs, lands = started[2:2 + n], started[2 + n:]
    HBM = pl.BlockSpec(memory_space=pltpu.HBM)
    SEM = pl.BlockSpec(memory_space=pltpu.SEMAPHORE)

    def body(*refs):
        ins, land = refs[:n], refs[n:2 * n]
        send_sem, recv_sem = refs[2 * n], refs[2 * n + 1]
        x, y, c = _place()
        me = 2 * x + y
        for k, (ox, oy) in enumerate(_other_chips(x, y)):
            for a in range(n):
                src = ins[a].at[me] if per_chip_src else ins[a]
                cp = pltpu.make_async_remote_copy(src_ref=src, dst_ref=land[a].at[2 * ox + oy], send_sem=send_sem.at[k * n + a],
                                                  recv_sem=recv_sem.at[k * n + a], device_id=(ox, oy, c), device_id_type=MESH)
                cp.wait_send()
                cp.wait_recv()

    res = pl.pallas_call(
        body, name=name,
        out_shape=[pltpu.HBM(a.shape, a.dtype) for a in srcs] + [pltpu.HBM(b.shape, b.dtype) for b in lands],
        in_specs=[HBM] * (2 * n) + [SEM, SEM, pl.BlockSpec(memory_space=pl.ANY)], out_specs=[HBM] * (2 * n),
        input_output_aliases={k: k for k in range(2 * n)},
        compiler_params=pltpu.CompilerParams(has_side_effects=pltpu.SideEffectType.DATAFLOW_SIDE_EFFECTING),
    )(*srcs, *lands, send, recv, after)
    return res[n:]


def _half_to_sibling(name, blocks):
    n = len(blocks)
    ANY = pl.BlockSpec(memory_space=pl.ANY)

    def body(*refs):
        ins, outs = refs[:n], refs[n:2 * n]
        send, recv = refs[2 * n:]
        x, y, c = _place()
        cps = [pltpu.make_async_remote_copy(src_ref=ins[a].at[:, _half_rows(blocks[a].shape[1], 1 - c)], dst_ref=outs[a],
                                            send_sem=send.at[a], recv_sem=recv.at[a], device_id=(x, y, 1 - c),
                                            device_id_type=MESH) for a in range(n)]
        for cp in cps:
            cp.start()
        for cp in cps:
            cp.wait_recv()
        for cp in cps:
            cp.wait_send()

    return pl.pallas_call(
        body, name=name, in_specs=[ANY] * n, out_specs=[ANY] * n,
        out_shape=[jax.ShapeDtypeStruct((4, b.shape[1] // 2, b.shape[2]), b.dtype) for b in blocks],
        scratch_shapes=[pltpu.SemaphoreType.DMA((n,)), pltpu.SemaphoreType.DMA((n,))])(*blocks)


def _sibling_swap(name, arrs):
    n = len(arrs)
    ANY = pl.BlockSpec(memory_space=pl.ANY)

    def body(*refs):
        ins, outs = refs[:n], refs[n:2 * n]
        send, recv = refs[2 * n:]
        x, y, c = _place()
        cps = [pltpu.make_async_remote_copy(src_ref=ins[a], dst_ref=outs[a], send_sem=send.at[a], recv_sem=recv.at[a],
                                            device_id=(x, y, 1 - c), device_id_type=MESH) for a in range(n)]
        for cp in cps:
            cp.start()
        for cp in cps:
            cp.wait_recv()
        for cp in cps:
            cp.wait_send()

    return pl.pallas_call(
        body, name=name, in_specs=[ANY] * n, out_specs=[ANY] * n,
        out_shape=[jax.ShapeDtypeStruct(a.shape, a.dtype) for a in arrs],
        scratch_shapes=[pltpu.SemaphoreType.DMA((n,)), pltpu.SemaphoreType.DMA((n,))])(*arrs)


N_DEV = 8


def _all_sum_small(vec):
    P = vec.shape[1]

    def body(v_ref, o_ref, buf, send, recv):
        x, y, c = _place()
        me = 4 * x + 2 * y + c
        buf[me] = v_ref[...]

        def peer(r):
            return ((1 - x) if (r >> 2) & 1 else x, (1 - y) if (r >> 1) & 1 else y, (1 - c) if r & 1 else c)

        sends = []
        for r in range(1, N_DEV):
            cp = pltpu.make_async_remote_copy(src_ref=v_ref, dst_ref=buf.at[me], send_sem=send.at[r], recv_sem=recv.at[r],
                                              device_id=peer(r), device_id_type=MESH)
            cp.start()
            sends.append(cp)
        for r in range(1, N_DEV):
            px, py, pc = peer(r)
            pltpu.make_async_remote_copy(src_ref=v_ref, dst_ref=buf.at[4 * px + 2 * py + pc], send_sem=send.at[r],
                                         recv_sem=recv.at[r], device_id=(px, py, pc), device_id_type=MESH).wait_recv()
        for cp in sends:
            cp.wait_send()
        tot = buf[0]
        for d in range(1, N_DEV):
            tot = tot + buf[d]
        o_ref[...] = tot

    return pl.pallas_call(
        body, name="all_sum_small", in_specs=[pl.BlockSpec(memory_space=pltpu.VMEM)],
        out_specs=pl.BlockSpec(memory_space=pltpu.VMEM), out_shape=jax.ShapeDtypeStruct((1, P), F32),
        scratch_shapes=[pltpu.VMEM((N_DEV, 1, P), F32), pltpu.SemaphoreType.DMA((N_DEV,)), pltpu.SemaphoreType.DMA((N_DEV,))],
    )(vec)


def _half_sum(name, blocks, theirs, core, tr):
    _, R, C = blocks.shape
    hr = R // 2
    nb = hr // tr
    assert nb * tr == hr

    def body(c_ref, a_ref, b_ref, o_ref):
        o_ref[...] = (a_ref[...] + b_ref[...]).astype(BF16)

    grid_spec = pltpu.PrefetchScalarGridSpec(
        num_scalar_prefetch=1, grid=(4, nb),
        in_specs=[pl.BlockSpec((1, tr, C), lambda b, i, c_ref: (b, c_ref[0] * nb + i, 0)),
                  pl.BlockSpec((1, tr, C), lambda b, i, c_ref: (b, i, 0))],
        out_specs=pl.BlockSpec((1, tr, C), lambda b, i, c_ref: (b, i, 0)))
    return pl.pallas_call(body, name=name, grid_spec=grid_spec, out_shape=jax.ShapeDtypeStruct((4, hr, C), BF16),
                          compiler_params=_params(2, 40 << 20))(core, blocks, theirs)


def _sum4(name, stack, mine, chip, tr):
    _, R, C = stack.shape

    def body(chip_ref, s_ref, m_ref, o_ref):
        t = [jnp.where(chip_ref[0] == j, m_ref[j], s_ref[j]).astype(F32) for j in range(4)]
        o_ref[...] = ((t[0] + t[1]) + t[2]) + t[3]

    blk = pl.BlockSpec((4, tr, C), lambda i, chip_ref: (0, i, 0))
    grid_spec = pltpu.PrefetchScalarGridSpec(num_scalar_prefetch=1, grid=(R // tr,), in_specs=[blk, blk],
                                             out_specs=pl.BlockSpec((tr, C), lambda i, chip_ref: (i, 0)))
    return pl.pallas_call(body, name=name, grid_spec=grid_spec, out_shape=jax.ShapeDtypeStruct((R, C), F32),
                          compiler_params=_params(1, 40 << 20))(chip, stack, mine)


def _adamw_math(w, m, v, g):
    c1 = 1.0 - ADAM_B1 ** ADAM_STEP
    c2 = 1.0 - ADAM_B2 ** ADAM_STEP
    nm = ADAM_B1 * m + (1.0 - ADAM_B1) * g
    nv = ADAM_B2 * v + (1.0 - ADAM_B2) * (g * g)
    return -ADAM_LR * ((nm / c1) / (jnp.sqrt(nv / c2) + ADAM_EPS) + ADAM_WD * w), nm, nv


def _adamw(name, w, m, v, g, tr):
    R, C = w.shape

    def body(w_ref, m_ref, v_ref, ga_ref, g_ref, d_ref, nm_ref, nv_ref):
        g = ga_ref[...]
        g_ref[...] = g
        d_ref[...], nm_ref[...], nv_ref[...] = _adamw_math(w_ref[...], m_ref[...], v_ref[...], g)

    spec = pl.BlockSpec((tr, C), lambda i: (i, 0))
    return pl.pallas_call(body, name=name, grid=(R // tr,), in_specs=[spec] * 4, out_specs=[spec] * 4,
                          out_shape=[jax.ShapeDtypeStruct((R, C), F32)] * 4, compiler_params=_params(1, 40 << 20))(w, m, v, g)


def _adamw_halves(name, w, m, v, mine, theirs, core, tr):
    _, R, C = w.shape
    nb = (R // 2) // tr
    assert 2 * nb * tr == R

    def body(c_ref, w_ref, m_ref, v_ref, a_ref, b_ref, g_ref, d_ref, nm_ref, nv_ref):
        g = jnp.where((pl.program_id(0) // nb) == c_ref[0], a_ref[...], b_ref[...])
        g_ref[0] = g
        d_ref[0], nm_ref[0], nv_ref[0] = _adamw_math(w_ref[0], m_ref[0], v_ref[0], g)

    spec = pl.BlockSpec((1, tr, C), lambda i, c_ref: (0, i, 0))
    half = lambda own: pl.BlockSpec((tr, C), lambda i, c_ref, own=own: (
        jnp.clip(i - (c_ref[0] if own else 1 - c_ref[0]) * nb, 0, nb - 1), 0))
    grid_spec = pltpu.PrefetchScalarGridSpec(num_scalar_prefetch=1, grid=(R // tr,),
                                             in_specs=[spec, spec, spec, half(True), half(False)], out_specs=[spec] * 4)
    return pl.pallas_call(body, name=name, grid_spec=grid_spec, out_shape=[jax.ShapeDtypeStruct((1, R, C), F32)] * 4,
                          compiler_params=_params(1, 40 << 20))(core, w, m, v, mine, theirs)


def _row_tile(rows, cols, budget_bytes=1 << 20, mult=8):
    best = None
    for t in range(mult, rows + 1, mult):
        if rows % t == 0 and t * cols * 4 <= budget_bytes:
            best = t
    return best if best is not None else rows


def _ln_fwd(r, g, b):
    mu = jnp.mean(r, axis=1, keepdims=True)
    xc = r - mu
    rstd = lax.rsqrt(jnp.mean(xc * xc, axis=1, keepdims=True) + LN_EPS)
    xhat = xc * rstd
    return xhat * g + b, xhat, rstd


def _ln_bwd(dy, xhat, rstd, g):
    dxh = dy * g
    return rstd * (dxh - jnp.mean(dxh, axis=1, keepdims=True) - xhat * jnp.mean(dxh * xhat, axis=1, keepdims=True))


def _to_chip_blocks_cols(a):
    R, C4 = a.shape
    return a.reshape(R, 4, C4 // 4).transpose(1, 0, 2)


def _from_chip_blocks_cols(a):
    return a.transpose(1, 0, 2).reshape(a.shape[1], 4 * a.shape[2])


def kernel(x, w_in, b_forget, conv_w, conv_b, dt_bias, a_log, d_skip, ssm_norm_w, w_proj_attn, w_proj_ssm, b_gates, w_out, ln1_g, ln1_b, w_ffn_gate, w_ffn_up, w_ffn_down, ln2_g, ln2_b, loss_target, m_w_in, m_b_forget, m_conv_w, m_conv_b, m_dt_bias, m_a_log, m_d_skip, m_ssm_norm_w, m_w_proj_attn, m_w_proj_ssm, m_b_gates, m_w_out, m_ln1_g, m_ln1_b, m_w_ffn_gate, m_w_ffn_up, m_w_ffn_down, m_ln2_g, m_ln2_b, v_w_in, v_b_forget, v_conv_w, v_conv_b, v_dt_bias, v_a_log, v_d_skip, v_ssm_norm_w, v_w_proj_attn, v_w_proj_ssm, v_b_gates, v_w_out, v_ln1_g, v_ln1_b, v_w_ffn_gate, v_w_ffn_up, v_w_ffn_down, v_ln2_g, v_ln2_b):
    S = x.shape[1]
    D = D_MODEL
    TM, TM2, TM3, TA, AQF, LC, CV, TS, TB = (min(TILES[k], S) for k in ("TM", "TM2", "TM3", "TA", "AQF", "LC", "CV", "TS", "TB"))
    xf = x[0]
    tgt = loss_target[0]
    xb = xf.astype(BF16)

    shards = [w_in[0].astype(BF16), conv_w[0], w_proj_attn[0].astype(BF16), w_proj_ssm[0].astype(BF16), w_out[0].astype(BF16),
              w_ffn_gate[0].astype(BF16), w_ffn_up[0].astype(BF16), w_ffn_down[0].astype(BF16)]
    chip = 2 * lax.axis_index("x") + lax.axis_index("y")
    own = lambda gathered, mine: [lax.dynamic_update_slice(g, sh[None], (chip, 0, 0)) for g, sh in zip(gathered, mine)]
    g_in, g_cw = own(_chip_gather("gather_w_in", shards[:2], [True, False]), shards[:2])
    later, gather_token = _chip_copies_start("gather_rest_start", shards[2:], False, g_cw)
    w_full = _from_chip_blocks_cols(g_in)
    w_re = jnp.concatenate([w_full[:, 0:3072], w_full[:, 3088:5136], w_full[:, 5136:8208], w_full[:, 8240:10288],
                            w_full[:, 3072:3088], w_full[:, 8208:8240], jnp.zeros((D, 80), BF16)], axis=1)
    conv_w_full = _from_chip_blocks_cols(g_cw)

    def plain(accs, rows, vecs, j):
        return [accs[0]], []

    def q_scaled(accs, rows, vecs, j):
        return [accs[0] * jnp.where(j * 512 < D, 1.0 / math.sqrt(ATT_HEAD_DIM), 1.0)], []

    qkv, = _mm("proj_qkv", S, 3072, TM, 512, [(xb, D, 0)], [(w_re, 0)], [(0, 0)], q_scaled, [(3072, BF16, 0)],
               after=[gather_token])
    z, = _mm("proj_z", S, 2048, TM, 512, [(xb, D, 0)], [(w_re, RE_Z // 512)], [(0, 0)], plain, [(2048, F32, 0)])
    xbc_raw, = _mm("proj_xbc", S, 3072, TM, 512, [(xb, D, 0)], [(w_re, RE_XBC // 512)], [(0, 0)], plain, [(3072, F32, 0)])
    gl, = _mm("proj_gate", S, 2048, TM, 512, [(xb, D, 0)], [(w_re, RE_GATE // 512)], [(0, 0)], plain, [(2048, BF16, 0)])
    small, = _mm("proj_small", S, 128, TM, 128, [(xb, D, 0)], [(w_re, RE_SMALL // 128)], [(0, 0)], plain, [(128, F32, 0)])

    bvec = jnp.concatenate([b_forget, jnp.zeros((1, LANES - ATT_HEADS), F32)], axis=1)
    cum = _cum_fwd(small, bvec, TB)[:, :ATT_HEADS]
    cum_cols = cum.reshape(S, 8, 2).transpose(1, 0, 2)
    qa, ka = _attn_prep(qkv, cum_cols, TM)
    o, o32, lse_rows = _attn_fwd(qa, ka, qkv, AQF, TA)

    cb_row = conv_b
    xbc = _conv_fwd(xbc_raw, conv_w_full, cb_row, CV, 512)
    dt_raw = small[:, 16:48]
    dtc = dt_raw.reshape(S, SSM_GROUPS, 8).transpose(1, 0, 2)
    dtr = dt_raw.T.reshape(SSM_GROUPS, 8, S)
    bias_r = dt_bias.reshape(SSM_GROUPS, 1, 8)
    alog_b = jnp.repeat(a_log, ATT_HEAD_DIM, axis=1)
    dskip_b = jnp.repeat(d_skip, ATT_HEAD_DIM, axis=1)
    bias_c = dt_bias.reshape(SSM_GROUPS, 8, 1)
    alog_c = a_log.reshape(SSM_GROUPS, 8, 1)
    y_ssd, ssm, hs_all = _ssd_fwd(xbc, z, dtc, dtr, bias_r, alog_b, dskip_b, ssm_norm_w, bias_c, alog_c, LC)

    def merge(accs, rows, vecs, j):
        g0, g1 = _sigmoid(rows[0].astype(F32) + vecs[0]), _sigmoid(rows[1].astype(F32) + vecs[1])
        return [g0 * accs[0] + g1 * accs[1], accs[0], accs[1]], []

    g_pa, g_ps, g_out, g_fg, g_fu, g_fd = own(_chip_copies_wait("gather_rest_wait", later, False, o), shards[2:])
    wpa, wps, wout = g_pa.reshape(D, D), g_ps.reshape(SSM_INNER, D), g_out.reshape(D, D)
    wfg, wfu, wfd = _from_chip_blocks_cols(g_fg), _from_chip_blocks_cols(g_fu), g_fd.reshape(FFN_HIDDEN, D)
    mix, attn_d, ssm_d = _mm("merge", S, D, TM, 512, [(o, D, 0), (ssm, SSM_INNER, 0)], [(wpa, 0), (wps, 0)], [(0, 0), (1, 1)],
                             merge, [(D, BF16, 0), (D, BF16, 0), (D, BF16, 0)], rows=[(gl, 0), (gl, 2)],
                             vecs_n=[(b_gates, 0), (b_gates, 2)])

    def out_ln1(accs, rows, vecs, j):
        r1 = ALPHA * rows[0] + accs[0]
        h1, _, _ = _ln_fwd(r1, vecs[0], vecs[1])
        return [r1, h1, h1], []

    r1, h1, h1b = _mm("out_ln1", S, D, TM2, D, [(mix, D, 0)], [(wout, 0)], [(0, 0)], out_ln1,
                      [(D, F32, 0), (D, F32, 0), (D, BF16, 0)], rows=[(xf, 0)], vecs_n=[(ln1_g, 0), (ln1_b, 0)])

    FT = FFN_HIDDEN // 2

    def swiglu(accs, rows, vecs, j):
        g, u = accs
        return [g, u, g * _sigmoid(g) * u], []

    gate, up, hmid = _mm("ffn_up", S, FFN_HIDDEN, TM3, FT, [(h1b, D, 0)], [(wfg, 0), (wfu, 0)], [(0, 0), (0, 1)], swiglu,
                         [(FFN_HIDDEN, F32, 0), (FFN_HIDDEN, F32, 0), (FFN_HIDDEN, BF16, 0)])

    def down_ln2_loss(accs, rows, vecs, j):
        r2 = ALPHA * rows[0] + accs[0]
        yv, xhat, rstd = _ln_fwd(r2, vecs[0], vecs[1])
        diff = yv - rows[1]
        dy = diff * (1.0 / D_MODEL)
        dr2 = _ln_bwd(dy, xhat, rstd, vecs[0])
        return [dr2, dr2], [jnp.sum(dy * xhat, axis=0, keepdims=True), jnp.sum(dy, axis=0, keepdims=True),
                            (0.5 / D_MODEL) * jnp.sum(diff * diff, axis=0, keepdims=True)]

    dr2, dr2b, dln2_g, dln2_b, loss_lanes = _mm("ffn_down_ln2", S, D, TM3, D, [(hmid, FFN_HIDDEN, 0)], [(wfd, 0)], [(0, 0)],
                                               down_ln2_loss, [(D, F32, 0), (D, BF16, 0)], rows=[(h1, 0), (tgt, 0)],
                                               vecs_n=[(ln2_g, 0), (ln2_b, 0)], sums=[D, D, D])
    loss = lax.psum(jnp.sum(loss_lanes), ("x", "y", "c"))

    def dswiglu(accs, rows, vecs, j):
        g, u = rows
        sg = _sigmoid(g)
        return [accs[0] * u * sg * (1.0 + g * (1.0 - sg)), accs[0] * g * sg], []

    dgate, dup = _mm("ffn_down_bwd", S, FFN_HIDDEN, TM3, FT, [(dr2b, D, 0)], [(wfd, 0)], [(0, 0)], dswiglu,
                     [(FFN_HIDDEN, BF16, 0), (FFN_HIDDEN, BF16, 0)], nt=True, rows=[(gate, 0), (up, 0)])
    dwfd = _mm_tn("dw_ffn_down", hmid, dr2b, FFN_HIDDEN // 2, D, TS)
    dwfg = _mm_tn("dw_ffn_gate", h1b, dgate, D, FT, TS)
    dwfu = _mm_tn("dw_ffn_up", h1b, dup, D, FT, TS)
    core = lax.axis_index("c").astype(jnp.int32).reshape(1)

    def send_grads(tag, names_, blocks_, after_):
        theirs_ = _half_to_sibling("swap_halves_" + tag, blocks_)
        halves_ = [_half_sum("halfsum_" + nm, b, t, core, _row_tile(b.shape[1] // 2, b.shape[2], mult=16))
                   for nm, b, t in zip(names_, blocks_, theirs_)]
        started_, token_ = _chip_copies_start("scatter_" + tag + "_start", halves_, True, after_)
        return halves_, started_, token_

    ffn_names = ["w_ffn_gate", "w_ffn_up", "w_ffn_down"]
    ffn_halves, ffn_started, ffn_token = send_grads(
        "ffn", ffn_names, [_to_chip_blocks_cols(dwfg), _to_chip_blocks_cols(dwfu), dwfd.reshape(4, FFN_HIDDEN // 4, D)], dwfu)

    def dh1_ln1(accs, rows, vecs, j):
        dh1 = ALPHA * rows[0] + accs[0] + accs[1]
        _, xhat, rstd = _ln_fwd(rows[1], vecs[0], vecs[0])
        dr1 = _ln_bwd(dh1, xhat, rstd, vecs[0])
        return [dr1, dr1], [jnp.sum(dh1 * xhat, axis=0, keepdims=True), jnp.sum(dh1, axis=0, keepdims=True)]

    dr1, dr1b, dln1_g, dln1_b = _mm("ffn_up_bwd_ln1", S, D, TM2, D, [(dgate, FFN_HIDDEN, 0), (dup, FFN_HIDDEN, 0)],
                                    [(wfg, 0), (wfu, 0)], [(0, 0), (1, 1)], dh1_ln1, [(D, F32, 0), (D, BF16, 0)], nt=True,
                                    rows=[(dr2, 0), (r1, 0)], vecs_n=[(ln1_g, 0)], sums=[D, D], after=[ffn_token])

    def dmerge(accs, rows, vecs, j):
        dmix = accs[0]
        g0, g1 = _sigmoid(rows[0].astype(F32) + vecs[0]), _sigmoid(rows[1].astype(F32) + vecs[1])
        dgl0 = dmix * rows[2].astype(F32) * g0 * (1.0 - g0)
        dgl1 = dmix * rows[3].astype(F32) * g1 * (1.0 - g1)
        return [dmix * g0, dmix * g1, dgl0, dgl1], [jnp.sum(dgl0, axis=0, keepdims=True), jnp.sum(dgl1, axis=0, keepdims=True)]

    d_attn_d, d_ssm_d, dgl0, dgl1, dbg0, dbg1 = _mm(
        "out_bwd", S, D, TM, 512, [(dr1b, D, 0)], [(wout, 0)], [(0, 0)], dmerge, [(D, BF16, 0)] * 4, nt=True,
        rows=[(gl, 0), (gl, 2), (attn_d, 0), (ssm_d, 0)], vecs_n=[(b_gates, 0), (b_gates, 2)], sums=[D, D])
    dwout = _mm_tn("dw_out", mix, dr1b, D, D, TS)
    dwpa = _mm_tn("dw_proj_attn", o, d_attn_d, D, D, TS)
    dwps = _mm_tn("dw_proj_ssm", ssm, d_ssm_d, D, D, TS)
    mid_names = ["w_proj_attn", "w_proj_ssm", "w_out"]
    mid_halves, mid_started, mid_token = send_grads(
        "mid", mid_names, [dwpa.reshape(4, D // 4, D), dwps.reshape(4, SSM_INNER // 4, D), dwout.reshape(4, D // 4, D)], dwps)

    do, = _mm("proj_attn_bwd", S, D, TM, 512, [(d_attn_d, D, 0)], [(wpa, 0)], [(0, 0)], plain, [(D, BF16, 0)], nt=True,
              after=[mid_token])
    stats = _attn_stats(do, o32, lse_rows.transpose(0, 2, 1), AQF)
    dq, dk, dv, dck, dcq = _attn_bwd(qa, ka, qkv, do, stats, TA)

    def per_head(a):
        a = a.transpose(1, 0, 2).reshape(S, ATT_HEADS)
        return jnp.concatenate([a, jnp.zeros((S, LANES - ATT_HEADS), F32)], axis=1)

    dfl, dbf = _cum_bwd(per_head(dck.transpose(0, 2, 1)), per_head(dcq), small, bvec, TB)

    dssm, = _mm("proj_ssm_bwd", S, SSM_INNER, TM, 512, [(d_ssm_d, D, 0)], [(wps, 0)], [(0, 0)], plain, [(SSM_INNER, F32, 0)],
                nt=True)
    dxs, dbm, dcm, dz, ddt8, dnw, ddskip_b, dalog8, dbias8 = _ssd_bwd(
        xbc, z, y_ssd, dssm, hs_all, dtc, dtr, bias_r, a_log.reshape(SSM_GROUPS, 1, 8), alog_b, dskip_b, ssm_norm_w, bias_c,
        alog_c, LC)
    du_x, dcw_x, dcb_x = _conv_bwd("conv_bwd_x", xbc_raw, dxs, conv_w_full, cb_row, CV, 512, 0)
    du_b, dcw_b, dcb_b = _conv_bwd("conv_bwd_b", xbc_raw, dbm, conv_w_full, cb_row, CV, 512, SSM_INNER)
    du_c, dcw_c, dcb_c = _conv_bwd("conv_bwd_c", xbc_raw, dcm, conv_w_full, cb_row, CV, 512, SSM_INNER + SSM_GROUPS * SSM_STATE)
    dconv_w = jnp.concatenate([dcw_x, dcw_b, dcw_c], axis=1)
    dconv_b = jnp.concatenate([dcb_x, dcb_b, dcb_c], axis=1)
    ddt_raw = ddt8.transpose(1, 0, 2).reshape(S, SSM_HEADS)

    dsmall = jnp.concatenate([dfl[:, :ATT_HEADS], ddt_raw, jnp.zeros((S, 80), F32)], axis=1).astype(BF16)
    HB = SSM_GROUPS * SSM_STATE
    dw_q, dw_k, dw_v = (_mm_tn("dw_in_" + nm, xb, g_, D, D, TS) for nm, g_ in (("q", dq), ("k", dk), ("v", dv)))
    dw_z = _mm_tn("dw_in_z", xb, dz, D, D, TS)
    dw_xbc = jnp.concatenate([_mm_tn("dw_in_xs", xb, du_x, D, D, TS), _mm_tn("dw_in_b", xb, du_b, D, HB, TS),
                              _mm_tn("dw_in_c", xb, du_c, D, HB, TS)], axis=1)
    dw_g0, dw_g1 = _mm_tn("dw_in_g0", xb, dgl0, D, D, TS), _mm_tn("dw_in_g1", xb, dgl1, D, D, TS)
    dw_s = _mm_tn("dw_in_small", xb, dsmall, D, LANES, TS)
    dw_full = jnp.concatenate([dw_q, dw_k, dw_v, dw_s[:, 0:ATT_HEADS], dw_z, dw_xbc, dw_s[:, ATT_HEADS:ATT_HEADS + SSM_HEADS],
                               dw_g0, dw_g1], axis=1)

    in_halves, in_started, in_token = send_grads("in", ["w_in"], [_to_chip_blocks_cols(dw_full)], dw_full)
    def dx_first(accs, rows, vecs, j):
        return [ALPHA * rows[0] + sum(accs[1:], accs[0])], []

    def dx_more(accs, rows, vecs, j):
        return [rows[0] + sum(accs[1:], accs[0])], []

    wk = lambda col, width=D: (w_re, 0, col // width, width)
    dx_part, = _mm("dx_a", S, D, TM2, D, [(dq, D, 0), (dk, D, 0), (dv, D, 0), (dz, D, 0), (dz, D, 1)],
                   [wk(0), wk(1024), wk(2048), wk(RE_Z), wk(RE_Z + 1024)], [(k, k) for k in range(5)], dx_first,
                   [(D, F32, 0)], nt=True, rows=[(dr1, 0)], after=[in_token])
    grad_x, = _mm("dx_b", S, D, TM2, D,
                  [(du_x, D, 0), (du_x, D, 1), (du_b, HB, 0), (du_c, HB, 0), (dgl0, D, 0), (dgl1, D, 0), (dsmall, LANES, 0)],
                  [wk(RE_XBC), wk(RE_XBC + 1024), wk(RE_XBC + 2048, HB), wk(RE_XBC + 2048 + HB, HB), wk(RE_GATE),
                   wk(RE_GATE + 1024), wk(RE_SMALL, LANES)],
                  [(k, k) for k in range(7)], dx_more, [(D, F32, 0)], nt=True, rows=[(dx_part, 0)])
    names = ["w_in"] + mid_names + ffn_names
    halves = in_halves + mid_halves + ffn_halves
    stacks = (_chip_copies_wait("scatter_in_wait", in_started, True, grad_x)
              + _chip_copies_wait("scatter_mid_wait", mid_started, True, grad_x)
              + _chip_copies_wait("scatter_ffn_wait", ffn_started, True, grad_x))
    chip1 = chip.astype(jnp.int32).reshape(1)
    reduced = [_sum4("sum_" + nm, st, hv, chip1, _row_tile(st.shape[1], st.shape[2], mult=16))
               for nm, st, hv in zip(names, stacks, halves)]
    other = _sibling_swap("swap_reduced", reduced)
    big_w = [w_in, w_proj_attn, w_proj_ssm, w_out, w_ffn_gate, w_ffn_up, w_ffn_down]
    big_m = [m_w_in, m_w_proj_attn, m_w_proj_ssm, m_w_out, m_w_ffn_gate, m_w_ffn_up, m_w_ffn_down]
    big_v = [v_w_in, v_w_proj_attn, v_w_proj_ssm, v_w_out, v_w_ffn_gate, v_w_ffn_up, v_w_ffn_down]
    big = {}
    for nm, w_, m_, v_, mine, theirs in zip(names, big_w, big_m, big_v, reduced, other):
        big[nm] = _adamw_halves("adamw_" + nm, w_, m_, v_, mine, theirs, core, _row_tile(w_.shape[1] // 2, w_.shape[2]))

    dd_skip = ddskip_b.reshape(1, SSM_HEADS, ATT_HEAD_DIM).sum(axis=2)
    pieces = [dbf[:, :ATT_HEADS], dconv_w.reshape(1, SSM_CONV * SSM_CONV_DIM), dconv_b, dbias8.reshape(1, SSM_HEADS), dalog8.reshape(1, SSM_HEADS), dd_skip,
              dnw, dbg0, dbg1, dln1_g, dln1_b, dln2_g, dln2_b]
    widths = [p.shape[1] for p in pieces]
    total = sum(widths)
    P = -(-total // LANES) * LANES
    packed = jnp.concatenate(pieces + [jnp.zeros((1, P - total), F32)], axis=1)
    summed = _all_sum_small(packed)
    offs = [0]
    for wd in widths:
        offs.append(offs[-1] + wd)
    sm = [summed[:, offs[k]:offs[k + 1]] for k in range(len(pieces))]
    g_bf, g_cw_full, g_cb, g_dtb, g_al, g_ds, g_nw = sm[0], sm[1].reshape(SSM_CONV, SSM_CONV_DIM), sm[2], sm[3], sm[4], sm[5], sm[6]
    g_bg = jnp.concatenate([sm[7], sm[8]], axis=1)
    g_l1g, g_l1b, g_l2g, g_l2b = sm[9], sm[10], sm[11], sm[12]
    cshard = SSM_CONV_DIM // 4
    g_cw_shard = lax.dynamic_slice_in_dim(g_cw_full, chip * cshard, cshard, axis=1)
    small_names = ["b_forget", "conv_w", "conv_b", "dt_bias", "a_log", "d_skip", "ssm_norm_w", "b_gates", "ln1_g", "ln1_b",
                   "ln2_g", "ln2_b"]
    small_g = [g_bf, g_cw_shard.reshape(1, -1), g_cb, g_dtb, g_al, g_ds, g_nw, g_bg, g_l1g, g_l1b, g_l2g, g_l2b]
    small_w = [b_forget, conv_w[0].reshape(1, -1), conv_b, dt_bias, a_log, d_skip, ssm_norm_w, b_gates, ln1_g, ln1_b, ln2_g, ln2_b]
    small_m = [m_b_forget, m_conv_w[0].reshape(1, -1), m_conv_b, m_dt_bias, m_a_log, m_d_skip, m_ssm_norm_w, m_b_gates, m_ln1_g,
               m_ln1_b, m_ln2_g, m_ln2_b]
    small_v = [v_b_forget, v_conv_w[0].reshape(1, -1), v_conv_b, v_dt_bias, v_a_log, v_d_skip, v_ssm_norm_w, v_b_gates, v_ln1_g,
               v_ln1_b, v_ln2_g, v_ln2_b]
    sw = [a.shape[1] for a in small_w]
    stot = sum(sw)
    SP = -(-stot // LANES) * LANES

    def pack(parts):
        return jnp.concatenate(list(parts) + [jnp.zeros((1, SP - stot), F32)], axis=1).reshape(SP // LANES, LANES)

    sres = _adamw("adamw_small", pack(small_w), pack(small_m), pack(small_v), pack(small_g), SP // LANES)
    soffs = [0]
    for wd in sw:
        soffs.append(soffs[-1] + wd)
    smalls = {}
    for k, nm in enumerate(small_names):
        vals = [r.reshape(1, SP)[:, soffs[k]:soffs[k + 1]] for r in sres]
        if nm == "conv_w":
            vals = [v_.reshape(1, SSM_CONV, cshard) for v_ in vals]
        smalls[nm] = vals

    order = ["w_in", "b_forget", "conv_w", "conv_b", "dt_bias", "a_log", "d_skip", "ssm_norm_w", "w_proj_attn", "w_proj_ssm",
             "b_gates", "w_out", "ln1_g", "ln1_b", "w_ffn_gate", "w_ffn_up", "w_ffn_down", "ln2_g", "ln2_b"]
    allres = {**big, **smalls}
    outs = [loss, grad_x[None]]
    for idx in range(4):
        outs += [allres[nm][idx] for nm in order]
    return tuple(outs)
```

```python
import functools
import math

import jax
import jax.numpy as jnp
from jax import lax
from jax.experimental import pallas as pl
from jax.experimental.pallas import tpu as pltpu

F32, BF16 = jnp.float32, jnp.bfloat16
MESH = pl.DeviceIdType.MESH

D_MODEL = 1024
ATT_HEADS, ATT_HEAD_DIM = 16, 64
SSM_INNER, SSM_HEADS, SSM_GROUPS, SSM_STATE, SSM_CONV = 2048, 32, 4, 128, 4
SSM_CONV_DIM = SSM_INNER + 2 * SSM_GROUPS * SSM_STATE
GROUP_LANES = SSM_INNER // SSM_GROUPS
FFN_HIDDEN = 2816
ALPHA = 2.0 ** 0.25
LN_EPS = 1e-5
RMS_EPS = 1e-5
ADAM_LR, ADAM_B1, ADAM_B2, ADAM_EPS, ADAM_WD, ADAM_STEP = 0.001, 0.9, 0.999, 1e-08, 0.01, 10
IN_SIZES = (1024, 1024, 1024, 16, 2048, 3072, 32, 2048)
IN_WIDTH = sum(IN_SIZES)
RE_WIDTH = 3072 + 2048 + 3072 + 2048 + 128
RE_Z, RE_XBC, RE_GATE, RE_SMALL = 3072, 5120, 8192, 10240

LANES = 128
VMEM_CAP = 60 * 1024 * 1024
NEG = -1e30
TILES = dict(TM=1024, TM2=256, TM3=512, TA=512, AQF=2048, LC=256, CV=512, TS=2048, TB=256)


def _params(n_axes, vmem_bytes=None):
    return pltpu.CompilerParams(dimension_semantics=("arbitrary",) * n_axes,
                                vmem_limit_bytes=None if vmem_bytes is None else int(min(vmem_bytes, VMEM_CAP)))


def _sigmoid(v):
    return 1.0 / (1.0 + jnp.exp(-v))


def _softplus(v):
    return jnp.maximum(v, 0.0) + jnp.log(1.0 + jnp.exp(-jnp.abs(v)))


def _dot(a, b):
    return lax.dot_general(a, b, (((1,), (0,)), ((), ())), preferred_element_type=F32)


def _dot_nt(a, b):
    return lax.dot_general(a, b, (((1,), (1,)), ((), ())), preferred_element_type=F32)


def _dot_tn(a, b):
    return lax.dot_general(a, b, (((0,), (0,)), ((), ())), preferred_element_type=F32)


def _split3(v):
    hi = v.astype(BF16)
    r1 = v - hi.astype(F32)
    mid = r1.astype(BF16)
    lo = (r1 - mid.astype(F32)).astype(BF16)
    return hi, mid, lo


def _dot_exact_left(m01, v):
    hi, mid, lo = _split3(v)
    return _dot(m01, hi) + _dot(m01, mid) + _dot(m01, lo)


def _dot_exact_right(v, m01, terms=3):
    parts = _split3(v)[:terms]
    out = _dot(parts[0], m01)
    for p in parts[1:]:
        out = out + _dot(p, m01)
    return out


def _mm(name, M, N, tm, tn, lhs, rhs, pairs, e_fn, outs, *, nt=False, rows=(), vecs_n=(), sums=(), after=()):
    ni, nj = M // tm, N // tn
    assert ni * tm == M and nj * tn == N, (name, M, N, tm, tn)
    n_l, n_r, n_row, n_vn, n_o, n_s = len(lhs), len(rhs), len(rows), len(vecs_n), len(outs), len(sums)

    def body(*refs):
        pos = 0
        l_refs = refs[pos:pos + n_l]; pos += n_l
        r_refs = refs[pos:pos + n_r]; pos += n_r
        row_refs = refs[pos:pos + n_row]; pos += n_row
        vn_refs = refs[pos:pos + n_vn]; pos += n_vn + len(after)
        o_refs = refs[pos:pos + n_o]; pos += n_o
        s_refs = refs[pos:pos + n_s]; pos += n_s
        i, j = pl.program_id(0), pl.program_id(1)
        accs = []
        for li, ri in pairs:
            accs.append(_dot_nt(l_refs[li][...], r_refs[ri][...]) if nt else _dot(l_refs[li][...], r_refs[ri][...]))
        out_vals, sum_vals = e_fn(accs, [r[...] for r in row_refs], [r[...] for r in vn_refs], j)
        for r, v in zip(o_refs, out_vals):
            r[...] = v.astype(r.dtype)
        if n_s:
            col = pl.multiple_of(j * tn, LANES)

            @pl.when(i == 0)
            def _():
                for r, v in zip(s_refs, sum_vals):
                    r[:, pl.ds(col, tn)] = v

            @pl.when(i > 0)
            def _():
                for r, v in zip(s_refs, sum_vals):
                    r[:, pl.ds(col, tn)] += v

    in_specs, args, est = [], [], 0
    for arr, width, cb in lhs:
        in_specs.append(pl.BlockSpec((tm, width), lambda i, j, cb=cb: (i, cb)))
        args.append(arr); est += tm * width * arr.dtype.itemsize
    for arr, off, *ksub in rhs:
        if nt:
            kb, kw = ksub if ksub else (0, arr.shape[1])
            in_specs.append(pl.BlockSpec((tn, kw), lambda i, j, off=off, kb=kb: (j + off, kb)))
            est += tn * kw * arr.dtype.itemsize
        else:
            in_specs.append(pl.BlockSpec((arr.shape[0], tn), lambda i, j, off=off: (0, j + off)))
            est += tn * arr.shape[0] * arr.dtype.itemsize
        args.append(arr)
    for arr, off in rows:
        in_specs.append(pl.BlockSpec((tm, tn), lambda i, j, off=off: (i, j + off)))
        args.append(arr); est += tm * tn * arr.dtype.itemsize
    for arr, off in vecs_n:
        in_specs.append(pl.BlockSpec((1, tn), lambda i, j, off=off: (0, j + off)))
        args.append(arr); est += 8 * tn * 4
    for arr in after:
        in_specs.append(pl.BlockSpec(memory_space=pl.ANY))
        args.append(arr)
    out_shape, out_specs = [], []
    for total, dtype, off in outs:
        out_shape.append(jax.ShapeDtypeStruct((M, total), dtype))
        out_specs.append(pl.BlockSpec((tm, tn), lambda i, j, off=off: (i, j + off)))
        est += tm * tn * jnp.dtype(dtype).itemsize
    for total in sums:
        out_shape.append(jax.ShapeDtypeStruct((1, total), F32))
        out_specs.append(pl.BlockSpec((1, total), lambda i, j: (0, 0)))
        est += 8 * total * 4
    vmem = 2 * est + (len(pairs) + 4) * tm * tn * 4 + (8 << 20)
    return pl.pallas_call(body, name=name, grid=(ni, nj), in_specs=in_specs, out_specs=out_specs, out_shape=out_shape,
                          compiler_params=_params(2, vmem))(*args)


def _mm_tn(name, a, g, ta, tn, ts, a_cols=None, a_off=0):
    S = a.shape[0]
    Ka = a.shape[1] if a_cols is None else a_cols
    N = g.shape[1]
    assert Ka % ta == 0 and N % tn == 0 and S % ts == 0, (name, Ka, N, S)
    aoff = a_off // ta

    def body(a_ref, g_ref, o_ref):
        s = pl.program_id(2)
        part = _dot_tn(a_ref[...], g_ref[...])

        @pl.when(s == 0)
        def _():
            o_ref[...] = part

        @pl.when(s > 0)
        def _():
            o_ref[...] += part

    vmem = 2 * (ts * ta * 2 + ts * tn * 2 + ta * tn * 4) + 2 * ta * tn * 4 + (8 << 20)
    return pl.pallas_call(
        body, name=name, grid=(Ka // ta, N // tn, S // ts),
        in_specs=[pl.BlockSpec((ts, ta), lambda ia, jn, s: (s, ia + aoff)), pl.BlockSpec((ts, tn), lambda ia, jn, s: (s, jn))],
        out_specs=pl.BlockSpec((ta, tn), lambda ia, jn, s: (ia, jn)),
        out_shape=jax.ShapeDtypeStruct((Ka, N), F32), compiler_params=_params(3, vmem))(a, g)


def _tri(n, upper):
    r = lax.broadcasted_iota(jnp.int32, (n, n), 0)
    c = lax.broadcasted_iota(jnp.int32, (n, n), 1)
    return jnp.where((c >= r) if upper else (c <= r), 1.0, 0.0).astype(BF16)


def _logsig(v):
    return jnp.minimum(v, 0.0) - jnp.log(1.0 + jnp.exp(-jnp.abs(v)))


def _cum_fwd(small, bvec, tb):
    S = small.shape[0]

    def body(x_ref, b_ref, o_ref, carry):
        i = pl.program_id(0)

        @pl.when(i == 0)
        def _():
            carry[...] = jnp.zeros_like(carry)

        logf = _logsig(x_ref[...] + b_ref[...])
        cum = _dot_exact_left(_tri(tb, False), logf) + carry[0:1, :]
        o_ref[...] = cum
        carry[0:1, :] = cum[tb - 1:tb, :]

    return pl.pallas_call(
        body, name="cum_fwd", grid=(S // tb,),
        in_specs=[pl.BlockSpec((tb, LANES), lambda i: (i, 0)), pl.BlockSpec((1, LANES), lambda i: (0, 0))],
        out_specs=pl.BlockSpec((tb, LANES), lambda i: (i, 0)), out_shape=jax.ShapeDtypeStruct((S, LANES), F32),
        scratch_shapes=[pltpu.VMEM((8, LANES), F32)], compiler_params=_params(1))(small, bvec)


def _cum_bwd(dcum_k, dcum_q, small, bvec, tb):
    S = small.shape[0]
    nb = S // tb

    def body(dk_ref, dq_ref, x_ref, b_ref, o_ref, s_ref, carry):
        i = pl.program_id(0)

        @pl.when(i == 0)
        def _():
            carry[...] = jnp.zeros_like(carry)
            s_ref[...] = jnp.zeros_like(s_ref)

        rc = _dot_exact_left(_tri(tb, True), dk_ref[...] + dq_ref[...]) + carry[0:1, :]
        dfl = rc * _sigmoid(-(x_ref[...] + b_ref[...]))
        o_ref[...] = dfl
        s_ref[...] += jnp.sum(dfl, axis=0, keepdims=True)
        carry[0:1, :] = rc[0:1, :]

    rev = lambda i: (nb - 1 - i, 0)
    return pl.pallas_call(
        body, name="cum_bwd", grid=(nb,),
        in_specs=[pl.BlockSpec((tb, LANES), rev)] * 3 + [pl.BlockSpec((1, LANES), lambda i: (0, 0))],
        out_specs=[pl.BlockSpec((tb, LANES), rev), pl.BlockSpec((1, LANES), lambda i: (0, 0))],
        out_shape=[jax.ShapeDtypeStruct((S, LANES), F32), jax.ShapeDtypeStruct((1, LANES), F32)],
        scratch_shapes=[pltpu.VMEM((8, LANES), F32)], compiler_params=_params(1))(dcum_k, dcum_q, small, bvec)


N_AUG = 3


def _lane():
    return lax.broadcasted_iota(jnp.int32, (1, LANES), 1)


def _lane_mask():
    return _lane() < ATT_HEAD_DIM


def _aug_base(h):
    return ATT_HEAD_DIM * (1 - h)


def _attn_prep(qkv, cum_cols, T):
    S = qkv.shape[0]
    HP = ATT_HEADS // 2

    def body(q_ref, k_ref, c_ref, qa_ref, ka_ref):
        lane = _lane()
        q = q_ref[...]
        k = k_ref[...]
        one, zero = jnp.ones_like(q), jnp.zeros_like(q)
        for h in (0, 1):
            base = _aug_base(h)
            own = (lane < ATT_HEAD_DIM) if h == 0 else (lane >= ATT_HEAD_DIM)
            term_lanes = (lane >= base) & (lane < base + N_AUG)
            terms = [t.astype(F32) for t in _split3(c_ref[0, :, h:h + 1])]
            neg = jnp.where(lane == base, -terms[0], jnp.where(lane == base + 1, -terms[1], -terms[2])).astype(BF16)
            qa_ref[:, h * LANES:(h + 1) * LANES] = jnp.where(lane == base + N_AUG, zero, jnp.where(term_lanes, one, q))
            ka_ref[:, h * LANES:(h + 1) * LANES] = jnp.where(term_lanes, neg, jnp.where(lane == base + N_AUG, one,
                                                                                         jnp.where(own, k, zero)))

    return pl.pallas_call(
        body, name="attn_prep", grid=(S // T, HP),
        in_specs=[pl.BlockSpec((T, LANES), lambda i, hp: (i, hp)), pl.BlockSpec((T, LANES), lambda i, hp: (i, HP + hp)),
                  pl.BlockSpec((1, T, 2), lambda i, hp: (hp, i, 0))],
        out_specs=[pl.BlockSpec((T, 2 * LANES), lambda i, hp: (i, hp))] * 2,
        out_shape=[jax.ShapeDtypeStruct((S, 2 * D_MODEL), BF16)] * 2, compiler_params=_params(2))(qkv, qkv, cum_cols)


def _attn_fwd(qa, ka, qkv, T, TK):
    S = qkv.shape[0]
    nq = S // T
    r = T // TK
    HP = ATT_HEADS // 2

    def body(q0_ref, q1_ref, k0_ref, k1_ref, v_ref, o_ref, o32_ref, lse_ref):
        i = pl.program_id(1)
        qs = (q0_ref[...], q1_ref[...])
        k_refs = (k0_ref, k1_ref)
        row = lax.broadcasted_iota(jnp.int32, (TK, T), 0)
        col = lax.broadcasted_iota(jnp.int32, (TK, T), 1)
        head_rows = lax.broadcasted_iota(jnp.int32, (LANES, 1), 0) < ATT_HEAD_DIM

        def block(j, carry, q0):
            off = pl.multiple_of(j * TK, TK)
            vj = v_ref[pl.ds(off, TK), :]
            full = q0 is None
            q0 = 0 if full else q0
            m0, l0, m1, l1, acc = carry
            new, alphas, pvs = [], [], []
            for h, (m, l) in enumerate(((m0, l0), (m1, l1))):
                st = _dot_nt(k_refs[h][pl.ds(off, TK), :], qs[h][q0:, :])
                if not full:
                    st = jnp.where(row[:, :T - q0] <= col[:, :T - q0], st, NEG)
                m_old, l_old = m[:, q0:], l[:, q0:]
                m_new = jnp.maximum(m_old, jnp.max(st, axis=0, keepdims=True))
                p = jnp.exp(st - m_new)
                alpha = jnp.exp(m_old - m_new)
                l_new = alpha * l_old + jnp.sum(p, axis=0, keepdims=True)
                pvs.append(_dot_tn(vj, p.astype(BF16)))
                alphas.append(alpha)
                new += [m_new, l_new]
            part = acc[:, q0:] * jnp.where(head_rows, alphas[0], alphas[1]) + jnp.where(head_rows, pvs[0], pvs[1])
            if q0:
                keep = lambda old, upd: jnp.concatenate([old[:, :q0], upd], axis=1)
                return (keep(m0, new[0]), keep(l0, new[1]), keep(m1, new[2]), keep(l1, new[3]), keep(acc, part))
            return (new[0], new[1], new[2], new[3], part)

        init = (jnp.full((1, T), NEG, F32), jnp.zeros((1, T), F32), jnp.full((1, T), NEG, F32), jnp.zeros((1, T), F32),
                jnp.zeros((LANES, T), F32))
        n_full = i * r
        carry = lax.fori_loop(0, n_full // 2, lambda jj, c: block(2 * jj + 1, block(2 * jj, c, None), None), init)
        carry = lax.cond(n_full % 2 == 1, lambda c: block(n_full - 1, c, None), lambda c: c, carry)
        for d in range(r):
            carry = block(n_full + d, carry, d * TK)
        m0, l0, m1, l1, acc = carry
        out = (acc / jnp.where(head_rows, l0, l1)).T
        o_ref[...] = out.astype(BF16)
        o32_ref[...] = out
        lse_ref[0, 0:1, :] = m0 + jnp.log(l0)
        lse_ref[0, 1:2, :] = m1 + jnp.log(l1)

    vmem = 2 * (2 * T * LANES * 2 + 3 * S * LANES * 2 + T * LANES * (2 + 4) + 8 * T * 4) + 10 * T * TK * 4 + (8 << 20)
    qspec = lambda h: pl.BlockSpec((T, LANES), lambda hp, i, h=h: (i, 2 * hp + h))
    kspec = lambda h: pl.BlockSpec((S, LANES), lambda hp, i, h=h: (0, 2 * hp + h))
    return pl.pallas_call(
        body, name="attn_fwd", grid=(HP, nq),
        in_specs=[qspec(0), qspec(1), kspec(0), kspec(1), pl.BlockSpec((S, LANES), lambda hp, i: (0, 2 * HP + hp))],
        out_specs=[pl.BlockSpec((T, LANES), lambda hp, i: (i, hp)), pl.BlockSpec((T, LANES), lambda hp, i: (i, hp)),
                   pl.BlockSpec((1, 2, T), lambda hp, i: (hp, 0, i))],
        out_shape=[jax.ShapeDtypeStruct((S, D_MODEL), BF16), jax.ShapeDtypeStruct((S, D_MODEL), F32),
                   jax.ShapeDtypeStruct((HP, 2, S), F32)],
        compiler_params=_params(2, vmem))(qa, qa, ka, ka, qkv)


def _attn_stats(do, o32, lse_cols, T):
    S = do.shape[0]
    HP = ATT_HEADS // 2

    def body(do_ref, o_ref, lse_ref, st_ref):
        lane = lax.broadcasted_iota(jnp.int32, (LANES, 8), 0)
        c = lax.broadcasted_iota(jnp.int32, (LANES, 8), 1)
        sel = jnp.where(((c == 2) & (lane < ATT_HEAD_DIM)) | ((c == 3) & (lane >= ATT_HEAD_DIM)), 1.0, 0.0).astype(BF16)
        dd = _dot_exact_right(do_ref[...].astype(F32) * o_ref[...], sel)
        c8 = lax.broadcasted_iota(jnp.int32, (1, 8), 1)
        st_ref[0] = jnp.where(c8 == 0, lse_ref[0, :, 0:1], jnp.where(c8 == 1, lse_ref[0, :, 1:2], dd))

    return pl.pallas_call(
        body, name="attn_stats", grid=(HP, S // T),
        in_specs=[pl.BlockSpec((T, LANES), lambda hp, i: (i, hp)), pl.BlockSpec((T, LANES), lambda hp, i: (i, hp)),
                  pl.BlockSpec((1, T, 2), lambda hp, i: (hp, i, 0))],
        out_specs=pl.BlockSpec((1, T, 8), lambda hp, i: (hp, i, 0)), out_shape=jax.ShapeDtypeStruct((HP, S, 8), F32),
        compiler_params=_params(2))(do, o32, lse_cols)


def _attn_bwd(qa, ka, qkv, do, stats, T):
    S = qkv.shape[0]
    nq = S // T
    HP = ATT_HEADS // 2

    def body(k0_ref, k1_ref, v_ref, q0_ref, q1_ref, do_ref, st_ref, dq_ref, dk_ref, dv_ref, dck_ref, dcq_ref, dq_acc):
        j = pl.program_id(1)
        mA = _lane_mask()
        masks = (mA, jnp.logical_not(mA))
        q_refs = (q0_ref, q1_ref)

        @pl.when(j == 0)
        def _():
            dq_acc[...] = jnp.zeros_like(dq_acc)

        kas = (k0_ref[...], k1_ref[...])
        vj = v_ref[...]
        row = lax.broadcasted_iota(jnp.int32, (T, T), 0)
        col = lax.broadcasted_iota(jnp.int32, (T, T), 1)

        def block(i, carry, diag):
            dvt, dkt0, dkt1 = carry
            off = pl.multiple_of(i * T, T)
            doi = do_ref[pl.ds(off, T), :]
            zero = jnp.zeros_like(doi)
            dkts = [dkt0, dkt1]
            for h in (0, 1):
                qh = q_refs[h][pl.ds(off, T), :]
                doh = jnp.where(masks[h], doi, zero)
                lse = st_ref[0, pl.ds(off, T), h:h + 1]
                dd = st_ref[0, pl.ds(off, T), 2 + h:3 + h]
                sc = _dot_nt(qh, kas[h])
                if diag:
                    sc = jnp.where(row >= col, sc, NEG)
                p = jnp.exp(sc - lse)
                dp = _dot_nt(doh, vj)
                ds = (p * (dp - dd)).astype(BF16)
                dvt = dvt + _dot_tn(doh, p.astype(BF16))
                dkts[h] = dkts[h] + _dot_tn(qh, ds)
                dq_acc[h, pl.ds(off, T), :] += _dot(ds, kas[h])
            return (dvt, dkts[0], dkts[1])

        z = jnp.zeros((LANES, T), F32)
        carry = block(j, (z, z, z), True)
        dvt, dkt0, dkt1 = lax.fori_loop(j + 1, nq, lambda i, c: block(i, c, False), carry)
        dv_ref[...] = dvt.T.astype(BF16)
        dk_ref[...] = jnp.where(mA, dkt0.T, dkt1.T).astype(BF16)
        ones_q = (_aug_base(0), _aug_base(1))
        dck_ref[0, 0:1, :] = -dkt0[ones_q[0]:ones_q[0] + 1, :]
        dck_ref[0, 1:2, :] = -dkt1[ones_q[1]:ones_q[1] + 1, :]

        @pl.when(j == nq - 1)
        def _():
            dq0, dq1 = dq_acc[0], dq_acc[1]
            ones_k = (_aug_base(0) + N_AUG, _aug_base(1) + N_AUG)
            dq_ref[...] = (jnp.where(mA, dq0, dq1) * (1.0 / math.sqrt(ATT_HEAD_DIM))).astype(BF16)
            dcq_ref[0, :, 0:1] = dq0[:, ones_k[0]:ones_k[0] + 1]
            dcq_ref[0, :, 1:2] = dq1[:, ones_k[1]:ones_k[1] + 1]

    vmem = (2 * (3 * T * LANES * 2 + 3 * S * LANES * 2 + S * LANES * 4 + S * LANES * (2 + 4) + 2 * T * LANES * 2 + 8 * T * 4)
            + 2 * S * LANES * 4 + 12 * T * T * 4 + (8 << 20))
    kspec = lambda h: pl.BlockSpec((T, LANES), lambda hp, j, h=h: (j, 2 * hp + h))
    qspec = lambda h: pl.BlockSpec((S, LANES), lambda hp, j, h=h: (0, 2 * hp + h))
    blk = pl.BlockSpec((T, LANES), lambda hp, j: (j, hp))
    full = pl.BlockSpec((S, LANES), lambda hp, j: (0, hp))
    return pl.pallas_call(
        body, name="attn_bwd", grid=(HP, nq),
        in_specs=[kspec(0), kspec(1), pl.BlockSpec((T, LANES), lambda hp, j: (j, 2 * HP + hp)), qspec(0), qspec(1), full,
                  pl.BlockSpec((1, S, 8), lambda hp, j: (hp, 0, 0))],
        out_specs=[full, blk, blk, pl.BlockSpec((1, 2, T), lambda hp, j: (hp, 0, j)),
                   pl.BlockSpec((1, S, 2), lambda hp, j: (hp, 0, 0))],
        out_shape=[jax.ShapeDtypeStruct((S, D_MODEL), BF16)] * 3 + [jax.ShapeDtypeStruct((HP, 2, S), F32),
                                                                     jax.ShapeDtypeStruct((HP, S, 2), F32)],
        scratch_shapes=[pltpu.VMEM((2, S, LANES), F32)], compiler_params=_params(2, vmem))(ka, ka, qkv, qa, qa, do, stats)


HALO = 8


def _shift_down(x, d, above):
    r = pltpu.roll(x, d, 0)
    head = jnp.where(lax.broadcasted_iota(jnp.int32, (HALO, 1), 0) < d, pltpu.roll(above, d, 0), r[0:HALO])
    return head if x.shape[0] == HALO else jnp.concatenate([head, r[HALO:]], axis=0)


def _shift_up(x, d, below):
    n = x.shape[0]
    r = pltpu.roll(x, n - d, 0)
    tail = jnp.where(lax.broadcasted_iota(jnp.int32, (HALO, 1), 0) >= HALO - d, pltpu.roll(below, HALO - d, 0), r[n - HALO:])
    return jnp.concatenate([r[:n - HALO], tail], axis=0)


def _conv_fwd(u, w, b, ts, tc):
    S, C = u.shape
    hb = ts // HALO

    def body(u_ref, prev_ref, w_ref, b_ref, o_ref):
        i = pl.program_id(0)
        x = u_ref[...]
        above = jnp.where(i == 0, 0.0, prev_ref[...])
        acc = b_ref[...] + w_ref[3:4, :] * x
        for k in range(SSM_CONV - 1):
            acc = acc + w_ref[k:k + 1, :] * _shift_down(x, SSM_CONV - 1 - k, above)
        o_ref[...] = acc * _sigmoid(acc)

    return pl.pallas_call(
        body, name="conv_fwd", grid=(S // ts, C // tc),
        in_specs=[pl.BlockSpec((ts, tc), lambda i, j: (i, j)),
                  pl.BlockSpec((HALO, tc), lambda i, j: (jnp.maximum(i * hb - 1, 0), j)),
                  pl.BlockSpec((SSM_CONV, tc), lambda i, j: (0, j)), pl.BlockSpec((1, tc), lambda i, j: (0, j))],
        out_specs=pl.BlockSpec((ts, tc), lambda i, j: (i, j)), out_shape=jax.ShapeDtypeStruct((S, C), F32),
        compiler_params=_params(2))(u, u, w, b)


def _conv_bwd(name, u, dy, w, b, ts, tc, col0):
    S, C = dy.shape
    cb = col0 // tc
    assert cb * tc == col0
    hb = ts // HALO
    nb = S // ts

    def body(u_ref, uprev_ref, unext_ref, dy_ref, dynext_ref, w_ref, b_ref, du_ref, dw_ref, db_ref):
        i = pl.program_id(1)
        x = u_ref[...]
        above = jnp.where(i == 0, 0.0, uprev_ref[...])
        ws = [w_ref[k:k + 1, :] for k in range(SSM_CONV)]

        def dsilu(pre):
            sg = _sigmoid(pre)
            return sg * (1.0 + pre * (1.0 - sg))

        shifted = [_shift_down(x, SSM_CONV - 1 - k, above) for k in range(SSM_CONV - 1)] + [x]
        pre = b_ref[...]
        for k in range(SSM_CONV):
            pre = pre + ws[k] * shifted[k]
        g = dy_ref[...] * dsilu(pre)
        nxt = unext_ref[...]
        tail = x[ts - HALO:, :]
        pre_n = b_ref[...] + ws[SSM_CONV - 1] * nxt
        for k in range(SSM_CONV - 1):
            pre_n = pre_n + ws[k] * _shift_down(nxt, SSM_CONV - 1 - k, tail)
        g_next = jnp.where(i == nb - 1, 0.0, dynext_ref[...] * dsilu(pre_n))
        du = ws[SSM_CONV - 1] * g
        for k in range(SSM_CONV - 1):
            du = du + ws[k] * _shift_up(g, SSM_CONV - 1 - k, g_next)
        du_ref[...] = du.astype(du_ref.dtype)
        dws = [jnp.sum(g * shifted[k], axis=0, keepdims=True) for k in range(SSM_CONV)]
        dbs = jnp.sum(g, axis=0, keepdims=True)

        @pl.when(i == 0)
        def _():
            for k in range(SSM_CONV):
                dw_ref[k:k + 1, :] = dws[k]
            db_ref[...] = dbs

        @pl.when(i > 0)
        def _():
            for k in range(SSM_CONV):
                dw_ref[k:k + 1, :] += dws[k]
            db_ref[...] += dbs

    nxt = lambda off: (lambda j, i: (jnp.minimum((i + 1) * hb, S // HALO - 1), j + off))
    return pl.pallas_call(
        body, name=name, grid=(C // tc, nb),
        in_specs=[pl.BlockSpec((ts, tc), lambda j, i: (i, j + cb)),
                  pl.BlockSpec((HALO, tc), lambda j, i: (jnp.maximum(i * hb - 1, 0), j + cb)),
                  pl.BlockSpec((HALO, tc), nxt(cb)),
                  pl.BlockSpec((ts, tc), lambda j, i: (i, j)),
                  pl.BlockSpec((HALO, tc), nxt(0)),
                  pl.BlockSpec((SSM_CONV, tc), lambda j, i: (0, j + cb)), pl.BlockSpec((1, tc), lambda j, i: (0, j + cb))],
        out_specs=[pl.BlockSpec((ts, tc), lambda j, i: (i, j)), pl.BlockSpec((SSM_CONV, tc), lambda j, i: (0, j)),
                   pl.BlockSpec((1, tc), lambda j, i: (0, j))],
        out_shape=[jax.ShapeDtypeStruct((S, C), BF16), jax.ShapeDtypeStruct((SSM_CONV, C), F32), jax.ShapeDtypeStruct((1, C), F32)],
        compiler_params=_params(2))(u, u, u, dy, dy, w, b)


def _head_sum():
    lane = jnp.right_shift(lax.broadcasted_iota(jnp.int32, (GROUP_LANES, 8), 0), 6)
    r = lax.broadcasted_iota(jnp.int32, (GROUP_LANES, 8), 1)
    return jnp.where(lane == r, 1.0, 0.0).astype(BF16)


def _head_expand():
    r = lax.broadcasted_iota(jnp.int32, (8, GROUP_LANES), 0)
    c = jnp.right_shift(lax.broadcasted_iota(jnp.int32, (8, GROUP_LANES), 1), 6)
    return jnp.where(r == c, 1.0, 0.0).astype(BF16)


def _ssd_common(dtc_ref, dtr_ref, bias_r, alog_b, bias_c, alog_c, L):
    a_b = -jnp.exp(alog_b)
    dt = _dot_exact_right(_softplus(dtc_ref[0] + bias_r), _head_expand())
    acum = _dot_exact_left(_tri(L, False), dt * a_b)
    a_c = -jnp.exp(alog_c)
    dtr = _softplus(dtr_ref[0] + bias_c)
    acum_r = _dot_exact_right(dtr * a_c, _tri(L, True))
    return a_b, dt, acum, acum_r


def _ssd_specs(L, nc, rev):
    cc = (lambda c: nc - 1 - c) if rev else (lambda c: c)
    G = SSM_GROUPS
    blk = pl.BlockSpec((L, GROUP_LANES), lambda g, c: (cc(c), g))
    dtc = pl.BlockSpec((1, L, 8), lambda g, c: (g, cc(c), 0))
    rowv = pl.BlockSpec((1, 1, 8), lambda g, c: (g, 0, 0))
    xs = blk
    bm = pl.BlockSpec((L, SSM_STATE), lambda g, c: (cc(c), SSM_INNER // SSM_STATE + g))
    cm = pl.BlockSpec((L, SSM_STATE), lambda g, c: (cc(c), SSM_INNER // SSM_STATE + G + g))
    dtr = pl.BlockSpec((1, 8, L), lambda g, c: (g, 0, cc(c)))
    vec = pl.BlockSpec((1, GROUP_LANES), lambda g, c: (0, g))
    colv = pl.BlockSpec((1, 8, 1), lambda g, c: (g, 0, 0))
    hs = pl.BlockSpec((1, 1, SSM_STATE, GROUP_LANES), lambda g, c: (g, cc(c), 0, 0))
    return blk, xs, bm, cm, dtc, dtr, vec, rowv, colv, hs


def _ssd_fwd(xbc, z, dtc, dtr, bias_r, alog_b, dskip_b, normw, bias_c, alog_c, L):
    S = z.shape[0]
    nc = S // L
    blk, xs, bm, cm, dtcs, dtrs, vec, rowv, colv, hs = _ssd_specs(L, nc, False)

    def body(x_ref, b_ref, c_ref, z_ref, dtc_ref, dtr_ref, bias_ref, alog_ref, dskip_ref, nw_ref, biasc_ref, alogc_ref,
             y_ref, ssm_ref, hs_ref, h_scr):
        c = pl.program_id(1)

        @pl.when(c == 0)
        def _():
            h_scr[...] = jnp.zeros_like(h_scr)

        mA = _lane_mask()
        a_b, dt, acum, acum_r = _ssd_common(dtc_ref, dtr_ref, bias_ref[0], alog_ref[...], biasc_ref[0], alogc_ref[0], L)
        x = x_ref[...]
        cb, bb = c_ref[...].astype(BF16), b_ref[...].astype(BF16)
        hprev = h_scr[...]
        hs_ref[0, 0] = hprev
        xdt = x * dt
        xdt_b = xdt.astype(BF16)
        gmat = _dot_nt(cb, bb)
        row = lax.broadcasted_iota(jnp.int32, (L, L), 0)
        col = lax.broadcasted_iota(jnp.int32, (L, L), 1)
        parts = []
        for p in range(GROUP_LANES // LANES):
            xp = xdt_b[:, p * LANES:(p + 1) * LANES]
            yd = []
            for hh in (0, 1):
                r = 2 * p + hh
                acol = acum[:, r * ATT_HEAD_DIM:r * ATT_HEAD_DIM + 1]
                arow = acum_r[r:r + 1, :]
                lm = jnp.exp(jnp.where(row >= col, acol - arow, NEG))
                yd.append(_dot((gmat * lm).astype(BF16), xp))
            parts.append(jnp.where(mA, yd[0], yd[1]))
        ydiag = jnp.concatenate(parts, axis=1)
        yoff = jnp.exp(acum) * _dot(cb, hprev.astype(BF16))
        y = ydiag + yoff + dskip_ref[...] * x
        aend = acum[L - 1:L, :]
        wgt = (jnp.exp(aend - acum) * xdt).astype(BF16)
        h_scr[...] = jnp.exp(aend) * hprev + _dot_tn(bb, wgt)
        y_ref[...] = y
        zz = z_ref[...].astype(F32)
        u = y * (zz * _sigmoid(zz))
        rs = lax.rsqrt(jnp.mean(u * u, axis=1, keepdims=True) + RMS_EPS)
        ssm_ref[...] = (u * rs * nw_ref[...]).astype(BF16)

    return pl.pallas_call(
        body, name="ssd_fwd", grid=(SSM_GROUPS, nc),
        in_specs=[xs, bm, cm, blk, dtcs, dtrs, rowv, vec, vec, vec, colv, colv],
        out_specs=[blk, blk, hs],
        out_shape=[jax.ShapeDtypeStruct((S, SSM_INNER), F32), jax.ShapeDtypeStruct((S, SSM_INNER), BF16),
                   jax.ShapeDtypeStruct((SSM_GROUPS, nc, SSM_STATE, GROUP_LANES), F32)],
        scratch_shapes=[pltpu.VMEM((SSM_STATE, GROUP_LANES), F32)],
        compiler_params=_params(2, 48 << 20))(xbc, xbc, xbc, z, dtc, dtr, bias_r, alog_b, dskip_b, normw, bias_c, alog_c)


def _ssd_bwd(xbc, z, y, dssm, hs_all, dtc, dtr, bias_r, alog_r, alog_b, dskip_b, normw, bias_c, alog_c, L):
    S = z.shape[0]
    nc = S // L
    blk, xs, bm, cm, dtcs, dtrs, vec, rowv, colv, hs = _ssd_specs(L, nc, True)

    def body(x_ref, b_ref, c_ref, z_ref, y_ref, dssm_ref, hs_ref, dtc_ref, dtr_ref, bias_ref, alogr_ref, alog_ref, dskip_ref, nw_ref,
             biasc_ref, alogc_ref,
             dx_ref, db_ref, dc_ref, dz_ref, ddt_ref, dnw_ref, ddskip_ref, dalog_ref, dbias_ref, dh_scr):
        c = pl.program_id(1)

        @pl.when(c == 0)
        def _():
            dh_scr[...] = jnp.zeros_like(dh_scr)

        mA = _lane_mask()
        masks = (mA, jnp.logical_not(mA))
        a_b, dt, acum, acum_r = _ssd_common(dtc_ref, dtr_ref, bias_ref[0], alog_ref[...], biasc_ref[0], alogc_ref[0], L)
        x, zz, y, dssm = x_ref[...], z_ref[...].astype(F32), y_ref[...], dssm_ref[...]
        cb, bb = c_ref[...].astype(BF16), b_ref[...].astype(BF16)
        hprev = hs_ref[0, 0]
        hb = hprev.astype(BF16)
        ds = dh_scr[...]
        dsb = ds.astype(BF16)
        dskip = dskip_ref[...]
        aend = acum[L - 1:L, :]
        e_a, e_end = jnp.exp(acum), jnp.exp(aend)
        dte = jnp.exp(aend - acum)
        xdt = x * dt
        xdt_b = xdt.astype(BF16)
        sg = _sigmoid(zz)
        sz = zz * sg
        u = y * sz
        rs = lax.rsqrt(jnp.mean(u * u, axis=1, keepdims=True) + RMS_EPS)
        un = u * rs
        dun = dssm * nw_ref[...]
        du = rs * (dun - un * jnp.mean(dun * un, axis=1, keepdims=True))
        dy = du * sz
        dz_ref[...] = (du * y * sg * (1.0 + zz * (1.0 - sg))).astype(dz_ref.dtype)
        dy_b = dy.astype(BF16)
        dch_b = (dy * e_a).astype(BF16)
        dc = _dot_nt(dch_b, hb)
        dhprev = _dot_tn(cb, dch_b)
        gt = _dot_nt(bb, cb)
        row = lax.broadcasted_iota(jnp.int32, (L, L), 0)
        col = lax.broadcasted_iota(jnp.int32, (L, L), 1)
        dgt = jnp.zeros((L, L), F32)
        parts = []
        for p in range(GROUP_LANES // LANES):
            xp = xdt_b[:, p * LANES:(p + 1) * LANES]
            dyp = dy_b[:, p * LANES:(p + 1) * LANES]
            zero = jnp.zeros_like(dyp)
            acc = None
            for hh in (0, 1):
                r = 2 * p + hh
                acol = acum[:, r * ATT_HEAD_DIM:r * ATT_HEAD_DIM + 1]
                arow = acum_r[r:r + 1, :]
                lmt = jnp.exp(jnp.where(row <= col, arow - acol, NEG))
                dyh = jnp.where(masks[hh], dyp, zero)
                part = _dot((gt * lmt).astype(BF16), dyh)
                acc = part if acc is None else acc + part
                dgt = dgt + _dot_nt(xp, dyh) * lmt
            parts.append(acc)
        dxdt_diag = jnp.concatenate(parts, axis=1)
        dgt_b = dgt.astype(BF16)
        db = _dot(dgt_b, cb)
        dc = dc + _dot_tn(dgt_b, bb)
        dxdt_state = dte * _dot(bb, dsb)
        db = db + _dot_nt((dte * xdt).astype(BF16), dsb)
        dxdt = dxdt_diag + dxdt_state
        dy_r, xdt_r = dy_b.astype(F32), xdt_b.astype(F32)
        dac = dy_r * (y - dskip * x) - xdt_r * dxdt
        tail = jnp.sum(xdt_r * dxdt_state, axis=0, keepdims=True) + e_end * jnp.sum(ds * hprev, axis=0, keepdims=True)
        rowl = lax.broadcasted_iota(jnp.int32, (L, 1), 0)
        dac = dac + jnp.where(rowl == L - 1, tail, 0.0)
        rc = _dot_exact_left(_tri(L, True), dac)
        hsum = _head_sum()
        hs1 = _dot_exact_right(dxdt * x, hsum, 2)
        hs2 = _dot_exact_right(rc, hsum, 2)
        a8 = -jnp.exp(alogr_ref[0])
        dtraw8 = dtc_ref[0] + bias_ref[0]
        ddtraw = (hs1 + a8 * hs2) * _sigmoid(dtraw8)
        dx_ref[...] = dskip * dy + dxdt * dt
        db_ref[...] = db
        dc_ref[...] = dc
        ddt_ref[0] = ddtraw
        dh_scr[...] = e_end * ds + dhprev
        sums = (jnp.sum(dssm * un, axis=0, keepdims=True), jnp.sum(dy * x, axis=0, keepdims=True))
        refs = (dnw_ref, ddskip_ref)
        sums8 = (a8 * jnp.sum(hs2 * _softplus(dtraw8), axis=0, keepdims=True), jnp.sum(ddtraw, axis=0, keepdims=True))
        refs8 = (dalog_ref, dbias_ref)

        @pl.when(c == 0)
        def _():
            for r, v in zip(refs, sums):
                r[...] = v
            for r, v in zip(refs8, sums8):
                r[0] = v

        @pl.when(c > 0)
        def _():
            for r, v in zip(refs, sums):
                r[...] += v
            for r, v in zip(refs8, sums8):
                r[0] += v

    nbc = pl.BlockSpec((L, SSM_STATE), lambda g, c: (nc - 1 - c, g))
    return pl.pallas_call(
        body, name="ssd_bwd", grid=(SSM_GROUPS, nc),
        in_specs=[xs, bm, cm, blk, blk, blk, hs, dtcs, dtrs, rowv, rowv, vec, vec, vec, colv, colv],
        out_specs=[blk, nbc, nbc, blk, dtcs, vec, vec, rowv, rowv],
        out_shape=[jax.ShapeDtypeStruct((S, SSM_INNER), F32), jax.ShapeDtypeStruct((S, SSM_GROUPS * SSM_STATE), F32),
                   jax.ShapeDtypeStruct((S, SSM_GROUPS * SSM_STATE), F32), jax.ShapeDtypeStruct((S, SSM_INNER), BF16),
                   jax.ShapeDtypeStruct((SSM_GROUPS, S, 8), F32)] + [jax.ShapeDtypeStruct((1, SSM_INNER), F32)] * 2
                  + [jax.ShapeDtypeStruct((SSM_GROUPS, 1, 8), F32)] * 2,
        scratch_shapes=[pltpu.VMEM((SSM_STATE, GROUP_LANES), F32)],
        compiler_params=_params(2, 56 << 20))(xbc, xbc, xbc, z, y, dssm, hs_all, dtc, dtr, bias_r, alog_r, alog_b, dskip_b,
                                              normw, bias_c, alog_c)


def _place():
    return lax.axis_index("x"), lax.axis_index("y"), lax.axis_index("c")


def _other_chips(x, y):
    return [(1 - x, y), (x, 1 - y), (1 - x, 1 - y)]


def _half_rows(rows, which):
    hr = rows // 2
    if isinstance(which, int):
        return pl.ds(which * hr, hr)
    return pl.ds(pl.multiple_of(which * hr, 8), hr)


def _chip_gather(name, shards, split):
    n = len(shards)
    ANY = pl.BlockSpec(memory_space=pl.ANY)

    def body(*refs):
        ins, outs = refs[:n], refs[n:2 * n]
        send, recv, fsend, frecv = refs[2 * n:]
        x, y, c = _place()
        me = 2 * x + y
        sibling = (x, y, 1 - c)
        chips = _other_chips(x, y)

        def piece(a, chip_idx, which):
            if split[a]:
                return outs[a].at[chip_idx, _half_rows(shards[a].shape[0], which)]
            return outs[a].at[chip_idx]

        def ici(k, a, to_chip, src_chip):
            src = ins[a].at[_half_rows(shards[a].shape[0], c)] if split[a] else ins[a]
            return pltpu.make_async_remote_copy(src_ref=src, dst_ref=piece(a, src_chip, c), send_sem=send.at[k, a],
                                                recv_sem=recv.at[k, a], device_id=(*to_chip, c), device_id_type=MESH)

        def fwd(k, a, src_chip, which):
            return pltpu.make_async_remote_copy(src_ref=piece(a, src_chip, which), dst_ref=piece(a, src_chip, which),
                                                send_sem=fsend.at[k, a], recv_sem=frecv.at[k, a], device_id=sibling,
                                                device_id_type=MESH)

        sends = []
        for k, chip in enumerate(chips):
            for a in range(n):
                cp = ici(k, a, chip, me)
                cp.start()
                sends.append(cp)
        for k, (ox, oy) in enumerate(chips):
            src = 2 * ox + oy
            for a in range(n):
                ici(k, a, (ox, oy), src).wait_recv()
                if split[a]:
                    cp = fwd(k, a, src, c)
                    cp.start()
                    sends.append(cp)
        for k, (ox, oy) in enumerate(chips):
            for a in range(n):
                if split[a]:
                    fwd(k, a, 2 * ox + oy, 1 - c).wait_recv()
        for cp in sends:
            cp.wait_send()

    sem = pltpu.SemaphoreType.DMA((3, n))
    return pl.pallas_call(
        body, name=name, in_specs=[ANY] * n, out_specs=[ANY] * n,
        out_shape=[jax.ShapeDtypeStruct((4,) + s.shape, s.dtype) for s in shards],
        scratch_shapes=[sem, sem, sem, sem])(*shards)


def _chip_copies_start(name, srcs, per_chip_src, after):
    n = len(srcs)
    HBM = pl.BlockSpec(memory_space=pltpu.HBM)
    SEM = pl.BlockSpec(memory_space=pltpu.SEMAPHORE)
    lands = [pltpu.with_memory_space_constraint(lax.empty(a.shape if per_chip_src else (4,) + a.shape, a.dtype), pltpu.HBM)
             for a in srcs]

    def body(*refs):
        ins, land = refs[:n], refs[n:2 * n]
        send, recv = refs[2 * n + 1], refs[2 * n + 2]
        token = refs[-1]
        x, y, c = _place()
        me = 2 * x + y
        for k, (ox, oy) in enumerate(_other_chips(x, y)):
            for a in range(n):
                src = ins[a].at[2 * ox + oy] if per_chip_src else ins[a]
                pltpu.make_async_remote_copy(src_ref=src, dst_ref=land[a].at[me], send_sem=send.at[k * n + a], recv_sem=recv.at[k * n + a],
                                             device_id=(ox, oy, c), device_id_type=MESH).start()
        token[...] = jnp.zeros_like(token)

    sem = pltpu.SemaphoreType.DMA((3 * n,))
    res = pl.pallas_call(
        body, name=name,
        out_shape=[sem, sem] + [pltpu.HBM(a.shape, a.dtype) for a in srcs] + [pltpu.HBM(b.shape, b.dtype) for b in lands]
                  + [jax.ShapeDtypeStruct((8, LANES), F32)],
        in_specs=[HBM] * (2 * n) + [pl.BlockSpec(memory_space=pl.ANY)],
        out_specs=[SEM, SEM] + [HBM] * (2 * n) + [pl.BlockSpec(memory_space=pltpu.VMEM)],
        input_output_aliases={k: 2 + k for k in range(2 * n)},
        compiler_params=pltpu.CompilerParams(has_side_effects=pltpu.SideEffectType.DATAFLOW_SIDE_EFFECTING),
    )(*[pltpu.with_memory_space_constraint(a, pltpu.HBM) for a in srcs], *lands, after)
    return res[:-1], res[-1]


def _chip_copies_wait(name, started, per_chip_src, after):
    send, recv = started[0], started[1]
    n = (len(started) - 2) // 2
    srcs, lands = started[2:2 + n], started[2 + n:]
    HBM = pl.BlockSpec(memory_space=pltpu.HBM)
    SEM = pl.BlockSpec(memory_space=pltpu.SEMAPHORE)

    def body(*refs):
        ins, land = refs[:n], refs[n:2 * n]
        send_sem, recv_sem = refs[2 * n], refs[2 * n + 1]
        x, y, c = _place()
        me = 2 * x + y
        for k, (ox, oy) in enumerate(_other_chips(x, y)):
            for a in range(n):
                src = ins[a].at[me] if per_chip_src else ins[a]
                cp = pltpu.make_async_remote_copy(src_ref=src, dst_ref=land[a].at[2 * ox + oy], send_sem=send_sem.at[k * n + a],
                                                  recv_sem=recv_sem.at[k * n + a], device_id=(ox, oy, c), device_id_type=MESH)
                cp.wait_send()
                cp.wait_recv()

    res = pl.pallas_call(
        body, name=name,
        out_shape=[pltpu.HBM(a.shape, a.dtype) for a in srcs] + [pltpu.HBM(b.shape, b.dtype) for b in lands],
        in_specs=[HBM] * (2 * n) + [SEM, SEM, pl.BlockSpec(memory_space=pl.ANY)], out_specs=[HBM] * (2 * n),
        input_output_aliases={k: k for k in range(2 * n)},
        compiler_params=pltpu.CompilerParams(has_side_effects=pltpu.SideEffectType.DATAFLOW_SIDE_EFFECTING),
    )(*srcs, *lands, send, recv, after)
    return res[n:]


def _half_to_sibling(name, blocks):
    n = len(blocks)
    ANY = pl.BlockSpec(memory_space=pl.ANY)

    def body(*refs):
        ins, outs = refs[:n], refs[n:2 * n]
        send, recv = refs[2 * n:]
        x, y, c = _place()
        cps = [pltpu.make_async_remote_copy(src_ref=ins[a].at[:, _half_rows(blocks[a].shape[1], 1 - c)], dst_ref=outs[a],
                                            send_sem=send.at[a], recv_sem=recv.at[a], device_id=(x, y, 1 - c),
                                            device_id_type=MESH) for a in range(n)]
        for cp in cps:
            cp.start()
        for cp in cps:
            cp.wait_recv()
        for cp in cps:
            cp.wait_send()

    return pl.pallas_call(
        body, name=name, in_specs=[ANY] * n, out_specs=[ANY] * n,
        out_shape=[jax.ShapeDtypeStruct((4, b.shape[1] // 2, b.shape[2]), b.dtype) for b in blocks],
        scratch_shapes=[pltpu.SemaphoreType.DMA((n,)), pltpu.SemaphoreType.DMA((n,))])(*blocks)


def _sibling_swap(name, arrs):
    n = len(arrs)
    ANY = pl.BlockSpec(memory_space=pl.ANY)

    def body(*refs):
        ins, outs = refs[:n], refs[n:2 * n]
        send, recv = refs[2 * n:]
        x, y, c = _place()
        cps = [pltpu.make_async_remote_copy(src_ref=ins[a], dst_ref=outs[a], send_sem=send.at[a], recv_sem=recv.at[a],
                                            device_id=(x, y, 1 - c), device_id_type=MESH) for a in range(n)]
        for cp in cps:
            cp.start()
        for cp in cps:
            cp.wait_recv()
        for cp in cps:
            cp.wait_send()

    return pl.pallas_call(
        body, name=name, in_specs=[ANY] * n, out_specs=[ANY] * n,
        out_shape=[jax.ShapeDtypeStruct(a.shape, a.dtype) for a in arrs],
        scratch_shapes=[pltpu.SemaphoreType.DMA((n,)), pltpu.SemaphoreType.DMA((n,))])(*arrs)


N_DEV = 8


def _all_sum_small(vec):
    P = vec.shape[1]

    def body(v_ref, o_ref, buf, send, recv):
        x, y, c = _place()
        me = 4 * x + 2 * y + c
        buf[me] = v_ref[...]

        def peer(r):
            return ((1 - x) if (r >> 2) & 1 else x, (1 - y) if (r >> 1) & 1 else y, (1 - c) if r & 1 else c)

        sends = []
        for r in range(1, N_DEV):
            cp = pltpu.make_async_remote_copy(src_ref=v_ref, dst_ref=buf.at[me], send_sem=send.at[r], recv_sem=recv.at[r],
                                              device_id=peer(r), device_id_type=MESH)
            cp.start()
            sends.append(cp)
        for r in range(1, N_DEV):
            px, py, pc = peer(r)
            pltpu.make_async_remote_copy(src_ref=v_ref, dst_ref=buf.at[4 * px + 2 * py + pc], send_sem=send.at[r],
                                         recv_sem=recv.at[r], device_id=(px, py, pc), device_id_type=MESH).wait_recv()
        for cp in sends:
            cp.wait_send()
        tot = buf[0]
        for d in range(1, N_DEV):
            tot = tot + buf[d]
        o_ref[...] = tot

    return pl.pallas_call(
        body, name="all_sum_small", in_specs=[pl.BlockSpec(memory_space=pltpu.VMEM)],
        out_specs=pl.BlockSpec(memory_space=pltpu.VMEM), out_shape=jax.ShapeDtypeStruct((1, P), F32),
        scratch_shapes=[pltpu.VMEM((N_DEV, 1, P), F32), pltpu.SemaphoreType.DMA((N_DEV,)), pltpu.SemaphoreType.DMA((N_DEV,))],
    )(vec)


def _half_sum(name, blocks, theirs, core, tr):
    _, R, C = blocks.shape
    hr = R // 2
    nb = hr // tr
    assert nb * tr == hr

    def body(c_ref, a_ref, b_ref, o_ref):
        o_ref[...] = (a_ref[...] + b_ref[...]).astype(BF16)

    grid_spec = pltpu.PrefetchScalarGridSpec(
        num_scalar_prefetch=1, grid=(4, nb),
        in_specs=[pl.BlockSpec((1, tr, C), lambda b, i, c_ref: (b, c_ref[0] * nb + i, 0)),
                  pl.BlockSpec((1, tr, C), lambda b, i, c_ref: (b, i, 0))],
        out_specs=pl.BlockSpec((1, tr, C), lambda b, i, c_ref: (b, i, 0)))
    return pl.pallas_call(body, name=name, grid_spec=grid_spec, out_shape=jax.ShapeDtypeStruct((4, hr, C), BF16),
                          compiler_params=_params(2, 40 << 20))(core, blocks, theirs)


def _sum4(name, stack, mine, chip, tr):
    _, R, C = stack.shape

    def body(chip_ref, s_ref, m_ref, o_ref):
        t = [jnp.where(chip_ref[0] == j, m_ref[j], s_ref[j]).astype(F32) for j in range(4)]
        o_ref[...] = ((t[0] + t[1]) + t[2]) + t[3]

    blk = pl.BlockSpec((4, tr, C), lambda i, chip_ref: (0, i, 0))
    grid_spec = pltpu.PrefetchScalarGridSpec(num_scalar_prefetch=1, grid=(R // tr,), in_specs=[blk, blk],
                                             out_specs=pl.BlockSpec((tr, C), lambda i, chip_ref: (i, 0)))
    return pl.pallas_call(body, name=name, grid_spec=grid_spec, out_shape=jax.ShapeDtypeStruct((R, C), F32),
                          compiler_params=_params(1, 40 << 20))(chip, stack, mine)


def _adamw_math(w, m, v, g):
    c1 = 1.0 - ADAM_B1 ** ADAM_STEP
    c2 = 1.0 - ADAM_B2 ** ADAM_STEP
    nm = ADAM_B1 * m + (1.0 - ADAM_B1) * g
    nv = ADAM_B2 * v + (1.0 - ADAM_B2) * (g * g)
    return -ADAM_LR * ((nm / c1) / (jnp.sqrt(nv / c2) + ADAM_EPS) + ADAM_WD * w), nm, nv


def _adamw(name, w, m, v, g, tr):
    R, C = w.shape

    def body(w_ref, m_ref, v_ref, ga_ref, g_ref, d_ref, nm_ref, nv_ref):
        g = ga_ref[...]
        g_ref[...] = g
        d_ref[...], nm_ref[...], nv_ref[...] = _adamw_math(w_ref[...], m_ref[...], v_ref[...], g)

    spec = pl.BlockSpec((tr, C), lambda i: (i, 0))
    return pl.pallas_call(body, name=name, grid=(R // tr,), in_specs=[spec] * 4, out_specs=[spec] * 4,
                          out_shape=[jax.ShapeDtypeStruct((R, C), F32)] * 4, compiler_params=_params(1, 40 << 20))(w, m, v, g)


def _adamw_halves(name, w, m, v, mine, theirs, core, tr):
    _, R, C = w.shape
    nb = (R // 2) // tr
    assert 2 * nb * tr == R

    def body(c_ref, w_ref, m_ref, v_ref, a_ref, b_ref, g_ref, d_ref, nm_ref, nv_ref):
        g = jnp.where((pl.program_id(0) // nb) == c_ref[0], a_ref[...], b_ref[...])
        g_ref[0] = g
        d_ref[0], nm_ref[0], nv_ref[0] = _adamw_math(w_ref[0], m_ref[0], v_ref[0], g)

    spec = pl.BlockSpec((1, tr, C), lambda i, c_ref: (0, i, 0))
    half = lambda own: pl.BlockSpec((tr, C), lambda i, c_ref, own=own: (
        jnp.clip(i - (c_ref[0] if own else 1 - c_ref[0]) * nb, 0, nb - 1), 0))
    grid_spec = pltpu.PrefetchScalarGridSpec(num_scalar_prefetch=1, grid=(R // tr,),
                                             in_specs=[spec, spec, spec, half(True), half(False)], out_specs=[spec] * 4)
    return pl.pallas_call(body, name=name, grid_spec=grid_spec, out_shape=[jax.ShapeDtypeStruct((1, R, C), F32)] * 4,
                          compiler_params=_params(1, 40 << 20))(core, w, m, v, mine, theirs)


def _row_tile(rows, cols, budget_bytes=1 << 20, mult=8):
    best = None
    for t in range(mult, rows + 1, mult):
        if rows % t == 0 and t * cols * 4 <= budget_bytes:
            best = t
    return best if best is not None else rows


def _ln_fwd(r, g, b):
    mu = jnp.mean(r, axis=1, keepdims=True)
    xc = r - mu
    rstd = lax.rsqrt(jnp.mean(xc * xc, axis=1, keepdims=True) + LN_EPS)
    xhat = xc * rstd
    return xhat * g + b, xhat, rstd


def _ln_bwd(dy, xhat, rstd, g):
    dxh = dy * g
    return rstd * (dxh - jnp.mean(dxh, axis=1, keepdims=True) - xhat * jnp.mean(dxh * xhat, axis=1, keepdims=True))


def _to_chip_blocks_cols(a):
    R, C4 = a.shape
    return a.reshape(R, 4, C4 // 4).transpose(1, 0, 2)


def _from_chip_blocks_cols(a):
    return a.transpose(1, 0, 2).reshape(a.shape[1], 4 * a.shape[2])


def kernel(x, w_in, b_forget, conv_w, conv_b, dt_bias, a_log, d_skip, ssm_norm_w, w_proj_attn, w_proj_ssm, b_gates, w_out, ln1_g, ln1_b, w_ffn_gate, w_ffn_up, w_ffn_down, ln2_g, ln2_b, loss_target, m_w_in, m_b_forget, m_conv_w, m_conv_b, m_dt_bias, m_a_log, m_d_skip, m_ssm_norm_w, m_w_proj_attn, m_w_proj_ssm, m_b_gates, m_w_out, m_ln1_g, m_ln1_b, m_w_ffn_gate, m_w_ffn_up, m_w_ffn_down, m_ln2_g, m_ln2_b, v_w_in, v_b_forget, v_conv_w, v_conv_b, v_dt_bias, v_a_log, v_d_skip, v_ssm_norm_w, v_w_proj_attn, v_w_proj_ssm, v_b_gates, v_w_out, v_ln1_g, v_ln1_b, v_w_ffn_gate, v_w_ffn_up, v_w_ffn_down, v_ln2_g, v_ln2_b):
    S = x.shape[1]
    D = D_MODEL
    TM, TM2, TM3, TA, AQF, LC, CV, TS, TB = (min(TILES[k], S) for k in ("TM", "TM2", "TM3", "TA", "AQF", "LC", "CV", "TS", "TB"))
    xf = x[0]
    tgt = loss_target[0]
    xb = xf.astype(BF16)

    shards = [w_in[0].astype(BF16), conv_w[0], w_proj_attn[0].astype(BF16), w_proj_ssm[0].astype(BF16), w_out[0].astype(BF16),
              w_ffn_gate[0].astype(BF16), w_ffn_up[0].astype(BF16), w_ffn_down[0].astype(BF16)]
    chip = 2 * lax.axis_index("x") + lax.axis_index("y")
    own = lambda gathered, mine: [lax.dynamic_update_slice(g, sh[None], (chip, 0, 0)) for g, sh in zip(gathered, mine)]
    g_in, g_cw = own(_chip_gather("gather_w_in", shards[:2], [True, False]), shards[:2])
    later, gather_token = _chip_copies_start("gather_rest_start", shards[2:], False, g_cw)
    w_full = _from_chip_blocks_cols(g_in)
    w_re = jnp.concatenate([w_full[:, 0:3072], w_full[:, 3088:5136], w_full[:, 5136:8208], w_full[:, 8240:10288],
                            w_full[:, 3072:3088], w_full[:, 8208:8240], jnp.zeros((D, 80), BF16)], axis=1)
    conv_w_full = _from_chip_blocks_cols(g_cw)

    def plain(accs, rows, vecs, j):
        return [accs[0]], []

    def q_scaled(accs, rows, vecs, j):
        return [accs[0] * jnp.where(j * 512 < D, 1.0 / math.sqrt(ATT_HEAD_DIM), 1.0)], []

    qkv, = _mm("proj_qkv", S, 3072, TM, 512, [(xb, D, 0)], [(w_re, 0)], [(0, 0)], q_scaled, [(3072, BF16, 0)],
               after=[gather_token])
    z, = _mm("proj_z", S, 2048, TM, 512, [(xb, D, 0)], [(w_re, RE_Z // 512)], [(0, 0)], plain, [(2048, BF16, 0)])
    xbc_raw, = _mm("proj_xbc", S, 3072, TM, 512, [(xb, D, 0)], [(w_re, RE_XBC // 512)], [(0, 0)], plain, [(3072, F32, 0)])
    gl, = _mm("proj_gate", S, 2048, TM, 512, [(xb, D, 0)], [(w_re, RE_GATE // 512)], [(0, 0)], plain, [(2048, BF16, 0)])
    small, = _mm("proj_small", S, 128, TM, 128, [(xb, D, 0)], [(w_re, RE_SMALL // 128)], [(0, 0)], plain, [(128, F32, 0)])

    bvec = jnp.concatenate([b_forget, jnp.zeros((1, LANES - ATT_HEADS), F32)], axis=1)
    cum = _cum_fwd(small, bvec, TB)[:, :ATT_HEADS]
    cum_cols = cum.reshape(S, 8, 2).transpose(1, 0, 2)
    qa, ka = _attn_prep(qkv, cum_cols, TM)
    o, o32, lse_rows = _attn_fwd(qa, ka, qkv, AQF, TA)

    cb_row = conv_b
    xbc = _conv_fwd(xbc_raw, conv_w_full, cb_row, CV, 512)
    dt_raw = small[:, 16:48]
    dtc = dt_raw.reshape(S, SSM_GROUPS, 8).transpose(1, 0, 2)
    dtr = dt_raw.T.reshape(SSM_GROUPS, 8, S)
    bias_r = dt_bias.reshape(SSM_GROUPS, 1, 8)
    alog_b = jnp.repeat(a_log, ATT_HEAD_DIM, axis=1)
    dskip_b = jnp.repeat(d_skip, ATT_HEAD_DIM, axis=1)
    bias_c = dt_bias.reshape(SSM_GROUPS, 8, 1)
    alog_c = a_log.reshape(SSM_GROUPS, 8, 1)
    y_ssd, ssm, hs_all = _ssd_fwd(xbc, z, dtc, dtr, bias_r, alog_b, dskip_b, ssm_norm_w, bias_c, alog_c, LC)

    def merge(accs, rows, vecs, j):
        g0, g1 = _sigmoid(rows[0].astype(F32) + vecs[0]), _sigmoid(rows[1].astype(F32) + vecs[1])
        return [g0 * accs[0] + g1 * accs[1], accs[0], accs[1]], []

    g_pa, g_ps, g_out, g_fg, g_fu, g_fd = own(_chip_copies_wait("gather_rest_wait", later, False, o), shards[2:])
    wpa, wps, wout = g_pa.reshape(D, D), g_ps.reshape(SSM_INNER, D), g_out.reshape(D, D)
    wfg, wfu, wfd = _from_chip_blocks_cols(g_fg), _from_chip_blocks_cols(g_fu), g_fd.reshape(FFN_HIDDEN, D)
    mix, attn_d, ssm_d = _mm("merge", S, D, TM, 512, [(o, D, 0), (ssm, SSM_INNER, 0)], [(wpa, 0), (wps, 0)], [(0, 0), (1, 1)],
                             merge, [(D, BF16, 0), (D, BF16, 0), (D, BF16, 0)], rows=[(gl, 0), (gl, 2)],
                             vecs_n=[(b_gates, 0), (b_gates, 2)])

    def out_ln1(accs, rows, vecs, j):
        r1 = ALPHA * rows[0] + accs[0]
        h1, _, _ = _ln_fwd(r1, vecs[0], vecs[1])
        return [r1, h1, h1], []

    r1, h1, h1b = _mm("out_ln1", S, D, TM2, D, [(mix, D, 0)], [(wout, 0)], [(0, 0)], out_ln1,
                      [(D, F32, 0), (D, F32, 0), (D, BF16, 0)], rows=[(xf, 0)], vecs_n=[(ln1_g, 0), (ln1_b, 0)])

    FT = FFN_HIDDEN // 2

    def swiglu(accs, rows, vecs, j):
        g, u = accs
        return [g, u, g * _sigmoid(g) * u], []

    gate, up, hmid = _mm("ffn_up", S, FFN_HIDDEN, TM3, FT, [(h1b, D, 0)], [(wfg, 0), (wfu, 0)], [(0, 0), (0, 1)], swiglu,
                         [(FFN_HIDDEN, BF16, 0), (FFN_HIDDEN, BF16, 0), (FFN_HIDDEN, BF16, 0)])

    def down_ln2_loss(accs, rows, vecs, j):
        r2 = ALPHA * rows[0] + accs[0]
        yv, xhat, rstd = _ln_fwd(r2, vecs[0], vecs[1])
        diff = yv - rows[1]
        dy = diff * (1.0 / D_MODEL)
        dr2 = _ln_bwd(dy, xhat, rstd, vecs[0])
        return [dr2, dr2], [jnp.sum(dy * xhat, axis=0, keepdims=True), jnp.sum(dy, axis=0, keepdims=True),
                            (0.5 / D_MODEL) * jnp.sum(diff * diff, axis=0, keepdims=True)]

    dr2, dr2b, dln2_g, dln2_b, loss_lanes = _mm("ffn_down_ln2", S, D, TM3, D, [(hmid, FFN_HIDDEN, 0)], [(wfd, 0)], [(0, 0)],
                                               down_ln2_loss, [(D, F32, 0), (D, BF16, 0)], rows=[(h1, 0), (tgt, 0)],
                                               vecs_n=[(ln2_g, 0), (ln2_b, 0)], sums=[D, D, D])
    loss = lax.psum(jnp.sum(loss_lanes), ("x", "y", "c"))

    def dswiglu(accs, rows, vecs, j):
        g, u = rows[0].astype(F32), rows[1].astype(F32)
        sg = _sigmoid(g)
        return [accs[0] * u * sg * (1.0 + g * (1.0 - sg)), accs[0] * g * sg], []

    dgate, dup = _mm("ffn_down_bwd", S, FFN_HIDDEN, TM3, FT, [(dr2b, D, 0)], [(wfd, 0)], [(0, 0)], dswiglu,
                     [(FFN_HIDDEN, BF16, 0), (FFN_HIDDEN, BF16, 0)], nt=True, rows=[(gate, 0), (up, 0)])
    dwfd = _mm_tn("dw_ffn_down", hmid, dr2b, FFN_HIDDEN // 2, D, TS)
    dwfg = _mm_tn("dw_ffn_gate", h1b, dgate, D, FT, TS)
    dwfu = _mm_tn("dw_ffn_up", h1b, dup, D, FT, TS)
    core = lax.axis_index("c").astype(jnp.int32).reshape(1)

    def send_grads(tag, names_, blocks_, after_):
        theirs_ = _half_to_sibling("swap_halves_" + tag, blocks_)
        halves_ = [_half_sum("halfsum_" + nm, b, t, core, _row_tile(b.shape[1] // 2, b.shape[2], mult=16))
                   for nm, b, t in zip(names_, blocks_, theirs_)]
        started_, token_ = _chip_copies_start("scatter_" + tag + "_start", halves_, True, after_)
        return halves_, started_, token_

    ffn_names = ["w_ffn_gate", "w_ffn_up", "w_ffn_down"]
    ffn_halves, ffn_started, ffn_token = send_grads(
        "ffn", ffn_names, [_to_chip_blocks_cols(dwfg), _to_chip_blocks_cols(dwfu), dwfd.reshape(4, FFN_HIDDEN // 4, D)], dwfu)

    def dh1_ln1(accs, rows, vecs, j):
        dh1 = ALPHA * rows[0] + accs[0] + accs[1]
        _, xhat, rstd = _ln_fwd(rows[1], vecs[0], vecs[0])
        dr1 = _ln_bwd(dh1, xhat, rstd, vecs[0])
        return [dr1, dr1], [jnp.sum(dh1 * xhat, axis=0, keepdims=True), jnp.sum(dh1, axis=0, keepdims=True)]

    dr1, dr1b, dln1_g, dln1_b = _mm("ffn_up_bwd_ln1", S, D, TM2, D, [(dgate, FFN_HIDDEN, 0), (dup, FFN_HIDDEN, 0)],
                                    [(wfg, 0), (wfu, 0)], [(0, 0), (1, 1)], dh1_ln1, [(D, F32, 0), (D, BF16, 0)], nt=True,
                                    rows=[(dr2, 0), (r1, 0)], vecs_n=[(ln1_g, 0)], sums=[D, D], after=[ffn_token])

    def dmerge(accs, rows, vecs, j):
        dmix = accs[0]
        g0, g1 = _sigmoid(rows[0].astype(F32) + vecs[0]), _sigmoid(rows[1].astype(F32) + vecs[1])
        dgl0 = dmix * rows[2].astype(F32) * g0 * (1.0 - g0)
        dgl1 = dmix * rows[3].astype(F32) * g1 * (1.0 - g1)
        return [dmix * g0, dmix * g1, dgl0, dgl1], [jnp.sum(dgl0, axis=0, keepdims=True), jnp.sum(dgl1, axis=0, keepdims=True)]

    d_attn_d, d_ssm_d, dgl0, dgl1, dbg0, dbg1 = _mm(
        "out_bwd", S, D, TM, 512, [(dr1b, D, 0)], [(wout, 0)], [(0, 0)], dmerge, [(D, BF16, 0)] * 4, nt=True,
        rows=[(gl, 0), (gl, 2), (attn_d, 0), (ssm_d, 0)], vecs_n=[(b_gates, 0), (b_gates, 2)], sums=[D, D])
    dwout = _mm_tn("dw_out", mix, dr1b, D, D, TS)
    dwpa = _mm_tn("dw_proj_attn", o, d_attn_d, D, D, TS)
    dwps = _mm_tn("dw_proj_ssm", ssm, d_ssm_d, D, D, TS)
    mid_names = ["w_proj_attn", "w_proj_ssm", "w_out"]
    mid_halves, mid_started, mid_token = send_grads(
        "mid", mid_names, [dwpa.reshape(4, D // 4, D), dwps.reshape(4, SSM_INNER // 4, D), dwout.reshape(4, D // 4, D)], dwps)

    do, = _mm("proj_attn_bwd", S, D, TM, 512, [(d_attn_d, D, 0)], [(wpa, 0)], [(0, 0)], plain, [(D, BF16, 0)], nt=True,
              after=[mid_token])
    stats = _attn_stats(do, o32, lse_rows.transpose(0, 2, 1), AQF)
    dq, dk, dv, dck, dcq = _attn_bwd(qa, ka, qkv, do, stats, TA)

    def per_head(a):
        a = a.transpose(1, 0, 2).reshape(S, ATT_HEADS)
        return jnp.concatenate([a, jnp.zeros((S, LANES - ATT_HEADS), F32)], axis=1)

    dfl, dbf = _cum_bwd(per_head(dck.transpose(0, 2, 1)), per_head(dcq), small, bvec, TB)

    dssm, = _mm("proj_ssm_bwd", S, SSM_INNER, TM, 512, [(d_ssm_d, D, 0)], [(wps, 0)], [(0, 0)], plain, [(SSM_INNER, F32, 0)],
                nt=True)
    dxs, dbm, dcm, dz, ddt8, dnw, ddskip_b, dalog8, dbias8 = _ssd_bwd(
        xbc, z, y_ssd, dssm, hs_all, dtc, dtr, bias_r, a_log.reshape(SSM_GROUPS, 1, 8), alog_b, dskip_b, ssm_norm_w, bias_c,
        alog_c, LC)
    du_x, dcw_x, dcb_x = _conv_bwd("conv_bwd_x", xbc_raw, dxs, conv_w_full, cb_row, CV, 512, 0)
    du_b, dcw_b, dcb_b = _conv_bwd("conv_bwd_b", xbc_raw, dbm, conv_w_full, cb_row, CV, 512, SSM_INNER)
    du_c, dcw_c, dcb_c = _conv_bwd("conv_bwd_c", xbc_raw, dcm, conv_w_full, cb_row, CV, 512, SSM_INNER + SSM_GROUPS * SSM_STATE)
    dconv_w = jnp.concatenate([dcw_x, dcw_b, dcw_c], axis=1)
    dconv_b = jnp.concatenate([dcb_x, dcb_b, dcb_c], axis=1)
    ddt_raw = ddt8.transpose(1, 0, 2).reshape(S, SSM_HEADS)

    dsmall = jnp.concatenate([dfl[:, :ATT_HEADS], ddt_raw, jnp.zeros((S, 80), F32)], axis=1).astype(BF16)
    HB = SSM_GROUPS * SSM_STATE
    dw_q, dw_k, dw_v = (_mm_tn("dw_in_" + nm, xb, g_, D, D, TS) for nm, g_ in (("q", dq), ("k", dk), ("v", dv)))
    dw_z = _mm_tn("dw_in_z", xb, dz, D, D, TS)
    dw_xbc = jnp.concatenate([_mm_tn("dw_in_xs", xb, du_x, D, D, TS), _mm_tn("dw_in_b", xb, du_b, D, HB, TS),
                              _mm_tn("dw_in_c", xb, du_c, D, HB, TS)], axis=1)
    dw_g0, dw_g1 = _mm_tn("dw_in_g0", xb, dgl0, D, D, TS), _mm_tn("dw_in_g1", xb, dgl1, D, D, TS)
    dw_s = _mm_tn("dw_in_small", xb, dsmall, D, LANES, TS)
    dw_full = jnp.concatenate([dw_q, dw_k, dw_v, dw_s[:, 0:ATT_HEADS], dw_z, dw_xbc, dw_s[:, ATT_HEADS:ATT_HEADS + SSM_HEADS],
                               dw_g0, dw_g1], axis=1)

    in_halves, in_started, in_token = send_grads("in", ["w_in"], [_to_chip_blocks_cols(dw_full)], dw_full)
    def dx_first(accs, rows, vecs, j):
        return [ALPHA * rows[0] + sum(accs[1:], accs[0])], []

    def dx_more(accs, rows, vecs, j):
        return [rows[0] + sum(accs[1:], accs[0])], []

    wk = lambda col, width=D: (w_re, 0, col // width, width)
    dx_part, = _mm("dx_a", S, D, TM2, D, [(dq, D, 0), (dk, D, 0), (dv, D, 0), (dz, D, 0), (dz, D, 1)],
                   [wk(0), wk(1024), wk(2048), wk(RE_Z), wk(RE_Z + 1024)], [(k, k) for k in range(5)], dx_first,
                   [(D, F32, 0)], nt=True, rows=[(dr1, 0)], after=[in_token])
    grad_x, = _mm("dx_b", S, D, TM2, D,
                  [(du_x, D, 0), (du_x, D, 1), (du_b, HB, 0), (du_c, HB, 0), (dgl0, D, 0), (dgl1, D, 0), (dsmall, LANES, 0)],
                  [wk(RE_XBC), wk(RE_XBC + 1024), wk(RE_XBC + 2048, HB), wk(RE_XBC + 2048 + HB, HB), wk(RE_GATE),
                   wk(RE_GATE + 1024), wk(RE_SMALL, LANES)],
                  [(k, k) for k in range(7)], dx_more, [(D, F32, 0)], nt=True, rows=[(dx_part, 0)])
    names = ["w_in"] + mid_names + ffn_names
    halves = in_halves + mid_halves + ffn_halves
    stacks = (_chip_copies_wait("scatter_in_wait", in_started, True, grad_x)
              + _chip_copies_wait("scatter_mid_wait", mid_started, True, grad_x)
              + _chip_copies_wait("scatter_ffn_wait", ffn_started, True, grad_x))
    chip1 = chip.astype(jnp.int32).reshape(1)
    reduced = [_sum4("sum_" + nm, st, hv, chip1, _row_tile(st.shape[1], st.shape[2], mult=16))
               for nm, st, hv in zip(names, stacks, halves)]
    other = _sibling_swap("swap_reduced", reduced)
    big_w = [w_in, w_proj_attn, w_proj_ssm, w_out, w_ffn_gate, w_ffn_up, w_ffn_down]
    big_m = [m_w_in, m_w_proj_attn, m_w_proj_ssm, m_w_out, m_w_ffn_gate, m_w_ffn_up, m_w_ffn_down]
    big_v = [v_w_in, v_w_proj_attn, v_w_proj_ssm, v_w_out, v_w_ffn_gate, v_w_ffn_up, v_w_ffn_down]
    big = {}
    for nm, w_, m_, v_, mine, theirs in zip(names, big_w, big_m, big_v, reduced, other):
        big[nm] = _adamw_halves("adamw_" + nm, w_, m_, v_, mine, theirs, core, _row_tile(w_.shape[1] // 2, w_.shape[2]))

    dd_skip = ddskip_b.reshape(1, SSM_HEADS, ATT_HEAD_DIM).sum(axis=2)
    pieces = [dbf[:, :ATT_HEADS], dconv_w.reshape(1, SSM_CONV * SSM_CONV_DIM), dconv_b, dbias8.reshape(1, SSM_HEADS), dalog8.reshape(1, SSM_HEADS), dd_skip,
              dnw, dbg0, dbg1, dln1_g, dln1_b, dln2_g, dln2_b]
    widths = [p.shape[1] for p in pieces]
    total = sum(widths)
    P = -(-total // LANES) * LANES
    packed = jnp.concatenate(pieces + [jnp.zeros((1, P - total), F32)], axis=1)
    summed = _all_sum_small(packed)
    offs = [0]
    for wd in widths:
        offs.append(offs[-1] + wd)
    sm = [summed[:, offs[k]:offs[k + 1]] for k in range(len(pieces))]
    g_bf, g_cw_full, g_cb, g_dtb, g_al, g_ds, g_nw = sm[0], sm[1].reshape(SSM_CONV, SSM_CONV_DIM), sm[2], sm[3], sm[4], sm[5], sm[6]
    g_bg = jnp.concatenate([sm[7], sm[8]], axis=1)
    g_l1g, g_l1b, g_l2g, g_l2b = sm[9], sm[10], sm[11], sm[12]
    cshard = SSM_CONV_DIM // 4
    g_cw_shard = lax.dynamic_slice_in_dim(g_cw_full, chip * cshard, cshard, axis=1)
    small_names = ["b_forget", "conv_w", "conv_b", "dt_bias", "a_log", "d_skip", "ssm_norm_w", "b_gates", "ln1_g", "ln1_b",
                   "ln2_g", "ln2_b"]
    small_g = [g_bf, g_cw_shard.reshape(1, -1), g_cb, g_dtb, g_al, g_ds, g_nw, g_bg, g_l1g, g_l1b, g_l2g, g_l2b]
    small_w = [b_forget, conv_w[0].reshape(1, -1), conv_b, dt_bias, a_log, d_skip, ssm_norm_w, b_gates, ln1_g, ln1_b, ln2_g, ln2_b]
    small_m = [m_b_forget, m_conv_w[0].reshape(1, -1), m_conv_b, m_dt_bias, m_a_log, m_d_skip, m_ssm_norm_w, m_b_gates, m_ln1_g,
               m_ln1_b, m_ln2_g, m_ln2_b]
    small_v = [v_b_forget, v_conv_w[0].reshape(1, -1), v_conv_b, v_dt_bias, v_a_log, v_d_skip, v_ssm_norm_w, v_b_gates, v_ln1_g,
               v_ln1_b, v_ln2_g, v_ln2_b]
    sw = [a.shape[1] for a in small_w]
    stot = sum(sw)
    SP = -(-stot // LANES) * LANES

    def pack(parts):
        return jnp.concatenate(list(parts) + [jnp.zeros((1, SP - stot), F32)], axis=1).reshape(SP // LANES, LANES)

    sres = _adamw("adamw_small", pack(small_w), pack(small_m), pack(small_v), pack(small_g), SP // LANES)
    soffs = [0]
    for wd in sw:
        soffs.append(soffs[-1] + wd)
    smalls = {}
    for k, nm in enumerate(small_names):
        vals = [r.reshape(1, SP)[:, soffs[k]:soffs[k + 1]] for r in sres]
        if nm == "conv_w":
            vals = [v_.reshape(1, SSM_CONV, cshard) for v_ in vals]
        smalls[nm] = vals

    order = ["w_in", "b_forget", "conv_w", "conv_b", "dt_bias", "a_log", "d_skip", "ssm_norm_w", "w_proj_attn", "w_proj_ssm",
             "b_gates", "w_out", "ln1_g", "ln1_b", "w_ffn_gate", "w_ffn_up", "w_ffn_down", "ln2_g", "ln2_b"]
    allres = {**big, **smalls}
    outs = [loss, grad_x[None]]
    for idx in range(4):
        outs += [allres[nm][idx] for nm in order]
    return tuple(outs)
```

```python
import functools
import math

import jax
import jax.numpy as jnp
from jax import lax
from jax.experimental import pallas as pl
from jax.experimental.pallas import tpu as pltpu

F32, BF16 = jnp.float32, jnp.bfloat16
MESH = pl.DeviceIdType.MESH

D_MODEL = 1024
ATT_HEADS, ATT_HEAD_DIM = 16, 64
SSM_INNER, SSM_HEADS, SSM_GROUPS, SSM_STATE, SSM_CONV = 2048, 32, 4, 128, 4
SSM_CONV_DIM = SSM_INNER + 2 * SSM_GROUPS * SSM_STATE
GROUP_LANES = SSM_INNER // SSM_GROUPS
FFN_HIDDEN = 2816
ALPHA = 2.0 ** 0.25
LN_EPS = 1e-5
RMS_EPS = 1e-5
ADAM_LR, ADAM_B1, ADAM_B2, ADAM_EPS, ADAM_WD, ADAM_STEP = 0.001, 0.9, 0.999, 1e-08, 0.01, 10
IN_SIZES = (1024, 1024, 1024, 16, 2048, 3072, 32, 2048)
IN_WIDTH = sum(IN_SIZES)
RE_WIDTH = 3072 + 2048 + 3072 + 2048 + 128
RE_Z, RE_XBC, RE_GATE, RE_SMALL = 3072, 5120, 8192, 10240

LANES = 128
VMEM_CAP = 60 * 1024 * 1024
NEG = -1e30
TILES = dict(TM=1024, TM2=256, TM3=512, TA=512, AQF=2048, LC=256, CV=512, TS=2048, TB=256)


def _params(n_axes, vmem_bytes=None):
    return pltpu.CompilerParams(dimension_semantics=("arbitrary",) * n_axes,
                                vmem_limit_bytes=None if vmem_bytes is None else int(min(vmem_bytes, VMEM_CAP)))


def _sigmoid(v):
    return 1.0 / (1.0 + jnp.exp(-v))


def _softplus(v):
    return jnp.maximum(v, 0.0) + jnp.log(1.0 + jnp.exp(-jnp.abs(v)))


def _dot(a, b):
    return lax.dot_general(a, b, (((1,), (0,)), ((), ())), preferred_element_type=F32)


def _dot_nt(a, b):
    return lax.dot_general(a, b, (((1,), (1,)), ((), ())), preferred_element_type=F32)


def _dot_tn(a, b):
    return lax.dot_general(a, b, (((0,), (0,)), ((), ())), preferred_element_type=F32)


def _split3(v):
    hi = v.astype(BF16)
    r1 = v - hi.astype(F32)
    mid = r1.astype(BF16)
    lo = (r1 - mid.astype(F32)).astype(BF16)
    return hi, mid, lo


def _dot_exact_left(m01, v):
    hi, mid, lo = _split3(v)
    return _dot(m01, hi) + _dot(m01, mid) + _dot(m01, lo)


def _dot_exact_right(v, m01, terms=3):
    parts = _split3(v)[:terms]
    out = _dot(parts[0], m01)
    for p in parts[1:]:
        out = out + _dot(p, m01)
    return out


def _mm(name, M, N, tm, tn, lhs, rhs, pairs, e_fn, outs, *, nt=False, rows=(), vecs_n=(), sums=(), after=()):
    ni, nj = M // tm, N // tn
    assert ni * tm == M and nj * tn == N, (name, M, N, tm, tn)
    n_l, n_r, n_row, n_vn, n_o, n_s = len(lhs), len(rhs), len(rows), len(vecs_n), len(outs), len(sums)

    def body(*refs):
        pos = 0
        l_refs = refs[pos:pos + n_l]; pos += n_l
        r_refs = refs[pos:pos + n_r]; pos += n_r
        row_refs = refs[pos:pos + n_row]; pos += n_row
        vn_refs = refs[pos:pos + n_vn]; pos += n_vn + len(after)
        o_refs = refs[pos:pos + n_o]; pos += n_o
        s_refs = refs[pos:pos + n_s]; pos += n_s
        i, j = pl.program_id(0), pl.program_id(1)
        accs = []
        for li, ri in pairs:
            accs.append(_dot_nt(l_refs[li][...], r_refs[ri][...]) if nt else _dot(l_refs[li][...], r_refs[ri][...]))
        out_vals, sum_vals = e_fn(accs, [r[...] for r in row_refs], [r[...] for r in vn_refs], j)
        for r, v in zip(o_refs, out_vals):
            r[...] = v.astype(r.dtype)
        if n_s:
            col = pl.multiple_of(j * tn, LANES)

            @pl.when(i == 0)
            def _():
                for r, v in zip(s_refs, sum_vals):
                    r[:, pl.ds(col, tn)] = v

            @pl.when(i > 0)
            def _():
                for r, v in zip(s_refs, sum_vals):
                    r[:, pl.ds(col, tn)] += v

    in_specs, args, est = [], [], 0
    for arr, width, cb in lhs:
        in_specs.append(pl.BlockSpec((tm, width), lambda i, j, cb=cb: (i, cb)))
        args.append(arr); est += tm * width * arr.dtype.itemsize
    for arr, off, *ksub in rhs:
        if nt:
            kb, kw = ksub if ksub else (0, arr.shape[1])
            in_specs.append(pl.BlockSpec((tn, kw), lambda i, j, off=off, kb=kb: (j + off, kb)))
            est += tn * kw * arr.dtype.itemsize
        else:
            in_specs.append(pl.BlockSpec((arr.shape[0], tn), lambda i, j, off=off: (0, j + off)))
            est += tn * arr.shape[0] * arr.dtype.itemsize
        args.append(arr)
    for arr, off in rows:
        in_specs.append(pl.BlockSpec((tm, tn), lambda i, j, off=off: (i, j + off)))
        args.append(arr); est += tm * tn * arr.dtype.itemsize
    for arr, off in vecs_n:
        in_specs.append(pl.BlockSpec((1, tn), lambda i, j, off=off: (0, j + off)))
        args.append(arr); est += 8 * tn * 4
    for arr in after:
        in_specs.append(pl.BlockSpec(memory_space=pl.ANY))
        args.append(arr)
    out_shape, out_specs = [], []
    for total, dtype, off in outs:
        out_shape.append(jax.ShapeDtypeStruct((M, total), dtype))
        out_specs.append(pl.BlockSpec((tm, tn), lambda i, j, off=off: (i, j + off)))
        est += tm * tn * jnp.dtype(dtype).itemsize
    for total in sums:
        out_shape.append(jax.ShapeDtypeStruct((1, total), F32))
        out_specs.append(pl.BlockSpec((1, total), lambda i, j: (0, 0)))
        est += 8 * total * 4
    vmem = 2 * est + (len(pairs) + 4) * tm * tn * 4 + (8 << 20)
    return pl.pallas_call(body, name=name, grid=(ni, nj), in_specs=in_specs, out_specs=out_specs, out_shape=out_shape,
                          compiler_params=_params(2, vmem))(*args)


def _mm_tn(name, a, g, ta, tn, ts, a_cols=None, a_off=0):
    S = a.shape[0]
    Ka = a.shape[1] if a_cols is None else a_cols
    N = g.shape[1]
    assert Ka % ta == 0 and N % tn == 0 and S % ts == 0, (name, Ka, N, S)
    aoff = a_off // ta

    def body(a_ref, g_ref, o_ref):
        s = pl.program_id(2)
        part = _dot_tn(a_ref[...], g_ref[...])

        @pl.when(s == 0)
        def _():
            o_ref[...] = part

        @pl.when(s > 0)
        def _():
            o_ref[...] += part

    vmem = 2 * (ts * ta * 2 + ts * tn * 2 + ta * tn * 4) + 2 * ta * tn * 4 + (8 << 20)
    return pl.pallas_call(
        body, name=name, grid=(Ka // ta, N // tn, S // ts),
        in_specs=[pl.BlockSpec((ts, ta), lambda ia, jn, s: (s, ia + aoff)), pl.BlockSpec((ts, tn), lambda ia, jn, s: (s, jn))],
        out_specs=pl.BlockSpec((ta, tn), lambda ia, jn, s: (ia, jn)),
        out_shape=jax.ShapeDtypeStruct((Ka, N), F32), compiler_params=_params(3, vmem))(a, g)


def _tri(n, upper):
    r = lax.broadcasted_iota(jnp.int32, (n, n), 0)
    c = lax.broadcasted_iota(jnp.int32, (n, n), 1)
    return jnp.where((c >= r) if upper else (c <= r), 1.0, 0.0).astype(BF16)


def _logsig(v):
    return jnp.minimum(v, 0.0) - jnp.log(1.0 + jnp.exp(-jnp.abs(v)))


def _cum_fwd(small, bvec, tb):
    S = small.shape[0]

    def body(x_ref, b_ref, o_ref, carry):
        i = pl.program_id(0)

        @pl.when(i == 0)
        def _():
            carry[...] = jnp.zeros_like(carry)

        logf = _logsig(x_ref[...] + b_ref[...])
        cum = _dot_exact_left(_tri(tb, False), logf) + carry[0:1, :]
        o_ref[...] = cum
        carry[0:1, :] = cum[tb - 1:tb, :]

    return pl.pallas_call(
        body, name="cum_fwd", grid=(S // tb,),
        in_specs=[pl.BlockSpec((tb, LANES), lambda i: (i, 0)), pl.BlockSpec((1, LANES), lambda i: (0, 0))],
        out_specs=pl.BlockSpec((tb, LANES), lambda i: (i, 0)), out_shape=jax.ShapeDtypeStruct((S, LANES), F32),
        scratch_shapes=[pltpu.VMEM((8, LANES), F32)], compiler_params=_params(1))(small, bvec)


def _cum_bwd(dcum_k, dcum_q, small, bvec, tb):
    S = small.shape[0]
    nb = S // tb

    def body(dk_ref, dq_ref, x_ref, b_ref, o_ref, s_ref, carry):
        i = pl.program_id(0)

        @pl.when(i == 0)
        def _():
            carry[...] = jnp.zeros_like(carry)
            s_ref[...] = jnp.zeros_like(s_ref)

        rc = _dot_exact_left(_tri(tb, True), dk_ref[...] + dq_ref[...]) + carry[0:1, :]
        dfl = rc * _sigmoid(-(x_ref[...] + b_ref[...]))
        o_ref[...] = dfl
        s_ref[...] += jnp.sum(dfl, axis=0, keepdims=True)
        carry[0:1, :] = rc[0:1, :]

    rev = lambda i: (nb - 1 - i, 0)
    return pl.pallas_call(
        body, name="cum_bwd", grid=(nb,),
        in_specs=[pl.BlockSpec((tb, LANES), rev)] * 3 + [pl.BlockSpec((1, LANES), lambda i: (0, 0))],
        out_specs=[pl.BlockSpec((tb, LANES), rev), pl.BlockSpec((1, LANES), lambda i: (0, 0))],
        out_shape=[jax.ShapeDtypeStruct((S, LANES), F32), jax.ShapeDtypeStruct((1, LANES), F32)],
        scratch_shapes=[pltpu.VMEM((8, LANES), F32)], compiler_params=_params(1))(dcum_k, dcum_q, small, bvec)


N_AUG = 3


def _lane():
    return lax.broadcasted_iota(jnp.int32, (1, LANES), 1)


def _lane_mask():
    return _lane() < ATT_HEAD_DIM


def _aug_base(h):
    return ATT_HEAD_DIM * (1 - h)


def _attn_prep(qkv, cum_cols, T):
    S = qkv.shape[0]
    HP = ATT_HEADS // 2

    def body(q_ref, k_ref, c_ref, qa_ref, ka_ref):
        lane = _lane()
        q = q_ref[...]
        k = k_ref[...]
        one, zero = jnp.ones_like(q), jnp.zeros_like(q)
        for h in (0, 1):
            base = _aug_base(h)
            own = (lane < ATT_HEAD_DIM) if h == 0 else (lane >= ATT_HEAD_DIM)
            term_lanes = (lane >= base) & (lane < base + N_AUG)
            terms = [t.astype(F32) for t in _split3(c_ref[0, :, h:h + 1])]
            neg = jnp.where(lane == base, -terms[0], jnp.where(lane == base + 1, -terms[1], -terms[2])).astype(BF16)
            qa_ref[:, h * LANES:(h + 1) * LANES] = jnp.where(lane == base + N_AUG, zero, jnp.where(term_lanes, one, q))
            ka_ref[:, h * LANES:(h + 1) * LANES] = jnp.where(term_lanes, neg, jnp.where(lane == base + N_AUG, one,
                                                                                         jnp.where(own, k, zero)))

    return pl.pallas_call(
        body, name="attn_prep", grid=(S // T, HP),
        in_specs=[pl.BlockSpec((T, LANES), lambda i, hp: (i, hp)), pl.BlockSpec((T, LANES), lambda i, hp: (i, HP + hp)),
                  pl.BlockSpec((1, T, 2), lambda i, hp: (hp, i, 0))],
        out_specs=[pl.BlockSpec((T, 2 * LANES), lambda i, hp: (i, hp))] * 2,
        out_shape=[jax.ShapeDtypeStruct((S, 2 * D_MODEL), BF16)] * 2, compiler_params=_params(2))(qkv, qkv, cum_cols)


def _attn_fwd(qa, ka, qkv, T, TK):
    S = qkv.shape[0]
    nq = S // T
    r = T // TK
    HP = ATT_HEADS // 2

    def body(q0_ref, q1_ref, k0_ref, k1_ref, v_ref, o_ref, o32_ref, lse_ref):
        i = pl.program_id(1)
        qs = (q0_ref[...], q1_ref[...])
        k_refs = (k0_ref, k1_ref)
        row = lax.broadcasted_iota(jnp.int32, (TK, T), 0)
        col = lax.broadcasted_iota(jnp.int32, (TK, T), 1)
        head_rows = lax.broadcasted_iota(jnp.int32, (LANES, 1), 0) < ATT_HEAD_DIM

        def block(j, carry, q0):
            off = pl.multiple_of(j * TK, TK)
            vj = v_ref[pl.ds(off, TK), :]
            full = q0 is None
            q0 = 0 if full else q0
            m0, l0, m1, l1, acc = carry
            new, alphas, pvs = [], [], []
            for h, (m, l) in enumerate(((m0, l0), (m1, l1))):
                st = _dot_nt(k_refs[h][pl.ds(off, TK), :], qs[h][q0:, :])
                if not full:
                    st = jnp.where(row[:, :T - q0] <= col[:, :T - q0], st, NEG)
                m_old, l_old = m[:, q0:], l[:, q0:]
                m_new = jnp.maximum(m_old, jnp.max(st, axis=0, keepdims=True))
                p = jnp.exp(st - m_new)
                alpha = jnp.exp(m_old - m_new)
                l_new = alpha * l_old + jnp.sum(p, axis=0, keepdims=True)
                pvs.append(_dot_tn(vj, p.astype(BF16)))
                alphas.append(alpha)
                new += [m_new, l_new]
            part = acc[:, q0:] * jnp.where(head_rows, alphas[0], alphas[1]) + jnp.where(head_rows, pvs[0], pvs[1])
            if q0:
                keep = lambda old, upd: jnp.concatenate([old[:, :q0], upd], axis=1)
                return (keep(m0, new[0]), keep(l0, new[1]), keep(m1, new[2]), keep(l1, new[3]), keep(acc, part))
            return (new[0], new[1], new[2], new[3], part)

        init = (jnp.full((1, T), NEG, F32), jnp.zeros((1, T), F32), jnp.full((1, T), NEG, F32), jnp.zeros((1, T), F32),
                jnp.zeros((LANES, T), F32))
        n_full = i * r
        carry = lax.fori_loop(0, n_full // 2, lambda jj, c: block(2 * jj + 1, block(2 * jj, c, None), None), init)
        carry = lax.cond(n_full % 2 == 1, lambda c: block(n_full - 1, c, None), lambda c: c, carry)
        for d in range(r):
            carry = block(n_full + d, carry, d * TK)
        m0, l0, m1, l1, acc = carry
        out = (acc / jnp.where(head_rows, l0, l1)).T
        o_ref[...] = out.astype(BF16)
        o32_ref[...] = out
        lse_ref[0, 0:1, :] = m0 + jnp.log(l0)
        lse_ref[0, 1:2, :] = m1 + jnp.log(l1)

    vmem = 2 * (2 * T * LANES * 2 + 3 * S * LANES * 2 + T * LANES * (2 + 4) + 8 * T * 4) + 10 * T * TK * 4 + (8 << 20)
    qspec = lambda h: pl.BlockSpec((T, LANES), lambda hp, i, h=h: (i, 2 * hp + h))
    kspec = lambda h: pl.BlockSpec((S, LANES), lambda hp, i, h=h: (0, 2 * hp + h))
    return pl.pallas_call(
        body, name="attn_fwd", grid=(HP, nq),
        in_specs=[qspec(0), qspec(1), kspec(0), kspec(1), pl.BlockSpec((S, LANES), lambda hp, i: (0, 2 * HP + hp))],
        out_specs=[pl.BlockSpec((T, LANES), lambda hp, i: (i, hp)), pl.BlockSpec((T, LANES), lambda hp, i: (i, hp)),
                   pl.BlockSpec((1, 2, T), lambda hp, i: (hp, 0, i))],
        out_shape=[jax.ShapeDtypeStruct((S, D_MODEL), BF16), jax.ShapeDtypeStruct((S, D_MODEL), F32),
                   jax.ShapeDtypeStruct((HP, 2, S), F32)],
        compiler_params=_params(2, vmem))(qa, qa, ka, ka, qkv)


def _attn_stats(do, o32, lse_cols, T):
    S = do.shape[0]
    HP = ATT_HEADS // 2

    def body(do_ref, o_ref, lse_ref, st_ref):
        lane = lax.broadcasted_iota(jnp.int32, (LANES, 8), 0)
        c = lax.broadcasted_iota(jnp.int32, (LANES, 8), 1)
        sel = jnp.where(((c == 2) & (lane < ATT_HEAD_DIM)) | ((c == 3) & (lane >= ATT_HEAD_DIM)), 1.0, 0.0).astype(BF16)
        dd = _dot_exact_right(do_ref[...].astype(F32) * o_ref[...], sel)
        c8 = lax.broadcasted_iota(jnp.int32, (1, 8), 1)
        st_ref[0] = jnp.where(c8 == 0, lse_ref[0, :, 0:1], jnp.where(c8 == 1, lse_ref[0, :, 1:2], dd))

    return pl.pallas_call(
        body, name="attn_stats", grid=(HP, S // T),
        in_specs=[pl.BlockSpec((T, LANES), lambda hp, i: (i, hp)), pl.BlockSpec((T, LANES), lambda hp, i: (i, hp)),
                  pl.BlockSpec((1, T, 2), lambda hp, i: (hp, i, 0))],
        out_specs=pl.BlockSpec((1, T, 8), lambda hp, i: (hp, i, 0)), out_shape=jax.ShapeDtypeStruct((HP, S, 8), F32),
        compiler_params=_params(2))(do, o32, lse_cols)


def _attn_bwd(qa, ka, qkv, do, stats, T):
    S = qkv.shape[0]
    nq = S // T
    HP = ATT_HEADS // 2

    def body(k0_ref, k1_ref, v_ref, q0_ref, q1_ref, do_ref, st_ref, dq_ref, dk_ref, dv_ref, dck_ref, dcq_ref, dq_acc):
        j = pl.program_id(1)
        mA = _lane_mask()
        masks = (mA, jnp.logical_not(mA))
        q_refs = (q0_ref, q1_ref)

        @pl.when(j == 0)
        def _():
            dq_acc[...] = jnp.zeros_like(dq_acc)

        kas = (k0_ref[...], k1_ref[...])
        vj = v_ref[...]
        row = lax.broadcasted_iota(jnp.int32, (T, T), 0)
        col = lax.broadcasted_iota(jnp.int32, (T, T), 1)

        def block(i, carry, diag):
            dvt, dkt0, dkt1 = carry
            off = pl.multiple_of(i * T, T)
            doi = do_ref[pl.ds(off, T), :]
            zero = jnp.zeros_like(doi)
            dkts = [dkt0, dkt1]
            for h in (0, 1):
                qh = q_refs[h][pl.ds(off, T), :]
                doh = jnp.where(masks[h], doi, zero)
                lse = st_ref[0, pl.ds(off, T), h:h + 1]
                dd = st_ref[0, pl.ds(off, T), 2 + h:3 + h]
                sc = _dot_nt(qh, kas[h])
                if diag:
                    sc = jnp.where(row >= col, sc, NEG)
                p = jnp.exp(sc - lse)
                dp = _dot_nt(doh, vj)
                ds = (p * (dp - dd)).astype(BF16)
                dvt = dvt + _dot_tn(doh, p.astype(BF16))
                dkts[h] = dkts[h] + _dot_tn(qh, ds)
                dq_acc[h, pl.ds(off, T), :] += _dot(ds, kas[h])
            return (dvt, dkts[0], dkts[1])

        z = jnp.zeros((LANES, T), F32)
        carry = block(j, (z, z, z), True)
        dvt, dkt0, dkt1 = lax.fori_loop(j + 1, nq, lambda i, c: block(i, c, False), carry)
        dv_ref[...] = dvt.T.astype(BF16)
        dk_ref[...] = jnp.where(mA, dkt0.T, dkt1.T).astype(BF16)
        ones_q = (_aug_base(0), _aug_base(1))
        dck_ref[0, 0:1, :] = -dkt0[ones_q[0]:ones_q[0] + 1, :]
        dck_ref[0, 1:2, :] = -dkt1[ones_q[1]:ones_q[1] + 1, :]

        @pl.when(j == nq - 1)
        def _():
            dq0, dq1 = dq_acc[0], dq_acc[1]
            ones_k = (_aug_base(0) + N_AUG, _aug_base(1) + N_AUG)
            dq_ref[...] = (jnp.where(mA, dq0, dq1) * (1.0 / math.sqrt(ATT_HEAD_DIM))).astype(BF16)
            dcq_ref[0, :, 0:1] = dq0[:, ones_k[0]:ones_k[0] + 1]
            dcq_ref[0, :, 1:2] = dq1[:, ones_k[1]:ones_k[1] + 1]

    vmem = (2 * (3 * T * LANES * 2 + 3 * S * LANES * 2 + S * LANES * 4 + S * LANES * (2 + 4) + 2 * T * LANES * 2 + 8 * T * 4)
            + 2 * S * LANES * 4 + 12 * T * T * 4 + (8 << 20))
    kspec = lambda h: pl.BlockSpec((T, LANES), lambda hp, j, h=h: (j, 2 * hp + h))
    qspec = lambda h: pl.BlockSpec((S, LANES), lambda hp, j, h=h: (0, 2 * hp + h))
    blk = pl.BlockSpec((T, LANES), lambda hp, j: (j, hp))
    full = pl.BlockSpec((S, LANES), lambda hp, j: (0, hp))
    return pl.pallas_call(
        body, name="attn_bwd", grid=(HP, nq),
        in_specs=[kspec(0), kspec(1), pl.BlockSpec((T, LANES), lambda hp, j: (j, 2 * HP + hp)), qspec(0), qspec(1), full,
                  pl.BlockSpec((1, S, 8), lambda hp, j: (hp, 0, 0))],
        out_specs=[full, blk, blk, pl.BlockSpec((1, 2, T), lambda hp, j: (hp, 0, j)),
                   pl.BlockSpec((1, S, 2), lambda hp, j: (hp, 0, 0))],
        out_shape=[jax.ShapeDtypeStruct((S, D_MODEL), BF16)] * 3 + [jax.ShapeDtypeStruct((HP, 2, S), F32),
                                                                     jax.ShapeDtypeStruct((HP, S, 2), F32)],
        scratch_shapes=[pltpu.VMEM((2, S, LANES), F32)], compiler_params=_params(2, vmem))(ka, ka, qkv, qa, qa, do, stats)


HALO = 8


def _shift_down(x, d, above):
    r = pltpu.roll(x, d, 0)
    head = jnp.where(lax.broadcasted_iota(jnp.int32, (HALO, 1), 0) < d, pltpu.roll(above, d, 0), r[0:HALO])
    return head if x.shape[0] == HALO else jnp.concatenate([head, r[HALO:]], axis=0)


def _shift_up(x, d, below):
    n = x.shape[0]
    r = pltpu.roll(x, n - d, 0)
    tail = jnp.where(lax.broadcasted_iota(jnp.int32, (HALO, 1), 0) >= HALO - d, pltpu.roll(below, HALO - d, 0), r[n - HALO:])
    return jnp.concatenate([r[:n - HALO], tail], axis=0)


def _conv_fwd(u, w, b, ts, tc):
    S, C = u.shape
    hb = ts // HALO

    def body(u_ref, prev_ref, w_ref, b_ref, o_ref):
        i = pl.program_id(0)
        x = u_ref[...]
        above = jnp.where(i == 0, 0.0, prev_ref[...])
        acc = b_ref[...] + w_ref[3:4, :] * x
        for k in range(SSM_CONV - 1):
            acc = acc + w_ref[k:k + 1, :] * _shift_down(x, SSM_CONV - 1 - k, above)
        o_ref[...] = acc * _sigmoid(acc)

    return pl.pallas_call(
        body, name="conv_fwd", grid=(S // ts, C // tc),
        in_specs=[pl.BlockSpec((ts, tc), lambda i, j: (i, j)),
                  pl.BlockSpec((HALO, tc), lambda i, j: (jnp.maximum(i * hb - 1, 0), j)),
                  pl.BlockSpec((SSM_CONV, tc), lambda i, j: (0, j)), pl.BlockSpec((1, tc), lambda i, j: (0, j))],
        out_specs=pl.BlockSpec((ts, tc), lambda i, j: (i, j)), out_shape=jax.ShapeDtypeStruct((S, C), F32),
        compiler_params=_params(2))(u, u, w, b)


def _conv_bwd(name, u, dy, w, b, ts, tc, col0):
    S, C = dy.shape
    cb = col0 // tc
    assert cb * tc == col0
    hb = ts // HALO
    nb = S // ts

    def body(u_ref, uprev_ref, unext_ref, dy_ref, dynext_ref, w_ref, b_ref, du_ref, dw_ref, db_ref):
        i = pl.program_id(1)
        x = u_ref[...]
        above = jnp.where(i == 0, 0.0, uprev_ref[...])
        ws = [w_ref[k:k + 1, :] for k in range(SSM_CONV)]

        def dsilu(pre):
            sg = _sigmoid(pre)
            return sg * (1.0 + pre * (1.0 - sg))

        shifted = [_shift_down(x, SSM_CONV - 1 - k, above) for k in range(SSM_CONV - 1)] + [x]
        pre = b_ref[...]
        for k in range(SSM_CONV):
            pre = pre + ws[k] * shifted[k]
        g = dy_ref[...] * dsilu(pre)
        nxt = unext_ref[...]
        tail = x[ts - HALO:, :]
        pre_n = b_ref[...] + ws[SSM_CONV - 1] * nxt
        for k in range(SSM_CONV - 1):
            pre_n = pre_n + ws[k] * _shift_down(nxt, SSM_CONV - 1 - k, tail)
        g_next = jnp.where(i == nb - 1, 0.0, dynext_ref[...] * dsilu(pre_n))
        du = ws[SSM_CONV - 1] * g
        for k in range(SSM_CONV - 1):
            du = du + ws[k] * _shift_up(g, SSM_CONV - 1 - k, g_next)
        du_ref[...] = du.astype(du_ref.dtype)
        dws = [jnp.sum(g * shifted[k], axis=0, keepdims=True) for k in range(SSM_CONV)]
        dbs = jnp.sum(g, axis=0, keepdims=True)

        @pl.when(i == 0)
        def _():
            for k in range(SSM_CONV):
                dw_ref[k:k + 1, :] = dws[k]
            db_ref[...] = dbs

        @pl.when(i > 0)
        def _():
            for k in range(SSM_CONV):
                dw_ref[k:k + 1, :] += dws[k]
            db_ref[...] += dbs

    nxt = lambda off: (lambda j, i: (jnp.minimum((i + 1) * hb, S // HALO - 1), j + off))
    return pl.pallas_call(
        body, name=name, grid=(C // tc, nb),
        in_specs=[pl.BlockSpec((ts, tc), lambda j, i: (i, j + cb)),
                  pl.BlockSpec((HALO, tc), lambda j, i: (jnp.maximum(i * hb - 1, 0), j + cb)),
                  pl.BlockSpec((HALO, tc), nxt(cb)),
                  pl.BlockSpec((ts, tc), lambda j, i: (i, j)),
                  pl.BlockSpec((HALO, tc), nxt(0)),
                  pl.BlockSpec((SSM_CONV, tc), lambda j, i: (0, j + cb)), pl.BlockSpec((1, tc), lambda j, i: (0, j + cb))],
        out_specs=[pl.BlockSpec((ts, tc), lambda j, i: (i, j)), pl.BlockSpec((SSM_CONV, tc), lambda j, i: (0, j)),
                   pl.BlockSpec((1, tc), lambda j, i: (0, j))],
        out_shape=[jax.ShapeDtypeStruct((S, C), BF16), jax.ShapeDtypeStruct((SSM_CONV, C), F32), jax.ShapeDtypeStruct((1, C), F32)],
        compiler_params=_params(2))(u, u, u, dy, dy, w, b)


def _head_sum():
    lane = jnp.right_shift(lax.broadcasted_iota(jnp.int32, (GROUP_LANES, 8), 0), 6)
    r = lax.broadcasted_iota(jnp.int32, (GROUP_LANES, 8), 1)
    return jnp.where(lane == r, 1.0, 0.0).astype(BF16)


def _head_expand():
    r = lax.broadcasted_iota(jnp.int32, (8, GROUP_LANES), 0)
    c = jnp.right_shift(lax.broadcasted_iota(jnp.int32, (8, GROUP_LANES), 1), 6)
    return jnp.where(r == c, 1.0, 0.0).astype(BF16)


def _ssd_common(dtc_ref, dtr_ref, bias_r, alog_b, bias_c, alog_c, L):
    a_b = -jnp.exp(alog_b)
    dt = _dot_exact_right(_softplus(dtc_ref[0] + bias_r), _head_expand())
    acum = _dot_exact_left(_tri(L, False), dt * a_b)
    a_c = -jnp.exp(alog_c)
    dtr = _softplus(dtr_ref[0] + bias_c)
    acum_r = _dot_exact_right(dtr * a_c, _tri(L, True))
    return a_b, dt, acum, acum_r


def _ssd_specs(L, nc, rev):
    cc = (lambda c: nc - 1 - c) if rev else (lambda c: c)
    G = SSM_GROUPS
    blk = pl.BlockSpec((L, GROUP_LANES), lambda g, c: (cc(c), g))
    dtc = pl.BlockSpec((1, L, 8), lambda g, c: (g, cc(c), 0))
    rowv = pl.BlockSpec((1, 1, 8), lambda g, c: (g, 0, 0))
    xs = blk
    bm = pl.BlockSpec((L, SSM_STATE), lambda g, c: (cc(c), SSM_INNER // SSM_STATE + g))
    cm = pl.BlockSpec((L, SSM_STATE), lambda g, c: (cc(c), SSM_INNER // SSM_STATE + G + g))
    dtr = pl.BlockSpec((1, 8, L), lambda g, c: (g, 0, cc(c)))
    vec = pl.BlockSpec((1, GROUP_LANES), lambda g, c: (0, g))
    colv = pl.BlockSpec((1, 8, 1), lambda g, c: (g, 0, 0))
    hs = pl.BlockSpec((1, 1, SSM_STATE, GROUP_LANES), lambda g, c: (g, cc(c), 0, 0))
    return blk, xs, bm, cm, dtc, dtr, vec, rowv, colv, hs


def _ssd_fwd(xbc, z, dtc, dtr, bias_r, alog_b, dskip_b, normw, bias_c, alog_c, L):
    S = z.shape[0]
    nc = S // L
    blk, xs, bm, cm, dtcs, dtrs, vec, rowv, colv, hs = _ssd_specs(L, nc, False)

    def body(x_ref, b_ref, c_ref, z_ref, dtc_ref, dtr_ref, bias_ref, alog_ref, dskip_ref, nw_ref, biasc_ref, alogc_ref,
             y_ref, ssm_ref, hs_ref, h_scr):
        c = pl.program_id(1)

        @pl.when(c == 0)
        def _():
            h_scr[...] = jnp.zeros_like(h_scr)

        mA = _lane_mask()
        a_b, dt, acum, acum_r = _ssd_common(dtc_ref, dtr_ref, bias_ref[0], alog_ref[...], biasc_ref[0], alogc_ref[0], L)
        x = x_ref[...]
        cb, bb = c_ref[...].astype(BF16), b_ref[...].astype(BF16)
        hprev = h_scr[...]
        hs_ref[0, 0] = hprev
        xdt = x * dt
        xdt_b = xdt.astype(BF16)
        gmat = _dot_nt(cb, bb)
        row = lax.broadcasted_iota(jnp.int32, (L, L), 0)
        col = lax.broadcasted_iota(jnp.int32, (L, L), 1)
        parts = []
        for p in range(GROUP_LANES // LANES):
            xp = xdt_b[:, p * LANES:(p + 1) * LANES]
            yd = []
            for hh in (0, 1):
                r = 2 * p + hh
                acol = acum[:, r * ATT_HEAD_DIM:r * ATT_HEAD_DIM + 1]
                arow = acum_r[r:r + 1, :]
                lm = jnp.exp(jnp.where(row >= col, acol - arow, NEG))
                yd.append(_dot((gmat * lm).astype(BF16), xp))
            parts.append(jnp.where(mA, yd[0], yd[1]))
        ydiag = jnp.concatenate(parts, axis=1)
        yoff = jnp.exp(acum) * _dot(cb, hprev.astype(BF16))
        y = ydiag + yoff + dskip_ref[...] * x
        aend = acum[L - 1:L, :]
        wgt = (jnp.exp(aend - acum) * xdt).astype(BF16)
        h_scr[...] = jnp.exp(aend) * hprev + _dot_tn(bb, wgt)
        y_ref[...] = y
        zz = z_ref[...].astype(F32)
        u = y * (zz * _sigmoid(zz))
        rs = lax.rsqrt(jnp.mean(u * u, axis=1, keepdims=True) + RMS_EPS)
        ssm_ref[...] = (u * rs * nw_ref[...]).astype(BF16)

    return pl.pallas_call(
        body, name="ssd_fwd", grid=(SSM_GROUPS, nc),
        in_specs=[xs, bm, cm, blk, dtcs, dtrs, rowv, vec, vec, vec, colv, colv],
        out_specs=[blk, blk, hs],
        out_shape=[jax.ShapeDtypeStruct((S, SSM_INNER), F32), jax.ShapeDtypeStruct((S, SSM_INNER), BF16),
                   jax.ShapeDtypeStruct((SSM_GROUPS, nc, SSM_STATE, GROUP_LANES), F32)],
        scratch_shapes=[pltpu.VMEM((SSM_STATE, GROUP_LANES), F32)],
        compiler_params=_params(2, 48 << 20))(xbc, xbc, xbc, z, dtc, dtr, bias_r, alog_b, dskip_b, normw, bias_c, alog_c)


def _ssd_bwd(xbc, z, y, dssm, hs_all, dtc, dtr, bias_r, alog_r, alog_b, dskip_b, normw, bias_c, alog_c, L):
    S = z.shape[0]
    nc = S // L
    blk, xs, bm, cm, dtcs, dtrs, vec, rowv, colv, hs = _ssd_specs(L, nc, True)

    def body(x_ref, b_ref, c_ref, z_ref, y_ref, dssm_ref, hs_ref, dtc_ref, dtr_ref, bias_ref, alogr_ref, alog_ref, dskip_ref, nw_ref,
             biasc_ref, alogc_ref,
             dx_ref, db_ref, dc_ref, dz_ref, ddt_ref, dnw_ref, ddskip_ref, dalog_ref, dbias_ref, dh_scr):
        c = pl.program_id(1)

        @pl.when(c == 0)
        def _():
            dh_scr[...] = jnp.zeros_like(dh_scr)

        mA = _lane_mask()
        masks = (mA, jnp.logical_not(mA))
        a_b, dt, acum, acum_r = _ssd_common(dtc_ref, dtr_ref, bias_ref[0], alog_ref[...], biasc_ref[0], alogc_ref[0], L)
        x, zz, y, dssm = x_ref[...], z_ref[...].astype(F32), y_ref[...], dssm_ref[...]
        cb, bb = c_ref[...].astype(BF16), b_ref[...].astype(BF16)
        hprev = hs_ref[0, 0]
        hb = hprev.astype(BF16)
        ds = dh_scr[...]
        dsb = ds.astype(BF16)
        dskip = dskip_ref[...]
        aend = acum[L - 1:L, :]
        e_a, e_end = jnp.exp(acum), jnp.exp(aend)
        dte = jnp.exp(aend - acum)
        xdt = x * dt
        xdt_b = xdt.astype(BF16)
        sg = _sigmoid(zz)
        sz = zz * sg
        u = y * sz
        rs = lax.rsqrt(jnp.mean(u * u, axis=1, keepdims=True) + RMS_EPS)
        un = u * rs
        dun = dssm * nw_ref[...]
        du = rs * (dun - un * jnp.mean(dun * un, axis=1, keepdims=True))
        dy = du * sz
        dz_ref[...] = (du * y * sg * (1.0 + zz * (1.0 - sg))).astype(dz_ref.dtype)
        dy_b = dy.astype(BF16)
        dch_b = (dy * e_a).astype(BF16)
        dc = _dot_nt(dch_b, hb)
        dhprev = _dot_tn(cb, dch_b)
        gt = _dot_nt(bb, cb)
        row = lax.broadcasted_iota(jnp.int32, (L, L), 0)
        col = lax.broadcasted_iota(jnp.int32, (L, L), 1)
        dgt = jnp.zeros((L, L), F32)
        parts = []
        for p in range(GROUP_LANES // LANES):
            xp = xdt_b[:, p * LANES:(p + 1) * LANES]
            dyp = dy_b[:, p * LANES:(p + 1) * LANES]
            zero = jnp.zeros_like(dyp)
            acc = None
            for hh in (0, 1):
                r = 2 * p + hh
                acol = acum[:, r * ATT_HEAD_DIM:r * ATT_HEAD_DIM + 1]
                arow = acum_r[r:r + 1, :]
                lmt = jnp.exp(jnp.where(row <= col, arow - acol, NEG))
                dyh = jnp.where(masks[hh], dyp, zero)
                part = _dot((gt * lmt).astype(BF16), dyh)
                acc = part if acc is None else acc + part
                dgt = dgt + _dot_nt(xp, dyh) * lmt
            parts.append(acc)
        dxdt_diag = jnp.concatenate(parts, axis=1)
        dgt_b = dgt.astype(BF16)
        db = _dot(dgt_b, cb)
        dc = dc + _dot_tn(dgt_b, bb)
        dxdt_state = dte * _dot(bb, dsb)
        db = db + _dot_nt((dte * xdt).astype(BF16), dsb)
        dxdt = dxdt_diag + dxdt_state
        dy_r, xdt_r = dy_b.astype(F32), xdt_b.astype(F32)
        dac = dy_r * (y - dskip * x) - xdt_r * dxdt
        tail = jnp.sum(xdt_r * dxdt_state, axis=0, keepdims=True) + e_end * jnp.sum(ds * hprev, axis=0, keepdims=True)
        rowl = lax.broadcasted_iota(jnp.int32, (L, 1), 0)
        dac = dac + jnp.where(rowl == L - 1, tail, 0.0)
        rc = _dot_exact_left(_tri(L, True), dac)
        hsum = _head_sum()
        hs1 = _dot_exact_right(dxdt * x, hsum, 2)
        hs2 = _dot_exact_right(rc, hsum, 2)
        a8 = -jnp.exp(alogr_ref[0])
        dtraw8 = dtc_ref[0] + bias_ref[0]
        ddtraw = (hs1 + a8 * hs2) * _sigmoid(dtraw8)
        dx_ref[...] = dskip * dy + dxdt * dt
        db_ref[...] = db
        dc_ref[...] = dc
        ddt_ref[0] = ddtraw
        dh_scr[...] = e_end * ds + dhprev
        sums = (jnp.sum(dssm * un, axis=0, keepdims=True), jnp.sum(dy * x, axis=0, keepdims=True))
        refs = (dnw_ref, ddskip_ref)
        sums8 = (a8 * jnp.sum(hs2 * _softplus(dtraw8), axis=0, keepdims=True), jnp.sum(ddtraw, axis=0, keepdims=True))
        refs8 = (dalog_ref, dbias_ref)

        @pl.when(c == 0)
        def _():
            for r, v in zip(refs, sums):
                r[...] = v
            for r, v in zip(refs8, sums8):
                r[0] = v

        @pl.when(c > 0)
        def _():
            for r, v in zip(refs, sums):
                r[...] += v
            for r, v in zip(refs8, sums8):
                r[0] += v

    nbc = pl.BlockSpec((L, SSM_STATE), lambda g, c: (nc - 1 - c, g))
    return pl.pallas_call(
        body, name="ssd_bwd", grid=(SSM_GROUPS, nc),
        in_specs=[xs, bm, cm, blk, blk, blk, hs, dtcs, dtrs, rowv, rowv, vec, vec, vec, colv, colv],
        out_specs=[blk, nbc, nbc, blk, dtcs, vec, vec, rowv, rowv],
        out_shape=[jax.ShapeDtypeStruct((S, SSM_INNER), F32), jax.ShapeDtypeStruct((S, SSM_GROUPS * SSM_STATE), F32),
                   jax.ShapeDtypeStruct((S, SSM_GROUPS * SSM_STATE), F32), jax.ShapeDtypeStruct((S, SSM_INNER), BF16),
                   jax.ShapeDtypeStruct((SSM_GROUPS, S, 8), F32)] + [jax.ShapeDtypeStruct((1, SSM_INNER), F32)] * 2
                  + [jax.ShapeDtypeStruct((SSM_GROUPS, 1, 8), F32)] * 2,
        scratch_shapes=[pltpu.VMEM((SSM_STATE, GROUP_LANES), F32)],
        compiler_params=_params(2, 56 << 20))(xbc, xbc, xbc, z, y, dssm, hs_all, dtc, dtr, bias_r, alog_r, alog_b, dskip_b,
                                              normw, bias_c, alog_c)


def _place():
    return lax.axis_index("x"), lax.axis_index("y"), lax.axis_index("c")


def _other_chips(x, y):
    return [(1 - x, y), (x, 1 - y), (1 - x, 1 - y)]


def _half_rows(rows, which):
    hr = rows // 2
    if isinstance(which, int):
        return pl.ds(which * hr, hr)
    return pl.ds(pl.multiple_of(which * hr, 8), hr)


def _chip_gather(name, shards, split):
    n = len(shards)
    ANY = pl.BlockSpec(memory_space=pl.ANY)

    def body(*refs):
        ins, outs = refs[:n], refs[n:2 * n]
        send, recv, fsend, frecv = refs[2 * n:]
        x, y, c = _place()
        me = 2 * x + y
        sibling = (x, y, 1 - c)
        chips = _other_chips(x, y)

        def piece(a, chip_idx, which):
            if split[a]:
                return outs[a].at[chip_idx, _half_rows(shards[a].shape[0], which)]
            return outs[a].at[chip_idx]

        def ici(k, a, to_chip, src_chip):
            src = ins[a].at[_half_rows(shards[a].shape[0], c)] if split[a] else ins[a]
            return pltpu.make_async_remote_copy(src_ref=src, dst_ref=piece(a, src_chip, c), send_sem=send.at[k, a],
                                                recv_sem=recv.at[k, a], device_id=(*to_chip, c), device_id_type=MESH)

        def fwd(k, a, src_chip, which):
            return pltpu.make_async_remote_copy(src_ref=piece(a, src_chip, which), dst_ref=piece(a, src_chip, which),
                                                send_sem=fsend.at[k, a], recv_sem=frecv.at[k, a], device_id=sibling,
                                                device_id_type=MESH)

        sends = []
        for k, chip in enumerate(chips):
            for a in range(n):
                cp = ici(k, a, chip, me)
                cp.start()
                sends.append(cp)
        for k, (ox, oy) in enumerate(chips):
            src = 2 * ox + oy
            for a in range(n):
                ici(k, a, (ox, oy), src).wait_recv()
                if split[a]:
                    cp = fwd(k, a, src, c)
                    cp.start()
                    sends.append(cp)
        for k, (ox, oy) in enumerate(chips):
            for a in range(n):
                if split[a]:
                    fwd(k, a, 2 * ox + oy, 1 - c).wait_recv()
        for cp in sends:
            cp.wait_send()

    sem = pltpu.SemaphoreType.DMA((3, n))
    return pl.pallas_call(
        body, name=name, in_specs=[ANY] * n, out_specs=[ANY] * n,
        out_shape=[jax.ShapeDtypeStruct((4,) + s.shape, s.dtype) for s in shards],
        scratch_shapes=[sem, sem, sem, sem])(*shards)


def _chip_copies_start(name, srcs, per_chip_src, after):
    n = len(srcs)
    HBM = pl.BlockSpec(memory_space=pltpu.HBM)
    SEM = pl.BlockSpec(memory_space=pltpu.SEMAPHORE)
    lands = [pltpu.with_memory_space_constraint(lax.empty(a.shape if per_chip_src else (4,) + a.shape, a.dtype), pltpu.HBM)
             for a in srcs]

    def body(*refs):
        ins, land = refs[:n], refs[n:2 * n]
        send, recv = refs[2 * n + 1], refs[2 * n + 2]
        token = refs[-1]
        x, y, c = _place()
        me = 2 * x + y
        for k, (ox, oy) in enumerate(_other_chips(x, y)):
            for a in range(n):
                src = ins[a].at[2 * ox + oy] if per_chip_src else ins[a]
                pltpu.make_async_remote_copy(src_ref=src, dst_ref=land[a].at[me], send_sem=send.at[k * n + a], recv_sem=recv.at[k * n + a],
                                             device_id=(ox, oy, c), device_id_type=MESH).start()
        token[...] = jnp.zeros_like(token)

    sem = pltpu.SemaphoreType.DMA((3 * n,))
    res = pl.pallas_call(
        body, name=name,
        out_shape=[sem, sem] + [pltpu.HBM(a.shape, a.dtype) for a in srcs] + [pltpu.HBM(b.shape, b.dtype) for b in lands]
                  + [jax.ShapeDtypeStruct((8, LANES), F32)],
        in_specs=[HBM] * (2 * n) + [pl.BlockSpec(memory_space=pl.ANY)],
        out_specs=[SEM, SEM] + [HBM] * (2 * n) + [pl.BlockSpec(memory_space=pltpu.VMEM)],
        input_output_aliases={k: 2 + k for k in range(2 * n)},
        compiler_params=pltpu.CompilerParams(has_side_effects=pltpu.SideEffectType.DATAFLOW_SIDE_EFFECTING),
    )(*[pltpu.with_memory_space_constraint(a, pltpu.HBM) for a in srcs], *lands, after)
    return res[:-1], res[-1]


def _chip_copies_wait(name, started, per_chip_src, after):
    send, recv = started[0], started[1]
    n = (len(started) - 2) // 2
    srcs, lands = started[2:2 + n], started[2 + n:]
    HBM = pl.BlockSpec(memory_space=pltpu.HBM)
    SEM = pl.BlockSpec(memory_space=pltpu.SEMAPHORE)

    def body(*refs):
        ins, land = refs[:n], refs[n:2 * n]
        send_sem, recv_sem = refs[2 * n], refs[2 * n + 1]
        x, y, c = _place()
        me = 2 * x + y
        for k, (ox, oy) in enumerate(_other_chips(x, y)):
            for a in range(n):
                src = ins[a].at[me] if per_chip_src else ins[a]
                cp = pltpu.make_async_remote_copy(src_ref=src, dst_ref=land[a].at[2 * ox + oy], send_sem=send_sem.at[k * n + a],
                                                  recv_sem=recv_sem.at[k * n + a], device_id=(ox, oy, c), device_id_type=MESH)
                cp.wait_send()
                cp.wait_recv()

    res = pl.pallas_call(
        body, name=name,
        out_shape=[pltpu.HBM(a.shape, a.dtype) for a in srcs] + [pltpu.HBM(b.shape, b.dtype) for b in lands],
        in_specs=[HBM] * (2 * n) + [SEM, SEM, pl.BlockSpec(memory_space=pl.ANY)], out_specs=[HBM] * (2 * n),
        input_output_aliases={k: k for k in range(2 * n)},
        compiler_params=pltpu.CompilerParams(has_side_effects=pltpu.SideEffectType.DATAFLOW_SIDE_EFFECTING),
    )(*srcs, *lands, send, recv, after)
    return res[n:]


def _half_to_sibling(name, blocks):
    n = len(blocks)
    ANY = pl.BlockSpec(memory_space=pl.ANY)

    def body(*refs):
        ins, outs = refs[:n], refs[n:2 * n]
        send, recv = refs[2 * n:]
        x, y, c = _place()
        cps = [pltpu.make_async_remote_copy(src_ref=ins[a].at[:, _half_rows(blocks[a].shape[1], 1 - c)], dst_ref=outs[a],
                                            send_sem=send.at[a], recv_sem=recv.at[a], device_id=(x, y, 1 - c),
                                            device_id_type=MESH) for a in range(n)]
        for cp in cps:
            cp.start()
        for cp in cps:
            cp.wait_recv()
        for cp in cps:
            cp.wait_send()

    return pl.pallas_call(
        body, name=name, in_specs=[ANY] * n, out_specs=[ANY] * n,
        out_shape=[jax.ShapeDtypeStruct((4, b.shape[1] // 2, b.shape[2]), b.dtype) for b in blocks],
        scratch_shapes=[pltpu.SemaphoreType.DMA((n,)), pltpu.SemaphoreType.DMA((n,))])(*blocks)


def _sibling_swap(name, arrs):
    n = len(arrs)
    ANY = pl.BlockSpec(memory_space=pl.ANY)

    def body(*refs):
        ins, outs = refs[:n], refs[n:2 * n]
        send, recv = refs[2 * n:]
        x, y, c = _place()
        cps = [pltpu.make_async_remote_copy(src_ref=ins[a], dst_ref=outs[a], send_sem=send.at[a], recv_sem=recv.at[a],
                                            device_id=(x, y, 1 - c), device_id_type=MESH) for a in range(n)]
        for cp in cps:
            cp.start()
        for cp in cps:
            cp.wait_recv()
        for cp in cps:
            cp.wait_send()

    return pl.pallas_call(
        body, name=name, in_specs=[ANY] * n, out_specs=[ANY] * n,
        out_shape=[jax.ShapeDtypeStruct(a.shape, a.dtype) for a in arrs],
        scratch_shapes=[pltpu.SemaphoreType.DMA((n,)), pltpu.SemaphoreType.DMA((n,))])(*arrs)


N_DEV = 8


def _all_sum_small(vec):
    P = vec.shape[1]

    def body(v_ref, o_ref, buf, send, recv):
        x, y, c = _place()
        me = 4 * x + 2 * y + c
        buf[me] = v_ref[...]

        def peer(r):
            return ((1 - x) if (r >> 2) & 1 else x, (1 - y) if (r >> 1) & 1 else y, (1 - c) if r & 1 else c)

        sends = []
        for r in range(1, N_DEV):
            cp = pltpu.make_async_remote_copy(src_ref=v_ref, dst_ref=buf.at[me], send_sem=send.at[r], recv_sem=recv.at[r],
                                              device_id=peer(r), device_id_type=MESH)
            cp.start()
            sends.append(cp)
        for r in range(1, N_DEV):
            px, py, pc = peer(r)
            pltpu.make_async_remote_copy(src_ref=v_ref, dst_ref=buf.at[4 * px + 2 * py + pc], send_sem=send.at[r],
                                         recv_sem=recv.at[r], device_id=(px, py, pc), device_id_type=MESH).wait_recv()
        for cp in sends:
            cp.wait_send()
        tot = buf[0]
        for d in range(1, N_DEV):
            tot = tot + buf[d]
        o_ref[...] = tot

    return pl.pallas_call(
        body, name="all_sum_small", in_specs=[pl.BlockSpec(memory_space=pltpu.VMEM)],
        out_specs=pl.BlockSpec(memory_space=pltpu.VMEM), out_shape=jax.ShapeDtypeStruct((1, P), F32),
        scratch_shapes=[pltpu.VMEM((N_DEV, 1, P), F32), pltpu.SemaphoreType.DMA((N_DEV,)), pltpu.SemaphoreType.DMA((N_DEV,))],
    )(vec)


def _half_sum(name, blocks, theirs, core, tr):
    _, R, C = blocks.shape
    hr = R // 2
    nb = hr // tr
    assert nb * tr == hr

    def body(c_ref, a_ref, b_ref, o_ref):
        o_ref[...] = (a_ref[...] + b_ref[...]).astype(BF16)

    grid_spec = pltpu.PrefetchScalarGridSpec(
        num_scalar_prefetch=1, grid=(4, nb),
        in_specs=[pl.BlockSpec((1, tr, C), lambda b, i, c_ref: (b, c_ref[0] * nb + i, 0)),
                  pl.BlockSpec((1, tr, C), lambda b, i, c_ref: (b, i, 0))],
        out_specs=pl.BlockSpec((1, tr, C), lambda b, i, c_ref: (b, i, 0)))
    return pl.pallas_call(body, name=name, grid_spec=grid_spec, out_shape=jax.ShapeDtypeStruct((4, hr, C), BF16),
                          compiler_params=_params(2, 40 << 20))(core, blocks, theirs)


def _sum4(name, stack, mine, chip, tr):
    _, R, C = stack.shape

    def body(chip_ref, s_ref, m_ref, o_ref):
        t = [jnp.where(chip_ref[0] == j, m_ref[j], s_ref[j]).astype(F32) for j in range(4)]
        o_ref[...] = ((t[0] + t[1]) + t[2]) + t[3]

    blk = pl.BlockSpec((4, tr, C), lambda i, chip_ref: (0, i, 0))
    grid_spec = pltpu.PrefetchScalarGridSpec(num_scalar_prefetch=1, grid=(R // tr,), in_specs=[blk, blk],
                                             out_specs=pl.BlockSpec((tr, C), lambda i, chip_ref: (i, 0)))
    return pl.pallas_call(body, name=name, grid_spec=grid_spec, out_shape=jax.ShapeDtypeStruct((R, C), F32),
                          compiler_params=_params(1, 40 << 20))(chip, stack, mine)


def _adamw_math(w, m, v, g):
    c1 = 1.0 - ADAM_B1 ** ADAM_STEP
    c2 = 1.0 - ADAM_B2 ** ADAM_STEP
    nm = ADAM_B1 * m + (1.0 - ADAM_B1) * g
    nv = ADAM_B2 * v + (1.0 - ADAM_B2) * (g * g)
    return -ADAM_LR * ((nm / c1) / (jnp.sqrt(nv / c2) + ADAM_EPS) + ADAM_WD * w), nm, nv


def _adamw(name, w, m, v, g, tr):
    R, C = w.shape

    def body(w_ref, m_ref, v_ref, ga_ref, g_ref, d_ref, nm_ref, nv_ref):
        g = ga_ref[...]
        g_ref[...] = g
        d_ref[...], nm_ref[...], nv_ref[...] = _adamw_math(w_ref[...], m_ref[...], v_ref[...], g)

    spec = pl.BlockSpec((tr, C), lambda i: (i, 0))
    return pl.pallas_call(body, name=name, grid=(R // tr,), in_specs=[spec] * 4, out_specs=[spec] * 4,
                          out_shape=[jax.ShapeDtypeStruct((R, C), F32)] * 4, compiler_params=_params(1, 40 << 20))(w, m, v, g)


def _adamw_halves(name, w, m, v, mine, theirs, core, tr):
    _, R, C = w.shape
    nb = (R // 2) // tr
    assert 2 * nb * tr == R

    def body(c_ref, w_ref, m_ref, v_ref, a_ref, b_ref, g_ref, d_ref, nm_ref, nv_ref):
        g = jnp.where((pl.program_id(0) // nb) == c_ref[0], a_ref[...], b_ref[...])
        g_ref[0] = g
        d_ref[0], nm_ref[0], nv_ref[0] = _adamw_math(w_ref[0], m_ref[0], v_ref[0], g)

    spec = pl.BlockSpec((1, tr, C), lambda i, c_ref: (0, i, 0))
    half = lambda own: pl.BlockSpec((tr, C), lambda i, c_ref, own=own: (
        jnp.clip(i - (c_ref[0] if own else 1 - c_ref[0]) * nb, 0, nb - 1), 0))
    grid_spec = pltpu.PrefetchScalarGridSpec(num_scalar_prefetch=1, grid=(R // tr,),
                                             in_specs=[spec, spec, spec, half(True), half(False)], out_specs=[spec] * 4)
    return pl.pallas_call(body, name=name, grid_spec=grid_spec, out_shape=[jax.ShapeDtypeStruct((1, R, C), F32)] * 4,
                          compiler_params=_params(1, 40 << 20))(core, w, m, v, mine, theirs)


def _row_tile(rows, cols, budget_bytes=1 << 20, mult=8):
    best = None
    for t in range(mult, rows + 1, mult):
        if rows % t == 0 and t * cols * 4 <= budget_bytes:
            best = t
    return best if best is not None else rows


def _ln_fwd(r, g, b):
    mu = jnp.mean(r, axis=1, keepdims=True)
    xc = r - mu
    rstd = lax.rsqrt(jnp.mean(xc * xc, axis=1, keepdims=True) + LN_EPS)
    xhat = xc * rstd
    return xhat * g + b, xhat, rstd


def _ln_bwd(dy, xhat, rstd, g):
    dxh = dy * g
    return rstd * (dxh - jnp.mean(dxh, axis=1, keepdims=True) - xhat * jnp.mean(dxh * xhat, axis=1, keepdims=True))


def _col_segments(sections, width):
    out, cur, room = [], [], width
    for arr, lo, hi in sections:
        while lo < hi:
            take = min(room, hi - lo)
            cur.append((arr, lo, lo + take))
            lo, room = lo + take, room - take
            if room == 0:
                out.append(cur)
                cur, room = [], width
    assert not cur
    return out


def _to_chip_blocks_cols(a):
    R, C4 = a.shape
    return a.reshape(R, 4, C4 // 4).transpose(1, 0, 2)


def _from_chip_blocks_cols(a):
    return a.transpose(1, 0, 2).reshape(a.shape[1], 4 * a.shape[2])


def kernel(x, w_in, b_forget, conv_w, conv_b, dt_bias, a_log, d_skip, ssm_norm_w, w_proj_attn, w_proj_ssm, b_gates, w_out, ln1_g, ln1_b, w_ffn_gate, w_ffn_up, w_ffn_down, ln2_g, ln2_b, loss_target, m_w_in, m_b_forget, m_conv_w, m_conv_b, m_dt_bias, m_a_log, m_d_skip, m_ssm_norm_w, m_w_proj_attn, m_w_proj_ssm, m_b_gates, m_w_out, m_ln1_g, m_ln1_b, m_w_ffn_gate, m_w_ffn_up, m_w_ffn_down, m_ln2_g, m_ln2_b, v_w_in, v_b_forget, v_conv_w, v_conv_b, v_dt_bias, v_a_log, v_d_skip, v_ssm_norm_w, v_w_proj_attn, v_w_proj_ssm, v_b_gates, v_w_out, v_ln1_g, v_ln1_b, v_w_ffn_gate, v_w_ffn_up, v_w_ffn_down, v_ln2_g, v_ln2_b):
    S = x.shape[1]
    D = D_MODEL
    TM, TM2, TM3, TA, AQF, LC, CV, TS, TB = (min(TILES[k], S) for k in ("TM", "TM2", "TM3", "TA", "AQF", "LC", "CV", "TS", "TB"))
    xf = x[0]
    tgt = loss_target[0]
    xb = xf.astype(BF16)

    shards = [w_in[0].astype(BF16), conv_w[0], w_proj_attn[0].astype(BF16), w_proj_ssm[0].astype(BF16), w_out[0].astype(BF16),
              w_ffn_gate[0].astype(BF16), w_ffn_up[0].astype(BF16), w_ffn_down[0].astype(BF16)]
    chip = 2 * lax.axis_index("x") + lax.axis_index("y")
    own = lambda gathered, mine: [lax.dynamic_update_slice(g, sh[None], (chip, 0, 0)) for g, sh in zip(gathered, mine)]
    g_in, g_cw = own(_chip_gather("gather_w_in", shards[:2], [True, False]), shards[:2])
    later, gather_token = _chip_copies_start("gather_rest_start", shards[2:], False, g_cw)
    shard_w = IN_WIDTH // 4

    def w_cols(lo, hi):
        return [g_in[j][:, max(lo, j * shard_w) - j * shard_w:min(hi, (j + 1) * shard_w) - j * shard_w]
                for j in range(4) if max(lo, j * shard_w) < min(hi, (j + 1) * shard_w)]

    w_re = jnp.concatenate(w_cols(0, 3072) + w_cols(3088, 5136) + w_cols(5136, 8208) + w_cols(8240, 10288)
                           + w_cols(3072, 3088) + w_cols(8208, 8240) + [jnp.zeros((D, 80), BF16)], axis=1)
    conv_w_full = _from_chip_blocks_cols(g_cw)

    def plain(accs, rows, vecs, j):
        return [accs[0]], []

    def q_scaled(accs, rows, vecs, j):
        return [accs[0] * jnp.where(j * 512 < D, 1.0 / math.sqrt(ATT_HEAD_DIM), 1.0)], []

    qkv, = _mm("proj_qkv", S, 3072, TM, 512, [(xb, D, 0)], [(w_re, 0)], [(0, 0)], q_scaled, [(3072, BF16, 0)],
               after=[gather_token])
    z, = _mm("proj_z", S, 2048, TM, 512, [(xb, D, 0)], [(w_re, RE_Z // 512)], [(0, 0)], plain, [(2048, BF16, 0)])
    xbc_raw, = _mm("proj_xbc", S, 3072, TM, 512, [(xb, D, 0)], [(w_re, RE_XBC // 512)], [(0, 0)], plain, [(3072, F32, 0)])
    gl, = _mm("proj_gate", S, 2048, TM, 512, [(xb, D, 0)], [(w_re, RE_GATE // 512)], [(0, 0)], plain, [(2048, BF16, 0)])
    small, = _mm("proj_small", S, 128, TM, 128, [(xb, D, 0)], [(w_re, RE_SMALL // 128)], [(0, 0)], plain, [(128, F32, 0)])

    bvec = jnp.concatenate([b_forget, jnp.zeros((1, LANES - ATT_HEADS), F32)], axis=1)
    cum = _cum_fwd(small, bvec, TB)[:, :ATT_HEADS]
    cum_cols = cum.reshape(S, 8, 2).transpose(1, 0, 2)
    qa, ka = _attn_prep(qkv, cum_cols, TM)
    o, o32, lse_rows = _attn_fwd(qa, ka, qkv, AQF, TA)

    cb_row = conv_b
    xbc = _conv_fwd(xbc_raw, conv_w_full, cb_row, CV, 512)
    dt_raw = small[:, 16:48]
    dtc = dt_raw.reshape(S, SSM_GROUPS, 8).transpose(1, 0, 2)
    dtr = dt_raw.T.reshape(SSM_GROUPS, 8, S)
    bias_r = dt_bias.reshape(SSM_GROUPS, 1, 8)
    alog_b = jnp.repeat(a_log, ATT_HEAD_DIM, axis=1)
    dskip_b = jnp.repeat(d_skip, ATT_HEAD_DIM, axis=1)
    bias_c = dt_bias.reshape(SSM_GROUPS, 8, 1)
    alog_c = a_log.reshape(SSM_GROUPS, 8, 1)
    y_ssd, ssm, hs_all = _ssd_fwd(xbc, z, dtc, dtr, bias_r, alog_b, dskip_b, ssm_norm_w, bias_c, alog_c, LC)

    def merge(accs, rows, vecs, j):
        g0, g1 = _sigmoid(rows[0].astype(F32) + vecs[0]), _sigmoid(rows[1].astype(F32) + vecs[1])
        return [g0 * accs[0] + g1 * accs[1], accs[0], accs[1]], []

    g_pa, g_ps, g_out, g_fg, g_fu, g_fd = own(_chip_copies_wait("gather_rest_wait", later, False, o), shards[2:])
    wpa, wps, wout = g_pa.reshape(D, D), g_ps.reshape(SSM_INNER, D), g_out.reshape(D, D)
    wfg, wfu, wfd = _from_chip_blocks_cols(g_fg), _from_chip_blocks_cols(g_fu), g_fd.reshape(FFN_HIDDEN, D)
    mix, attn_d, ssm_d = _mm("merge", S, D, TM, 512, [(o, D, 0), (ssm, SSM_INNER, 0)], [(wpa, 0), (wps, 0)], [(0, 0), (1, 1)],
                             merge, [(D, BF16, 0), (D, BF16, 0), (D, BF16, 0)], rows=[(gl, 0), (gl, 2)],
                             vecs_n=[(b_gates, 0), (b_gates, 2)])

    def out_ln1(accs, rows, vecs, j):
        r1 = ALPHA * rows[0] + accs[0]
        h1, _, _ = _ln_fwd(r1, vecs[0], vecs[1])
        return [r1, h1, h1], []

    r1, h1, h1b = _mm("out_ln1", S, D, TM2, D, [(mix, D, 0)], [(wout, 0)], [(0, 0)], out_ln1,
                      [(D, F32, 0), (D, F32, 0), (D, BF16, 0)], rows=[(xf, 0)], vecs_n=[(ln1_g, 0), (ln1_b, 0)])

    FT = FFN_HIDDEN // 2

    def swiglu(accs, rows, vecs, j):
        g, u = accs
        return [g, u, g * _sigmoid(g) * u], []

    gate, up, hmid = _mm("ffn_up", S, FFN_HIDDEN, TM3, FT, [(h1b, D, 0)], [(wfg, 0), (wfu, 0)], [(0, 0), (0, 1)], swiglu,
                         [(FFN_HIDDEN, BF16, 0), (FFN_HIDDEN, BF16, 0), (FFN_HIDDEN, BF16, 0)])

    def down_ln2_loss(accs, rows, vecs, j):
        r2 = ALPHA * rows[0] + accs[0]
        yv, xhat, rstd = _ln_fwd(r2, vecs[0], vecs[1])
        diff = yv - rows[1]
        dy = diff * (1.0 / D_MODEL)
        dr2 = _ln_bwd(dy, xhat, rstd, vecs[0])
        return [dr2, dr2], [jnp.sum(dy * xhat, axis=0, keepdims=True), jnp.sum(dy, axis=0, keepdims=True),
                            (0.5 / D_MODEL) * jnp.sum(diff * diff, axis=0, keepdims=True)]

    dr2, dr2b, dln2_g, dln2_b, loss_lanes = _mm("ffn_down_ln2", S, D, TM3, D, [(hmid, FFN_HIDDEN, 0)], [(wfd, 0)], [(0, 0)],
                                               down_ln2_loss, [(D, F32, 0), (D, BF16, 0)], rows=[(h1, 0), (tgt, 0)],
                                               vecs_n=[(ln2_g, 0), (ln2_b, 0)], sums=[D, D, D])
    loss = lax.psum(jnp.sum(loss_lanes), ("x", "y", "c"))

    def dswiglu(accs, rows, vecs, j):
        g, u = rows[0].astype(F32), rows[1].astype(F32)
        sg = _sigmoid(g)
        return [accs[0] * u * sg * (1.0 + g * (1.0 - sg)), accs[0] * g * sg], []

    dgate, dup = _mm("ffn_down_bwd", S, FFN_HIDDEN, TM3, FT, [(dr2b, D, 0)], [(wfd, 0)], [(0, 0)], dswiglu,
                     [(FFN_HIDDEN, BF16, 0), (FFN_HIDDEN, BF16, 0)], nt=True, rows=[(gate, 0), (up, 0)])
    dwfd = _mm_tn("dw_ffn_down", hmid, dr2b, FFN_HIDDEN // 2, D, TS)
    dwfg = _mm_tn("dw_ffn_gate", h1b, dgate, D, FT, TS)
    dwfu = _mm_tn("dw_ffn_up", h1b, dup, D, FT, TS)
    core = lax.axis_index("c").astype(jnp.int32).reshape(1)

    def send_grads(tag, names_, blocks_, after_):
        theirs_ = _half_to_sibling("swap_halves_" + tag, blocks_)
        halves_ = [_half_sum("halfsum_" + nm, b, t, core, _row_tile(b.shape[1] // 2, b.shape[2], mult=16))
                   for nm, b, t in zip(names_, blocks_, theirs_)]
        started_, token_ = _chip_copies_start("scatter_" + tag + "_start", halves_, True, after_)
        return halves_, started_, token_

    ffn_names = ["w_ffn_gate", "w_ffn_up", "w_ffn_down"]
    ffn_halves, ffn_started, ffn_token = send_grads(
        "ffn", ffn_names, [_to_chip_blocks_cols(dwfg), _to_chip_blocks_cols(dwfu), dwfd.reshape(4, FFN_HIDDEN // 4, D)], dwfu)

    def dh1_ln1(accs, rows, vecs, j):
        dh1 = ALPHA * rows[0] + accs[0] + accs[1]
        _, xhat, rstd = _ln_fwd(rows[1], vecs[0], vecs[0])
        dr1 = _ln_bwd(dh1, xhat, rstd, vecs[0])
        return [dr1, dr1], [jnp.sum(dh1 * xhat, axis=0, keepdims=True), jnp.sum(dh1, axis=0, keepdims=True)]

    dr1, dr1b, dln1_g, dln1_b = _mm("ffn_up_bwd_ln1", S, D, TM2, D, [(dgate, FFN_HIDDEN, 0), (dup, FFN_HIDDEN, 0)],
                                    [(wfg, 0), (wfu, 0)], [(0, 0), (1, 1)], dh1_ln1, [(D, F32, 0), (D, BF16, 0)], nt=True,
                                    rows=[(dr2, 0), (r1, 0)], vecs_n=[(ln1_g, 0)], sums=[D, D], after=[ffn_token])

    def dmerge(accs, rows, vecs, j):
        dmix = accs[0]
        g0, g1 = _sigmoid(rows[0].astype(F32) + vecs[0]), _sigmoid(rows[1].astype(F32) + vecs[1])
        dgl0 = dmix * rows[2].astype(F32) * g0 * (1.0 - g0)
        dgl1 = dmix * rows[3].astype(F32) * g1 * (1.0 - g1)
        return [dmix * g0, dmix * g1, dgl0, dgl1], [jnp.sum(dgl0, axis=0, keepdims=True), jnp.sum(dgl1, axis=0, keepdims=True)]

    d_attn_d, d_ssm_d, dgl0, dgl1, dbg0, dbg1 = _mm(
        "out_bwd", S, D, TM, 512, [(dr1b, D, 0)], [(wout, 0)], [(0, 0)], dmerge, [(D, BF16, 0)] * 4, nt=True,
        rows=[(gl, 0), (gl, 2), (attn_d, 0), (ssm_d, 0)], vecs_n=[(b_gates, 0), (b_gates, 2)], sums=[D, D])
    dwout = _mm_tn("dw_out", mix, dr1b, D, D, TS)
    dwpa = _mm_tn("dw_proj_attn", o, d_attn_d, D, D, TS)
    dwps = _mm_tn("dw_proj_ssm", ssm, d_ssm_d, D, D, TS)
    mid_names = ["w_proj_attn", "w_proj_ssm", "w_out"]
    mid_halves, mid_started, mid_token = send_grads(
        "mid", mid_names, [dwpa.reshape(4, D // 4, D), dwps.reshape(4, SSM_INNER // 4, D), dwout.reshape(4, D // 4, D)], dwps)

    do, = _mm("proj_attn_bwd", S, D, TM, 512, [(d_attn_d, D, 0)], [(wpa, 0)], [(0, 0)], plain, [(D, BF16, 0)], nt=True,
              after=[mid_token])
    stats = _attn_stats(do, o32, lse_rows.transpose(0, 2, 1), AQF)
    dq, dk, dv, dck, dcq = _attn_bwd(qa, ka, qkv, do, stats, TA)

    def per_head(a):
        a = a.transpose(1, 0, 2).reshape(S, ATT_HEADS)
        return jnp.concatenate([a, jnp.zeros((S, LANES - ATT_HEADS), F32)], axis=1)

    dfl, dbf = _cum_bwd(per_head(dck.transpose(0, 2, 1)), per_head(dcq), small, bvec, TB)

    dssm, = _mm("proj_ssm_bwd", S, SSM_INNER, TM, 512, [(d_ssm_d, D, 0)], [(wps, 0)], [(0, 0)], plain, [(SSM_INNER, F32, 0)],
                nt=True)
    dxs, dbm, dcm, dz, ddt8, dnw, ddskip_b, dalog8, dbias8 = _ssd_bwd(
        xbc, z, y_ssd, dssm, hs_all, dtc, dtr, bias_r, a_log.reshape(SSM_GROUPS, 1, 8), alog_b, dskip_b, ssm_norm_w, bias_c,
        alog_c, LC)
    du_x, dcw_x, dcb_x = _conv_bwd("conv_bwd_x", xbc_raw, dxs, conv_w_full, cb_row, CV, 512, 0)
    du_b, dcw_b, dcb_b = _conv_bwd("conv_bwd_b", xbc_raw, dbm, conv_w_full, cb_row, CV, 512, SSM_INNER)
    du_c, dcw_c, dcb_c = _conv_bwd("conv_bwd_c", xbc_raw, dcm, conv_w_full, cb_row, CV, 512, SSM_INNER + SSM_GROUPS * SSM_STATE)
    dconv_w = jnp.concatenate([dcw_x, dcw_b, dcw_c], axis=1)
    dconv_b = jnp.concatenate([dcb_x, dcb_b, dcb_c], axis=1)
    ddt_raw = ddt8.transpose(1, 0, 2).reshape(S, SSM_HEADS)

    dsmall = jnp.concatenate([dfl[:, :ATT_HEADS], ddt_raw, jnp.zeros((S, 80), F32)], axis=1).astype(BF16)
    HB = SSM_GROUPS * SSM_STATE
    dw_q, dw_k, dw_v = (_mm_tn("dw_in_" + nm, xb, g_, D, D, TS) for nm, g_ in (("q", dq), ("k", dk), ("v", dv)))
    dw_z = _mm_tn("dw_in_z", xb, dz, D, D, TS)
    dw_xs, dw_b, dw_c = _mm_tn("dw_in_xs", xb, du_x, D, D, TS), _mm_tn("dw_in_b", xb, du_b, D, HB, TS), _mm_tn("dw_in_c", xb, du_c, D, HB, TS)
    dw_g0, dw_g1 = _mm_tn("dw_in_g0", xb, dgl0, D, D, TS), _mm_tn("dw_in_g1", xb, dgl1, D, D, TS)
    dw_s = _mm_tn("dw_in_small", xb, dsmall, D, LANES, TS)
    whole = lambda a: (a, 0, a.shape[1])
    dw_sections = [whole(dw_q), whole(dw_k), whole(dw_v), (dw_s, 0, ATT_HEADS), whole(dw_z), whole(dw_xs), whole(dw_b), whole(dw_c),
                   (dw_s, ATT_HEADS, ATT_HEADS + SSM_HEADS), whole(dw_g0), whole(dw_g1)]
    dw_blocks = jnp.stack([jnp.concatenate([a[:, lo:hi] for a, lo, hi in segs], axis=1)
                           for segs in _col_segments(dw_sections, shard_w)])

    in_halves, in_started, in_token = send_grads("in", ["w_in"], [dw_blocks], dw_blocks)
    def dx_first(accs, rows, vecs, j):
        return [ALPHA * rows[0] + sum(accs[1:], accs[0])], []

    def dx_more(accs, rows, vecs, j):
        return [rows[0] + sum(accs[1:], accs[0])], []

    wk = lambda col, width=D: (w_re, 0, col // width, width)
    dx_part, = _mm("dx_a", S, D, TM2, D, [(dq, D, 0), (dk, D, 0), (dv, D, 0), (dz, D, 0), (dz, D, 1)],
                   [wk(0), wk(1024), wk(2048), wk(RE_Z), wk(RE_Z + 1024)], [(k, k) for k in range(5)], dx_first,
                   [(D, F32, 0)], nt=True, rows=[(dr1, 0)], after=[in_token])
    grad_x, = _mm("dx_b", S, D, TM2, D,
                  [(du_x, D, 0), (du_x, D, 1), (du_b, HB, 0), (du_c, HB, 0), (dgl0, D, 0), (dgl1, D, 0), (dsmall, LANES, 0)],
                  [wk(RE_XBC), wk(RE_XBC + 1024), wk(RE_XBC + 2048, HB), wk(RE_XBC + 2048 + HB, HB), wk(RE_GATE),
                   wk(RE_GATE + 1024), wk(RE_SMALL, LANES)],
                  [(k, k) for k in range(7)], dx_more, [(D, F32, 0)], nt=True, rows=[(dx_part, 0)])
    names = ["w_in"] + mid_names + ffn_names
    halves = in_halves + mid_halves + ffn_halves
    stacks = (_chip_copies_wait("scatter_in_wait", in_started, True, grad_x)
              + _chip_copies_wait("scatter_mid_wait", mid_started, True, grad_x)
              + _chip_copies_wait("scatter_ffn_wait", ffn_started, True, grad_x))
    chip1 = chip.astype(jnp.int32).reshape(1)
    reduced = [_sum4("sum_" + nm, st, hv, chip1, _row_tile(st.shape[1], st.shape[2], mult=16))
               for nm, st, hv in zip(names, stacks, halves)]
    other = _sibling_swap("swap_reduced", reduced)
    big_w = [w_in, w_proj_attn, w_proj_ssm, w_out, w_ffn_gate, w_ffn_up, w_ffn_down]
    big_m = [m_w_in, m_w_proj_attn, m_w_proj_ssm, m_w_out, m_w_ffn_gate, m_w_ffn_up, m_w_ffn_down]
    big_v = [v_w_in, v_w_proj_attn, v_w_proj_ssm, v_w_out, v_w_ffn_gate, v_w_ffn_up, v_w_ffn_down]
    big = {}
    for nm, w_, m_, v_, mine, theirs in zip(names, big_w, big_m, big_v, reduced, other):
        big[nm] = _adamw_halves("adamw_" + nm, w_, m_, v_, mine, theirs, core, _row_tile(w_.shape[1] // 2, w_.shape[2]))

    dd_skip = ddskip_b.reshape(1, SSM_HEADS, ATT_HEAD_DIM).sum(axis=2)
    pieces = [dbf[:, :ATT_HEADS], dconv_w.reshape(1, SSM_CONV * SSM_CONV_DIM), dconv_b, dbias8.reshape(1, SSM_HEADS), dalog8.reshape(1, SSM_HEADS), dd_skip,
              dnw, dbg0, dbg1, dln1_g, dln1_b, dln2_g, dln2_b]
    widths = [p.shape[1] for p in pieces]
    total = sum(widths)
    P = -(-total // LANES) * LANES
    packed = jnp.concatenate(pieces + [jnp.zeros((1, P - total), F32)], axis=1)
    summed = _all_sum_small(packed)
    offs = [0]
    for wd in widths:
        offs.append(offs[-1] + wd)
    sm = [summed[:, offs[k]:offs[k + 1]] for k in range(len(pieces))]
    g_bf, g_cw_full, g_cb, g_dtb, g_al, g_ds, g_nw = sm[0], sm[1].reshape(SSM_CONV, SSM_CONV_DIM), sm[2], sm[3], sm[4], sm[5], sm[6]
    g_bg = jnp.concatenate([sm[7], sm[8]], axis=1)
    g_l1g, g_l1b, g_l2g, g_l2b = sm[9], sm[10], sm[11], sm[12]
    cshard = SSM_CONV_DIM // 4
    g_cw_shard = lax.dynamic_slice_in_dim(g_cw_full, chip * cshard, cshard, axis=1)
    small_names = ["b_forget", "conv_w", "conv_b", "dt_bias", "a_log", "d_skip", "ssm_norm_w", "b_gates", "ln1_g", "ln1_b",
                   "ln2_g", "ln2_b"]
    small_g = [g_bf, g_cw_shard.reshape(1, -1), g_cb, g_dtb, g_al, g_ds, g_nw, g_bg, g_l1g, g_l1b, g_l2g, g_l2b]
    small_w = [b_forget, conv_w[0].reshape(1, -1), conv_b, dt_bias, a_log, d_skip, ssm_norm_w, b_gates, ln1_g, ln1_b, ln2_g, ln2_b]
    small_m = [m_b_forget, m_conv_w[0].reshape(1, -1), m_conv_b, m_dt_bias, m_a_log, m_d_skip, m_ssm_norm_w, m_b_gates, m_ln1_g,
               m_ln1_b, m_ln2_g, m_ln2_b]
    small_v = [v_b_forget, v_conv_w[0].reshape(1, -1), v_conv_b, v_dt_bias, v_a_log, v_d_skip, v_ssm_norm_w, v_b_gates, v_ln1_g,
               v_ln1_b, v_ln2_g, v_ln2_b]
    sw = [a.shape[1] for a in small_w]
    stot = sum(sw)
    SP = -(-stot // LANES) * LANES

    def pack(parts):
        return jnp.concatenate(list(parts) + [jnp.zeros((1, SP - stot), F32)], axis=1).reshape(SP // LANES, LANES)

    sres = _adamw("adamw_small", pack(small_w), pack(small_m), pack(small_v), pack(small_g), SP // LANES)
    soffs = [0]
    for wd in sw:
        soffs.append(soffs[-1] + wd)
    smalls = {}
    for k, nm in enumerate(small_names):
        vals = [r.reshape(1, SP)[:, soffs[k]:soffs[k + 1]] for r in sres]
        if nm == "conv_w":
            vals = [v_.reshape(1, SSM_CONV, cshard) for v_ in vals]
        smalls[nm] = vals

    order = ["w_in", "b_forget", "conv_w", "conv_b", "dt_bias", "a_log", "d_skip", "ssm_norm_w", "w_proj_attn", "w_proj_ssm",
             "b_gates", "w_out", "ln1_g", "ln1_b", "w_ffn_gate", "w_ffn_up", "w_ffn_down", "ln2_g", "ln2_b"]
    allres = {**big, **smalls}
    outs = [loss, grad_x[None]]
    for idx in range(4):
        outs += [allres[nm][idx] for nm in order]
    return tuple(outs)
```

```python
import functools
import math

import jax
import jax.numpy as jnp
from jax import lax
from jax.experimental import pallas as pl
from jax.experimental.pallas import tpu as pltpu

F32, BF16 = jnp.float32, jnp.bfloat16
MESH = pl.DeviceIdType.MESH

D_MODEL = 1024
ATT_HEADS, ATT_HEAD_DIM = 16, 64
SSM_INNER, SSM_HEADS, SSM_GROUPS, SSM_STATE, SSM_CONV = 2048, 32, 4, 128, 4
SSM_CONV_DIM = SSM_INNER + 2 * SSM_GROUPS * SSM_STATE
GROUP_LANES = SSM_INNER // SSM_GROUPS
FFN_HIDDEN = 2816
ALPHA = 2.0 ** 0.25
LN_EPS = 1e-5
RMS_EPS = 1e-5
ADAM_LR, ADAM_B1, ADAM_B2, ADAM_EPS, ADAM_WD, ADAM_STEP = 0.001, 0.9, 0.999, 1e-08, 0.01, 10
IN_SIZES = (1024, 1024, 1024, 16, 2048, 3072, 32, 2048)
IN_WIDTH = sum(IN_SIZES)
RE_WIDTH = 3072 + 2048 + 3072 + 2048 + 128
RE_Z, RE_XBC, RE_GATE, RE_SMALL = 3072, 5120, 8192, 10240

LANES = 128
VMEM_CAP = 60 * 1024 * 1024
NEG = -1e30
TILES = dict(TM=1024, TM2=256, TM3=512, TA=512, AQF=2048, LC=256, CV=512, TS=2048, TB=256)


def _params(n_axes, vmem_bytes=None):
    return pltpu.CompilerParams(dimension_semantics=("arbitrary",) * n_axes,
                                vmem_limit_bytes=None if vmem_bytes is None else int(min(vmem_bytes, VMEM_CAP)))


def _sigmoid(v):
    return 1.0 / (1.0 + jnp.exp(-v))


def _softplus(v):
    return jnp.maximum(v, 0.0) + jnp.log(1.0 + jnp.exp(-jnp.abs(v)))


def _dot(a, b):
    return lax.dot_general(a, b, (((1,), (0,)), ((), ())), preferred_element_type=F32)


def _dot_nt(a, b):
    return lax.dot_general(a, b, (((1,), (1,)), ((), ())), preferred_element_type=F32)


def _dot_tn(a, b):
    return lax.dot_general(a, b, (((0,), (0,)), ((), ())), preferred_element_type=F32)


def _split3(v):
    hi = v.astype(BF16)
    r1 = v - hi.astype(F32)
    mid = r1.astype(BF16)
    lo = (r1 - mid.astype(F32)).astype(BF16)
    return hi, mid, lo


def _dot_exact_left(m01, v):
    hi, mid, lo = _split3(v)
    return _dot(m01, hi) + _dot(m01, mid) + _dot(m01, lo)


def _dot_exact_right(v, m01, terms=3):
    parts = _split3(v)[:terms]
    out = _dot(parts[0], m01)
    for p in parts[1:]:
        out = out + _dot(p, m01)
    return out


def _mm(name, M, N, tm, tn, lhs, rhs, pairs, e_fn, outs, *, nt=False, rows=(), vecs_n=(), sums=(), after=()):
    ni, nj = M // tm, N // tn
    assert ni * tm == M and nj * tn == N, (name, M, N, tm, tn)
    n_l, n_r, n_row, n_vn, n_o, n_s = len(lhs), len(rhs), len(rows), len(vecs_n), len(outs), len(sums)

    def body(*refs):
        pos = 0
        l_refs = refs[pos:pos + n_l]; pos += n_l
        r_refs = refs[pos:pos + n_r]; pos += n_r
        row_refs = refs[pos:pos + n_row]; pos += n_row
        vn_refs = refs[pos:pos + n_vn]; pos += n_vn + len(after)
        o_refs = refs[pos:pos + n_o]; pos += n_o
        s_refs = refs[pos:pos + n_s]; pos += n_s
        i, j = pl.program_id(0), pl.program_id(1)
        accs = []
        for li, ri in pairs:
            accs.append(_dot_nt(l_refs[li][...], r_refs[ri][...]) if nt else _dot(l_refs[li][...], r_refs[ri][...]))
        out_vals, sum_vals = e_fn(accs, [r[...] for r in row_refs], [r[...] for r in vn_refs], j)
        for r, v in zip(o_refs, out_vals):
            r[...] = v.astype(r.dtype)
        if n_s:
            col = pl.multiple_of(j * tn, LANES)

            @pl.when(i == 0)
            def _():
                for r, v in zip(s_refs, sum_vals):
                    r[:, pl.ds(col, tn)] = v

            @pl.when(i > 0)
            def _():
                for r, v in zip(s_refs, sum_vals):
                    r[:, pl.ds(col, tn)] += v

    in_specs, args, est = [], [], 0
    for arr, width, cb in lhs:
        in_specs.append(pl.BlockSpec((tm, width), lambda i, j, cb=cb: (i, cb)))
        args.append(arr); est += tm * width * arr.dtype.itemsize
    for arr, off, *ksub in rhs:
        if nt:
            kb, kw = ksub if ksub else (0, arr.shape[1])
            in_specs.append(pl.BlockSpec((tn, kw), lambda i, j, off=off, kb=kb: (j + off, kb)))
            est += tn * kw * arr.dtype.itemsize
        else:
            in_specs.append(pl.BlockSpec((arr.shape[0], tn), lambda i, j, off=off: (0, j + off)))
            est += tn * arr.shape[0] * arr.dtype.itemsize
        args.append(arr)
    for arr, off in rows:
        in_specs.append(pl.BlockSpec((tm, tn), lambda i, j, off=off: (i, j + off)))
        args.append(arr); est += tm * tn * arr.dtype.itemsize
    for arr, off in vecs_n:
        in_specs.append(pl.BlockSpec((1, tn), lambda i, j, off=off: (0, j + off)))
        args.append(arr); est += 8 * tn * 4
    for arr in after:
        in_specs.append(pl.BlockSpec(memory_space=pl.ANY))
        args.append(arr)
    out_shape, out_specs = [], []
    for total, dtype, off in outs:
        out_shape.append(jax.ShapeDtypeStruct((M, total), dtype))
        out_specs.append(pl.BlockSpec((tm, tn), lambda i, j, off=off: (i, j + off)))
        est += tm * tn * jnp.dtype(dtype).itemsize
    for total in sums:
        out_shape.append(jax.ShapeDtypeStruct((1, total), F32))
        out_specs.append(pl.BlockSpec((1, total), lambda i, j: (0, 0)))
        est += 8 * total * 4
    vmem = 2 * est + (len(pairs) + 4) * tm * tn * 4 + (8 << 20)
    return pl.pallas_call(body, name=name, grid=(ni, nj), in_specs=in_specs, out_specs=out_specs, out_shape=out_shape,
                          compiler_params=_params(2, vmem))(*args)


def _mm_tn(name, a, g, ta, tn, ts, a_cols=None, a_off=0):
    S = a.shape[0]
    Ka = a.shape[1] if a_cols is None else a_cols
    N = g.shape[1]
    assert Ka % ta == 0 and N % tn == 0 and S % ts == 0, (name, Ka, N, S)
    aoff = a_off // ta

    def body(a_ref, g_ref, o_ref):
        s = pl.program_id(2)
        part = _dot_tn(a_ref[...], g_ref[...])

        @pl.when(s == 0)
        def _():
            o_ref[...] = part

        @pl.when(s > 0)
        def _():
            o_ref[...] += part

    vmem = 2 * (ts * ta * 2 + ts * tn * 2 + ta * tn * 4) + 2 * ta * tn * 4 + (8 << 20)
    return pl.pallas_call(
        body, name=name, grid=(Ka // ta, N // tn, S // ts),
        in_specs=[pl.BlockSpec((ts, ta), lambda ia, jn, s: (s, ia + aoff)), pl.BlockSpec((ts, tn), lambda ia, jn, s: (s, jn))],
        out_specs=pl.BlockSpec((ta, tn), lambda ia, jn, s: (ia, jn)),
        out_shape=jax.ShapeDtypeStruct((Ka, N), F32), compiler_params=_params(3, vmem))(a, g)


def _tri(n, upper):
    r = lax.broadcasted_iota(jnp.int32, (n, n), 0)
    c = lax.broadcasted_iota(jnp.int32, (n, n), 1)
    return jnp.where((c >= r) if upper else (c <= r), 1.0, 0.0).astype(BF16)


def _logsig(v):
    return jnp.minimum(v, 0.0) - jnp.log(1.0 + jnp.exp(-jnp.abs(v)))


def _cum_fwd(small, bvec, tb):
    S = small.shape[0]

    def body(x_ref, b_ref, o_ref, carry):
        i = pl.program_id(0)

        @pl.when(i == 0)
        def _():
            carry[...] = jnp.zeros_like(carry)

        logf = _logsig(x_ref[...] + b_ref[...])
        cum = _dot_exact_left(_tri(tb, False), logf) + carry[0:1, :]
        o_ref[...] = cum
        carry[0:1, :] = cum[tb - 1:tb, :]

    return pl.pallas_call(
        body, name="cum_fwd", grid=(S // tb,),
        in_specs=[pl.BlockSpec((tb, LANES), lambda i: (i, 0)), pl.BlockSpec((1, LANES), lambda i: (0, 0))],
        out_specs=pl.BlockSpec((tb, LANES), lambda i: (i, 0)), out_shape=jax.ShapeDtypeStruct((S, LANES), F32),
        scratch_shapes=[pltpu.VMEM((8, LANES), F32)], compiler_params=_params(1))(small, bvec)


def _cum_bwd(dcum_k, dcum_q, small, bvec, tb):
    S = small.shape[0]
    nb = S // tb

    def body(dk_ref, dq_ref, x_ref, b_ref, o_ref, s_ref, carry):
        i = pl.program_id(0)

        @pl.when(i == 0)
        def _():
            carry[...] = jnp.zeros_like(carry)
            s_ref[...] = jnp.zeros_like(s_ref)

        rc = _dot_exact_left(_tri(tb, True), dk_ref[...] + dq_ref[...]) + carry[0:1, :]
        dfl = rc * _sigmoid(-(x_ref[...] + b_ref[...]))
        o_ref[...] = dfl
        s_ref[...] += jnp.sum(dfl, axis=0, keepdims=True)
        carry[0:1, :] = rc[0:1, :]

    rev = lambda i: (nb - 1 - i, 0)
    return pl.pallas_call(
        body, name="cum_bwd", grid=(nb,),
        in_specs=[pl.BlockSpec((tb, LANES), rev)] * 3 + [pl.BlockSpec((1, LANES), lambda i: (0, 0))],
        out_specs=[pl.BlockSpec((tb, LANES), rev), pl.BlockSpec((1, LANES), lambda i: (0, 0))],
        out_shape=[jax.ShapeDtypeStruct((S, LANES), F32), jax.ShapeDtypeStruct((1, LANES), F32)],
        scratch_shapes=[pltpu.VMEM((8, LANES), F32)], compiler_params=_params(1))(dcum_k, dcum_q, small, bvec)


N_AUG = 3


def _lane():
    return lax.broadcasted_iota(jnp.int32, (1, LANES), 1)


def _lane_mask():
    return _lane() < ATT_HEAD_DIM


def _aug_base(h):
    return ATT_HEAD_DIM * (1 - h)


def _attn_prep(qkv, cum_cols, T):
    S = qkv.shape[0]
    HP = ATT_HEADS // 2

    def body(q_ref, k_ref, c_ref, qa_ref, ka_ref):
        lane = _lane()
        q = q_ref[...]
        k = k_ref[...]
        one, zero = jnp.ones_like(q), jnp.zeros_like(q)
        for h in (0, 1):
            base = _aug_base(h)
            own = (lane < ATT_HEAD_DIM) if h == 0 else (lane >= ATT_HEAD_DIM)
            term_lanes = (lane >= base) & (lane < base + N_AUG)
            terms = [t.astype(F32) for t in _split3(c_ref[0, :, h:h + 1])]
            neg = jnp.where(lane == base, -terms[0], jnp.where(lane == base + 1, -terms[1], -terms[2])).astype(BF16)
            qa_ref[:, h * LANES:(h + 1) * LANES] = jnp.where(lane == base + N_AUG, zero, jnp.where(term_lanes, one, q))
            ka_ref[:, h * LANES:(h + 1) * LANES] = jnp.where(term_lanes, neg, jnp.where(lane == base + N_AUG, one,
                                                                                         jnp.where(own, k, zero)))

    return pl.pallas_call(
        body, name="attn_prep", grid=(S // T, HP),
        in_specs=[pl.BlockSpec((T, LANES), lambda i, hp: (i, hp)), pl.BlockSpec((T, LANES), lambda i, hp: (i, HP + hp)),
                  pl.BlockSpec((1, T, 2), lambda i, hp: (hp, i, 0))],
        out_specs=[pl.BlockSpec((T, 2 * LANES), lambda i, hp: (i, hp))] * 2,
        out_shape=[jax.ShapeDtypeStruct((S, 2 * D_MODEL), BF16)] * 2, compiler_params=_params(2))(qkv, qkv, cum_cols)


def _attn_fwd(qa, ka, qkv, T, TK):
    S = qkv.shape[0]
    nq = S // T
    r = T // TK
    HP = ATT_HEADS // 2

    def body(q0_ref, q1_ref, k0_ref, k1_ref, v_ref, o_ref, o32_ref, lse_ref):
        i = pl.program_id(1)
        qs = (q0_ref[...], q1_ref[...])
        k_refs = (k0_ref, k1_ref)
        row = lax.broadcasted_iota(jnp.int32, (TK, T), 0)
        col = lax.broadcasted_iota(jnp.int32, (TK, T), 1)
        head_rows = lax.broadcasted_iota(jnp.int32, (LANES, 1), 0) < ATT_HEAD_DIM

        def block(j, carry, q0):
            off = pl.multiple_of(j * TK, TK)
            vj = v_ref[pl.ds(off, TK), :]
            full = q0 is None
            q0 = 0 if full else q0
            m0, l0, m1, l1, acc = carry
            new, alphas, pvs = [], [], []
            for h, (m, l) in enumerate(((m0, l0), (m1, l1))):
                st = _dot_nt(k_refs[h][pl.ds(off, TK), :], qs[h][q0:, :])
                if not full:
                    st = jnp.where(row[:, :T - q0] <= col[:, :T - q0], st, NEG)
                m_old, l_old = m[:, q0:], l[:, q0:]
                m_new = jnp.maximum(m_old, jnp.max(st, axis=0, keepdims=True))
                p = jnp.exp(st - m_new)
                alpha = jnp.exp(m_old - m_new)
                l_new = alpha * l_old + jnp.sum(p, axis=0, keepdims=True)
                pvs.append(_dot_tn(vj, p.astype(BF16)))
                alphas.append(alpha)
                new += [m_new, l_new]
            part = acc[:, q0:] * jnp.where(head_rows, alphas[0], alphas[1]) + jnp.where(head_rows, pvs[0], pvs[1])
            if q0:
                keep = lambda old, upd: jnp.concatenate([old[:, :q0], upd], axis=1)
                return (keep(m0, new[0]), keep(l0, new[1]), keep(m1, new[2]), keep(l1, new[3]), keep(acc, part))
            return (new[0], new[1], new[2], new[3], part)

        init = (jnp.full((1, T), NEG, F32), jnp.zeros((1, T), F32), jnp.full((1, T), NEG, F32), jnp.zeros((1, T), F32),
                jnp.zeros((LANES, T), F32))
        n_full = i * r
        carry = lax.fori_loop(0, n_full // 2, lambda jj, c: block(2 * jj + 1, block(2 * jj, c, None), None), init)
        carry = lax.cond(n_full % 2 == 1, lambda c: block(n_full - 1, c, None), lambda c: c, carry)
        for d in range(r):
            carry = block(n_full + d, carry, d * TK)
        m0, l0, m1, l1, acc = carry
        out = (acc / jnp.where(head_rows, l0, l1)).T
        o_ref[...] = out.astype(BF16)
        o32_ref[...] = out
        lse_ref[0, 0:1, :] = m0 + jnp.log(l0)
        lse_ref[0, 1:2, :] = m1 + jnp.log(l1)

    vmem = 2 * (2 * T * LANES * 2 + 3 * S * LANES * 2 + T * LANES * (2 + 4) + 8 * T * 4) + 10 * T * TK * 4 + (8 << 20)
    qspec = lambda h: pl.BlockSpec((T, LANES), lambda hp, i, h=h: (i, 2 * hp + h))
    kspec = lambda h: pl.BlockSpec((S, LANES), lambda hp, i, h=h: (0, 2 * hp + h))
    return pl.pallas_call(
        body, name="attn_fwd", grid=(HP, nq),
        in_specs=[qspec(0), qspec(1), kspec(0), kspec(1), pl.BlockSpec((S, LANES), lambda hp, i: (0, 2 * HP + hp))],
        out_specs=[pl.BlockSpec((T, LANES), lambda hp, i: (i, hp)), pl.BlockSpec((T, LANES), lambda hp, i: (i, hp)),
                   pl.BlockSpec((1, 2, T), lambda hp, i: (hp, 0, i))],
        out_shape=[jax.ShapeDtypeStruct((S, D_MODEL), BF16), jax.ShapeDtypeStruct((S, D_MODEL), F32),
                   jax.ShapeDtypeStruct((HP, 2, S), F32)],
        compiler_params=_params(2, vmem))(qa, qa, ka, ka, qkv)


def _attn_stats(do, o32, lse_cols, T):
    S = do.shape[0]
    HP = ATT_HEADS // 2

    def body(do_ref, o_ref, lse_ref, st_ref):
        lane = lax.broadcasted_iota(jnp.int32, (LANES, 8), 0)
        c = lax.broadcasted_iota(jnp.int32, (LANES, 8), 1)
        sel = jnp.where(((c == 2) & (lane < ATT_HEAD_DIM)) | ((c == 3) & (lane >= ATT_HEAD_DIM)), 1.0, 0.0).astype(BF16)
        dd = _dot_exact_right(do_ref[...].astype(F32) * o_ref[...], sel)
        c8 = lax.broadcasted_iota(jnp.int32, (1, 8), 1)
        st_ref[0] = jnp.where(c8 == 0, lse_ref[0, :, 0:1], jnp.where(c8 == 1, lse_ref[0, :, 1:2], dd))

    return pl.pallas_call(
        body, name="attn_stats", grid=(HP, S // T),
        in_specs=[pl.BlockSpec((T, LANES), lambda hp, i: (i, hp)), pl.BlockSpec((T, LANES), lambda hp, i: (i, hp)),
                  pl.BlockSpec((1, T, 2), lambda hp, i: (hp, i, 0))],
        out_specs=pl.BlockSpec((1, T, 8), lambda hp, i: (hp, i, 0)), out_shape=jax.ShapeDtypeStruct((HP, S, 8), F32),
        compiler_params=_params(2))(do, o32, lse_cols)


def _attn_bwd(qa, ka, qkv, do, stats, T):
    S = qkv.shape[0]
    nq = S // T
    HP = ATT_HEADS // 2

    def body(k0_ref, k1_ref, v_ref, q0_ref, q1_ref, do_ref, st_ref, dq_ref, dk_ref, dv_ref, dck_ref, dcq_ref, dq_acc):
        j = pl.program_id(1)
        mA = _lane_mask()
        masks = (mA, jnp.logical_not(mA))
        q_refs = (q0_ref, q1_ref)

        @pl.when(j == 0)
        def _():
            dq_acc[...] = jnp.zeros_like(dq_acc)

        kas = (k0_ref[...], k1_ref[...])
        vj = v_ref[...]
        row = lax.broadcasted_iota(jnp.int32, (T, T), 0)
        col = lax.broadcasted_iota(jnp.int32, (T, T), 1)

        def block(i, carry, diag):
            dvt, dkt0, dkt1 = carry
            off = pl.multiple_of(i * T, T)
            doi = do_ref[pl.ds(off, T), :]
            zero = jnp.zeros_like(doi)
            dkts = [dkt0, dkt1]
            for h in (0, 1):
                qh = q_refs[h][pl.ds(off, T), :]
                doh = jnp.where(masks[h], doi, zero)
                lse = st_ref[0, pl.ds(off, T), h:h + 1]
                dd = st_ref[0, pl.ds(off, T), 2 + h:3 + h]
                sc = _dot_nt(qh, kas[h])
                if diag:
                    sc = jnp.where(row >= col, sc, NEG)
                p = jnp.exp(sc - lse)
                dp = _dot_nt(doh, vj)
                ds = (p * (dp - dd)).astype(BF16)
                dvt = dvt + _dot_tn(doh, p.astype(BF16))
                dkts[h] = dkts[h] + _dot_tn(qh, ds)
                dq_acc[h, pl.ds(off, T), :] += _dot(ds, kas[h])
            return (dvt, dkts[0], dkts[1])

        z = jnp.zeros((LANES, T), F32)
        carry = block(j, (z, z, z), True)
        dvt, dkt0, dkt1 = lax.fori_loop(j + 1, nq, lambda i, c: block(i, c, False), carry)
        dv_ref[...] = dvt.T.astype(BF16)
        dk_ref[...] = jnp.where(mA, dkt0.T, dkt1.T).astype(BF16)
        ones_q = (_aug_base(0), _aug_base(1))
        dck_ref[0, 0:1, :] = -dkt0[ones_q[0]:ones_q[0] + 1, :]
        dck_ref[0, 1:2, :] = -dkt1[ones_q[1]:ones_q[1] + 1, :]

        @pl.when(j == nq - 1)
        def _():
            dq0, dq1 = dq_acc[0], dq_acc[1]
            ones_k = (_aug_base(0) + N_AUG, _aug_base(1) + N_AUG)
            dq_ref[...] = (jnp.where(mA, dq0, dq1) * (1.0 / math.sqrt(ATT_HEAD_DIM))).astype(BF16)
            dcq_ref[0, :, 0:1] = dq0[:, ones_k[0]:ones_k[0] + 1]
            dcq_ref[0, :, 1:2] = dq1[:, ones_k[1]:ones_k[1] + 1]

    vmem = (2 * (3 * T * LANES * 2 + 3 * S * LANES * 2 + S * LANES * 4 + S * LANES * (2 + 4) + 2 * T * LANES * 2 + 8 * T * 4)
            + 2 * S * LANES * 4 + 12 * T * T * 4 + (8 << 20))
    kspec = lambda h: pl.BlockSpec((T, LANES), lambda hp, j, h=h: (j, 2 * hp + h))
    qspec = lambda h: pl.BlockSpec((S, LANES), lambda hp, j, h=h: (0, 2 * hp + h))
    blk = pl.BlockSpec((T, LANES), lambda hp, j: (j, hp))
    full = pl.BlockSpec((S, LANES), lambda hp, j: (0, hp))
    return pl.pallas_call(
        body, name="attn_bwd", grid=(HP, nq),
        in_specs=[kspec(0), kspec(1), pl.BlockSpec((T, LANES), lambda hp, j: (j, 2 * HP + hp)), qspec(0), qspec(1), full,
                  pl.BlockSpec((1, S, 8), lambda hp, j: (hp, 0, 0))],
        out_specs=[full, blk, blk, pl.BlockSpec((1, 2, T), lambda hp, j: (hp, 0, j)),
                   pl.BlockSpec((1, S, 2), lambda hp, j: (hp, 0, 0))],
        out_shape=[jax.ShapeDtypeStruct((S, D_MODEL), BF16)] * 3 + [jax.ShapeDtypeStruct((HP, 2, S), F32),
                                                                     jax.ShapeDtypeStruct((HP, S, 2), F32)],
        scratch_shapes=[pltpu.VMEM((2, S, LANES), F32)], compiler_params=_params(2, vmem))(ka, ka, qkv, qa, qa, do, stats)


HALO = 8


def _shift_down(x, d, above):
    r = pltpu.roll(x, d, 0)
    head = jnp.where(lax.broadcasted_iota(jnp.int32, (HALO, 1), 0) < d, pltpu.roll(above, d, 0), r[0:HALO])
    return head if x.shape[0] == HALO else jnp.concatenate([head, r[HALO:]], axis=0)


def _shift_up(x, d, below):
    n = x.shape[0]
    r = pltpu.roll(x, n - d, 0)
    tail = jnp.where(lax.broadcasted_iota(jnp.int32, (HALO, 1), 0) >= HALO - d, pltpu.roll(below, HALO - d, 0), r[n - HALO:])
    return jnp.concatenate([r[:n - HALO], tail], axis=0)


def _conv_fwd(u, w, b, ts, tc):
    S, C = u.shape
    hb = ts // HALO

    def body(u_ref, prev_ref, w_ref, b_ref, o_ref):
        i = pl.program_id(0)
        x = u_ref[...]
        above = jnp.where(i == 0, 0.0, prev_ref[...])
        acc = b_ref[...] + w_ref[3:4, :] * x
        for k in range(SSM_CONV - 1):
            acc = acc + w_ref[k:k + 1, :] * _shift_down(x, SSM_CONV - 1 - k, above)
        o_ref[...] = acc * _sigmoid(acc)

    return pl.pallas_call(
        body, name="conv_fwd", grid=(S // ts, C // tc),
        in_specs=[pl.BlockSpec((ts, tc), lambda i, j: (i, j)),
                  pl.BlockSpec((HALO, tc), lambda i, j: (jnp.maximum(i * hb - 1, 0), j)),
                  pl.BlockSpec((SSM_CONV, tc), lambda i, j: (0, j)), pl.BlockSpec((1, tc), lambda i, j: (0, j))],
        out_specs=pl.BlockSpec((ts, tc), lambda i, j: (i, j)), out_shape=jax.ShapeDtypeStruct((S, C), F32),
        compiler_params=_params(2))(u, u, w, b)


def _conv_bwd(name, u, dy, w, b, ts, tc, col0):
    S, C = dy.shape
    cb = col0 // tc
    assert cb * tc == col0
    hb = ts // HALO
    nb = S // ts

    def body(u_ref, uprev_ref, unext_ref, dy_ref, dynext_ref, w_ref, b_ref, du_ref, dw_ref, db_ref):
        i = pl.program_id(1)
        x = u_ref[...]
        above = jnp.where(i == 0, 0.0, uprev_ref[...])
        ws = [w_ref[k:k + 1, :] for k in range(SSM_CONV)]

        def dsilu(pre):
            sg = _sigmoid(pre)
            return sg * (1.0 + pre * (1.0 - sg))

        shifted = [_shift_down(x, SSM_CONV - 1 - k, above) for k in range(SSM_CONV - 1)] + [x]
        pre = b_ref[...]
        for k in range(SSM_CONV):
            pre = pre + ws[k] * shifted[k]
        g = dy_ref[...] * dsilu(pre)
        nxt = unext_ref[...]
        tail = x[ts - HALO:, :]
        pre_n = b_ref[...] + ws[SSM_CONV - 1] * nxt
        for k in range(SSM_CONV - 1):
            pre_n = pre_n + ws[k] * _shift_down(nxt, SSM_CONV - 1 - k, tail)
        g_next = jnp.where(i == nb - 1, 0.0, dynext_ref[...] * dsilu(pre_n))
        du = ws[SSM_CONV - 1] * g
        for k in range(SSM_CONV - 1):
            du = du + ws[k] * _shift_up(g, SSM_CONV - 1 - k, g_next)
        du_ref[...] = du.astype(du_ref.dtype)
        dws = [jnp.sum(g * shifted[k], axis=0, keepdims=True) for k in range(SSM_CONV)]
        dbs = jnp.sum(g, axis=0, keepdims=True)

        @pl.when(i == 0)
        def _():
            for k in range(SSM_CONV):
                dw_ref[k:k + 1, :] = dws[k]
            db_ref[...] = dbs

        @pl.when(i > 0)
        def _():
            for k in range(SSM_CONV):
                dw_ref[k:k + 1, :] += dws[k]
            db_ref[...] += dbs

    nxt = lambda off: (lambda j, i: (jnp.minimum((i + 1) * hb, S // HALO - 1), j + off))
    return pl.pallas_call(
        body, name=name, grid=(C // tc, nb),
        in_specs=[pl.BlockSpec((ts, tc), lambda j, i: (i, j + cb)),
                  pl.BlockSpec((HALO, tc), lambda j, i: (jnp.maximum(i * hb - 1, 0), j + cb)),
                  pl.BlockSpec((HALO, tc), nxt(cb)),
                  pl.BlockSpec((ts, tc), lambda j, i: (i, j)),
                  pl.BlockSpec((HALO, tc), nxt(0)),
                  pl.BlockSpec((SSM_CONV, tc), lambda j, i: (0, j + cb)), pl.BlockSpec((1, tc), lambda j, i: (0, j + cb))],
        out_specs=[pl.BlockSpec((ts, tc), lambda j, i: (i, j)), pl.BlockSpec((SSM_CONV, tc), lambda j, i: (0, j)),
                   pl.BlockSpec((1, tc), lambda j, i: (0, j))],
        out_shape=[jax.ShapeDtypeStruct((S, C), BF16), jax.ShapeDtypeStruct((SSM_CONV, C), F32), jax.ShapeDtypeStruct((1, C), F32)],
        compiler_params=_params(2))(u, u, u, dy, dy, w, b)


def _head_sum():
    lane = jnp.right_shift(lax.broadcasted_iota(jnp.int32, (GROUP_LANES, 8), 0), 6)
    r = lax.broadcasted_iota(jnp.int32, (GROUP_LANES, 8), 1)
    return jnp.where(lane == r, 1.0, 0.0).astype(BF16)


def _head_expand():
    r = lax.broadcasted_iota(jnp.int32, (8, GROUP_LANES), 0)
    c = jnp.right_shift(lax.broadcasted_iota(jnp.int32, (8, GROUP_LANES), 1), 6)
    return jnp.where(r == c, 1.0, 0.0).astype(BF16)


def _ssd_common(dtc_ref, dtr_ref, bias_r, alog_b, bias_c, alog_c, L):
    a_b = -jnp.exp(alog_b)
    dt = _dot_exact_right(_softplus(dtc_ref[0] + bias_r), _head_expand())
    acum = _dot_exact_left(_tri(L, False), dt * a_b)
    a_c = -jnp.exp(alog_c)
    dtr = _softplus(dtr_ref[0] + bias_c)
    acum_r = _dot_exact_right(dtr * a_c, _tri(L, True))
    return a_b, dt, acum, acum_r


def _ssd_specs(L, nc, rev):
    cc = (lambda c: nc - 1 - c) if rev else (lambda c: c)
    G = SSM_GROUPS
    blk = pl.BlockSpec((L, GROUP_LANES), lambda g, c: (cc(c), g))
    dtc = pl.BlockSpec((1, L, 8), lambda g, c: (g, cc(c), 0))
    rowv = pl.BlockSpec((1, 1, 8), lambda g, c: (g, 0, 0))
    xs = blk
    bm = pl.BlockSpec((L, SSM_STATE), lambda g, c: (cc(c), SSM_INNER // SSM_STATE + g))
    cm = pl.BlockSpec((L, SSM_STATE), lambda g, c: (cc(c), SSM_INNER // SSM_STATE + G + g))
    dtr = pl.BlockSpec((1, 8, L), lambda g, c: (g, 0, cc(c)))
    vec = pl.BlockSpec((1, GROUP_LANES), lambda g, c: (0, g))
    colv = pl.BlockSpec((1, 8, 1), lambda g, c: (g, 0, 0))
    hs = pl.BlockSpec((1, 1, SSM_STATE, GROUP_LANES), lambda g, c: (g, cc(c), 0, 0))
    return blk, xs, bm, cm, dtc, dtr, vec, rowv, colv, hs


def _ssd_fwd(xbc, z, dtc, dtr, bias_r, alog_b, dskip_b, normw, bias_c, alog_c, L):
    S = z.shape[0]
    nc = S // L
    blk, xs, bm, cm, dtcs, dtrs, vec, rowv, colv, hs = _ssd_specs(L, nc, False)

    def body(x_ref, b_ref, c_ref, z_ref, dtc_ref, dtr_ref, bias_ref, alog_ref, dskip_ref, nw_ref, biasc_ref, alogc_ref,
             y_ref, ssm_ref, hs_ref, h_scr):
        c = pl.program_id(1)

        @pl.when(c == 0)
        def _():
            h_scr[...] = jnp.zeros_like(h_scr)

        mA = _lane_mask()
        a_b, dt, acum, acum_r = _ssd_common(dtc_ref, dtr_ref, bias_ref[0], alog_ref[...], biasc_ref[0], alogc_ref[0], L)
        x = x_ref[...]
        cb, bb = c_ref[...].astype(BF16), b_ref[...].astype(BF16)
        hprev = h_scr[...]
        hs_ref[0, 0] = hprev
        xdt = x * dt
        xdt_b = xdt.astype(BF16)
        gmat = _dot_nt(cb, bb)
        row = lax.broadcasted_iota(jnp.int32, (L, L), 0)
        col = lax.broadcasted_iota(jnp.int32, (L, L), 1)
        parts = []
        for p in range(GROUP_LANES // LANES):
            xp = xdt_b[:, p * LANES:(p + 1) * LANES]
            yd = []
            for hh in (0, 1):
                r = 2 * p + hh
                acol = acum[:, r * ATT_HEAD_DIM:r * ATT_HEAD_DIM + 1]
                arow = acum_r[r:r + 1, :]
                lm = jnp.exp(jnp.where(row >= col, acol - arow, NEG))
                yd.append(_dot((gmat * lm).astype(BF16), xp))
            parts.append(jnp.where(mA, yd[0], yd[1]))
        ydiag = jnp.concatenate(parts, axis=1)
        yoff = jnp.exp(acum) * _dot(cb, hprev.astype(BF16))
        y = ydiag + yoff + dskip_ref[...] * x
        aend = acum[L - 1:L, :]
        wgt = (jnp.exp(aend - acum) * xdt).astype(BF16)
        h_scr[...] = jnp.exp(aend) * hprev + _dot_tn(bb, wgt)
        y_ref[...] = y
        zz = z_ref[...].astype(F32)
        u = y * (zz * _sigmoid(zz))
        rs = lax.rsqrt(jnp.mean(u * u, axis=1, keepdims=True) + RMS_EPS)
        ssm_ref[...] = (u * rs * nw_ref[...]).astype(BF16)

    return pl.pallas_call(
        body, name="ssd_fwd", grid=(SSM_GROUPS, nc),
        in_specs=[xs, bm, cm, blk, dtcs, dtrs, rowv, vec, vec, vec, colv, colv],
        out_specs=[blk, blk, hs],
        out_shape=[jax.ShapeDtypeStruct((S, SSM_INNER), F32), jax.ShapeDtypeStruct((S, SSM_INNER), BF16),
                   jax.ShapeDtypeStruct((SSM_GROUPS, nc, SSM_STATE, GROUP_LANES), F32)],
        scratch_shapes=[pltpu.VMEM((SSM_STATE, GROUP_LANES), F32)],
        compiler_params=_params(2, 48 << 20))(xbc, xbc, xbc, z, dtc, dtr, bias_r, alog_b, dskip_b, normw, bias_c, alog_c)


def _ssd_bwd(xbc, z, y, dssm, hs_all, dtc, dtr, bias_r, alog_r, alog_b, dskip_b, normw, bias_c, alog_c, L):
    S = z.shape[0]
    nc = S // L
    blk, xs, bm, cm, dtcs, dtrs, vec, rowv, colv, hs = _ssd_specs(L, nc, True)

    def body(x_ref, b_ref, c_ref, z_ref, y_ref, dssm_ref, hs_ref, dtc_ref, dtr_ref, bias_ref, alogr_ref, alog_ref, dskip_ref, nw_ref,
             biasc_ref, alogc_ref,
             dx_ref, db_ref, dc_ref, dz_ref, ddt_ref, dnw_ref, ddskip_ref, dalog_ref, dbias_ref, dh_scr):
        c = pl.program_id(1)

        @pl.when(c == 0)
        def _():
            dh_scr[...] = jnp.zeros_like(dh_scr)

        mA = _lane_mask()
        masks = (mA, jnp.logical_not(mA))
        a_b, dt, acum, acum_r = _ssd_common(dtc_ref, dtr_ref, bias_ref[0], alog_ref[...], biasc_ref[0], alogc_ref[0], L)
        x, zz, y, dssm = x_ref[...], z_ref[...].astype(F32), y_ref[...], dssm_ref[...]
        cb, bb = c_ref[...].astype(BF16), b_ref[...].astype(BF16)
        hprev = hs_ref[0, 0]
        hb = hprev.astype(BF16)
        ds = dh_scr[...]
        dsb = ds.astype(BF16)
        dskip = dskip_ref[...]
        aend = acum[L - 1:L, :]
        e_a, e_end = jnp.exp(acum), jnp.exp(aend)
        dte = jnp.exp(aend - acum)
        xdt = x * dt
        xdt_b = xdt.astype(BF16)
        sg = _sigmoid(zz)
        sz = zz * sg
        u = y * sz
        rs = lax.rsqrt(jnp.mean(u * u, axis=1, keepdims=True) + RMS_EPS)
        un = u * rs
        dun = dssm * nw_ref[...]
        du = rs * (dun - un * jnp.mean(dun * un, axis=1, keepdims=True))
        dy = du * sz
        dz_ref[...] = (du * y * sg * (1.0 + zz * (1.0 - sg))).astype(dz_ref.dtype)
        dy_b = dy.astype(BF16)
        dch_b = (dy * e_a).astype(BF16)
        dc = _dot_nt(dch_b, hb)
        dhprev = _dot_tn(cb, dch_b)
        gt = _dot_nt(bb, cb)
        row = lax.broadcasted_iota(jnp.int32, (L, L), 0)
        col = lax.broadcasted_iota(jnp.int32, (L, L), 1)
        dgt = jnp.zeros((L, L), F32)
        parts = []
        for p in range(GROUP_LANES // LANES):
            xp = xdt_b[:, p * LANES:(p + 1) * LANES]
            dyp = dy_b[:, p * LANES:(p + 1) * LANES]
            zero = jnp.zeros_like(dyp)
            acc = None
            for hh in (0, 1):
                r = 2 * p + hh
                acol = acum[:, r * ATT_HEAD_DIM:r * ATT_HEAD_DIM + 1]
                arow = acum_r[r:r + 1, :]
                lmt = jnp.exp(jnp.where(row <= col, arow - acol, NEG))
                dyh = jnp.where(masks[hh], dyp, zero)
                part = _dot((gt * lmt).astype(BF16), dyh)
                acc = part if acc is None else acc + part
                dgt = dgt + _dot_nt(xp, dyh) * lmt
            parts.append(acc)
        dxdt_diag = jnp.concatenate(parts, axis=1)
        dgt_b = dgt.astype(BF16)
        db = _dot(dgt_b, cb)
        dc = dc + _dot_tn(dgt_b, bb)
        dxdt_state = dte * _dot(bb, dsb)
        db = db + _dot_nt((dte * xdt).astype(BF16), dsb)
        dxdt = dxdt_diag + dxdt_state
        dy_r, xdt_r = dy_b.astype(F32), xdt_b.astype(F32)
        dac = dy_r * (y - dskip * x) - xdt_r * dxdt
        tail = jnp.sum(xdt_r * dxdt_state, axis=0, keepdims=True) + e_end * jnp.sum(ds * hprev, axis=0, keepdims=True)
        rowl = lax.broadcasted_iota(jnp.int32, (L, 1), 0)
        dac = dac + jnp.where(rowl == L - 1, tail, 0.0)
        rc = _dot_exact_left(_tri(L, True), dac)
        hsum = _head_sum()
        hs1 = _dot_exact_right(dxdt * x, hsum, 2)
        hs2 = _dot_exact_right(rc, hsum, 2)
        a8 = -jnp.exp(alogr_ref[0])
        dtraw8 = dtc_ref[0] + bias_ref[0]
        ddtraw = (hs1 + a8 * hs2) * _sigmoid(dtraw8)
        dx_ref[...] = dskip * dy + dxdt * dt
        db_ref[...] = db
        dc_ref[...] = dc
        ddt_ref[0] = ddtraw
        dh_scr[...] = e_end * ds + dhprev
        sums = (jnp.sum(dssm * un, axis=0, keepdims=True), jnp.sum(dy * x, axis=0, keepdims=True))
        refs = (dnw_ref, ddskip_ref)
        sums8 = (a8 * jnp.sum(hs2 * _softplus(dtraw8), axis=0, keepdims=True), jnp.sum(ddtraw, axis=0, keepdims=True))
        refs8 = (dalog_ref, dbias_ref)

        @pl.when(c == 0)
        def _():
            for r, v in zip(refs, sums):
                r[...] = v
            for r, v in zip(refs8, sums8):
                r[0] = v

        @pl.when(c > 0)
        def _():
            for r, v in zip(refs, sums):
                r[...] += v
            for r, v in zip(refs8, sums8):
                r[0] += v

    nbc = pl.BlockSpec((L, SSM_STATE), lambda g, c: (nc - 1 - c, g))
    return pl.pallas_call(
        body, name="ssd_bwd", grid=(SSM_GROUPS, nc),
        in_specs=[xs, bm, cm, blk, blk, blk, hs, dtcs, dtrs, rowv, rowv, vec, vec, vec, colv, colv],
        out_specs=[blk, nbc, nbc, blk, dtcs, vec, vec, rowv, rowv],
        out_shape=[jax.ShapeDtypeStruct((S, SSM_INNER), F32), jax.ShapeDtypeStruct((S, SSM_GROUPS * SSM_STATE), F32),
                   jax.ShapeDtypeStruct((S, SSM_GROUPS * SSM_STATE), F32), jax.ShapeDtypeStruct((S, SSM_INNER), BF16),
                   jax.ShapeDtypeStruct((SSM_GROUPS, S, 8), F32)] + [jax.ShapeDtypeStruct((1, SSM_INNER), F32)] * 2
                  + [jax.ShapeDtypeStruct((SSM_GROUPS, 1, 8), F32)] * 2,
        scratch_shapes=[pltpu.VMEM((SSM_STATE, GROUP_LANES), F32)],
        compiler_params=_params(2, 56 << 20))(xbc, xbc, xbc, z, y, dssm, hs_all, dtc, dtr, bias_r, alog_r, alog_b, dskip_b,
                                              normw, bias_c, alog_c)


def _place():
    return lax.axis_index("x"), lax.axis_index("y"), lax.axis_index("c")


def _other_chips(x, y):
    return [(1 - x, y), (x, 1 - y), (1 - x, 1 - y)]


def _half_rows(rows, which):
    hr = rows // 2
    if isinstance(which, int):
        return pl.ds(which * hr, hr)
    return pl.ds(pl.multiple_of(which * hr, 8), hr)


def _chip_gather(name, shards, split):
    n = len(shards)
    ANY = pl.BlockSpec(memory_space=pl.ANY)

    def body(*refs):
        ins, outs = refs[:n], refs[n:2 * n]
        send, recv, fsend, frecv = refs[2 * n:]
        x, y, c = _place()
        me = 2 * x + y
        sibling = (x, y, 1 - c)
        chips = _other_chips(x, y)

        def piece(a, chip_idx, which):
            if split[a]:
                return outs[a].at[chip_idx, _half_rows(shards[a].shape[0], which)]
            return outs[a].at[chip_idx]

        def ici(k, a, to_chip, src_chip):
            src = ins[a].at[_half_rows(shards[a].shape[0], c)] if split[a] else ins[a]
            return pltpu.make_async_remote_copy(src_ref=src, dst_ref=piece(a, src_chip, c), send_sem=send.at[k, a],
                                                recv_sem=recv.at[k, a], device_id=(*to_chip, c), device_id_type=MESH)

        def fwd(k, a, src_chip, which):
            return pltpu.make_async_remote_copy(src_ref=piece(a, src_chip, which), dst_ref=piece(a, src_chip, which),
                                                send_sem=fsend.at[k, a], recv_sem=frecv.at[k, a], device_id=sibling,
                                                device_id_type=MESH)

        sends = []
        for k, chip in enumerate(chips):
            for a in range(n):
                cp = ici(k, a, chip, me)
                cp.start()
                sends.append(cp)
        for k, (ox, oy) in enumerate(chips):
            src = 2 * ox + oy
            for a in range(n):
                ici(k, a, (ox, oy), src).wait_recv()
                if split[a]:
                    cp = fwd(k, a, src, c)
                    cp.start()
                    sends.append(cp)
        for k, (ox, oy) in enumerate(chips):
            for a in range(n):
                if split[a]:
                    fwd(k, a, 2 * ox + oy, 1 - c).wait_recv()
        for cp in sends:
            cp.wait_send()

    sem = pltpu.SemaphoreType.DMA((3, n))
    return pl.pallas_call(
        body, name=name, in_specs=[ANY] * n, out_specs=[ANY] * n,
        out_shape=[jax.ShapeDtypeStruct((4,) + s.shape, s.dtype) for s in shards],
        scratch_shapes=[sem, sem, sem, sem])(*shards)


def _chip_copies_start(name, srcs, per_chip_src, after):
    n = len(srcs)
    HBM = pl.BlockSpec(memory_space=pltpu.HBM)
    SEM = pl.BlockSpec(memory_space=pltpu.SEMAPHORE)
    lands = [pltpu.with_memory_space_constraint(lax.empty(a.shape if per_chip_src else (4,) + a.shape, a.dtype), pltpu.HBM)
             for a in srcs]

    def body(*refs):
        ins, land = refs[:n], refs[n:2 * n]
        send, recv = refs[2 * n + 1], refs[2 * n + 2]
        token = refs[-1]
        x, y, c = _place()
        me = 2 * x + y
        for k, (ox, oy) in enumerate(_other_chips(x, y)):
            for a in range(n):
                src = ins[a].at[2 * ox + oy] if per_chip_src else ins[a]
                pltpu.make_async_remote_copy(src_ref=src, dst_ref=land[a].at[me], send_sem=send.at[k * n + a], recv_sem=recv.at[k * n + a],
                                             device_id=(ox, oy, c), device_id_type=MESH).start()
        token[...] = jnp.zeros_like(token)

    sem = pltpu.SemaphoreType.DMA((3 * n,))
    res = pl.pallas_call(
        body, name=name,
        out_shape=[sem, sem] + [pltpu.HBM(a.shape, a.dtype) for a in srcs] + [pltpu.HBM(b.shape, b.dtype) for b in lands]
                  + [jax.ShapeDtypeStruct((8, LANES), F32)],
        in_specs=[HBM] * (2 * n) + [pl.BlockSpec(memory_space=pl.ANY)],
        out_specs=[SEM, SEM] + [HBM] * (2 * n) + [pl.BlockSpec(memory_space=pltpu.VMEM)],
        input_output_aliases={k: 2 + k for k in range(2 * n)},
        compiler_params=pltpu.CompilerParams(has_side_effects=pltpu.SideEffectType.DATAFLOW_SIDE_EFFECTING),
    )(*[pltpu.with_memory_space_constraint(a, pltpu.HBM) for a in srcs], *lands, after)
    return res[:-1], res[-1]


def _chip_copies_wait(name, started, per_chip_src, after):
    send, recv = started[0], started[1]
    n = (len(started) - 2) // 2
    srcs, lands = started[2:2 + n], started[2 + n:]
    HBM = pl.BlockSpec(memory_space=pltpu.HBM)
    SEM = pl.BlockSpec(memory_space=pltpu.SEMAPHORE)

    def body(*refs):
        ins, land = refs[:n], refs[n:2 * n]
        send_sem, recv_sem = refs[2 * n], refs[2 * n + 1]
        x, y, c = _place()
        me = 2 * x + y
        for k, (ox, oy) in enumerate(_other_chips(x, y)):
            for a in range(n):
                src = ins[a].at[me] if per_chip_src else ins[a]
                cp = pltpu.make_async_remote_copy(src_ref=src, dst_ref=land[a].at[2 * ox + oy], send_sem=send_sem.at[k * n + a],
                                                  recv_sem=recv_sem.at[k * n + a], device_id=(ox, oy, c), device_id_type=MESH)
                cp.wait_send()
                cp.wait_recv()

    res = pl.pallas_call(
        body, name=name,
        out_shape=[pltpu.HBM(a.shape, a.dtype) for a in srcs] + [pltpu.HBM(b.shape, b.dtype) for b in lands],
        in_specs=[HBM] * (2 * n) + [SEM, SEM, pl.BlockSpec(memory_space=pl.ANY)], out_specs=[HBM] * (2 * n),
        input_output_aliases={k: k for k in range(2 * n)},
        compiler_params=pltpu.CompilerParams(has_side_effects=pltpu.SideEffectType.DATAFLOW_SIDE_EFFECTING),
    )(*srcs, *lands, send, recv, after)
    return res[n:]


def _half_to_sibling(name, blocks):
    n = len(blocks)
    ANY = pl.BlockSpec(memory_space=pl.ANY)

    def body(*refs):
        ins, outs = refs[:n], refs[n:2 * n]
        send, recv = refs[2 * n:]
        x, y, c = _place()
        cps = [pltpu.make_async_remote_copy(src_ref=ins[a].at[:, _half_rows(blocks[a].shape[1], 1 - c)], dst_ref=outs[a],
                                            send_sem=send.at[a], recv_sem=recv.at[a], device_id=(x, y, 1 - c),
                                            device_id_type=MESH) for a in range(n)]
        for cp in cps:
            cp.start()
        for cp in cps:
            cp.wait_recv()
        for cp in cps:
            cp.wait_send()

    return pl.pallas_call(
        body, name=name, in_specs=[ANY] * n, out_specs=[ANY] * n,
        out_shape=[jax.ShapeDtypeStruct((4, b.shape[1] // 2, b.shape[2]), b.dtype) for b in blocks],
        scratch_shapes=[pltpu.SemaphoreType.DMA((n,)), pltpu.SemaphoreType.DMA((n,))])(*blocks)


def _sibling_swap(name, arrs):
    n = len(arrs)
    ANY = pl.BlockSpec(memory_space=pl.ANY)

    def body(*refs):
        ins, outs = refs[:n], refs[n:2 * n]
        send, recv = refs[2 * n:]
        x, y, c = _place()
        cps = [pltpu.make_async_remote_copy(src_ref=ins[a], dst_ref=outs[a], send_sem=send.at[a], recv_sem=recv.at[a],
                                            device_id=(x, y, 1 - c), device_id_type=MESH) for a in range(n)]
        for cp in cps:
            cp.start()
        for cp in cps:
            cp.wait_recv()
        for cp in cps:
            cp.wait_send()

    return pl.pallas_call(
        body, name=name, in_specs=[ANY] * n, out_specs=[ANY] * n,
        out_shape=[jax.ShapeDtypeStruct(a.shape, a.dtype) for a in arrs],
        scratch_shapes=[pltpu.SemaphoreType.DMA((n,)), pltpu.SemaphoreType.DMA((n,))])(*arrs)


N_DEV = 8


def _all_sum_small(vec):
    P = vec.shape[1]

    def body(v_ref, o_ref, buf, send, recv):
        x, y, c = _place()
        me = 4 * x + 2 * y + c
        buf[me] = v_ref[...]

        def peer(r):
            return ((1 - x) if (r >> 2) & 1 else x, (1 - y) if (r >> 1) & 1 else y, (1 - c) if r & 1 else c)

        sends = []
        for r in range(1, N_DEV):
            cp = pltpu.make_async_remote_copy(src_ref=v_ref, dst_ref=buf.at[me], send_sem=send.at[r], recv_sem=recv.at[r],
                                              device_id=peer(r), device_id_type=MESH)
            cp.start()
            sends.append(cp)
        for r in range(1, N_DEV):
            px, py, pc = peer(r)
            pltpu.make_async_remote_copy(src_ref=v_ref, dst_ref=buf.at[4 * px + 2 * py + pc], send_sem=send.at[r],
                                         recv_sem=recv.at[r], device_id=(px, py, pc), device_id_type=MESH).wait_recv()
        for cp in sends:
            cp.wait_send()
        tot = buf[0]
        for d in range(1, N_DEV):
            tot = tot + buf[d]
        o_ref[...] = tot

    return pl.pallas_call(
        body, name="all_sum_small", in_specs=[pl.BlockSpec(memory_space=pltpu.VMEM)],
        out_specs=pl.BlockSpec(memory_space=pltpu.VMEM), out_shape=jax.ShapeDtypeStruct((1, P), F32),
        scratch_shapes=[pltpu.VMEM((N_DEV, 1, P), F32), pltpu.SemaphoreType.DMA((N_DEV,)), pltpu.SemaphoreType.DMA((N_DEV,))],
    )(vec)


def _half_sum(name, blocks, theirs, core, tr):
    _, R, C = blocks.shape
    hr = R // 2
    nb = hr // tr
    assert nb * tr == hr

    def body(c_ref, a_ref, b_ref, o_ref):
        o_ref[...] = (a_ref[...] + b_ref[...]).astype(BF16)

    grid_spec = pltpu.PrefetchScalarGridSpec(
        num_scalar_prefetch=1, grid=(4, nb),
        in_specs=[pl.BlockSpec((1, tr, C), lambda b, i, c_ref: (b, c_ref[0] * nb + i, 0)),
                  pl.BlockSpec((1, tr, C), lambda b, i, c_ref: (b, i, 0))],
        out_specs=pl.BlockSpec((1, tr, C), lambda b, i, c_ref: (b, i, 0)))
    return pl.pallas_call(body, name=name, grid_spec=grid_spec, out_shape=jax.ShapeDtypeStruct((4, hr, C), BF16),
                          compiler_params=_params(2, 40 << 20))(core, blocks, theirs)


def _sum4(name, stack, mine, chip, tr):
    _, R, C = stack.shape

    def body(chip_ref, s_ref, m_ref, o_ref):
        t = [jnp.where(chip_ref[0] == j, m_ref[j], s_ref[j]).astype(F32) for j in range(4)]
        o_ref[...] = ((t[0] + t[1]) + t[2]) + t[3]

    blk = pl.BlockSpec((4, tr, C), lambda i, chip_ref: (0, i, 0))
    grid_spec = pltpu.PrefetchScalarGridSpec(num_scalar_prefetch=1, grid=(R // tr,), in_specs=[blk, blk],
                                             out_specs=pl.BlockSpec((tr, C), lambda i, chip_ref: (i, 0)))
    return pl.pallas_call(body, name=name, grid_spec=grid_spec, out_shape=jax.ShapeDtypeStruct((R, C), F32),
                          compiler_params=_params(1, 40 << 20))(chip, stack, mine)


def _adamw_math(w, m, v, g):
    c1 = 1.0 - ADAM_B1 ** ADAM_STEP
    c2 = 1.0 - ADAM_B2 ** ADAM_STEP
    nm = ADAM_B1 * m + (1.0 - ADAM_B1) * g
    nv = ADAM_B2 * v + (1.0 - ADAM_B2) * (g * g)
    return -ADAM_LR * ((nm / c1) / (jnp.sqrt(nv / c2) + ADAM_EPS) + ADAM_WD * w), nm, nv


def _adamw(name, w, m, v, g, tr):
    R, C = w.shape

    def body(w_ref, m_ref, v_ref, ga_ref, g_ref, d_ref, nm_ref, nv_ref):
        g = ga_ref[...]
        g_ref[...] = g
        d_ref[...], nm_ref[...], nv_ref[...] = _adamw_math(w_ref[...], m_ref[...], v_ref[...], g)

    spec = pl.BlockSpec((tr, C), lambda i: (i, 0))
    return pl.pallas_call(body, name=name, grid=(R // tr,), in_specs=[spec] * 4, out_specs=[spec] * 4,
                          out_shape=[jax.ShapeDtypeStruct((R, C), F32)] * 4, compiler_params=_params(1, 40 << 20))(w, m, v, g)


def _adamw_flat(name, w, m, v, g, tr):
    R, C = w.shape

    def body(w_ref, m_ref, v_ref, g_ref, d_ref, nm_ref, nv_ref):
        d_ref[...], nm_ref[...], nv_ref[...] = _adamw_math(w_ref[...], m_ref[...], v_ref[...], g_ref[...])

    spec = pl.BlockSpec((tr, C), lambda i: (i, 0))
    return pl.pallas_call(body, name=name, grid=(R // tr,), in_specs=[spec] * 4, out_specs=[spec] * 3,
                          out_shape=[jax.ShapeDtypeStruct((R, C), F32)] * 3, compiler_params=_params(1, 48 << 20))(w, m, v, g)


def _adamw_halves(name, w, m, v, mine, theirs, core, tr):
    _, R, C = w.shape
    nb = (R // 2) // tr
    assert 2 * nb * tr == R

    def body(c_ref, w_ref, m_ref, v_ref, a_ref, b_ref, g_ref, d_ref, nm_ref, nv_ref):
        g = jnp.where((pl.program_id(0) // nb) == c_ref[0], a_ref[...], b_ref[...])
        g_ref[0] = g
        d_ref[0], nm_ref[0], nv_ref[0] = _adamw_math(w_ref[0], m_ref[0], v_ref[0], g)

    spec = pl.BlockSpec((1, tr, C), lambda i, c_ref: (0, i, 0))
    half = lambda own: pl.BlockSpec((tr, C), lambda i, c_ref, own=own: (
        jnp.clip(i - (c_ref[0] if own else 1 - c_ref[0]) * nb, 0, nb - 1), 0))
    grid_spec = pltpu.PrefetchScalarGridSpec(num_scalar_prefetch=1, grid=(R // tr,),
                                             in_specs=[spec, spec, spec, half(True), half(False)], out_specs=[spec] * 4)
    return pl.pallas_call(body, name=name, grid_spec=grid_spec, out_shape=[jax.ShapeDtypeStruct((1, R, C), F32)] * 4,
                          compiler_params=_params(1, 40 << 20))(core, w, m, v, mine, theirs)


def _row_tile(rows, cols, budget_bytes=1 << 20, mult=8):
    best = None
    for t in range(mult, rows + 1, mult):
        if rows % t == 0 and t * cols * 4 <= budget_bytes:
            best = t
    return best if best is not None else rows


def _ln_fwd(r, g, b):
    mu = jnp.mean(r, axis=1, keepdims=True)
    xc = r - mu
    rstd = lax.rsqrt(jnp.mean(xc * xc, axis=1, keepdims=True) + LN_EPS)
    xhat = xc * rstd
    return xhat * g + b, xhat, rstd


def _ln_bwd(dy, xhat, rstd, g):
    dxh = dy * g
    return rstd * (dxh - jnp.mean(dxh, axis=1, keepdims=True) - xhat * jnp.mean(dxh * xhat, axis=1, keepdims=True))


def _col_segments(sections, width):
    out, cur, room = [], [], width
    for arr, lo, hi in sections:
        while lo < hi:
            take = min(room, hi - lo)
            cur.append((arr, lo, lo + take))
            lo, room = lo + take, room - take
            if room == 0:
                out.append(cur)
                cur, room = [], width
    assert not cur
    return out


def _to_chip_blocks_cols(a):
    R, C4 = a.shape
    return a.reshape(R, 4, C4 // 4).transpose(1, 0, 2)


def _from_chip_blocks_cols(a):
    return a.transpose(1, 0, 2).reshape(a.shape[1], 4 * a.shape[2])


def kernel(x, w_in, b_forget, conv_w, conv_b, dt_bias, a_log, d_skip, ssm_norm_w, w_proj_attn, w_proj_ssm, b_gates, w_out, ln1_g, ln1_b, w_ffn_gate, w_ffn_up, w_ffn_down, ln2_g, ln2_b, loss_target, m_w_in, m_b_forget, m_conv_w, m_conv_b, m_dt_bias, m_a_log, m_d_skip, m_ssm_norm_w, m_w_proj_attn, m_w_proj_ssm, m_b_gates, m_w_out, m_ln1_g, m_ln1_b, m_w_ffn_gate, m_w_ffn_up, m_w_ffn_down, m_ln2_g, m_ln2_b, v_w_in, v_b_forget, v_conv_w, v_conv_b, v_dt_bias, v_a_log, v_d_skip, v_ssm_norm_w, v_w_proj_attn, v_w_proj_ssm, v_b_gates, v_w_out, v_ln1_g, v_ln1_b, v_w_ffn_gate, v_w_ffn_up, v_w_ffn_down, v_ln2_g, v_ln2_b):
    S = x.shape[1]
    D = D_MODEL
    TM, TM2, TM3, TA, AQF, LC, CV, TS, TB = (min(TILES[k], S) for k in ("TM", "TM2", "TM3", "TA", "AQF", "LC", "CV", "TS", "TB"))
    xf = x[0]
    tgt = loss_target[0]
    xb = xf.astype(BF16)

    shards = [w_in[0].astype(BF16), conv_w[0], w_proj_attn[0].astype(BF16), w_proj_ssm[0].astype(BF16), w_out[0].astype(BF16),
              w_ffn_gate[0].astype(BF16), w_ffn_up[0].astype(BF16), w_ffn_down[0].astype(BF16)]
    chip = 2 * lax.axis_index("x") + lax.axis_index("y")
    own = lambda gathered, mine: [lax.dynamic_update_slice(g, sh[None], (chip, 0, 0)) for g, sh in zip(gathered, mine)]
    g_in, g_cw = own(_chip_gather("gather_w_in", shards[:2], [True, False]), shards[:2])
    later, gather_token = _chip_copies_start("gather_rest_start", shards[2:], False, g_cw)
    shard_w = IN_WIDTH // 4

    def w_cols(lo, hi):
        return [g_in[j][:, max(lo, j * shard_w) - j * shard_w:min(hi, (j + 1) * shard_w) - j * shard_w]
                for j in range(4) if max(lo, j * shard_w) < min(hi, (j + 1) * shard_w)]

    w_re = jnp.concatenate(w_cols(0, 3072) + w_cols(3088, 5136) + w_cols(5136, 8208) + w_cols(8240, 10288)
                           + w_cols(3072, 3088) + w_cols(8208, 8240) + [jnp.zeros((D, 80), BF16)], axis=1)
    conv_w_full = _from_chip_blocks_cols(g_cw)

    def plain(accs, rows, vecs, j):
        return [accs[0]], []

    def q_scaled(accs, rows, vecs, j):
        return [accs[0] * jnp.where(j * 512 < D, 1.0 / math.sqrt(ATT_HEAD_DIM), 1.0)], []

    qkv, = _mm("proj_qkv", S, 3072, TM, 512, [(xb, D, 0)], [(w_re, 0)], [(0, 0)], q_scaled, [(3072, BF16, 0)],
               after=[gather_token])
    z, = _mm("proj_z", S, 2048, TM, 512, [(xb, D, 0)], [(w_re, RE_Z // 512)], [(0, 0)], plain, [(2048, BF16, 0)])
    xbc_raw, = _mm("proj_xbc", S, 3072, TM, 512, [(xb, D, 0)], [(w_re, RE_XBC // 512)], [(0, 0)], plain, [(3072, F32, 0)])
    gl, = _mm("proj_gate", S, 2048, TM, 512, [(xb, D, 0)], [(w_re, RE_GATE // 512)], [(0, 0)], plain, [(2048, BF16, 0)])
    small, = _mm("proj_small", S, 128, TM, 128, [(xb, D, 0)], [(w_re, RE_SMALL // 128)], [(0, 0)], plain, [(128, F32, 0)])

    bvec = jnp.concatenate([b_forget, jnp.zeros((1, LANES - ATT_HEADS), F32)], axis=1)
    cum = _cum_fwd(small, bvec, TB)[:, :ATT_HEADS]
    cum_cols = cum.reshape(S, 8, 2).transpose(1, 0, 2)
    qa, ka = _attn_prep(qkv, cum_cols, TM)
    o, o32, lse_rows = _attn_fwd(qa, ka, qkv, AQF, TA)

    cb_row = conv_b
    xbc = _conv_fwd(xbc_raw, conv_w_full, cb_row, CV, 512)
    dt_raw = small[:, 16:48]
    dtc = dt_raw.reshape(S, SSM_GROUPS, 8).transpose(1, 0, 2)
    dtr = dt_raw.T.reshape(SSM_GROUPS, 8, S)
    bias_r = dt_bias.reshape(SSM_GROUPS, 1, 8)
    alog_b = jnp.repeat(a_log, ATT_HEAD_DIM, axis=1)
    dskip_b = jnp.repeat(d_skip, ATT_HEAD_DIM, axis=1)
    bias_c = dt_bias.reshape(SSM_GROUPS, 8, 1)
    alog_c = a_log.reshape(SSM_GROUPS, 8, 1)
    y_ssd, ssm, hs_all = _ssd_fwd(xbc, z, dtc, dtr, bias_r, alog_b, dskip_b, ssm_norm_w, bias_c, alog_c, LC)

    def merge(accs, rows, vecs, j):
        g0, g1 = _sigmoid(rows[0].astype(F32) + vecs[0]), _sigmoid(rows[1].astype(F32) + vecs[1])
        return [g0 * accs[0] + g1 * accs[1], accs[0], accs[1]], []

    g_pa, g_ps, g_out, g_fg, g_fu, g_fd = own(_chip_copies_wait("gather_rest_wait", later, False, o), shards[2:])
    wpa, wps, wout = g_pa.reshape(D, D), g_ps.reshape(SSM_INNER, D), g_out.reshape(D, D)
    wfg, wfu, wfd = _from_chip_blocks_cols(g_fg), _from_chip_blocks_cols(g_fu), g_fd.reshape(FFN_HIDDEN, D)
    mix, attn_d, ssm_d = _mm("merge", S, D, TM, 512, [(o, D, 0), (ssm, SSM_INNER, 0)], [(wpa, 0), (wps, 0)], [(0, 0), (1, 1)],
                             merge, [(D, BF16, 0), (D, BF16, 0), (D, BF16, 0)], rows=[(gl, 0), (gl, 2)],
                             vecs_n=[(b_gates, 0), (b_gates, 2)])

    def out_ln1(accs, rows, vecs, j):
        r1 = ALPHA * rows[0] + accs[0]
        h1, _, _ = _ln_fwd(r1, vecs[0], vecs[1])
        return [r1, h1, h1], []

    r1, h1, h1b = _mm("out_ln1", S, D, TM2, D, [(mix, D, 0)], [(wout, 0)], [(0, 0)], out_ln1,
                      [(D, F32, 0), (D, F32, 0), (D, BF16, 0)], rows=[(xf, 0)], vecs_n=[(ln1_g, 0), (ln1_b, 0)])

    FT = FFN_HIDDEN // 2

    def swiglu(accs, rows, vecs, j):
        g, u = accs
        return [g, u, g * _sigmoid(g) * u], []

    gate, up, hmid = _mm("ffn_up", S, FFN_HIDDEN, TM3, FT, [(h1b, D, 0)], [(wfg, 0), (wfu, 0)], [(0, 0), (0, 1)], swiglu,
                         [(FFN_HIDDEN, BF16, 0), (FFN_HIDDEN, BF16, 0), (FFN_HIDDEN, BF16, 0)])

    def down_ln2_loss(accs, rows, vecs, j):
        r2 = ALPHA * rows[0] + accs[0]
        yv, xhat, rstd = _ln_fwd(r2, vecs[0], vecs[1])
        diff = yv - rows[1]
        dy = diff * (1.0 / D_MODEL)
        dr2 = _ln_bwd(dy, xhat, rstd, vecs[0])
        return [dr2, dr2], [jnp.sum(dy * xhat, axis=0, keepdims=True), jnp.sum(dy, axis=0, keepdims=True),
                            (0.5 / D_MODEL) * jnp.sum(diff * diff, axis=0, keepdims=True)]

    dr2, dr2b, dln2_g, dln2_b, loss_lanes = _mm("ffn_down_ln2", S, D, TM3, D, [(hmid, FFN_HIDDEN, 0)], [(wfd, 0)], [(0, 0)],
                                               down_ln2_loss, [(D, F32, 0), (D, BF16, 0)], rows=[(h1, 0), (tgt, 0)],
                                               vecs_n=[(ln2_g, 0), (ln2_b, 0)], sums=[D, D, D])
    loss = lax.psum(jnp.sum(loss_lanes), ("x", "y", "c"))

    def dswiglu(accs, rows, vecs, j):
        g, u = rows[0].astype(F32), rows[1].astype(F32)
        sg = _sigmoid(g)
        return [accs[0] * u * sg * (1.0 + g * (1.0 - sg)), accs[0] * g * sg], []

    dgate, dup = _mm("ffn_down_bwd", S, FFN_HIDDEN, TM3, FT, [(dr2b, D, 0)], [(wfd, 0)], [(0, 0)], dswiglu,
                     [(FFN_HIDDEN, BF16, 0), (FFN_HIDDEN, BF16, 0)], nt=True, rows=[(gate, 0), (up, 0)])
    dwfd = _mm_tn("dw_ffn_down", hmid, dr2b, FFN_HIDDEN // 2, D, TS)
    dwfg = _mm_tn("dw_ffn_gate", h1b, dgate, D, FT, TS)
    dwfu = _mm_tn("dw_ffn_up", h1b, dup, D, FT, TS)
    core = lax.axis_index("c").astype(jnp.int32).reshape(1)

    def send_grads(tag, names_, blocks_, after_):
        theirs_ = _half_to_sibling("swap_halves_" + tag, blocks_)
        halves_ = [_half_sum("halfsum_" + nm, b, t, core, _row_tile(b.shape[1] // 2, b.shape[2], mult=16))
                   for nm, b, t in zip(names_, blocks_, theirs_)]
        started_, token_ = _chip_copies_start("scatter_" + tag + "_start", halves_, True, after_)
        return halves_, started_, token_

    ffn_names = ["w_ffn_gate", "w_ffn_up", "w_ffn_down"]
    ffn_halves, ffn_started, ffn_token = send_grads(
        "ffn", ffn_names, [_to_chip_blocks_cols(dwfg), _to_chip_blocks_cols(dwfu), dwfd.reshape(4, FFN_HIDDEN // 4, D)], dwfu)

    def dh1_ln1(accs, rows, vecs, j):
        dh1 = ALPHA * rows[0] + accs[0] + accs[1]
        _, xhat, rstd = _ln_fwd(rows[1], vecs[0], vecs[0])
        dr1 = _ln_bwd(dh1, xhat, rstd, vecs[0])
        return [dr1, dr1], [jnp.sum(dh1 * xhat, axis=0, keepdims=True), jnp.sum(dh1, axis=0, keepdims=True)]

    dr1, dr1b, dln1_g, dln1_b = _mm("ffn_up_bwd_ln1", S, D, TM2, D, [(dgate, FFN_HIDDEN, 0), (dup, FFN_HIDDEN, 0)],
                                    [(wfg, 0), (wfu, 0)], [(0, 0), (1, 1)], dh1_ln1, [(D, F32, 0), (D, BF16, 0)], nt=True,
                                    rows=[(dr2, 0), (r1, 0)], vecs_n=[(ln1_g, 0)], sums=[D, D], after=[ffn_token])

    def dmerge(accs, rows, vecs, j):
        dmix = accs[0]
        g0, g1 = _sigmoid(rows[0].astype(F32) + vecs[0]), _sigmoid(rows[1].astype(F32) + vecs[1])
        dgl0 = dmix * rows[2].astype(F32) * g0 * (1.0 - g0)
        dgl1 = dmix * rows[3].astype(F32) * g1 * (1.0 - g1)
        return [dmix * g0, dmix * g1, dgl0, dgl1], [jnp.sum(dgl0, axis=0, keepdims=True), jnp.sum(dgl1, axis=0, keepdims=True)]

    d_attn_d, d_ssm_d, dgl0, dgl1, dbg0, dbg1 = _mm(
        "out_bwd", S, D, TM, 512, [(dr1b, D, 0)], [(wout, 0)], [(0, 0)], dmerge, [(D, BF16, 0)] * 4, nt=True,
        rows=[(gl, 0), (gl, 2), (attn_d, 0), (ssm_d, 0)], vecs_n=[(b_gates, 0), (b_gates, 2)], sums=[D, D])
    dwout = _mm_tn("dw_out", mix, dr1b, D, D, TS)
    dwpa = _mm_tn("dw_proj_attn", o, d_attn_d, D, D, TS)
    dwps = _mm_tn("dw_proj_ssm", ssm, d_ssm_d, D, D, TS)
    mid_names = ["w_proj_attn", "w_proj_ssm", "w_out"]
    mid_halves, mid_started, mid_token = send_grads(
        "mid", mid_names, [dwpa.reshape(4, D // 4, D), dwps.reshape(4, SSM_INNER // 4, D), dwout.reshape(4, D // 4, D)], dwps)

    do, = _mm("proj_attn_bwd", S, D, TM, 512, [(d_attn_d, D, 0)], [(wpa, 0)], [(0, 0)], plain, [(D, BF16, 0)], nt=True,
              after=[mid_token])
    stats = _attn_stats(do, o32, lse_rows.transpose(0, 2, 1), AQF)
    dq, dk, dv, dck, dcq = _attn_bwd(qa, ka, qkv, do, stats, TA)

    def per_head(a):
        a = a.transpose(1, 0, 2).reshape(S, ATT_HEADS)
        return jnp.concatenate([a, jnp.zeros((S, LANES - ATT_HEADS), F32)], axis=1)

    dfl, dbf = _cum_bwd(per_head(dck.transpose(0, 2, 1)), per_head(dcq), small, bvec, TB)

    dssm, = _mm("proj_ssm_bwd", S, SSM_INNER, TM, 512, [(d_ssm_d, D, 0)], [(wps, 0)], [(0, 0)], plain, [(SSM_INNER, F32, 0)],
                nt=True)
    dxs, dbm, dcm, dz, ddt8, dnw, ddskip_b, dalog8, dbias8 = _ssd_bwd(
        xbc, z, y_ssd, dssm, hs_all, dtc, dtr, bias_r, a_log.reshape(SSM_GROUPS, 1, 8), alog_b, dskip_b, ssm_norm_w, bias_c,
        alog_c, LC)
    du_x, dcw_x, dcb_x = _conv_bwd("conv_bwd_x", xbc_raw, dxs, conv_w_full, cb_row, CV, 512, 0)
    du_b, dcw_b, dcb_b = _conv_bwd("conv_bwd_b", xbc_raw, dbm, conv_w_full, cb_row, CV, 512, SSM_INNER)
    du_c, dcw_c, dcb_c = _conv_bwd("conv_bwd_c", xbc_raw, dcm, conv_w_full, cb_row, CV, 512, SSM_INNER + SSM_GROUPS * SSM_STATE)
    dconv_w = jnp.concatenate([dcw_x, dcw_b, dcw_c], axis=1)
    dconv_b = jnp.concatenate([dcb_x, dcb_b, dcb_c], axis=1)
    ddt_raw = ddt8.transpose(1, 0, 2).reshape(S, SSM_HEADS)

    dsmall = jnp.concatenate([dfl[:, :ATT_HEADS], ddt_raw, jnp.zeros((S, 80), F32)], axis=1).astype(BF16)
    HB = SSM_GROUPS * SSM_STATE
    dw_q, dw_k, dw_v = (_mm_tn("dw_in_" + nm, xb, g_, D, D, TS) for nm, g_ in (("q", dq), ("k", dk), ("v", dv)))
    dw_z = _mm_tn("dw_in_z", xb, dz, D, D, TS)
    dw_xs, dw_b, dw_c = _mm_tn("dw_in_xs", xb, du_x, D, D, TS), _mm_tn("dw_in_b", xb, du_b, D, HB, TS), _mm_tn("dw_in_c", xb, du_c, D, HB, TS)
    dw_g0, dw_g1 = _mm_tn("dw_in_g0", xb, dgl0, D, D, TS), _mm_tn("dw_in_g1", xb, dgl1, D, D, TS)
    dw_s = _mm_tn("dw_in_small", xb, dsmall, D, LANES, TS)
    whole = lambda a: (a, 0, a.shape[1])
    dw_sections = [whole(dw_q), whole(dw_k), whole(dw_v), (dw_s, 0, ATT_HEADS), whole(dw_z), whole(dw_xs), whole(dw_b), whole(dw_c),
                   (dw_s, ATT_HEADS, ATT_HEADS + SSM_HEADS), whole(dw_g0), whole(dw_g1)]
    dw_blocks = jnp.stack([jnp.concatenate([a[:, lo:hi] for a, lo, hi in segs], axis=1)
                           for segs in _col_segments(dw_sections, shard_w)])

    in_halves, in_started, in_token = send_grads("in", ["w_in"], [dw_blocks], dw_blocks)
    def dx_first(accs, rows, vecs, j):
        return [ALPHA * rows[0] + sum(accs[1:], accs[0])], []

    def dx_more(accs, rows, vecs, j):
        return [rows[0] + sum(accs[1:], accs[0])], []

    wk = lambda col, width=D: (w_re, 0, col // width, width)
    dx_part, = _mm("dx_a", S, D, TM2, D, [(dq, D, 0), (dk, D, 0), (dv, D, 0), (dz, D, 0), (dz, D, 1)],
                   [wk(0), wk(1024), wk(2048), wk(RE_Z), wk(RE_Z + 1024)], [(k, k) for k in range(5)], dx_first,
                   [(D, F32, 0)], nt=True, rows=[(dr1, 0)], after=[in_token])
    grad_x, = _mm("dx_b", S, D, TM2, D,
                  [(du_x, D, 0), (du_x, D, 1), (du_b, HB, 0), (du_c, HB, 0), (dgl0, D, 0), (dgl1, D, 0), (dsmall, LANES, 0)],
                  [wk(RE_XBC), wk(RE_XBC + 1024), wk(RE_XBC + 2048, HB), wk(RE_XBC + 2048 + HB, HB), wk(RE_GATE),
                   wk(RE_GATE + 1024), wk(RE_SMALL, LANES)],
                  [(k, k) for k in range(7)], dx_more, [(D, F32, 0)], nt=True, rows=[(dx_part, 0)])
    names = ["w_in"] + mid_names + ffn_names
    halves = in_halves + mid_halves + ffn_halves
    stacks = (_chip_copies_wait("scatter_in_wait", in_started, True, grad_x)
              + _chip_copies_wait("scatter_mid_wait", mid_started, True, grad_x)
              + _chip_copies_wait("scatter_ffn_wait", ffn_started, True, grad_x))
    chip1 = chip.astype(jnp.int32).reshape(1)
    reduced = [_sum4("sum_" + nm, st, hv, chip1, _row_tile(st.shape[1], st.shape[2], mult=16))
               for nm, st, hv in zip(names, stacks, halves)]
    other = _sibling_swap("swap_reduced", reduced)
    big_w = [w_in, w_proj_attn, w_proj_ssm, w_out, w_ffn_gate, w_ffn_up, w_ffn_down]
    big_m = [m_w_in, m_w_proj_attn, m_w_proj_ssm, m_w_out, m_w_ffn_gate, m_w_ffn_up, m_w_ffn_down]
    big_v = [v_w_in, v_w_proj_attn, v_w_proj_ssm, v_w_out, v_w_ffn_gate, v_w_ffn_up, v_w_ffn_down]
    big = {}
    lower, upper = jnp.where(core[0] == 0, reduced[0], other[0]), jnp.where(core[0] == 0, other[0], reduced[0])
    g_in_t = jnp.concatenate([lower.T, upper.T], axis=1)
    flat = lambda a: jnp.transpose(a, (2, 0, 1)).reshape(-1, LANES)
    unflat = lambda a: jnp.transpose(a.reshape(shard_w, 1, D), (1, 2, 0))
    flat_rows = shard_w * D // LANES
    big["w_in"] = [g_in_t.T[None]] + [unflat(r) for r in _adamw_flat("adamw_w_in", flat(w_in), flat(m_w_in), flat(v_w_in),
                                                                      g_in_t.reshape(-1, LANES), _row_tile(flat_rows, LANES, 3 << 20))]
    for nm, w_, m_, v_, mine, theirs in list(zip(names, big_w, big_m, big_v, reduced, other))[1:]:
        big[nm] = _adamw_halves("adamw_" + nm, w_, m_, v_, mine, theirs, core, _row_tile(w_.shape[1] // 2, w_.shape[2]))

    dd_skip = ddskip_b.reshape(1, SSM_HEADS, ATT_HEAD_DIM).sum(axis=2)
    pieces = [dbf[:, :ATT_HEADS], dconv_w.reshape(1, SSM_CONV * SSM_CONV_DIM), dconv_b, dbias8.reshape(1, SSM_HEADS), dalog8.reshape(1, SSM_HEADS), dd_skip,
              dnw, dbg0, dbg1, dln1_g, dln1_b, dln2_g, dln2_b]
    widths = [p.shape[1] for p in pieces]
    total = sum(widths)
    P = -(-total // LANES) * LANES
    packed = jnp.concatenate(pieces + [jnp.zeros((1, P - total), F32)], axis=1)
    summed = _all_sum_small(packed)
    offs = [0]
    for wd in widths:
        offs.append(offs[-1] + wd)
    sm = [summed[:, offs[k]:offs[k + 1]] for k in range(len(pieces))]
    g_bf, g_cw_full, g_cb, g_dtb, g_al, g_ds, g_nw = sm[0], sm[1].reshape(SSM_CONV, SSM_CONV_DIM), sm[2], sm[3], sm[4], sm[5], sm[6]
    g_bg = jnp.concatenate([sm[7], sm[8]], axis=1)
    g_l1g, g_l1b, g_l2g, g_l2b = sm[9], sm[10], sm[11], sm[12]
    cshard = SSM_CONV_DIM // 4
    g_cw_shard = lax.dynamic_slice_in_dim(g_cw_full, chip * cshard, cshard, axis=1)
    small_names = ["b_forget", "conv_w", "conv_b", "dt_bias", "a_log", "d_skip", "ssm_norm_w", "b_gates", "ln1_g", "ln1_b",
                   "ln2_g", "ln2_b"]
    small_g = [g_bf, g_cw_shard.reshape(1, -1), g_cb, g_dtb, g_al, g_ds, g_nw, g_bg, g_l1g, g_l1b, g_l2g, g_l2b]
    small_w = [b_forget, conv_w[0].reshape(1, -1), conv_b, dt_bias, a_log, d_skip, ssm_norm_w, b_gates, ln1_g, ln1_b, ln2_g, ln2_b]
    small_m = [m_b_forget, m_conv_w[0].reshape(1, -1), m_conv_b, m_dt_bias, m_a_log, m_d_skip, m_ssm_norm_w, m_b_gates, m_ln1_g,
               m_ln1_b, m_ln2_g, m_ln2_b]
    small_v = [v_b_forget, v_conv_w[0].reshape(1, -1), v_conv_b, v_dt_bias, v_a_log, v_d_skip, v_ssm_norm_w, v_b_gates, v_ln1_g,
               v_ln1_b, v_ln2_g, v_ln2_b]
    sw = [a.shape[1] for a in small_w]
    stot = sum(sw)
    SP = -(-stot // LANES) * LANES

    def pack(parts):
        return jnp.concatenate(list(parts) + [jnp.zeros((1, SP - stot), F32)], axis=1).reshape(SP // LANES, LANES)

    sres = _adamw("adamw_small", pack(small_w), pack(small_m), pack(small_v), pack(small_g), SP // LANES)
    soffs = [0]
    for wd in sw:
        soffs.append(soffs[-1] + wd)
    smalls = {}
    for k, nm in enumerate(small_names):
        vals = [r.reshape(1, SP)[:, soffs[k]:soffs[k + 1]] for r in sres]
        if nm == "conv_w":
            vals = [v_.reshape(1, SSM_CONV, cshard) for v_ in vals]
        smalls[nm] = vals

    order = ["w_in", "b_forget", "conv_w", "conv_b", "dt_bias", "a_log", "d_skip", "ssm_norm_w", "w_proj_attn", "w_proj_ssm",
             "b_gates", "w_out", "ln1_g", "ln1_b", "w_ffn_gate", "w_ffn_up", "w_ffn_down", "ln2_g", "ln2_b"]
    allres = {**big, **smalls}
    outs = [loss, grad_x[None]]
    for idx in range(4):
        outs += [allres[nm][idx] for nm in order]
    return tuple(outs)
```

```python
import functools
import math

import jax
import jax.numpy as jnp
from jax import lax
from jax.experimental import pallas as pl
from jax.experimental.pallas import tpu as pltpu

F32, BF16 = jnp.float32, jnp.bfloat16
MESH = pl.DeviceIdType.MESH

D_MODEL = 1024
ATT_HEADS, ATT_HEAD_DIM = 16, 64
SSM_INNER, SSM_HEADS, SSM_GROUPS, SSM_STATE, SSM_CONV = 2048, 32, 4, 128, 4
SSM_CONV_DIM = SSM_INNER + 2 * SSM_GROUPS * SSM_STATE
GROUP_LANES = SSM_INNER // SSM_GROUPS
FFN_HIDDEN = 2816
ALPHA = 2.0 ** 0.25
LN_EPS = 1e-5
RMS_EPS = 1e-5
ADAM_LR, ADAM_B1, ADAM_B2, ADAM_EPS, ADAM_WD, ADAM_STEP = 0.001, 0.9, 0.999, 1e-08, 0.01, 10
IN_SIZES = (1024, 1024, 1024, 16, 2048, 3072, 32, 2048)
IN_WIDTH = sum(IN_SIZES)
RE_WIDTH = 3072 + 2048 + 3072 + 2048 + 128
RE_Z, RE_XBC, RE_GATE, RE_SMALL = 3072, 5120, 8192, 10240

LANES = 128
VMEM_CAP = 60 * 1024 * 1024
NEG = -1e30
TILES = dict(TM=1024, TM2=256, TM3=512, TA=512, AQF=2048, LC=256, CV=512, TS=2048, TB=256)


def _params(n_axes, vmem_bytes=None):
    return pltpu.CompilerParams(dimension_semantics=("arbitrary",) * n_axes,
                                vmem_limit_bytes=None if vmem_bytes is None else int(min(vmem_bytes, VMEM_CAP)))


def _sigmoid(v):
    return 1.0 / (1.0 + jnp.exp(-v))


def _softplus(v):
    return jnp.maximum(v, 0.0) + jnp.log(1.0 + jnp.exp(-jnp.abs(v)))


def _dot(a, b):
    return lax.dot_general(a, b, (((1,), (0,)), ((), ())), preferred_element_type=F32)


def _dot_nt(a, b):
    return lax.dot_general(a, b, (((1,), (1,)), ((), ())), preferred_element_type=F32)


def _dot_tn(a, b):
    return lax.dot_general(a, b, (((0,), (0,)), ((), ())), preferred_element_type=F32)


def _split3(v):
    hi = v.astype(BF16)
    r1 = v - hi.astype(F32)
    mid = r1.astype(BF16)
    lo = (r1 - mid.astype(F32)).astype(BF16)
    return hi, mid, lo


def _dot_exact_left(m01, v):
    hi, mid, lo = _split3(v)
    return _dot(m01, hi) + _dot(m01, mid) + _dot(m01, lo)


def _dot_exact_right(v, m01, terms=3):
    parts = _split3(v)[:terms]
    out = _dot(parts[0], m01)
    for p in parts[1:]:
        out = out + _dot(p, m01)
    return out


def _mm(name, M, N, tm, tn, lhs, rhs, pairs, e_fn, outs, *, nt=False, rows=(), vecs_n=(), sums=(), after=()):
    ni, nj = M // tm, N // tn
    assert ni * tm == M and nj * tn == N, (name, M, N, tm, tn)
    n_l, n_r, n_row, n_vn, n_o, n_s = len(lhs), len(rhs), len(rows), len(vecs_n), len(outs), len(sums)

    def body(*refs):
        pos = 0
        l_refs = refs[pos:pos + n_l]; pos += n_l
        r_refs = refs[pos:pos + n_r]; pos += n_r
        row_refs = refs[pos:pos + n_row]; pos += n_row
        vn_refs = refs[pos:pos + n_vn]; pos += n_vn + len(after)
        o_refs = refs[pos:pos + n_o]; pos += n_o
        s_refs = refs[pos:pos + n_s]; pos += n_s
        i, j = pl.program_id(0), pl.program_id(1)
        accs = []
        for li, ri in pairs:
            accs.append(_dot_nt(l_refs[li][...], r_refs[ri][...]) if nt else _dot(l_refs[li][...], r_refs[ri][...]))
        out_vals, sum_vals = e_fn(accs, [r[...] for r in row_refs], [r[...] for r in vn_refs], j)
        for r, v in zip(o_refs, out_vals):
            r[...] = v.astype(r.dtype)
        if n_s:
            col = pl.multiple_of(j * tn, LANES)

            @pl.when(i == 0)
            def _():
                for r, v in zip(s_refs, sum_vals):
                    r[:, pl.ds(col, tn)] = v

            @pl.when(i > 0)
            def _():
                for r, v in zip(s_refs, sum_vals):
                    r[:, pl.ds(col, tn)] += v

    in_specs, args, est = [], [], 0
    for arr, width, cb in lhs:
        in_specs.append(pl.BlockSpec((tm, width), lambda i, j, cb=cb: (i, cb)))
        args.append(arr); est += tm * width * arr.dtype.itemsize
    for arr, off, *ksub in rhs:
        if nt:
            kb, kw = ksub if ksub else (0, arr.shape[1])
            in_specs.append(pl.BlockSpec((tn, kw), lambda i, j, off=off, kb=kb: (j + off, kb)))
            est += tn * kw * arr.dtype.itemsize
        else:
            in_specs.append(pl.BlockSpec((arr.shape[0], tn), lambda i, j, off=off: (0, j + off)))
            est += tn * arr.shape[0] * arr.dtype.itemsize
        args.append(arr)
    for arr, off in rows:
        in_specs.append(pl.BlockSpec((tm, tn), lambda i, j, off=off: (i, j + off)))
        args.append(arr); est += tm * tn * arr.dtype.itemsize
    for arr, off in vecs_n:
        in_specs.append(pl.BlockSpec((1, tn), lambda i, j, off=off: (0, j + off)))
        args.append(arr); est += 8 * tn * 4
    for arr in after:
        in_specs.append(pl.BlockSpec(memory_space=pl.ANY))
        args.append(arr)
    out_shape, out_specs = [], []
    for total, dtype, off in outs:
        out_shape.append(jax.ShapeDtypeStruct((M, total), dtype))
        out_specs.append(pl.BlockSpec((tm, tn), lambda i, j, off=off: (i, j + off)))
        est += tm * tn * jnp.dtype(dtype).itemsize
    for total in sums:
        out_shape.append(jax.ShapeDtypeStruct((1, total), F32))
        out_specs.append(pl.BlockSpec((1, total), lambda i, j: (0, 0)))
        est += 8 * total * 4
    vmem = 2 * est + (len(pairs) + 4) * tm * tn * 4 + (8 << 20)
    return pl.pallas_call(body, name=name, grid=(ni, nj), in_specs=in_specs, out_specs=out_specs, out_shape=out_shape,
                          compiler_params=_params(2, vmem))(*args)


def _mm_tn(name, a, g, ta, tn, ts, a_cols=None, a_off=0):
    S = a.shape[0]
    Ka = a.shape[1] if a_cols is None else a_cols
    N = g.shape[1]
    assert Ka % ta == 0 and N % tn == 0 and S % ts == 0, (name, Ka, N, S)
    aoff = a_off // ta

    def body(a_ref, g_ref, o_ref):
        s = pl.program_id(2)
        part = _dot_tn(a_ref[...], g_ref[...])

        @pl.when(s == 0)
        def _():
            o_ref[...] = part

        @pl.when(s > 0)
        def _():
            o_ref[...] += part

    vmem = 2 * (ts * ta * 2 + ts * tn * 2 + ta * tn * 4) + 2 * ta * tn * 4 + (8 << 20)
    return pl.pallas_call(
        body, name=name, grid=(Ka // ta, N // tn, S // ts),
        in_specs=[pl.BlockSpec((ts, ta), lambda ia, jn, s: (s, ia + aoff)), pl.BlockSpec((ts, tn), lambda ia, jn, s: (s, jn))],
        out_specs=pl.BlockSpec((ta, tn), lambda ia, jn, s: (ia, jn)),
        out_shape=jax.ShapeDtypeStruct((Ka, N), F32), compiler_params=_params(3, vmem))(a, g)


def _tri(n, upper):
    r = lax.broadcasted_iota(jnp.int32, (n, n), 0)
    c = lax.broadcasted_iota(jnp.int32, (n, n), 1)
    return jnp.where((c >= r) if upper else (c <= r), 1.0, 0.0).astype(BF16)


def _logsig(v):
    return jnp.minimum(v, 0.0) - jnp.log(1.0 + jnp.exp(-jnp.abs(v)))


def _cum_fwd(small, bvec, tb):
    S = small.shape[0]

    def body(x_ref, b_ref, o_ref, carry):
        i = pl.program_id(0)

        @pl.when(i == 0)
        def _():
            carry[...] = jnp.zeros_like(carry)

        logf = _logsig(x_ref[...] + b_ref[...])
        cum = _dot_exact_left(_tri(tb, False), logf) + carry[0:1, :]
        o_ref[...] = cum
        carry[0:1, :] = cum[tb - 1:tb, :]

    return pl.pallas_call(
        body, name="cum_fwd", grid=(S // tb,),
        in_specs=[pl.BlockSpec((tb, LANES), lambda i: (i, 0)), pl.BlockSpec((1, LANES), lambda i: (0, 0))],
        out_specs=pl.BlockSpec((tb, LANES), lambda i: (i, 0)), out_shape=jax.ShapeDtypeStruct((S, LANES), F32),
        scratch_shapes=[pltpu.VMEM((8, LANES), F32)], compiler_params=_params(1))(small, bvec)


def _cum_bwd(dcum_k, dcum_q, small, bvec, tb):
    S = small.shape[0]
    nb = S // tb

    def body(dk_ref, dq_ref, x_ref, b_ref, o_ref, s_ref, carry):
        i = pl.program_id(0)

        @pl.when(i == 0)
        def _():
            carry[...] = jnp.zeros_like(carry)
            s_ref[...] = jnp.zeros_like(s_ref)

        rc = _dot_exact_left(_tri(tb, True), dk_ref[...] + dq_ref[...]) + carry[0:1, :]
        dfl = rc * _sigmoid(-(x_ref[...] + b_ref[...]))
        o_ref[...] = dfl
        s_ref[...] += jnp.sum(dfl, axis=0, keepdims=True)
        carry[0:1, :] = rc[0:1, :]

    rev = lambda i: (nb - 1 - i, 0)
    return pl.pallas_call(
        body, name="cum_bwd", grid=(nb,),
        in_specs=[pl.BlockSpec((tb, LANES), rev)] * 3 + [pl.BlockSpec((1, LANES), lambda i: (0, 0))],
        out_specs=[pl.BlockSpec((tb, LANES), rev), pl.BlockSpec((1, LANES), lambda i: (0, 0))],
        out_shape=[jax.ShapeDtypeStruct((S, LANES), F32), jax.ShapeDtypeStruct((1, LANES), F32)],
        scratch_shapes=[pltpu.VMEM((8, LANES), F32)], compiler_params=_params(1))(dcum_k, dcum_q, small, bvec)


N_AUG = 3


def _lane():
    return lax.broadcasted_iota(jnp.int32, (1, LANES), 1)


def _lane_mask():
    return _lane() < ATT_HEAD_DIM


def _aug_base(h):
    return ATT_HEAD_DIM * (1 - h)


def _attn_prep(qkv, cum_cols, T):
    S = qkv.shape[0]
    HP = ATT_HEADS // 2

    def body(q_ref, k_ref, c_ref, qa_ref, ka_ref):
        lane = _lane()
        q = q_ref[...]
        k = k_ref[...]
        one, zero = jnp.ones_like(q), jnp.zeros_like(q)
        for h in (0, 1):
            base = _aug_base(h)
            own = (lane < ATT_HEAD_DIM) if h == 0 else (lane >= ATT_HEAD_DIM)
            term_lanes = (lane >= base) & (lane < base + N_AUG)
            terms = [t.astype(F32) for t in _split3(c_ref[0, :, h:h + 1])]
            neg = jnp.where(lane == base, -terms[0], jnp.where(lane == base + 1, -terms[1], -terms[2])).astype(BF16)
            qa_ref[:, h * LANES:(h + 1) * LANES] = jnp.where(lane == base + N_AUG, zero, jnp.where(term_lanes, one, q))
            ka_ref[:, h * LANES:(h + 1) * LANES] = jnp.where(term_lanes, neg, jnp.where(lane == base + N_AUG, one,
                                                                                         jnp.where(own, k, zero)))

    return pl.pallas_call(
        body, name="attn_prep", grid=(S // T, HP),
        in_specs=[pl.BlockSpec((T, LANES), lambda i, hp: (i, hp)), pl.BlockSpec((T, LANES), lambda i, hp: (i, HP + hp)),
                  pl.BlockSpec((1, T, 2), lambda i, hp: (hp, i, 0))],
        out_specs=[pl.BlockSpec((T, 2 * LANES), lambda i, hp: (i, hp))] * 2,
        out_shape=[jax.ShapeDtypeStruct((S, 2 * D_MODEL), BF16)] * 2, compiler_params=_params(2))(qkv, qkv, cum_cols)


def _attn_fwd(qa, ka, qkv, T, TK):
    S = qkv.shape[0]
    nq = S // T
    r = T // TK
    HP = ATT_HEADS // 2

    def body(q0_ref, q1_ref, k0_ref, k1_ref, v_ref, o_ref, o32_ref, lse_ref):
        i = pl.program_id(1)
        qs = (q0_ref[...], q1_ref[...])
        k_refs = (k0_ref, k1_ref)
        row = lax.broadcasted_iota(jnp.int32, (TK, T), 0)
        col = lax.broadcasted_iota(jnp.int32, (TK, T), 1)
        head_rows = lax.broadcasted_iota(jnp.int32, (LANES, 1), 0) < ATT_HEAD_DIM

        def block(j, carry, q0):
            off = pl.multiple_of(j * TK, TK)
            vj = v_ref[pl.ds(off, TK), :]
            full = q0 is None
            q0 = 0 if full else q0
            m0, l0, m1, l1, acc = carry
            new, alphas, pvs = [], [], []
            for h, (m, l) in enumerate(((m0, l0), (m1, l1))):
                st = _dot_nt(k_refs[h][pl.ds(off, TK), :], qs[h][q0:, :])
                if not full:
                    st = jnp.where(row[:, :T - q0] <= col[:, :T - q0], st, NEG)
                m_old, l_old = m[:, q0:], l[:, q0:]
                m_new = jnp.maximum(m_old, jnp.max(st, axis=0, keepdims=True))
                p = jnp.exp(st - m_new)
                alpha = jnp.exp(m_old - m_new)
                l_new = alpha * l_old + jnp.sum(p, axis=0, keepdims=True)
                pvs.append(_dot_tn(vj, p.astype(BF16)))
                alphas.append(alpha)
                new += [m_new, l_new]
            part = acc[:, q0:] * jnp.where(head_rows, alphas[0], alphas[1]) + jnp.where(head_rows, pvs[0], pvs[1])
            if q0:
                keep = lambda old, upd: jnp.concatenate([old[:, :q0], upd], axis=1)
                return (keep(m0, new[0]), keep(l0, new[1]), keep(m1, new[2]), keep(l1, new[3]), keep(acc, part))
            return (new[0], new[1], new[2], new[3], part)

        init = (jnp.full((1, T), NEG, F32), jnp.zeros((1, T), F32), jnp.full((1, T), NEG, F32), jnp.zeros((1, T), F32),
                jnp.zeros((LANES, T), F32))
        n_full = i * r
        carry = lax.fori_loop(0, n_full // 2, lambda jj, c: block(2 * jj + 1, block(2 * jj, c, None), None), init)
        carry = lax.cond(n_full % 2 == 1, lambda c: block(n_full - 1, c, None), lambda c: c, carry)
        for d in range(r):
            carry = block(n_full + d, carry, d * TK)
        m0, l0, m1, l1, acc = carry
        out = (acc / jnp.where(head_rows, l0, l1)).T
        o_ref[...] = out.astype(BF16)
        o32_ref[...] = out
        lse_ref[0, 0:1, :] = m0 + jnp.log(l0)
        lse_ref[0, 1:2, :] = m1 + jnp.log(l1)

    vmem = 2 * (2 * T * LANES * 2 + 3 * S * LANES * 2 + T * LANES * (2 + 4) + 8 * T * 4) + 10 * T * TK * 4 + (8 << 20)
    qspec = lambda h: pl.BlockSpec((T, LANES), lambda hp, i, h=h: (i, 2 * hp + h))
    kspec = lambda h: pl.BlockSpec((S, LANES), lambda hp, i, h=h: (0, 2 * hp + h))
    return pl.pallas_call(
        body, name="attn_fwd", grid=(HP, nq),
        in_specs=[qspec(0), qspec(1), kspec(0), kspec(1), pl.BlockSpec((S, LANES), lambda hp, i: (0, 2 * HP + hp))],
        out_specs=[pl.BlockSpec((T, LANES), lambda hp, i: (i, hp)), pl.BlockSpec((T, LANES), lambda hp, i: (i, hp)),
                   pl.BlockSpec((1, 2, T), lambda hp, i: (hp, 0, i))],
        out_shape=[jax.ShapeDtypeStruct((S, D_MODEL), BF16), jax.ShapeDtypeStruct((S, D_MODEL), F32),
                   jax.ShapeDtypeStruct((HP, 2, S), F32)],
        compiler_params=_params(2, vmem))(qa, qa, ka, ka, qkv)


def _attn_stats(do, o32, lse_cols, T):
    S = do.shape[0]
    HP = ATT_HEADS // 2

    def body(do_ref, o_ref, lse_ref, st_ref):
        lane = lax.broadcasted_iota(jnp.int32, (LANES, 8), 0)
        c = lax.broadcasted_iota(jnp.int32, (LANES, 8), 1)
        sel = jnp.where(((c == 2) & (lane < ATT_HEAD_DIM)) | ((c == 3) & (lane >= ATT_HEAD_DIM)), 1.0, 0.0).astype(BF16)
        dd = _dot_exact_right(do_ref[...].astype(F32) * o_ref[...], sel)
        c8 = lax.broadcasted_iota(jnp.int32, (1, 8), 1)
        st_ref[0] = jnp.where(c8 == 0, lse_ref[0, :, 0:1], jnp.where(c8 == 1, lse_ref[0, :, 1:2], dd))

    return pl.pallas_call(
        body, name="attn_stats", grid=(HP, S // T),
        in_specs=[pl.BlockSpec((T, LANES), lambda hp, i: (i, hp)), pl.BlockSpec((T, LANES), lambda hp, i: (i, hp)),
                  pl.BlockSpec((1, T, 2), lambda hp, i: (hp, i, 0))],
        out_specs=pl.BlockSpec((1, T, 8), lambda hp, i: (hp, i, 0)), out_shape=jax.ShapeDtypeStruct((HP, S, 8), F32),
        compiler_params=_params(2))(do, o32, lse_cols)


def _attn_bwd(qa, ka, qkv, do, stats, T):
    S = qkv.shape[0]
    nq = S // T
    HP = ATT_HEADS // 2

    def body(k0_ref, k1_ref, v_ref, q0_ref, q1_ref, do_ref, st_ref, dq_ref, dk_ref, dv_ref, dck_ref, dcq_ref, dq_acc):
        j = pl.program_id(1)
        mA = _lane_mask()
        masks = (mA, jnp.logical_not(mA))
        q_refs = (q0_ref, q1_ref)

        @pl.when(j == 0)
        def _():
            dq_acc[...] = jnp.zeros_like(dq_acc)

        kas = (k0_ref[...], k1_ref[...])
        vj = v_ref[...]
        row = lax.broadcasted_iota(jnp.int32, (T, T), 0)
        col = lax.broadcasted_iota(jnp.int32, (T, T), 1)

        def block(i, carry, diag):
            dvt, dkt0, dkt1 = carry
            off = pl.multiple_of(i * T, T)
            doi = do_ref[pl.ds(off, T), :]
            zero = jnp.zeros_like(doi)
            dkts = [dkt0, dkt1]
            for h in (0, 1):
                qh = q_refs[h][pl.ds(off, T), :]
                doh = jnp.where(masks[h], doi, zero)
                lse = st_ref[0, pl.ds(off, T), h:h + 1]
                dd = st_ref[0, pl.ds(off, T), 2 + h:3 + h]
                sc = _dot_nt(qh, kas[h])
                if diag:
                    sc = jnp.where(row >= col, sc, NEG)
                p = jnp.exp(sc - lse)
                dp = _dot_nt(doh, vj)
                ds = (p * (dp - dd)).astype(BF16)
                dvt = dvt + _dot_tn(doh, p.astype(BF16))
                dkts[h] = dkts[h] + _dot_tn(qh, ds)
                dq_acc[h, pl.ds(off, T), :] += _dot(ds, kas[h])
            return (dvt, dkts[0], dkts[1])

        z = jnp.zeros((LANES, T), F32)
        carry = block(j, (z, z, z), True)
        dvt, dkt0, dkt1 = lax.fori_loop(j + 1, nq, lambda i, c: block(i, c, False), carry)
        dv_ref[...] = dvt.T.astype(BF16)
        dk_ref[...] = jnp.where(mA, dkt0.T, dkt1.T).astype(BF16)
        ones_q = (_aug_base(0), _aug_base(1))
        dck_ref[0, 0:1, :] = -dkt0[ones_q[0]:ones_q[0] + 1, :]
        dck_ref[0, 1:2, :] = -dkt1[ones_q[1]:ones_q[1] + 1, :]

        @pl.when(j == nq - 1)
        def _():
            dq0, dq1 = dq_acc[0], dq_acc[1]
            ones_k = (_aug_base(0) + N_AUG, _aug_base(1) + N_AUG)
            dq_ref[...] = (jnp.where(mA, dq0, dq1) * (1.0 / math.sqrt(ATT_HEAD_DIM))).astype(BF16)
            dcq_ref[0, :, 0:1] = dq0[:, ones_k[0]:ones_k[0] + 1]
            dcq_ref[0, :, 1:2] = dq1[:, ones_k[1]:ones_k[1] + 1]

    vmem = (2 * (3 * T * LANES * 2 + 3 * S * LANES * 2 + S * LANES * 4 + S * LANES * (2 + 4) + 2 * T * LANES * 2 + 8 * T * 4)
            + 2 * S * LANES * 4 + 12 * T * T * 4 + (8 << 20))
    kspec = lambda h: pl.BlockSpec((T, LANES), lambda hp, j, h=h: (j, 2 * hp + h))
    qspec = lambda h: pl.BlockSpec((S, LANES), lambda hp, j, h=h: (0, 2 * hp + h))
    blk = pl.BlockSpec((T, LANES), lambda hp, j: (j, hp))
    full = pl.BlockSpec((S, LANES), lambda hp, j: (0, hp))
    return pl.pallas_call(
        body, name="attn_bwd", grid=(HP, nq),
        in_specs=[kspec(0), kspec(1), pl.BlockSpec((T, LANES), lambda hp, j: (j, 2 * HP + hp)), qspec(0), qspec(1), full,
                  pl.BlockSpec((1, S, 8), lambda hp, j: (hp, 0, 0))],
        out_specs=[full, blk, blk, pl.BlockSpec((1, 2, T), lambda hp, j: (hp, 0, j)),
                   pl.BlockSpec((1, S, 2), lambda hp, j: (hp, 0, 0))],
        out_shape=[jax.ShapeDtypeStruct((S, D_MODEL), BF16)] * 3 + [jax.ShapeDtypeStruct((HP, 2, S), F32),
                                                                     jax.ShapeDtypeStruct((HP, S, 2), F32)],
        scratch_shapes=[pltpu.VMEM((2, S, LANES), F32)], compiler_params=_params(2, vmem))(ka, ka, qkv, qa, qa, do, stats)


HALO = 8


def _shift_down(x, d, above):
    r = pltpu.roll(x, d, 0)
    head = jnp.where(lax.broadcasted_iota(jnp.int32, (HALO, 1), 0) < d, pltpu.roll(above, d, 0), r[0:HALO])
    return head if x.shape[0] == HALO else jnp.concatenate([head, r[HALO:]], axis=0)


def _shift_up(x, d, below):
    n = x.shape[0]
    r = pltpu.roll(x, n - d, 0)
    tail = jnp.where(lax.broadcasted_iota(jnp.int32, (HALO, 1), 0) >= HALO - d, pltpu.roll(below, HALO - d, 0), r[n - HALO:])
    return jnp.concatenate([r[:n - HALO], tail], axis=0)


def _conv_fwd(u, w, b, ts, tc):
    S, C = u.shape
    hb = ts // HALO

    def body(u_ref, prev_ref, w_ref, b_ref, o_ref):
        i = pl.program_id(0)
        x = u_ref[...]
        above = jnp.where(i == 0, 0.0, prev_ref[...])
        acc = b_ref[...] + w_ref[3:4, :] * x
        for k in range(SSM_CONV - 1):
            acc = acc + w_ref[k:k + 1, :] * _shift_down(x, SSM_CONV - 1 - k, above)
        o_ref[...] = acc * _sigmoid(acc)

    return pl.pallas_call(
        body, name="conv_fwd", grid=(S // ts, C // tc),
        in_specs=[pl.BlockSpec((ts, tc), lambda i, j: (i, j)),
                  pl.BlockSpec((HALO, tc), lambda i, j: (jnp.maximum(i * hb - 1, 0), j)),
                  pl.BlockSpec((SSM_CONV, tc), lambda i, j: (0, j)), pl.BlockSpec((1, tc), lambda i, j: (0, j))],
        out_specs=pl.BlockSpec((ts, tc), lambda i, j: (i, j)), out_shape=jax.ShapeDtypeStruct((S, C), F32),
        compiler_params=_params(2))(u, u, w, b)


def _conv_bwd(name, u, dy, w, b, ts, tc, col0):
    S, C = dy.shape
    cb = col0 // tc
    assert cb * tc == col0
    hb = ts // HALO
    nb = S // ts

    def body(u_ref, uprev_ref, unext_ref, dy_ref, dynext_ref, w_ref, b_ref, du_ref, dw_ref, db_ref):
        i = pl.program_id(1)
        x = u_ref[...]
        above = jnp.where(i == 0, 0.0, uprev_ref[...])
        ws = [w_ref[k:k + 1, :] for k in range(SSM_CONV)]

        def dsilu(pre):
            sg = _sigmoid(pre)
            return sg * (1.0 + pre * (1.0 - sg))

        shifted = [_shift_down(x, SSM_CONV - 1 - k, above) for k in range(SSM_CONV - 1)] + [x]
        pre = b_ref[...]
        for k in range(SSM_CONV):
            pre = pre + ws[k] * shifted[k]
        g = dy_ref[...] * dsilu(pre)
        nxt = unext_ref[...]
        tail = x[ts - HALO:, :]
        pre_n = b_ref[...] + ws[SSM_CONV - 1] * nxt
        for k in range(SSM_CONV - 1):
            pre_n = pre_n + ws[k] * _shift_down(nxt, SSM_CONV - 1 - k, tail)
        g_next = jnp.where(i == nb - 1, 0.0, dynext_ref[...] * dsilu(pre_n))
        du = ws[SSM_CONV - 1] * g
        for k in range(SSM_CONV - 1):
            du = du + ws[k] * _shift_up(g, SSM_CONV - 1 - k, g_next)
        du_ref[...] = du.astype(du_ref.dtype)
        dws = [jnp.sum(g * shifted[k], axis=0, keepdims=True) for k in range(SSM_CONV)]
        dbs = jnp.sum(g, axis=0, keepdims=True)

        @pl.when(i == 0)
        def _():
            for k in range(SSM_CONV):
                dw_ref[k:k + 1, :] = dws[k]
            db_ref[...] = dbs

        @pl.when(i > 0)
        def _():
            for k in range(SSM_CONV):
                dw_ref[k:k + 1, :] += dws[k]
            db_ref[...] += dbs

    nxt = lambda off: (lambda j, i: (jnp.minimum((i + 1) * hb, S // HALO - 1), j + off))
    return pl.pallas_call(
        body, name=name, grid=(C // tc, nb),
        in_specs=[pl.BlockSpec((ts, tc), lambda j, i: (i, j + cb)),
                  pl.BlockSpec((HALO, tc), lambda j, i: (jnp.maximum(i * hb - 1, 0), j + cb)),
                  pl.BlockSpec((HALO, tc), nxt(cb)),
                  pl.BlockSpec((ts, tc), lambda j, i: (i, j)),
                  pl.BlockSpec((HALO, tc), nxt(0)),
                  pl.BlockSpec((SSM_CONV, tc), lambda j, i: (0, j + cb)), pl.BlockSpec((1, tc), lambda j, i: (0, j + cb))],
        out_specs=[pl.BlockSpec((ts, tc), lambda j, i: (i, j)), pl.BlockSpec((SSM_CONV, tc), lambda j, i: (0, j)),
                   pl.BlockSpec((1, tc), lambda j, i: (0, j))],
        out_shape=[jax.ShapeDtypeStruct((S, C), BF16), jax.ShapeDtypeStruct((SSM_CONV, C), F32), jax.ShapeDtypeStruct((1, C), F32)],
        compiler_params=_params(2))(u, u, u, dy, dy, w, b)


def _head_sum():
    lane = jnp.right_shift(lax.broadcasted_iota(jnp.int32, (GROUP_LANES, 8), 0), 6)
    r = lax.broadcasted_iota(jnp.int32, (GROUP_LANES, 8), 1)
    return jnp.where(lane == r, 1.0, 0.0).astype(BF16)


def _head_expand():
    r = lax.broadcasted_iota(jnp.int32, (8, GROUP_LANES), 0)
    c = jnp.right_shift(lax.broadcasted_iota(jnp.int32, (8, GROUP_LANES), 1), 6)
    return jnp.where(r == c, 1.0, 0.0).astype(BF16)


def _ssd_common(dtc_ref, dtr_ref, bias_r, alog_b, bias_c, alog_c, L):
    a_b = -jnp.exp(alog_b)
    dt = _dot_exact_right(_softplus(dtc_ref[0] + bias_r), _head_expand())
    acum = _dot_exact_left(_tri(L, False), dt * a_b)
    a_c = -jnp.exp(alog_c)
    dtr = _softplus(dtr_ref[0] + bias_c)
    acum_r = _dot_exact_right(dtr * a_c, _tri(L, True))
    return a_b, dt, acum, acum_r


def _ssd_specs(L, nc, rev):
    cc = (lambda c: nc - 1 - c) if rev else (lambda c: c)
    G = SSM_GROUPS
    blk = pl.BlockSpec((L, GROUP_LANES), lambda g, c: (cc(c), g))
    dtc = pl.BlockSpec((1, L, 8), lambda g, c: (g, cc(c), 0))
    rowv = pl.BlockSpec((1, 1, 8), lambda g, c: (g, 0, 0))
    xs = blk
    bm = pl.BlockSpec((L, SSM_STATE), lambda g, c: (cc(c), SSM_INNER // SSM_STATE + g))
    cm = pl.BlockSpec((L, SSM_STATE), lambda g, c: (cc(c), SSM_INNER // SSM_STATE + G + g))
    dtr = pl.BlockSpec((1, 8, L), lambda g, c: (g, 0, cc(c)))
    vec = pl.BlockSpec((1, GROUP_LANES), lambda g, c: (0, g))
    colv = pl.BlockSpec((1, 8, 1), lambda g, c: (g, 0, 0))
    hs = pl.BlockSpec((1, 1, SSM_STATE, GROUP_LANES), lambda g, c: (g, cc(c), 0, 0))
    return blk, xs, bm, cm, dtc, dtr, vec, rowv, colv, hs


def _ssd_fwd(xbc, z, dtc, dtr, bias_r, alog_b, dskip_b, normw, bias_c, alog_c, L):
    S = z.shape[0]
    nc = S // L
    blk, xs, bm, cm, dtcs, dtrs, vec, rowv, colv, hs = _ssd_specs(L, nc, False)

    def body(x_ref, b_ref, c_ref, z_ref, dtc_ref, dtr_ref, bias_ref, alog_ref, dskip_ref, nw_ref, biasc_ref, alogc_ref,
             y_ref, ssm_ref, hs_ref, h_scr):
        c = pl.program_id(1)

        @pl.when(c == 0)
        def _():
            h_scr[...] = jnp.zeros_like(h_scr)

        mA = _lane_mask()
        a_b, dt, acum, acum_r = _ssd_common(dtc_ref, dtr_ref, bias_ref[0], alog_ref[...], biasc_ref[0], alogc_ref[0], L)
        x = x_ref[...]
        cb, bb = c_ref[...].astype(BF16), b_ref[...].astype(BF16)
        hprev = h_scr[...]
        hs_ref[0, 0] = hprev
        xdt = x * dt
        xdt_b = xdt.astype(BF16)
        gmat = _dot_nt(cb, bb)
        row = lax.broadcasted_iota(jnp.int32, (L, L), 0)
        col = lax.broadcasted_iota(jnp.int32, (L, L), 1)
        parts = []
        for p in range(GROUP_LANES // LANES):
            xp = xdt_b[:, p * LANES:(p + 1) * LANES]
            yd = []
            for hh in (0, 1):
                r = 2 * p + hh
                acol = acum[:, r * ATT_HEAD_DIM:r * ATT_HEAD_DIM + 1]
                arow = acum_r[r:r + 1, :]
                lm = jnp.exp(jnp.where(row >= col, acol - arow, NEG))
                yd.append(_dot((gmat * lm).astype(BF16), xp))
            parts.append(jnp.where(mA, yd[0], yd[1]))
        ydiag = jnp.concatenate(parts, axis=1)
        yoff = jnp.exp(acum) * _dot(cb, hprev.astype(BF16))
        y = ydiag + yoff + dskip_ref[...] * x
        aend = acum[L - 1:L, :]
        wgt = (jnp.exp(aend - acum) * xdt).astype(BF16)
        h_scr[...] = jnp.exp(aend) * hprev + _dot_tn(bb, wgt)
        y_ref[...] = y
        zz = z_ref[...].astype(F32)
        u = y * (zz * _sigmoid(zz))
        rs = lax.rsqrt(jnp.mean(u * u, axis=1, keepdims=True) + RMS_EPS)
        ssm_ref[...] = (u * rs * nw_ref[...]).astype(BF16)

    return pl.pallas_call(
        body, name="ssd_fwd", grid=(SSM_GROUPS, nc),
        in_specs=[xs, bm, cm, blk, dtcs, dtrs, rowv, vec, vec, vec, colv, colv],
        out_specs=[blk, blk, hs],
        out_shape=[jax.ShapeDtypeStruct((S, SSM_INNER), F32), jax.ShapeDtypeStruct((S, SSM_INNER), BF16),
                   jax.ShapeDtypeStruct((SSM_GROUPS, nc, SSM_STATE, GROUP_LANES), F32)],
        scratch_shapes=[pltpu.VMEM((SSM_STATE, GROUP_LANES), F32)],
        compiler_params=_params(2, 48 << 20))(xbc, xbc, xbc, z, dtc, dtr, bias_r, alog_b, dskip_b, normw, bias_c, alog_c)


def _ssd_bwd(xbc, z, y, dssm, hs_all, dtc, dtr, bias_r, alog_r, alog_b, dskip_b, normw, bias_c, alog_c, L):
    S = z.shape[0]
    nc = S // L
    blk, xs, bm, cm, dtcs, dtrs, vec, rowv, colv, hs = _ssd_specs(L, nc, True)

    def body(x_ref, b_ref, c_ref, z_ref, y_ref, dssm_ref, hs_ref, dtc_ref, dtr_ref, bias_ref, alogr_ref, alog_ref, dskip_ref, nw_ref,
             biasc_ref, alogc_ref,
             dx_ref, db_ref, dc_ref, dz_ref, ddt_ref, dnw_ref, ddskip_ref, dalog_ref, dbias_ref, dh_scr):
        c = pl.program_id(1)

        @pl.when(c == 0)
        def _():
            dh_scr[...] = jnp.zeros_like(dh_scr)

        mA = _lane_mask()
        masks = (mA, jnp.logical_not(mA))
        a_b, dt, acum, acum_r = _ssd_common(dtc_ref, dtr_ref, bias_ref[0], alog_ref[...], biasc_ref[0], alogc_ref[0], L)
        x, zz, y, dssm = x_ref[...], z_ref[...].astype(F32), y_ref[...], dssm_ref[...]
        cb, bb = c_ref[...].astype(BF16), b_ref[...].astype(BF16)
        hprev = hs_ref[0, 0]
        hb = hprev.astype(BF16)
        ds = dh_scr[...]
        dsb = ds.astype(BF16)
        dskip = dskip_ref[...]
        aend = acum[L - 1:L, :]
        e_a, e_end = jnp.exp(acum), jnp.exp(aend)
        dte = jnp.exp(aend - acum)
        xdt = x * dt
        xdt_b = xdt.astype(BF16)
        sg = _sigmoid(zz)
        sz = zz * sg
        u = y * sz
        rs = lax.rsqrt(jnp.mean(u * u, axis=1, keepdims=True) + RMS_EPS)
        un = u * rs
        dun = dssm * nw_ref[...]
        du = rs * (dun - un * jnp.mean(dun * un, axis=1, keepdims=True))
        dy = du * sz
        dz_ref[...] = (du * y * sg * (1.0 + zz * (1.0 - sg))).astype(dz_ref.dtype)
        dy_b = dy.astype(BF16)
        dch_b = (dy * e_a).astype(BF16)
        dc = _dot_nt(dch_b, hb)
        dhprev = _dot_tn(cb, dch_b)
        gt = _dot_nt(bb, cb)
        row = lax.broadcasted_iota(jnp.int32, (L, L), 0)
        col = lax.broadcasted_iota(jnp.int32, (L, L), 1)
        dgt = jnp.zeros((L, L), F32)
        parts = []
        for p in range(GROUP_LANES // LANES):
            xp = xdt_b[:, p * LANES:(p + 1) * LANES]
            dyp = dy_b[:, p * LANES:(p + 1) * LANES]
            zero = jnp.zeros_like(dyp)
            acc = None
            for hh in (0, 1):
                r = 2 * p + hh
                acol = acum[:, r * ATT_HEAD_DIM:r * ATT_HEAD_DIM + 1]
                arow = acum_r[r:r + 1, :]
                lmt = jnp.exp(jnp.where(row <= col, arow - acol, NEG))
                dyh = jnp.where(masks[hh], dyp, zero)
                part = _dot((gt * lmt).astype(BF16), dyh)
                acc = part if acc is None else acc + part
                dgt = dgt + _dot_nt(xp, dyh) * lmt
            parts.append(acc)
        dxdt_diag = jnp.concatenate(parts, axis=1)
        dgt_b = dgt.astype(BF16)
        db = _dot(dgt_b, cb)
        dc = dc + _dot_tn(dgt_b, bb)
        dxdt_state = dte * _dot(bb, dsb)
        db = db + _dot_nt((dte * xdt).astype(BF16), dsb)
        dxdt = dxdt_diag + dxdt_state
        dy_r, xdt_r = dy_b.astype(F32), xdt_b.astype(F32)
        dac = dy_r * (y - dskip * x) - xdt_r * dxdt
        tail = jnp.sum(xdt_r * dxdt_state, axis=0, keepdims=True) + e_end * jnp.sum(ds * hprev, axis=0, keepdims=True)
        rowl = lax.broadcasted_iota(jnp.int32, (L, 1), 0)
        dac = dac + jnp.where(rowl == L - 1, tail, 0.0)
        rc = _dot_exact_left(_tri(L, True), dac)
        hsum = _head_sum()
        hs1 = _dot_exact_right(dxdt * x, hsum, 2)
        hs2 = _dot_exact_right(rc, hsum, 2)
        a8 = -jnp.exp(alogr_ref[0])
        dtraw8 = dtc_ref[0] + bias_ref[0]
        ddtraw = (hs1 + a8 * hs2) * _sigmoid(dtraw8)
        dx_ref[...] = dskip * dy + dxdt * dt
        db_ref[...] = db
        dc_ref[...] = dc
        ddt_ref[0] = ddtraw
        dh_scr[...] = e_end * ds + dhprev
        sums = (jnp.sum(dssm * un, axis=0, keepdims=True), jnp.sum(dy * x, axis=0, keepdims=True))
        refs = (dnw_ref, ddskip_ref)
        sums8 = (a8 * jnp.sum(hs2 * _softplus(dtraw8), axis=0, keepdims=True), jnp.sum(ddtraw, axis=0, keepdims=True))
        refs8 = (dalog_ref, dbias_ref)

        @pl.when(c == 0)
        def _():
            for r, v in zip(refs, sums):
                r[...] = v
            for r, v in zip(refs8, sums8):
                r[0] = v

        @pl.when(c > 0)
        def _():
            for r, v in zip(refs, sums):
                r[...] += v
            for r, v in zip(refs8, sums8):
                r[0] += v

    nbc = pl.BlockSpec((L, SSM_STATE), lambda g, c: (nc - 1 - c, g))
    return pl.pallas_call(
        body, name="ssd_bwd", grid=(SSM_GROUPS, nc),
        in_specs=[xs, bm, cm, blk, blk, blk, hs, dtcs, dtrs, rowv, rowv, vec, vec, vec, colv, colv],
        out_specs=[blk, nbc, nbc, blk, dtcs, vec, vec, rowv, rowv],
        out_shape=[jax.ShapeDtypeStruct((S, SSM_INNER), F32), jax.ShapeDtypeStruct((S, SSM_GROUPS * SSM_STATE), F32),
                   jax.ShapeDtypeStruct((S, SSM_GROUPS * SSM_STATE), F32), jax.ShapeDtypeStruct((S, SSM_INNER), BF16),
                   jax.ShapeDtypeStruct((SSM_GROUPS, S, 8), F32)] + [jax.ShapeDtypeStruct((1, SSM_INNER), F32)] * 2
                  + [jax.ShapeDtypeStruct((SSM_GROUPS, 1, 8), F32)] * 2,
        scratch_shapes=[pltpu.VMEM((SSM_STATE, GROUP_LANES), F32)],
        compiler_params=_params(2, 56 << 20))(xbc, xbc, xbc, z, y, dssm, hs_all, dtc, dtr, bias_r, alog_r, alog_b, dskip_b,
                                              normw, bias_c, alog_c)


def _place():
    return lax.axis_index("x"), lax.axis_index("y"), lax.axis_index("c")


def _other_chips(x, y):
    return [(1 - x, y), (x, 1 - y), (1 - x, 1 - y)]


def _half_rows(rows, which):
    hr = rows // 2
    if isinstance(which, int):
        return pl.ds(which * hr, hr)
    return pl.ds(pl.multiple_of(which * hr, 8), hr)


def _chip_gather(name, shards, split):
    n = len(shards)
    ANY = pl.BlockSpec(memory_space=pl.ANY)

    def body(*refs):
        ins, outs = refs[:n], refs[n:2 * n]
        send, recv, fsend, frecv = refs[2 * n:]
        x, y, c = _place()
        me = 2 * x + y
        sibling = (x, y, 1 - c)
        chips = _other_chips(x, y)

        def piece(a, chip_idx, which):
            if split[a]:
                return outs[a].at[chip_idx, _half_rows(shards[a].shape[0], which)]
            return outs[a].at[chip_idx]

        def ici(k, a, to_chip, src_chip):
            src = ins[a].at[_half_rows(shards[a].shape[0], c)] if split[a] else ins[a]
            return pltpu.make_async_remote_copy(src_ref=src, dst_ref=piece(a, src_chip, c), send_sem=send.at[k, a],
                                                recv_sem=recv.at[k, a], device_id=(*to_chip, c), device_id_type=MESH)

        def fwd(k, a, src_chip, which):
            return pltpu.make_async_remote_copy(src_ref=piece(a, src_chip, which), dst_ref=piece(a, src_chip, which),
                                                send_sem=fsend.at[k, a], recv_sem=frecv.at[k, a], device_id=sibling,
                                                device_id_type=MESH)

        sends = []
        for k, chip in enumerate(chips):
            for a in range(n):
                cp = ici(k, a, chip, me)
                cp.start()
                sends.append(cp)
        for k, (ox, oy) in enumerate(chips):
            src = 2 * ox + oy
            for a in range(n):
                ici(k, a, (ox, oy), src).wait_recv()
                if split[a]:
                    cp = fwd(k, a, src, c)
                    cp.start()
                    sends.append(cp)
        for k, (ox, oy) in enumerate(chips):
            for a in range(n):
                if split[a]:
                    fwd(k, a, 2 * ox + oy, 1 - c).wait_recv()
        for cp in sends:
            cp.wait_send()

    sem = pltpu.SemaphoreType.DMA((3, n))
    return pl.pallas_call(
        body, name=name, in_specs=[ANY] * n, out_specs=[ANY] * n,
        out_shape=[jax.ShapeDtypeStruct((4,) + s.shape, s.dtype) for s in shards],
        scratch_shapes=[sem, sem, sem, sem])(*shards)


def _chip_copies_start(name, srcs, per_chip_src, after):
    n = len(srcs)
    HBM = pl.BlockSpec(memory_space=pltpu.HBM)
    SEM = pl.BlockSpec(memory_space=pltpu.SEMAPHORE)
    lands = [pltpu.with_memory_space_constraint(lax.empty(a.shape if per_chip_src else (4,) + a.shape, a.dtype), pltpu.HBM)
             for a in srcs]

    def body(*refs):
        ins, land = refs[:n], refs[n:2 * n]
        send, recv = refs[2 * n + 1], refs[2 * n + 2]
        token = refs[-1]
        x, y, c = _place()
        me = 2 * x + y
        for k, (ox, oy) in enumerate(_other_chips(x, y)):
            for a in range(n):
                src = ins[a].at[2 * ox + oy] if per_chip_src else ins[a]
                pltpu.make_async_remote_copy(src_ref=src, dst_ref=land[a].at[me], send_sem=send.at[k * n + a], recv_sem=recv.at[k * n + a],
                                             device_id=(ox, oy, c), device_id_type=MESH).start()
        token[...] = jnp.zeros_like(token)

    sem = pltpu.SemaphoreType.DMA((3 * n,))
    res = pl.pallas_call(
        body, name=name,
        out_shape=[sem, sem] + [pltpu.HBM(a.shape, a.dtype) for a in srcs] + [pltpu.HBM(b.shape, b.dtype) for b in lands]
                  + [jax.ShapeDtypeStruct((8, LANES), F32)],
        in_specs=[HBM] * (2 * n) + [pl.BlockSpec(memory_space=pl.ANY)],
        out_specs=[SEM, SEM] + [HBM] * (2 * n) + [pl.BlockSpec(memory_space=pltpu.VMEM)],
        input_output_aliases={k: 2 + k for k in range(2 * n)},
        compiler_params=pltpu.CompilerParams(has_side_effects=pltpu.SideEffectType.DATAFLOW_SIDE_EFFECTING),
    )(*[pltpu.with_memory_space_constraint(a, pltpu.HBM) for a in srcs], *lands, after)
    return res[:-1], res[-1]


def _chip_copies_wait(name, started, per_chip_src, after):
    send, recv = started[0], started[1]
    n = (len(started) - 2) // 2
    srcs, lands = started[2:2 + n], started[2 + n:]
    HBM = pl.BlockSpec(memory_space=pltpu.HBM)
    SEM = pl.BlockSpec(memory_space=pltpu.SEMAPHORE)

    def body(*refs):
        ins, land = refs[:n], refs[n:2 * n]
        send_sem, recv_sem = refs[2 * n], refs[2 * n + 1]
        x, y, c = _place()
        me = 2 * x + y
        for k, (ox, oy) in enumerate(_other_chips(x, y)):
            for a in range(n):
                src = ins[a].at[me] if per_chip_src else ins[a]
                cp = pltpu.make_async_remote_copy(src_ref=src, dst_ref=land[a].at[2 * ox + oy], send_sem=send_sem.at[k * n + a],
                                                  recv_sem=recv_sem.at[k * n + a], device_id=(ox, oy, c), device_id_type=MESH)
                cp.wait_send()
                cp.wait_recv()

    res = pl.pallas_call(
        body, name=name,
        out_shape=[pltpu.HBM(a.shape, a.dtype) for a in srcs] + [pltpu.HBM(b.shape, b.dtype) for b in lands],
        in_specs=[HBM] * (2 * n) + [SEM, SEM, pl.BlockSpec(memory_space=pl.ANY)], out_specs=[HBM] * (2 * n),
        input_output_aliases={k: k for k in range(2 * n)},
        compiler_params=pltpu.CompilerParams(has_side_effects=pltpu.SideEffectType.DATAFLOW_SIDE_EFFECTING),
    )(*srcs, *lands, send, recv, after)
    return res[n:]


def _half_to_sibling(name, blocks):
    n = len(blocks)
    ANY = pl.BlockSpec(memory_space=pl.ANY)

    def body(*refs):
        ins, outs = refs[:n], refs[n:2 * n]
        send, recv = refs[2 * n:]
        x, y, c = _place()
        cps = [pltpu.make_async_remote_copy(src_ref=ins[a].at[:, _half_rows(blocks[a].shape[1], 1 - c)], dst_ref=outs[a],
                                            send_sem=send.at[a], recv_sem=recv.at[a], device_id=(x, y, 1 - c),
                                            device_id_type=MESH) for a in range(n)]
        for cp in cps:
            cp.start()
        for cp in cps:
            cp.wait_recv()
        for cp in cps:
            cp.wait_send()

    return pl.pallas_call(
        body, name=name, in_specs=[ANY] * n, out_specs=[ANY] * n,
        out_shape=[jax.ShapeDtypeStruct((4, b.shape[1] // 2, b.shape[2]), b.dtype) for b in blocks],
        scratch_shapes=[pltpu.SemaphoreType.DMA((n,)), pltpu.SemaphoreType.DMA((n,))])(*blocks)


def _sibling_swap(name, arrs):
    n = len(arrs)
    ANY = pl.BlockSpec(memory_space=pl.ANY)

    def body(*refs):
        ins, outs = refs[:n], refs[n:2 * n]
        send, recv = refs[2 * n:]
        x, y, c = _place()
        cps = [pltpu.make_async_remote_copy(src_ref=ins[a], dst_ref=outs[a], send_sem=send.at[a], recv_sem=recv.at[a],
                                            device_id=(x, y, 1 - c), device_id_type=MESH) for a in range(n)]
        for cp in cps:
            cp.start()
        for cp in cps:
            cp.wait_recv()
        for cp in cps:
            cp.wait_send()

    return pl.pallas_call(
        body, name=name, in_specs=[ANY] * n, out_specs=[ANY] * n,
        out_shape=[jax.ShapeDtypeStruct(a.shape, a.dtype) for a in arrs],
        scratch_shapes=[pltpu.SemaphoreType.DMA((n,)), pltpu.SemaphoreType.DMA((n,))])(*arrs)


N_DEV = 8


def _all_sum_small(vec):
    P = vec.shape[1]

    def body(v_ref, o_ref, buf, send, recv):
        x, y, c = _place()
        me = 4 * x + 2 * y + c
        buf[me] = v_ref[...]

        def peer(r):
            return ((1 - x) if (r >> 2) & 1 else x, (1 - y) if (r >> 1) & 1 else y, (1 - c) if r & 1 else c)

        sends = []
        for r in range(1, N_DEV):
            cp = pltpu.make_async_remote_copy(src_ref=v_ref, dst_ref=buf.at[me], send_sem=send.at[r], recv_sem=recv.at[r],
                                              device_id=peer(r), device_id_type=MESH)
            cp.start()
            sends.append(cp)
        for r in range(1, N_DEV):
            px, py, pc = peer(r)
            pltpu.make_async_remote_copy(src_ref=v_ref, dst_ref=buf.at[4 * px + 2 * py + pc], send_sem=send.at[r],
                                         recv_sem=recv.at[r], device_id=(px, py, pc), device_id_type=MESH).wait_recv()
        for cp in sends:
            cp.wait_send()
        tot = buf[0]
        for d in range(1, N_DEV):
            tot = tot + buf[d]
        o_ref[...] = tot

    return pl.pallas_call(
        body, name="all_sum_small", in_specs=[pl.BlockSpec(memory_space=pltpu.VMEM)],
        out_specs=pl.BlockSpec(memory_space=pltpu.VMEM), out_shape=jax.ShapeDtypeStruct((1, P), F32),
        scratch_shapes=[pltpu.VMEM((N_DEV, 1, P), F32), pltpu.SemaphoreType.DMA((N_DEV,)), pltpu.SemaphoreType.DMA((N_DEV,))],
    )(vec)


def _half_sum(name, blocks, theirs, core, tr):
    _, R, C = blocks.shape
    hr = R // 2
    nb = hr // tr
    assert nb * tr == hr

    def body(c_ref, a_ref, b_ref, o_ref):
        o_ref[...] = (a_ref[...] + b_ref[...]).astype(BF16)

    grid_spec = pltpu.PrefetchScalarGridSpec(
        num_scalar_prefetch=1, grid=(4, nb),
        in_specs=[pl.BlockSpec((1, tr, C), lambda b, i, c_ref: (b, c_ref[0] * nb + i, 0)),
                  pl.BlockSpec((1, tr, C), lambda b, i, c_ref: (b, i, 0))],
        out_specs=pl.BlockSpec((1, tr, C), lambda b, i, c_ref: (b, i, 0)))
    return pl.pallas_call(body, name=name, grid_spec=grid_spec, out_shape=jax.ShapeDtypeStruct((4, hr, C), BF16),
                          compiler_params=_params(2, 40 << 20))(core, blocks, theirs)


def _sum4(name, stack, mine, chip, tr):
    _, R, C = stack.shape

    def body(chip_ref, s_ref, m_ref, o_ref):
        t = [jnp.where(chip_ref[0] == j, m_ref[j], s_ref[j]).astype(F32) for j in range(4)]
        o_ref[...] = ((t[0] + t[1]) + t[2]) + t[3]

    blk = pl.BlockSpec((4, tr, C), lambda i, chip_ref: (0, i, 0))
    grid_spec = pltpu.PrefetchScalarGridSpec(num_scalar_prefetch=1, grid=(R // tr,), in_specs=[blk, blk],
                                             out_specs=pl.BlockSpec((tr, C), lambda i, chip_ref: (i, 0)))
    return pl.pallas_call(body, name=name, grid_spec=grid_spec, out_shape=jax.ShapeDtypeStruct((R, C), F32),
                          compiler_params=_params(1, 40 << 20))(chip, stack, mine)


def _adamw_math(w, m, v, g):
    c1 = 1.0 - ADAM_B1 ** ADAM_STEP
    c2 = 1.0 - ADAM_B2 ** ADAM_STEP
    nm = ADAM_B1 * m + (1.0 - ADAM_B1) * g
    nv = ADAM_B2 * v + (1.0 - ADAM_B2) * (g * g)
    return -ADAM_LR * ((nm / c1) / (jnp.sqrt(nv / c2) + ADAM_EPS) + ADAM_WD * w), nm, nv


def _adamw(name, w, m, v, g, tr):
    R, C = w.shape

    def body(w_ref, m_ref, v_ref, ga_ref, g_ref, d_ref, nm_ref, nv_ref):
        g = ga_ref[...]
        g_ref[...] = g
        d_ref[...], nm_ref[...], nv_ref[...] = _adamw_math(w_ref[...], m_ref[...], v_ref[...], g)

    spec = pl.BlockSpec((tr, C), lambda i: (i, 0))
    return pl.pallas_call(body, name=name, grid=(R // tr,), in_specs=[spec] * 4, out_specs=[spec] * 4,
                          out_shape=[jax.ShapeDtypeStruct((R, C), F32)] * 4, compiler_params=_params(1, 40 << 20))(w, m, v, g)


def _adamw_flat(name, w, m, v, g, tr):
    R, C = w.shape

    def body(w_ref, m_ref, v_ref, g_ref, d_ref, nm_ref, nv_ref):
        d_ref[...], nm_ref[...], nv_ref[...] = _adamw_math(w_ref[...], m_ref[...], v_ref[...], g_ref[...])

    spec = pl.BlockSpec((tr, C), lambda i: (i, 0))
    return pl.pallas_call(body, name=name, grid=(R // tr,), in_specs=[spec] * 4, out_specs=[spec] * 3,
                          out_shape=[jax.ShapeDtypeStruct((R, C), F32)] * 3, compiler_params=_params(1, 48 << 20))(w, m, v, g)


def _adamw_halves(name, w, m, v, mine, theirs, core, tr):
    _, R, C = w.shape
    nb = (R // 2) // tr
    assert 2 * nb * tr == R

    def body(c_ref, w_ref, m_ref, v_ref, a_ref, b_ref, g_ref, d_ref, nm_ref, nv_ref):
        g = jnp.where((pl.program_id(0) // nb) == c_ref[0], a_ref[...], b_ref[...])
        g_ref[0] = g
        d_ref[0], nm_ref[0], nv_ref[0] = _adamw_math(w_ref[0], m_ref[0], v_ref[0], g)

    spec = pl.BlockSpec((1, tr, C), lambda i, c_ref: (0, i, 0))
    half = lambda own: pl.BlockSpec((tr, C), lambda i, c_ref, own=own: (
        jnp.clip(i - (c_ref[0] if own else 1 - c_ref[0]) * nb, 0, nb - 1), 0))
    grid_spec = pltpu.PrefetchScalarGridSpec(num_scalar_prefetch=1, grid=(R // tr,),
                                             in_specs=[spec, spec, spec, half(True), half(False)], out_specs=[spec] * 4)
    return pl.pallas_call(body, name=name, grid_spec=grid_spec, out_shape=[jax.ShapeDtypeStruct((1, R, C), F32)] * 4,
                          compiler_params=_params(1, 40 << 20))(core, w, m, v, mine, theirs)


def _row_tile(rows, cols, budget_bytes=1 << 20, mult=8):
    best = None
    for t in range(mult, rows + 1, mult):
        if rows % t == 0 and t * cols * 4 <= budget_bytes:
            best = t
    return best if best is not None else rows


def _ln_fwd(r, g, b):
    mu = jnp.mean(r, axis=1, keepdims=True)
    xc = r - mu
    rstd = lax.rsqrt(jnp.mean(xc * xc, axis=1, keepdims=True) + LN_EPS)
    xhat = xc * rstd
    return xhat * g + b, xhat, rstd


def _ln_bwd(dy, xhat, rstd, g):
    dxh = dy * g
    return rstd * (dxh - jnp.mean(dxh, axis=1, keepdims=True) - xhat * jnp.mean(dxh * xhat, axis=1, keepdims=True))


def _col_segments(sections, width):
    out, cur, room = [], [], width
    for arr, lo, hi in sections:
        while lo < hi:
            take = min(room, hi - lo)
            cur.append((arr, lo, lo + take))
            lo, room = lo + take, room - take
            if room == 0:
                out.append(cur)
                cur, room = [], width
    assert not cur
    return out


def _to_chip_blocks_cols(a):
    R, C4 = a.shape
    return a.reshape(R, 4, C4 // 4).transpose(1, 0, 2)


def _from_chip_blocks_cols(a):
    return a.transpose(1, 0, 2).reshape(a.shape[1], 4 * a.shape[2])


def kernel(x, w_in, b_forget, conv_w, conv_b, dt_bias, a_log, d_skip, ssm_norm_w, w_proj_attn, w_proj_ssm, b_gates, w_out, ln1_g, ln1_b, w_ffn_gate, w_ffn_up, w_ffn_down, ln2_g, ln2_b, loss_target, m_w_in, m_b_forget, m_conv_w, m_conv_b, m_dt_bias, m_a_log, m_d_skip, m_ssm_norm_w, m_w_proj_attn, m_w_proj_ssm, m_b_gates, m_w_out, m_ln1_g, m_ln1_b, m_w_ffn_gate, m_w_ffn_up, m_w_ffn_down, m_ln2_g, m_ln2_b, v_w_in, v_b_forget, v_conv_w, v_conv_b, v_dt_bias, v_a_log, v_d_skip, v_ssm_norm_w, v_w_proj_attn, v_w_proj_ssm, v_b_gates, v_w_out, v_ln1_g, v_ln1_b, v_w_ffn_gate, v_w_ffn_up, v_w_ffn_down, v_ln2_g, v_ln2_b):
    S = x.shape[1]
    D = D_MODEL
    TM, TM2, TM3, TA, AQF, LC, CV, TS, TB = (min(TILES[k], S) for k in ("TM", "TM2", "TM3", "TA", "AQF", "LC", "CV", "TS", "TB"))
    xf = x[0]
    tgt = loss_target[0]
    xb = xf.astype(BF16)

    shards = [w_in[0].astype(BF16), conv_w[0], w_proj_attn[0].astype(BF16), w_proj_ssm[0].astype(BF16), w_out[0].astype(BF16),
              w_ffn_gate[0].astype(BF16), w_ffn_up[0].astype(BF16), w_ffn_down[0].astype(BF16)]
    chip = 2 * lax.axis_index("x") + lax.axis_index("y")
    own = lambda gathered, mine: [lax.dynamic_update_slice(g, sh[None], (chip, 0, 0)) for g, sh in zip(gathered, mine)]
    g_in, g_cw = own(_chip_gather("gather_w_in", shards[:2], [True, False]), shards[:2])
    later, gather_token = _chip_copies_start("gather_rest_start", shards[2:], False, g_cw)
    shard_w = IN_WIDTH // 4

    def w_cols(lo, hi):
        return [g_in[j][:, max(lo, j * shard_w) - j * shard_w:min(hi, (j + 1) * shard_w) - j * shard_w]
                for j in range(4) if max(lo, j * shard_w) < min(hi, (j + 1) * shard_w)]

    w_re = jnp.concatenate(w_cols(0, 3072) + w_cols(3088, 5136) + w_cols(5136, 8208) + w_cols(8240, 10288)
                           + w_cols(3072, 3088) + w_cols(8208, 8240) + [jnp.zeros((D, 80), BF16)], axis=1)
    conv_w_full = _from_chip_blocks_cols(g_cw)

    def plain(accs, rows, vecs, j):
        return [accs[0]], []

    def q_scaled(accs, rows, vecs, j):
        return [accs[0] * jnp.where(j * 512 < D, 1.0 / math.sqrt(ATT_HEAD_DIM), 1.0)], []

    qkv, = _mm("proj_qkv", S, 3072, TM, 512, [(xb, D, 0)], [(w_re, 0)], [(0, 0)], q_scaled, [(3072, BF16, 0)],
               after=[gather_token])
    z, = _mm("proj_z", S, 2048, TM, 512, [(xb, D, 0)], [(w_re, RE_Z // 512)], [(0, 0)], plain, [(2048, BF16, 0)])
    xbc_raw, = _mm("proj_xbc", S, 3072, TM, 512, [(xb, D, 0)], [(w_re, RE_XBC // 512)], [(0, 0)], plain, [(3072, F32, 0)])
    gl, = _mm("proj_gate", S, 2048, TM, 512, [(xb, D, 0)], [(w_re, RE_GATE // 512)], [(0, 0)], plain, [(2048, BF16, 0)])
    small, = _mm("proj_small", S, 128, TM, 128, [(xb, D, 0)], [(w_re, RE_SMALL // 128)], [(0, 0)], plain, [(128, F32, 0)])

    bvec = jnp.concatenate([b_forget, jnp.zeros((1, LANES - ATT_HEADS), F32)], axis=1)
    cum = _cum_fwd(small, bvec, TB)[:, :ATT_HEADS]
    cum_cols = cum.reshape(S, 8, 2).transpose(1, 0, 2)
    qa, ka = _attn_prep(qkv, cum_cols, TM)
    o, o32, lse_rows = _attn_fwd(qa, ka, qkv, AQF, TA)

    cb_row = conv_b
    xbc = _conv_fwd(xbc_raw, conv_w_full, cb_row, CV, 512)
    dt_raw = small[:, 16:48]
    dtc = dt_raw.reshape(S, SSM_GROUPS, 8).transpose(1, 0, 2)
    dtr = dt_raw.T.reshape(SSM_GROUPS, 8, S)
    bias_r = dt_bias.reshape(SSM_GROUPS, 1, 8)
    alog_b = jnp.repeat(a_log, ATT_HEAD_DIM, axis=1)
    dskip_b = jnp.repeat(d_skip, ATT_HEAD_DIM, axis=1)
    bias_c = dt_bias.reshape(SSM_GROUPS, 8, 1)
    alog_c = a_log.reshape(SSM_GROUPS, 8, 1)
    y_ssd, ssm, hs_all = _ssd_fwd(xbc, z, dtc, dtr, bias_r, alog_b, dskip_b, ssm_norm_w, bias_c, alog_c, LC)

    def merge(accs, rows, vecs, j):
        g0, g1 = _sigmoid(rows[0].astype(F32) + vecs[0]), _sigmoid(rows[1].astype(F32) + vecs[1])
        return [g0 * accs[0] + g1 * accs[1], accs[0], accs[1]], []

    g_pa, g_ps, g_out, g_fg, g_fu, g_fd = own(_chip_copies_wait("gather_rest_wait", later, False, o), shards[2:])
    wpa, wps, wout = g_pa.reshape(D, D), g_ps.reshape(SSM_INNER, D), g_out.reshape(D, D)
    wfg, wfu, wfd = _from_chip_blocks_cols(g_fg), _from_chip_blocks_cols(g_fu), g_fd.reshape(FFN_HIDDEN, D)
    mix, attn_d, ssm_d = _mm("merge", S, D, TM, 512, [(o, D, 0), (ssm, SSM_INNER, 0)], [(wpa, 0), (wps, 0)], [(0, 0), (1, 1)],
                             merge, [(D, BF16, 0), (D, BF16, 0), (D, BF16, 0)], rows=[(gl, 0), (gl, 2)],
                             vecs_n=[(b_gates, 0), (b_gates, 2)])

    def out_ln1(accs, rows, vecs, j):
        r1 = ALPHA * rows[0] + accs[0]
        h1, _, _ = _ln_fwd(r1, vecs[0], vecs[1])
        return [r1, h1, h1], []

    r1, h1, h1b = _mm("out_ln1", S, D, TM2, D, [(mix, D, 0)], [(wout, 0)], [(0, 0)], out_ln1,
                      [(D, F32, 0), (D, F32, 0), (D, BF16, 0)], rows=[(xf, 0)], vecs_n=[(ln1_g, 0), (ln1_b, 0)])

    FT = FFN_HIDDEN // 2

    def swiglu(accs, rows, vecs, j):
        g, u = accs
        return [g, u, g * _sigmoid(g) * u], []

    gate, up, hmid = _mm("ffn_up", S, FFN_HIDDEN, TM3, FT, [(h1b, D, 0)], [(wfg, 0), (wfu, 0)], [(0, 0), (0, 1)], swiglu,
                         [(FFN_HIDDEN, BF16, 0), (FFN_HIDDEN, BF16, 0), (FFN_HIDDEN, BF16, 0)])

    def down_ln2_loss(accs, rows, vecs, j):
        r2 = ALPHA * rows[0] + accs[0]
        yv, xhat, rstd = _ln_fwd(r2, vecs[0], vecs[1])
        diff = yv - rows[1]
        dy = diff * (1.0 / D_MODEL)
        dr2 = _ln_bwd(dy, xhat, rstd, vecs[0])
        return [dr2, dr2], [jnp.sum(dy * xhat, axis=0, keepdims=True), jnp.sum(dy, axis=0, keepdims=True),
                            (0.5 / D_MODEL) * jnp.sum(diff * diff, axis=0, keepdims=True)]

    dr2, dr2b, dln2_g, dln2_b, loss_lanes = _mm("ffn_down_ln2", S, D, TM3, D, [(hmid, FFN_HIDDEN, 0)], [(wfd, 0)], [(0, 0)],
                                               down_ln2_loss, [(D, F32, 0), (D, BF16, 0)], rows=[(h1, 0), (tgt, 0)],
                                               vecs_n=[(ln2_g, 0), (ln2_b, 0)], sums=[D, D, D])
    loss = lax.psum(jnp.sum(loss_lanes), ("x", "y", "c"))

    def dswiglu(accs, rows, vecs, j):
        g, u = rows[0].astype(F32), rows[1].astype(F32)
        sg = _sigmoid(g)
        return [accs[0] * u * sg * (1.0 + g * (1.0 - sg)), accs[0] * g * sg], []

    dgate, dup = _mm("ffn_down_bwd", S, FFN_HIDDEN, TM3, FT, [(dr2b, D, 0)], [(wfd, 0)], [(0, 0)], dswiglu,
                     [(FFN_HIDDEN, BF16, 0), (FFN_HIDDEN, BF16, 0)], nt=True, rows=[(gate, 0), (up, 0)])
    dwfd = _mm_tn("dw_ffn_down", hmid, dr2b, FFN_HIDDEN // 2, D, TS)
    dwfg = _mm_tn("dw_ffn_gate", h1b, dgate, D, FT, TS)
    dwfu = _mm_tn("dw_ffn_up", h1b, dup, D, FT, TS)
    core = lax.axis_index("c").astype(jnp.int32).reshape(1)

    def send_grads(tag, names_, blocks_, after_):
        theirs_ = _half_to_sibling("swap_halves_" + tag, blocks_)
        halves_ = [_half_sum("halfsum_" + nm, b, t, core, _row_tile(b.shape[1] // 2, b.shape[2], mult=16))
                   for nm, b, t in zip(names_, blocks_, theirs_)]
        started_, token_ = _chip_copies_start("scatter_" + tag + "_start", halves_, True, after_)
        return halves_, started_, token_

    ffn_names = ["w_ffn_gate", "w_ffn_up", "w_ffn_down"]
    ffn_halves, ffn_started, ffn_token = send_grads(
        "ffn", ffn_names, [_to_chip_blocks_cols(dwfg), _to_chip_blocks_cols(dwfu), dwfd.reshape(4, FFN_HIDDEN // 4, D)], dwfu)

    def dh1_ln1(accs, rows, vecs, j):
        dh1 = ALPHA * rows[0] + accs[0] + accs[1]
        _, xhat, rstd = _ln_fwd(rows[1], vecs[0], vecs[0])
        dr1 = _ln_bwd(dh1, xhat, rstd, vecs[0])
        return [dr1, dr1], [jnp.sum(dh1 * xhat, axis=0, keepdims=True), jnp.sum(dh1, axis=0, keepdims=True)]

    dr1, dr1b, dln1_g, dln1_b = _mm("ffn_up_bwd_ln1", S, D, TM2, D, [(dgate, FFN_HIDDEN, 0), (dup, FFN_HIDDEN, 0)],
                                    [(wfg, 0), (wfu, 0)], [(0, 0), (1, 1)], dh1_ln1, [(D, F32, 0), (D, BF16, 0)], nt=True,
                                    rows=[(dr2, 0), (r1, 0)], vecs_n=[(ln1_g, 0)], sums=[D, D], after=[ffn_token])

    def dmerge(accs, rows, vecs, j):
        dmix = accs[0]
        g0, g1 = _sigmoid(rows[0].astype(F32) + vecs[0]), _sigmoid(rows[1].astype(F32) + vecs[1])
        dgl0 = dmix * rows[2].astype(F32) * g0 * (1.0 - g0)
        dgl1 = dmix * rows[3].astype(F32) * g1 * (1.0 - g1)
        return [dmix * g0, dmix * g1, dgl0, dgl1], [jnp.sum(dgl0, axis=0, keepdims=True), jnp.sum(dgl1, axis=0, keepdims=True)]

    d_attn_d, d_ssm_d, dgl0, dgl1, dbg0, dbg1 = _mm(
        "out_bwd", S, D, TM, 512, [(dr1b, D, 0)], [(wout, 0)], [(0, 0)], dmerge, [(D, BF16, 0)] * 4, nt=True,
        rows=[(gl, 0), (gl, 2), (attn_d, 0), (ssm_d, 0)], vecs_n=[(b_gates, 0), (b_gates, 2)], sums=[D, D])
    dwout = _mm_tn("dw_out", mix, dr1b, D, D, TS)
    dwpa = _mm_tn("dw_proj_attn", o, d_attn_d, D, D, TS)
    dwps = _mm_tn("dw_proj_ssm", ssm, d_ssm_d, D, D, TS)
    mid_names = ["w_proj_attn", "w_proj_ssm", "w_out"]
    mid_halves, mid_started, mid_token = send_grads(
        "mid", mid_names, [dwpa.reshape(4, D // 4, D), dwps.reshape(4, SSM_INNER // 4, D), dwout.reshape(4, D // 4, D)], dwps)

    do, = _mm("proj_attn_bwd", S, D, TM, 512, [(d_attn_d, D, 0)], [(wpa, 0)], [(0, 0)], plain, [(D, BF16, 0)], nt=True,
              after=[mid_token])
    stats = _attn_stats(do, o32, lse_rows.transpose(0, 2, 1), AQF)
    dq, dk, dv, dck, dcq = _attn_bwd(qa, ka, qkv, do, stats, TA)

    def per_head(a):
        a = a.transpose(1, 0, 2).reshape(S, ATT_HEADS)
        return jnp.concatenate([a, jnp.zeros((S, LANES - ATT_HEADS), F32)], axis=1)

    dfl, dbf = _cum_bwd(per_head(dck.transpose(0, 2, 1)), per_head(dcq), small, bvec, TB)

    dssm, = _mm("proj_ssm_bwd", S, SSM_INNER, TM, 512, [(d_ssm_d, D, 0)], [(wps, 0)], [(0, 0)], plain, [(SSM_INNER, F32, 0)],
                nt=True)
    dxs, dbm, dcm, dz, ddt8, dnw, ddskip_b, dalog8, dbias8 = _ssd_bwd(
        xbc, z, y_ssd, dssm, hs_all, dtc, dtr, bias_r, a_log.reshape(SSM_GROUPS, 1, 8), alog_b, dskip_b, ssm_norm_w, bias_c,
        alog_c, LC)
    du_x, dcw_x, dcb_x = _conv_bwd("conv_bwd_x", xbc_raw, dxs, conv_w_full, cb_row, CV, 512, 0)
    du_b, dcw_b, dcb_b = _conv_bwd("conv_bwd_b", xbc_raw, dbm, conv_w_full, cb_row, CV, 512, SSM_INNER)
    du_c, dcw_c, dcb_c = _conv_bwd("conv_bwd_c", xbc_raw, dcm, conv_w_full, cb_row, CV, 512, SSM_INNER + SSM_GROUPS * SSM_STATE)
    dconv_w = jnp.concatenate([dcw_x, dcw_b, dcw_c], axis=1)
    dconv_b = jnp.concatenate([dcb_x, dcb_b, dcb_c], axis=1)
    ddt_raw = ddt8.transpose(1, 0, 2).reshape(S, SSM_HEADS)

    dsmall = jnp.concatenate([dfl[:, :ATT_HEADS], ddt_raw, jnp.zeros((S, 80), F32)], axis=1).astype(BF16)
    HB = SSM_GROUPS * SSM_STATE
    dw_q, dw_k, dw_v = (_mm_tn("dw_in_" + nm, xb, g_, D, D, TS) for nm, g_ in (("q", dq), ("k", dk), ("v", dv)))
    dw_z = _mm_tn("dw_in_z", xb, dz, D, D, TS)
    dw_xs, dw_b, dw_c = _mm_tn("dw_in_xs", xb, du_x, D, D, TS), _mm_tn("dw_in_b", xb, du_b, D, HB, TS), _mm_tn("dw_in_c", xb, du_c, D, HB, TS)
    dw_g0, dw_g1 = _mm_tn("dw_in_g0", xb, dgl0, D, D, TS), _mm_tn("dw_in_g1", xb, dgl1, D, D, TS)
    dw_s = _mm_tn("dw_in_small", xb, dsmall, D, LANES, TS)
    whole = lambda a: (a, 0, a.shape[1])
    dw_sections = [whole(dw_q), whole(dw_k), whole(dw_v), (dw_s, 0, ATT_HEADS), whole(dw_z), whole(dw_xs), whole(dw_b), whole(dw_c),
                   (dw_s, ATT_HEADS, ATT_HEADS + SSM_HEADS), whole(dw_g0), whole(dw_g1)]
    dw_blocks = jnp.stack([jnp.concatenate([a[:, lo:hi] for a, lo, hi in segs], axis=1)
                           for segs in _col_segments(dw_sections, shard_w)])

    in_halves, in_started, in_token = send_grads("in", ["w_in"], [dw_blocks], dw_blocks)
    def dx_first(accs, rows, vecs, j):
        return [ALPHA * rows[0] + sum(accs[1:], accs[0])], []

    def dx_more(accs, rows, vecs, j):
        return [rows[0] + sum(accs[1:], accs[0])], []

    wk = lambda col, width=D: (w_re, 0, col // width, width)
    dx_part, = _mm("dx_a", S, D, TM2, D, [(dq, D, 0), (dk, D, 0), (dv, D, 0), (dz, D, 0), (dz, D, 1)],
                   [wk(0), wk(1024), wk(2048), wk(RE_Z), wk(RE_Z + 1024)], [(k, k) for k in range(5)], dx_first,
                   [(D, F32, 0)], nt=True, rows=[(dr1, 0)], after=[in_token])
    grad_x, = _mm("dx_b", S, D, TM2, D,
                  [(du_x, D, 0), (du_x, D, 1), (du_b, HB, 0), (du_c, HB, 0), (dgl0, D, 0), (dgl1, D, 0), (dsmall, LANES, 0)],
                  [wk(RE_XBC), wk(RE_XBC + 1024), wk(RE_XBC + 2048, HB), wk(RE_XBC + 2048 + HB, HB), wk(RE_GATE),
                   wk(RE_GATE + 1024), wk(RE_SMALL, LANES)],
                  [(k, k) for k in range(7)], dx_more, [(D, F32, 0)], nt=True, rows=[(dx_part, 0)])
    names = ["w_in"] + mid_names + ffn_names
    halves = in_halves + mid_halves + ffn_halves
    stacks = (_chip_copies_wait("scatter_in_wait", in_started, True, grad_x)
              + _chip_copies_wait("scatter_mid_wait", mid_started, True, grad_x)
              + _chip_copies_wait("scatter_ffn_wait", ffn_started, True, grad_x))
    chip1 = chip.astype(jnp.int32).reshape(1)
    reduced = [_sum4("sum_" + nm, st, hv, chip1, _row_tile(st.shape[1], st.shape[2], mult=16))
               for nm, st, hv in zip(names, stacks, halves)]
    other = _sibling_swap("swap_reduced", reduced)
    big_w = [w_in, w_proj_attn, w_proj_ssm, w_out, w_ffn_gate, w_ffn_up, w_ffn_down]
    big_m = [m_w_in, m_w_proj_attn, m_w_proj_ssm, m_w_out, m_w_ffn_gate, m_w_ffn_up, m_w_ffn_down]
    big_v = [v_w_in, v_w_proj_attn, v_w_proj_ssm, v_w_out, v_w_ffn_gate, v_w_ffn_up, v_w_ffn_down]
    big = {}
    lower, upper = jnp.where(core[0] == 0, reduced[0], other[0]), jnp.where(core[0] == 0, other[0], reduced[0])
    g_in_t = jnp.concatenate([lower.T, upper.T], axis=1)
    flat = lambda a: jnp.transpose(a, (2, 0, 1)).reshape(-1, LANES)
    unflat = lambda a: jnp.transpose(a.reshape(shard_w, 1, D), (1, 2, 0))
    flat_rows = shard_w * D // LANES
    big["w_in"] = [g_in_t.T[None]] + [unflat(r) for r in _adamw_flat("adamw_w_in", flat(w_in), flat(m_w_in), flat(v_w_in),
                                                                      g_in_t.reshape(-1, LANES), _row_tile(flat_rows, LANES, 3 << 20))]
    for nm, w_, m_, v_, mine, theirs in list(zip(names, big_w, big_m, big_v, reduced, other))[1:]:
        if nm in ("w_ffn_gate", "w_ffn_up"):
            lower, upper = jnp.where(core[0] == 0, mine, theirs), jnp.where(core[0] == 0, theirs, mine)
            across = lambda a: jnp.transpose(a, (0, 2, 1))
            res = _adamw("adamw_" + nm, across(w_)[0], across(m_)[0], across(v_)[0], jnp.concatenate([lower.T, upper.T], axis=1),
                         _row_tile(w_.shape[2], w_.shape[1]))
            big[nm] = [across(r[None]) for r in res]
            continue
        big[nm] = _adamw_halves("adamw_" + nm, w_, m_, v_, mine, theirs, core, _row_tile(w_.shape[1] // 2, w_.shape[2]))

    dd_skip = ddskip_b.reshape(1, SSM_HEADS, ATT_HEAD_DIM).sum(axis=2)
    pieces = [dbf[:, :ATT_HEADS], dconv_w.reshape(1, SSM_CONV * SSM_CONV_DIM), dconv_b, dbias8.reshape(1, SSM_HEADS), dalog8.reshape(1, SSM_HEADS), dd_skip,
              dnw, dbg0, dbg1, dln1_g, dln1_b, dln2_g, dln2_b]
    widths = [p.shape[1] for p in pieces]
    total = sum(widths)
    P = -(-total // LANES) * LANES
    packed = jnp.concatenate(pieces + [jnp.zeros((1, P - total), F32)], axis=1)
    summed = _all_sum_small(packed)
    offs = [0]
    for wd in widths:
        offs.append(offs[-1] + wd)
    sm = [summed[:, offs[k]:offs[k + 1]] for k in range(len(pieces))]
    g_bf, g_cw_full, g_cb, g_dtb, g_al, g_ds, g_nw = sm[0], sm[1].reshape(SSM_CONV, SSM_CONV_DIM), sm[2], sm[3], sm[4], sm[5], sm[6]
    g_bg = jnp.concatenate([sm[7], sm[8]], axis=1)
    g_l1g, g_l1b, g_l2g, g_l2b = sm[9], sm[10], sm[11], sm[12]
    cshard = SSM_CONV_DIM // 4
    g_cw_shard = lax.dynamic_slice_in_dim(g_cw_full, chip * cshard, cshard, axis=1)
    small_names = ["b_forget", "conv_w", "conv_b", "dt_bias", "a_log", "d_skip", "ssm_norm_w", "b_gates", "ln1_g", "ln1_b",
                   "ln2_g", "ln2_b"]
    small_g = [g_bf, g_cw_shard.reshape(1, -1), g_cb, g_dtb, g_al, g_ds, g_nw, g_bg, g_l1g, g_l1b, g_l2g, g_l2b]
    small_w = [b_forget, conv_w[0].reshape(1, -1), conv_b, dt_bias, a_log, d_skip, ssm_norm_w, b_gates, ln1_g, ln1_b, ln2_g, ln2_b]
    small_m = [m_b_forget, m_conv_w[0].reshape(1, -1), m_conv_b, m_dt_bias, m_a_log, m_d_skip, m_ssm_norm_w, m_b_gates, m_ln1_g,
               m_ln1_b, m_ln2_g, m_ln2_b]
    small_v = [v_b_forget, v_conv_w[0].reshape(1, -1), v_conv_b, v_dt_bias, v_a_log, v_d_skip, v_ssm_norm_w, v_b_gates, v_ln1_g,
               v_ln1_b, v_ln2_g, v_ln2_b]
    sw = [a.shape[1] for a in small_w]
    stot = sum(sw)
    SP = -(-stot // LANES) * LANES

    def pack(parts):
        return jnp.concatenate(list(parts) + [jnp.zeros((1, SP - stot), F32)], axis=1).reshape(SP // LANES, LANES)

    sres = _adamw("adamw_small", pack(small_w), pack(small_m), pack(small_v), pack(small_g), SP // LANES)
    soffs = [0]
    for wd in sw:
        soffs.append(soffs[-1] + wd)
    smalls = {}
    for k, nm in enumerate(small_names):
        vals = [r.reshape(1, SP)[:, soffs[k]:soffs[k + 1]] for r in sres]
        if nm == "conv_w":
            vals = [v_.reshape(1, SSM_CONV, cshard) for v_ in vals]
        smalls[nm] = vals

    order = ["w_in", "b_forget", "conv_w", "conv_b", "dt_bias", "a_log", "d_skip", "ssm_norm_w", "w_proj_attn", "w_proj_ssm",
             "b_gates", "w_out", "ln1_g", "ln1_b", "w_ffn_gate", "w_ffn_up", "w_ffn_down", "ln2_g", "ln2_b"]
    allres = {**big, **smalls}
    outs = [loss, grad_x[None]]
    for idx in range(4):
        outs += [allres[nm][idx] for nm in order]
    return tuple(outs)
```

```python
import functools
import math

import jax
import jax.numpy as jnp
from jax import lax
from jax.experimental import pallas as pl
from jax.experimental.pallas import tpu as pltpu

F32, BF16 = jnp.float32, jnp.bfloat16
MESH = pl.DeviceIdType.MESH

D_MODEL = 1024
ATT_HEADS, ATT_HEAD_DIM = 16, 64
SSM_INNER, SSM_HEADS, SSM_GROUPS, SSM_STATE, SSM_CONV = 2048, 32, 4, 128, 4
SSM_CONV_DIM = SSM_INNER + 2 * SSM_GROUPS * SSM_STATE
GROUP_LANES = SSM_INNER // SSM_GROUPS
FFN_HIDDEN = 2816
ALPHA = 2.0 ** 0.25
LN_EPS = 1e-5
RMS_EPS = 1e-5
ADAM_LR, ADAM_B1, ADAM_B2, ADAM_EPS, ADAM_WD, ADAM_STEP = 0.001, 0.9, 0.999, 1e-08, 0.01, 10
IN_SIZES = (1024, 1024, 1024, 16, 2048, 3072, 32, 2048)
IN_WIDTH = sum(IN_SIZES)
RE_WIDTH = 3072 + 2048 + 3072 + 2048 + 128
RE_Z, RE_XBC, RE_GATE, RE_SMALL = 3072, 5120, 8192, 10240

LANES = 128
VMEM_CAP = 60 * 1024 * 1024
NEG = -1e30
TILES = dict(TM=1024, TM2=256, TM3=512, TA=512, AQF=2048, LC=256, CV=512, TS=2048, TB=256)


def _params(n_axes, vmem_bytes=None):
    return pltpu.CompilerParams(dimension_semantics=("arbitrary",) * n_axes,
                                vmem_limit_bytes=None if vmem_bytes is None else int(min(vmem_bytes, VMEM_CAP)))


def _sigmoid(v):
    return 1.0 / (1.0 + jnp.exp(-v))


def _softplus(v):
    return jnp.maximum(v, 0.0) + jnp.log(1.0 + jnp.exp(-jnp.abs(v)))


def _dot(a, b):
    return lax.dot_general(a, b, (((1,), (0,)), ((), ())), preferred_element_type=F32)


def _dot_nt(a, b):
    return lax.dot_general(a, b, (((1,), (1,)), ((), ())), preferred_element_type=F32)


def _dot_tn(a, b):
    return lax.dot_general(a, b, (((0,), (0,)), ((), ())), preferred_element_type=F32)


def _split3(v):
    hi = v.astype(BF16)
    r1 = v - hi.astype(F32)
    mid = r1.astype(BF16)
    lo = (r1 - mid.astype(F32)).astype(BF16)
    return hi, mid, lo


def _dot_exact_left(m01, v):
    hi, mid, lo = _split3(v)
    return _dot(m01, hi) + _dot(m01, mid) + _dot(m01, lo)


def _dot_exact_right(v, m01, terms=3):
    parts = _split3(v)[:terms]
    out = _dot(parts[0], m01)
    for p in parts[1:]:
        out = out + _dot(p, m01)
    return out


def _mm(name, M, N, tm, tn, lhs, rhs, pairs, e_fn, outs, *, nt=False, rows=(), vecs_n=(), sums=(), after=()):
    ni, nj = M // tm, N // tn
    assert ni * tm == M and nj * tn == N, (name, M, N, tm, tn)
    n_l, n_r, n_row, n_vn, n_o, n_s = len(lhs), len(rhs), len(rows), len(vecs_n), len(outs), len(sums)

    def body(*refs):
        pos = 0
        l_refs = refs[pos:pos + n_l]; pos += n_l
        r_refs = refs[pos:pos + n_r]; pos += n_r
        row_refs = refs[pos:pos + n_row]; pos += n_row
        vn_refs = refs[pos:pos + n_vn]; pos += n_vn + len(after)
        o_refs = refs[pos:pos + n_o]; pos += n_o
        s_refs = refs[pos:pos + n_s]; pos += n_s
        i, j = pl.program_id(0), pl.program_id(1)
        accs = []
        for li, ri in pairs:
            accs.append(_dot_nt(l_refs[li][...], r_refs[ri][...]) if nt else _dot(l_refs[li][...], r_refs[ri][...]))
        out_vals, sum_vals = e_fn(accs, [r[...] for r in row_refs], [r[...] for r in vn_refs], j)
        for r, v in zip(o_refs, out_vals):
            r[...] = v.astype(r.dtype)
        if n_s:
            col = pl.multiple_of(j * tn, LANES)

            @pl.when(i == 0)
            def _():
                for r, v in zip(s_refs, sum_vals):
                    r[:, pl.ds(col, tn)] = v

            @pl.when(i > 0)
            def _():
                for r, v in zip(s_refs, sum_vals):
                    r[:, pl.ds(col, tn)] += v

    in_specs, args, est = [], [], 0
    for arr, width, cb in lhs:
        in_specs.append(pl.BlockSpec((tm, width), lambda i, j, cb=cb: (i, cb)))
        args.append(arr); est += tm * width * arr.dtype.itemsize
    for arr, off, *ksub in rhs:
        if nt:
            kb, kw = ksub if ksub else (0, arr.shape[1])
            in_specs.append(pl.BlockSpec((tn, kw), lambda i, j, off=off, kb=kb: (j + off, kb)))
            est += tn * kw * arr.dtype.itemsize
        else:
            in_specs.append(pl.BlockSpec((arr.shape[0], tn), lambda i, j, off=off: (0, j + off)))
            est += tn * arr.shape[0] * arr.dtype.itemsize
        args.append(arr)
    for arr, off in rows:
        in_specs.append(pl.BlockSpec((tm, tn), lambda i, j, off=off: (i, j + off)))
        args.append(arr); est += tm * tn * arr.dtype.itemsize
    for arr, off in vecs_n:
        in_specs.append(pl.BlockSpec((1, tn), lambda i, j, off=off: (0, j + off)))
        args.append(arr); est += 8 * tn * 4
    for arr in after:
        in_specs.append(pl.BlockSpec(memory_space=pl.ANY))
        args.append(arr)
    out_shape, out_specs = [], []
    for total, dtype, off in outs:
        out_shape.append(jax.ShapeDtypeStruct((M, total), dtype))
        out_specs.append(pl.BlockSpec((tm, tn), lambda i, j, off=off: (i, j + off)))
        est += tm * tn * jnp.dtype(dtype).itemsize
    for total in sums:
        out_shape.append(jax.ShapeDtypeStruct((1, total), F32))
        out_specs.append(pl.BlockSpec((1, total), lambda i, j: (0, 0)))
        est += 8 * total * 4
    vmem = 2 * est + (len(pairs) + 4) * tm * tn * 4 + (8 << 20)
    return pl.pallas_call(body, name=name, grid=(ni, nj), in_specs=in_specs, out_specs=out_specs, out_shape=out_shape,
                          compiler_params=_params(2, vmem))(*args)


def _mm_tn(name, a, g, ta, tn, ts, a_cols=None, a_off=0):
    S = a.shape[0]
    Ka = a.shape[1] if a_cols is None else a_cols
    N = g.shape[1]
    assert Ka % ta == 0 and N % tn == 0 and S % ts == 0, (name, Ka, N, S)
    aoff = a_off // ta

    def body(a_ref, g_ref, o_ref):
        s = pl.program_id(2)
        part = _dot_tn(a_ref[...], g_ref[...])

        @pl.when(s == 0)
        def _():
            o_ref[...] = part

        @pl.when(s > 0)
        def _():
            o_ref[...] += part

    vmem = 2 * (ts * ta * 2 + ts * tn * 2 + ta * tn * 4) + 2 * ta * tn * 4 + (8 << 20)
    return pl.pallas_call(
        body, name=name, grid=(Ka // ta, N // tn, S // ts),
        in_specs=[pl.BlockSpec((ts, ta), lambda ia, jn, s: (s, ia + aoff)), pl.BlockSpec((ts, tn), lambda ia, jn, s: (s, jn))],
        out_specs=pl.BlockSpec((ta, tn), lambda ia, jn, s: (ia, jn)),
        out_shape=jax.ShapeDtypeStruct((Ka, N), F32), compiler_params=_params(3, vmem))(a, g)


def _tri(n, upper):
    r = lax.broadcasted_iota(jnp.int32, (n, n), 0)
    c = lax.broadcasted_iota(jnp.int32, (n, n), 1)
    return jnp.where((c >= r) if upper else (c <= r), 1.0, 0.0).astype(BF16)


def _logsig(v):
    return jnp.minimum(v, 0.0) - jnp.log(1.0 + jnp.exp(-jnp.abs(v)))


def _cum_fwd(small, bvec, tb):
    S = small.shape[0]

    def body(x_ref, b_ref, o_ref, carry):
        i = pl.program_id(0)

        @pl.when(i == 0)
        def _():
            carry[...] = jnp.zeros_like(carry)

        logf = _logsig(x_ref[...] + b_ref[...])
        cum = _dot_exact_left(_tri(tb, False), logf) + carry[0:1, :]
        o_ref[...] = cum
        carry[0:1, :] = cum[tb - 1:tb, :]

    return pl.pallas_call(
        body, name="cum_fwd", grid=(S // tb,),
        in_specs=[pl.BlockSpec((tb, LANES), lambda i: (i, 0)), pl.BlockSpec((1, LANES), lambda i: (0, 0))],
        out_specs=pl.BlockSpec((tb, LANES), lambda i: (i, 0)), out_shape=jax.ShapeDtypeStruct((S, LANES), F32),
        scratch_shapes=[pltpu.VMEM((8, LANES), F32)], compiler_params=_params(1))(small, bvec)


def _cum_bwd(dcum_k, dcum_q, small, bvec, tb):
    S = small.shape[0]
    nb = S // tb

    def body(dk_ref, dq_ref, x_ref, b_ref, o_ref, s_ref, carry):
        i = pl.program_id(0)

        @pl.when(i == 0)
        def _():
            carry[...] = jnp.zeros_like(carry)
            s_ref[...] = jnp.zeros_like(s_ref)

        rc = _dot_exact_left(_tri(tb, True), dk_ref[...] + dq_ref[...]) + carry[0:1, :]
        dfl = rc * _sigmoid(-(x_ref[...] + b_ref[...]))
        o_ref[...] = dfl
        s_ref[...] += jnp.sum(dfl, axis=0, keepdims=True)
        carry[0:1, :] = rc[0:1, :]

    rev = lambda i: (nb - 1 - i, 0)
    return pl.pallas_call(
        body, name="cum_bwd", grid=(nb,),
        in_specs=[pl.BlockSpec((tb, LANES), rev)] * 3 + [pl.BlockSpec((1, LANES), lambda i: (0, 0))],
        out_specs=[pl.BlockSpec((tb, LANES), rev), pl.BlockSpec((1, LANES), lambda i: (0, 0))],
        out_shape=[jax.ShapeDtypeStruct((S, LANES), F32), jax.ShapeDtypeStruct((1, LANES), F32)],
        scratch_shapes=[pltpu.VMEM((8, LANES), F32)], compiler_params=_params(1))(dcum_k, dcum_q, small, bvec)


N_AUG = 3


def _lane():
    return lax.broadcasted_iota(jnp.int32, (1, LANES), 1)


def _lane_mask():
    return _lane() < ATT_HEAD_DIM


def _aug_base(h):
    return ATT_HEAD_DIM * (1 - h)


def _attn_prep(qkv, cum_cols, T):
    S = qkv.shape[0]
    HP = ATT_HEADS // 2

    def body(q_ref, k_ref, c_ref, qa_ref, ka_ref):
        lane = _lane()
        q = q_ref[...]
        k = k_ref[...]
        one, zero = jnp.ones_like(q), jnp.zeros_like(q)
        for h in (0, 1):
            base = _aug_base(h)
            own = (lane < ATT_HEAD_DIM) if h == 0 else (lane >= ATT_HEAD_DIM)
            term_lanes = (lane >= base) & (lane < base + N_AUG)
            terms = [t.astype(F32) for t in _split3(c_ref[0, :, h:h + 1])]
            neg = jnp.where(lane == base, -terms[0], jnp.where(lane == base + 1, -terms[1], -terms[2])).astype(BF16)
            qa_ref[:, h * LANES:(h + 1) * LANES] = jnp.where(lane == base + N_AUG, zero, jnp.where(term_lanes, one, q))
            ka_ref[:, h * LANES:(h + 1) * LANES] = jnp.where(term_lanes, neg, jnp.where(lane == base + N_AUG, one,
                                                                                         jnp.where(own, k, zero)))

    return pl.pallas_call(
        body, name="attn_prep", grid=(S // T, HP),
        in_specs=[pl.BlockSpec((T, LANES), lambda i, hp: (i, hp)), pl.BlockSpec((T, LANES), lambda i, hp: (i, HP + hp)),
                  pl.BlockSpec((1, T, 2), lambda i, hp: (hp, i, 0))],
        out_specs=[pl.BlockSpec((T, 2 * LANES), lambda i, hp: (i, hp))] * 2,
        out_shape=[jax.ShapeDtypeStruct((S, 2 * D_MODEL), BF16)] * 2, compiler_params=_params(2))(qkv, qkv, cum_cols)


def _attn_fwd(qa, ka, qkv, T, TK):
    S = qkv.shape[0]
    nq = S // T
    r = T // TK
    HP = ATT_HEADS // 2

    def body(q0_ref, q1_ref, k0_ref, k1_ref, v_ref, o_ref, o32_ref, lse_ref):
        i = pl.program_id(1)
        qs = (q0_ref[...], q1_ref[...])
        k_refs = (k0_ref, k1_ref)
        row = lax.broadcasted_iota(jnp.int32, (TK, T), 0)
        col = lax.broadcasted_iota(jnp.int32, (TK, T), 1)
        head_rows = lax.broadcasted_iota(jnp.int32, (LANES, 1), 0) < ATT_HEAD_DIM

        def block(j, carry, q0):
            off = pl.multiple_of(j * TK, TK)
            vj = v_ref[pl.ds(off, TK), :]
            full = q0 is None
            q0 = 0 if full else q0
            m0, l0, m1, l1, acc = carry
            new, alphas, pvs = [], [], []
            for h, (m, l) in enumerate(((m0, l0), (m1, l1))):
                st = _dot_nt(k_refs[h][pl.ds(off, TK), :], qs[h][q0:, :])
                if not full:
                    st = jnp.where(row[:, :T - q0] <= col[:, :T - q0], st, NEG)
                m_old, l_old = m[:, q0:], l[:, q0:]
                m_new = jnp.maximum(m_old, jnp.max(st, axis=0, keepdims=True))
                p = jnp.exp(st - m_new)
                alpha = jnp.exp(m_old - m_new)
                l_new = alpha * l_old + jnp.sum(p, axis=0, keepdims=True)
                pvs.append(_dot_tn(vj, p.astype(BF16)))
                alphas.append(alpha)
                new += [m_new, l_new]
            part = acc[:, q0:] * jnp.where(head_rows, alphas[0], alphas[1]) + jnp.where(head_rows, pvs[0], pvs[1])
            if q0:
                keep = lambda old, upd: jnp.concatenate([old[:, :q0], upd], axis=1)
                return (keep(m0, new[0]), keep(l0, new[1]), keep(m1, new[2]), keep(l1, new[3]), keep(acc, part))
            return (new[0], new[1], new[2], new[3], part)

        init = (jnp.full((1, T), NEG, F32), jnp.zeros((1, T), F32), jnp.full((1, T), NEG, F32), jnp.zeros((1, T), F32),
                jnp.zeros((LANES, T), F32))
        n_full = i * r
        carry = lax.fori_loop(0, n_full // 2, lambda jj, c: block(2 * jj + 1, block(2 * jj, c, None), None), init)
        carry = lax.cond(n_full % 2 == 1, lambda c: block(n_full - 1, c, None), lambda c: c, carry)
        for d in range(r):
            carry = block(n_full + d, carry, d * TK)
        m0, l0, m1, l1, acc = carry
        out = (acc / jnp.where(head_rows, l0, l1)).T
        o_ref[...] = out.astype(BF16)
        o32_ref[...] = out
        lse_ref[0, 0:1, :] = m0 + jnp.log(l0)
        lse_ref[0, 1:2, :] = m1 + jnp.log(l1)

    vmem = 2 * (2 * T * LANES * 2 + 3 * S * LANES * 2 + T * LANES * (2 + 4) + 8 * T * 4) + 10 * T * TK * 4 + (8 << 20)
    qspec = lambda h: pl.BlockSpec((T, LANES), lambda hp, i, h=h: (i, 2 * hp + h))
    kspec = lambda h: pl.BlockSpec((S, LANES), lambda hp, i, h=h: (0, 2 * hp + h))
    return pl.pallas_call(
        body, name="attn_fwd", grid=(HP, nq),
        in_specs=[qspec(0), qspec(1), kspec(0), kspec(1), pl.BlockSpec((S, LANES), lambda hp, i: (0, 2 * HP + hp))],
        out_specs=[pl.BlockSpec((T, LANES), lambda hp, i: (i, hp)), pl.BlockSpec((T, LANES), lambda hp, i: (i, hp)),
                   pl.BlockSpec((1, 2, T), lambda hp, i: (hp, 0, i))],
        out_shape=[jax.ShapeDtypeStruct((S, D_MODEL), BF16), jax.ShapeDtypeStruct((S, D_MODEL), F32),
                   jax.ShapeDtypeStruct((HP, 2, S), F32)],
        compiler_params=_params(2, vmem))(qa, qa, ka, ka, qkv)


def _attn_stats(do, o32, lse_cols, T):
    S = do.shape[0]
    HP = ATT_HEADS // 2

    def body(do_ref, o_ref, lse_ref, st_ref):
        lane = lax.broadcasted_iota(jnp.int32, (LANES, 8), 0)
        c = lax.broadcasted_iota(jnp.int32, (LANES, 8), 1)
        sel = jnp.where(((c == 2) & (lane < ATT_HEAD_DIM)) | ((c == 3) & (lane >= ATT_HEAD_DIM)), 1.0, 0.0).astype(BF16)
        dd = _dot_exact_right(do_ref[...].astype(F32) * o_ref[...], sel)
        c8 = lax.broadcasted_iota(jnp.int32, (1, 8), 1)
        st_ref[0] = jnp.where(c8 == 0, lse_ref[0, :, 0:1], jnp.where(c8 == 1, lse_ref[0, :, 1:2], dd))

    return pl.pallas_call(
        body, name="attn_stats", grid=(HP, S // T),
        in_specs=[pl.BlockSpec((T, LANES), lambda hp, i: (i, hp)), pl.BlockSpec((T, LANES), lambda hp, i: (i, hp)),
                  pl.BlockSpec((1, T, 2), lambda hp, i: (hp, i, 0))],
        out_specs=pl.BlockSpec((1, T, 8), lambda hp, i: (hp, i, 0)), out_shape=jax.ShapeDtypeStruct((HP, S, 8), F32),
        compiler_params=_params(2))(do, o32, lse_cols)


def _attn_bwd(qa, ka, qkv, do, stats, T):
    S = qkv.shape[0]
    nq = S // T
    HP = ATT_HEADS // 2

    def body(k0_ref, k1_ref, v_ref, q0_ref, q1_ref, do_ref, st_ref, dq_ref, dk_ref, dv_ref, dck_ref, dcq_ref, dq_acc):
        j = pl.program_id(1)
        mA = _lane_mask()
        masks = (mA, jnp.logical_not(mA))
        q_refs = (q0_ref, q1_ref)

        @pl.when(j == 0)
        def _():
            dq_acc[...] = jnp.zeros_like(dq_acc)

        kas = (k0_ref[...], k1_ref[...])
        vj = v_ref[...]
        row = lax.broadcasted_iota(jnp.int32, (T, T), 0)
        col = lax.broadcasted_iota(jnp.int32, (T, T), 1)

        def block(i, carry, diag):
            dvt, dkt0, dkt1 = carry
            off = pl.multiple_of(i * T, T)
            doi = do_ref[pl.ds(off, T), :]
            zero = jnp.zeros_like(doi)
            dkts = [dkt0, dkt1]
            for h in (0, 1):
                qh = q_refs[h][pl.ds(off, T), :]
                doh = jnp.where(masks[h], doi, zero)
                lse = st_ref[0, pl.ds(off, T), h:h + 1]
                dd = st_ref[0, pl.ds(off, T), 2 + h:3 + h]
                sc = _dot_nt(qh, kas[h])
                if diag:
                    sc = jnp.where(row >= col, sc, NEG)
                p = jnp.exp(sc - lse)
                dp = _dot_nt(doh, vj)
                ds = (p * (dp - dd)).astype(BF16)
                dvt = dvt + _dot_tn(doh, p.astype(BF16))
                dkts[h] = dkts[h] + _dot_tn(qh, ds)
                dq_acc[h, pl.ds(off, T), :] += _dot(ds, kas[h])
            return (dvt, dkts[0], dkts[1])

        z = jnp.zeros((LANES, T), F32)
        carry = block(j, (z, z, z), True)
        dvt, dkt0, dkt1 = lax.fori_loop(j + 1, nq, lambda i, c: block(i, c, False), carry)
        dv_ref[...] = dvt.T.astype(BF16)
        dk_ref[...] = jnp.where(mA, dkt0.T, dkt1.T).astype(BF16)
        ones_q = (_aug_base(0), _aug_base(1))
        dck_ref[0, 0:1, :] = -dkt0[ones_q[0]:ones_q[0] + 1, :]
        dck_ref[0, 1:2, :] = -dkt1[ones_q[1]:ones_q[1] + 1, :]

        @pl.when(j == nq - 1)
        def _():
            dq0, dq1 = dq_acc[0], dq_acc[1]
            ones_k = (_aug_base(0) + N_AUG, _aug_base(1) + N_AUG)
            dq_ref[...] = (jnp.where(mA, dq0, dq1) * (1.0 / math.sqrt(ATT_HEAD_DIM))).astype(BF16)
            dcq_ref[0, :, 0:1] = dq0[:, ones_k[0]:ones_k[0] + 1]
            dcq_ref[0, :, 1:2] = dq1[:, ones_k[1]:ones_k[1] + 1]

    vmem = (2 * (3 * T * LANES * 2 + 3 * S * LANES * 2 + S * LANES * 4 + S * LANES * (2 + 4) + 2 * T * LANES * 2 + 8 * T * 4)
            + 2 * S * LANES * 4 + 12 * T * T * 4 + (8 << 20))
    kspec = lambda h: pl.BlockSpec((T, LANES), lambda hp, j, h=h: (j, 2 * hp + h))
    qspec = lambda h: pl.BlockSpec((S, LANES), lambda hp, j, h=h: (0, 2 * hp + h))
    blk = pl.BlockSpec((T, LANES), lambda hp, j: (j, hp))
    full = pl.BlockSpec((S, LANES), lambda hp, j: (0, hp))
    return pl.pallas_call(
        body, name="attn_bwd", grid=(HP, nq),
        in_specs=[kspec(0), kspec(1), pl.BlockSpec((T, LANES), lambda hp, j: (j, 2 * HP + hp)), qspec(0), qspec(1), full,
                  pl.BlockSpec((1, S, 8), lambda hp, j: (hp, 0, 0))],
        out_specs=[full, blk, blk, pl.BlockSpec((1, 2, T), lambda hp, j: (hp, 0, j)),
                   pl.BlockSpec((1, S, 2), lambda hp, j: (hp, 0, 0))],
        out_shape=[jax.ShapeDtypeStruct((S, D_MODEL), BF16)] * 3 + [jax.ShapeDtypeStruct((HP, 2, S), F32),
                                                                     jax.ShapeDtypeStruct((HP, S, 2), F32)],
        scratch_shapes=[pltpu.VMEM((2, S, LANES), F32)], compiler_params=_params(2, vmem))(ka, ka, qkv, qa, qa, do, stats)


HALO = 8


def _shift_down(x, d, above):
    r = pltpu.roll(x, d, 0)
    head = jnp.where(lax.broadcasted_iota(jnp.int32, (HALO, 1), 0) < d, pltpu.roll(above, d, 0), r[0:HALO])
    return head if x.shape[0] == HALO else jnp.concatenate([head, r[HALO:]], axis=0)


def _shift_up(x, d, below):
    n = x.shape[0]
    r = pltpu.roll(x, n - d, 0)
    tail = jnp.where(lax.broadcasted_iota(jnp.int32, (HALO, 1), 0) >= HALO - d, pltpu.roll(below, HALO - d, 0), r[n - HALO:])
    return jnp.concatenate([r[:n - HALO], tail], axis=0)


def _conv_fwd(u, w, b, ts, tc):
    S, C = u.shape
    hb = ts // HALO

    def body(u_ref, prev_ref, w_ref, b_ref, o_ref):
        i = pl.program_id(0)
        x = u_ref[...]
        above = jnp.where(i == 0, 0.0, prev_ref[...])
        acc = b_ref[...] + w_ref[3:4, :] * x
        for k in range(SSM_CONV - 1):
            acc = acc + w_ref[k:k + 1, :] * _shift_down(x, SSM_CONV - 1 - k, above)
        o_ref[...] = acc * _sigmoid(acc)

    return pl.pallas_call(
        body, name="conv_fwd", grid=(S // ts, C // tc),
        in_specs=[pl.BlockSpec((ts, tc), lambda i, j: (i, j)),
                  pl.BlockSpec((HALO, tc), lambda i, j: (jnp.maximum(i * hb - 1, 0), j)),
                  pl.BlockSpec((SSM_CONV, tc), lambda i, j: (0, j)), pl.BlockSpec((1, tc), lambda i, j: (0, j))],
        out_specs=pl.BlockSpec((ts, tc), lambda i, j: (i, j)), out_shape=jax.ShapeDtypeStruct((S, C), F32),
        compiler_params=_params(2))(u, u, w, b)


def _conv_bwd(name, u, dy, w, b, ts, tc, col0):
    S, C = dy.shape
    cb = col0 // tc
    assert cb * tc == col0
    hb = ts // HALO
    nb = S // ts

    def body(u_ref, uprev_ref, unext_ref, dy_ref, dynext_ref, w_ref, b_ref, du_ref, dw_ref, db_ref):
        i = pl.program_id(1)
        x = u_ref[...]
        above = jnp.where(i == 0, 0.0, uprev_ref[...])
        ws = [w_ref[k:k + 1, :] for k in range(SSM_CONV)]

        def dsilu(pre):
            sg = _sigmoid(pre)
            return sg * (1.0 + pre * (1.0 - sg))

        shifted = [_shift_down(x, SSM_CONV - 1 - k, above) for k in range(SSM_CONV - 1)] + [x]
        pre = b_ref[...]
        for k in range(SSM_CONV):
            pre = pre + ws[k] * shifted[k]
        g = dy_ref[...] * dsilu(pre)
        nxt = unext_ref[...]
        tail = x[ts - HALO:, :]
        pre_n = b_ref[...] + ws[SSM_CONV - 1] * nxt
        for k in range(SSM_CONV - 1):
            pre_n = pre_n + ws[k] * _shift_down(nxt, SSM_CONV - 1 - k, tail)
        g_next = jnp.where(i == nb - 1, 0.0, dynext_ref[...] * dsilu(pre_n))
        du = ws[SSM_CONV - 1] * g
        for k in range(SSM_CONV - 1):
            du = du + ws[k] * _shift_up(g, SSM_CONV - 1 - k, g_next)
        du_ref[...] = du.astype(du_ref.dtype)
        dws = [jnp.sum(g * shifted[k], axis=0, keepdims=True) for k in range(SSM_CONV)]
        dbs = jnp.sum(g, axis=0, keepdims=True)

        @pl.when(i == 0)
        def _():
            for k in range(SSM_CONV):
                dw_ref[k:k + 1, :] = dws[k]
            db_ref[...] = dbs

        @pl.when(i > 0)
        def _():
            for k in range(SSM_CONV):
                dw_ref[k:k + 1, :] += dws[k]
            db_ref[...] += dbs

    nxt = lambda off: (lambda j, i: (jnp.minimum((i + 1) * hb, S // HALO - 1), j + off))
    return pl.pallas_call(
        body, name=name, grid=(C // tc, nb),
        in_specs=[pl.BlockSpec((ts, tc), lambda j, i: (i, j + cb)),
                  pl.BlockSpec((HALO, tc), lambda j, i: (jnp.maximum(i * hb - 1, 0), j + cb)),
                  pl.BlockSpec((HALO, tc), nxt(cb)),
                  pl.BlockSpec((ts, tc), lambda j, i: (i, j)),
                  pl.BlockSpec((HALO, tc), nxt(0)),
                  pl.BlockSpec((SSM_CONV, tc), lambda j, i: (0, j + cb)), pl.BlockSpec((1, tc), lambda j, i: (0, j + cb))],
        out_specs=[pl.BlockSpec((ts, tc), lambda j, i: (i, j)), pl.BlockSpec((SSM_CONV, tc), lambda j, i: (0, j)),
                   pl.BlockSpec((1, tc), lambda j, i: (0, j))],
        out_shape=[jax.ShapeDtypeStruct((S, C), BF16), jax.ShapeDtypeStruct((SSM_CONV, C), F32), jax.ShapeDtypeStruct((1, C), F32)],
        compiler_params=_params(2))(u, u, u, dy, dy, w, b)


def _head_sum():
    lane = jnp.right_shift(lax.broadcasted_iota(jnp.int32, (GROUP_LANES, 8), 0), 6)
    r = lax.broadcasted_iota(jnp.int32, (GROUP_LANES, 8), 1)
    return jnp.where(lane == r, 1.0, 0.0).astype(BF16)


def _head_expand():
    r = lax.broadcasted_iota(jnp.int32, (8, GROUP_LANES), 0)
    c = jnp.right_shift(lax.broadcasted_iota(jnp.int32, (8, GROUP_LANES), 1), 6)
    return jnp.where(r == c, 1.0, 0.0).astype(BF16)


def _ssd_common(dtc_ref, dtr_ref, bias_r, alog_b, bias_c, alog_c, L):
    a_b = -jnp.exp(alog_b)
    dt = _dot_exact_right(_softplus(dtc_ref[0] + bias_r), _head_expand())
    acum = _dot_exact_left(_tri(L, False), dt * a_b)
    a_c = -jnp.exp(alog_c)
    dtr = _softplus(dtr_ref[0] + bias_c)
    acum_r = _dot_exact_right(dtr * a_c, _tri(L, True))
    return a_b, dt, acum, acum_r


def _ssd_specs(L, nc, rev):
    cc = (lambda c: nc - 1 - c) if rev else (lambda c: c)
    G = SSM_GROUPS
    blk = pl.BlockSpec((L, GROUP_LANES), lambda g, c: (cc(c), g))
    dtc = pl.BlockSpec((1, L, 8), lambda g, c: (g, cc(c), 0))
    rowv = pl.BlockSpec((1, 1, 8), lambda g, c: (g, 0, 0))
    xs = blk
    bm = pl.BlockSpec((L, SSM_STATE), lambda g, c: (cc(c), SSM_INNER // SSM_STATE + g))
    cm = pl.BlockSpec((L, SSM_STATE), lambda g, c: (cc(c), SSM_INNER // SSM_STATE + G + g))
    dtr = pl.BlockSpec((1, 8, L), lambda g, c: (g, 0, cc(c)))
    vec = pl.BlockSpec((1, GROUP_LANES), lambda g, c: (0, g))
    colv = pl.BlockSpec((1, 8, 1), lambda g, c: (g, 0, 0))
    hs = pl.BlockSpec((1, 1, SSM_STATE, GROUP_LANES), lambda g, c: (g, cc(c), 0, 0))
    return blk, xs, bm, cm, dtc, dtr, vec, rowv, colv, hs


def _ssd_fwd(xbc, z, dtc, dtr, bias_r, alog_b, dskip_b, normw, bias_c, alog_c, L):
    S = z.shape[0]
    nc = S // L
    blk, xs, bm, cm, dtcs, dtrs, vec, rowv, colv, hs = _ssd_specs(L, nc, False)

    def body(x_ref, b_ref, c_ref, z_ref, dtc_ref, dtr_ref, bias_ref, alog_ref, dskip_ref, nw_ref, biasc_ref, alogc_ref,
             y_ref, ssm_ref, hs_ref, h_scr):
        c = pl.program_id(1)

        @pl.when(c == 0)
        def _():
            h_scr[...] = jnp.zeros_like(h_scr)

        mA = _lane_mask()
        a_b, dt, acum, acum_r = _ssd_common(dtc_ref, dtr_ref, bias_ref[0], alog_ref[...], biasc_ref[0], alogc_ref[0], L)
        x = x_ref[...]
        cb, bb = c_ref[...].astype(BF16), b_ref[...].astype(BF16)
        hprev = h_scr[...]
        hs_ref[0, 0] = hprev
        xdt = x * dt
        xdt_b = xdt.astype(BF16)
        gmat = _dot_nt(cb, bb)
        row = lax.broadcasted_iota(jnp.int32, (L, L), 0)
        col = lax.broadcasted_iota(jnp.int32, (L, L), 1)
        parts = []
        for p in range(GROUP_LANES // LANES):
            xp = xdt_b[:, p * LANES:(p + 1) * LANES]
            yd = []
            for hh in (0, 1):
                r = 2 * p + hh
                acol = acum[:, r * ATT_HEAD_DIM:r * ATT_HEAD_DIM + 1]
                arow = acum_r[r:r + 1, :]
                lm = jnp.exp(jnp.where(row >= col, acol - arow, NEG))
                yd.append(_dot((gmat * lm).astype(BF16), xp))
            parts.append(jnp.where(mA, yd[0], yd[1]))
        ydiag = jnp.concatenate(parts, axis=1)
        yoff = jnp.exp(acum) * _dot(cb, hprev.astype(BF16))
        y = ydiag + yoff + dskip_ref[...] * x
        aend = acum[L - 1:L, :]
        wgt = (jnp.exp(aend - acum) * xdt).astype(BF16)
        h_scr[...] = jnp.exp(aend) * hprev + _dot_tn(bb, wgt)
        y_ref[...] = y
        zz = z_ref[...].astype(F32)
        u = y * (zz * _sigmoid(zz))
        rs = lax.rsqrt(jnp.mean(u * u, axis=1, keepdims=True) + RMS_EPS)
        ssm_ref[...] = (u * rs * nw_ref[...]).astype(BF16)

    return pl.pallas_call(
        body, name="ssd_fwd", grid=(SSM_GROUPS, nc),
        in_specs=[xs, bm, cm, blk, dtcs, dtrs, rowv, vec, vec, vec, colv, colv],
        out_specs=[blk, blk, hs],
        out_shape=[jax.ShapeDtypeStruct((S, SSM_INNER), F32), jax.ShapeDtypeStruct((S, SSM_INNER), BF16),
                   jax.ShapeDtypeStruct((SSM_GROUPS, nc, SSM_STATE, GROUP_LANES), F32)],
        scratch_shapes=[pltpu.VMEM((SSM_STATE, GROUP_LANES), F32)],
        compiler_params=_params(2, 48 << 20))(xbc, xbc, xbc, z, dtc, dtr, bias_r, alog_b, dskip_b, normw, bias_c, alog_c)


def _ssd_bwd(xbc, z, y, dssm, hs_all, dtc, dtr, bias_r, alog_r, alog_b, dskip_b, normw, bias_c, alog_c, L):
    S = z.shape[0]
    nc = S // L
    blk, xs, bm, cm, dtcs, dtrs, vec, rowv, colv, hs = _ssd_specs(L, nc, True)

    def body(x_ref, b_ref, c_ref, z_ref, y_ref, dssm_ref, hs_ref, dtc_ref, dtr_ref, bias_ref, alogr_ref, alog_ref, dskip_ref, nw_ref,
             biasc_ref, alogc_ref,
             dx_ref, db_ref, dc_ref, dz_ref, ddt_ref, dnw_ref, ddskip_ref, dalog_ref, dbias_ref, dh_scr):
        c = pl.program_id(1)

        @pl.when(c == 0)
        def _():
            dh_scr[...] = jnp.zeros_like(dh_scr)

        mA = _lane_mask()
        masks = (mA, jnp.logical_not(mA))
        a_b, dt, acum, acum_r = _ssd_common(dtc_ref, dtr_ref, bias_ref[0], alog_ref[...], biasc_ref[0], alogc_ref[0], L)
        x, zz, y, dssm = x_ref[...], z_ref[...].astype(F32), y_ref[...], dssm_ref[...]
        cb, bb = c_ref[...].astype(BF16), b_ref[...].astype(BF16)
        hprev = hs_ref[0, 0]
        hb = hprev.astype(BF16)
        ds = dh_scr[...]
        dsb = ds.astype(BF16)
        dskip = dskip_ref[...]
        aend = acum[L - 1:L, :]
        e_a, e_end = jnp.exp(acum), jnp.exp(aend)
        dte = jnp.exp(aend - acum)
        xdt = x * dt
        xdt_b = xdt.astype(BF16)
        sg = _sigmoid(zz)
        sz = zz * sg
        u = y * sz
        rs = lax.rsqrt(jnp.mean(u * u, axis=1, keepdims=True) + RMS_EPS)
        un = u * rs
        dun = dssm * nw_ref[...]
        du = rs * (dun - un * jnp.mean(dun * un, axis=1, keepdims=True))
        dy = du * sz
        dz_ref[...] = (du * y * sg * (1.0 + zz * (1.0 - sg))).astype(dz_ref.dtype)
        dy_b = dy.astype(BF16)
        dch_b = (dy * e_a).astype(BF16)
        dc = _dot_nt(dch_b, hb)
        dhprev = _dot_tn(cb, dch_b)
        gt = _dot_nt(bb, cb)
        row = lax.broadcasted_iota(jnp.int32, (L, L), 0)
        col = lax.broadcasted_iota(jnp.int32, (L, L), 1)
        dgt = jnp.zeros((L, L), F32)
        parts = []
        for p in range(GROUP_LANES // LANES):
            xp = xdt_b[:, p * LANES:(p + 1) * LANES]
            dyp = dy_b[:, p * LANES:(p + 1) * LANES]
            zero = jnp.zeros_like(dyp)
            acc = None
            for hh in (0, 1):
                r = 2 * p + hh
                acol = acum[:, r * ATT_HEAD_DIM:r * ATT_HEAD_DIM + 1]
                arow = acum_r[r:r + 1, :]
                lmt = jnp.exp(jnp.where(row <= col, arow - acol, NEG))
                dyh = jnp.where(masks[hh], dyp, zero)
                part = _dot((gt * lmt).astype(BF16), dyh)
                acc = part if acc is None else acc + part
                dgt = dgt + _dot_nt(xp, dyh) * lmt
            parts.append(acc)
        dxdt_diag = jnp.concatenate(parts, axis=1)
        dgt_b = dgt.astype(BF16)
        db = _dot(dgt_b, cb)
        dc = dc + _dot_tn(dgt_b, bb)
        dxdt_state = dte * _dot(bb, dsb)
        db = db + _dot_nt((dte * xdt).astype(BF16), dsb)
        dxdt = dxdt_diag + dxdt_state
        dy_r, xdt_r = dy_b.astype(F32), xdt_b.astype(F32)
        dac = dy_r * (y - dskip * x) - xdt_r * dxdt
        tail = jnp.sum(xdt_r * dxdt_state, axis=0, keepdims=True) + e_end * jnp.sum(ds * hprev, axis=0, keepdims=True)
        rowl = lax.broadcasted_iota(jnp.int32, (L, 1), 0)
        dac = dac + jnp.where(rowl == L - 1, tail, 0.0)
        rc = _dot_exact_left(_tri(L, True), dac)
        hsum = _head_sum()
        hs1 = _dot_exact_right(dxdt * x, hsum, 2)
        hs2 = _dot_exact_right(rc, hsum, 2)
        a8 = -jnp.exp(alogr_ref[0])
        dtraw8 = dtc_ref[0] + bias_ref[0]
        ddtraw = (hs1 + a8 * hs2) * _sigmoid(dtraw8)
        dx_ref[...] = dskip * dy + dxdt * dt
        db_ref[...] = db
        dc_ref[...] = dc
        ddt_ref[0] = ddtraw
        dh_scr[...] = e_end * ds + dhprev
        sums = (jnp.sum(dssm * un, axis=0, keepdims=True), jnp.sum(dy * x, axis=0, keepdims=True))
        refs = (dnw_ref, ddskip_ref)
        sums8 = (a8 * jnp.sum(hs2 * _softplus(dtraw8), axis=0, keepdims=True), jnp.sum(ddtraw, axis=0, keepdims=True))
        refs8 = (dalog_ref, dbias_ref)

        @pl.when(c == 0)
        def _():
            for r, v in zip(refs, sums):
                r[...] = v
            for r, v in zip(refs8, sums8):
                r[0] = v

        @pl.when(c > 0)
        def _():
            for r, v in zip(refs, sums):
                r[...] += v
            for r, v in zip(refs8, sums8):
                r[0] += v

    nbc = pl.BlockSpec((L, SSM_STATE), lambda g, c: (nc - 1 - c, g))
    return pl.pallas_call(
        body, name="ssd_bwd", grid=(SSM_GROUPS, nc),
        in_specs=[xs, bm, cm, blk, blk, blk, hs, dtcs, dtrs, rowv, rowv, vec, vec, vec, colv, colv],
        out_specs=[blk, nbc, nbc, blk, dtcs, vec, vec, rowv, rowv],
        out_shape=[jax.ShapeDtypeStruct((S, SSM_INNER), F32), jax.ShapeDtypeStruct((S, SSM_GROUPS * SSM_STATE), F32),
                   jax.ShapeDtypeStruct((S, SSM_GROUPS * SSM_STATE), F32), jax.ShapeDtypeStruct((S, SSM_INNER), BF16),
                   jax.ShapeDtypeStruct((SSM_GROUPS, S, 8), F32)] + [jax.ShapeDtypeStruct((1, SSM_INNER), F32)] * 2
                  + [jax.ShapeDtypeStruct((SSM_GROUPS, 1, 8), F32)] * 2,
        scratch_shapes=[pltpu.VMEM((SSM_STATE, GROUP_LANES), F32)],
        compiler_params=_params(2, 56 << 20))(xbc, xbc, xbc, z, y, dssm, hs_all, dtc, dtr, bias_r, alog_r, alog_b, dskip_b,
                                              normw, bias_c, alog_c)


def _place():
    return lax.axis_index("x"), lax.axis_index("y"), lax.axis_index("c")


def _other_chips(x, y):
    return [(1 - x, y), (x, 1 - y), (1 - x, 1 - y)]


def _half_rows(rows, which):
    hr = rows // 2
    if isinstance(which, int):
        return pl.ds(which * hr, hr)
    return pl.ds(pl.multiple_of(which * hr, 8), hr)


def _chip_gather(name, shards, split):
    n = len(shards)
    ANY = pl.BlockSpec(memory_space=pl.ANY)

    def body(*refs):
        ins, outs = refs[:n], refs[n:2 * n]
        send, recv, fsend, frecv = refs[2 * n:]
        x, y, c = _place()
        me = 2 * x + y
        sibling = (x, y, 1 - c)
        chips = _other_chips(x, y)

        def piece(a, chip_idx, which):
            if split[a]:
                return outs[a].at[chip_idx, _half_rows(shards[a].shape[0], which)]
            return outs[a].at[chip_idx]

        def ici(k, a, to_chip, src_chip):
            src = ins[a].at[_half_rows(shards[a].shape[0], c)] if split[a] else ins[a]
            return pltpu.make_async_remote_copy(src_ref=src, dst_ref=piece(a, src_chip, c), send_sem=send.at[k, a],
                                                recv_sem=recv.at[k, a], device_id=(*to_chip, c), device_id_type=MESH)

        def fwd(k, a, src_chip, which):
            return pltpu.make_async_remote_copy(src_ref=piece(a, src_chip, which), dst_ref=piece(a, src_chip, which),
                                                send_sem=fsend.at[k, a], recv_sem=frecv.at[k, a], device_id=sibling,
                                                device_id_type=MESH)

        sends = []
        for k, chip in enumerate(chips):
            for a in range(n):
                cp = ici(k, a, chip, me)
                cp.start()
                sends.append(cp)
        for k, (ox, oy) in enumerate(chips):
            src = 2 * ox + oy
            for a in range(n):
                ici(k, a, (ox, oy), src).wait_recv()
                if split[a]:
                    cp = fwd(k, a, src, c)
                    cp.start()
                    sends.append(cp)
        for k, (ox, oy) in enumerate(chips):
            for a in range(n):
                if split[a]:
                    fwd(k, a, 2 * ox + oy, 1 - c).wait_recv()
        for cp in sends:
            cp.wait_send()

    sem = pltpu.SemaphoreType.DMA((3, n))
    return pl.pallas_call(
        body, name=name, in_specs=[ANY] * n, out_specs=[ANY] * n,
        out_shape=[jax.ShapeDtypeStruct((4,) + s.shape, s.dtype) for s in shards],
        scratch_shapes=[sem, sem, sem, sem])(*shards)


def _chip_copies_start(name, srcs, per_chip_src, after):
    n = len(srcs)
    HBM = pl.BlockSpec(memory_space=pltpu.HBM)
    SEM = pl.BlockSpec(memory_space=pltpu.SEMAPHORE)
    lands = [pltpu.with_memory_space_constraint(lax.empty(a.shape if per_chip_src else (4,) + a.shape, a.dtype), pltpu.HBM)
             for a in srcs]

    def body(*refs):
        ins, land = refs[:n], refs[n:2 * n]
        send, recv = refs[2 * n + 1], refs[2 * n + 2]
        token = refs[-1]
        x, y, c = _place()
        me = 2 * x + y
        for k, (ox, oy) in enumerate(_other_chips(x, y)):
            for a in range(n):
                src = ins[a].at[2 * ox + oy] if per_chip_src else ins[a]
                pltpu.make_async_remote_copy(src_ref=src, dst_ref=land[a].at[me], send_sem=send.at[k * n + a], recv_sem=recv.at[k * n + a],
                                             device_id=(ox, oy, c), device_id_type=MESH).start()
        token[...] = jnp.zeros_like(token)

    sem = pltpu.SemaphoreType.DMA((3 * n,))
    res = pl.pallas_call(
        body, name=name,
        out_shape=[sem, sem] + [pltpu.HBM(a.shape, a.dtype) for a in srcs] + [pltpu.HBM(b.shape, b.dtype) for b in lands]
                  + [jax.ShapeDtypeStruct((8, LANES), F32)],
        in_specs=[HBM] * (2 * n) + [pl.BlockSpec(memory_space=pl.ANY)],
        out_specs=[SEM, SEM] + [HBM] * (2 * n) + [pl.BlockSpec(memory_space=pltpu.VMEM)],
        input_output_aliases={k: 2 + k for k in range(2 * n)},
        compiler_params=pltpu.CompilerParams(has_side_effects=pltpu.SideEffectType.DATAFLOW_SIDE_EFFECTING),
    )(*[pltpu.with_memory_space_constraint(a, pltpu.HBM) for a in srcs], *lands, after)
    return res[:-1], res[-1]


def _chip_copies_wait(name, started, per_chip_src, after):
    send, recv = started[0], started[1]
    n = (len(started) - 2) // 2
    srcs, lands = started[2:2 + n], started[2 + n:]
    HBM = pl.BlockSpec(memory_space=pltpu.HBM)
    SEM = pl.BlockSpec(memory_space=pltpu.SEMAPHORE)

    def body(*refs):
        ins, land = refs[:n], refs[n:2 * n]
        send_sem, recv_sem = refs[2 * n], refs[2 * n + 1]
        x, y, c = _place()
        me = 2 * x + y
        for k, (ox, oy) in enumerate(_other_chips(x, y)):
            for a in range(n):
                src = ins[a].at[me] if per_chip_src else ins[a]
                cp = pltpu.make_async_remote_copy(src_ref=src, dst_ref=land[a].at[2 * ox + oy], send_sem=send_sem.at[k * n + a],
                                                  recv_sem=recv_sem.at[k * n + a], device_id=(ox, oy, c), device_id_type=MESH)
                cp.wait_send()
                cp.wait_recv()

    res = pl.pallas_call(
        body, name=name,
        out_shape=[pltpu.HBM(a.shape, a.dtype) for a in srcs] + [pltpu.HBM(b.shape, b.dtype) for b in lands],
        in_specs=[HBM] * (2 * n) + [SEM, SEM, pl.BlockSpec(memory_space=pl.ANY)], out_specs=[HBM] * (2 * n),
        input_output_aliases={k: k for k in range(2 * n)},
        compiler_params=pltpu.CompilerParams(has_side_effects=pltpu.SideEffectType.DATAFLOW_SIDE_EFFECTING),
    )(*srcs, *lands, send, recv, after)
    return res[n:]


def _half_to_sibling(name, blocks):
    n = len(blocks)
    ANY = pl.BlockSpec(memory_space=pl.ANY)

    def body(*refs):
        ins, outs = refs[:n], refs[n:2 * n]
        send, recv = refs[2 * n:]
        x, y, c = _place()
        cps = [pltpu.make_async_remote_copy(src_ref=ins[a].at[:, _half_rows(blocks[a].shape[1], 1 - c)], dst_ref=outs[a],
                                            send_sem=send.at[a], recv_sem=recv.at[a], device_id=(x, y, 1 - c),
                                            device_id_type=MESH) for a in range(n)]
        for cp in cps:
            cp.start()
        for cp in cps:
            cp.wait_recv()
        for cp in cps:
            cp.wait_send()

    return pl.pallas_call(
        body, name=name, in_specs=[ANY] * n, out_specs=[ANY] * n,
        out_shape=[jax.ShapeDtypeStruct((4, b.shape[1] // 2, b.shape[2]), b.dtype) for b in blocks],
        scratch_shapes=[pltpu.SemaphoreType.DMA((n,)), pltpu.SemaphoreType.DMA((n,))])(*blocks)


def _sibling_swap(name, arrs):
    n = len(arrs)
    ANY = pl.BlockSpec(memory_space=pl.ANY)

    def body(*refs):
        ins, outs = refs[:n], refs[n:2 * n]
        send, recv = refs[2 * n:]
        x, y, c = _place()
        cps = [pltpu.make_async_remote_copy(src_ref=ins[a], dst_ref=outs[a], send_sem=send.at[a], recv_sem=recv.at[a],
                                            device_id=(x, y, 1 - c), device_id_type=MESH) for a in range(n)]
        for cp in cps:
            cp.start()
        for cp in cps:
            cp.wait_recv()
        for cp in cps:
            cp.wait_send()

    return pl.pallas_call(
        body, name=name, in_specs=[ANY] * n, out_specs=[ANY] * n,
        out_shape=[jax.ShapeDtypeStruct(a.shape, a.dtype) for a in arrs],
        scratch_shapes=[pltpu.SemaphoreType.DMA((n,)), pltpu.SemaphoreType.DMA((n,))])(*arrs)


N_DEV = 8


def _all_sum_small(vec):
    P = vec.shape[1]

    def body(v_ref, o_ref, buf, send, recv):
        x, y, c = _place()
        me = 4 * x + 2 * y + c
        buf[me] = v_ref[...]

        def peer(r):
            return ((1 - x) if (r >> 2) & 1 else x, (1 - y) if (r >> 1) & 1 else y, (1 - c) if r & 1 else c)

        sends = []
        for r in range(1, N_DEV):
            cp = pltpu.make_async_remote_copy(src_ref=v_ref, dst_ref=buf.at[me], send_sem=send.at[r], recv_sem=recv.at[r],
                                              device_id=peer(r), device_id_type=MESH)
            cp.start()
            sends.append(cp)
        for r in range(1, N_DEV):
            px, py, pc = peer(r)
            pltpu.make_async_remote_copy(src_ref=v_ref, dst_ref=buf.at[4 * px + 2 * py + pc], send_sem=send.at[r],
                                         recv_sem=recv.at[r], device_id=(px, py, pc), device_id_type=MESH).wait_recv()
        for cp in sends:
            cp.wait_send()
        tot = buf[0]
        for d in range(1, N_DEV):
            tot = tot + buf[d]
        o_ref[...] = tot

    return pl.pallas_call(
        body, name="all_sum_small", in_specs=[pl.BlockSpec(memory_space=pltpu.VMEM)],
        out_specs=pl.BlockSpec(memory_space=pltpu.VMEM), out_shape=jax.ShapeDtypeStruct((1, P), F32),
        scratch_shapes=[pltpu.VMEM((N_DEV, 1, P), F32), pltpu.SemaphoreType.DMA((N_DEV,)), pltpu.SemaphoreType.DMA((N_DEV,))],
    )(vec)


def _half_sum(name, blocks, theirs, core, tr):
    _, R, C = blocks.shape
    hr = R // 2
    nb = hr // tr
    assert nb * tr == hr

    def body(c_ref, a_ref, b_ref, o_ref):
        o_ref[...] = (a_ref[...] + b_ref[...]).astype(BF16)

    grid_spec = pltpu.PrefetchScalarGridSpec(
        num_scalar_prefetch=1, grid=(4, nb),
        in_specs=[pl.BlockSpec((1, tr, C), lambda b, i, c_ref: (b, c_ref[0] * nb + i, 0)),
                  pl.BlockSpec((1, tr, C), lambda b, i, c_ref: (b, i, 0))],
        out_specs=pl.BlockSpec((1, tr, C), lambda b, i, c_ref: (b, i, 0)))
    return pl.pallas_call(body, name=name, grid_spec=grid_spec, out_shape=jax.ShapeDtypeStruct((4, hr, C), BF16),
                          compiler_params=_params(2, 40 << 20))(core, blocks, theirs)


def _sum4(name, stack, mine, chip, tr):
    _, R, C = stack.shape

    def body(chip_ref, s_ref, m_ref, o_ref):
        t = [jnp.where(chip_ref[0] == j, m_ref[j], s_ref[j]).astype(F32) for j in range(4)]
        o_ref[...] = ((t[0] + t[1]) + t[2]) + t[3]

    blk = pl.BlockSpec((4, tr, C), lambda i, chip_ref: (0, i, 0))
    grid_spec = pltpu.PrefetchScalarGridSpec(num_scalar_prefetch=1, grid=(R // tr,), in_specs=[blk, blk],
                                             out_specs=pl.BlockSpec((tr, C), lambda i, chip_ref: (i, 0)))
    return pl.pallas_call(body, name=name, grid_spec=grid_spec, out_shape=jax.ShapeDtypeStruct((R, C), F32),
                          compiler_params=_params(1, 40 << 20))(chip, stack, mine)


def _adamw_math(w, m, v, g):
    c1 = 1.0 - ADAM_B1 ** ADAM_STEP
    c2 = 1.0 - ADAM_B2 ** ADAM_STEP
    nm = ADAM_B1 * m + (1.0 - ADAM_B1) * g
    nv = ADAM_B2 * v + (1.0 - ADAM_B2) * (g * g)
    return -ADAM_LR * ((nm / c1) / (jnp.sqrt(nv / c2) + ADAM_EPS) + ADAM_WD * w), nm, nv


def _adamw(name, w, m, v, g, tr):
    R, C = w.shape

    def body(w_ref, m_ref, v_ref, ga_ref, g_ref, d_ref, nm_ref, nv_ref):
        g = ga_ref[...]
        g_ref[...] = g
        d_ref[...], nm_ref[...], nv_ref[...] = _adamw_math(w_ref[...], m_ref[...], v_ref[...], g)

    spec = pl.BlockSpec((tr, C), lambda i: (i, 0))
    return pl.pallas_call(body, name=name, grid=(R // tr,), in_specs=[spec] * 4, out_specs=[spec] * 4,
                          out_shape=[jax.ShapeDtypeStruct((R, C), F32)] * 4, compiler_params=_params(1, 40 << 20))(w, m, v, g)


def _adamw_flat(name, w, m, v, g, tr):
    R, C = w.shape

    def body(w_ref, m_ref, v_ref, ga_ref, g_ref, d_ref, nm_ref, nv_ref):
        g = ga_ref[...]
        g_ref[...] = g
        d_ref[...], nm_ref[...], nv_ref[...] = _adamw_math(w_ref[...], m_ref[...], v_ref[...], g)

    spec = pl.BlockSpec((tr, C), lambda i: (i, 0))
    return pl.pallas_call(body, name=name, grid=(R // tr,), in_specs=[spec] * 4, out_specs=[spec] * 4,
                          out_shape=[jax.ShapeDtypeStruct((R, C), F32)] * 4, compiler_params=_params(1, 56 << 20))(w, m, v, g)


def _adamw_halves(name, w, m, v, mine, theirs, core, tr):
    _, R, C = w.shape
    nb = (R // 2) // tr
    assert 2 * nb * tr == R

    def body(c_ref, w_ref, m_ref, v_ref, a_ref, b_ref, g_ref, d_ref, nm_ref, nv_ref):
        g = jnp.where((pl.program_id(0) // nb) == c_ref[0], a_ref[...], b_ref[...])
        g_ref[0] = g
        d_ref[0], nm_ref[0], nv_ref[0] = _adamw_math(w_ref[0], m_ref[0], v_ref[0], g)

    spec = pl.BlockSpec((1, tr, C), lambda i, c_ref: (0, i, 0))
    half = lambda own: pl.BlockSpec((tr, C), lambda i, c_ref, own=own: (
        jnp.clip(i - (c_ref[0] if own else 1 - c_ref[0]) * nb, 0, nb - 1), 0))
    grid_spec = pltpu.PrefetchScalarGridSpec(num_scalar_prefetch=1, grid=(R // tr,),
                                             in_specs=[spec, spec, spec, half(True), half(False)], out_specs=[spec] * 4)
    return pl.pallas_call(body, name=name, grid_spec=grid_spec, out_shape=[jax.ShapeDtypeStruct((1, R, C), F32)] * 4,
                          compiler_params=_params(1, 40 << 20))(core, w, m, v, mine, theirs)


def _row_tile(rows, cols, budget_bytes=1 << 20, mult=8):
    best = None
    for t in range(mult, rows + 1, mult):
        if rows % t == 0 and t * cols * 4 <= budget_bytes:
            best = t
    return best if best is not None else rows


def _ln_fwd(r, g, b):
    mu = jnp.mean(r, axis=1, keepdims=True)
    xc = r - mu
    rstd = lax.rsqrt(jnp.mean(xc * xc, axis=1, keepdims=True) + LN_EPS)
    xhat = xc * rstd
    return xhat * g + b, xhat, rstd


def _ln_bwd(dy, xhat, rstd, g):
    dxh = dy * g
    return rstd * (dxh - jnp.mean(dxh, axis=1, keepdims=True) - xhat * jnp.mean(dxh * xhat, axis=1, keepdims=True))


def _col_segments(sections, width):
    out, cur, room = [], [], width
    for arr, lo, hi in sections:
        while lo < hi:
            take = min(room, hi - lo)
            cur.append((arr, lo, lo + take))
            lo, room = lo + take, room - take
            if room == 0:
                out.append(cur)
                cur, room = [], width
    assert not cur
    return out


def _to_chip_blocks_cols(a):
    R, C4 = a.shape
    return a.reshape(R, 4, C4 // 4).transpose(1, 0, 2)


def _from_chip_blocks_cols(a):
    return a.transpose(1, 0, 2).reshape(a.shape[1], 4 * a.shape[2])


def kernel(x, w_in, b_forget, conv_w, conv_b, dt_bias, a_log, d_skip, ssm_norm_w, w_proj_attn, w_proj_ssm, b_gates, w_out, ln1_g, ln1_b, w_ffn_gate, w_ffn_up, w_ffn_down, ln2_g, ln2_b, loss_target, m_w_in, m_b_forget, m_conv_w, m_conv_b, m_dt_bias, m_a_log, m_d_skip, m_ssm_norm_w, m_w_proj_attn, m_w_proj_ssm, m_b_gates, m_w_out, m_ln1_g, m_ln1_b, m_w_ffn_gate, m_w_ffn_up, m_w_ffn_down, m_ln2_g, m_ln2_b, v_w_in, v_b_forget, v_conv_w, v_conv_b, v_dt_bias, v_a_log, v_d_skip, v_ssm_norm_w, v_w_proj_attn, v_w_proj_ssm, v_b_gates, v_w_out, v_ln1_g, v_ln1_b, v_w_ffn_gate, v_w_ffn_up, v_w_ffn_down, v_ln2_g, v_ln2_b):
    S = x.shape[1]
    D = D_MODEL
    TM, TM2, TM3, TA, AQF, LC, CV, TS, TB = (min(TILES[k], S) for k in ("TM", "TM2", "TM3", "TA", "AQF", "LC", "CV", "TS", "TB"))
    xf = x[0]
    tgt = loss_target[0]
    xb = xf.astype(BF16)

    shards = [w_in[0].astype(BF16), conv_w[0], w_proj_attn[0].astype(BF16), w_proj_ssm[0].astype(BF16), w_out[0].astype(BF16),
              w_ffn_gate[0].astype(BF16), w_ffn_up[0].astype(BF16), w_ffn_down[0].astype(BF16)]
    chip = 2 * lax.axis_index("x") + lax.axis_index("y")
    own = lambda gathered, mine: [lax.dynamic_update_slice(g, sh[None], (chip, 0, 0)) for g, sh in zip(gathered, mine)]
    g_in, g_cw = own(_chip_gather("gather_w_in", shards[:2], [True, False]), shards[:2])
    later, gather_token = _chip_copies_start("gather_rest_start", shards[2:], False, g_cw)
    shard_w = IN_WIDTH // 4

    def w_cols(lo, hi):
        return [g_in[j][:, max(lo, j * shard_w) - j * shard_w:min(hi, (j + 1) * shard_w) - j * shard_w]
                for j in range(4) if max(lo, j * shard_w) < min(hi, (j + 1) * shard_w)]

    w_re = jnp.concatenate(w_cols(0, 3072) + w_cols(3088, 5136) + w_cols(5136, 8208) + w_cols(8240, 10288)
                           + w_cols(3072, 3088) + w_cols(8208, 8240) + [jnp.zeros((D, 80), BF16)], axis=1)
    conv_w_full = _from_chip_blocks_cols(g_cw)

    def plain(accs, rows, vecs, j):
        return [accs[0]], []

    def q_scaled(accs, rows, vecs, j):
        return [accs[0] * jnp.where(j * 512 < D, 1.0 / math.sqrt(ATT_HEAD_DIM), 1.0)], []

    qkv, = _mm("proj_qkv", S, 3072, TM, 512, [(xb, D, 0)], [(w_re, 0)], [(0, 0)], q_scaled, [(3072, BF16, 0)],
               after=[gather_token])
    z, = _mm("proj_z", S, 2048, TM, 512, [(xb, D, 0)], [(w_re, RE_Z // 512)], [(0, 0)], plain, [(2048, BF16, 0)])
    xbc_raw, = _mm("proj_xbc", S, 3072, TM, 512, [(xb, D, 0)], [(w_re, RE_XBC // 512)], [(0, 0)], plain, [(3072, F32, 0)])
    gl, = _mm("proj_gate", S, 2048, TM, 512, [(xb, D, 0)], [(w_re, RE_GATE // 512)], [(0, 0)], plain, [(2048, BF16, 0)])
    small, = _mm("proj_small", S, 128, TM, 128, [(xb, D, 0)], [(w_re, RE_SMALL // 128)], [(0, 0)], plain, [(128, F32, 0)])

    bvec = jnp.concatenate([b_forget, jnp.zeros((1, LANES - ATT_HEADS), F32)], axis=1)
    cum = _cum_fwd(small, bvec, TB)[:, :ATT_HEADS]
    cum_cols = cum.reshape(S, 8, 2).transpose(1, 0, 2)
    qa, ka = _attn_prep(qkv, cum_cols, TM)
    o, o32, lse_rows = _attn_fwd(qa, ka, qkv, AQF, TA)

    cb_row = conv_b
    xbc = _conv_fwd(xbc_raw, conv_w_full, cb_row, CV, 512)
    dt_raw = small[:, 16:48]
    dtc = dt_raw.reshape(S, SSM_GROUPS, 8).transpose(1, 0, 2)
    dtr = dt_raw.T.reshape(SSM_GROUPS, 8, S)
    bias_r = dt_bias.reshape(SSM_GROUPS, 1, 8)
    alog_b = jnp.repeat(a_log, ATT_HEAD_DIM, axis=1)
    dskip_b = jnp.repeat(d_skip, ATT_HEAD_DIM, axis=1)
    bias_c = dt_bias.reshape(SSM_GROUPS, 8, 1)
    alog_c = a_log.reshape(SSM_GROUPS, 8, 1)
    y_ssd, ssm, hs_all = _ssd_fwd(xbc, z, dtc, dtr, bias_r, alog_b, dskip_b, ssm_norm_w, bias_c, alog_c, LC)

    def merge(accs, rows, vecs, j):
        g0, g1 = _sigmoid(rows[0].astype(F32) + vecs[0]), _sigmoid(rows[1].astype(F32) + vecs[1])
        return [g0 * accs[0] + g1 * accs[1], accs[0], accs[1]], []

    g_pa, g_ps, g_out, g_fg, g_fu, g_fd = own(_chip_copies_wait("gather_rest_wait", later, False, o), shards[2:])
    wpa, wps, wout = g_pa.reshape(D, D), g_ps.reshape(SSM_INNER, D), g_out.reshape(D, D)
    wfg, wfu, wfd = _from_chip_blocks_cols(g_fg), _from_chip_blocks_cols(g_fu), g_fd.reshape(FFN_HIDDEN, D)
    mix, attn_d, ssm_d = _mm("merge", S, D, TM, 512, [(o, D, 0), (ssm, SSM_INNER, 0)], [(wpa, 0), (wps, 0)], [(0, 0), (1, 1)],
                             merge, [(D, BF16, 0), (D, BF16, 0), (D, BF16, 0)], rows=[(gl, 0), (gl, 2)],
                             vecs_n=[(b_gates, 0), (b_gates, 2)])

    def out_ln1(accs, rows, vecs, j):
        r1 = ALPHA * rows[0] + accs[0]
        h1, _, _ = _ln_fwd(r1, vecs[0], vecs[1])
        return [r1, h1, h1], []

    r1, h1, h1b = _mm("out_ln1", S, D, TM2, D, [(mix, D, 0)], [(wout, 0)], [(0, 0)], out_ln1,
                      [(D, F32, 0), (D, F32, 0), (D, BF16, 0)], rows=[(xf, 0)], vecs_n=[(ln1_g, 0), (ln1_b, 0)])

    FT = FFN_HIDDEN // 2

    def swiglu(accs, rows, vecs, j):
        g, u = accs
        return [g, u, g * _sigmoid(g) * u], []

    gate, up, hmid = _mm("ffn_up", S, FFN_HIDDEN, TM3, FT, [(h1b, D, 0)], [(wfg, 0), (wfu, 0)], [(0, 0), (0, 1)], swiglu,
                         [(FFN_HIDDEN, BF16, 0), (FFN_HIDDEN, BF16, 0), (FFN_HIDDEN, BF16, 0)])

    def down_ln2_loss(accs, rows, vecs, j):
        r2 = ALPHA * rows[0] + accs[0]
        yv, xhat, rstd = _ln_fwd(r2, vecs[0], vecs[1])
        diff = yv - rows[1]
        dy = diff * (1.0 / D_MODEL)
        dr2 = _ln_bwd(dy, xhat, rstd, vecs[0])
        return [dr2, dr2], [jnp.sum(dy * xhat, axis=0, keepdims=True), jnp.sum(dy, axis=0, keepdims=True),
                            (0.5 / D_MODEL) * jnp.sum(diff * diff, axis=0, keepdims=True)]

    dr2, dr2b, dln2_g, dln2_b, loss_lanes = _mm("ffn_down_ln2", S, D, TM3, D, [(hmid, FFN_HIDDEN, 0)], [(wfd, 0)], [(0, 0)],
                                               down_ln2_loss, [(D, F32, 0), (D, BF16, 0)], rows=[(h1, 0), (tgt, 0)],
                                               vecs_n=[(ln2_g, 0), (ln2_b, 0)], sums=[D, D, D])
    loss = lax.psum(jnp.sum(loss_lanes), ("x", "y", "c"))

    def dswiglu(accs, rows, vecs, j):
        g, u = rows[0].astype(F32), rows[1].astype(F32)
        sg = _sigmoid(g)
        return [accs[0] * u * sg * (1.0 + g * (1.0 - sg)), accs[0] * g * sg], []

    dgate, dup = _mm("ffn_down_bwd", S, FFN_HIDDEN, TM3, FT, [(dr2b, D, 0)], [(wfd, 0)], [(0, 0)], dswiglu,
                     [(FFN_HIDDEN, BF16, 0), (FFN_HIDDEN, BF16, 0)], nt=True, rows=[(gate, 0), (up, 0)])
    dwfd = _mm_tn("dw_ffn_down", hmid, dr2b, FFN_HIDDEN // 2, D, TS)
    dwfg = _mm_tn("dw_ffn_gate", h1b, dgate, D, FT, TS)
    dwfu = _mm_tn("dw_ffn_up", h1b, dup, D, FT, TS)
    core = lax.axis_index("c").astype(jnp.int32).reshape(1)

    def send_grads(tag, names_, blocks_, after_):
        theirs_ = _half_to_sibling("swap_halves_" + tag, blocks_)
        halves_ = [_half_sum("halfsum_" + nm, b, t, core, _row_tile(b.shape[1] // 2, b.shape[2], mult=16))
                   for nm, b, t in zip(names_, blocks_, theirs_)]
        started_, token_ = _chip_copies_start("scatter_" + tag + "_start", halves_, True, after_)
        return halves_, started_, token_

    ffn_names = ["w_ffn_gate", "w_ffn_up", "w_ffn_down"]
    ffn_halves, ffn_started, ffn_token = send_grads(
        "ffn", ffn_names, [_to_chip_blocks_cols(dwfg), _to_chip_blocks_cols(dwfu), dwfd.reshape(4, FFN_HIDDEN // 4, D)], dwfu)

    def dh1_ln1(accs, rows, vecs, j):
        dh1 = ALPHA * rows[0] + accs[0] + accs[1]
        _, xhat, rstd = _ln_fwd(rows[1], vecs[0], vecs[0])
        dr1 = _ln_bwd(dh1, xhat, rstd, vecs[0])
        return [dr1, dr1], [jnp.sum(dh1 * xhat, axis=0, keepdims=True), jnp.sum(dh1, axis=0, keepdims=True)]

    dr1, dr1b, dln1_g, dln1_b = _mm("ffn_up_bwd_ln1", S, D, TM2, D, [(dgate, FFN_HIDDEN, 0), (dup, FFN_HIDDEN, 0)],
                                    [(wfg, 0), (wfu, 0)], [(0, 0), (1, 1)], dh1_ln1, [(D, F32, 0), (D, BF16, 0)], nt=True,
                                    rows=[(dr2, 0), (r1, 0)], vecs_n=[(ln1_g, 0)], sums=[D, D], after=[ffn_token])

    def dmerge(accs, rows, vecs, j):
        dmix = accs[0]
        g0, g1 = _sigmoid(rows[0].astype(F32) + vecs[0]), _sigmoid(rows[1].astype(F32) + vecs[1])
        dgl0 = dmix * rows[2].astype(F32) * g0 * (1.0 - g0)
        dgl1 = dmix * rows[3].astype(F32) * g1 * (1.0 - g1)
        return [dmix * g0, dmix * g1, dgl0, dgl1], [jnp.sum(dgl0, axis=0, keepdims=True), jnp.sum(dgl1, axis=0, keepdims=True)]

    d_attn_d, d_ssm_d, dgl0, dgl1, dbg0, dbg1 = _mm(
        "out_bwd", S, D, TM, 512, [(dr1b, D, 0)], [(wout, 0)], [(0, 0)], dmerge, [(D, BF16, 0)] * 4, nt=True,
        rows=[(gl, 0), (gl, 2), (attn_d, 0), (ssm_d, 0)], vecs_n=[(b_gates, 0), (b_gates, 2)], sums=[D, D])
    dwout = _mm_tn("dw_out", mix, dr1b, D, D, TS)
    dwpa = _mm_tn("dw_proj_attn", o, d_attn_d, D, D, TS)
    dwps = _mm_tn("dw_proj_ssm", ssm, d_ssm_d, D, D, TS)
    mid_names = ["w_proj_attn", "w_proj_ssm", "w_out"]
    mid_halves, mid_started, mid_token = send_grads(
        "mid", mid_names, [dwpa.reshape(4, D // 4, D), dwps.reshape(4, SSM_INNER // 4, D), dwout.reshape(4, D // 4, D)], dwps)

    do, = _mm("proj_attn_bwd", S, D, TM, 512, [(d_attn_d, D, 0)], [(wpa, 0)], [(0, 0)], plain, [(D, BF16, 0)], nt=True,
              after=[mid_token])
    stats = _attn_stats(do, o32, lse_rows.transpose(0, 2, 1), AQF)
    dq, dk, dv, dck, dcq = _attn_bwd(qa, ka, qkv, do, stats, TA)

    def per_head(a):
        a = a.transpose(1, 0, 2).reshape(S, ATT_HEADS)
        return jnp.concatenate([a, jnp.zeros((S, LANES - ATT_HEADS), F32)], axis=1)

    dfl, dbf = _cum_bwd(per_head(dck.transpose(0, 2, 1)), per_head(dcq), small, bvec, TB)

    dssm, = _mm("proj_ssm_bwd", S, SSM_INNER, TM, 512, [(d_ssm_d, D, 0)], [(wps, 0)], [(0, 0)], plain, [(SSM_INNER, F32, 0)],
                nt=True)
    dxs, dbm, dcm, dz, ddt8, dnw, ddskip_b, dalog8, dbias8 = _ssd_bwd(
        xbc, z, y_ssd, dssm, hs_all, dtc, dtr, bias_r, a_log.reshape(SSM_GROUPS, 1, 8), alog_b, dskip_b, ssm_norm_w, bias_c,
        alog_c, LC)
    du_x, dcw_x, dcb_x = _conv_bwd("conv_bwd_x", xbc_raw, dxs, conv_w_full, cb_row, CV, 512, 0)
    du_b, dcw_b, dcb_b = _conv_bwd("conv_bwd_b", xbc_raw, dbm, conv_w_full, cb_row, CV, 512, SSM_INNER)
    du_c, dcw_c, dcb_c = _conv_bwd("conv_bwd_c", xbc_raw, dcm, conv_w_full, cb_row, CV, 512, SSM_INNER + SSM_GROUPS * SSM_STATE)
    dconv_w = jnp.concatenate([dcw_x, dcw_b, dcw_c], axis=1)
    dconv_b = jnp.concatenate([dcb_x, dcb_b, dcb_c], axis=1)
    ddt_raw = ddt8.transpose(1, 0, 2).reshape(S, SSM_HEADS)

    dsmall = jnp.concatenate([dfl[:, :ATT_HEADS], ddt_raw, jnp.zeros((S, 80), F32)], axis=1).astype(BF16)
    HB = SSM_GROUPS * SSM_STATE
    dw_q, dw_k, dw_v = (_mm_tn("dw_in_" + nm, xb, g_, D, D, TS) for nm, g_ in (("q", dq), ("k", dk), ("v", dv)))
    dw_z = _mm_tn("dw_in_z", xb, dz, D, D, TS)
    dw_xs, dw_b, dw_c = _mm_tn("dw_in_xs", xb, du_x, D, D, TS), _mm_tn("dw_in_b", xb, du_b, D, HB, TS), _mm_tn("dw_in_c", xb, du_c, D, HB, TS)
    dw_g0, dw_g1 = _mm_tn("dw_in_g0", xb, dgl0, D, D, TS), _mm_tn("dw_in_g1", xb, dgl1, D, D, TS)
    dw_s = _mm_tn("dw_in_small", xb, dsmall, D, LANES, TS)
    whole = lambda a: (a, 0, a.shape[1])
    dw_sections = [whole(dw_q), whole(dw_k), whole(dw_v), (dw_s, 0, ATT_HEADS), whole(dw_z), whole(dw_xs), whole(dw_b), whole(dw_c),
                   (dw_s, ATT_HEADS, ATT_HEADS + SSM_HEADS), whole(dw_g0), whole(dw_g1)]
    dw_blocks = jnp.stack([jnp.concatenate([a[:, lo:hi] for a, lo, hi in segs], axis=1)
                           for segs in _col_segments(dw_sections, shard_w)])

    in_halves, in_started, in_token = send_grads("in", ["w_in"], [dw_blocks], dw_blocks)
    def dx_first(accs, rows, vecs, j):
        return [ALPHA * rows[0] + sum(accs[1:], accs[0])], []

    def dx_more(accs, rows, vecs, j):
        return [rows[0] + sum(accs[1:], accs[0])], []

    wk = lambda col, width=D: (w_re, 0, col // width, width)
    dx_part, = _mm("dx_a", S, D, TM2, D, [(dq, D, 0), (dk, D, 0), (dv, D, 0), (dz, D, 0), (dz, D, 1)],
                   [wk(0), wk(1024), wk(2048), wk(RE_Z), wk(RE_Z + 1024)], [(k, k) for k in range(5)], dx_first,
                   [(D, F32, 0)], nt=True, rows=[(dr1, 0)], after=[in_token])
    grad_x, = _mm("dx_b", S, D, TM2, D,
                  [(du_x, D, 0), (du_x, D, 1), (du_b, HB, 0), (du_c, HB, 0), (dgl0, D, 0), (dgl1, D, 0), (dsmall, LANES, 0)],
                  [wk(RE_XBC), wk(RE_XBC + 1024), wk(RE_XBC + 2048, HB), wk(RE_XBC + 2048 + HB, HB), wk(RE_GATE),
                   wk(RE_GATE + 1024), wk(RE_SMALL, LANES)],
                  [(k, k) for k in range(7)], dx_more, [(D, F32, 0)], nt=True, rows=[(dx_part, 0)])
    names = ["w_in"] + mid_names + ffn_names
    halves = in_halves + mid_halves + ffn_halves
    stacks = (_chip_copies_wait("scatter_in_wait", in_started, True, grad_x)
              + _chip_copies_wait("scatter_mid_wait", mid_started, True, grad_x)
              + _chip_copies_wait("scatter_ffn_wait", ffn_started, True, grad_x))
    chip1 = chip.astype(jnp.int32).reshape(1)
    reduced = [_sum4("sum_" + nm, st, hv, chip1, _row_tile(st.shape[1], st.shape[2], mult=16))
               for nm, st, hv in zip(names, stacks, halves)]
    other = _sibling_swap("swap_reduced", reduced)
    big_w = [w_in, w_proj_attn, w_proj_ssm, w_out, w_ffn_gate, w_ffn_up, w_ffn_down]
    big_m = [m_w_in, m_w_proj_attn, m_w_proj_ssm, m_w_out, m_w_ffn_gate, m_w_ffn_up, m_w_ffn_down]
    big_v = [v_w_in, v_w_proj_attn, v_w_proj_ssm, v_w_out, v_w_ffn_gate, v_w_ffn_up, v_w_ffn_down]
    big = {}
    lower, upper = jnp.where(core[0] == 0, reduced[0], other[0]), jnp.where(core[0] == 0, other[0], reduced[0])
    g_in_t = jnp.concatenate([lower.T, upper.T], axis=1)
    flat = lambda a: jnp.transpose(a, (2, 0, 1)).reshape(-1, LANES)
    unflat = lambda a: jnp.transpose(a.reshape(shard_w, 1, D), (1, 2, 0))
    flat_rows = shard_w * D // LANES
    big["w_in"] = [unflat(r) for r in _adamw_flat("adamw_w_in", flat(w_in), flat(m_w_in), flat(v_w_in),
                                                  g_in_t.reshape(-1, LANES), _row_tile(flat_rows, LANES, 3 << 20))]
    for nm, w_, m_, v_, mine, theirs in list(zip(names, big_w, big_m, big_v, reduced, other))[1:]:
        if nm in ("w_ffn_gate", "w_ffn_up"):
            lower, upper = jnp.where(core[0] == 0, mine, theirs), jnp.where(core[0] == 0, theirs, mine)
            across = lambda a: jnp.transpose(a, (0, 2, 1))
            res = _adamw("adamw_" + nm, across(w_)[0], across(m_)[0], across(v_)[0], jnp.concatenate([lower.T, upper.T], axis=1),
                         _row_tile(w_.shape[2], w_.shape[1]))
            big[nm] = [across(r[None]) for r in res]
            continue
        big[nm] = _adamw_halves("adamw_" + nm, w_, m_, v_, mine, theirs, core, _row_tile(w_.shape[1] // 2, w_.shape[2]))

    dd_skip = ddskip_b.reshape(1, SSM_HEADS, ATT_HEAD_DIM).sum(axis=2)
    pieces = [dbf[:, :ATT_HEADS], dconv_w.reshape(1, SSM_CONV * SSM_CONV_DIM), dconv_b, dbias8.reshape(1, SSM_HEADS), dalog8.reshape(1, SSM_HEADS), dd_skip,
              dnw, dbg0, dbg1, dln1_g, dln1_b, dln2_g, dln2_b]
    widths = [p.shape[1] for p in pieces]
    total = sum(widths)
    P = -(-total // LANES) * LANES
    packed = jnp.concatenate(pieces + [jnp.zeros((1, P - total), F32)], axis=1)
    summed = _all_sum_small(packed)
    offs = [0]
    for wd in widths:
        offs.append(offs[-1] + wd)
    sm = [summed[:, offs[k]:offs[k + 1]] for k in range(len(pieces))]
    g_bf, g_cw_full, g_cb, g_dtb, g_al, g_ds, g_nw = sm[0], sm[1].reshape(SSM_CONV, SSM_CONV_DIM), sm[2], sm[3], sm[4], sm[5], sm[6]
    g_bg = jnp.concatenate([sm[7], sm[8]], axis=1)
    g_l1g, g_l1b, g_l2g, g_l2b = sm[9], sm[10], sm[11], sm[12]
    cshard = SSM_CONV_DIM // 4
    g_cw_shard = lax.dynamic_slice_in_dim(g_cw_full, chip * cshard, cshard, axis=1)
    small_names = ["b_forget", "conv_w", "conv_b", "dt_bias", "a_log", "d_skip", "ssm_norm_w", "b_gates", "ln1_g", "ln1_b",
                   "ln2_g", "ln2_b"]
    small_g = [g_bf, g_cw_shard.reshape(1, -1), g_cb, g_dtb, g_al, g_ds, g_nw, g_bg, g_l1g, g_l1b, g_l2g, g_l2b]
    small_w = [b_forget, conv_w[0].reshape(1, -1), conv_b, dt_bias, a_log, d_skip, ssm_norm_w, b_gates, ln1_g, ln1_b, ln2_g, ln2_b]
    small_m = [m_b_forget, m_conv_w[0].reshape(1, -1), m_conv_b, m_dt_bias, m_a_log, m_d_skip, m_ssm_norm_w, m_b_gates, m_ln1_g,
               m_ln1_b, m_ln2_g, m_ln2_b]
    small_v = [v_b_forget, v_conv_w[0].reshape(1, -1), v_conv_b, v_dt_bias, v_a_log, v_d_skip, v_ssm_norm_w, v_b_gates, v_ln1_g,
               v_ln1_b, v_ln2_g, v_ln2_b]
    sw = [a.shape[1] for a in small_w]
    stot = sum(sw)
    SP = -(-stot // LANES) * LANES

    def pack(parts):
        return jnp.concatenate(list(parts) + [jnp.zeros((1, SP - stot), F32)], axis=1).reshape(SP // LANES, LANES)

    sres = _adamw("adamw_small", pack(small_w), pack(small_m), pack(small_v), pack(small_g), SP // LANES)
    soffs = [0]
    for wd in sw:
        soffs.append(soffs[-1] + wd)
    smalls = {}
    for k, nm in enumerate(small_names):
        vals = [r.reshape(1, SP)[:, soffs[k]:soffs[k + 1]] for r in sres]
        if nm == "conv_w":
            vals = [v_.reshape(1, SSM_CONV, cshard) for v_ in vals]
        smalls[nm] = vals

    order = ["w_in", "b_forget", "conv_w", "conv_b", "dt_bias", "a_log", "d_skip", "ssm_norm_w", "w_proj_attn", "w_proj_ssm",
             "b_gates", "w_out", "ln1_g", "ln1_b", "w_ffn_gate", "w_ffn_up", "w_ffn_down", "ln2_g", "ln2_b"]
    allres = {**big, **smalls}
    outs = [loss, grad_x[None]]
    for idx in range(4):
        outs += [allres[nm][idx] for nm in order]
    return tuple(outs)
```

```python
import functools
import math

import jax
import jax.numpy as jnp
from jax import lax
from jax.experimental import pallas as pl
from jax.experimental.pallas import tpu as pltpu

F32, BF16 = jnp.float32, jnp.bfloat16
MESH = pl.DeviceIdType.MESH

D_MODEL = 1024
ATT_HEADS, ATT_HEAD_DIM = 16, 64
SSM_INNER, SSM_HEADS, SSM_GROUPS, SSM_STATE, SSM_CONV = 2048, 32, 4, 128, 4
SSM_CONV_DIM = SSM_INNER + 2 * SSM_GROUPS * SSM_STATE
GROUP_LANES = SSM_INNER // SSM_GROUPS
FFN_HIDDEN = 2816
ALPHA = 2.0 ** 0.25
LN_EPS = 1e-5
RMS_EPS = 1e-5
ADAM_LR, ADAM_B1, ADAM_B2, ADAM_EPS, ADAM_WD, ADAM_STEP = 0.001, 0.9, 0.999, 1e-08, 0.01, 10
IN_SIZES = (1024, 1024, 1024, 16, 2048, 3072, 32, 2048)
IN_WIDTH = sum(IN_SIZES)
RE_WIDTH = 3072 + 2048 + 3072 + 2048 + 128
RE_Z, RE_XBC, RE_GATE, RE_SMALL = 3072, 5120, 8192, 10240

LANES = 128
VMEM_CAP = 60 * 1024 * 1024
NEG = -1e30
TILES = dict(TM=1024, TM2=256, TM3=512, TA=512, AQF=2048, LC=256, CV=512, TS=2048, TB=256)


def _params(n_axes, vmem_bytes=None):
    return pltpu.CompilerParams(dimension_semantics=("arbitrary",) * n_axes,
                                vmem_limit_bytes=None if vmem_bytes is None else int(min(vmem_bytes, VMEM_CAP)))


def _sigmoid(v):
    return 1.0 / (1.0 + jnp.exp(-v))


def _softplus(v):
    return jnp.maximum(v, 0.0) + jnp.log(1.0 + jnp.exp(-jnp.abs(v)))


def _dot(a, b):
    return lax.dot_general(a, b, (((1,), (0,)), ((), ())), preferred_element_type=F32)


def _dot_nt(a, b):
    return lax.dot_general(a, b, (((1,), (1,)), ((), ())), preferred_element_type=F32)


def _dot_tn(a, b):
    return lax.dot_general(a, b, (((0,), (0,)), ((), ())), preferred_element_type=F32)


def _split3(v):
    hi = v.astype(BF16)
    r1 = v - hi.astype(F32)
    mid = r1.astype(BF16)
    lo = (r1 - mid.astype(F32)).astype(BF16)
    return hi, mid, lo


def _dot_exact_left(m01, v):
    hi, mid, lo = _split3(v)
    return _dot(m01, hi) + _dot(m01, mid) + _dot(m01, lo)


def _dot_exact_right(v, m01, terms=3):
    parts = _split3(v)[:terms]
    out = _dot(parts[0], m01)
    for p in parts[1:]:
        out = out + _dot(p, m01)
    return out


def _mm(name, M, N, tm, tn, lhs, rhs, pairs, e_fn, outs, *, nt=False, rows=(), vecs_n=(), sums=(), after=()):
    ni, nj = M // tm, N // tn
    assert ni * tm == M and nj * tn == N, (name, M, N, tm, tn)
    n_l, n_r, n_row, n_vn, n_o, n_s = len(lhs), len(rhs), len(rows), len(vecs_n), len(outs), len(sums)

    def body(*refs):
        pos = 0
        l_refs = refs[pos:pos + n_l]; pos += n_l
        r_refs = refs[pos:pos + n_r]; pos += n_r
        row_refs = refs[pos:pos + n_row]; pos += n_row
        vn_refs = refs[pos:pos + n_vn]; pos += n_vn + len(after)
        o_refs = refs[pos:pos + n_o]; pos += n_o
        s_refs = refs[pos:pos + n_s]; pos += n_s
        i, j = pl.program_id(0), pl.program_id(1)
        accs = []
        for li, ri in pairs:
            accs.append(_dot_nt(l_refs[li][...], r_refs[ri][...]) if nt else _dot(l_refs[li][...], r_refs[ri][...]))
        out_vals, sum_vals = e_fn(accs, [r[...] for r in row_refs], [r[...] for r in vn_refs], j)
        for r, v in zip(o_refs, out_vals):
            r[...] = v.astype(r.dtype)
        if n_s:
            col = pl.multiple_of(j * tn, LANES)

            @pl.when(i == 0)
            def _():
                for r, v in zip(s_refs, sum_vals):
                    r[:, pl.ds(col, tn)] = v

            @pl.when(i > 0)
            def _():
                for r, v in zip(s_refs, sum_vals):
                    r[:, pl.ds(col, tn)] += v

    in_specs, args, est = [], [], 0
    for arr, width, cb in lhs:
        in_specs.append(pl.BlockSpec((tm, width), lambda i, j, cb=cb: (i, cb)))
        args.append(arr); est += tm * width * arr.dtype.itemsize
    for arr, off, *ksub in rhs:
        if nt:
            kb, kw = ksub if ksub else (0, arr.shape[1])
            in_specs.append(pl.BlockSpec((tn, kw), lambda i, j, off=off, kb=kb: (j + off, kb)))
            est += tn * kw * arr.dtype.itemsize
        else:
            in_specs.append(pl.BlockSpec((arr.shape[0], tn), lambda i, j, off=off: (0, j + off)))
            est += tn * arr.shape[0] * arr.dtype.itemsize
        args.append(arr)
    for arr, off in rows:
        in_specs.append(pl.BlockSpec((tm, tn), lambda i, j, off=off: (i, j + off)))
        args.append(arr); est += tm * tn * arr.dtype.itemsize
    for arr, off in vecs_n:
        in_specs.append(pl.BlockSpec((1, tn), lambda i, j, off=off: (0, j + off)))
        args.append(arr); est += 8 * tn * 4
    for arr in after:
        in_specs.append(pl.BlockSpec(memory_space=pl.ANY))
        args.append(arr)
    out_shape, out_specs = [], []
    for total, dtype, off in outs:
        out_shape.append(jax.ShapeDtypeStruct((M, total), dtype))
        out_specs.append(pl.BlockSpec((tm, tn), lambda i, j, off=off: (i, j + off)))
        est += tm * tn * jnp.dtype(dtype).itemsize
    for total in sums:
        out_shape.append(jax.ShapeDtypeStruct((1, total), F32))
        out_specs.append(pl.BlockSpec((1, total), lambda i, j: (0, 0)))
        est += 8 * total * 4
    vmem = 2 * est + (len(pairs) + 4) * tm * tn * 4 + (8 << 20)
    return pl.pallas_call(body, name=name, grid=(ni, nj), in_specs=in_specs, out_specs=out_specs, out_shape=out_shape,
                          compiler_params=_params(2, vmem))(*args)


def _mm_tn(name, a, g, ta, tn, ts, a_cols=None, a_off=0):
    S = a.shape[0]
    Ka = a.shape[1] if a_cols is None else a_cols
    N = g.shape[1]
    assert Ka % ta == 0 and N % tn == 0 and S % ts == 0, (name, Ka, N, S)
    aoff = a_off // ta

    def body(a_ref, g_ref, o_ref):
        s = pl.program_id(2)
        part = _dot_tn(a_ref[...], g_ref[...])

        @pl.when(s == 0)
        def _():
            o_ref[...] = part

        @pl.when(s > 0)
        def _():
            o_ref[...] += part

    vmem = 2 * (ts * ta * 2 + ts * tn * 2 + ta * tn * 4) + 2 * ta * tn * 4 + (8 << 20)
    return pl.pallas_call(
        body, name=name, grid=(Ka // ta, N // tn, S // ts),
        in_specs=[pl.BlockSpec((ts, ta), lambda ia, jn, s: (s, ia + aoff)), pl.BlockSpec((ts, tn), lambda ia, jn, s: (s, jn))],
        out_specs=pl.BlockSpec((ta, tn), lambda ia, jn, s: (ia, jn)),
        out_shape=jax.ShapeDtypeStruct((Ka, N), F32), compiler_params=_params(3, vmem))(a, g)


def _mm_tn_shared(name, a, gs, ts):
    S, Ka = a.shape
    n = len(gs)
    assert S % ts == 0, (name, S, ts)

    def body(a_ref, *refs):
        s = pl.program_id(0)
        a_tile = a_ref[...]
        for g_ref, o_ref in zip(refs[:n], refs[n:]):
            part = _dot_tn(a_tile, g_ref[...])

            @pl.when(s == 0)
            def _(o_ref=o_ref, part=part):
                o_ref[...] = part

            @pl.when(s > 0)
            def _(o_ref=o_ref, part=part):
                o_ref[...] += part

    widths = [g.shape[1] for g in gs]
    vmem = 2 * ts * (Ka + sum(widths)) * 2 + 3 * Ka * sum(widths) * 4 + (8 << 20)
    return pl.pallas_call(
        body, name=name, grid=(S // ts,),
        in_specs=[pl.BlockSpec((ts, Ka), lambda s: (s, 0))] + [pl.BlockSpec((ts, w), lambda s: (s, 0)) for w in widths],
        out_specs=[pl.BlockSpec((Ka, w), lambda s: (0, 0)) for w in widths],
        out_shape=[jax.ShapeDtypeStruct((Ka, w), F32) for w in widths], compiler_params=_params(1, vmem))(a, *gs)


def _tri(n, upper):
    r = lax.broadcasted_iota(jnp.int32, (n, n), 0)
    c = lax.broadcasted_iota(jnp.int32, (n, n), 1)
    return jnp.where((c >= r) if upper else (c <= r), 1.0, 0.0).astype(BF16)


def _logsig(v):
    return jnp.minimum(v, 0.0) - jnp.log(1.0 + jnp.exp(-jnp.abs(v)))


def _cum_fwd(small, bvec, tb):
    S = small.shape[0]

    def body(x_ref, b_ref, o_ref, carry):
        i = pl.program_id(0)

        @pl.when(i == 0)
        def _():
            carry[...] = jnp.zeros_like(carry)

        logf = _logsig(x_ref[...] + b_ref[...])
        cum = _dot_exact_left(_tri(tb, False), logf) + carry[0:1, :]
        o_ref[...] = cum
        carry[0:1, :] = cum[tb - 1:tb, :]

    return pl.pallas_call(
        body, name="cum_fwd", grid=(S // tb,),
        in_specs=[pl.BlockSpec((tb, LANES), lambda i: (i, 0)), pl.BlockSpec((1, LANES), lambda i: (0, 0))],
        out_specs=pl.BlockSpec((tb, LANES), lambda i: (i, 0)), out_shape=jax.ShapeDtypeStruct((S, LANES), F32),
        scratch_shapes=[pltpu.VMEM((8, LANES), F32)], compiler_params=_params(1))(small, bvec)


def _cum_bwd(dcum_k, dcum_q, small, bvec, tb):
    S = small.shape[0]
    nb = S // tb

    def body(dk_ref, dq_ref, x_ref, b_ref, o_ref, s_ref, carry):
        i = pl.program_id(0)

        @pl.when(i == 0)
        def _():
            carry[...] = jnp.zeros_like(carry)
            s_ref[...] = jnp.zeros_like(s_ref)

        rc = _dot_exact_left(_tri(tb, True), dk_ref[...] + dq_ref[...]) + carry[0:1, :]
        dfl = rc * _sigmoid(-(x_ref[...] + b_ref[...]))
        o_ref[...] = dfl
        s_ref[...] += jnp.sum(dfl, axis=0, keepdims=True)
        carry[0:1, :] = rc[0:1, :]

    rev = lambda i: (nb - 1 - i, 0)
    return pl.pallas_call(
        body, name="cum_bwd", grid=(nb,),
        in_specs=[pl.BlockSpec((tb, LANES), rev)] * 3 + [pl.BlockSpec((1, LANES), lambda i: (0, 0))],
        out_specs=[pl.BlockSpec((tb, LANES), rev), pl.BlockSpec((1, LANES), lambda i: (0, 0))],
        out_shape=[jax.ShapeDtypeStruct((S, LANES), F32), jax.ShapeDtypeStruct((1, LANES), F32)],
        scratch_shapes=[pltpu.VMEM((8, LANES), F32)], compiler_params=_params(1))(dcum_k, dcum_q, small, bvec)


N_AUG = 3


def _lane():
    return lax.broadcasted_iota(jnp.int32, (1, LANES), 1)


def _lane_mask():
    return _lane() < ATT_HEAD_DIM


def _aug_base(h):
    return ATT_HEAD_DIM * (1 - h)


def _attn_prep(qkv, cum_cols, T):
    S = qkv.shape[0]
    HP = ATT_HEADS // 2

    def body(q_ref, k_ref, c_ref, qa_ref, ka_ref):
        lane = _lane()
        q = q_ref[...]
        k = k_ref[...]
        one, zero = jnp.ones_like(q), jnp.zeros_like(q)
        for h in (0, 1):
            base = _aug_base(h)
            own = (lane < ATT_HEAD_DIM) if h == 0 else (lane >= ATT_HEAD_DIM)
            term_lanes = (lane >= base) & (lane < base + N_AUG)
            terms = [t.astype(F32) for t in _split3(c_ref[0, :, h:h + 1])]
            neg = jnp.where(lane == base, -terms[0], jnp.where(lane == base + 1, -terms[1], -terms[2])).astype(BF16)
            qa_ref[:, h * LANES:(h + 1) * LANES] = jnp.where(lane == base + N_AUG, zero, jnp.where(term_lanes, one, q))
            ka_ref[:, h * LANES:(h + 1) * LANES] = jnp.where(term_lanes, neg, jnp.where(lane == base + N_AUG, one,
                                                                                         jnp.where(own, k, zero)))

    return pl.pallas_call(
        body, name="attn_prep", grid=(S // T, HP),
        in_specs=[pl.BlockSpec((T, LANES), lambda i, hp: (i, hp)), pl.BlockSpec((T, LANES), lambda i, hp: (i, HP + hp)),
                  pl.BlockSpec((1, T, 2), lambda i, hp: (hp, i, 0))],
        out_specs=[pl.BlockSpec((T, 2 * LANES), lambda i, hp: (i, hp))] * 2,
        out_shape=[jax.ShapeDtypeStruct((S, 2 * D_MODEL), BF16)] * 2, compiler_params=_params(2))(qkv, qkv, cum_cols)


def _attn_fwd(qa, ka, qkv, T, TK):
    S = qkv.shape[0]
    nq = S // T
    r = T // TK
    HP = ATT_HEADS // 2

    def body(q0_ref, q1_ref, k0_ref, k1_ref, v_ref, o_ref, o32_ref, lse_ref):
        i = pl.program_id(1)
        qs = (q0_ref[...], q1_ref[...])
        k_refs = (k0_ref, k1_ref)
        row = lax.broadcasted_iota(jnp.int32, (TK, T), 0)
        col = lax.broadcasted_iota(jnp.int32, (TK, T), 1)
        head_rows = lax.broadcasted_iota(jnp.int32, (LANES, 1), 0) < ATT_HEAD_DIM

        def block(j, carry, q0):
            off = pl.multiple_of(j * TK, TK)
            vj = v_ref[pl.ds(off, TK), :]
            full = q0 is None
            q0 = 0 if full else q0
            m0, l0, m1, l1, acc = carry
            new, alphas, pvs = [], [], []
            for h, (m, l) in enumerate(((m0, l0), (m1, l1))):
                st = _dot_nt(k_refs[h][pl.ds(off, TK), :], qs[h][q0:, :])
                if not full:
                    st = jnp.where(row[:, :T - q0] <= col[:, :T - q0], st, NEG)
                m_old, l_old = m[:, q0:], l[:, q0:]
                m_new = jnp.maximum(m_old, jnp.max(st, axis=0, keepdims=True))
                p = jnp.exp(st - m_new)
                alpha = jnp.exp(m_old - m_new)
                l_new = alpha * l_old + jnp.sum(p, axis=0, keepdims=True)
                pvs.append(_dot_tn(vj, p.astype(BF16)))
                alphas.append(alpha)
                new += [m_new, l_new]
            part = acc[:, q0:] * jnp.where(head_rows, alphas[0], alphas[1]) + jnp.where(head_rows, pvs[0], pvs[1])
            if q0:
                keep = lambda old, upd: jnp.concatenate([old[:, :q0], upd], axis=1)
                return (keep(m0, new[0]), keep(l0, new[1]), keep(m1, new[2]), keep(l1, new[3]), keep(acc, part))
            return (new[0], new[1], new[2], new[3], part)

        init = (jnp.full((1, T), NEG, F32), jnp.zeros((1, T), F32), jnp.full((1, T), NEG, F32), jnp.zeros((1, T), F32),
                jnp.zeros((LANES, T), F32))
        n_full = i * r
        carry = lax.fori_loop(0, n_full // 2, lambda jj, c: block(2 * jj + 1, block(2 * jj, c, None), None), init)
        carry = lax.cond(n_full % 2 == 1, lambda c: block(n_full - 1, c, None), lambda c: c, carry)
        for d in range(r):
            carry = block(n_full + d, carry, d * TK)
        m0, l0, m1, l1, acc = carry
        out = (acc / jnp.where(head_rows, l0, l1)).T
        o_ref[...] = out.astype(BF16)
        o32_ref[...] = out
        lse_ref[0, 0:1, :] = m0 + jnp.log(l0)
        lse_ref[0, 1:2, :] = m1 + jnp.log(l1)

    vmem = 2 * (2 * T * LANES * 2 + 3 * S * LANES * 2 + T * LANES * (2 + 4) + 8 * T * 4) + 10 * T * TK * 4 + (8 << 20)
    qspec = lambda h: pl.BlockSpec((T, LANES), lambda hp, i, h=h: (i, 2 * hp + h))
    kspec = lambda h: pl.BlockSpec((S, LANES), lambda hp, i, h=h: (0, 2 * hp + h))
    return pl.pallas_call(
        body, name="attn_fwd", grid=(HP, nq),
        in_specs=[qspec(0), qspec(1), kspec(0), kspec(1), pl.BlockSpec((S, LANES), lambda hp, i: (0, 2 * HP + hp))],
        out_specs=[pl.BlockSpec((T, LANES), lambda hp, i: (i, hp)), pl.BlockSpec((T, LANES), lambda hp, i: (i, hp)),
                   pl.BlockSpec((1, 2, T), lambda hp, i: (hp, 0, i))],
        out_shape=[jax.ShapeDtypeStruct((S, D_MODEL), BF16), jax.ShapeDtypeStruct((S, D_MODEL), F32),
                   jax.ShapeDtypeStruct((HP, 2, S), F32)],
        compiler_params=_params(2, vmem))(qa, qa, ka, ka, qkv)


def _attn_stats(do, o32, lse_cols, T):
    S = do.shape[0]
    HP = ATT_HEADS // 2

    def body(do_ref, o_ref, lse_ref, st_ref):
        lane = lax.broadcasted_iota(jnp.int32, (LANES, 8), 0)
        c = lax.broadcasted_iota(jnp.int32, (LANES, 8), 1)
        sel = jnp.where(((c == 2) & (lane < ATT_HEAD_DIM)) | ((c == 3) & (lane >= ATT_HEAD_DIM)), 1.0, 0.0).astype(BF16)
        dd = _dot_exact_right(do_ref[...].astype(F32) * o_ref[...], sel)
        c8 = lax.broadcasted_iota(jnp.int32, (1, 8), 1)
        st_ref[0] = jnp.where(c8 == 0, lse_ref[0, :, 0:1], jnp.where(c8 == 1, lse_ref[0, :, 1:2], dd))

    return pl.pallas_call(
        body, name="attn_stats", grid=(HP, S // T),
        in_specs=[pl.BlockSpec((T, LANES), lambda hp, i: (i, hp)), pl.BlockSpec((T, LANES), lambda hp, i: (i, hp)),
                  pl.BlockSpec((1, T, 2), lambda hp, i: (hp, i, 0))],
        out_specs=pl.BlockSpec((1, T, 8), lambda hp, i: (hp, i, 0)), out_shape=jax.ShapeDtypeStruct((HP, S, 8), F32),
        compiler_params=_params(2))(do, o32, lse_cols)


def _attn_bwd(qa, ka, qkv, do, stats, T):
    S = qkv.shape[0]
    nq = S // T
    HP = ATT_HEADS // 2

    def body(k0_ref, k1_ref, v_ref, q0_ref, q1_ref, do_ref, st_ref, dq_ref, dk_ref, dv_ref, dck_ref, dcq_ref, dq_acc):
        j = pl.program_id(1)
        mA = _lane_mask()
        masks = (mA, jnp.logical_not(mA))
        q_refs = (q0_ref, q1_ref)

        @pl.when(j == 0)
        def _():
            dq_acc[...] = jnp.zeros_like(dq_acc)

        kas = (k0_ref[...], k1_ref[...])
        vj = v_ref[...]
        row = lax.broadcasted_iota(jnp.int32, (T, T), 0)
        col = lax.broadcasted_iota(jnp.int32, (T, T), 1)

        def block(i, carry, diag):
            dvt, dkt0, dkt1 = carry
            off = pl.multiple_of(i * T, T)
            doi = do_ref[pl.ds(off, T), :]
            zero = jnp.zeros_like(doi)
            dkts = [dkt0, dkt1]
            for h in (0, 1):
                qh = q_refs[h][pl.ds(off, T), :]
                doh = jnp.where(masks[h], doi, zero)
                lse = st_ref[0, pl.ds(off, T), h:h + 1]
                dd = st_ref[0, pl.ds(off, T), 2 + h:3 + h]
                sc = _dot_nt(qh, kas[h])
                if diag:
                    sc = jnp.where(row >= col, sc, NEG)
                p = jnp.exp(sc - lse)
                dp = _dot_nt(doh, vj)
                ds = (p * (dp - dd)).astype(BF16)
                dvt = dvt + _dot_tn(doh, p.astype(BF16))
                dkts[h] = dkts[h] + _dot_tn(qh, ds)
                dq_acc[h, pl.ds(off, T), :] += _dot(ds, kas[h])
            return (dvt, dkts[0], dkts[1])

        z = jnp.zeros((LANES, T), F32)
        carry = block(j, (z, z, z), True)
        dvt, dkt0, dkt1 = lax.fori_loop(j + 1, nq, lambda i, c: block(i, c, False), carry)
        dv_ref[...] = dvt.T.astype(BF16)
        dk_ref[...] = jnp.where(mA, dkt0.T, dkt1.T).astype(BF16)
        ones_q = (_aug_base(0), _aug_base(1))
        dck_ref[0, 0:1, :] = -dkt0[ones_q[0]:ones_q[0] + 1, :]
        dck_ref[0, 1:2, :] = -dkt1[ones_q[1]:ones_q[1] + 1, :]

        @pl.when(j == nq - 1)
        def _():
            dq0, dq1 = dq_acc[0], dq_acc[1]
            ones_k = (_aug_base(0) + N_AUG, _aug_base(1) + N_AUG)
            dq_ref[...] = (jnp.where(mA, dq0, dq1) * (1.0 / math.sqrt(ATT_HEAD_DIM))).astype(BF16)
            dcq_ref[0, :, 0:1] = dq0[:, ones_k[0]:ones_k[0] + 1]
            dcq_ref[0, :, 1:2] = dq1[:, ones_k[1]:ones_k[1] + 1]

    vmem = (2 * (3 * T * LANES * 2 + 3 * S * LANES * 2 + S * LANES * 4 + S * LANES * (2 + 4) + 2 * T * LANES * 2 + 8 * T * 4)
            + 2 * S * LANES * 4 + 12 * T * T * 4 + (8 << 20))
    kspec = lambda h: pl.BlockSpec((T, LANES), lambda hp, j, h=h: (j, 2 * hp + h))
    qspec = lambda h: pl.BlockSpec((S, LANES), lambda hp, j, h=h: (0, 2 * hp + h))
    blk = pl.BlockSpec((T, LANES), lambda hp, j: (j, hp))
    full = pl.BlockSpec((S, LANES), lambda hp, j: (0, hp))
    return pl.pallas_call(
        body, name="attn_bwd", grid=(HP, nq),
        in_specs=[kspec(0), kspec(1), pl.BlockSpec((T, LANES), lambda hp, j: (j, 2 * HP + hp)), qspec(0), qspec(1), full,
                  pl.BlockSpec((1, S, 8), lambda hp, j: (hp, 0, 0))],
        out_specs=[full, blk, blk, pl.BlockSpec((1, 2, T), lambda hp, j: (hp, 0, j)),
                   pl.BlockSpec((1, S, 2), lambda hp, j: (hp, 0, 0))],
        out_shape=[jax.ShapeDtypeStruct((S, D_MODEL), BF16)] * 3 + [jax.ShapeDtypeStruct((HP, 2, S), F32),
                                                                     jax.ShapeDtypeStruct((HP, S, 2), F32)],
        scratch_shapes=[pltpu.VMEM((2, S, LANES), F32)], compiler_params=_params(2, vmem))(ka, ka, qkv, qa, qa, do, stats)


HALO = 8


def _shift_down(x, d, above):
    r = pltpu.roll(x, d, 0)
    head = jnp.where(lax.broadcasted_iota(jnp.int32, (HALO, 1), 0) < d, pltpu.roll(above, d, 0), r[0:HALO])
    return head if x.shape[0] == HALO else jnp.concatenate([head, r[HALO:]], axis=0)


def _shift_up(x, d, below):
    n = x.shape[0]
    r = pltpu.roll(x, n - d, 0)
    tail = jnp.where(lax.broadcasted_iota(jnp.int32, (HALO, 1), 0) >= HALO - d, pltpu.roll(below, HALO - d, 0), r[n - HALO:])
    return jnp.concatenate([r[:n - HALO], tail], axis=0)


def _conv_fwd(u, w, b, ts, tc):
    S, C = u.shape
    hb = ts // HALO

    def body(u_ref, prev_ref, w_ref, b_ref, o_ref):
        i = pl.program_id(0)
        x = u_ref[...]
        above = jnp.where(i == 0, 0.0, prev_ref[...])
        acc = b_ref[...] + w_ref[3:4, :] * x
        for k in range(SSM_CONV - 1):
            acc = acc + w_ref[k:k + 1, :] * _shift_down(x, SSM_CONV - 1 - k, above)
        o_ref[...] = acc * _sigmoid(acc)

    return pl.pallas_call(
        body, name="conv_fwd", grid=(S // ts, C // tc),
        in_specs=[pl.BlockSpec((ts, tc), lambda i, j: (i, j)),
                  pl.BlockSpec((HALO, tc), lambda i, j: (jnp.maximum(i * hb - 1, 0), j)),
                  pl.BlockSpec((SSM_CONV, tc), lambda i, j: (0, j)), pl.BlockSpec((1, tc), lambda i, j: (0, j))],
        out_specs=pl.BlockSpec((ts, tc), lambda i, j: (i, j)), out_shape=jax.ShapeDtypeStruct((S, C), F32),
        compiler_params=_params(2))(u, u, w, b)


def _conv_bwd(name, u, dy, w, b, ts, tc, col0):
    S, C = dy.shape
    cb = col0 // tc
    assert cb * tc == col0
    hb = ts // HALO
    nb = S // ts

    def body(u_ref, uprev_ref, unext_ref, dy_ref, dynext_ref, w_ref, b_ref, du_ref, dw_ref, db_ref):
        i = pl.program_id(1)
        x = u_ref[...]
        above = jnp.where(i == 0, 0.0, uprev_ref[...])
        ws = [w_ref[k:k + 1, :] for k in range(SSM_CONV)]

        def dsilu(pre):
            sg = _sigmoid(pre)
            return sg * (1.0 + pre * (1.0 - sg))

        shifted = [_shift_down(x, SSM_CONV - 1 - k, above) for k in range(SSM_CONV - 1)] + [x]
        pre = b_ref[...]
        for k in range(SSM_CONV):
            pre = pre + ws[k] * shifted[k]
        g = dy_ref[...] * dsilu(pre)
        nxt = unext_ref[...]
        tail = x[ts - HALO:, :]
        pre_n = b_ref[...] + ws[SSM_CONV - 1] * nxt
        for k in range(SSM_CONV - 1):
            pre_n = pre_n + ws[k] * _shift_down(nxt, SSM_CONV - 1 - k, tail)
        g_next = jnp.where(i == nb - 1, 0.0, dynext_ref[...] * dsilu(pre_n))
        du = ws[SSM_CONV - 1] * g
        for k in range(SSM_CONV - 1):
            du = du + ws[k] * _shift_up(g, SSM_CONV - 1 - k, g_next)
        du_ref[...] = du.astype(du_ref.dtype)
        dws = [jnp.sum(g * shifted[k], axis=0, keepdims=True) for k in range(SSM_CONV)]
        dbs = jnp.sum(g, axis=0, keepdims=True)

        @pl.when(i == 0)
        def _():
            for k in range(SSM_CONV):
                dw_ref[k:k + 1, :] = dws[k]
            db_ref[...] = dbs

        @pl.when(i > 0)
        def _():
            for k in range(SSM_CONV):
                dw_ref[k:k + 1, :] += dws[k]
            db_ref[...] += dbs

    nxt = lambda off: (lambda j, i: (jnp.minimum((i + 1) * hb, S // HALO - 1), j + off))
    return pl.pallas_call(
        body, name=name, grid=(C // tc, nb),
        in_specs=[pl.BlockSpec((ts, tc), lambda j, i: (i, j + cb)),
                  pl.BlockSpec((HALO, tc), lambda j, i: (jnp.maximum(i * hb - 1, 0), j + cb)),
                  pl.BlockSpec((HALO, tc), nxt(cb)),
                  pl.BlockSpec((ts, tc), lambda j, i: (i, j)),
                  pl.BlockSpec((HALO, tc), nxt(0)),
                  pl.BlockSpec((SSM_CONV, tc), lambda j, i: (0, j + cb)), pl.BlockSpec((1, tc), lambda j, i: (0, j + cb))],
        out_specs=[pl.BlockSpec((ts, tc), lambda j, i: (i, j)), pl.BlockSpec((SSM_CONV, tc), lambda j, i: (0, j)),
                   pl.BlockSpec((1, tc), lambda j, i: (0, j))],
        out_shape=[jax.ShapeDtypeStruct((S, C), BF16), jax.ShapeDtypeStruct((SSM_CONV, C), F32), jax.ShapeDtypeStruct((1, C), F32)],
        compiler_params=_params(2))(u, u, u, dy, dy, w, b)


def _head_sum():
    lane = jnp.right_shift(lax.broadcasted_iota(jnp.int32, (GROUP_LANES, 8), 0), 6)
    r = lax.broadcasted_iota(jnp.int32, (GROUP_LANES, 8), 1)
    return jnp.where(lane == r, 1.0, 0.0).astype(BF16)


def _head_expand():
    r = lax.broadcasted_iota(jnp.int32, (8, GROUP_LANES), 0)
    c = jnp.right_shift(lax.broadcasted_iota(jnp.int32, (8, GROUP_LANES), 1), 6)
    return jnp.where(r == c, 1.0, 0.0).astype(BF16)


def _ssd_common(dtc_ref, dtr_ref, bias_r, alog_b, bias_c, alog_c, L):
    a_b = -jnp.exp(alog_b)
    dt = _dot_exact_right(_softplus(dtc_ref[0] + bias_r), _head_expand())
    acum = _dot_exact_left(_tri(L, False), dt * a_b)
    a_c = -jnp.exp(alog_c)
    dtr = _softplus(dtr_ref[0] + bias_c)
    acum_r = _dot_exact_right(dtr * a_c, _tri(L, True))
    return a_b, dt, acum, acum_r


def _ssd_specs(L, nc, rev):
    cc = (lambda c: nc - 1 - c) if rev else (lambda c: c)
    G = SSM_GROUPS
    blk = pl.BlockSpec((L, GROUP_LANES), lambda g, c: (cc(c), g))
    dtc = pl.BlockSpec((1, L, 8), lambda g, c: (g, cc(c), 0))
    rowv = pl.BlockSpec((1, 1, 8), lambda g, c: (g, 0, 0))
    xs = blk
    bm = pl.BlockSpec((L, SSM_STATE), lambda g, c: (cc(c), SSM_INNER // SSM_STATE + g))
    cm = pl.BlockSpec((L, SSM_STATE), lambda g, c: (cc(c), SSM_INNER // SSM_STATE + G + g))
    dtr = pl.BlockSpec((1, 8, L), lambda g, c: (g, 0, cc(c)))
    vec = pl.BlockSpec((1, GROUP_LANES), lambda g, c: (0, g))
    colv = pl.BlockSpec((1, 8, 1), lambda g, c: (g, 0, 0))
    hs = pl.BlockSpec((1, 1, SSM_STATE, GROUP_LANES), lambda g, c: (g, cc(c), 0, 0))
    return blk, xs, bm, cm, dtc, dtr, vec, rowv, colv, hs


def _ssd_fwd(xbc, z, dtc, dtr, bias_r, alog_b, dskip_b, normw, bias_c, alog_c, L):
    S = z.shape[0]
    nc = S // L
    blk, xs, bm, cm, dtcs, dtrs, vec, rowv, colv, hs = _ssd_specs(L, nc, False)

    def body(x_ref, b_ref, c_ref, z_ref, dtc_ref, dtr_ref, bias_ref, alog_ref, dskip_ref, nw_ref, biasc_ref, alogc_ref,
             y_ref, ssm_ref, hs_ref, h_scr):
        c = pl.program_id(1)

        @pl.when(c == 0)
        def _():
            h_scr[...] = jnp.zeros_like(h_scr)

        mA = _lane_mask()
        a_b, dt, acum, acum_r = _ssd_common(dtc_ref, dtr_ref, bias_ref[0], alog_ref[...], biasc_ref[0], alogc_ref[0], L)
        x = x_ref[...]
        cb, bb = c_ref[...].astype(BF16), b_ref[...].astype(BF16)
        hprev = h_scr[...]
        hs_ref[0, 0] = hprev
        xdt = x * dt
        xdt_b = xdt.astype(BF16)
        gmat = _dot_nt(cb, bb)
        row = lax.broadcasted_iota(jnp.int32, (L, L), 0)
        col = lax.broadcasted_iota(jnp.int32, (L, L), 1)
        parts = []
        for p in range(GROUP_LANES // LANES):
            xp = xdt_b[:, p * LANES:(p + 1) * LANES]
            yd = []
            for hh in (0, 1):
                r = 2 * p + hh
                acol = acum[:, r * ATT_HEAD_DIM:r * ATT_HEAD_DIM + 1]
                arow = acum_r[r:r + 1, :]
                lm = jnp.exp(jnp.where(row >= col, acol - arow, NEG))
                yd.append(_dot((gmat * lm).astype(BF16), xp))
            parts.append(jnp.where(mA, yd[0], yd[1]))
        ydiag = jnp.concatenate(parts, axis=1)
        yoff = jnp.exp(acum) * _dot(cb, hprev.astype(BF16))
        y = ydiag + yoff + dskip_ref[...] * x
        aend = acum[L - 1:L, :]
        wgt = (jnp.exp(aend - acum) * xdt).astype(BF16)
        h_scr[...] = jnp.exp(aend) * hprev + _dot_tn(bb, wgt)
        y_ref[...] = y
        zz = z_ref[...].astype(F32)
        u = y * (zz * _sigmoid(zz))
        rs = lax.rsqrt(jnp.mean(u * u, axis=1, keepdims=True) + RMS_EPS)
        ssm_ref[...] = (u * rs * nw_ref[...]).astype(BF16)

    return pl.pallas_call(
        body, name="ssd_fwd", grid=(SSM_GROUPS, nc),
        in_specs=[xs, bm, cm, blk, dtcs, dtrs, rowv, vec, vec, vec, colv, colv],
        out_specs=[blk, blk, hs],
        out_shape=[jax.ShapeDtypeStruct((S, SSM_INNER), F32), jax.ShapeDtypeStruct((S, SSM_INNER), BF16),
                   jax.ShapeDtypeStruct((SSM_GROUPS, nc, SSM_STATE, GROUP_LANES), F32)],
        scratch_shapes=[pltpu.VMEM((SSM_STATE, GROUP_LANES), F32)],
        compiler_params=_params(2, 48 << 20))(xbc, xbc, xbc, z, dtc, dtr, bias_r, alog_b, dskip_b, normw, bias_c, alog_c)


def _ssd_bwd(xbc, z, y, dssm, hs_all, dtc, dtr, bias_r, alog_r, alog_b, dskip_b, normw, bias_c, alog_c, L):
    S = z.shape[0]
    nc = S // L
    blk, xs, bm, cm, dtcs, dtrs, vec, rowv, colv, hs = _ssd_specs(L, nc, True)

    def body(x_ref, b_ref, c_ref, z_ref, y_ref, dssm_ref, hs_ref, dtc_ref, dtr_ref, bias_ref, alogr_ref, alog_ref, dskip_ref, nw_ref,
             biasc_ref, alogc_ref,
             dx_ref, db_ref, dc_ref, dz_ref, ddt_ref, dnw_ref, ddskip_ref, dalog_ref, dbias_ref, dh_scr):
        c = pl.program_id(1)

        @pl.when(c == 0)
        def _():
            dh_scr[...] = jnp.zeros_like(dh_scr)

        mA = _lane_mask()
        masks = (mA, jnp.logical_not(mA))
        a_b, dt, acum, acum_r = _ssd_common(dtc_ref, dtr_ref, bias_ref[0], alog_ref[...], biasc_ref[0], alogc_ref[0], L)
        x, zz, y, dssm = x_ref[...], z_ref[...].astype(F32), y_ref[...], dssm_ref[...]
        cb, bb = c_ref[...].astype(BF16), b_ref[...].astype(BF16)
        hprev = hs_ref[0, 0]
        hb = hprev.astype(BF16)
        ds = dh_scr[...]
        dsb = ds.astype(BF16)
        dskip = dskip_ref[...]
        aend = acum[L - 1:L, :]
        e_a, e_end = jnp.exp(acum), jnp.exp(aend)
        dte = jnp.exp(aend - acum)
        xdt = x * dt
        xdt_b = xdt.astype(BF16)
        sg = _sigmoid(zz)
        sz = zz * sg
        u = y * sz
        rs = lax.rsqrt(jnp.mean(u * u, axis=1, keepdims=True) + RMS_EPS)
        un = u * rs
        dun = dssm * nw_ref[...]
        du = rs * (dun - un * jnp.mean(dun * un, axis=1, keepdims=True))
        dy = du * sz
        dz_ref[...] = (du * y * sg * (1.0 + zz * (1.0 - sg))).astype(dz_ref.dtype)
        dy_b = dy.astype(BF16)
        dch_b = (dy * e_a).astype(BF16)
        dc = _dot_nt(dch_b, hb)
        dhprev = _dot_tn(cb, dch_b)
        gt = _dot_nt(bb, cb)
        row = lax.broadcasted_iota(jnp.int32, (L, L), 0)
        col = lax.broadcasted_iota(jnp.int32, (L, L), 1)
        dgt = jnp.zeros((L, L), F32)
        parts = []
        for p in range(GROUP_LANES // LANES):
            xp = xdt_b[:, p * LANES:(p + 1) * LANES]
            dyp = dy_b[:, p * LANES:(p + 1) * LANES]
            zero = jnp.zeros_like(dyp)
            acc = None
            for hh in (0, 1):
                r = 2 * p + hh
                acol = acum[:, r * ATT_HEAD_DIM:r * ATT_HEAD_DIM + 1]
                arow = acum_r[r:r + 1, :]
                lmt = jnp.exp(jnp.where(row <= col, arow - acol, NEG))
                dyh = jnp.where(masks[hh], dyp, zero)
                part = _dot((gt * lmt).astype(BF16), dyh)
                acc = part if acc is None else acc + part
                dgt = dgt + _dot_nt(xp, dyh) * lmt
            parts.append(acc)
        dxdt_diag = jnp.concatenate(parts, axis=1)
        dgt_b = dgt.astype(BF16)
        db = _dot(dgt_b, cb)
        dc = dc + _dot_tn(dgt_b, bb)
        dxdt_state = dte * _dot(bb, dsb)
        db = db + _dot_nt((dte * xdt).astype(BF16), dsb)
        dxdt = dxdt_diag + dxdt_state
        dy_r, xdt_r = dy_b.astype(F32), xdt_b.astype(F32)
        dac = dy_r * (y - dskip * x) - xdt_r * dxdt
        tail = jnp.sum(xdt_r * dxdt_state, axis=0, keepdims=True) + e_end * jnp.sum(ds * hprev, axis=0, keepdims=True)
        rowl = lax.broadcasted_iota(jnp.int32, (L, 1), 0)
        dac = dac + jnp.where(rowl == L - 1, tail, 0.0)
        rc = _dot_exact_left(_tri(L, True), dac)
        hsum = _head_sum()
        hs1 = _dot_exact_right(dxdt * x, hsum, 2)
        hs2 = _dot_exact_right(rc, hsum, 2)
        a8 = -jnp.exp(alogr_ref[0])
        dtraw8 = dtc_ref[0] + bias_ref[0]
        ddtraw = (hs1 + a8 * hs2) * _sigmoid(dtraw8)
        dx_ref[...] = dskip * dy + dxdt * dt
        db_ref[...] = db
        dc_ref[...] = dc
        ddt_ref[0] = ddtraw
        dh_scr[...] = e_end * ds + dhprev
        sums = (jnp.sum(dssm * un, axis=0, keepdims=True), jnp.sum(dy * x, axis=0, keepdims=True))
        refs = (dnw_ref, ddskip_ref)
        sums8 = (a8 * jnp.sum(hs2 * _softplus(dtraw8), axis=0, keepdims=True), jnp.sum(ddtraw, axis=0, keepdims=True))
        refs8 = (dalog_ref, dbias_ref)

        @pl.when(c == 0)
        def _():
            for r, v in zip(refs, sums):
                r[...] = v
            for r, v in zip(refs8, sums8):
                r[0] = v

        @pl.when(c > 0)
        def _():
            for r, v in zip(refs, sums):
                r[...] += v
            for r, v in zip(refs8, sums8):
                r[0] += v

    nbc = pl.BlockSpec((L, SSM_STATE), lambda g, c: (nc - 1 - c, g))
    return pl.pallas_call(
        body, name="ssd_bwd", grid=(SSM_GROUPS, nc),
        in_specs=[xs, bm, cm, blk, blk, blk, hs, dtcs, dtrs, rowv, rowv, vec, vec, vec, colv, colv],
        out_specs=[blk, nbc, nbc, blk, dtcs, vec, vec, rowv, rowv],
        out_shape=[jax.ShapeDtypeStruct((S, SSM_INNER), F32), jax.ShapeDtypeStruct((S, SSM_GROUPS * SSM_STATE), F32),
                   jax.ShapeDtypeStruct((S, SSM_GROUPS * SSM_STATE), F32), jax.ShapeDtypeStruct((S, SSM_INNER), BF16),
                   jax.ShapeDtypeStruct((SSM_GROUPS, S, 8), F32)] + [jax.ShapeDtypeStruct((1, SSM_INNER), F32)] * 2
                  + [jax.ShapeDtypeStruct((SSM_GROUPS, 1, 8), F32)] * 2,
        scratch_shapes=[pltpu.VMEM((SSM_STATE, GROUP_LANES), F32)],
        compiler_params=_params(2, 56 << 20))(xbc, xbc, xbc, z, y, dssm, hs_all, dtc, dtr, bias_r, alog_r, alog_b, dskip_b,
                                              normw, bias_c, alog_c)


def _place():
    return lax.axis_index("x"), lax.axis_index("y"), lax.axis_index("c")


def _other_chips(x, y):
    return [(1 - x, y), (x, 1 - y), (1 - x, 1 - y)]


def _half_rows(rows, which):
    hr = rows // 2
    if isinstance(which, int):
        return pl.ds(which * hr, hr)
    return pl.ds(pl.multiple_of(which * hr, 8), hr)


def _chip_gather(name, shards, split):
    n = len(shards)
    ANY = pl.BlockSpec(memory_space=pl.ANY)

    def body(*refs):
        ins, outs = refs[:n], refs[n:2 * n]
        send, recv, fsend, frecv = refs[2 * n:]
        x, y, c = _place()
        me = 2 * x + y
        sibling = (x, y, 1 - c)
        chips = _other_chips(x, y)

        def piece(a, chip_idx, which):
            if split[a]:
                return outs[a].at[chip_idx, _half_rows(shards[a].shape[0], which)]
            return outs[a].at[chip_idx]

        def ici(k, a, to_chip, src_chip):
            src = ins[a].at[_half_rows(shards[a].shape[0], c)] if split[a] else ins[a]
            return pltpu.make_async_remote_copy(src_ref=src, dst_ref=piece(a, src_chip, c), send_sem=send.at[k, a],
                                                recv_sem=recv.at[k, a], device_id=(*to_chip, c), device_id_type=MESH)

        def fwd(k, a, src_chip, which):
            return pltpu.make_async_remote_copy(src_ref=piece(a, src_chip, which), dst_ref=piece(a, src_chip, which),
                                                send_sem=fsend.at[k, a], recv_sem=frecv.at[k, a], device_id=sibling,
                                                device_id_type=MESH)

        sends = []
        for k, chip in enumerate(chips):
            for a in range(n):
                cp = ici(k, a, chip, me)
                cp.start()
                sends.append(cp)
        for k, (ox, oy) in enumerate(chips):
            src = 2 * ox + oy
            for a in range(n):
                ici(k, a, (ox, oy), src).wait_recv()
                if split[a]:
                    cp = fwd(k, a, src, c)
                    cp.start()
                    sends.append(cp)
        for k, (ox, oy) in enumerate(chips):
            for a in range(n):
                if split[a]:
                    fwd(k, a, 2 * ox + oy, 1 - c).wait_recv()
        for cp in sends:
            cp.wait_send()

    sem = pltpu.SemaphoreType.DMA((3, n))
    return pl.pallas_call(
        body, name=name, in_specs=[ANY] * n, out_specs=[ANY] * n,
        out_shape=[jax.ShapeDtypeStruct((4,) + s.shape, s.dtype) for s in shards],
        scratch_shapes=[sem, sem, sem, sem])(*shards)


def _chip_copies_start(name, srcs, per_chip_src, after):
    n = len(srcs)
    HBM = pl.BlockSpec(memory_space=pltpu.HBM)
    SEM = pl.BlockSpec(memory_space=pltpu.SEMAPHORE)
    lands = [pltpu.with_memory_space_constraint(lax.empty(a.shape if per_chip_src else (4,) + a.shape, a.dtype), pltpu.HBM)
             for a in srcs]

    def body(*refs):
        ins, land = refs[:n], refs[n:2 * n]
        send, recv = refs[2 * n + 1], refs[2 * n + 2]
        token = refs[-1]
        x, y, c = _place()
        me = 2 * x + y
        for k, (ox, oy) in enumerate(_other_chips(x, y)):
            for a in range(n):
                src = ins[a].at[2 * ox + oy] if per_chip_src else ins[a]
                pltpu.make_async_remote_copy(src_ref=src, dst_ref=land[a].at[me], send_sem=send.at[k * n + a], recv_sem=recv.at[k * n + a],
                                             device_id=(ox, oy, c), device_id_type=MESH).start()
        token[...] = jnp.zeros_like(token)

    sem = pltpu.SemaphoreType.DMA((3 * n,))
    res = pl.pallas_call(
        body, name=name,
        out_shape=[sem, sem] + [pltpu.HBM(a.shape, a.dtype) for a in srcs] + [pltpu.HBM(b.shape, b.dtype) for b in lands]
                  + [jax.ShapeDtypeStruct((8, LANES), F32)],
        in_specs=[HBM] * (2 * n) + [pl.BlockSpec(memory_space=pl.ANY)],
        out_specs=[SEM, SEM] + [HBM] * (2 * n) + [pl.BlockSpec(memory_space=pltpu.VMEM)],
        input_output_aliases={k: 2 + k for k in range(2 * n)},
        compiler_params=pltpu.CompilerParams(has_side_effects=pltpu.SideEffectType.DATAFLOW_SIDE_EFFECTING),
    )(*[pltpu.with_memory_space_constraint(a, pltpu.HBM) for a in srcs], *lands, after)
    return res[:-1], res[-1]


def _chip_copies_wait(name, started, per_chip_src, after):
    send, recv = started[0], started[1]
    n = (len(started) - 2) // 2
    srcs, lands = started[2:2 + n], started[2 + n:]
    HBM = pl.BlockSpec(memory_space=pltpu.HBM)
    SEM = pl.BlockSpec(memory_space=pltpu.SEMAPHORE)

    def body(*refs):
        ins, land = refs[:n], refs[n:2 * n]
        send_sem, recv_sem = refs[2 * n], refs[2 * n + 1]
        x, y, c = _place()
        me = 2 * x + y
        for k, (ox, oy) in enumerate(_other_chips(x, y)):
            for a in range(n):
                src = ins[a].at[me] if per_chip_src else ins[a]
                cp = pltpu.make_async_remote_copy(src_ref=src, dst_ref=land[a].at[2 * ox + oy], send_sem=send_sem.at[k * n + a],
                                                  recv_sem=recv_sem.at[k * n + a], device_id=(ox, oy, c), device_id_type=MESH)
                cp.wait_send()
                cp.wait_recv()

    res = pl.pallas_call(
        body, name=name,
        out_shape=[pltpu.HBM(a.shape, a.dtype) for a in srcs] + [pltpu.HBM(b.shape, b.dtype) for b in lands],
        in_specs=[HBM] * (2 * n) + [SEM, SEM, pl.BlockSpec(memory_space=pl.ANY)], out_specs=[HBM] * (2 * n),
        input_output_aliases={k: k for k in range(2 * n)},
        compiler_params=pltpu.CompilerParams(has_side_effects=pltpu.SideEffectType.DATAFLOW_SIDE_EFFECTING),
    )(*srcs, *lands, send, recv, after)
    return res[n:]


def _half_to_sibling(name, blocks):
    n = len(blocks)
    ANY = pl.BlockSpec(memory_space=pl.ANY)

    def body(*refs):
        ins, outs = refs[:n], refs[n:2 * n]
        send, recv = refs[2 * n:]
        x, y, c = _place()
        cps = [pltpu.make_async_remote_copy(src_ref=ins[a].at[:, _half_rows(blocks[a].shape[1], 1 - c)], dst_ref=outs[a],
                                            send_sem=send.at[a], recv_sem=recv.at[a], device_id=(x, y, 1 - c),
                                            device_id_type=MESH) for a in range(n)]
        for cp in cps:
            cp.start()
        for cp in cps:
            cp.wait_recv()
        for cp in cps:
            cp.wait_send()

    return pl.pallas_call(
        body, name=name, in_specs=[ANY] * n, out_specs=[ANY] * n,
        out_shape=[jax.ShapeDtypeStruct((4, b.shape[1] // 2, b.shape[2]), b.dtype) for b in blocks],
        scratch_shapes=[pltpu.SemaphoreType.DMA((n,)), pltpu.SemaphoreType.DMA((n,))])(*blocks)


def _sibling_swap(name, arrs):
    n = len(arrs)
    ANY = pl.BlockSpec(memory_space=pl.ANY)

    def body(*refs):
        ins, outs = refs[:n], refs[n:2 * n]
        send, recv = refs[2 * n:]
        x, y, c = _place()
        cps = [pltpu.make_async_remote_copy(src_ref=ins[a], dst_ref=outs[a], send_sem=send.at[a], recv_sem=recv.at[a],
                                            device_id=(x, y, 1 - c), device_id_type=MESH) for a in range(n)]
        for cp in cps:
            cp.start()
        for cp in cps:
            cp.wait_recv()
        for cp in cps:
            cp.wait_send()

    return pl.pallas_call(
        body, name=name, in_specs=[ANY] * n, out_specs=[ANY] * n,
        out_shape=[jax.ShapeDtypeStruct(a.shape, a.dtype) for a in arrs],
        scratch_shapes=[pltpu.SemaphoreType.DMA((n,)), pltpu.SemaphoreType.DMA((n,))])(*arrs)


N_DEV = 8


def _all_sum_small(vec):
    P = vec.shape[1]

    def body(v_ref, o_ref, buf, send, recv):
        x, y, c = _place()
        me = 4 * x + 2 * y + c
        buf[me] = v_ref[...]

        def peer(r):
            return ((1 - x) if (r >> 2) & 1 else x, (1 - y) if (r >> 1) & 1 else y, (1 - c) if r & 1 else c)

        sends = []
        for r in range(1, N_DEV):
            cp = pltpu.make_async_remote_copy(src_ref=v_ref, dst_ref=buf.at[me], send_sem=send.at[r], recv_sem=recv.at[r],
                                              device_id=peer(r), device_id_type=MESH)
            cp.start()
            sends.append(cp)
        for r in range(1, N_DEV):
            px, py, pc = peer(r)
            pltpu.make_async_remote_copy(src_ref=v_ref, dst_ref=buf.at[4 * px + 2 * py + pc], send_sem=send.at[r],
                                         recv_sem=recv.at[r], device_id=(px, py, pc), device_id_type=MESH).wait_recv()
        for cp in sends:
            cp.wait_send()
        tot = buf[0]
        for d in range(1, N_DEV):
            tot = tot + buf[d]
        o_ref[...] = tot

    return pl.pallas_call(
        body, name="all_sum_small", in_specs=[pl.BlockSpec(memory_space=pltpu.VMEM)],
        out_specs=pl.BlockSpec(memory_space=pltpu.VMEM), out_shape=jax.ShapeDtypeStruct((1, P), F32),
        scratch_shapes=[pltpu.VMEM((N_DEV, 1, P), F32), pltpu.SemaphoreType.DMA((N_DEV,)), pltpu.SemaphoreType.DMA((N_DEV,))],
    )(vec)


def _half_sum(name, blocks, theirs, core, tr):
    _, R, C = blocks.shape
    hr = R // 2
    nb = hr // tr
    assert nb * tr == hr

    def body(c_ref, a_ref, b_ref, o_ref):
        o_ref[...] = (a_ref[...] + b_ref[...]).astype(BF16)

    grid_spec = pltpu.PrefetchScalarGridSpec(
        num_scalar_prefetch=1, grid=(4, nb),
        in_specs=[pl.BlockSpec((1, tr, C), lambda b, i, c_ref: (b, c_ref[0] * nb + i, 0)),
                  pl.BlockSpec((1, tr, C), lambda b, i, c_ref: (b, i, 0))],
        out_specs=pl.BlockSpec((1, tr, C), lambda b, i, c_ref: (b, i, 0)))
    return pl.pallas_call(body, name=name, grid_spec=grid_spec, out_shape=jax.ShapeDtypeStruct((4, hr, C), BF16),
                          compiler_params=_params(2, 40 << 20))(core, blocks, theirs)


def _sum4(name, stack, mine, chip, tr):
    _, R, C = stack.shape

    def body(chip_ref, s_ref, m_ref, o_ref):
        t = [jnp.where(chip_ref[0] == j, m_ref[j], s_ref[j]).astype(F32) for j in range(4)]
        o_ref[...] = ((t[0] + t[1]) + t[2]) + t[3]

    blk = pl.BlockSpec((4, tr, C), lambda i, chip_ref: (0, i, 0))
    grid_spec = pltpu.PrefetchScalarGridSpec(num_scalar_prefetch=1, grid=(R // tr,), in_specs=[blk, blk],
                                             out_specs=pl.BlockSpec((tr, C), lambda i, chip_ref: (i, 0)))
    return pl.pallas_call(body, name=name, grid_spec=grid_spec, out_shape=jax.ShapeDtypeStruct((R, C), F32),
                          compiler_params=_params(1, 40 << 20))(chip, stack, mine)


def _adamw_math(w, m, v, g):
    c1 = 1.0 - ADAM_B1 ** ADAM_STEP
    c2 = 1.0 - ADAM_B2 ** ADAM_STEP
    nm = ADAM_B1 * m + (1.0 - ADAM_B1) * g
    nv = ADAM_B2 * v + (1.0 - ADAM_B2) * (g * g)
    return -ADAM_LR * ((nm / c1) / (jnp.sqrt(nv / c2) + ADAM_EPS) + ADAM_WD * w), nm, nv


def _adamw(name, w, m, v, g, tr):
    R, C = w.shape

    def body(w_ref, m_ref, v_ref, ga_ref, g_ref, d_ref, nm_ref, nv_ref):
        g = ga_ref[...]
        g_ref[...] = g
        d_ref[...], nm_ref[...], nv_ref[...] = _adamw_math(w_ref[...], m_ref[...], v_ref[...], g)

    spec = pl.BlockSpec((tr, C), lambda i: (i, 0))
    return pl.pallas_call(body, name=name, grid=(R // tr,), in_specs=[spec] * 4, out_specs=[spec] * 4,
                          out_shape=[jax.ShapeDtypeStruct((R, C), F32)] * 4, compiler_params=_params(1, 40 << 20))(w, m, v, g)


def _adamw_flat(name, w, m, v, g, tr):
    R, C = w.shape

    def body(w_ref, m_ref, v_ref, g_ref, d_ref, nm_ref, nv_ref):
        d_ref[...], nm_ref[...], nv_ref[...] = _adamw_math(w_ref[...], m_ref[...], v_ref[...], g_ref[...])

    spec = pl.BlockSpec((tr, C), lambda i: (i, 0))
    return pl.pallas_call(body, name=name, grid=(R // tr,), in_specs=[spec] * 4, out_specs=[spec] * 3,
                          out_shape=[jax.ShapeDtypeStruct((R, C), F32)] * 3, compiler_params=_params(1, 48 << 20))(w, m, v, g)


def _adamw_halves(name, w, m, v, mine, theirs, core, tr):
    _, R, C = w.shape
    nb = (R // 2) // tr
    assert 2 * nb * tr == R

    def body(c_ref, w_ref, m_ref, v_ref, a_ref, b_ref, g_ref, d_ref, nm_ref, nv_ref):
        g = jnp.where((pl.program_id(0) // nb) == c_ref[0], a_ref[...], b_ref[...])
        g_ref[0] = g
        d_ref[0], nm_ref[0], nv_ref[0] = _adamw_math(w_ref[0], m_ref[0], v_ref[0], g)

    spec = pl.BlockSpec((1, tr, C), lambda i, c_ref: (0, i, 0))
    half = lambda own: pl.BlockSpec((tr, C), lambda i, c_ref, own=own: (
        jnp.clip(i - (c_ref[0] if own else 1 - c_ref[0]) * nb, 0, nb - 1), 0))
    grid_spec = pltpu.PrefetchScalarGridSpec(num_scalar_prefetch=1, grid=(R // tr,),
                                             in_specs=[spec, spec, spec, half(True), half(False)], out_specs=[spec] * 4)
    return pl.pallas_call(body, name=name, grid_spec=grid_spec, out_shape=[jax.ShapeDtypeStruct((1, R, C), F32)] * 4,
                          compiler_params=_params(1, 40 << 20))(core, w, m, v, mine, theirs)


def _row_tile(rows, cols, budget_bytes=1 << 20, mult=8):
    best = None
    for t in range(mult, rows + 1, mult):
        if rows % t == 0 and t * cols * 4 <= budget_bytes:
            best = t
    return best if best is not None else rows


def _ln_fwd(r, g, b):
    mu = jnp.mean(r, axis=1, keepdims=True)
    xc = r - mu
    rstd = lax.rsqrt(jnp.mean(xc * xc, axis=1, keepdims=True) + LN_EPS)
    xhat = xc * rstd
    return xhat * g + b, xhat, rstd


def _ln_bwd(dy, xhat, rstd, g):
    dxh = dy * g
    return rstd * (dxh - jnp.mean(dxh, axis=1, keepdims=True) - xhat * jnp.mean(dxh * xhat, axis=1, keepdims=True))


def _col_segments(sections, width):
    out, cur, room = [], [], width
    for arr, lo, hi in sections:
        while lo < hi:
            take = min(room, hi - lo)
            cur.append((arr, lo, lo + take))
            lo, room = lo + take, room - take
            if room == 0:
                out.append(cur)
                cur, room = [], width
    assert not cur
    return out


def _to_chip_blocks_cols(a):
    R, C4 = a.shape
    return a.reshape(R, 4, C4 // 4).transpose(1, 0, 2)


def _from_chip_blocks_cols(a):
    return a.transpose(1, 0, 2).reshape(a.shape[1], 4 * a.shape[2])


def kernel(x, w_in, b_forget, conv_w, conv_b, dt_bias, a_log, d_skip, ssm_norm_w, w_proj_attn, w_proj_ssm, b_gates, w_out, ln1_g, ln1_b, w_ffn_gate, w_ffn_up, w_ffn_down, ln2_g, ln2_b, loss_target, m_w_in, m_b_forget, m_conv_w, m_conv_b, m_dt_bias, m_a_log, m_d_skip, m_ssm_norm_w, m_w_proj_attn, m_w_proj_ssm, m_b_gates, m_w_out, m_ln1_g, m_ln1_b, m_w_ffn_gate, m_w_ffn_up, m_w_ffn_down, m_ln2_g, m_ln2_b, v_w_in, v_b_forget, v_conv_w, v_conv_b, v_dt_bias, v_a_log, v_d_skip, v_ssm_norm_w, v_w_proj_attn, v_w_proj_ssm, v_b_gates, v_w_out, v_ln1_g, v_ln1_b, v_w_ffn_gate, v_w_ffn_up, v_w_ffn_down, v_ln2_g, v_ln2_b):
    S = x.shape[1]
    D = D_MODEL
    TM, TM2, TM3, TA, AQF, LC, CV, TS, TB = (min(TILES[k], S) for k in ("TM", "TM2", "TM3", "TA", "AQF", "LC", "CV", "TS", "TB"))
    xf = x[0]
    tgt = loss_target[0]
    xb = xf.astype(BF16)

    shards = [w_in[0].astype(BF16), conv_w[0], w_proj_attn[0].astype(BF16), w_proj_ssm[0].astype(BF16), w_out[0].astype(BF16),
              w_ffn_gate[0].astype(BF16), w_ffn_up[0].astype(BF16), w_ffn_down[0].astype(BF16)]
    chip = 2 * lax.axis_index("x") + lax.axis_index("y")
    own = lambda gathered, mine: [lax.dynamic_update_slice(g, sh[None], (chip, 0, 0)) for g, sh in zip(gathered, mine)]
    g_in, g_cw = own(_chip_gather("gather_w_in", shards[:2], [True, False]), shards[:2])
    later, gather_token = _chip_copies_start("gather_rest_start", shards[2:], False, g_cw)
    shard_w = IN_WIDTH // 4

    def w_cols(lo, hi):
        return [g_in[j][:, max(lo, j * shard_w) - j * shard_w:min(hi, (j + 1) * shard_w) - j * shard_w]
                for j in range(4) if max(lo, j * shard_w) < min(hi, (j + 1) * shard_w)]

    w_re = jnp.concatenate(w_cols(0, 3072) + w_cols(3088, 5136) + w_cols(5136, 8208) + w_cols(8240, 10288)
                           + w_cols(3072, 3088) + w_cols(8208, 8240) + [jnp.zeros((D, 80), BF16)], axis=1)
    conv_w_full = _from_chip_blocks_cols(g_cw)

    def plain(accs, rows, vecs, j):
        return [accs[0]], []

    def q_scaled(accs, rows, vecs, j):
        return [accs[0] * jnp.where(j * 512 < D, 1.0 / math.sqrt(ATT_HEAD_DIM), 1.0)], []

    qkv, = _mm("proj_qkv", S, 3072, TM, 512, [(xb, D, 0)], [(w_re, 0)], [(0, 0)], q_scaled, [(3072, BF16, 0)],
               after=[gather_token])
    z, = _mm("proj_z", S, 2048, TM, 512, [(xb, D, 0)], [(w_re, RE_Z // 512)], [(0, 0)], plain, [(2048, BF16, 0)])
    xbc_raw, = _mm("proj_xbc", S, 3072, TM, 512, [(xb, D, 0)], [(w_re, RE_XBC // 512)], [(0, 0)], plain, [(3072, F32, 0)])
    gl, = _mm("proj_gate", S, 2048, TM, 512, [(xb, D, 0)], [(w_re, RE_GATE // 512)], [(0, 0)], plain, [(2048, BF16, 0)])
    small, = _mm("proj_small", S, 128, TM, 128, [(xb, D, 0)], [(w_re, RE_SMALL // 128)], [(0, 0)], plain, [(128, F32, 0)])

    bvec = jnp.concatenate([b_forget, jnp.zeros((1, LANES - ATT_HEADS), F32)], axis=1)
    cum = _cum_fwd(small, bvec, TB)[:, :ATT_HEADS]
    cum_cols = cum.reshape(S, 8, 2).transpose(1, 0, 2)
    qa, ka = _attn_prep(qkv, cum_cols, TM)
    o, o32, lse_rows = _attn_fwd(qa, ka, qkv, AQF, TA)

    cb_row = conv_b
    xbc = _conv_fwd(xbc_raw, conv_w_full, cb_row, CV, 512)
    dt_raw = small[:, 16:48]
    dtc = dt_raw.reshape(S, SSM_GROUPS, 8).transpose(1, 0, 2)
    dtr = dt_raw.T.reshape(SSM_GROUPS, 8, S)
    bias_r = dt_bias.reshape(SSM_GROUPS, 1, 8)
    alog_b = jnp.repeat(a_log, ATT_HEAD_DIM, axis=1)
    dskip_b = jnp.repeat(d_skip, ATT_HEAD_DIM, axis=1)
    bias_c = dt_bias.reshape(SSM_GROUPS, 8, 1)
    alog_c = a_log.reshape(SSM_GROUPS, 8, 1)
    y_ssd, ssm, hs_all = _ssd_fwd(xbc, z, dtc, dtr, bias_r, alog_b, dskip_b, ssm_norm_w, bias_c, alog_c, LC)

    def merge(accs, rows, vecs, j):
        g0, g1 = _sigmoid(rows[0].astype(F32) + vecs[0]), _sigmoid(rows[1].astype(F32) + vecs[1])
        return [g0 * accs[0] + g1 * accs[1], accs[0], accs[1]], []

    g_pa, g_ps, g_out, g_fg, g_fu, g_fd = own(_chip_copies_wait("gather_rest_wait", later, False, o), shards[2:])
    wpa, wps, wout = g_pa.reshape(D, D), g_ps.reshape(SSM_INNER, D), g_out.reshape(D, D)
    wfg, wfu, wfd = _from_chip_blocks_cols(g_fg), _from_chip_blocks_cols(g_fu), g_fd.reshape(FFN_HIDDEN, D)
    mix, attn_d, ssm_d = _mm("merge", S, D, TM, 512, [(o, D, 0), (ssm, SSM_INNER, 0)], [(wpa, 0), (wps, 0)], [(0, 0), (1, 1)],
                             merge, [(D, BF16, 0), (D, BF16, 0), (D, BF16, 0)], rows=[(gl, 0), (gl, 2)],
                             vecs_n=[(b_gates, 0), (b_gates, 2)])

    def out_ln1(accs, rows, vecs, j):
        r1 = ALPHA * rows[0] + accs[0]
        h1, _, _ = _ln_fwd(r1, vecs[0], vecs[1])
        return [r1, h1, h1], []

    r1, h1, h1b = _mm("out_ln1", S, D, TM2, D, [(mix, D, 0)], [(wout, 0)], [(0, 0)], out_ln1,
                      [(D, F32, 0), (D, F32, 0), (D, BF16, 0)], rows=[(xf, 0)], vecs_n=[(ln1_g, 0), (ln1_b, 0)])

    FT = FFN_HIDDEN // 2

    def swiglu(accs, rows, vecs, j):
        g, u = accs
        return [g, u, g * _sigmoid(g) * u], []

    gate, up, hmid = _mm("ffn_up", S, FFN_HIDDEN, TM3, FT, [(h1b, D, 0)], [(wfg, 0), (wfu, 0)], [(0, 0), (0, 1)], swiglu,
                         [(FFN_HIDDEN, BF16, 0), (FFN_HIDDEN, BF16, 0), (FFN_HIDDEN, BF16, 0)])

    def down_ln2_loss(accs, rows, vecs, j):
        r2 = ALPHA * rows[0] + accs[0]
        yv, xhat, rstd = _ln_fwd(r2, vecs[0], vecs[1])
        diff = yv - rows[1]
        dy = diff * (1.0 / D_MODEL)
        dr2 = _ln_bwd(dy, xhat, rstd, vecs[0])
        return [dr2, dr2], [jnp.sum(dy * xhat, axis=0, keepdims=True), jnp.sum(dy, axis=0, keepdims=True),
                            (0.5 / D_MODEL) * jnp.sum(diff * diff, axis=0, keepdims=True)]

    dr2, dr2b, dln2_g, dln2_b, loss_lanes = _mm("ffn_down_ln2", S, D, TM3, D, [(hmid, FFN_HIDDEN, 0)], [(wfd, 0)], [(0, 0)],
                                               down_ln2_loss, [(D, F32, 0), (D, BF16, 0)], rows=[(h1, 0), (tgt, 0)],
                                               vecs_n=[(ln2_g, 0), (ln2_b, 0)], sums=[D, D, D])
    loss = lax.psum(jnp.sum(loss_lanes), ("x", "y", "c"))

    def dswiglu(accs, rows, vecs, j):
        g, u = rows[0].astype(F32), rows[1].astype(F32)
        sg = _sigmoid(g)
        return [accs[0] * u * sg * (1.0 + g * (1.0 - sg)), accs[0] * g * sg], []

    dgate, dup = _mm("ffn_down_bwd", S, FFN_HIDDEN, TM3, FT, [(dr2b, D, 0)], [(wfd, 0)], [(0, 0)], dswiglu,
                     [(FFN_HIDDEN, BF16, 0), (FFN_HIDDEN, BF16, 0)], nt=True, rows=[(gate, 0), (up, 0)])
    dwfd = _mm_tn("dw_ffn_down", hmid, dr2b, FFN_HIDDEN // 2, D, TS)
    dwfg = _mm_tn("dw_ffn_gate", h1b, dgate, D, FT, TS)
    dwfu = _mm_tn("dw_ffn_up", h1b, dup, D, FT, TS)
    core = lax.axis_index("c").astype(jnp.int32).reshape(1)

    def send_grads(tag, names_, blocks_, after_):
        theirs_ = _half_to_sibling("swap_halves_" + tag, blocks_)
        halves_ = [_half_sum("halfsum_" + nm, b, t, core, _row_tile(b.shape[1] // 2, b.shape[2], mult=16))
                   for nm, b, t in zip(names_, blocks_, theirs_)]
        started_, token_ = _chip_copies_start("scatter_" + tag + "_start", halves_, True, after_)
        return halves_, started_, token_

    ffn_names = ["w_ffn_gate", "w_ffn_up", "w_ffn_down"]
    ffn_halves, ffn_started, ffn_token = send_grads(
        "ffn", ffn_names, [_to_chip_blocks_cols(dwfg), _to_chip_blocks_cols(dwfu), dwfd.reshape(4, FFN_HIDDEN // 4, D)], dwfu)

    def dh1_ln1(accs, rows, vecs, j):
        dh1 = ALPHA * rows[0] + accs[0] + accs[1]
        _, xhat, rstd = _ln_fwd(rows[1], vecs[0], vecs[0])
        dr1 = _ln_bwd(dh1, xhat, rstd, vecs[0])
        return [dr1, dr1], [jnp.sum(dh1 * xhat, axis=0, keepdims=True), jnp.sum(dh1, axis=0, keepdims=True)]

    dr1, dr1b, dln1_g, dln1_b = _mm("ffn_up_bwd_ln1", S, D, TM2, D, [(dgate, FFN_HIDDEN, 0), (dup, FFN_HIDDEN, 0)],
                                    [(wfg, 0), (wfu, 0)], [(0, 0), (1, 1)], dh1_ln1, [(D, F32, 0), (D, BF16, 0)], nt=True,
                                    rows=[(dr2, 0), (r1, 0)], vecs_n=[(ln1_g, 0)], sums=[D, D], after=[ffn_token])

    def dmerge(accs, rows, vecs, j):
        dmix = accs[0]
        g0, g1 = _sigmoid(rows[0].astype(F32) + vecs[0]), _sigmoid(rows[1].astype(F32) + vecs[1])
        dgl0 = dmix * rows[2].astype(F32) * g0 * (1.0 - g0)
        dgl1 = dmix * rows[3].astype(F32) * g1 * (1.0 - g1)
        return [dmix * g0, dmix * g1, dgl0, dgl1], [jnp.sum(dgl0, axis=0, keepdims=True), jnp.sum(dgl1, axis=0, keepdims=True)]

    d_attn_d, d_ssm_d, dgl0, dgl1, dbg0, dbg1 = _mm(
        "out_bwd", S, D, TM, 512, [(dr1b, D, 0)], [(wout, 0)], [(0, 0)], dmerge, [(D, BF16, 0)] * 4, nt=True,
        rows=[(gl, 0), (gl, 2), (attn_d, 0), (ssm_d, 0)], vecs_n=[(b_gates, 0), (b_gates, 2)], sums=[D, D])
    dwout = _mm_tn("dw_out", mix, dr1b, D, D, TS)
    dwpa = _mm_tn("dw_proj_attn", o, d_attn_d, D, D, TS)
    dwps = _mm_tn("dw_proj_ssm", ssm, d_ssm_d, D, D, TS)
    mid_names = ["w_proj_attn", "w_proj_ssm", "w_out"]
    mid_halves, mid_started, mid_token = send_grads(
        "mid", mid_names, [dwpa.reshape(4, D // 4, D), dwps.reshape(4, SSM_INNER // 4, D), dwout.reshape(4, D // 4, D)], dwps)

    do, = _mm("proj_attn_bwd", S, D, TM, 512, [(d_attn_d, D, 0)], [(wpa, 0)], [(0, 0)], plain, [(D, BF16, 0)], nt=True,
              after=[mid_token])
    stats = _attn_stats(do, o32, lse_rows.transpose(0, 2, 1), AQF)
    dq, dk, dv, dck, dcq = _attn_bwd(qa, ka, qkv, do, stats, TA)

    def per_head(a):
        a = a.transpose(1, 0, 2).reshape(S, ATT_HEADS)
        return jnp.concatenate([a, jnp.zeros((S, LANES - ATT_HEADS), F32)], axis=1)

    dfl, dbf = _cum_bwd(per_head(dck.transpose(0, 2, 1)), per_head(dcq), small, bvec, TB)

    dssm, = _mm("proj_ssm_bwd", S, SSM_INNER, TM, 512, [(d_ssm_d, D, 0)], [(wps, 0)], [(0, 0)], plain, [(SSM_INNER, F32, 0)],
                nt=True)
    dxs, dbm, dcm, dz, ddt8, dnw, ddskip_b, dalog8, dbias8 = _ssd_bwd(
        xbc, z, y_ssd, dssm, hs_all, dtc, dtr, bias_r, a_log.reshape(SSM_GROUPS, 1, 8), alog_b, dskip_b, ssm_norm_w, bias_c,
        alog_c, LC)
    du_x, dcw_x, dcb_x = _conv_bwd("conv_bwd_x", xbc_raw, dxs, conv_w_full, cb_row, CV, 512, 0)
    du_b, dcw_b, dcb_b = _conv_bwd("conv_bwd_b", xbc_raw, dbm, conv_w_full, cb_row, CV, 512, SSM_INNER)
    du_c, dcw_c, dcb_c = _conv_bwd("conv_bwd_c", xbc_raw, dcm, conv_w_full, cb_row, CV, 512, SSM_INNER + SSM_GROUPS * SSM_STATE)
    dconv_w = jnp.concatenate([dcw_x, dcw_b, dcw_c], axis=1)
    dconv_b = jnp.concatenate([dcb_x, dcb_b, dcb_c], axis=1)
    ddt_raw = ddt8.transpose(1, 0, 2).reshape(S, SSM_HEADS)

    dsmall = jnp.concatenate([dfl[:, :ATT_HEADS], ddt_raw, jnp.zeros((S, 80), F32)], axis=1).astype(BF16)
    HB = SSM_GROUPS * SSM_STATE
    dw_q, dw_k, dw_v = _mm_tn_shared("dw_in_qkv", xb, [dq, dk, dv], TS // 2)
    dw_z = _mm_tn("dw_in_z", xb, dz, D, D, TS)
    dw_xs, dw_b, dw_c = _mm_tn("dw_in_xs", xb, du_x, D, D, TS), _mm_tn("dw_in_b", xb, du_b, D, HB, TS), _mm_tn("dw_in_c", xb, du_c, D, HB, TS)
    dw_g0, dw_g1 = _mm_tn("dw_in_g0", xb, dgl0, D, D, TS), _mm_tn("dw_in_g1", xb, dgl1, D, D, TS)
    dw_s = _mm_tn("dw_in_small", xb, dsmall, D, LANES, TS)
    whole = lambda a: (a, 0, a.shape[1])
    dw_sections = [whole(dw_q), whole(dw_k), whole(dw_v), (dw_s, 0, ATT_HEADS), whole(dw_z), whole(dw_xs), whole(dw_b), whole(dw_c),
                   (dw_s, ATT_HEADS, ATT_HEADS + SSM_HEADS), whole(dw_g0), whole(dw_g1)]
    dw_blocks = jnp.stack([jnp.concatenate([a[:, lo:hi] for a, lo, hi in segs], axis=1)
                           for segs in _col_segments(dw_sections, shard_w)])

    in_halves, in_started, in_token = send_grads("in", ["w_in"], [dw_blocks], dw_blocks)
    def dx_first(accs, rows, vecs, j):
        return [ALPHA * rows[0] + sum(accs[1:], accs[0])], []

    def dx_more(accs, rows, vecs, j):
        return [rows[0] + sum(accs[1:], accs[0])], []

    wk = lambda col, width=D: (w_re, 0, col // width, width)
    dx_part, = _mm("dx_a", S, D, TM2, D, [(dq, D, 0), (dk, D, 0), (dv, D, 0), (dz, D, 0), (dz, D, 1)],
                   [wk(0), wk(1024), wk(2048), wk(RE_Z), wk(RE_Z + 1024)], [(k, k) for k in range(5)], dx_first,
                   [(D, F32, 0)], nt=True, rows=[(dr1, 0)], after=[in_token])
    grad_x, = _mm("dx_b", S, D, TM2, D,
                  [(du_x, D, 0), (du_x, D, 1), (du_b, HB, 0), (du_c, HB, 0), (dgl0, D, 0), (dgl1, D, 0), (dsmall, LANES, 0)],
                  [wk(RE_XBC), wk(RE_XBC + 1024), wk(RE_XBC + 2048, HB), wk(RE_XBC + 2048 + HB, HB), wk(RE_GATE),
                   wk(RE_GATE + 1024), wk(RE_SMALL, LANES)],
                  [(k, k) for k in range(7)], dx_more, [(D, F32, 0)], nt=True, rows=[(dx_part, 0)])
    names = ["w_in"] + mid_names + ffn_names
    halves = in_halves + mid_halves + ffn_halves
    stacks = (_chip_copies_wait("scatter_in_wait", in_started, True, grad_x)
              + _chip_copies_wait("scatter_mid_wait", mid_started, True, grad_x)
              + _chip_copies_wait("scatter_ffn_wait", ffn_started, True, grad_x))
    chip1 = chip.astype(jnp.int32).reshape(1)
    reduced = [_sum4("sum_" + nm, st, hv, chip1, _row_tile(st.shape[1], st.shape[2], mult=16))
               for nm, st, hv in zip(names, stacks, halves)]
    other = _sibling_swap("swap_reduced", reduced)
    big_w = [w_in, w_proj_attn, w_proj_ssm, w_out, w_ffn_gate, w_ffn_up, w_ffn_down]
    big_m = [m_w_in, m_w_proj_attn, m_w_proj_ssm, m_w_out, m_w_ffn_gate, m_w_ffn_up, m_w_ffn_down]
    big_v = [v_w_in, v_w_proj_attn, v_w_proj_ssm, v_w_out, v_w_ffn_gate, v_w_ffn_up, v_w_ffn_down]
    big = {}
    lower, upper = jnp.where(core[0] == 0, reduced[0], other[0]), jnp.where(core[0] == 0, other[0], reduced[0])
    g_in_t = jnp.concatenate([lower.T, upper.T], axis=1)
    flat = lambda a: jnp.transpose(a, (2, 0, 1)).reshape(-1, LANES)
    unflat = lambda a: jnp.transpose(a.reshape(shard_w, 1, D), (1, 2, 0))
    flat_rows = shard_w * D // LANES
    big["w_in"] = [g_in_t.T[None]] + [unflat(r) for r in _adamw_flat("adamw_w_in", flat(w_in), flat(m_w_in), flat(v_w_in),
                                                                      g_in_t.reshape(-1, LANES), _row_tile(flat_rows, LANES, 3 << 20))]
    for nm, w_, m_, v_, mine, theirs in list(zip(names, big_w, big_m, big_v, reduced, other))[1:]:
        if nm in ("w_ffn_gate", "w_ffn_up"):
            lower, upper = jnp.where(core[0] == 0, mine, theirs), jnp.where(core[0] == 0, theirs, mine)
            across = lambda a: jnp.transpose(a, (0, 2, 1))
            res = _adamw("adamw_" + nm, across(w_)[0], across(m_)[0], across(v_)[0], jnp.concatenate([lower.T, upper.T], axis=1),
                         _row_tile(w_.shape[2], w_.shape[1]))
            big[nm] = [across(r[None]) for r in res]
            continue
        big[nm] = _adamw_halves("adamw_" + nm, w_, m_, v_, mine, theirs, core, _row_tile(w_.shape[1] // 2, w_.shape[2]))

    dd_skip = ddskip_b.reshape(1, SSM_HEADS, ATT_HEAD_DIM).sum(axis=2)
    pieces = [dbf[:, :ATT_HEADS], dconv_w.reshape(1, SSM_CONV * SSM_CONV_DIM), dconv_b, dbias8.reshape(1, SSM_HEADS), dalog8.reshape(1, SSM_HEADS), dd_skip,
              dnw, dbg0, dbg1, dln1_g, dln1_b, dln2_g, dln2_b]
    widths = [p.shape[1] for p in pieces]
    total = sum(widths)
    P = -(-total // LANES) * LANES
    packed = jnp.concatenate(pieces + [jnp.zeros((1, P - total), F32)], axis=1)
    summed = _all_sum_small(packed)
    offs = [0]
    for wd in widths:
        offs.append(offs[-1] + wd)
    sm = [summed[:, offs[k]:offs[k + 1]] for k in range(len(pieces))]
    g_bf, g_cw_full, g_cb, g_dtb, g_al, g_ds, g_nw = sm[0], sm[1].reshape(SSM_CONV, SSM_CONV_DIM), sm[2], sm[3], sm[4], sm[5], sm[6]
    g_bg = jnp.concatenate([sm[7], sm[8]], axis=1)
    g_l1g, g_l1b, g_l2g, g_l2b = sm[9], sm[10], sm[11], sm[12]
    cshard = SSM_CONV_DIM // 4
    g_cw_shard = lax.dynamic_slice_in_dim(g_cw_full, chip * cshard, cshard, axis=1)
    small_names = ["b_forget", "conv_w", "conv_b", "dt_bias", "a_log", "d_skip", "ssm_norm_w", "b_gates", "ln1_g", "ln1_b",
                   "ln2_g", "ln2_b"]
    small_g = [g_bf, g_cw_shard.reshape(1, -1), g_cb, g_dtb, g_al, g_ds, g_nw, g_bg, g_l1g, g_l1b, g_l2g, g_l2b]
    small_w = [b_forget, conv_w[0].reshape(1, -1), conv_b, dt_bias, a_log, d_skip, ssm_norm_w, b_gates, ln1_g, ln1_b, ln2_g, ln2_b]
    small_m = [m_b_forget, m_conv_w[0].reshape(1, -1), m_conv_b, m_dt_bias, m_a_log, m_d_skip, m_ssm_norm_w, m_b_gates, m_ln1_g,
               m_ln1_b, m_ln2_g, m_ln2_b]
    small_v = [v_b_forget, v_conv_w[0].reshape(1, -1), v_conv_b, v_dt_bias, v_a_log, v_d_skip, v_ssm_norm_w, v_b_gates, v_ln1_g,
               v_ln1_b, v_ln2_g, v_ln2_b]
    sw = [a.shape[1] for a in small_w]
    stot = sum(sw)
    SP = -(-stot // LANES) * LANES

    def pack(parts):
        return jnp.concatenate(list(parts) + [jnp.zeros((1, SP - stot), F32)], axis=1).reshape(SP // LANES, LANES)

    sres = _adamw("adamw_small", pack(small_w), pack(small_m), pack(small_v), pack(small_g), SP // LANES)
    soffs = [0]
    for wd in sw:
        soffs.append(soffs[-1] + wd)
    smalls = {}
    for k, nm in enumerate(small_names):
        vals = [r.reshape(1, SP)[:, soffs[k]:soffs[k + 1]] for r in sres]
        if nm == "conv_w":
            vals = [v_.reshape(1, SSM_CONV, cshard) for v_ in vals]
        smalls[nm] = vals

    order = ["w_in", "b_forget", "conv_w", "conv_b", "dt_bias", "a_log", "d_skip", "ssm_norm_w", "w_proj_attn", "w_proj_ssm",
             "b_gates", "w_out", "ln1_g", "ln1_b", "w_ffn_gate", "w_ffn_up", "w_ffn_down", "ln2_g", "ln2_b"]
    allres = {**big, **smalls}
    outs = [loss, grad_x[None]]
    for idx in range(4):
        outs += [allres[nm][idx] for nm in order]
    return tuple(outs)
```
